```python
import math
import jax
import jax.numpy as jnp
from jax import lax
import numpy as np

D_MODEL = 2048
BATCH = 8
SEQ = 4096
DEPTH = 2

N_A_LAYERS = max(1, DEPTH // 2)
N_B_LAYERS = DEPTH - N_A_LAYERS
S5_GROUP_CH = 16
S5_GROUPS = D_MODEL // S5_GROUP_CH
S5_STATE = 64
HEAD_DIM = 128
N_HEADS = D_MODEL // HEAD_DIM
N_KV_HEADS = 4
DILATED_PATTERNS = ((128, 1), (512, 4), (2048, 16))
N_GROUPS = len(DILATED_PATTERNS)
FFN_HIDDEN = ((8 * D_MODEL // 3 + 255) // 256) * 256
EPS = 1e-6
NEG_INF = -1e30

kernel_name = "yoco_s5_dilated_attention_hybrid"


def rms_norm(x, g):
    xf = x.astype(jnp.float32)
    r = lax.rsqrt(jnp.mean(xf * xf, axis=-1, keepdims=True) + EPS)
    return (xf * r * g.astype(jnp.float32)).astype(x.dtype)


def swiglu(h, w_in, w_out):
    gate, up = jnp.split(h @ w_in, 2, axis=-1)
    return (jax.nn.silu(gate) * up) @ w_out


def s5_mixer(u, lam_re, lam_im, log_dt, b_re, b_im, c_re, c_im, d_skip, w_glu):
    bsz, seq, dm = u.shape
    f32 = jnp.float32
    ug = u.astype(f32).reshape(bsz, seq, S5_GROUPS, S5_GROUP_CH)
    lr = lam_re.astype(f32)
    li = lam_im.astype(f32)
    dt = jnp.exp(log_dt.astype(f32))[:, None]
    mag = jnp.exp(lr * dt)
    ang = li * dt
    lb_re = mag * jnp.cos(ang)
    lb_im = mag * jnp.sin(ang)
    nr = lb_re - 1.0
    den = lr * lr + li * li
    f_re = (nr * lr + lb_im * li) / den
    f_im = (lb_im * lr - nr * li) / den
    br = b_re.astype(f32)
    bi = b_im.astype(f32)
    bb_re = f_re[..., None] * br - f_im[..., None] * bi
    bb_im = f_re[..., None] * bi + f_im[..., None] * br
    bu_re = jnp.einsum('bsgc,gpc->bsgp', ug, bb_re)
    bu_im = jnp.einsum('bsgc,gpc->bsgp', ug, bb_im)
    a_re = jnp.broadcast_to(lb_re, bu_re.shape)
    a_im = jnp.broadcast_to(lb_im, bu_im.shape)

    def combine(e1, e2):
        a1r, a1i, b1r, b1i = e1
        a2r, a2i, b2r, b2i = e2
        return (a2r * a1r - a2i * a1i,
                a2r * a1i + a2i * a1r,
                a2r * b1r - a2i * b1i + b2r,
                a2r * b1i + a2i * b1r + b2i)

    _, _, xs_re, xs_im = lax.associative_scan(combine, (a_re, a_im, bu_re, bu_im), axis=1)
    y = (jnp.einsum('bsgp,gcp->bsgc', xs_re, c_re.astype(f32))
         - jnp.einsum('bsgp,gcp->bsgc', xs_im, c_im.astype(f32)))
    y = y.reshape(bsz, seq, dm) + d_skip.astype(f32) * u.astype(f32)
    z = jax.nn.gelu(y).astype(u.dtype)
    val, gate = jnp.split(z @ w_glu, 2, axis=-1)
    return val * jax.nn.sigmoid(gate)


def dilated_window_attention(q, k, v, window, dilation):
    bsz, seq, nh, hd = q.shape
    nkv = k.shape[2]
    rep = nh // nkv
    n = seq // dilation
    blk = window // dilation
    nb = -(-n // blk)
    pad = nb * blk - n

    def to_residue(t):
        h = t.shape[2]
        t = t.reshape(bsz, n, dilation, h, hd).transpose(0, 2, 1, 3, 4)
        t = t.reshape(bsz * dilation, n, h, hd)
        t = jnp.pad(t, ((0, 0), (0, pad), (0, 0), (0, 0)))
        return t.reshape(bsz * dilation, nb, blk, h, hd)

    def with_prev(t):
        prev = jnp.pad(t, ((0, 0), (1, 0), (0, 0), (0, 0), (0, 0)))[:, :-1]
        return jnp.concatenate([prev, t], axis=2)

    qr = to_residue(q).reshape(bsz * dilation, nb, blk, nkv, rep, hd)
    kb = with_prev(to_residue(k))
    vb = with_prev(to_residue(v))
    scores = jnp.einsum('znqkge,znske->znkgqs', qr, kb).astype(jnp.float32) * (hd ** -0.5)
    qi = jnp.arange(blk)[:, None]
    si = jnp.arange(2 * blk)[None, :]
    dist = qi + blk - si
    band = (dist >= 0) & (dist <= blk)
    valid = band[None] & ((jnp.arange(nb)[:, None, None] > 0) | (si[None] >= blk))
    scores = jnp.where(valid[None, :, None, None], scores, NEG_INF)
    m = jnp.max(scores, axis=-1, keepdims=True)
    p = jnp.exp(scores - m)
    l = jnp.sum(p, axis=-1, keepdims=True)
    out = jnp.einsum('znkgqs,znske->znqkge', (p / l).astype(v.dtype), vb)
    lse = (m + jnp.log(l))[..., 0]
    out = out.reshape(bsz * dilation, nb * blk, nh, hd)[:, :n]
    out = out.reshape(bsz, dilation, n, nh, hd).transpose(0, 2, 1, 3, 4).reshape(bsz, seq, nh, hd)
    lse = lse.transpose(0, 1, 4, 2, 3).reshape(bsz * dilation, nb * blk, nh)[:, :n]
    lse = lse.reshape(bsz, dilation, n, nh).transpose(0, 2, 1, 3).reshape(bsz, seq, nh)
    return out, lse


def dilated_mixer(h, k, v, w_q, w_o):
    bsz, seq, _ = h.shape
    q = (h @ w_q).reshape(bsz, seq, N_GROUPS, N_HEADS, HEAD_DIM)
    outs = []
    lses = []
    for g, (window, dilation) in enumerate(DILATED_PATTERNS):
        o, l = dilated_window_attention(q[:, :, g], k, v, window, dilation)
        outs.append(o)
        lses.append(l)
    wts = jax.nn.softmax(jnp.stack(lses, axis=0), axis=0)
    o = jnp.sum(wts[..., None] * jnp.stack(outs, axis=0).astype(jnp.float32), axis=0)
    return o.astype(h.dtype).reshape(bsz, seq, N_HEADS * HEAD_DIM) @ w_o


def _fwd_setup_inputs(seed: int = 0) -> dict:
    key = jax.random.key(seed)
    ks = jax.random.split(key, 24)
    f32 = jnp.float32

    def nrm(k, shape, scale):
        return jax.random.normal(k, shape, f32) * scale

    na, nbl, D, F = N_A_LAYERS, N_B_LAYERS, D_MODEL, FFN_HIDDEN
    G, P, C = S5_GROUPS, S5_STATE, S5_GROUP_CH
    x = nrm(ks[0], (BATCH, SEQ, D), 1.0)
    n_idx = jnp.arange(P, dtype=f32)
    s5_lam_re = -0.5 + nrm(ks[1], (na, G, P), 0.01)
    s5_lam_im = math.pi * n_idx + nrm(ks[2], (na, G, P), 0.01)
    s5_log_dt = jax.random.uniform(ks[3], (na, G), f32, math.log(1e-3), math.log(1e-1))
    s5_b_re = nrm(ks[4], (na, G, P, C), (2 * C) ** -0.5)
    s5_b_im = nrm(ks[5], (na, G, P, C), (2 * C) ** -0.5)
    s5_c_re = nrm(ks[6], (na, G, C, P), (2 * P) ** -0.5)
    s5_c_im = nrm(ks[7], (na, G, C, P), (2 * P) ** -0.5)
    s5_d = nrm(ks[8], (na, D), 1.0)
    s5_w_glu = nrm(ks[9], (na, D, 2 * D), D ** -0.5)
    a_norm_mix = 1.0 + nrm(ks[10], (na, D), 0.02)
    ffn_norm = 1.0 + nrm(ks[11], (DEPTH, D), 0.02)
    ffn_w_in = nrm(ks[12], (DEPTH, D, 2 * F), D ** -0.5)
    ffn_w_out = nrm(ks[13], (DEPTH, F, D), F ** -0.5)
    b_norm_mix = 1.0 + nrm(ks[14], (nbl, D), 0.02)
    attn_w_q = nrm(ks[15], (nbl, D, N_GROUPS * N_HEADS * HEAD_DIM), D ** -0.5)
    attn_w_o = nrm(ks[16], (nbl, N_HEADS * HEAD_DIM, D), (N_HEADS * HEAD_DIM) ** -0.5)
    kv_norm = 1.0 + nrm(ks[17], (D,), 0.02)
    w_kv = nrm(ks[18], (D, 2 * N_KV_HEADS * HEAD_DIM), D ** -0.5)
    final_norm = 1.0 + nrm(ks[19], (D,), 0.02)
    return {"x": x, "s5_lam_re": s5_lam_re, "s5_lam_im": s5_lam_im, "s5_log_dt": s5_log_dt,
            "s5_b_re": s5_b_re, "s5_b_im": s5_b_im, "s5_c_re": s5_c_re, "s5_c_im": s5_c_im,
            "s5_d": s5_d, "s5_w_glu": s5_w_glu, "a_norm_mix": a_norm_mix,
            "ffn_norm": ffn_norm, "ffn_w_in": ffn_w_in, "ffn_w_out": ffn_w_out,
            "b_norm_mix": b_norm_mix, "attn_w_q": attn_w_q, "attn_w_o": attn_w_o,
            "kv_norm": kv_norm, "w_kv": w_kv, "final_norm": final_norm}


def _fwd_reference(x, s5_lam_re, s5_lam_im, s5_log_dt, s5_b_re, s5_b_im, s5_c_re, s5_c_im,
              s5_d, s5_w_glu, a_norm_mix, ffn_norm, ffn_w_in, ffn_w_out,
              b_norm_mix, attn_w_q, attn_w_o, kv_norm, w_kv, final_norm):
    bsz, seq, _ = x.shape
    k = None
    v = None
    for layer in range(DEPTH):
        if layer < N_A_LAYERS:
            i = layer
            h = rms_norm(x, a_norm_mix[i])
            x = x + s5_mixer(h, s5_lam_re[i], s5_lam_im[i], s5_log_dt[i], s5_b_re[i], s5_b_im[i],
                             s5_c_re[i], s5_c_im[i], s5_d[i], s5_w_glu[i])
        else:
            j = layer - N_A_LAYERS
            if j == 0:
                kv = (rms_norm(x, kv_norm) @ w_kv).reshape(bsz, seq, 2, N_KV_HEADS, HEAD_DIM)
                k = kv[:, :, 0]
                v = kv[:, :, 1]
            h = rms_norm(x, b_norm_mix[j])
            x = x + dilated_mixer(h, k, v, attn_w_q[j], attn_w_o[j])
        x = x + swiglu(rms_norm(x, ffn_norm[layer]), ffn_w_in[layer], ffn_w_out[layer])
    return rms_norm(x, final_norm)


import jax as _jax
import jax.numpy as _jnp

TWIN_FORMAT = 'train_step'
FWD_PARAMS = ['x', 's5_lam_re', 's5_lam_im', 's5_log_dt', 's5_b_re', 's5_b_im', 's5_c_re', 's5_c_im', 's5_d', 's5_w_glu', 'a_norm_mix', 'ffn_norm', 'ffn_w_in', 'ffn_w_out', 'b_norm_mix', 'attn_w_q', 'attn_w_o', 'kv_norm', 'w_kv', 'final_norm']
TWIN_WEIGHTS = ['s5_lam_re', 's5_lam_im', 's5_log_dt', 's5_b_re', 's5_b_im', 's5_c_re', 's5_c_im', 's5_d', 's5_w_glu', 'a_norm_mix', 'ffn_norm', 'ffn_w_in', 'ffn_w_out', 'b_norm_mix', 'attn_w_q', 'attn_w_o', 'kv_norm', 'w_kv', 'final_norm']
TWIN_DIFF_INPUT = 'x'
TWIN_INPUTS = ['x', 's5_lam_re', 's5_lam_im', 's5_log_dt', 's5_b_re', 's5_b_im', 's5_c_re', 's5_c_im', 's5_d', 's5_w_glu', 'a_norm_mix', 'ffn_norm', 'ffn_w_in', 'ffn_w_out', 'b_norm_mix', 'attn_w_q', 'attn_w_o', 'kv_norm', 'w_kv', 'final_norm', 'loss_target', 'm_s5_lam_re', 'm_s5_lam_im', 'm_s5_log_dt', 'm_s5_b_re', 'm_s5_b_im', 'm_s5_c_re', 'm_s5_c_im', 'm_s5_d', 'm_s5_w_glu', 'm_a_norm_mix', 'm_ffn_norm', 'm_ffn_w_in', 'm_ffn_w_out', 'm_b_norm_mix', 'm_attn_w_q', 'm_attn_w_o', 'm_kv_norm', 'm_w_kv', 'm_final_norm', 'v_s5_lam_re', 'v_s5_lam_im', 'v_s5_log_dt', 'v_s5_b_re', 'v_s5_b_im', 'v_s5_c_re', 'v_s5_c_im', 'v_s5_d', 'v_s5_w_glu', 'v_a_norm_mix', 'v_ffn_norm', 'v_ffn_w_in', 'v_ffn_w_out', 'v_b_norm_mix', 'v_attn_w_q', 'v_attn_w_o', 'v_kv_norm', 'v_w_kv', 'v_final_norm']
TWIN_OUTPUTS = ['loss', 'grad_x', 'grad_s5_lam_re', 'grad_s5_lam_im', 'grad_s5_log_dt', 'grad_s5_b_re', 'grad_s5_b_im', 'grad_s5_c_re', 'grad_s5_c_im', 'grad_s5_d', 'grad_s5_w_glu', 'grad_a_norm_mix', 'grad_ffn_norm', 'grad_ffn_w_in', 'grad_ffn_w_out', 'grad_b_norm_mix', 'grad_attn_w_q', 'grad_attn_w_o', 'grad_kv_norm', 'grad_w_kv', 'grad_final_norm', 'delta_s5_lam_re', 'delta_s5_lam_im', 'delta_s5_log_dt', 'delta_s5_b_re', 'delta_s5_b_im', 'delta_s5_c_re', 'delta_s5_c_im', 'delta_s5_d', 'delta_s5_w_glu', 'delta_a_norm_mix', 'delta_ffn_norm', 'delta_ffn_w_in', 'delta_ffn_w_out', 'delta_b_norm_mix', 'delta_attn_w_q', 'delta_attn_w_o', 'delta_kv_norm', 'delta_w_kv', 'delta_final_norm', 'new_m_s5_lam_re', 'new_m_s5_lam_im', 'new_m_s5_log_dt', 'new_m_s5_b_re', 'new_m_s5_b_im', 'new_m_s5_c_re', 'new_m_s5_c_im', 'new_m_s5_d', 'new_m_s5_w_glu', 'new_m_a_norm_mix', 'new_m_ffn_norm', 'new_m_ffn_w_in', 'new_m_ffn_w_out', 'new_m_b_norm_mix', 'new_m_attn_w_q', 'new_m_attn_w_o', 'new_m_kv_norm', 'new_m_w_kv', 'new_m_final_norm', 'new_v_s5_lam_re', 'new_v_s5_lam_im', 'new_v_s5_log_dt', 'new_v_s5_b_re', 'new_v_s5_b_im', 'new_v_s5_c_re', 'new_v_s5_c_im', 'new_v_s5_d', 'new_v_s5_w_glu', 'new_v_a_norm_mix', 'new_v_ffn_norm', 'new_v_ffn_w_in', 'new_v_ffn_w_out', 'new_v_b_norm_mix', 'new_v_attn_w_q', 'new_v_attn_w_o', 'new_v_kv_norm', 'new_v_w_kv', 'new_v_final_norm']
TWIN_LEAF_KINDS = {'loss': 'loss', 'grad_x': 'grad_x', 'grad_s5_lam_re': 'grad_w', 'grad_s5_lam_im': 'grad_w', 'grad_s5_log_dt': 'grad_w', 'grad_s5_b_re': 'grad_w', 'grad_s5_b_im': 'grad_w', 'grad_s5_c_re': 'grad_w', 'grad_s5_c_im': 'grad_w', 'grad_s5_d': 'grad_w', 'grad_s5_w_glu': 'grad_w', 'grad_a_norm_mix': 'grad_w', 'grad_ffn_norm': 'grad_w', 'grad_ffn_w_in': 'grad_w', 'grad_ffn_w_out': 'grad_w', 'grad_b_norm_mix': 'grad_w', 'grad_attn_w_q': 'grad_w', 'grad_attn_w_o': 'grad_w', 'grad_kv_norm': 'grad_w', 'grad_w_kv': 'grad_w', 'grad_final_norm': 'grad_w', 'delta_s5_lam_re': 'delta_w', 'delta_s5_lam_im': 'delta_w', 'delta_s5_log_dt': 'delta_w', 'delta_s5_b_re': 'delta_w', 'delta_s5_b_im': 'delta_w', 'delta_s5_c_re': 'delta_w', 'delta_s5_c_im': 'delta_w', 'delta_s5_d': 'delta_w', 'delta_s5_w_glu': 'delta_w', 'delta_a_norm_mix': 'delta_w', 'delta_ffn_norm': 'delta_w', 'delta_ffn_w_in': 'delta_w', 'delta_ffn_w_out': 'delta_w', 'delta_b_norm_mix': 'delta_w', 'delta_attn_w_q': 'delta_w', 'delta_attn_w_o': 'delta_w', 'delta_kv_norm': 'delta_w', 'delta_w_kv': 'delta_w', 'delta_final_norm': 'delta_w', 'new_m_s5_lam_re': 'new_m', 'new_m_s5_lam_im': 'new_m', 'new_m_s5_log_dt': 'new_m', 'new_m_s5_b_re': 'new_m', 'new_m_s5_b_im': 'new_m', 'new_m_s5_c_re': 'new_m', 'new_m_s5_c_im': 'new_m', 'new_m_s5_d': 'new_m', 'new_m_s5_w_glu': 'new_m', 'new_m_a_norm_mix': 'new_m', 'new_m_ffn_norm': 'new_m', 'new_m_ffn_w_in': 'new_m', 'new_m_ffn_w_out': 'new_m', 'new_m_b_norm_mix': 'new_m', 'new_m_attn_w_q': 'new_m', 'new_m_attn_w_o': 'new_m', 'new_m_kv_norm': 'new_m', 'new_m_w_kv': 'new_m', 'new_m_final_norm': 'new_m', 'new_v_s5_lam_re': 'new_v', 'new_v_s5_lam_im': 'new_v', 'new_v_s5_log_dt': 'new_v', 'new_v_s5_b_re': 'new_v', 'new_v_s5_b_im': 'new_v', 'new_v_s5_c_re': 'new_v', 'new_v_s5_c_im': 'new_v', 'new_v_s5_d': 'new_v', 'new_v_s5_w_glu': 'new_v', 'new_v_a_norm_mix': 'new_v', 'new_v_ffn_norm': 'new_v', 'new_v_ffn_w_in': 'new_v', 'new_v_ffn_w_out': 'new_v', 'new_v_b_norm_mix': 'new_v', 'new_v_attn_w_q': 'new_v', 'new_v_attn_w_o': 'new_v', 'new_v_kv_norm': 'new_v', 'new_v_w_kv': 'new_v', 'new_v_final_norm': 'new_v'}


def _forward(args):
    return _fwd_reference(*[args[k] for k in FWD_PARAMS])


def _output_shape():
    def fwd():
        inp = _fwd_setup_inputs(0)
        return _fwd_reference(*[inp[k] for k in FWD_PARAMS])
    out = _jax.eval_shape(fwd)
    return out.shape, out.dtype

N_MICROBATCH = 1
ADAM_LR = 0.001
ADAM_B1 = 0.9
ADAM_B2 = 0.999
ADAM_EPS = 1e-08
ADAM_WD = 0.01
ADAM_STEP = 10
PER_EXAMPLE_BATCH_AXIS = {'x': 0, 'loss_target': 0}
SHARED_INPUTS = []
_WEIGHT_DTYPES = {'s5_lam_re': _jnp.float32, 's5_lam_im': _jnp.float32, 's5_log_dt': _jnp.float32, 's5_b_re': _jnp.float32, 's5_b_im': _jnp.float32, 's5_c_re': _jnp.float32, 's5_c_im': _jnp.float32, 's5_d': _jnp.float32, 's5_w_glu': _jnp.float32, 'a_norm_mix': _jnp.float32, 'ffn_norm': _jnp.float32, 'ffn_w_in': _jnp.float32, 'ffn_w_out': _jnp.float32, 'b_norm_mix': _jnp.float32, 'attn_w_q': _jnp.float32, 'attn_w_o': _jnp.float32, 'kv_norm': _jnp.float32, 'w_kv': _jnp.float32, 'final_norm': _jnp.float32}
MOMENT_SCALE = {'s5_lam_re': 1.817202e-03, 's5_lam_im': 1.992741e-03, 's5_log_dt': 1.231739e+00, 's5_b_re': 1.230614e-03, 's5_b_im': 1.211437e-03, 's5_c_re': 2.439677e-03, 's5_c_im': 2.452046e-03, 's5_d': 4.181545e-02, 's5_w_glu': 2.866793e-02, 'a_norm_mix': 4.457860e-02, 'ffn_norm': 6.394405e-02, 'ffn_w_in': 2.665775e-02, 'ffn_w_out': 4.346838e-02, 'b_norm_mix': 1.209642e-02, 'attn_w_q': 6.931532e-03, 'attn_w_o': 1.974243e-02, 'kv_norm': 2.259221e-02, 'w_kv': 3.238392e-02, 'final_norm': 1.600162e+01}


def _to_microbatches(a, axis):
    t = _jnp.moveaxis(a, axis, 0)
    t = t.reshape((N_MICROBATCH, t.shape[0] // N_MICROBATCH) + t.shape[1:])
    return _jnp.moveaxis(t, 1, axis + 1)


def setup_inputs(seed: int = 0) -> dict:
    inp = _fwd_setup_inputs(seed)
    key = _jax.random.fold_in(_jax.random.key(seed), 7919)
    shape, _ = _output_shape()
    out = dict(inp)
    out["loss_target"] = _jax.random.normal(_jax.random.fold_in(key, 0), shape, _jnp.float32)
    for i, name in enumerate(TWIN_WEIGHTS):
        w = inp[name].astype(_jnp.float32)
        if MOMENT_SCALE is None:
            s = _jnp.sqrt(_jnp.mean(_jnp.square(w)) + 1e-30)
        else:
            s = MOMENT_SCALE[name]
        km, kv = _jax.random.split(_jax.random.fold_in(key, i + 1))
        out[name] = w
        out["m_" + name] = s * _jax.random.normal(km, w.shape, _jnp.float32)
        out["v_" + name] = (s * s) * _jax.random.uniform(kv, w.shape, _jnp.float32, 0.5, 1.5)
    if N_MICROBATCH > 1:
        for name, axis in PER_EXAMPLE_BATCH_AXIS.items():
            out[name] = _to_microbatches(out[name], axis)
    return {'x': out['x'], 's5_lam_re': out['s5_lam_re'], 's5_lam_im': out['s5_lam_im'], 's5_log_dt': out['s5_log_dt'], 's5_b_re': out['s5_b_re'], 's5_b_im': out['s5_b_im'], 's5_c_re': out['s5_c_re'], 's5_c_im': out['s5_c_im'], 's5_d': out['s5_d'], 's5_w_glu': out['s5_w_glu'], 'a_norm_mix': out['a_norm_mix'], 'ffn_norm': out['ffn_norm'], 'ffn_w_in': out['ffn_w_in'], 'ffn_w_out': out['ffn_w_out'], 'b_norm_mix': out['b_norm_mix'], 'attn_w_q': out['attn_w_q'], 'attn_w_o': out['attn_w_o'], 'kv_norm': out['kv_norm'], 'w_kv': out['w_kv'], 'final_norm': out['final_norm'], 'loss_target': out['loss_target'], 'm_s5_lam_re': out['m_s5_lam_re'], 'm_s5_lam_im': out['m_s5_lam_im'], 'm_s5_log_dt': out['m_s5_log_dt'], 'm_s5_b_re': out['m_s5_b_re'], 'm_s5_b_im': out['m_s5_b_im'], 'm_s5_c_re': out['m_s5_c_re'], 'm_s5_c_im': out['m_s5_c_im'], 'm_s5_d': out['m_s5_d'], 'm_s5_w_glu': out['m_s5_w_glu'], 'm_a_norm_mix': out['m_a_norm_mix'], 'm_ffn_norm': out['m_ffn_norm'], 'm_ffn_w_in': out['m_ffn_w_in'], 'm_ffn_w_out': out['m_ffn_w_out'], 'm_b_norm_mix': out['m_b_norm_mix'], 'm_attn_w_q': out['m_attn_w_q'], 'm_attn_w_o': out['m_attn_w_o'], 'm_kv_norm': out['m_kv_norm'], 'm_w_kv': out['m_w_kv'], 'm_final_norm': out['m_final_norm'], 'v_s5_lam_re': out['v_s5_lam_re'], 'v_s5_lam_im': out['v_s5_lam_im'], 'v_s5_log_dt': out['v_s5_log_dt'], 'v_s5_b_re': out['v_s5_b_re'], 'v_s5_b_im': out['v_s5_b_im'], 'v_s5_c_re': out['v_s5_c_re'], 'v_s5_c_im': out['v_s5_c_im'], 'v_s5_d': out['v_s5_d'], 'v_s5_w_glu': out['v_s5_w_glu'], 'v_a_norm_mix': out['v_a_norm_mix'], 'v_ffn_norm': out['v_ffn_norm'], 'v_ffn_w_in': out['v_ffn_w_in'], 'v_ffn_w_out': out['v_ffn_w_out'], 'v_b_norm_mix': out['v_b_norm_mix'], 'v_attn_w_q': out['v_attn_w_q'], 'v_attn_w_o': out['v_attn_w_o'], 'v_kv_norm': out['v_kv_norm'], 'v_w_kv': out['v_w_kv'], 'v_final_norm': out['v_final_norm']}


def _loss(weights, diff, rest, loss_target):
    with _jax.named_scope("forward"):
        args = {**rest, TWIN_DIFF_INPUT: diff, **{k: w.astype(_WEIGHT_DTYPES[k]) for k, w in weights.items()}}
        y = _forward(args)
    with _jax.named_scope("loss_head"):
        err = _jnp.square(y.astype(_jnp.float32) - loss_target)
        return 0.5 * _jnp.sum(_jnp.mean(err, axis=-1)) if err.ndim else 0.5 * err


def _adamw(w, g, m, v):
    m = ADAM_B1 * m + (1.0 - ADAM_B1) * g
    v = ADAM_B2 * v + (1.0 - ADAM_B2) * _jnp.square(g)
    m_hat = m / (1.0 - ADAM_B1 ** ADAM_STEP)
    v_hat = v / (1.0 - ADAM_B2 ** ADAM_STEP)
    delta = -ADAM_LR * (m_hat / (_jnp.sqrt(v_hat) + ADAM_EPS) + ADAM_WD * w)
    return delta, m, v


def reference(x, s5_lam_re, s5_lam_im, s5_log_dt, s5_b_re, s5_b_im, s5_c_re, s5_c_im, s5_d, s5_w_glu, a_norm_mix, ffn_norm, ffn_w_in, ffn_w_out, b_norm_mix, attn_w_q, attn_w_o, kv_norm, w_kv, final_norm, loss_target, m_s5_lam_re, m_s5_lam_im, m_s5_log_dt, m_s5_b_re, m_s5_b_im, m_s5_c_re, m_s5_c_im, m_s5_d, m_s5_w_glu, m_a_norm_mix, m_ffn_norm, m_ffn_w_in, m_ffn_w_out, m_b_norm_mix, m_attn_w_q, m_attn_w_o, m_kv_norm, m_w_kv, m_final_norm, v_s5_lam_re, v_s5_lam_im, v_s5_log_dt, v_s5_b_re, v_s5_b_im, v_s5_c_re, v_s5_c_im, v_s5_d, v_s5_w_glu, v_a_norm_mix, v_ffn_norm, v_ffn_w_in, v_ffn_w_out, v_b_norm_mix, v_attn_w_q, v_attn_w_o, v_kv_norm, v_w_kv, v_final_norm):
    given = dict(x=x, s5_lam_re=s5_lam_re, s5_lam_im=s5_lam_im, s5_log_dt=s5_log_dt, s5_b_re=s5_b_re, s5_b_im=s5_b_im, s5_c_re=s5_c_re, s5_c_im=s5_c_im, s5_d=s5_d, s5_w_glu=s5_w_glu, a_norm_mix=a_norm_mix, ffn_norm=ffn_norm, ffn_w_in=ffn_w_in, ffn_w_out=ffn_w_out, b_norm_mix=b_norm_mix, attn_w_q=attn_w_q, attn_w_o=attn_w_o, kv_norm=kv_norm, w_kv=w_kv, final_norm=final_norm, loss_target=loss_target, m_s5_lam_re=m_s5_lam_re, m_s5_lam_im=m_s5_lam_im, m_s5_log_dt=m_s5_log_dt, m_s5_b_re=m_s5_b_re, m_s5_b_im=m_s5_b_im, m_s5_c_re=m_s5_c_re, m_s5_c_im=m_s5_c_im, m_s5_d=m_s5_d, m_s5_w_glu=m_s5_w_glu, m_a_norm_mix=m_a_norm_mix, m_ffn_norm=m_ffn_norm, m_ffn_w_in=m_ffn_w_in, m_ffn_w_out=m_ffn_w_out, m_b_norm_mix=m_b_norm_mix, m_attn_w_q=m_attn_w_q, m_attn_w_o=m_attn_w_o, m_kv_norm=m_kv_norm, m_w_kv=m_w_kv, m_final_norm=m_final_norm, v_s5_lam_re=v_s5_lam_re, v_s5_lam_im=v_s5_lam_im, v_s5_log_dt=v_s5_log_dt, v_s5_b_re=v_s5_b_re, v_s5_b_im=v_s5_b_im, v_s5_c_re=v_s5_c_re, v_s5_c_im=v_s5_c_im, v_s5_d=v_s5_d, v_s5_w_glu=v_s5_w_glu, v_a_norm_mix=v_a_norm_mix, v_ffn_norm=v_ffn_norm, v_ffn_w_in=v_ffn_w_in, v_ffn_w_out=v_ffn_w_out, v_b_norm_mix=v_b_norm_mix, v_attn_w_q=v_attn_w_q, v_attn_w_o=v_attn_w_o, v_kv_norm=v_kv_norm, v_w_kv=v_w_kv, v_final_norm=v_final_norm)
    weights = {n: given[n] for n in TWIN_WEIGHTS}
    shared = {n: given[n] for n in SHARED_INPUTS}
    per_example = {n: given[n] for n in ['x']}
    grad_fn = _jax.value_and_grad(_loss, argnums=(0, 1))

    def one_microbatch(ex, loss_target):
        ex = dict(ex)
        diff = ex.pop(TWIN_DIFF_INPUT)
        return grad_fn(weights, diff, {**shared, **ex}, loss_target)

    if N_MICROBATCH == 1:
        loss, (grad_w, grad_x) = one_microbatch(per_example, given["loss_target"])
    else:
        def body(carry, xs):
            loss_sum, grad_sum = carry
            l_k, (gw_k, gx_k) = one_microbatch(xs[0], xs[1])
            with _jax.named_scope("update"):
                return (loss_sum + l_k, _jax.tree.map(_jnp.add, grad_sum, gw_k)), gx_k

        init = (_jnp.zeros((), _jnp.float32), _jax.tree.map(_jnp.zeros_like, weights))
        (loss, grad_w), grad_x = _jax.lax.scan(body, init, (per_example, given["loss_target"]))
    with _jax.named_scope("update"):
        delta_w, new_m, new_v = {}, {}, {}
        for n in TWIN_WEIGHTS:
            delta_w[n], new_m[n], new_v[n] = _adamw(weights[n], grad_w[n], given["m_" + n], given["v_" + n])
    return (loss, grad_x, *[grad_w[n] for n in TWIN_WEIGHTS], *[delta_w[n] for n in TWIN_WEIGHTS],
            *[new_m[n] for n in TWIN_WEIGHTS], *[new_v[n] for n in TWIN_WEIGHTS])
```

```python
import functools
import math

import jax
import jax.numpy as jnp
from jax import lax
from jax.experimental import pallas as pl
from jax.experimental.pallas import tpu as pltpu

F32 = jnp.float32
BF16 = jnp.bfloat16

EPS = 1e-6
NEG_INF = -1e30
HEAD_DIM = 128
N_KV_HEADS = 4
DILATIONS = (1, 4, 16)
ATT_BLK = 128
S5_C = 16
S5_P = 64
S5_GB = 16
S5_CH = S5_GB * S5_C
S5_W = S5_GB * S5_P
SCAN_LANES = 256
N_DEV = 8

ADAM_LR = 0.001
ADAM_B1 = 0.9
ADAM_B2 = 0.999
ADAM_EPS = 1e-08
ADAM_WD = 0.01
ADAM_STEP = 10

VMEM_LIMIT_BYTES = 56 * 1024 * 1024
MESH = pl.DeviceIdType.MESH
ANY = pl.BlockSpec(memory_space=pl.ANY)


def _tile(n, pref, align=128):
    t = (min(pref, n) // align) * align
    while t >= align:
        if n % t == 0:
            return t
        t -= align
    return n


def _params(sem):
    return pltpu.CompilerParams(dimension_semantics=sem, vmem_limit_bytes=VMEM_LIMIT_BYTES)


def _sigmoid(x):
    return 1.0 / (1.0 + jnp.exp(-x))


NN = (((1,), (0,)), ((), ()))
NT = (((1,), (1,)), ((), ()))
TN = (((0,), (0,)), ((), ()))


def _dot(a, b, dims=NN):
    return lax.dot_general(a, b, dims, preferred_element_type=F32)


def _matmul(name, grid, ins, in_specs, products, dims, out_shapes, out_specs, acc_shapes, epilogue):
    n_in, n_out, nk = len(ins), len(out_shapes), grid[2]

    def body(*refs):
        in_refs = refs[:n_in]
        out_refs = refs[n_in:n_in + n_out]
        acc_refs = refs[n_in + n_out:]

        def prods():
            vals = [None] * len(acc_shapes)
            for ai, bi, ci in products:
                d = _dot(in_refs[ai][...], in_refs[bi][...], dims)
                vals[ci] = d if vals[ci] is None else vals[ci] + d
            return vals

        if nk == 1:
            epilogue(in_refs, out_refs, prods())
        else:
            k = pl.program_id(2)

            @pl.when(k == 0)
            def _():
                for a in acc_refs:
                    a[...] = jnp.zeros_like(a)

            for a, v in zip(acc_refs, prods()):
                a[...] += v

            @pl.when(k == nk - 1)
            def _():
                epilogue(in_refs, out_refs, [a[...] for a in acc_refs])

    scratch = [] if nk == 1 else [pltpu.VMEM(s, F32) for s in acc_shapes]
    return pl.pallas_call(
        body, grid=grid, in_specs=in_specs, out_specs=out_specs, out_shape=out_shapes,
        scratch_shapes=scratch, compiler_params=_params(("parallel", "parallel", "arbitrary")),
        name=name)(*ins)


def _mm_dual_fwd(name, a, w, res, kind):
    T, K = a.shape
    N = w.shape[1] // 2
    tm, tn = _tile(T, 512), _tile(N, 512)
    nj = N // tn
    grid = (T // tm, nj, 1)
    ins = [a, w, w]
    in_specs = [pl.BlockSpec((tm, K), lambda i, j, k: (i, 0)),
                pl.BlockSpec((K, tn), lambda i, j, k: (0, j)),
                pl.BlockSpec((K, tn), lambda i, j, k: (0, j + nj))]
    pair_spec = pl.BlockSpec((2, tm, tn), lambda i, j, k: (0, i, j))
    tile_spec = pl.BlockSpec((tm, tn), lambda i, j, k: (i, j))
    if kind == "glu":
        ins.append(res)
        in_specs.append(tile_spec)

        def epilogue(in_refs, out_refs, accs):
            val, gate = accs
            out_refs[0][...] = in_refs[3][...] + val * _sigmoid(gate)
            out_refs[1][0] = val.astype(BF16)
            out_refs[1][1] = gate.astype(BF16)

        out_shapes = [jax.ShapeDtypeStruct((T, N), F32), jax.ShapeDtypeStruct((2, T, N), BF16)]
        out_specs = [tile_spec, pair_spec]
    else:
        def epilogue(in_refs, out_refs, accs):
            g, u = accs
            out_refs[0][0] = g.astype(BF16)
            out_refs[0][1] = u.astype(BF16)
            out_refs[1][...] = (g * _sigmoid(g) * u).astype(BF16)

        out_shapes = [jax.ShapeDtypeStruct((2, T, N), BF16), jax.ShapeDtypeStruct((T, N), BF16)]
        out_specs = [pair_spec, tile_spec]
    return _matmul(name, grid, ins, in_specs, [(0, 1, 0), (0, 2, 1)], NN, out_shapes, out_specs,
                   [(tm, tn), (tm, tn)], epilogue)


def _mm_kv(name, a, w):
    T, K = a.shape
    N = w.shape[1] // 2
    tm, tn = _tile(T, 512), _tile(N, 512)
    nj = N // tn
    tile_spec = pl.BlockSpec((tm, tn), lambda i, j, k: (i, j))

    def epilogue(in_refs, out_refs, accs):
        out_refs[0][...] = accs[0].astype(BF16)
        out_refs[1][...] = accs[1].astype(BF16)

    return _matmul(name, (T // tm, nj, 1), [a, w, w],
                   [pl.BlockSpec((tm, K), lambda i, j, k: (i, 0)),
                    pl.BlockSpec((K, tn), lambda i, j, k: (0, j)),
                    pl.BlockSpec((K, tn), lambda i, j, k: (0, j + nj))],
                   [(0, 1, 0), (0, 2, 1)], NN,
                   [jax.ShapeDtypeStruct((T, N), BF16)] * 2, [tile_spec, tile_spec],
                   [(tm, tn), (tm, tn)], epilogue)


def _mm_nn(name, a, w, res=None, out_dtype=F32):
    T, K = a.shape
    N = w.shape[1]
    tm, tn, tk = _tile(T, 1024), _tile(N, 1024), _tile(K, 2048 if K <= 2048 else 512)
    grid = (T // tm, N // tn, K // tk)
    tile_spec = pl.BlockSpec((tm, tn), lambda i, j, k: (i, j))
    ins = [a, w]
    in_specs = [pl.BlockSpec((tm, tk), lambda i, j, k: (i, k)),
                pl.BlockSpec((tk, tn), lambda i, j, k: (k, j))]
    if res is not None:
        ins.append(res)
        in_specs.append(tile_spec)

    def epilogue(in_refs, out_refs, accs):
        v = accs[0]
        if res is not None:
            v = v + in_refs[2][...]
        out_refs[0][...] = v.astype(out_dtype)

    return _matmul(name, grid, ins, in_specs, [(0, 1, 0)], NN,
                   [jax.ShapeDtypeStruct((T, N), out_dtype)], [tile_spec], [(tm, tn)], epilogue)[0]


def _mm_nt(name, a_list, w, out_dtype=F32):
    T, Np = a_list[0].shape
    Ko = w.shape[0]
    n_parts = len(a_list)
    tm, tn, tk = _tile(T, 1024), _tile(Ko, 1024), _tile(Np, 512)
    nkp = Np // tk
    grid = (T // tm, Ko // tn, nkp)
    ins = list(a_list) + [w] * n_parts
    in_specs = [pl.BlockSpec((tm, tk), lambda i, j, k: (i, k)) for _ in a_list]
    in_specs += [pl.BlockSpec((tn, tk), functools.partial(lambda i, j, k, p: (j, p * nkp + k), p=p))
                 for p in range(n_parts)]
    products = [(p, n_parts + p, 0) for p in range(n_parts)]

    def epilogue(in_refs, out_refs, accs):
        out_refs[0][...] = accs[0].astype(out_dtype)

    return _matmul(name, grid, ins, in_specs, products, NT,
                   [jax.ShapeDtypeStruct((T, Ko), out_dtype)],
                   [pl.BlockSpec((tm, tn), lambda i, j, k: (i, j))], [(tm, tn)], epilogue)[0]


def _mm_nt_pair(name, a3, w):
    _, T, N = a3.shape
    Ko = w.shape[0]
    tm, tn, tk = _tile(T, 1024), _tile(Ko, 1024), _tile(N, 512)
    nkh = N // tk
    grid = (T // tm, Ko // tn, 2 * nkh)

    def epilogue(in_refs, out_refs, accs):
        out_refs[0][...] = accs[0]

    return _matmul(name, grid, [a3, w],
                   [pl.BlockSpec((None, tm, tk), lambda i, j, k: (k // nkh, i, k % nkh)),
                    pl.BlockSpec((tn, tk), lambda i, j, k: (j, k))],
                   [(0, 1, 0)], NT, [jax.ShapeDtypeStruct((T, Ko), F32)],
                   [pl.BlockSpec((tm, tn), lambda i, j, k: (i, j))], [(tm, tn)], epilogue)[0]


def _mm_nt_ffn_bwd(name, dx, w_out, gu):
    T, D = dx.shape
    Fh = w_out.shape[0]
    tm, tn = _tile(T, 512), _tile(Fh, 512)
    pair_spec = pl.BlockSpec((2, tm, tn), lambda i, j, k: (0, i, j))

    def epilogue(in_refs, out_refs, accs):
        da = accs[0]
        g = in_refs[2][0].astype(F32)
        u = in_refs[2][1].astype(F32)
        s = _sigmoid(g)
        out_refs[0][0] = (da * u * (s * (1.0 + g * (1.0 - s)))).astype(BF16)
        out_refs[0][1] = (da * (g * s)).astype(BF16)

    return _matmul(name, (T // tm, Fh // tn, 1), [dx, w_out, gu],
                   [pl.BlockSpec((tm, D), lambda i, j, k: (i, 0)),
                    pl.BlockSpec((tn, D), lambda i, j, k: (j, 0)),
                    pair_spec],
                   [(0, 1, 0)], NT, [jax.ShapeDtypeStruct((2, T, Fh), BF16)], [pair_spec],
                   [(tm, tn)], epilogue)[0]


def _mm_tn(name, a, d):
    T, Ko = a.shape
    N = d.shape[1]
    to, tn, tk = _tile(Ko, 1024), _tile(N, 1024), _tile(T, 1024)
    grid = (Ko // to, N // tn, T // tk)

    def epilogue(in_refs, out_refs, accs):
        out_refs[0][...] = accs[0]

    return _matmul(name, grid, [a, d],
                   [pl.BlockSpec((tk, to), lambda i, j, k: (k, i)),
                    pl.BlockSpec((tk, tn), lambda i, j, k: (k, j))],
                   [(0, 1, 0)], TN, [jax.ShapeDtypeStruct((Ko, N), F32)],
                   [pl.BlockSpec((to, tn), lambda i, j, k: (i, j))], [(to, tn)], epilogue)[0]


def _mm_tn_pair(name, a, d3):
    T, Ko = a.shape
    N = d3.shape[2]
    to, tn, tk = _tile(Ko, 2048), _tile(N, 512), _tile(T, 1024)
    njh = N // tn
    grid = (Ko // to, 2 * njh, T // tk)

    def epilogue(in_refs, out_refs, accs):
        out_refs[0][...] = accs[0]

    return _matmul(name, grid, [a, d3],
                   [pl.BlockSpec((tk, to), lambda i, j, k: (k, i)),
                    pl.BlockSpec((None, tk, tn), lambda i, j, k: (j // njh, k, j % njh))],
                   [(0, 1, 0)], TN, [jax.ShapeDtypeStruct((Ko, 2 * N), F32)],
                   [pl.BlockSpec((to, tn), lambda i, j, k: (i, j))], [(to, tn)], epilogue)[0]


def _rms_fwd(name, x, gains, dtypes):
    T, D = x.shape
    n = len(gains)
    tr = _tile(T, 512, 8)

    def body(x_ref, *refs):
        xv = x_ref[...]
        xr = xv * lax.rsqrt(jnp.mean(xv * xv, axis=-1, keepdims=True) + EPS)
        for g_ref, o_ref in zip(refs[:n], refs[n:]):
            o_ref[...] = (xr * g_ref[...]).astype(o_ref.dtype)

    row = pl.BlockSpec((tr, D), lambda i: (i, 0))
    vec = pl.BlockSpec((1, D), lambda i: (0, 0))
    return pl.pallas_call(
        body, grid=(T // tr,), in_specs=[row] + [vec] * n, out_specs=[row] * n,
        out_shape=[jax.ShapeDtypeStruct((T, D), dt) for dt in dtypes],
        compiler_params=_params(("parallel",)), name=name)(x, *gains)


def _rms_bwd(name, x, dres, gains, dhs):
    T, D = x.shape
    n = len(gains)
    tr = _tile(T, 256, 8)

    def body(x_ref, dres_ref, *refs):
        g_refs, dh_refs = refs[:n], refs[n:2 * n]
        dx_ref, dxb_ref = refs[2 * n], refs[2 * n + 1]
        dg_refs = refs[2 * n + 2:]
        xv = x_ref[...]
        r = lax.rsqrt(jnp.mean(xv * xv, axis=-1, keepdims=True) + EPS)
        xr = xv * r
        w = None
        for g_ref, dh_ref, dg_ref in zip(g_refs, dh_refs, dg_refs):
            dh = dh_ref[...].astype(F32)

            @pl.when(pl.program_id(0) == 0)
            def _():
                dg_ref[...] = jnp.zeros_like(dg_ref)

            dg_ref[...] += jnp.sum(dh * xr, axis=0, keepdims=True)
            wi = dh * g_ref[...]
            w = wi if w is None else w + wi
        dx = dres_ref[...] + r * (w - xr * jnp.mean(w * xr, axis=-1, keepdims=True))
        dx_ref[...] = dx
        dxb_ref[...] = dx.astype(BF16)

    row = pl.BlockSpec((tr, D), lambda i: (i, 0))
    vec = pl.BlockSpec((1, D), lambda i: (0, 0))
    outs = pl.pallas_call(
        body, grid=(T // tr,), in_specs=[row, row] + [vec] * n + [row] * n,
        out_specs=[row, row] + [vec] * n,
        out_shape=[jax.ShapeDtypeStruct((T, D), F32), jax.ShapeDtypeStruct((T, D), BF16)]
        + [jax.ShapeDtypeStruct((1, D), F32)] * n,
        compiler_params=_params(("arbitrary",)), name=name)(x, dres, *gains, *dhs)
    return outs[0], outs[1], outs[2:]


def _loss_head(x, gain, target):
    T, D = x.shape
    tr = _tile(T, 256, 8)

    def body(x_ref, g_ref, t_ref, loss_ref, dx_ref, dxb_ref, dg_ref):
        @pl.when(pl.program_id(0) == 0)
        def _():
            loss_ref[...] = jnp.zeros_like(loss_ref)
            dg_ref[...] = jnp.zeros_like(dg_ref)

        xv = x_ref[...]
        r = lax.rsqrt(jnp.mean(xv * xv, axis=-1, keepdims=True) + EPS)
        xr = xv * r
        err = xr * g_ref[...] - t_ref[...]
        part = jnp.sum(jnp.sum(err * err, axis=-1, keepdims=True), axis=0, keepdims=True) * (0.5 / D)
        loss_ref[...] += jnp.broadcast_to(part, loss_ref.shape)
        dy = err * (1.0 / D)
        dg_ref[...] += jnp.sum(dy * xr, axis=0, keepdims=True)
        w = dy * g_ref[...]
        dx = r * (w - xr * jnp.mean(w * xr, axis=-1, keepdims=True))
        dx_ref[...] = dx
        dxb_ref[...] = dx.astype(BF16)

    row = pl.BlockSpec((tr, D), lambda i: (i, 0))
    vec = pl.BlockSpec((1, D), lambda i: (0, 0))
    return pl.pallas_call(
        body, grid=(T // tr,), in_specs=[row, vec, row],
        out_specs=[pl.BlockSpec((8, 128), lambda i: (0, 0)), row, row, vec],
        out_shape=[jax.ShapeDtypeStruct((8, 128), F32), jax.ShapeDtypeStruct((T, D), F32),
                   jax.ShapeDtypeStruct((T, D), BF16), jax.ShapeDtypeStruct((1, D), F32)],
        compiler_params=_params(("arbitrary",)), name="loss_head")(x, gain, target)


def _glu_bwd(dmix, vg):
    T, N = dmix.shape
    tr, tc = _tile(T, 512, 8), _tile(N, 1024)

    def body(d_ref, vg_ref, o_ref):
        d = d_ref[...]
        val = vg_ref[0].astype(F32)
        s = _sigmoid(vg_ref[1].astype(F32))
        o_ref[0] = (d * s).astype(BF16)
        o_ref[1] = (d * val * s * (1.0 - s)).astype(BF16)

    pair = pl.BlockSpec((2, tr, tc), lambda i, j: (0, i, j))
    return pl.pallas_call(
        body, grid=(T // tr, N // tc), in_specs=[pl.BlockSpec((tr, tc), lambda i, j: (i, j)), pair],
        out_specs=pair, out_shape=jax.ShapeDtypeStruct((2, T, N), BF16),
        compiler_params=_params(("parallel", "parallel")), name="glu_bwd")(dmix, vg)


def _cmul(xr, xi, yr, yi):
    return xr * yr - xi * yi, xr * yi + xi * yr


def _s5_consts(ar, ai, reverse):
    w = ar.shape[1]
    if reverse:
        ai = -ai
    p = [None] * 9
    p[1] = (ar, ai)
    p[2] = _cmul(*p[1], *p[1])
    p[3] = _cmul(*p[2], *p[1])
    p[4] = _cmul(*p[2], *p[2])
    p[5] = _cmul(*p[4], *p[1])
    p[6] = _cmul(*p[4], *p[2])
    p[7] = _cmul(*p[4], *p[3])
    p[8] = _cmul(*p[4], *p[4])
    row = lax.broadcasted_iota(jnp.int32, (8, w), 0)

    def bc(v):
        return jnp.broadcast_to(v, (8, w))

    outs = []
    for s in (1, 2, 4):
        valid = (row < 8 - s) if reverse else (row >= s)
        outs.append(jnp.where(valid, bc(p[s][0]), 0.0))
        outs.append(jnp.where(valid, bc(p[s][1]), 0.0))
    pr = jnp.zeros((8, w), F32)
    pi = jnp.zeros((8, w), F32)
    for r in range(8):
        e = 8 - r if reverse else r + 1
        pr = jnp.where(row == r, bc(p[e][0]), pr)
        pi = jnp.where(row == r, bc(p[e][1]), pi)
    return outs + [pr, pi]


def _s5_scan(xr_ref, xi_ref, consts_ref, cr_ref, ci_ref, reverse):
    tc, w = xr_ref.shape
    n_tb = tc // 8
    for lb in range(w // SCAN_LANES):
        ls = slice(lb * SCAN_LANES, (lb + 1) * SCAN_LANES)
        cs = [consts_ref[i, :, ls] for i in range(8)]

        def step(t, carry, ls=ls, cs=cs):
            cr, ci = carry
            tb = n_tb - 1 - t if reverse else t
            r0 = pl.multiple_of(tb * 8, 8)
            xr = xr_ref[pl.ds(r0, 8), ls]
            xi = xi_ref[pl.ds(r0, 8), ls]
            for si, s in enumerate((1, 2, 4)):
                sh = 8 - s if reverse else s
                sr = pltpu.roll(xr, sh, 0)
                sj = pltpu.roll(xi, sh, 0)
                a_r, a_i = cs[2 * si], cs[2 * si + 1]
                xr, xi = xr + a_r * sr - a_i * sj, xi + a_r * sj + a_i * sr
            p_r, p_i = cs[6], cs[7]
            xr, xi = xr + p_r * cr - p_i * ci, xi + p_r * ci + p_i * cr
            xr_ref[pl.ds(r0, 8), ls] = xr
            xi_ref[pl.ds(r0, 8), ls] = xi
            last = 0 if reverse else 7
            return xr[last:last + 1], xi[last:last + 1]

        cr, ci = lax.fori_loop(0, n_tb, step, (cr_ref[:, ls], ci_ref[:, ls]), unroll=2)
        cr_ref[:, ls] = cr
        ci_ref[:, ls] = ci


def _gelu(y):
    k = math.sqrt(2.0 / math.pi)
    return 0.5 * y * (1.0 + jnp.tanh(k * (y + 0.044715 * (y * y * y))))


def _gelu_grad(y):
    k = math.sqrt(2.0 / math.pi)
    t = jnp.tanh(k * (y + 0.044715 * (y * y * y)))
    return 0.5 * (1.0 + t) + 0.5 * y * (1.0 - t * t) * (k * (1.0 + 3.0 * 0.044715 * (y * y)))


def _s5_specs(tc, nch, rev):
    def ch(c):
        return nch - 1 - c if rev else c

    return dict(
        act=pl.BlockSpec((tc, S5_CH), lambda i, c: (ch(c), i)),
        bb=pl.BlockSpec((None, S5_CH, S5_W), lambda i, c: (i, 0, 0)),
        cc=pl.BlockSpec((None, S5_W, S5_CH), lambda i, c: (i, 0, 0)),
        a=pl.BlockSpec((1, S5_W), lambda i, c: (0, i)),
        d=pl.BlockSpec((1, S5_CH), lambda i, c: (0, i)),
        h=pl.BlockSpec((None, 2, S5_W), lambda i, c: (ch(c), 0, i)),
    )


def _s5_fwd(u, bbr, bbi, ccr, cci, a_re, a_im, d_skip):
    T, D = u.shape
    nb = D // S5_CH
    tc = _tile(T, 512, 8)
    nch = T // tc
    sp = _s5_specs(tc, nch, False)

    def body(u_ref, bbr_ref, bbi_ref, ccr_ref, cci_ref, ar_ref, ai_ref, d_ref, z_ref, h_ref,
             xr, xi, consts, cr, ci):
        @pl.when(pl.program_id(1) == 0)
        def _():
            for idx, v in enumerate(_s5_consts(ar_ref[...], ai_ref[...], False)):
                consts[idx] = v
            cr[...] = jnp.zeros_like(cr)
            ci[...] = jnp.zeros_like(ci)

        h_ref[0:1, :] = cr[...]
        h_ref[1:2, :] = ci[...]
        uv = u_ref[...]
        ub = uv.astype(BF16)
        xr[...] = _dot(ub, bbr_ref[...])
        xi[...] = _dot(ub, bbi_ref[...])
        _s5_scan(xr, xi, consts, cr, ci, False)
        y = (_dot(xr[...].astype(BF16), ccr_ref[...]) - _dot(xi[...].astype(BF16), cci_ref[...])
             + d_ref[...] * uv)
        z_ref[...] = _gelu(y).astype(BF16)

    return pl.pallas_call(
        body, grid=(nb, nch),
        in_specs=[sp["act"], sp["bb"], sp["bb"], sp["cc"], sp["cc"], sp["a"], sp["a"], sp["d"]],
        out_specs=[sp["act"], sp["h"]],
        out_shape=[jax.ShapeDtypeStruct((T, D), BF16), jax.ShapeDtypeStruct((nch, 2, nb * S5_W), F32)],
        scratch_shapes=[pltpu.VMEM((tc, S5_W), F32), pltpu.VMEM((tc, S5_W), F32),
                        pltpu.VMEM((8, 8, S5_W), F32), pltpu.VMEM((1, S5_W), F32),
                        pltpu.VMEM((1, S5_W), F32)],
        compiler_params=_params(("parallel", "arbitrary")), name="s5_fwd",
    )(u, bbr, bbi, ccr, cci, a_re, a_im, d_skip)


def _s5_bwd(u, dz, h0, bbr, bbi, ccr, cci, a_re, a_im, d_skip):
    T, D = u.shape
    nb = D // S5_CH
    tc = _tile(T, 512, 8)
    nch = T // tc
    n_tb = tc // 8
    sp = _s5_specs(tc, nch, True)

    def body(u_ref, dz_ref, h_ref, bbr_ref, bbi_ref, ccr_ref, cci_ref, ar_ref, ai_ref, d_ref,
             du_ref, dd_ref, dar_ref, dai_ref, dbbr_ref, dbbi_ref, dccr_ref, dcci_ref,
             xr, xi, lr, li, cf, cb, fr, fi, br, bi, accr, acci):
        c = pl.program_id(1)

        @pl.when(c == 0)
        def _():
            for idx, v in enumerate(_s5_consts(ar_ref[...], ai_ref[...], False)):
                cf[idx] = v
            for idx, v in enumerate(_s5_consts(ar_ref[...], ai_ref[...], True)):
                cb[idx] = v
            for ref in (br, bi, accr, acci, dd_ref, dbbr_ref, dbbi_ref, dccr_ref, dcci_ref):
                ref[...] = jnp.zeros_like(ref)

        uv = u_ref[...]
        ub = uv.astype(BF16)
        xr[...] = _dot(ub, bbr_ref[...])
        xi[...] = _dot(ub, bbi_ref[...])
        fr[...] = h_ref[0:1, :]
        fi[...] = h_ref[1:2, :]
        _s5_scan(xr, xi, cf, fr, fi, False)
        xrb = xr[...].astype(BF16)
        xib = xi[...].astype(BF16)
        dsk = d_ref[...]
        y = _dot(xrb, ccr_ref[...]) - _dot(xib, cci_ref[...]) + dsk * uv
        dy = dz_ref[...] * _gelu_grad(y)
        dd_ref[...] += jnp.sum(dy * uv, axis=0, keepdims=True)
        dyb = dy.astype(BF16)
        dccr_ref[...] += _dot(xrb, dyb, TN)
        dcci_ref[...] += _dot(xib, dyb, TN)
        lr[...] = _dot(dyb, ccr_ref[...], NT)
        li[...] = -_dot(dyb, cci_ref[...], NT)
        _s5_scan(lr, li, cb, br, bi, True)

        for lb in range(S5_W // SCAN_LANES):
            ls = slice(lb * SCAN_LANES, (lb + 1) * SCAN_LANES)
            h_r = h_ref[0:1, ls]
            h_i = h_ref[1:2, ls]
            row = lax.broadcasted_iota(jnp.int32, (8, SCAN_LANES), 0)

            def step(tb, acc, ls=ls, h_r=h_r, h_i=h_i, row=row):
                a_r, a_i = acc
                r0 = pl.multiple_of(tb * 8, 8)
                p0 = pl.multiple_of(jnp.maximum(tb - 1, 0) * 8, 8)
                first = tb == 0
                prev_r = jnp.where(first, h_r, xr[pl.ds(p0, 8), ls][7:8])
                prev_i = jnp.where(first, h_i, xi[pl.ds(p0, 8), ls][7:8])
                xpr = jnp.where(row == 0, prev_r, pltpu.roll(xr[pl.ds(r0, 8), ls], 1, 0))
                xpi = jnp.where(row == 0, prev_i, pltpu.roll(xi[pl.ds(r0, 8), ls], 1, 0))
                l_r = lr[pl.ds(r0, 8), ls]
                l_i = li[pl.ds(r0, 8), ls]
                return a_r + l_r * xpr + l_i * xpi, a_i - l_r * xpi + l_i * xpr

            a_r, a_i = lax.fori_loop(0, n_tb, step, (accr[:, ls], acci[:, ls]))
            accr[:, ls] = a_r
            acci[:, ls] = a_i

        lrb = lr[...].astype(BF16)
        lib = li[...].astype(BF16)
        dbbr_ref[...] += _dot(ub, lrb, TN)
        dbbi_ref[...] += _dot(ub, lib, TN)
        du_ref[...] = dy * dsk + _dot(lrb, bbr_ref[...], NT) + _dot(lib, bbi_ref[...], NT)

        @pl.when(c == nch - 1)
        def _():
            dar_ref[...] = jnp.sum(accr[...], axis=0, keepdims=True)
            dai_ref[...] = jnp.sum(acci[...], axis=0, keepdims=True)

    big = pltpu.VMEM((tc, S5_W), F32)
    rowv = pltpu.VMEM((1, S5_W), F32)
    return pl.pallas_call(
        body, grid=(nb, nch),
        in_specs=[sp["act"], sp["act"], sp["h"], sp["bb"], sp["bb"], sp["cc"], sp["cc"],
                  sp["a"], sp["a"], sp["d"]],
        out_specs=[sp["act"], sp["d"], sp["a"], sp["a"], sp["bb"], sp["bb"], sp["cc"], sp["cc"]],
        out_shape=[jax.ShapeDtypeStruct((T, D), F32), jax.ShapeDtypeStruct((1, D), F32),
                   jax.ShapeDtypeStruct((1, nb * S5_W), F32), jax.ShapeDtypeStruct((1, nb * S5_W), F32),
                   jax.ShapeDtypeStruct((nb, S5_CH, S5_W), F32), jax.ShapeDtypeStruct((nb, S5_CH, S5_W), F32),
                   jax.ShapeDtypeStruct((nb, S5_W, S5_CH), F32), jax.ShapeDtypeStruct((nb, S5_W, S5_CH), F32)],
        scratch_shapes=[big, big, big, big, pltpu.VMEM((8, 8, S5_W), F32), pltpu.VMEM((8, 8, S5_W), F32),
                        rowv, rowv, rowv, rowv, pltpu.VMEM((8, S5_W), F32), pltpu.VMEM((8, S5_W), F32)],
        compiler_params=_params(("parallel", "arbitrary")), name="s5_bwd",
    )(u, dz, h0, bbr, bbi, ccr, cci, a_re, a_im, d_skip)


def _s5_disc(lr, li, ldt):
    dt = jnp.exp(ldt)
    mag = jnp.exp(lr * dt)
    ang = li * dt
    cs, sn = jnp.cos(ang), jnp.sin(ang)
    lbr, lbi = mag * cs, mag * sn
    nr = lbr - 1.0
    den = lr * lr + li * li
    f_re = (nr * lr + lbi * li) / den
    f_im = (lbi * lr - nr * li) / den
    return dt, mag, cs, sn, lbr, lbi, nr, den, f_re, f_im


def _s5_param_fwd(lr, li, ldt, bt_re, bt_im):
    c, g, p = bt_re.shape

    def body(lr_ref, li_ref, ldt_ref, br_ref, bi_ref, lbr_ref, lbi_ref, bbr_ref, bbi_ref):
        _, _, _, _, lbr, lbi, _, _, f_re, f_im = _s5_disc(lr_ref[...], li_ref[...], ldt_ref[...])
        lbr_ref[...] = lbr
        lbi_ref[...] = lbi
        for ch in range(c):
            b_r, b_i = br_ref[ch], bi_ref[ch]
            bbr_ref[ch] = f_re * b_r - f_im * b_i
            bbi_ref[ch] = f_re * b_i + f_im * b_r

    gp = jax.ShapeDtypeStruct((g, p), F32)
    cgp = jax.ShapeDtypeStruct((c, g, p), F32)
    return pl.pallas_call(body, out_shape=[gp, gp, cgp, cgp], name="s5_param_fwd")(lr, li, ldt, bt_re, bt_im)


def _s5_param_bwd(lr, li, ldt, bt_re, bt_im, dlbr, dlbi, dbbr, dbbi):
    c, g, p = bt_re.shape

    def body(lr_ref, li_ref, ldt_ref, br_ref, bi_ref, dlbr_ref, dlbi_ref, dbbr_ref, dbbi_ref,
             dlr_ref, dli_ref, dldt_ref, dbr_ref, dbi_ref):
        l_r, l_i = lr_ref[...], li_ref[...]
        dt, mag, cs, sn, lbr, lbi, nr, den, f_re, f_im = _s5_disc(l_r, l_i, ldt_ref[...])
        dfr = jnp.zeros_like(l_r)
        dfi = jnp.zeros_like(l_r)
        for ch in range(c):
            b_r, b_i = br_ref[ch], bi_ref[ch]
            g_r, g_i = dbbr_ref[ch], dbbi_ref[ch]
            dbr_ref[ch] = f_re * g_r + f_im * g_i
            dbi_ref[ch] = f_re * g_i - f_im * g_r
            dfr = dfr + g_r * b_r + g_i * b_i
            dfi = dfi + g_i * b_r - g_r * b_i
        inv = 1.0 / den
        d_nr = (dfr * l_r - dfi * l_i) * inv
        d_lbi = (dfr * l_i + dfi * l_r) * inv + dlbi_ref[...]
        d_lbr = d_nr + dlbr_ref[...]
        d_den = -(dfr * f_re + dfi * f_im) * inv
        d_mag = d_lbr * cs + d_lbi * sn
        d_ang = d_lbi * lbr - d_lbr * lbi
        dlr_ref[...] = (dfr * nr + dfi * lbi) * inv + 2.0 * d_den * l_r + d_mag * mag * dt
        dli_ref[...] = (dfr * lbi - dfi * nr) * inv + 2.0 * d_den * l_i + d_ang * dt
        dldt_ref[...] = jnp.sum(d_mag * mag * l_r + d_ang * l_i, axis=1, keepdims=True) * dt

    gp = jax.ShapeDtypeStruct((g, p), F32)
    cgp = jax.ShapeDtypeStruct((c, g, p), F32)
    return pl.pallas_call(body, out_shape=[gp, gp, jax.ShapeDtypeStruct((g, 1), F32), cgp, cgp],
                          name="s5_param_bwd")(lr, li, ldt, bt_re, bt_im, dlbr, dlbi, dbbr, dbbi)


def _block_diag(t, inner_first):
    g, c, p = t.shape
    nb = g // S5_GB
    t4 = t.reshape(nb, S5_GB, c, p)
    eye = jnp.eye(S5_GB, dtype=t.dtype)
    if inner_first:
        e = t4[:, :, :, None, :] * eye[None, :, None, :, None]
        return e.reshape(nb, S5_GB * c, S5_GB * p)
    t4 = t4.transpose(0, 1, 3, 2)
    e = t4[:, :, :, None, :] * eye[None, :, None, :, None]
    return e.reshape(nb, S5_GB * p, S5_GB * c)


def _block_diag_extract(e, inner_first, c, p):
    nb = e.shape[0]
    eye = jnp.eye(S5_GB, dtype=e.dtype)
    if inner_first:
        e5 = e.reshape(nb, S5_GB, c, S5_GB, p)
        return (e5 * eye[None, :, None, :, None]).sum(3).reshape(nb * S5_GB, c, p)
    e5 = e.reshape(nb, S5_GB, p, S5_GB, c)
    return (e5 * eye[None, :, None, :, None]).sum(3).transpose(0, 1, 3, 2).reshape(nb * S5_GB, c, p)


def _att_masks(rep, gb):
    rows = rep * ATT_BLK
    qi = lax.broadcasted_iota(jnp.int32, (rows, 2 * ATT_BLK), 0) % ATT_BLK
    si = lax.broadcasted_iota(jnp.int32, (rows, 2 * ATT_BLK), 1)
    prev = (si < ATT_BLK) & (si >= qi) & (gb > 0)
    cur = (si >= ATT_BLK) & (si - ATT_BLK <= qi)
    return prev | cur


def _stack_heads(ref, rows, rep):
    return jnp.concatenate([ref[rows, j * HEAD_DIM:(j + 1) * HEAD_DIM] for j in range(rep)], axis=0)


def _att_geometry(T, H, dil):
    n = T // dil
    rep = H // N_KV_HEADS
    rc = _tile(n, 1024, ATT_BLK)
    return n, rep, rc, n // rc, rc // ATT_BLK


def _att_fwd(q, k, v, grp, dil):
    T = q.shape[0]
    H = q.shape[1] // HEAD_DIM // len(DILATIONS)
    n, rep, rc, nrc, nblk = _att_geometry(T, H, dil)
    hw = rep * HEAD_DIM
    scale = HEAD_DIM ** -0.5
    q2 = q.reshape(n, dil * q.shape[1])
    k2 = k.reshape(n, dil * k.shape[1])
    v2 = v.reshape(n, dil * v.shape[1])
    qcols = q.shape[1] // hw
    ocols = H * HEAD_DIM // hw

    def body(q_ref, k_ref, v_ref, o_ref, l_ref):
        ch = pl.program_id(2)

        def blk(b, _):
            gb = ch * nblk + b
            rows = pl.ds(pl.multiple_of(b * ATT_BLK, ATT_BLK), ATT_BLK)
            kc = pl.ds(pl.multiple_of(gb * ATT_BLK, ATT_BLK), ATT_BLK)
            kp = pl.ds(pl.multiple_of(jnp.maximum(gb - 1, 0) * ATT_BLK, ATT_BLK), ATT_BLK)
            qs = _stack_heads(q_ref, rows, rep)
            kcat = jnp.concatenate([k_ref[kp, :], k_ref[kc, :]], axis=0)
            vcat = jnp.concatenate([v_ref[kp, :], v_ref[kc, :]], axis=0)
            s = jnp.where(_att_masks(rep, gb), _dot(qs, kcat, NT) * scale, NEG_INF)
            m = jnp.max(s, axis=-1, keepdims=True)
            p = jnp.exp(s - m)
            l = jnp.sum(p, axis=-1, keepdims=True)
            o = _dot(p.astype(BF16), vcat) / l
            lse = jnp.broadcast_to(m + jnp.log(l), (rep * ATT_BLK, HEAD_DIM))
            for j in range(rep):
                o_ref[rows, j * HEAD_DIM:(j + 1) * HEAD_DIM] = o[j * ATT_BLK:(j + 1) * ATT_BLK]
                l_ref[rows, j * HEAD_DIM:(j + 1) * HEAD_DIM] = lse[j * ATT_BLK:(j + 1) * ATT_BLK]
            return 0

        lax.fori_loop(0, nblk, blk, 0)

    qspec = pl.BlockSpec((rc, hw), lambda r, h, c: (c, r * qcols + grp * N_KV_HEADS + h))
    kspec = pl.BlockSpec((n, HEAD_DIM), lambda r, h, c: (0, r * N_KV_HEADS + h))
    ospec = pl.BlockSpec((rc, hw), lambda r, h, c: (c, r * ocols + h))
    o, l = pl.pallas_call(
        body, grid=(dil, N_KV_HEADS, nrc), in_specs=[qspec, kspec, kspec], out_specs=[ospec, ospec],
        out_shape=[jax.ShapeDtypeStruct((n, dil * H * HEAD_DIM), F32)] * 2,
        compiler_params=_params(("parallel", "parallel", "arbitrary")), name=f"att_fwd_d{dil}",
    )(q2, k2, v2)
    return o.reshape(T, H * HEAD_DIM), l.reshape(T, H * HEAD_DIM)


def _att_combine(outs, lses):
    T, W = outs[0].shape
    ng = len(outs)
    tr, tcol = _tile(T, 512, 8), _tile(W, 512)

    def body(*refs):
        o_refs, l_refs = refs[:ng], refs[ng:2 * ng]
        ob_ref, lse_ref = refs[2 * ng:]
        ls = [r[...] for r in l_refs]
        m = functools.reduce(jnp.maximum, ls)
        es = [jnp.exp(l - m) for l in ls]
        den = functools.reduce(lambda a, b: a + b, es)
        num = functools.reduce(lambda a, b: a + b, [e * o[...] for e, o in zip(es, o_refs)])
        ob_ref[...] = (num / den).astype(BF16)
        lse_ref[...] = m + jnp.log(den)

    spec = pl.BlockSpec((tr, tcol), lambda i, j: (i, j))
    return pl.pallas_call(
        body, grid=(T // tr, W // tcol), in_specs=[spec] * (2 * ng), out_specs=[spec, spec],
        out_shape=[jax.ShapeDtypeStruct((T, W), BF16), jax.ShapeDtypeStruct((T, W), F32)],
        compiler_params=_params(("parallel", "parallel")), name="att_combine")(*outs, *lses)


def _att_bwd(q, k, v, o, do, lse, grp, dil):
    T = q.shape[0]
    H = o.shape[1] // HEAD_DIM
    n, rep, rc, nrc, nblk = _att_geometry(T, H, dil)
    hw = rep * HEAD_DIM
    scale = HEAD_DIM ** -0.5
    q2 = q.reshape(n, dil * q.shape[1])
    k2 = k.reshape(n, dil * k.shape[1])
    v2 = v.reshape(n, dil * v.shape[1])
    o2 = o.reshape(n, dil * o.shape[1])
    do2 = do.reshape(n, dil * do.shape[1])
    l2 = lse.reshape(n, dil * lse.shape[1])
    qcols = q.shape[1] // hw
    ocols = H * HEAD_DIM // hw

    def body(q_ref, k_ref, v_ref, o_ref, do_ref, l_ref, dq_ref, dk_ref, dv_ref):
        ch = pl.program_id(2)

        @pl.when(ch == 0)
        def _():
            dk_ref[...] = jnp.zeros_like(dk_ref)
            dv_ref[...] = jnp.zeros_like(dv_ref)

        def blk(b, _):
            gb = ch * nblk + b
            rows = pl.ds(pl.multiple_of(b * ATT_BLK, ATT_BLK), ATT_BLK)
            kc = pl.ds(pl.multiple_of(gb * ATT_BLK, ATT_BLK), ATT_BLK)
            kp = pl.ds(pl.multiple_of(jnp.maximum(gb - 1, 0) * ATT_BLK, ATT_BLK), ATT_BLK)
            qs = _stack_heads(q_ref, rows, rep)
            dos = _stack_heads(do_ref, rows, rep)
            os_ = _stack_heads(o_ref, rows, rep)
            lss = _stack_heads(l_ref, rows, rep)
            kcat = jnp.concatenate([k_ref[kp, :], k_ref[kc, :]], axis=0)
            vcat = jnp.concatenate([v_ref[kp, :], v_ref[kc, :]], axis=0)
            s = _dot(qs, kcat, NT) * scale
            p = jnp.where(_att_masks(rep, gb), jnp.exp(s - jnp.concatenate([lss, lss], axis=1)), 0.0)
            delta = jnp.sum(dos.astype(F32) * os_.astype(F32), axis=-1, keepdims=True)
            dp = _dot(dos, vcat, NT)
            ds = (p * (dp - delta) * scale).astype(BF16)
            dvc = _dot(p.astype(BF16), dos, TN)
            dkc = _dot(ds, qs, TN)
            dqs = _dot(ds, kcat)
            for j in range(rep):
                dq_ref[rows, j * HEAD_DIM:(j + 1) * HEAD_DIM] = dqs[j * ATT_BLK:(j + 1) * ATT_BLK].astype(BF16)
            dk_ref[kc, :] += dkc[ATT_BLK:]
            dv_ref[kc, :] += dvc[ATT_BLK:]

            @pl.when(gb > 0)
            def _():
                dk_ref[kp, :] += dkc[:ATT_BLK]
                dv_ref[kp, :] += dvc[:ATT_BLK]

            return 0

        lax.fori_loop(0, nblk, blk, 0)

    qspec = pl.BlockSpec((rc, hw), lambda r, h, c: (c, r * qcols + grp * N_KV_HEADS + h))
    kspec = pl.BlockSpec((n, HEAD_DIM), lambda r, h, c: (0, r * N_KV_HEADS + h))
    ospec = pl.BlockSpec((rc, hw), lambda r, h, c: (c, r * ocols + h))
    dq, dk, dv = pl.pallas_call(
        body, grid=(dil, N_KV_HEADS, nrc), in_specs=[qspec, kspec, kspec, ospec, ospec, ospec],
        out_specs=[ospec, kspec, kspec],
        out_shape=[jax.ShapeDtypeStruct((n, dil * H * HEAD_DIM), BF16),
                   jax.ShapeDtypeStruct((n, dil * N_KV_HEADS * HEAD_DIM), F32),
                   jax.ShapeDtypeStruct((n, dil * N_KV_HEADS * HEAD_DIM), F32)],
        compiler_params=_params(("parallel", "parallel", "arbitrary")), name=f"att_bwd_d{dil}",
    )(q2, k2, v2, o2, do2, l2)
    return (dq.reshape(T, H * HEAD_DIM), dk.reshape(T, N_KV_HEADS * HEAD_DIM),
            dv.reshape(T, N_KV_HEADS * HEAD_DIM))


def _sum_kv(dks, dvs):
    T, W = dks[0].shape
    ng = len(dks)
    tr = _tile(T, 512, 8)

    def body(*refs):
        o_ref = refs[2 * ng]
        o_ref[0] = functools.reduce(lambda a, b: a + b, [r[...] for r in refs[:ng]]).astype(BF16)
        o_ref[1] = functools.reduce(lambda a, b: a + b, [r[...] for r in refs[ng:2 * ng]]).astype(BF16)

    spec = pl.BlockSpec((tr, W), lambda i: (i, 0))
    return pl.pallas_call(
        body, grid=(T // tr,), in_specs=[spec] * (2 * ng),
        out_specs=pl.BlockSpec((2, tr, W), lambda i: (0, i, 0)),
        out_shape=jax.ShapeDtypeStruct((2, T, W), BF16),
        compiler_params=_params(("parallel",)), name="sum_kv")(*dks, *dvs)


def _local_step(x, tgt, w):
    T, D = x.shape
    g = {}

    (u0,) = _rms_fwd("rms_a", x, [w["a_norm"]], [F32])
    lbr, lbi, bbt_re, bbt_im = _s5_param_fwd(w["lam_re"], w["lam_im"], w["log_dt"], w["bt_re"], w["bt_im"])
    a_re, a_im = lbr.reshape(1, -1), lbi.reshape(1, -1)
    bbr = _block_diag(bbt_re.transpose(1, 0, 2), True).astype(BF16)
    bbi = _block_diag(bbt_im.transpose(1, 0, 2), True).astype(BF16)
    ccr = _block_diag(w["c_re"], False).astype(BF16)
    cci = _block_diag(w["c_im"], False).astype(BF16)
    z, h0 = _s5_fwd(u0, bbr, bbi, ccr, cci, a_re, a_im, w["s5_d"])
    x1, vg = _mm_dual_fwd("glu_fwd", z, w["glu"], x, "glu")

    def ffn_fwd(xin, layer):
        (nrm,) = _rms_fwd(f"rms_f{layer}", xin, [w["ffn_norm"][layer:layer + 1]], [BF16])
        gu, act = _mm_dual_fwd(f"ffn_in{layer}", nrm, w["win"][layer], None, "ffn")
        xout = _mm_nn(f"ffn_out{layer}", act, w["wout"][layer], res=xin)
        return xout, nrm, gu, act

    x2, n1, gu0, act0 = ffn_fwd(x1, 0)
    kvn, hb = _rms_fwd("rms_b", x2, [w["kv_norm"], w["b_norm"]], [BF16, BF16])
    k, v = _mm_kv("kv_proj", kvn, w["wkv"])
    q = _mm_nn("q_proj", hb, w["wq"], out_dtype=BF16)
    outs, lses = [], []
    for grp, dil in enumerate(DILATIONS):
        o_g, l_g = _att_fwd(q, k, v, grp, dil)
        outs.append(o_g)
        lses.append(l_g)
    o, lse = _att_combine(outs, lses)
    x3 = _mm_nn("o_proj", o, w["wo"], res=x2)
    x4, n3, gu1, act1 = ffn_fwd(x3, 1)
    loss_blk, dx4, dx4b, g["final_norm"] = _loss_head(x4, w["final_norm"], tgt)

    def ffn_bwd(dx, dxb, xin, nrm, gu, act, layer):
        dgu = _mm_nt_ffn_bwd(f"ffn_dact{layer}", dxb, w["wout"][layer], gu)
        g_wout = _mm_tn(f"ffn_dwout{layer}", act, dxb)
        g_win = _mm_tn_pair(f"ffn_dwin{layer}", nrm, dgu)
        dn = _mm_nt_pair(f"ffn_dn{layer}", dgu, w["win"][layer])
        dxo, dxob, (dgn,) = _rms_bwd(f"rms_f{layer}_bwd", xin, dx, [w["ffn_norm"][layer:layer + 1]], [dn])
        return dxo, dxob, g_win, g_wout, dgn

    dx3, dx3b, g_win1, g_wout1, dfn1 = ffn_bwd(dx4, dx4b, x3, n3, gu1, act1, 1)
    do = _mm_nt("o_proj_dx", [dx3b], w["wo"], out_dtype=BF16)
    g["wo"] = _mm_tn("o_proj_dw", o, dx3b)
    dqs, dks, dvs = [], [], []
    for grp, dil in enumerate(DILATIONS):
        dq_g, dk_g, dv_g = _att_bwd(q, k, v, o, do, lse, grp, dil)
        dqs.append(dq_g)
        dks.append(dk_g)
        dvs.append(dv_g)
    dkv = _sum_kv(dks, dvs)
    dhb = _mm_nt("q_proj_dx", dqs, w["wq"])
    g["wq"] = _mm_tn("q_proj_dw", hb, jnp.concatenate(dqs, axis=1))
    dkvn = _mm_nt_pair("kv_proj_dx", dkv, w["wkv"])
    g["wkv"] = _mm_tn_pair("kv_proj_dw", kvn, dkv)
    dx2, dx2b, (g["kv_norm"], g["b_norm"]) = _rms_bwd(
        "rms_b_bwd", x2, dx3, [w["kv_norm"], w["b_norm"]], [dkvn, dhb])
    dx1, dx1b, g_win0, g_wout0, dfn0 = ffn_bwd(dx2, dx2b, x1, n1, gu0, act0, 0)
    g["win"] = [g_win0, g_win1]
    g["wout"] = [g_wout0, g_wout1]
    g["ffn_norm"] = jnp.concatenate([dfn0, dfn1], axis=0)

    dvg = _glu_bwd(dx1, vg)
    dz = _mm_nt_pair("glu_dx", dvg, w["glu"])
    g["glu"] = _mm_tn_pair("glu_dw", z, dvg)
    du, g["s5_d"], da_re, da_im, dbbr, dbbi, dccr, dcci = _s5_bwd(
        u0, dz, h0, bbr, bbi, ccr, cci, a_re, a_im, w["s5_d"])
    C, G, P = w["bt_re"].shape
    g["c_re"] = _block_diag_extract(dccr, False, C, P)
    g["c_im"] = -_block_diag_extract(dcci, False, C, P)
    g["lam_re"], g["lam_im"], g["log_dt"], g["bt_re"], g["bt_im"] = _s5_param_bwd(
        w["lam_re"], w["lam_im"], w["log_dt"], w["bt_re"], w["bt_im"],
        da_re.reshape(G, P), da_im.reshape(G, P),
        _block_diag_extract(dbbr, True, C, P).transpose(1, 0, 2),
        _block_diag_extract(dbbi, True, C, P).transpose(1, 0, 2))
    grad_x, _, (g["a_norm"],) = _rms_bwd("rms_a_bwd", x, dx1, [w["a_norm"]], [du])
    return loss_blk, grad_x, g


def _coords():
    return lax.axis_index("x"), lax.axis_index("y"), lax.axis_index("c")


def _dev_index(dev):
    return 4 * dev[0] + 2 * dev[1] + dev[2]


def _shard_window(ref, axis, width, idx):
    sl = [slice(None)] * len(ref.shape)
    sl[axis] = pl.ds(pl.multiple_of(idx * width, width), width)
    return ref.at[tuple(sl)]


def _all_gather(name, shards, axes):
    na = len(shards)
    widths = [s.shape[ax] for s, ax in zip(shards, axes)]
    out_shapes = []
    for s, ax in zip(shards, axes):
        shp = list(s.shape)
        shp[ax] *= N_DEV
        out_shapes.append(jax.ShapeDtypeStruct(tuple(shp), s.dtype))

    def body(*refs):
        ins, outs = refs[:na], refs[na:2 * na]
        send_sems, recv_sems, local_sems = refs[2 * na:]
        x, y, c = _coords()
        me, sib = (x, y, c), (x, y, 1 - c)
        chips = [(1 - x, y), (x, 1 - y), (1 - x, 1 - y)]

        def blk(a, dev):
            return _shard_window(outs[a], axes[a], widths[a], _dev_index(dev))

        def copy(a, kk, block, to, src=None):
            return pltpu.make_async_remote_copy(
                src_ref=blk(a, block) if src is None else src, dst_ref=blk(a, block),
                send_sem=send_sems.at[a, kk], recv_sem=recv_sems.at[a, kk],
                device_id=to, device_id_type=MESH)

        local = [pltpu.make_async_copy(ins[a], blk(a, me), local_sems.at[a]) for a in range(na)]
        for cp in local:
            cp.start()
        sent = []
        for a in range(na):
            first = [copy(a, 0, me, sib, src=ins[a])]
            first += [copy(a, 1 + j, me, (*chip, c), src=ins[a]) for j, chip in enumerate(chips)]
            for cp in first:
                cp.start()
            sent += first
        for a in range(na):
            for j, chip in enumerate(chips):
                copy(a, 1 + j, (*chip, c), me).wait_recv()
                fwd = copy(a, 4 + j, (*chip, c), sib)
                fwd.start()
                sent.append(fwd)
        for a in range(na):
            copy(a, 0, sib, me).wait_recv()
            for j, chip in enumerate(chips):
                copy(a, 4 + j, (*chip, 1 - c), me).wait_recv()
        for cp in sent:
            cp.wait_send()
        for cp in local:
            cp.wait()

    return pl.pallas_call(
        body, out_shape=out_shapes, in_specs=[ANY] * na, out_specs=[ANY] * na,
        scratch_shapes=[pltpu.SemaphoreType.DMA((na, 7)), pltpu.SemaphoreType.DMA((na, 7)),
                        pltpu.SemaphoreType.DMA((na,))],
        name=name)(*shards)


def _pair_exchange(grads, axes):
    na = len(grads)
    widths = [gr.shape[ax] // N_DEV for gr, ax in zip(grads, axes)]
    out_shapes = []
    for gr, ax, wd in zip(grads, axes, widths):
        shp = list(gr.shape)
        shp[ax] = wd
        out_shapes.append(jax.ShapeDtypeStruct((4, *shp), gr.dtype))

    def body(*refs):
        ins, outs = refs[:na], refs[na:2 * na]
        send_sems, recv_sems = refs[2 * na:]
        x, y, c = _coords()
        sib = (x, y, 1 - c)
        chips = [(x, y), (1 - x, y), (x, 1 - y), (1 - x, 1 - y)]
        cps = []
        for a in range(na):
            for kk, chip in enumerate(chips):
                cp = pltpu.make_async_remote_copy(
                    src_ref=_shard_window(ins[a], axes[a], widths[a], _dev_index((*chip, 1 - c))),
                    dst_ref=outs[a].at[kk], send_sem=send_sems.at[a, kk], recv_sem=recv_sems.at[a, kk],
                    device_id=sib, device_id_type=MESH)
                cp.start()
                cps.append(cp)
        for cp in cps:
            cp.wait()

    return pl.pallas_call(
        body, out_shape=out_shapes, in_specs=[ANY] * na, out_specs=[ANY] * na,
        scratch_shapes=[pltpu.SemaphoreType.DMA((na, 4)), pltpu.SemaphoreType.DMA((na, 4))],
        name="rs_pair_exchange")(*grads)


def _pair_sum(name, grad, land, axis, pos):
    wd = grad.shape[axis] // N_DEV
    shard_shape = land.shape[1:]
    rows, cols = shard_shape
    tr = _tile(rows, 256, 8)
    nt = rows // tr

    def dev_of(kk, pos_ref):
        return 4 * (pos_ref[0] ^ (kk & 1)) + 2 * (pos_ref[1] ^ (kk >> 1)) + pos_ref[2]

    if axis == 1:
        gspec = pl.BlockSpec((tr, wd), lambda kk, t, p: (t, dev_of(kk, p)))
    else:
        gspec = pl.BlockSpec((tr, cols), lambda kk, t, p: (dev_of(kk, p) * nt + t, 0))
    lspec = pl.BlockSpec((None, tr, cols), lambda kk, t, p: (kk, t, 0))
    p0spec = pl.BlockSpec((tr, cols), lambda kk, t, p: (jnp.where(kk == 0, t, nt - 1), 0))
    pbspec = pl.BlockSpec((None, tr, cols), lambda kk, t, p: (jnp.maximum(kk - 1, 0), jnp.where(kk == 0, 0, t), 0))

    def body(pos_ref, g_ref, l_ref, p0_ref, pb_ref):
        kk = pl.program_id(0)
        s = g_ref[...] + l_ref[...]

        @pl.when(kk == 0)
        def _():
            p0_ref[...] = s

        @pl.when(kk > 0)
        def _():
            pb_ref[...] = s.astype(BF16)

    return pl.pallas_call(
        body,
        grid_spec=pltpu.PrefetchScalarGridSpec(
            num_scalar_prefetch=1, grid=(4, nt), in_specs=[gspec, lspec], out_specs=[p0spec, pbspec]),
        out_shape=[jax.ShapeDtypeStruct(shard_shape, F32), jax.ShapeDtypeStruct((3, *shard_shape), BF16)],
        compiler_params=_params(("arbitrary", "arbitrary")), name=name)(pos, grad, land)


def _chip_exchange(parts):
    na = len(parts)

    def body(*refs):
        ins, outs = refs[:na], refs[na:2 * na]
        send_sems, recv_sems = refs[2 * na:]
        x, y, c = _coords()
        chips = [(1 - x, y), (x, 1 - y), (1 - x, 1 - y)]
        cps = []
        for a in range(na):
            for kk, chip in enumerate(chips):
                cp = pltpu.make_async_remote_copy(
                    src_ref=ins[a].at[kk], dst_ref=outs[a].at[kk],
                    send_sem=send_sems.at[a, kk], recv_sem=recv_sems.at[a, kk],
                    device_id=(*chip, c), device_id_type=MESH)
                cp.start()
                cps.append(cp)
        for cp in cps:
            cp.wait()

    return pl.pallas_call(
        body, out_shape=[jax.ShapeDtypeStruct(p.shape, p.dtype) for p in parts],
        in_specs=[ANY] * na, out_specs=[ANY] * na,
        scratch_shapes=[pltpu.SemaphoreType.DMA((na, 3)), pltpu.SemaphoreType.DMA((na, 3))],
        name="rs_chip_exchange")(*parts)


def _adamw(name, parts, w, m, v):
    rows, cols = w.shape
    tr = _tile(rows, 256, 8)
    npart = len(parts)
    c1 = 1.0 - ADAM_B1 ** ADAM_STEP
    c2 = 1.0 - ADAM_B2 ** ADAM_STEP

    def body(*refs):
        p_refs = refs[:npart]
        w_ref, m_ref, v_ref, g_ref, d_ref, nm_ref, nv_ref = refs[npart:]
        g = None
        for r in p_refs:
            if len(r.shape) == 3:
                for i in range(r.shape[0]):
                    t = r[i].astype(F32)
                    g = t if g is None else g + t
            else:
                t = r[...].astype(F32)
                g = t if g is None else g + t
        nm = ADAM_B1 * m_ref[...] + (1.0 - ADAM_B1) * g
        nv = ADAM_B2 * v_ref[...] + (1.0 - ADAM_B2) * (g * g)
        g_ref[...] = g
        nm_ref[...] = nm
        nv_ref[...] = nv
        d_ref[...] = -ADAM_LR * ((nm / c1) / (jnp.sqrt(nv / c2) + ADAM_EPS) + ADAM_WD * w_ref[...])

    spec = pl.BlockSpec((tr, cols), lambda i: (i, 0))
    pspecs = [pl.BlockSpec((p.shape[0], tr, cols), lambda i: (0, i, 0)) if p.ndim == 3 else spec
              for p in parts]
    return pl.pallas_call(
        body, grid=(rows // tr,), in_specs=pspecs + [spec] * 3, out_specs=[spec] * 4,
        out_shape=[jax.ShapeDtypeStruct((rows, cols), F32)] * 4,
        compiler_params=_params(("parallel",)), name=name)(*parts, w, m, v)


SMALL_NAMES = ("s5_lam_re", "s5_lam_im", "s5_log_dt", "s5_b_re", "s5_b_im", "s5_c_re", "s5_c_im",
               "ffn_norm", "b_norm_mix", "kv_norm", "final_norm")
SMALL_PAD = 1024


def _pack(parts):
    flat = []
    for p in parts:
        f = p.reshape(-1)
        pad = (-f.shape[0]) % SMALL_PAD
        if pad:
            f = jnp.concatenate([f, jnp.zeros((pad,), f.dtype)])
        flat.append(f)
    return jnp.concatenate(flat).reshape(-1, 128)


def _unpack(packed, shapes):
    flat = packed.reshape(-1)
    out, off = [], 0
    for shp in shapes:
        size = math.prod(shp)
        out.append(flat[off:off + size].reshape(shp))
        off += size + (-size) % SMALL_PAD
    return out


def kernel(x, s5_lam_re, s5_lam_im, s5_log_dt, s5_b_re, s5_b_im, s5_c_re, s5_c_im, s5_d, s5_w_glu, a_norm_mix, ffn_norm, ffn_w_in, ffn_w_out, b_norm_mix, attn_w_q, attn_w_o, kv_norm, w_kv, final_norm, loss_target, m_s5_lam_re, m_s5_lam_im, m_s5_log_dt, m_s5_b_re, m_s5_b_im, m_s5_c_re, m_s5_c_im, m_s5_d, m_s5_w_glu, m_a_norm_mix, m_ffn_norm, m_ffn_w_in, m_ffn_w_out, m_b_norm_mix, m_attn_w_q, m_attn_w_o, m_kv_norm, m_w_kv, m_final_norm, v_s5_lam_re, v_s5_lam_im, v_s5_log_dt, v_s5_b_re, v_s5_b_im, v_s5_c_re, v_s5_c_im, v_s5_d, v_s5_w_glu, v_a_norm_mix, v_ffn_norm, v_ffn_w_in, v_ffn_w_out, v_b_norm_mix, v_attn_w_q, v_attn_w_o, v_kv_norm, v_w_kv, v_final_norm):
    args = dict(locals())
    T, D = x.shape[1], x.shape[2]
    n_layers = ffn_w_in.shape[0]
    xi_, yi_, ci_ = _coords()
    pos = jnp.stack([xi_, yi_, ci_]).astype(jnp.int32)
    me = 4 * xi_ + 2 * yi_ + ci_

    big_names = ["glu"] + [f"win{l}" for l in range(n_layers)] + [f"wout{l}" for l in range(n_layers)] \
        + ["wkv", "wq", "wo"]
    big_shards = [s5_w_glu[0]] + [ffn_w_in[l] for l in range(n_layers)] \
        + [ffn_w_out[l] for l in range(n_layers)] + [w_kv, attn_w_q[0], attn_w_o[0]]
    big_axes = [1] + [1] * n_layers + [0] * n_layers + [0, 1, 0]
    vec_shard = jnp.concatenate([s5_d, a_norm_mix], axis=0)
    gathered = _all_gather("weights_all_gather", [vec_shard] + [s.astype(BF16) for s in big_shards],
                           [1] + big_axes)
    vecs, full = gathered[0], dict(zip(big_names, gathered[1:]))

    G, P, C = s5_b_re.shape[1:]
    w = dict(
        a_norm=vecs[1:2], s5_d=vecs[0:1], glu=full["glu"],
        win=[full[f"win{l}"] for l in range(n_layers)], wout=[full[f"wout{l}"] for l in range(n_layers)],
        wkv=full["wkv"], wq=full["wq"], wo=full["wo"],
        ffn_norm=ffn_norm, b_norm=b_norm_mix, kv_norm=kv_norm.reshape(1, D), final_norm=final_norm.reshape(1, D),
        lam_re=s5_lam_re[0], lam_im=s5_lam_im[0], log_dt=s5_log_dt.reshape(G, 1),
        bt_re=s5_b_re[0].transpose(2, 0, 1), bt_im=s5_b_im[0].transpose(2, 0, 1),
        c_re=s5_c_re[0], c_im=s5_c_im[0],
    )
    loss_blk, grad_x, g = _local_step(x[0], loss_target[0], w)
    loss = lax.psum(loss_blk[0, 0], ("x", "y", "c"))

    big_grads = [g["glu"]] + g["win"] + g["wout"] + [g["wkv"], g["wq"], g["wo"]]
    lands = _pair_exchange(big_grads, big_axes)
    p0s, pbs = [], []
    for name, gr, land, ax in zip(big_names, big_grads, lands, big_axes):
        p0, pb = _pair_sum(f"rs_pair_sum_{name}", gr, land, ax, pos)
        p0s.append(p0)
        pbs.append(pb)
    recvd = _chip_exchange(pbs)

    out = {}

    def put(name, res, shape):
        for kind, r in zip(("grad", "delta", "new_m", "new_v"), res):
            out[f"{kind}_{name}"] = r.reshape(shape)

    big_out_names = ["s5_w_glu"] + ["ffn_w_in"] * n_layers + ["ffn_w_out"] * n_layers \
        + ["w_kv", "attn_w_q", "attn_w_o"]
    layered = {}
    for name, oname, shard, p0, rc in zip(big_names, big_out_names, big_shards, p0s, recvd):
        if oname in ("ffn_w_in", "ffn_w_out"):
            layer = int(name[-1])
            res = _adamw(f"adamw_{name}", [p0, rc], shard, args["m_" + oname][layer], args["v_" + oname][layer])
            layered.setdefault(oname, []).append(res)
        else:
            mm, vv = args["m_" + oname], args["v_" + oname]
            res = _adamw(f"adamw_{name}", [p0, rc], shard, mm.reshape(shard.shape), vv.reshape(shard.shape))
            put(oname, res, args[oname].shape)
    for oname, per_layer in layered.items():
        put(oname, [jnp.stack([r[i] for r in per_layer]) for i in range(4)], args[oname].shape)

    small_g = dict(
        s5_lam_re=g["lam_re"], s5_lam_im=g["lam_im"], s5_log_dt=g["log_dt"],
        s5_b_re=g["bt_re"].transpose(1, 2, 0), s5_b_im=g["bt_im"].transpose(1, 2, 0),
        s5_c_re=g["c_re"], s5_c_im=g["c_im"], ffn_norm=g["ffn_norm"], b_norm_mix=g["b_norm"],
        kv_norm=g["kv_norm"], final_norm=g["final_norm"])
    packed = _pack([small_g[n] for n in SMALL_NAMES] + [g["s5_d"], g["a_norm"]])
    rows = packed.shape[0]
    (all_parts,) = _all_gather("small_grads_all_gather", [packed], [0])
    all_parts = all_parts.reshape(N_DEV, rows, 128)
    n_rep_rows = _pack([small_g[n] for n in SMALL_NAMES]).shape[0]
    w_pack = _pack([args[n] for n in SMALL_NAMES])
    m_pack = _pack([args["m_" + n] for n in SMALL_NAMES])
    v_pack = _pack([args["v_" + n] for n in SMALL_NAMES])
    res = _adamw("adamw_small", [all_parts[:, :n_rep_rows]], w_pack, m_pack, v_pack)
    shapes = [args[n].shape for n in SMALL_NAMES]
    unpacked = [_unpack(r, shapes) for r in res]
    for i, n in enumerate(SMALL_NAMES):
        put(n, [u[i] for u in unpacked], args[n].shape)
    ws = D // N_DEV
    tail = all_parts[:, n_rep_rows:].reshape(N_DEV, 2, D)
    tail = lax.dynamic_slice_in_dim(tail, me * ws, ws, axis=2)
    res = _adamw("adamw_vec", [tail], vec_shard,
                 jnp.concatenate([m_s5_d, m_a_norm_mix], axis=0), jnp.concatenate([v_s5_d, v_a_norm_mix], axis=0))
    put("s5_d", [r[0:1] for r in res], s5_d.shape)
    put("a_norm_mix", [r[1:2] for r in res], a_norm_mix.shape)

    names = ("s5_lam_re", "s5_lam_im", "s5_log_dt", "s5_b_re", "s5_b_im", "s5_c_re", "s5_c_im", "s5_d",
             "s5_w_glu", "a_norm_mix", "ffn_norm", "ffn_w_in", "ffn_w_out", "b_norm_mix", "attn_w_q",
             "attn_w_o", "kv_norm", "w_kv", "final_norm")
    result = [loss, grad_x.reshape(x.shape)]
    for kind in ("grad", "delta", "new_m", "new_v"):
        result += [out[f"{kind}_{n}"] for n in names]
    return tuple(result)
```

```python
import functools
import math

import jax
import jax.numpy as jnp
from jax import lax
from jax.experimental import pallas as pl
from jax.experimental.pallas import tpu as pltpu

F32 = jnp.float32
BF16 = jnp.bfloat16

EPS = 1e-6
NEG_INF = -1e30
HEAD_DIM = 128
N_KV_HEADS = 4
DILATIONS = (1, 4, 16)
ATT_BLK = 128
S5_C = 16
S5_P = 64
S5_GB = 16
S5_CH = S5_GB * S5_C
S5_W = S5_GB * S5_P
S5_UNROLL = 4
N_DEV = 8

ADAM_LR = 0.001
ADAM_B1 = 0.9
ADAM_B2 = 0.999
ADAM_EPS = 1e-08
ADAM_WD = 0.01
ADAM_STEP = 10

VMEM_LIMIT_BYTES = 56 * 1024 * 1024
MESH = pl.DeviceIdType.MESH
ANY = pl.BlockSpec(memory_space=pl.ANY)


def _tile(n, pref, align=128):
    t = (min(pref, n) // align) * align
    while t >= align:
        if n % t == 0:
            return t
        t -= align
    return n


def _params(sem):
    return pltpu.CompilerParams(dimension_semantics=sem, vmem_limit_bytes=VMEM_LIMIT_BYTES)


def _sigmoid(x):
    return 1.0 / (1.0 + jnp.exp(-x))


NN = (((1,), (0,)), ((), ()))
NT = (((1,), (1,)), ((), ()))
TN = (((0,), (0,)), ((), ()))


def _dot(a, b, dims=NN):
    return lax.dot_general(a, b, dims, preferred_element_type=F32)


def _matmul(name, grid, ins, in_specs, products, dims, out_shapes, out_specs, acc_shapes, epilogue):
    n_in, n_out, nk = len(ins), len(out_shapes), grid[2]

    def body(*refs):
        in_refs = refs[:n_in]
        out_refs = refs[n_in:n_in + n_out]
        acc_refs = refs[n_in + n_out:]

        def prods():
            vals = [None] * len(acc_shapes)
            for ai, bi, ci in products:
                d = _dot(in_refs[ai][...], in_refs[bi][...], dims)
                vals[ci] = d if vals[ci] is None else vals[ci] + d
            return vals

        if nk == 1:
            epilogue(in_refs, out_refs, prods())
        else:
            k = pl.program_id(2)

            @pl.when(k == 0)
            def _():
                for a in acc_refs:
                    a[...] = jnp.zeros_like(a)

            for a, v in zip(acc_refs, prods()):
                a[...] += v

            @pl.when(k == nk - 1)
            def _():
                epilogue(in_refs, out_refs, [a[...] for a in acc_refs])

    scratch = [] if nk == 1 else [pltpu.VMEM(s, F32) for s in acc_shapes]
    return pl.pallas_call(
        body, grid=grid, in_specs=in_specs, out_specs=out_specs, out_shape=out_shapes,
        scratch_shapes=scratch, compiler_params=_params(("parallel", "parallel", "arbitrary")),
        name=name)(*ins)


def _mm_dual_fwd(name, a, w, res, kind):
    T, K = a.shape
    N = w.shape[1] // 2
    tm, tn = _tile(T, 512), _tile(N, 512)
    nj = N // tn
    grid = (T // tm, nj, 1)
    ins = [a, w, w]
    in_specs = [pl.BlockSpec((tm, K), lambda i, j, k: (i, 0)),
                pl.BlockSpec((K, tn), lambda i, j, k: (0, j)),
                pl.BlockSpec((K, tn), lambda i, j, k: (0, j + nj))]
    pair_spec = pl.BlockSpec((2, tm, tn), lambda i, j, k: (0, i, j))
    tile_spec = pl.BlockSpec((tm, tn), lambda i, j, k: (i, j))
    if kind == "glu":
        ins.append(res)
        in_specs.append(tile_spec)

        def epilogue(in_refs, out_refs, accs):
            val, gate = accs
            out_refs[0][...] = in_refs[3][...] + val * _sigmoid(gate)
            out_refs[1][0] = val.astype(BF16)
            out_refs[1][1] = gate.astype(BF16)

        out_shapes = [jax.ShapeDtypeStruct((T, N), F32), jax.ShapeDtypeStruct((2, T, N), BF16)]
        out_specs = [tile_spec, pair_spec]
    else:
        def epilogue(in_refs, out_refs, accs):
            g, u = accs
            out_refs[0][0] = g.astype(BF16)
            out_refs[0][1] = u.astype(BF16)
            out_refs[1][...] = (g * _sigmoid(g) * u).astype(BF16)

        out_shapes = [jax.ShapeDtypeStruct((2, T, N), BF16), jax.ShapeDtypeStruct((T, N), BF16)]
        out_specs = [pair_spec, tile_spec]
    return _matmul(name, grid, ins, in_specs, [(0, 1, 0), (0, 2, 1)], NN, out_shapes, out_specs,
                   [(tm, tn), (tm, tn)], epilogue)


def _mm_kv(name, a, w):
    T, K = a.shape
    N = w.shape[1] // 2
    tm, tn = _tile(T, 512), _tile(N, 512)
    nj = N // tn
    tile_spec = pl.BlockSpec((tm, tn), lambda i, j, k: (i, j))

    def epilogue(in_refs, out_refs, accs):
        out_refs[0][...] = accs[0].astype(BF16)
        out_refs[1][...] = accs[1].astype(BF16)

    return _matmul(name, (T // tm, nj, 1), [a, w, w],
                   [pl.BlockSpec((tm, K), lambda i, j, k: (i, 0)),
                    pl.BlockSpec((K, tn), lambda i, j, k: (0, j)),
                    pl.BlockSpec((K, tn), lambda i, j, k: (0, j + nj))],
                   [(0, 1, 0), (0, 2, 1)], NN,
                   [jax.ShapeDtypeStruct((T, N), BF16)] * 2, [tile_spec, tile_spec],
                   [(tm, tn), (tm, tn)], epilogue)


def _mm_nn(name, a, w, res=None, out_dtype=F32):
    T, K = a.shape
    N = w.shape[1]
    tm, tn, tk = _tile(T, 1024), _tile(N, 1024), _tile(K, 2048 if K <= 2048 else 512)
    grid = (T // tm, N // tn, K // tk)
    tile_spec = pl.BlockSpec((tm, tn), lambda i, j, k: (i, j))
    ins = [a, w]
    in_specs = [pl.BlockSpec((tm, tk), lambda i, j, k: (i, k)),
                pl.BlockSpec((tk, tn), lambda i, j, k: (k, j))]
    if res is not None:
        ins.append(res)
        in_specs.append(tile_spec)

    def epilogue(in_refs, out_refs, accs):
        v = accs[0]
        if res is not None:
            v = v + in_refs[2][...]
        out_refs[0][...] = v.astype(out_dtype)

    return _matmul(name, grid, ins, in_specs, [(0, 1, 0)], NN,
                   [jax.ShapeDtypeStruct((T, N), out_dtype)], [tile_spec], [(tm, tn)], epilogue)[0]


def _mm_nt(name, a_list, w, out_dtype=F32):
    T, Np = a_list[0].shape
    Ko = w.shape[0]
    n_parts = len(a_list)
    tm, tn, tk = _tile(T, 1024), _tile(Ko, 1024), _tile(Np, 512)
    nkp = Np // tk
    grid = (T // tm, Ko // tn, nkp)
    ins = list(a_list) + [w] * n_parts
    in_specs = [pl.BlockSpec((tm, tk), lambda i, j, k: (i, k)) for _ in a_list]
    in_specs += [pl.BlockSpec((tn, tk), functools.partial(lambda i, j, k, p: (j, p * nkp + k), p=p))
                 for p in range(n_parts)]
    products = [(p, n_parts + p, 0) for p in range(n_parts)]

    def epilogue(in_refs, out_refs, accs):
        out_refs[0][...] = accs[0].astype(out_dtype)

    return _matmul(name, grid, ins, in_specs, products, NT,
                   [jax.ShapeDtypeStruct((T, Ko), out_dtype)],
                   [pl.BlockSpec((tm, tn), lambda i, j, k: (i, j))], [(tm, tn)], epilogue)[0]


def _mm_nt_pair(name, a3, w):
    _, T, N = a3.shape
    Ko = w.shape[0]
    tm, tn, tk = _tile(T, 1024), _tile(Ko, 1024), _tile(N, 512)
    nkh = N // tk
    grid = (T // tm, Ko // tn, 2 * nkh)

    def epilogue(in_refs, out_refs, accs):
        out_refs[0][...] = accs[0]

    return _matmul(name, grid, [a3, w],
                   [pl.BlockSpec((None, tm, tk), lambda i, j, k: (k // nkh, i, k % nkh)),
                    pl.BlockSpec((tn, tk), lambda i, j, k: (j, k))],
                   [(0, 1, 0)], NT, [jax.ShapeDtypeStruct((T, Ko), F32)],
                   [pl.BlockSpec((tm, tn), lambda i, j, k: (i, j))], [(tm, tn)], epilogue)[0]


def _mm_nt_ffn_bwd(name, dx, w_out, gu):
    T, D = dx.shape
    Fh = w_out.shape[0]
    tm, tn = _tile(T, 512), _tile(Fh, 512)
    pair_spec = pl.BlockSpec((2, tm, tn), lambda i, j, k: (0, i, j))

    def epilogue(in_refs, out_refs, accs):
        da = accs[0]
        g = in_refs[2][0].astype(F32)
        u = in_refs[2][1].astype(F32)
        s = _sigmoid(g)
        out_refs[0][0] = (da * u * (s * (1.0 + g * (1.0 - s)))).astype(BF16)
        out_refs[0][1] = (da * (g * s)).astype(BF16)

    return _matmul(name, (T // tm, Fh // tn, 1), [dx, w_out, gu],
                   [pl.BlockSpec((tm, D), lambda i, j, k: (i, 0)),
                    pl.BlockSpec((tn, D), lambda i, j, k: (j, 0)),
                    pair_spec],
                   [(0, 1, 0)], NT, [jax.ShapeDtypeStruct((2, T, Fh), BF16)], [pair_spec],
                   [(tm, tn)], epilogue)[0]


def _mm_tn(name, a, d):
    T, Ko = a.shape
    N = d.shape[1]
    to, tn, tk = _tile(Ko, 1024), _tile(N, 1024), _tile(T, 1024)
    grid = (Ko // to, N // tn, T // tk)

    def epilogue(in_refs, out_refs, accs):
        out_refs[0][...] = accs[0]

    return _matmul(name, grid, [a, d],
                   [pl.BlockSpec((tk, to), lambda i, j, k: (k, i)),
                    pl.BlockSpec((tk, tn), lambda i, j, k: (k, j))],
                   [(0, 1, 0)], TN, [jax.ShapeDtypeStruct((Ko, N), F32)],
                   [pl.BlockSpec((to, tn), lambda i, j, k: (i, j))], [(to, tn)], epilogue)[0]


def _mm_tn_pair(name, a, d3):
    T, Ko = a.shape
    N = d3.shape[2]
    to, tn, tk = _tile(Ko, 2048), _tile(N, 512), _tile(T, 1024)
    njh = N // tn
    grid = (Ko // to, 2 * njh, T // tk)

    def epilogue(in_refs, out_refs, accs):
        out_refs[0][...] = accs[0]

    return _matmul(name, grid, [a, d3],
                   [pl.BlockSpec((tk, to), lambda i, j, k: (k, i)),
                    pl.BlockSpec((None, tk, tn), lambda i, j, k: (j // njh, k, j % njh))],
                   [(0, 1, 0)], TN, [jax.ShapeDtypeStruct((Ko, 2 * N), F32)],
                   [pl.BlockSpec((to, tn), lambda i, j, k: (i, j))], [(to, tn)], epilogue)[0]


def _rms_fwd(name, x, gains, dtypes):
    T, D = x.shape
    n = len(gains)
    tr = _tile(T, 512, 8)

    def body(x_ref, *refs):
        xv = x_ref[...]
        xr = xv * lax.rsqrt(jnp.mean(xv * xv, axis=-1, keepdims=True) + EPS)
        for g_ref, o_ref in zip(refs[:n], refs[n:]):
            o_ref[...] = (xr * g_ref[...]).astype(o_ref.dtype)

    row = pl.BlockSpec((tr, D), lambda i: (i, 0))
    vec = pl.BlockSpec((1, D), lambda i: (0, 0))
    return pl.pallas_call(
        body, grid=(T // tr,), in_specs=[row] + [vec] * n, out_specs=[row] * n,
        out_shape=[jax.ShapeDtypeStruct((T, D), dt) for dt in dtypes],
        compiler_params=_params(("parallel",)), name=name)(x, *gains)


def _rms_bwd(name, x, dres, gains, dhs):
    T, D = x.shape
    n = len(gains)
    tr = _tile(T, 256, 8)

    def body(x_ref, dres_ref, *refs):
        g_refs, dh_refs = refs[:n], refs[n:2 * n]
        dx_ref, dxb_ref = refs[2 * n], refs[2 * n + 1]
        dg_refs = refs[2 * n + 2:]
        xv = x_ref[...]
        r = lax.rsqrt(jnp.mean(xv * xv, axis=-1, keepdims=True) + EPS)
        xr = xv * r
        w = None
        for g_ref, dh_ref, dg_ref in zip(g_refs, dh_refs, dg_refs):
            dh = dh_ref[...].astype(F32)

            @pl.when(pl.program_id(0) == 0)
            def _():
                dg_ref[...] = jnp.zeros_like(dg_ref)

            dg_ref[...] += jnp.sum(dh * xr, axis=0, keepdims=True)
            wi = dh * g_ref[...]
            w = wi if w is None else w + wi
        dx = dres_ref[...] + r * (w - xr * jnp.mean(w * xr, axis=-1, keepdims=True))
        dx_ref[...] = dx
        dxb_ref[...] = dx.astype(BF16)

    row = pl.BlockSpec((tr, D), lambda i: (i, 0))
    vec = pl.BlockSpec((1, D), lambda i: (0, 0))
    outs = pl.pallas_call(
        body, grid=(T // tr,), in_specs=[row, row] + [vec] * n + [row] * n,
        out_specs=[row, row] + [vec] * n,
        out_shape=[jax.ShapeDtypeStruct((T, D), F32), jax.ShapeDtypeStruct((T, D), BF16)]
        + [jax.ShapeDtypeStruct((1, D), F32)] * n,
        compiler_params=_params(("arbitrary",)), name=name)(x, dres, *gains, *dhs)
    return outs[0], outs[1], outs[2:]


def _loss_head(x, gain, target):
    T, D = x.shape
    tr = _tile(T, 256, 8)

    def body(x_ref, g_ref, t_ref, loss_ref, dx_ref, dxb_ref, dg_ref):
        @pl.when(pl.program_id(0) == 0)
        def _():
            loss_ref[...] = jnp.zeros_like(loss_ref)
            dg_ref[...] = jnp.zeros_like(dg_ref)

        xv = x_ref[...]
        r = lax.rsqrt(jnp.mean(xv * xv, axis=-1, keepdims=True) + EPS)
        xr = xv * r
        err = xr * g_ref[...] - t_ref[...]
        part = jnp.sum(jnp.sum(err * err, axis=-1, keepdims=True), axis=0, keepdims=True) * (0.5 / D)
        loss_ref[...] += jnp.broadcast_to(part, loss_ref.shape)
        dy = err * (1.0 / D)
        dg_ref[...] += jnp.sum(dy * xr, axis=0, keepdims=True)
        w = dy * g_ref[...]
        dx = r * (w - xr * jnp.mean(w * xr, axis=-1, keepdims=True))
        dx_ref[...] = dx
        dxb_ref[...] = dx.astype(BF16)

    row = pl.BlockSpec((tr, D), lambda i: (i, 0))
    vec = pl.BlockSpec((1, D), lambda i: (0, 0))
    return pl.pallas_call(
        body, grid=(T // tr,), in_specs=[row, vec, row],
        out_specs=[pl.BlockSpec((8, 128), lambda i: (0, 0)), row, row, vec],
        out_shape=[jax.ShapeDtypeStruct((8, 128), F32), jax.ShapeDtypeStruct((T, D), F32),
                   jax.ShapeDtypeStruct((T, D), BF16), jax.ShapeDtypeStruct((1, D), F32)],
        compiler_params=_params(("arbitrary",)), name="loss_head")(x, gain, target)


def _glu_bwd(dmix, vg):
    T, N = dmix.shape
    tr, tc = _tile(T, 512, 8), _tile(N, 1024)

    def body(d_ref, vg_ref, o_ref):
        d = d_ref[...]
        val = vg_ref[0].astype(F32)
        s = _sigmoid(vg_ref[1].astype(F32))
        o_ref[0] = (d * s).astype(BF16)
        o_ref[1] = (d * val * s * (1.0 - s)).astype(BF16)

    pair = pl.BlockSpec((2, tr, tc), lambda i, j: (0, i, j))
    return pl.pallas_call(
        body, grid=(T // tr, N // tc), in_specs=[pl.BlockSpec((tr, tc), lambda i, j: (i, j)), pair],
        out_specs=pair, out_shape=jax.ShapeDtypeStruct((2, T, N), BF16),
        compiler_params=_params(("parallel", "parallel")), name="glu_bwd")(dmix, vg)


def _to_state_tiles(x_ref, s, val):
    tc = val.shape[0]
    for j in range(S5_W // 128):
        x_ref[s, pl.ds(j, tc, stride=8), :] = val[:, 128 * j:128 * (j + 1)]


def _from_state_tiles(x_ref, s, tc):
    return jnp.concatenate([x_ref[s, pl.ds(j, tc, stride=8), :] for j in range(S5_W // 128)], axis=1)


def _s5_scan_fwd(xr_ref, xi_ref, ar_ref, ai_ref, cr_ref, ci_ref, tc, nblk):
    a = [(ar_ref[s], ai_ref[s]) for s in range(nblk)]

    def step(i, carry):
        carry = list(carry)
        for uu in range(S5_UNROLL):
            r0 = pl.multiple_of((i * S5_UNROLL + uu) * 8, 8)
            for s in range(nblk):
                cr, ci = carry[2 * s], carry[2 * s + 1]
                a_r, a_i = a[s]
                xr = a_r * cr - a_i * ci + xr_ref[s, pl.ds(r0, 8), :]
                xi = a_r * ci + a_i * cr + xi_ref[s, pl.ds(r0, 8), :]
                xr_ref[s, pl.ds(r0, 8), :] = xr
                xi_ref[s, pl.ds(r0, 8), :] = xi
                carry[2 * s], carry[2 * s + 1] = xr, xi
        return tuple(carry)

    init = []
    for s in range(nblk):
        init += [cr_ref[s], ci_ref[s]]
    out = lax.fori_loop(0, tc // S5_UNROLL, step, tuple(init))
    for s in range(nblk):
        cr_ref[s] = out[2 * s]
        ci_ref[s] = out[2 * s + 1]


def _s5_scan_bwd(lr_ref, li_ref, xr_ref, xi_ref, h_ref, ar_ref, ai_ref, cr_ref, ci_ref,
                 accr_ref, acci_ref, tc, nblk):
    a = [(ar_ref[s], ai_ref[s]) for s in range(nblk)]

    def one(s, r0, prev_r, prev_i, st):
        c_r, c_i, d_r, d_i = st
        a_r, a_i = a[s]
        l_r = lr_ref[s, pl.ds(r0, 8), :] + a_r * c_r + a_i * c_i
        l_i = li_ref[s, pl.ds(r0, 8), :] + a_r * c_i - a_i * c_r
        lr_ref[s, pl.ds(r0, 8), :] = l_r
        li_ref[s, pl.ds(r0, 8), :] = l_i
        return [l_r, l_i, d_r + l_r * prev_r + l_i * prev_i, d_i - l_r * prev_i + l_i * prev_r]

    def step(i, carry):
        carry = list(carry)
        for uu in range(S5_UNROLL):
            t = tc - 1 - (i * S5_UNROLL + uu)
            r0 = pl.multiple_of(t * 8, 8)
            p0 = pl.multiple_of((t - 1) * 8, 8)
            for s in range(nblk):
                carry[4 * s:4 * s + 4] = one(s, r0, xr_ref[s, pl.ds(p0, 8), :], xi_ref[s, pl.ds(p0, 8), :],
                                             carry[4 * s:4 * s + 4])
        return tuple(carry)

    init = []
    for s in range(nblk):
        init += [cr_ref[s], ci_ref[s], accr_ref[s], acci_ref[s]]
    carry = list(lax.fori_loop(0, tc // S5_UNROLL - 1, step, tuple(init)))
    for t in range(S5_UNROLL - 1, -1, -1):
        for s in range(nblk):
            if t > 0:
                prev_r, prev_i = xr_ref[s, 8 * (t - 1):8 * t, :], xi_ref[s, 8 * (t - 1):8 * t, :]
            else:
                prev_r, prev_i = h_ref[0, s], h_ref[1, s]
            carry[4 * s:4 * s + 4] = one(s, 8 * t, prev_r, prev_i, carry[4 * s:4 * s + 4])
    for s in range(nblk):
        cr_ref[s], ci_ref[s], accr_ref[s], acci_ref[s] = carry[4 * s:4 * s + 4]


def _gelu(y):
    k = math.sqrt(2.0 / math.pi)
    return 0.5 * y * (1.0 + jnp.tanh(k * (y + 0.044715 * (y * y * y))))


def _gelu_grad(y):
    k = math.sqrt(2.0 / math.pi)
    t = jnp.tanh(k * (y + 0.044715 * (y * y * y)))
    return 0.5 * (1.0 + t) + 0.5 * y * (1.0 - t * t) * (k * (1.0 + 3.0 * 0.044715 * (y * y)))


def _s5_specs(tc, nch, sbk, rev):
    def ch(c):
        return nch - 1 - c if rev else c

    return dict(
        act=pl.BlockSpec((tc, sbk * S5_CH), lambda i, c: (ch(c), i)),
        bb=pl.BlockSpec((sbk, S5_CH, S5_W), lambda i, c: (i, 0, 0)),
        cc=pl.BlockSpec((sbk, S5_W, S5_CH), lambda i, c: (i, 0, 0)),
        a=pl.BlockSpec((sbk, 8, 128), lambda i, c: (i, 0, 0)),
        d=pl.BlockSpec((1, sbk * S5_CH), lambda i, c: (0, i)),
        h=pl.BlockSpec((None, 2, sbk, 8, 128), lambda i, c: (ch(c), 0, i, 0, 0)),
    )


def _s5_blocks(nb, pref):
    return max(b for b in range(1, pref + 1) if nb % b == 0)


def _s5_fwd(u, bbr, bbi, ccr, cci, a_re, a_im, d_skip):
    T, D = u.shape
    nb = D // S5_CH
    sbk = _s5_blocks(nb, 4)
    tc = _tile(T, 512, 8)
    nch = T // tc
    sp = _s5_specs(tc, nch, sbk, False)

    def body(u_ref, bbr_ref, bbi_ref, ccr_ref, cci_ref, ar_ref, ai_ref, d_ref, z_ref, h_ref,
             xr, xi, cr, ci):
        @pl.when(pl.program_id(1) == 0)
        def _():
            cr[...] = jnp.zeros_like(cr)
            ci[...] = jnp.zeros_like(ci)

        h_ref[0] = cr[...]
        h_ref[1] = ci[...]
        for s in range(sbk):
            ub = u_ref[:, s * S5_CH:(s + 1) * S5_CH].astype(BF16)
            _to_state_tiles(xr, s, _dot(ub, bbr_ref[s]))
            _to_state_tiles(xi, s, _dot(ub, bbi_ref[s]))
        _s5_scan_fwd(xr, xi, ar_ref, ai_ref, cr, ci, tc, sbk)
        for s in range(sbk):
            cols = slice(s * S5_CH, (s + 1) * S5_CH)
            y = (_dot(_from_state_tiles(xr, s, tc).astype(BF16), ccr_ref[s])
                 - _dot(_from_state_tiles(xi, s, tc).astype(BF16), cci_ref[s])
                 + d_ref[:, cols] * u_ref[:, cols])
            z_ref[:, cols] = _gelu(y).astype(BF16)

    tiles = pltpu.VMEM((sbk, tc * 8, 128), F32)
    carry = pltpu.VMEM((sbk, 8, 128), F32)
    return pl.pallas_call(
        body, grid=(nb // sbk, nch),
        in_specs=[sp["act"], sp["bb"], sp["bb"], sp["cc"], sp["cc"], sp["a"], sp["a"], sp["d"]],
        out_specs=[sp["act"], sp["h"]],
        out_shape=[jax.ShapeDtypeStruct((T, D), BF16), jax.ShapeDtypeStruct((nch, 2, nb, 8, 128), F32)],
        scratch_shapes=[tiles, tiles, carry, carry],
        compiler_params=_params(("parallel", "arbitrary")), name="s5_fwd",
    )(u, bbr, bbi, ccr, cci, a_re, a_im, d_skip)


def _s5_bwd(u, dz, h0, bbr, bbi, ccr, cci, a_re, a_im, d_skip):
    T, D = u.shape
    nb = D // S5_CH
    sbk = _s5_blocks(nb, 2)
    tc = _tile(T, 512, 8)
    nch = T // tc
    sp = _s5_specs(tc, nch, sbk, True)

    def body(u_ref, dz_ref, h_ref, bbr_ref, bbi_ref, ccr_ref, cci_ref, ar_ref, ai_ref, d_ref,
             du_ref, dd_ref, dar_ref, dai_ref, dbbr_ref, dbbi_ref, dccr_ref, dcci_ref,
             xr, xi, lr, li, fr, fi, br, bi, accr, acci):
        c = pl.program_id(1)

        @pl.when(c == 0)
        def _():
            for ref in (br, bi, accr, acci, dd_ref, dbbr_ref, dbbi_ref, dccr_ref, dcci_ref):
                ref[...] = jnp.zeros_like(ref)

        for s in range(sbk):
            ub = u_ref[:, s * S5_CH:(s + 1) * S5_CH].astype(BF16)
            _to_state_tiles(xr, s, _dot(ub, bbr_ref[s]))
            _to_state_tiles(xi, s, _dot(ub, bbi_ref[s]))
        fr[...] = h_ref[0]
        fi[...] = h_ref[1]
        _s5_scan_fwd(xr, xi, ar_ref, ai_ref, fr, fi, tc, sbk)
        for s in range(sbk):
            cols = slice(s * S5_CH, (s + 1) * S5_CH)
            uv = u_ref[:, cols]
            xrb = _from_state_tiles(xr, s, tc).astype(BF16)
            xib = _from_state_tiles(xi, s, tc).astype(BF16)
            dsk = d_ref[:, cols]
            y = _dot(xrb, ccr_ref[s]) - _dot(xib, cci_ref[s]) + dsk * uv
            dy = dz_ref[:, cols] * _gelu_grad(y)
            dd_ref[:, cols] += jnp.sum(dy * uv, axis=0, keepdims=True)
            dyb = dy.astype(BF16)
            dccr_ref[s] += _dot(xrb, dyb, TN)
            dcci_ref[s] += _dot(xib, dyb, TN)
            _to_state_tiles(lr, s, _dot(dyb, ccr_ref[s], NT))
            _to_state_tiles(li, s, -_dot(dyb, cci_ref[s], NT))
            du_ref[:, cols] = dy * dsk
        _s5_scan_bwd(lr, li, xr, xi, h_ref, ar_ref, ai_ref, br, bi, accr, acci, tc, sbk)
        for s in range(sbk):
            cols = slice(s * S5_CH, (s + 1) * S5_CH)
            ub = u_ref[:, cols].astype(BF16)
            lrb = _from_state_tiles(lr, s, tc).astype(BF16)
            lib = _from_state_tiles(li, s, tc).astype(BF16)
            dbbr_ref[s] += _dot(ub, lrb, TN)
            dbbi_ref[s] += _dot(ub, lib, TN)
            du_ref[:, cols] += _dot(lrb, bbr_ref[s], NT) + _dot(lib, bbi_ref[s], NT)

        @pl.when(c == nch - 1)
        def _():
            dar_ref[...] = accr[...]
            dai_ref[...] = acci[...]

    tiles = pltpu.VMEM((sbk, tc * 8, 128), F32)
    carry = pltpu.VMEM((sbk, 8, 128), F32)
    return pl.pallas_call(
        body, grid=(nb // sbk, nch),
        in_specs=[sp["act"], sp["act"], sp["h"], sp["bb"], sp["bb"], sp["cc"], sp["cc"],
                  sp["a"], sp["a"], sp["d"]],
        out_specs=[sp["act"], sp["d"], sp["a"], sp["a"], sp["bb"], sp["bb"], sp["cc"], sp["cc"]],
        out_shape=[jax.ShapeDtypeStruct((T, D), F32), jax.ShapeDtypeStruct((1, D), F32),
                   jax.ShapeDtypeStruct((nb, 8, 128), F32), jax.ShapeDtypeStruct((nb, 8, 128), F32),
                   jax.ShapeDtypeStruct((nb, S5_CH, S5_W), F32), jax.ShapeDtypeStruct((nb, S5_CH, S5_W), F32),
                   jax.ShapeDtypeStruct((nb, S5_W, S5_CH), F32), jax.ShapeDtypeStruct((nb, S5_W, S5_CH), F32)],
        scratch_shapes=[tiles, tiles, tiles, tiles, carry, carry, carry, carry, carry, carry],
        compiler_params=_params(("parallel", "arbitrary")), name="s5_bwd",
    )(u, dz, h0, bbr, bbi, ccr, cci, a_re, a_im, d_skip)


def _s5_disc(lr, li, ldt):
    dt = jnp.exp(ldt)
    mag = jnp.exp(lr * dt)
    ang = li * dt
    cs, sn = jnp.cos(ang), jnp.sin(ang)
    lbr, lbi = mag * cs, mag * sn
    nr = lbr - 1.0
    den = lr * lr + li * li
    f_re = (nr * lr + lbi * li) / den
    f_im = (lbi * lr - nr * li) / den
    return dt, mag, cs, sn, lbr, lbi, nr, den, f_re, f_im


def _s5_param_fwd(lr, li, ldt, bt_re, bt_im):
    c, g, p = bt_re.shape

    def body(lr_ref, li_ref, ldt_ref, br_ref, bi_ref, lbr_ref, lbi_ref, bbr_ref, bbi_ref):
        _, _, _, _, lbr, lbi, _, _, f_re, f_im = _s5_disc(lr_ref[...], li_ref[...], ldt_ref[...])
        lbr_ref[...] = lbr
        lbi_ref[...] = lbi
        for ch in range(c):
            b_r, b_i = br_ref[ch], bi_ref[ch]
            bbr_ref[ch] = f_re * b_r - f_im * b_i
            bbi_ref[ch] = f_re * b_i + f_im * b_r

    gp = jax.ShapeDtypeStruct((g, p), F32)
    cgp = jax.ShapeDtypeStruct((c, g, p), F32)
    return pl.pallas_call(body, out_shape=[gp, gp, cgp, cgp], name="s5_param_fwd")(lr, li, ldt, bt_re, bt_im)


def _s5_param_bwd(lr, li, ldt, bt_re, bt_im, dlbr, dlbi, dbbr, dbbi):
    c, g, p = bt_re.shape

    def body(lr_ref, li_ref, ldt_ref, br_ref, bi_ref, dlbr_ref, dlbi_ref, dbbr_ref, dbbi_ref,
             dlr_ref, dli_ref, dldt_ref, dbr_ref, dbi_ref):
        l_r, l_i = lr_ref[...], li_ref[...]
        dt, mag, cs, sn, lbr, lbi, nr, den, f_re, f_im = _s5_disc(l_r, l_i, ldt_ref[...])
        dfr = jnp.zeros_like(l_r)
        dfi = jnp.zeros_like(l_r)
        for ch in range(c):
            b_r, b_i = br_ref[ch], bi_ref[ch]
            g_r, g_i = dbbr_ref[ch], dbbi_ref[ch]
            dbr_ref[ch] = f_re * g_r + f_im * g_i
            dbi_ref[ch] = f_re * g_i - f_im * g_r
            dfr = dfr + g_r * b_r + g_i * b_i
            dfi = dfi + g_i * b_r - g_r * b_i
        inv = 1.0 / den
        d_nr = (dfr * l_r - dfi * l_i) * inv
        d_lbi = (dfr * l_i + dfi * l_r) * inv + dlbi_ref[...]
        d_lbr = d_nr + dlbr_ref[...]
        d_den = -(dfr * f_re + dfi * f_im) * inv
        d_mag = d_lbr * cs + d_lbi * sn
        d_ang = d_lbi * lbr - d_lbr * lbi
        dlr_ref[...] = (dfr * nr + dfi * lbi) * inv + 2.0 * d_den * l_r + d_mag * mag * dt
        dli_ref[...] = (dfr * lbi - dfi * nr) * inv + 2.0 * d_den * l_i + d_ang * dt
        dldt_ref[...] = jnp.sum(d_mag * mag * l_r + d_ang * l_i, axis=1, keepdims=True) * dt

    gp = jax.ShapeDtypeStruct((g, p), F32)
    cgp = jax.ShapeDtypeStruct((c, g, p), F32)
    return pl.pallas_call(body, out_shape=[gp, gp, jax.ShapeDtypeStruct((g, 1), F32), cgp, cgp],
                          name="s5_param_bwd")(lr, li, ldt, bt_re, bt_im, dlbr, dlbi, dbbr, dbbi)


def _block_diag(t, inner_first):
    g, c, p = t.shape
    nb = g // S5_GB
    t4 = t.reshape(nb, S5_GB, c, p)
    eye = jnp.eye(S5_GB, dtype=t.dtype)
    if inner_first:
        e = t4[:, :, :, None, :] * eye[None, :, None, :, None]
        return e.reshape(nb, S5_GB * c, S5_GB * p)
    t4 = t4.transpose(0, 1, 3, 2)
    e = t4[:, :, :, None, :] * eye[None, :, None, :, None]
    return e.reshape(nb, S5_GB * p, S5_GB * c)


def _block_diag_extract(e, inner_first, c, p):
    nb = e.shape[0]
    eye = jnp.eye(S5_GB, dtype=e.dtype)
    if inner_first:
        e5 = e.reshape(nb, S5_GB, c, S5_GB, p)
        return (e5 * eye[None, :, None, :, None]).sum(3).reshape(nb * S5_GB, c, p)
    e5 = e.reshape(nb, S5_GB, p, S5_GB, c)
    return (e5 * eye[None, :, None, :, None]).sum(3).transpose(0, 1, 3, 2).reshape(nb * S5_GB, c, p)


def _att_masks(rep, gb):
    rows = rep * ATT_BLK
    qi = lax.broadcasted_iota(jnp.int32, (rows, 2 * ATT_BLK), 0) % ATT_BLK
    si = lax.broadcasted_iota(jnp.int32, (rows, 2 * ATT_BLK), 1)
    prev = (si < ATT_BLK) & (si >= qi) & (gb > 0)
    cur = (si >= ATT_BLK) & (si - ATT_BLK <= qi)
    return prev | cur


def _stack_heads(ref, rows, rep):
    return jnp.concatenate([ref[rows, j * HEAD_DIM:(j + 1) * HEAD_DIM] for j in range(rep)], axis=0)


def _att_geometry(T, H, dil):
    n = T // dil
    rep = H // N_KV_HEADS
    rc = _tile(n, 1024, ATT_BLK)
    return n, rep, rc, n // rc, rc // ATT_BLK


def _att_fwd(q, k, v, grp, dil):
    T = q.shape[0]
    H = q.shape[1] // HEAD_DIM // len(DILATIONS)
    n, rep, rc, nrc, nblk = _att_geometry(T, H, dil)
    hw = rep * HEAD_DIM
    scale = HEAD_DIM ** -0.5
    q2 = q.reshape(n, dil * q.shape[1])
    k2 = k.reshape(n, dil * k.shape[1])
    v2 = v.reshape(n, dil * v.shape[1])
    qcols = q.shape[1] // hw
    ocols = H * HEAD_DIM // hw

    def body(q_ref, k_ref, v_ref, o_ref, l_ref):
        ch = pl.program_id(2)

        def blk(b, _):
            gb = ch * nblk + b
            rows = pl.ds(pl.multiple_of(b * ATT_BLK, ATT_BLK), ATT_BLK)
            kc = pl.ds(pl.multiple_of(gb * ATT_BLK, ATT_BLK), ATT_BLK)
            kp = pl.ds(pl.multiple_of(jnp.maximum(gb - 1, 0) * ATT_BLK, ATT_BLK), ATT_BLK)
            qs = _stack_heads(q_ref, rows, rep)
            kcat = jnp.concatenate([k_ref[kp, :], k_ref[kc, :]], axis=0)
            vcat = jnp.concatenate([v_ref[kp, :], v_ref[kc, :]], axis=0)
            s = jnp.where(_att_masks(rep, gb), _dot(qs, kcat, NT) * scale, NEG_INF)
            m = jnp.max(s, axis=-1, keepdims=True)
            p = jnp.exp(s - m)
            l = jnp.sum(p, axis=-1, keepdims=True)
            o = _dot(p.astype(BF16), vcat) / l
            lse = jnp.broadcast_to(m + jnp.log(l), (rep * ATT_BLK, HEAD_DIM))
            for j in range(rep):
                o_ref[rows, j * HEAD_DIM:(j + 1) * HEAD_DIM] = o[j * ATT_BLK:(j + 1) * ATT_BLK]
                l_ref[rows, j * HEAD_DIM:(j + 1) * HEAD_DIM] = lse[j * ATT_BLK:(j + 1) * ATT_BLK]
            return 0

        lax.fori_loop(0, nblk, blk, 0)

    qspec = pl.BlockSpec((rc, hw), lambda r, h, c: (c, r * qcols + grp * N_KV_HEADS + h))
    kspec = pl.BlockSpec((n, HEAD_DIM), lambda r, h, c: (0, r * N_KV_HEADS + h))
    ospec = pl.BlockSpec((rc, hw), lambda r, h, c: (c, r * ocols + h))
    o, l = pl.pallas_call(
        body, grid=(dil, N_KV_HEADS, nrc), in_specs=[qspec, kspec, kspec], out_specs=[ospec, ospec],
        out_shape=[jax.ShapeDtypeStruct((n, dil * H * HEAD_DIM), F32)] * 2,
        compiler_params=_params(("parallel", "parallel", "arbitrary")), name=f"att_fwd_d{dil}",
    )(q2, k2, v2)
    return o.reshape(T, H * HEAD_DIM), l.reshape(T, H * HEAD_DIM)


def _att_combine(outs, lses):
    T, W = outs[0].shape
    ng = len(outs)
    tr, tcol = _tile(T, 512, 8), _tile(W, 512)

    def body(*refs):
        o_refs, l_refs = refs[:ng], refs[ng:2 * ng]
        ob_ref, lse_ref = refs[2 * ng:]
        ls = [r[...] for r in l_refs]
        m = functools.reduce(jnp.maximum, ls)
        es = [jnp.exp(l - m) for l in ls]
        den = functools.reduce(lambda a, b: a + b, es)
        num = functools.reduce(lambda a, b: a + b, [e * o[...] for e, o in zip(es, o_refs)])
        ob_ref[...] = (num / den).astype(BF16)
        lse_ref[...] = m + jnp.log(den)

    spec = pl.BlockSpec((tr, tcol), lambda i, j: (i, j))
    return pl.pallas_call(
        body, grid=(T // tr, W // tcol), in_specs=[spec] * (2 * ng), out_specs=[spec, spec],
        out_shape=[jax.ShapeDtypeStruct((T, W), BF16), jax.ShapeDtypeStruct((T, W), F32)],
        compiler_params=_params(("parallel", "parallel")), name="att_combine")(*outs, *lses)


def _att_bwd(q, k, v, o, do, lse, grp, dil):
    T = q.shape[0]
    H = o.shape[1] // HEAD_DIM
    n, rep, rc, nrc, nblk = _att_geometry(T, H, dil)
    hw = rep * HEAD_DIM
    scale = HEAD_DIM ** -0.5
    q2 = q.reshape(n, dil * q.shape[1])
    k2 = k.reshape(n, dil * k.shape[1])
    v2 = v.reshape(n, dil * v.shape[1])
    o2 = o.reshape(n, dil * o.shape[1])
    do2 = do.reshape(n, dil * do.shape[1])
    l2 = lse.reshape(n, dil * lse.shape[1])
    qcols = q.shape[1] // hw
    ocols = H * HEAD_DIM // hw

    def body(q_ref, k_ref, v_ref, o_ref, do_ref, l_ref, dq_ref, dk_ref, dv_ref):
        ch = pl.program_id(2)

        @pl.when(ch == 0)
        def _():
            dk_ref[...] = jnp.zeros_like(dk_ref)
            dv_ref[...] = jnp.zeros_like(dv_ref)

        def blk(b, _):
            gb = ch * nblk + b
            rows = pl.ds(pl.multiple_of(b * ATT_BLK, ATT_BLK), ATT_BLK)
            kc = pl.ds(pl.multiple_of(gb * ATT_BLK, ATT_BLK), ATT_BLK)
            kp = pl.ds(pl.multiple_of(jnp.maximum(gb - 1, 0) * ATT_BLK, ATT_BLK), ATT_BLK)
            qs = _stack_heads(q_ref, rows, rep)
            dos = _stack_heads(do_ref, rows, rep)
            os_ = _stack_heads(o_ref, rows, rep)
            lss = _stack_heads(l_ref, rows, rep)
            kcat = jnp.concatenate([k_ref[kp, :], k_ref[kc, :]], axis=0)
            vcat = jnp.concatenate([v_ref[kp, :], v_ref[kc, :]], axis=0)
            s = _dot(qs, kcat, NT) * scale
            p = jnp.where(_att_masks(rep, gb), jnp.exp(s - jnp.concatenate([lss, lss], axis=1)), 0.0)
            delta = jnp.sum(dos.astype(F32) * os_.astype(F32), axis=-1, keepdims=True)
            dp = _dot(dos, vcat, NT)
            ds = (p * (dp - delta) * scale).astype(BF16)
            dvc = _dot(p.astype(BF16), dos, TN)
            dkc = _dot(ds, qs, TN)
            dqs = _dot(ds, kcat)
            for j in range(rep):
                dq_ref[rows, j * HEAD_DIM:(j + 1) * HEAD_DIM] = dqs[j * ATT_BLK:(j + 1) * ATT_BLK].astype(BF16)
            dk_ref[kc, :] += dkc[ATT_BLK:]
            dv_ref[kc, :] += dvc[ATT_BLK:]

            @pl.when(gb > 0)
            def _():
                dk_ref[kp, :] += dkc[:ATT_BLK]
                dv_ref[kp, :] += dvc[:ATT_BLK]

            return 0

        lax.fori_loop(0, nblk, blk, 0)

    qspec = pl.BlockSpec((rc, hw), lambda r, h, c: (c, r * qcols + grp * N_KV_HEADS + h))
    kspec = pl.BlockSpec((n, HEAD_DIM), lambda r, h, c: (0, r * N_KV_HEADS + h))
    ospec = pl.BlockSpec((rc, hw), lambda r, h, c: (c, r * ocols + h))
    dq, dk, dv = pl.pallas_call(
        body, grid=(dil, N_KV_HEADS, nrc), in_specs=[qspec, kspec, kspec, ospec, ospec, ospec],
        out_specs=[ospec, kspec, kspec],
        out_shape=[jax.ShapeDtypeStruct((n, dil * H * HEAD_DIM), BF16),
                   jax.ShapeDtypeStruct((n, dil * N_KV_HEADS * HEAD_DIM), F32),
                   jax.ShapeDtypeStruct((n, dil * N_KV_HEADS * HEAD_DIM), F32)],
        compiler_params=_params(("parallel", "parallel", "arbitrary")), name=f"att_bwd_d{dil}",
    )(q2, k2, v2, o2, do2, l2)
    return (dq.reshape(T, H * HEAD_DIM), dk.reshape(T, N_KV_HEADS * HEAD_DIM),
            dv.reshape(T, N_KV_HEADS * HEAD_DIM))


def _sum_kv(dks, dvs):
    T, W = dks[0].shape
    ng = len(dks)
    tr = _tile(T, 512, 8)

    def body(*refs):
        o_ref = refs[2 * ng]
        o_ref[0] = functools.reduce(lambda a, b: a + b, [r[...] for r in refs[:ng]]).astype(BF16)
        o_ref[1] = functools.reduce(lambda a, b: a + b, [r[...] for r in refs[ng:2 * ng]]).astype(BF16)

    spec = pl.BlockSpec((tr, W), lambda i: (i, 0))
    return pl.pallas_call(
        body, grid=(T // tr,), in_specs=[spec] * (2 * ng),
        out_specs=pl.BlockSpec((2, tr, W), lambda i: (0, i, 0)),
        out_shape=jax.ShapeDtypeStruct((2, T, W), BF16),
        compiler_params=_params(("parallel",)), name="sum_kv")(*dks, *dvs)


def _local_step(x, tgt, w):
    T, D = x.shape
    g = {}

    (u0,) = _rms_fwd("rms_a", x, [w["a_norm"]], [F32])
    lbr, lbi, bbt_re, bbt_im = _s5_param_fwd(w["lam_re"], w["lam_im"], w["log_dt"], w["bt_re"], w["bt_im"])
    a_re, a_im = lbr.reshape(-1, 8, 128), lbi.reshape(-1, 8, 128)
    bbr = _block_diag(bbt_re.transpose(1, 0, 2), True).astype(BF16)
    bbi = _block_diag(bbt_im.transpose(1, 0, 2), True).astype(BF16)
    ccr = _block_diag(w["c_re"], False).astype(BF16)
    cci = _block_diag(w["c_im"], False).astype(BF16)
    z, h0 = _s5_fwd(u0, bbr, bbi, ccr, cci, a_re, a_im, w["s5_d"])
    x1, vg = _mm_dual_fwd("glu_fwd", z, w["glu"], x, "glu")

    def ffn_fwd(xin, layer):
        (nrm,) = _rms_fwd(f"rms_f{layer}", xin, [w["ffn_norm"][layer:layer + 1]], [BF16])
        gu, act = _mm_dual_fwd(f"ffn_in{layer}", nrm, w["win"][layer], None, "ffn")
        xout = _mm_nn(f"ffn_out{layer}", act, w["wout"][layer], res=xin)
        return xout, nrm, gu, act

    x2, n1, gu0, act0 = ffn_fwd(x1, 0)
    kvn, hb = _rms_fwd("rms_b", x2, [w["kv_norm"], w["b_norm"]], [BF16, BF16])
    k, v = _mm_kv("kv_proj", kvn, w["wkv"])
    q = _mm_nn("q_proj", hb, w["wq"], out_dtype=BF16)
    outs, lses = [], []
    for grp, dil in enumerate(DILATIONS):
        o_g, l_g = _att_fwd(q, k, v, grp, dil)
        outs.append(o_g)
        lses.append(l_g)
    o, lse = _att_combine(outs, lses)
    x3 = _mm_nn("o_proj", o, w["wo"], res=x2)
    x4, n3, gu1, act1 = ffn_fwd(x3, 1)
    loss_blk, dx4, dx4b, g["final_norm"] = _loss_head(x4, w["final_norm"], tgt)

    def ffn_bwd(dx, dxb, xin, nrm, gu, act, layer):
        dgu = _mm_nt_ffn_bwd(f"ffn_dact{layer}", dxb, w["wout"][layer], gu)
        g_wout = _mm_tn(f"ffn_dwout{layer}", act, dxb)
        g_win = _mm_tn_pair(f"ffn_dwin{layer}", nrm, dgu)
        dn = _mm_nt_pair(f"ffn_dn{layer}", dgu, w["win"][layer])
        dxo, dxob, (dgn,) = _rms_bwd(f"rms_f{layer}_bwd", xin, dx, [w["ffn_norm"][layer:layer + 1]], [dn])
        return dxo, dxob, g_win, g_wout, dgn

    dx3, dx3b, g_win1, g_wout1, dfn1 = ffn_bwd(dx4, dx4b, x3, n3, gu1, act1, 1)
    do = _mm_nt("o_proj_dx", [dx3b], w["wo"], out_dtype=BF16)
    g["wo"] = _mm_tn("o_proj_dw", o, dx3b)
    dqs, dks, dvs = [], [], []
    for grp, dil in enumerate(DILATIONS):
        dq_g, dk_g, dv_g = _att_bwd(q, k, v, o, do, lse, grp, dil)
        dqs.append(dq_g)
        dks.append(dk_g)
        dvs.append(dv_g)
    dkv = _sum_kv(dks, dvs)
    dhb = _mm_nt("q_proj_dx", dqs, w["wq"])
    g["wq"] = _mm_tn("q_proj_dw", hb, jnp.concatenate(dqs, axis=1))
    dkvn = _mm_nt_pair("kv_proj_dx", dkv, w["wkv"])
    g["wkv"] = _mm_tn_pair("kv_proj_dw", kvn, dkv)
    dx2, dx2b, (g["kv_norm"], g["b_norm"]) = _rms_bwd(
        "rms_b_bwd", x2, dx3, [w["kv_norm"], w["b_norm"]], [dkvn, dhb])
    dx1, dx1b, g_win0, g_wout0, dfn0 = ffn_bwd(dx2, dx2b, x1, n1, gu0, act0, 0)
    g["win"] = [g_win0, g_win1]
    g["wout"] = [g_wout0, g_wout1]
    g["ffn_norm"] = jnp.concatenate([dfn0, dfn1], axis=0)

    dvg = _glu_bwd(dx1, vg)
    dz = _mm_nt_pair("glu_dx", dvg, w["glu"])
    g["glu"] = _mm_tn_pair("glu_dw", z, dvg)
    du, g["s5_d"], da_re, da_im, dbbr, dbbi, dccr, dcci = _s5_bwd(
        u0, dz, h0, bbr, bbi, ccr, cci, a_re, a_im, w["s5_d"])
    C, G, P = w["bt_re"].shape
    g["c_re"] = _block_diag_extract(dccr, False, C, P)
    g["c_im"] = -_block_diag_extract(dcci, False, C, P)
    g["lam_re"], g["lam_im"], g["log_dt"], g["bt_re"], g["bt_im"] = _s5_param_bwd(
        w["lam_re"], w["lam_im"], w["log_dt"], w["bt_re"], w["bt_im"],
        da_re.reshape(G, P), da_im.reshape(G, P),
        _block_diag_extract(dbbr, True, C, P).transpose(1, 0, 2),
        _block_diag_extract(dbbi, True, C, P).transpose(1, 0, 2))
    grad_x, _, (g["a_norm"],) = _rms_bwd("rms_a_bwd", x, dx1, [w["a_norm"]], [du])
    return loss_blk, grad_x, g


def _coords():
    return lax.axis_index("x"), lax.axis_index("y"), lax.axis_index("c")


def _dev_index(dev):
    return 4 * dev[0] + 2 * dev[1] + dev[2]


def _shard_window(ref, axis, width, idx):
    sl = [slice(None)] * len(ref.shape)
    sl[axis] = pl.ds(pl.multiple_of(idx * width, width), width)
    return ref.at[tuple(sl)]


def _all_gather(name, shards, axes):
    na = len(shards)
    widths = [s.shape[ax] for s, ax in zip(shards, axes)]
    out_shapes = []
    for s, ax in zip(shards, axes):
        shp = list(s.shape)
        shp[ax] *= N_DEV
        out_shapes.append(jax.ShapeDtypeStruct(tuple(shp), s.dtype))

    def body(*refs):
        ins, outs = refs[:na], refs[na:2 * na]
        send_sems, recv_sems, local_sems = refs[2 * na:]
        x, y, c = _coords()
        me, sib = (x, y, c), (x, y, 1 - c)
        chips = [(1 - x, y), (x, 1 - y), (1 - x, 1 - y)]

        def blk(a, dev):
            return _shard_window(outs[a], axes[a], widths[a], _dev_index(dev))

        def copy(a, kk, block, to, src=None):
            return pltpu.make_async_remote_copy(
                src_ref=blk(a, block) if src is None else src, dst_ref=blk(a, block),
                send_sem=send_sems.at[a, kk], recv_sem=recv_sems.at[a, kk],
                device_id=to, device_id_type=MESH)

        local = [pltpu.make_async_copy(ins[a], blk(a, me), local_sems.at[a]) for a in range(na)]
        for cp in local:
            cp.start()
        sent = []
        for a in range(na):
            first = [copy(a, 0, me, sib, src=ins[a])]
            first += [copy(a, 1 + j, me, (*chip, c), src=ins[a]) for j, chip in enumerate(chips)]
            for cp in first:
                cp.start()
            sent += first
        for a in range(na):
            for j, chip in enumerate(chips):
                copy(a, 1 + j, (*chip, c), me).wait_recv()
                fwd = copy(a, 4 + j, (*chip, c), sib)
                fwd.start()
                sent.append(fwd)
        for a in range(na):
            copy(a, 0, sib, me).wait_recv()
            for j, chip in enumerate(chips):
                copy(a, 4 + j, (*chip, 1 - c), me).wait_recv()
        for cp in sent:
            cp.wait_send()
        for cp in local:
            cp.wait()

    return pl.pallas_call(
        body, out_shape=out_shapes, in_specs=[ANY] * na, out_specs=[ANY] * na,
        scratch_shapes=[pltpu.SemaphoreType.DMA((na, 7)), pltpu.SemaphoreType.DMA((na, 7)),
                        pltpu.SemaphoreType.DMA((na,))],
        name=name)(*shards)


def _pair_exchange(grads, axes):
    na = len(grads)
    widths = [gr.shape[ax] // N_DEV for gr, ax in zip(grads, axes)]
    out_shapes = []
    for gr, ax, wd in zip(grads, axes, widths):
        shp = list(gr.shape)
        shp[ax] = wd
        out_shapes.append(jax.ShapeDtypeStruct((4, *shp), gr.dtype))

    def body(*refs):
        ins, outs = refs[:na], refs[na:2 * na]
        send_sems, recv_sems = refs[2 * na:]
        x, y, c = _coords()
        sib = (x, y, 1 - c)
        chips = [(x, y), (1 - x, y), (x, 1 - y), (1 - x, 1 - y)]
        cps = []
        for a in range(na):
            for kk, chip in enumerate(chips):
                cp = pltpu.make_async_remote_copy(
                    src_ref=_shard_window(ins[a], axes[a], widths[a], _dev_index((*chip, 1 - c))),
                    dst_ref=outs[a].at[kk], send_sem=send_sems.at[a, kk], recv_sem=recv_sems.at[a, kk],
                    device_id=sib, device_id_type=MESH)
                cp.start()
                cps.append(cp)
        for cp in cps:
            cp.wait()

    return pl.pallas_call(
        body, out_shape=out_shapes, in_specs=[ANY] * na, out_specs=[ANY] * na,
        scratch_shapes=[pltpu.SemaphoreType.DMA((na, 4)), pltpu.SemaphoreType.DMA((na, 4))],
        name="rs_pair_exchange")(*grads)


def _pair_sum(name, grad, land, axis, pos):
    wd = grad.shape[axis] // N_DEV
    shard_shape = land.shape[1:]
    rows, cols = shard_shape
    tr = _tile(rows, 256, 8)
    nt = rows // tr

    def dev_of(kk, pos_ref):
        return 4 * (pos_ref[0] ^ (kk & 1)) + 2 * (pos_ref[1] ^ (kk >> 1)) + pos_ref[2]

    if axis == 1:
        gspec = pl.BlockSpec((tr, wd), lambda kk, t, p: (t, dev_of(kk, p)))
    else:
        gspec = pl.BlockSpec((tr, cols), lambda kk, t, p: (dev_of(kk, p) * nt + t, 0))
    lspec = pl.BlockSpec((None, tr, cols), lambda kk, t, p: (kk, t, 0))
    p0spec = pl.BlockSpec((tr, cols), lambda kk, t, p: (jnp.where(kk == 0, t, nt - 1), 0))
    pbspec = pl.BlockSpec((None, tr, cols), lambda kk, t, p: (jnp.maximum(kk - 1, 0), jnp.where(kk == 0, 0, t), 0))

    def body(pos_ref, g_ref, l_ref, p0_ref, pb_ref):
        kk = pl.program_id(0)
        s = g_ref[...] + l_ref[...]

        @pl.when(kk == 0)
        def _():
            p0_ref[...] = s

        @pl.when(kk > 0)
        def _():
            pb_ref[...] = s.astype(BF16)

    return pl.pallas_call(
        body,
        grid_spec=pltpu.PrefetchScalarGridSpec(
            num_scalar_prefetch=1, grid=(4, nt), in_specs=[gspec, lspec], out_specs=[p0spec, pbspec]),
        out_shape=[jax.ShapeDtypeStruct(shard_shape, F32), jax.ShapeDtypeStruct((3, *shard_shape), BF16)],
        compiler_params=_params(("arbitrary", "arbitrary")), name=name)(pos, grad, land)


def _chip_exchange(parts):
    na = len(parts)

    def body(*refs):
        ins, outs = refs[:na], refs[na:2 * na]
        send_sems, recv_sems = refs[2 * na:]
        x, y, c = _coords()
        chips = [(1 - x, y), (x, 1 - y), (1 - x, 1 - y)]
        cps = []
        for a in range(na):
            for kk, chip in enumerate(chips):
                cp = pltpu.make_async_remote_copy(
                    src_ref=ins[a].at[kk], dst_ref=outs[a].at[kk],
                    send_sem=send_sems.at[a, kk], recv_sem=recv_sems.at[a, kk],
                    device_id=(*chip, c), device_id_type=MESH)
                cp.start()
                cps.append(cp)
        for cp in cps:
            cp.wait()

    return pl.pallas_call(
        body, out_shape=[jax.ShapeDtypeStruct(p.shape, p.dtype) for p in parts],
        in_specs=[ANY] * na, out_specs=[ANY] * na,
        scratch_shapes=[pltpu.SemaphoreType.DMA((na, 3)), pltpu.SemaphoreType.DMA((na, 3))],
        name="rs_chip_exchange")(*parts)


def _adamw(name, parts, w, m, v):
    rows, cols = w.shape
    tr = _tile(rows, 256, 8)
    npart = len(parts)
    c1 = 1.0 - ADAM_B1 ** ADAM_STEP
    c2 = 1.0 - ADAM_B2 ** ADAM_STEP

    def body(*refs):
        p_refs = refs[:npart]
        w_ref, m_ref, v_ref, g_ref, d_ref, nm_ref, nv_ref = refs[npart:]
        g = None
        for r in p_refs:
            if len(r.shape) == 3:
                for i in range(r.shape[0]):
                    t = r[i].astype(F32)
                    g = t if g is None else g + t
            else:
                t = r[...].astype(F32)
                g = t if g is None else g + t
        nm = ADAM_B1 * m_ref[...] + (1.0 - ADAM_B1) * g
        nv = ADAM_B2 * v_ref[...] + (1.0 - ADAM_B2) * (g * g)
        g_ref[...] = g
        nm_ref[...] = nm
        nv_ref[...] = nv
        d_ref[...] = -ADAM_LR * ((nm / c1) / (jnp.sqrt(nv / c2) + ADAM_EPS) + ADAM_WD * w_ref[...])

    spec = pl.BlockSpec((tr, cols), lambda i: (i, 0))
    pspecs = [pl.BlockSpec((p.shape[0], tr, cols), lambda i: (0, i, 0)) if p.ndim == 3 else spec
              for p in parts]
    return pl.pallas_call(
        body, grid=(rows // tr,), in_specs=pspecs + [spec] * 3, out_specs=[spec] * 4,
        out_shape=[jax.ShapeDtypeStruct((rows, cols), F32)] * 4,
        compiler_params=_params(("parallel",)), name=name)(*parts, w, m, v)


SMALL_NAMES = ("s5_lam_re", "s5_lam_im", "s5_log_dt", "s5_b_re", "s5_b_im", "s5_c_re", "s5_c_im",
               "ffn_norm", "b_norm_mix", "kv_norm", "final_norm")
SMALL_PAD = 1024


def _pack(parts):
    flat = []
    for p in parts:
        f = p.reshape(-1)
        pad = (-f.shape[0]) % SMALL_PAD
        if pad:
            f = jnp.concatenate([f, jnp.zeros((pad,), f.dtype)])
        flat.append(f)
    return jnp.concatenate(flat).reshape(-1, 128)


def _unpack(packed, shapes):
    flat = packed.reshape(-1)
    out, off = [], 0
    for shp in shapes:
        size = math.prod(shp)
        out.append(flat[off:off + size].reshape(shp))
        off += size + (-size) % SMALL_PAD
    return out


def kernel(x, s5_lam_re, s5_lam_im, s5_log_dt, s5_b_re, s5_b_im, s5_c_re, s5_c_im, s5_d, s5_w_glu, a_norm_mix, ffn_norm, ffn_w_in, ffn_w_out, b_norm_mix, attn_w_q, attn_w_o, kv_norm, w_kv, final_norm, loss_target, m_s5_lam_re, m_s5_lam_im, m_s5_log_dt, m_s5_b_re, m_s5_b_im, m_s5_c_re, m_s5_c_im, m_s5_d, m_s5_w_glu, m_a_norm_mix, m_ffn_norm, m_ffn_w_in, m_ffn_w_out, m_b_norm_mix, m_attn_w_q, m_attn_w_o, m_kv_norm, m_w_kv, m_final_norm, v_s5_lam_re, v_s5_lam_im, v_s5_log_dt, v_s5_b_re, v_s5_b_im, v_s5_c_re, v_s5_c_im, v_s5_d, v_s5_w_glu, v_a_norm_mix, v_ffn_norm, v_ffn_w_in, v_ffn_w_out, v_b_norm_mix, v_attn_w_q, v_attn_w_o, v_kv_norm, v_w_kv, v_final_norm):
    args = dict(locals())
    T, D = x.shape[1], x.shape[2]
    n_layers = ffn_w_in.shape[0]
    xi_, yi_, ci_ = _coords()
    pos = jnp.stack([xi_, yi_, ci_]).astype(jnp.int32)
    me = 4 * xi_ + 2 * yi_ + ci_

    big_names = ["glu"] + [f"win{l}" for l in range(n_layers)] + [f"wout{l}" for l in range(n_layers)] \
        + ["wkv", "wq", "wo"]
    big_shards = [s5_w_glu[0]] + [ffn_w_in[l] for l in range(n_layers)] \
        + [ffn_w_out[l] for l in range(n_layers)] + [w_kv, attn_w_q[0], attn_w_o[0]]
    big_axes = [1] + [1] * n_layers + [0] * n_layers + [0, 1, 0]
    vec_shard = jnp.concatenate([s5_d, a_norm_mix], axis=0)
    gathered = _all_gather("weights_all_gather", [vec_shard] + [s.astype(BF16) for s in big_shards],
                           [1] + big_axes)
    vecs, full = gathered[0], dict(zip(big_names, gathered[1:]))

    G, P, C = s5_b_re.shape[1:]
    w = dict(
        a_norm=vecs[1:2], s5_d=vecs[0:1], glu=full["glu"],
        win=[full[f"win{l}"] for l in range(n_layers)], wout=[full[f"wout{l}"] for l in range(n_layers)],
        wkv=full["wkv"], wq=full["wq"], wo=full["wo"],
        ffn_norm=ffn_norm, b_norm=b_norm_mix, kv_norm=kv_norm.reshape(1, D), final_norm=final_norm.reshape(1, D),
        lam_re=s5_lam_re[0], lam_im=s5_lam_im[0], log_dt=s5_log_dt.reshape(G, 1),
        bt_re=s5_b_re[0].transpose(2, 0, 1), bt_im=s5_b_im[0].transpose(2, 0, 1),
        c_re=s5_c_re[0], c_im=s5_c_im[0],
    )
    loss_blk, grad_x, g = _local_step(x[0], loss_target[0], w)
    loss = lax.psum(loss_blk[0, 0], ("x", "y", "c"))

    big_grads = [g["glu"]] + g["win"] + g["wout"] + [g["wkv"], g["wq"], g["wo"]]
    lands = _pair_exchange(big_grads, big_axes)
    p0s, pbs = [], []
    for name, gr, land, ax in zip(big_names, big_grads, lands, big_axes):
        p0, pb = _pair_sum(f"rs_pair_sum_{name}", gr, land, ax, pos)
        p0s.append(p0)
        pbs.append(pb)
    recvd = _chip_exchange(pbs)

    out = {}

    def put(name, res, shape):
        for kind, r in zip(("grad", "delta", "new_m", "new_v"), res):
            out[f"{kind}_{name}"] = r.reshape(shape)

    big_out_names = ["s5_w_glu"] + ["ffn_w_in"] * n_layers + ["ffn_w_out"] * n_layers \
        + ["w_kv", "attn_w_q", "attn_w_o"]
    layered = {}
    for name, oname, shard, p0, rc in zip(big_names, big_out_names, big_shards, p0s, recvd):
        if oname in ("ffn_w_in", "ffn_w_out"):
            layer = int(name[-1])
            res = _adamw(f"adamw_{name}", [p0, rc], shard, args["m_" + oname][layer], args["v_" + oname][layer])
            layered.setdefault(oname, []).append(res)
        else:
            mm, vv = args["m_" + oname], args["v_" + oname]
            res = _adamw(f"adamw_{name}", [p0, rc], shard, mm.reshape(shard.shape), vv.reshape(shard.shape))
            put(oname, res, args[oname].shape)
    for oname, per_layer in layered.items():
        put(oname, [jnp.stack([r[i] for r in per_layer]) for i in range(4)], args[oname].shape)

    small_g = dict(
        s5_lam_re=g["lam_re"], s5_lam_im=g["lam_im"], s5_log_dt=g["log_dt"],
        s5_b_re=g["bt_re"].transpose(1, 2, 0), s5_b_im=g["bt_im"].transpose(1, 2, 0),
        s5_c_re=g["c_re"], s5_c_im=g["c_im"], ffn_norm=g["ffn_norm"], b_norm_mix=g["b_norm"],
        kv_norm=g["kv_norm"], final_norm=g["final_norm"])
    packed = _pack([small_g[n] for n in SMALL_NAMES] + [g["s5_d"], g["a_norm"]])
    rows = packed.shape[0]
    (all_parts,) = _all_gather("small_grads_all_gather", [packed], [0])
    all_parts = all_parts.reshape(N_DEV, rows, 128)
    n_rep_rows = _pack([small_g[n] for n in SMALL_NAMES]).shape[0]
    w_pack = _pack([args[n] for n in SMALL_NAMES])
    m_pack = _pack([args["m_" + n] for n in SMALL_NAMES])
    v_pack = _pack([args["v_" + n] for n in SMALL_NAMES])
    res = _adamw("adamw_small", [all_parts[:, :n_rep_rows]], w_pack, m_pack, v_pack)
    shapes = [args[n].shape for n in SMALL_NAMES]
    unpacked = [_unpack(r, shapes) for r in res]
    for i, n in enumerate(SMALL_NAMES):
        put(n, [u[i] for u in unpacked], args[n].shape)
    ws = D // N_DEV
    tail = all_parts[:, n_rep_rows:].reshape(N_DEV, 2, D)
    tail = lax.dynamic_slice_in_dim(tail, me * ws, ws, axis=2)
    res = _adamw("adamw_vec", [tail], vec_shard,
                 jnp.concatenate([m_s5_d, m_a_norm_mix], axis=0), jnp.concatenate([v_s5_d, v_a_norm_mix], axis=0))
    put("s5_d", [r[0:1] for r in res], s5_d.shape)
    put("a_norm_mix", [r[1:2] for r in res], a_norm_mix.shape)

    names = ("s5_lam_re", "s5_lam_im", "s5_log_dt", "s5_b_re", "s5_b_im", "s5_c_re", "s5_c_im", "s5_d",
             "s5_w_glu", "a_norm_mix", "ffn_norm", "ffn_w_in", "ffn_w_out", "b_norm_mix", "attn_w_q",
             "attn_w_o", "kv_norm", "w_kv", "final_norm")
    result = [loss, grad_x.reshape(x.shape)]
    for kind in ("grad", "delta", "new_m", "new_v"):
        result += [out[f"{kind}_{n}"] for n in names]
    return tuple(result)
```

```python
import functools
import math

import jax
import jax.numpy as jnp
from jax import lax
from jax.experimental import pallas as pl
from jax.experimental.pallas import tpu as pltpu

F32 = jnp.float32
BF16 = jnp.bfloat16

EPS = 1e-6
NEG_INF = -1e30
HEAD_DIM = 128
N_KV_HEADS = 4
DILATIONS = (1, 4, 16)
ATT_BLK = 128
S5_C = 16
S5_P = 64
S5_GB = 16
S5_CH = S5_GB * S5_C
S5_W = S5_GB * S5_P
S5_UNROLL = 4
N_DEV = 8

ADAM_LR = 0.001
ADAM_B1 = 0.9
ADAM_B2 = 0.999
ADAM_EPS = 1e-08
ADAM_WD = 0.01
ADAM_STEP = 10

VMEM_LIMIT_BYTES = 56 * 1024 * 1024
MESH = pl.DeviceIdType.MESH
ANY = pl.BlockSpec(memory_space=pl.ANY)


def _tile(n, pref, align=128):
    t = (min(pref, n) // align) * align
    while t >= align:
        if n % t == 0:
            return t
        t -= align
    return n


def _params(sem):
    return pltpu.CompilerParams(dimension_semantics=sem, vmem_limit_bytes=VMEM_LIMIT_BYTES)


def _sigmoid(x):
    return 1.0 / (1.0 + jnp.exp(-x))


NN = (((1,), (0,)), ((), ()))
NT = (((1,), (1,)), ((), ()))
TN = (((0,), (0,)), ((), ()))


def _dot(a, b, dims=NN):
    return lax.dot_general(a, b, dims, preferred_element_type=F32)


def _matmul(name, grid, ins, in_specs, products, dims, out_shapes, out_specs, acc_shapes, epilogue):
    n_in, n_out, nk = len(ins), len(out_shapes), grid[2]

    def body(*refs):
        in_refs = refs[:n_in]
        out_refs = refs[n_in:n_in + n_out]
        acc_refs = refs[n_in + n_out:]

        def prods():
            vals = [None] * len(acc_shapes)
            for ai, bi, ci in products:
                d = _dot(in_refs[ai][...], in_refs[bi][...], dims)
                vals[ci] = d if vals[ci] is None else vals[ci] + d
            return vals

        if nk == 1:
            epilogue(in_refs, out_refs, prods())
        else:
            k = pl.program_id(2)

            @pl.when(k == 0)
            def _():
                for a in acc_refs:
                    a[...] = jnp.zeros_like(a)

            for a, v in zip(acc_refs, prods()):
                a[...] += v

            @pl.when(k == nk - 1)
            def _():
                epilogue(in_refs, out_refs, [a[...] for a in acc_refs])

    scratch = [] if nk == 1 else [pltpu.VMEM(s, F32) for s in acc_shapes]
    return pl.pallas_call(
        body, grid=grid, in_specs=in_specs, out_specs=out_specs, out_shape=out_shapes,
        scratch_shapes=scratch, compiler_params=_params(("parallel", "parallel", "arbitrary")),
        name=name)(*ins)


def _mm_dual_fwd(name, a, w, res, kind):
    T, K = a.shape
    N = w.shape[1] // 2
    tm, tn = _tile(T, 512), _tile(N, 512)
    nj = N // tn
    grid = (T // tm, nj, 1)
    ins = [a, w, w]
    in_specs = [pl.BlockSpec((tm, K), lambda i, j, k: (i, 0)),
                pl.BlockSpec((K, tn), lambda i, j, k: (0, j)),
                pl.BlockSpec((K, tn), lambda i, j, k: (0, j + nj))]
    pair_spec = pl.BlockSpec((2, tm, tn), lambda i, j, k: (0, i, j))
    tile_spec = pl.BlockSpec((tm, tn), lambda i, j, k: (i, j))
    if kind == "glu":
        ins.append(res)
        in_specs.append(tile_spec)

        def epilogue(in_refs, out_refs, accs):
            val, gate = accs
            out_refs[0][...] = in_refs[3][...] + val * _sigmoid(gate)
            out_refs[1][0] = val.astype(BF16)
            out_refs[1][1] = gate.astype(BF16)

        out_shapes = [jax.ShapeDtypeStruct((T, N), F32), jax.ShapeDtypeStruct((2, T, N), BF16)]
        out_specs = [tile_spec, pair_spec]
    else:
        def epilogue(in_refs, out_refs, accs):
            g, u = accs
            out_refs[0][0] = g.astype(BF16)
            out_refs[0][1] = u.astype(BF16)
            out_refs[1][...] = (g * _sigmoid(g) * u).astype(BF16)

        out_shapes = [jax.ShapeDtypeStruct((2, T, N), BF16), jax.ShapeDtypeStruct((T, N), BF16)]
        out_specs = [pair_spec, tile_spec]
    return _matmul(name, grid, ins, in_specs, [(0, 1, 0), (0, 2, 1)], NN, out_shapes, out_specs,
                   [(tm, tn), (tm, tn)], epilogue)


def _mm_kv(name, a, w):
    T, K = a.shape
    N = w.shape[1] // 2
    tm, tn = _tile(T, 512), _tile(N, 512)
    nj = N // tn
    tile_spec = pl.BlockSpec((tm, tn), lambda i, j, k: (i, j))

    def epilogue(in_refs, out_refs, accs):
        out_refs[0][...] = accs[0].astype(BF16)
        out_refs[1][...] = accs[1].astype(BF16)

    return _matmul(name, (T // tm, nj, 1), [a, w, w],
                   [pl.BlockSpec((tm, K), lambda i, j, k: (i, 0)),
                    pl.BlockSpec((K, tn), lambda i, j, k: (0, j)),
                    pl.BlockSpec((K, tn), lambda i, j, k: (0, j + nj))],
                   [(0, 1, 0), (0, 2, 1)], NN,
                   [jax.ShapeDtypeStruct((T, N), BF16)] * 2, [tile_spec, tile_spec],
                   [(tm, tn), (tm, tn)], epilogue)


def _mm_nn(name, a, w, res=None, out_dtype=F32):
    T, K = a.shape
    N = w.shape[1]
    tm, tn, tk = _tile(T, 1024), _tile(N, 1024), _tile(K, 2048 if K <= 2048 else 512)
    grid = (T // tm, N // tn, K // tk)
    tile_spec = pl.BlockSpec((tm, tn), lambda i, j, k: (i, j))
    ins = [a, w]
    in_specs = [pl.BlockSpec((tm, tk), lambda i, j, k: (i, k)),
                pl.BlockSpec((tk, tn), lambda i, j, k: (k, j))]
    if res is not None:
        ins.append(res)
        in_specs.append(tile_spec)

    def epilogue(in_refs, out_refs, accs):
        v = accs[0]
        if res is not None:
            v = v + in_refs[2][...]
        out_refs[0][...] = v.astype(out_dtype)

    return _matmul(name, grid, ins, in_specs, [(0, 1, 0)], NN,
                   [jax.ShapeDtypeStruct((T, N), out_dtype)], [tile_spec], [(tm, tn)], epilogue)[0]


def _dep_operand(ins, in_specs, dep):
    if dep is not None:
        ins.append(dep)
        in_specs.append(pl.BlockSpec((8, 128), lambda *_: (0, 0)))


def _mm_nt(name, a_list, w, out_dtype=F32, dep=None):
    T, Np = a_list[0].shape
    Ko = w.shape[0]
    n_parts = len(a_list)
    tm, tn, tk = _tile(T, 1024), _tile(Ko, 1024), _tile(Np, 512)
    nkp = Np // tk
    grid = (T // tm, Ko // tn, nkp)
    ins = list(a_list) + [w] * n_parts
    in_specs = [pl.BlockSpec((tm, tk), lambda i, j, k: (i, k)) for _ in a_list]
    in_specs += [pl.BlockSpec((tn, tk), functools.partial(lambda i, j, k, p: (j, p * nkp + k), p=p))
                 for p in range(n_parts)]
    products = [(p, n_parts + p, 0) for p in range(n_parts)]
    _dep_operand(ins, in_specs, dep)

    def epilogue(in_refs, out_refs, accs):
        out_refs[0][...] = accs[0].astype(out_dtype)

    return _matmul(name, grid, ins, in_specs, products, NT,
                   [jax.ShapeDtypeStruct((T, Ko), out_dtype)],
                   [pl.BlockSpec((tm, tn), lambda i, j, k: (i, j))], [(tm, tn)], epilogue)[0]


def _mm_nt_pair(name, a3, w):
    _, T, N = a3.shape
    Ko = w.shape[0]
    tm, tn, tk = _tile(T, 1024), _tile(Ko, 1024), _tile(N, 512)
    nkh = N // tk
    grid = (T // tm, Ko // tn, 2 * nkh)

    def epilogue(in_refs, out_refs, accs):
        out_refs[0][...] = accs[0]

    return _matmul(name, grid, [a3, w],
                   [pl.BlockSpec((None, tm, tk), lambda i, j, k: (k // nkh, i, k % nkh)),
                    pl.BlockSpec((tn, tk), lambda i, j, k: (j, k))],
                   [(0, 1, 0)], NT, [jax.ShapeDtypeStruct((T, Ko), F32)],
                   [pl.BlockSpec((tm, tn), lambda i, j, k: (i, j))], [(tm, tn)], epilogue)[0]


def _mm_nt_ffn_bwd(name, dx, w_out, gu, dep=None):
    T, D = dx.shape
    Fh = w_out.shape[0]
    tm, tn = _tile(T, 512), _tile(Fh, 512)
    pair_spec = pl.BlockSpec((2, tm, tn), lambda i, j, k: (0, i, j))

    def epilogue(in_refs, out_refs, accs):
        da = accs[0]
        g = in_refs[2][0].astype(F32)
        u = in_refs[2][1].astype(F32)
        s = _sigmoid(g)
        out_refs[0][0] = (da * u * (s * (1.0 + g * (1.0 - s)))).astype(BF16)
        out_refs[0][1] = (da * (g * s)).astype(BF16)

    ins = [dx, w_out, gu]
    in_specs = [pl.BlockSpec((tm, D), lambda i, j, k: (i, 0)),
                pl.BlockSpec((tn, D), lambda i, j, k: (j, 0)),
                pair_spec]
    _dep_operand(ins, in_specs, dep)
    return _matmul(name, (T // tm, Fh // tn, 1), ins, in_specs,
                   [(0, 1, 0)], NT, [jax.ShapeDtypeStruct((2, T, Fh), BF16)], [pair_spec],
                   [(tm, tn)], epilogue)[0]


def _mm_tn(name, a, d):
    T, Ko = a.shape
    N = d.shape[1]
    to, tn, tk = _tile(Ko, 1024), _tile(N, 1024), _tile(T, 1024)
    grid = (Ko // to, N // tn, T // tk)

    def epilogue(in_refs, out_refs, accs):
        out_refs[0][...] = accs[0]

    return _matmul(name, grid, [a, d],
                   [pl.BlockSpec((tk, to), lambda i, j, k: (k, i)),
                    pl.BlockSpec((tk, tn), lambda i, j, k: (k, j))],
                   [(0, 1, 0)], TN, [jax.ShapeDtypeStruct((Ko, N), F32)],
                   [pl.BlockSpec((to, tn), lambda i, j, k: (i, j))], [(to, tn)], epilogue)[0]


def _mm_tn_pair(name, a, d3):
    T, Ko = a.shape
    N = d3.shape[2]
    to, tn, tk = _tile(Ko, 2048), _tile(N, 512), _tile(T, 1024)
    njh = N // tn
    grid = (Ko // to, 2 * njh, T // tk)

    def epilogue(in_refs, out_refs, accs):
        out_refs[0][...] = accs[0]

    return _matmul(name, grid, [a, d3],
                   [pl.BlockSpec((tk, to), lambda i, j, k: (k, i)),
                    pl.BlockSpec((None, tk, tn), lambda i, j, k: (j // njh, k, j % njh))],
                   [(0, 1, 0)], TN, [jax.ShapeDtypeStruct((Ko, 2 * N), F32)],
                   [pl.BlockSpec((to, tn), lambda i, j, k: (i, j))], [(to, tn)], epilogue)[0]


def _rms_fwd(name, x, gains, dtypes, dep=None):
    T, D = x.shape
    n = len(gains)
    tr = _tile(T, 512, 8)
    n_dep = 0 if dep is None else 1

    def body(x_ref, *refs):
        xv = x_ref[...]
        xr = xv * lax.rsqrt(jnp.mean(xv * xv, axis=-1, keepdims=True) + EPS)
        for g_ref, o_ref in zip(refs[:n], refs[n + n_dep:]):
            o_ref[...] = (xr * g_ref[...]).astype(o_ref.dtype)

    row = pl.BlockSpec((tr, D), lambda i: (i, 0))
    vec = pl.BlockSpec((1, D), lambda i: (0, 0))
    ins, in_specs = [x, *gains], [row] + [vec] * n
    _dep_operand(ins, in_specs, dep)
    return pl.pallas_call(
        body, grid=(T // tr,), in_specs=in_specs, out_specs=[row] * n,
        out_shape=[jax.ShapeDtypeStruct((T, D), dt) for dt in dtypes],
        compiler_params=_params(("parallel",)), name=name)(*ins)


def _rms_bwd(name, x, dres, gains, dhs):
    T, D = x.shape
    n = len(gains)
    tr = _tile(T, 256, 8)

    def body(x_ref, dres_ref, *refs):
        g_refs, dh_refs = refs[:n], refs[n:2 * n]
        dx_ref, dxb_ref = refs[2 * n], refs[2 * n + 1]
        dg_refs = refs[2 * n + 2:]
        xv = x_ref[...]
        r = lax.rsqrt(jnp.mean(xv * xv, axis=-1, keepdims=True) + EPS)
        xr = xv * r
        w = None
        for g_ref, dh_ref, dg_ref in zip(g_refs, dh_refs, dg_refs):
            dh = dh_ref[...].astype(F32)

            @pl.when(pl.program_id(0) == 0)
            def _():
                dg_ref[...] = jnp.zeros_like(dg_ref)

            dg_ref[...] += jnp.sum(dh * xr, axis=0, keepdims=True)
            wi = dh * g_ref[...]
            w = wi if w is None else w + wi
        dx = dres_ref[...] + r * (w - xr * jnp.mean(w * xr, axis=-1, keepdims=True))
        dx_ref[...] = dx
        dxb_ref[...] = dx.astype(BF16)

    row = pl.BlockSpec((tr, D), lambda i: (i, 0))
    vec = pl.BlockSpec((1, D), lambda i: (0, 0))
    outs = pl.pallas_call(
        body, grid=(T // tr,), in_specs=[row, row] + [vec] * n + [row] * n,
        out_specs=[row, row] + [vec] * n,
        out_shape=[jax.ShapeDtypeStruct((T, D), F32), jax.ShapeDtypeStruct((T, D), BF16)]
        + [jax.ShapeDtypeStruct((1, D), F32)] * n,
        compiler_params=_params(("arbitrary",)), name=name)(x, dres, *gains, *dhs)
    return outs[0], outs[1], outs[2:]


def _loss_head(x, gain, target):
    T, D = x.shape
    tr = _tile(T, 256, 8)

    def body(x_ref, g_ref, t_ref, loss_ref, dx_ref, dxb_ref, dg_ref):
        @pl.when(pl.program_id(0) == 0)
        def _():
            loss_ref[...] = jnp.zeros_like(loss_ref)
            dg_ref[...] = jnp.zeros_like(dg_ref)

        xv = x_ref[...]
        r = lax.rsqrt(jnp.mean(xv * xv, axis=-1, keepdims=True) + EPS)
        xr = xv * r
        err = xr * g_ref[...] - t_ref[...]
        part = jnp.sum(jnp.sum(err * err, axis=-1, keepdims=True), axis=0, keepdims=True) * (0.5 / D)
        loss_ref[...] += jnp.broadcast_to(part, loss_ref.shape)
        dy = err * (1.0 / D)
        dg_ref[...] += jnp.sum(dy * xr, axis=0, keepdims=True)
        w = dy * g_ref[...]
        dx = r * (w - xr * jnp.mean(w * xr, axis=-1, keepdims=True))
        dx_ref[...] = dx
        dxb_ref[...] = dx.astype(BF16)

    row = pl.BlockSpec((tr, D), lambda i: (i, 0))
    vec = pl.BlockSpec((1, D), lambda i: (0, 0))
    return pl.pallas_call(
        body, grid=(T // tr,), in_specs=[row, vec, row],
        out_specs=[pl.BlockSpec((8, 128), lambda i: (0, 0)), row, row, vec],
        out_shape=[jax.ShapeDtypeStruct((8, 128), F32), jax.ShapeDtypeStruct((T, D), F32),
                   jax.ShapeDtypeStruct((T, D), BF16), jax.ShapeDtypeStruct((1, D), F32)],
        compiler_params=_params(("arbitrary",)), name="loss_head")(x, gain, target)


def _glu_bwd(dmix, vg, dep=None):
    T, N = dmix.shape
    tr, tc = _tile(T, 512, 8), _tile(N, 1024)
    n_dep = 0 if dep is None else 1

    def body(d_ref, vg_ref, *refs):
        o_ref = refs[n_dep]
        d = d_ref[...]
        val = vg_ref[0].astype(F32)
        s = _sigmoid(vg_ref[1].astype(F32))
        o_ref[0] = (d * s).astype(BF16)
        o_ref[1] = (d * val * s * (1.0 - s)).astype(BF16)

    pair = pl.BlockSpec((2, tr, tc), lambda i, j: (0, i, j))
    ins, in_specs = [dmix, vg], [pl.BlockSpec((tr, tc), lambda i, j: (i, j)), pair]
    _dep_operand(ins, in_specs, dep)
    return pl.pallas_call(
        body, grid=(T // tr, N // tc), in_specs=in_specs,
        out_specs=pair, out_shape=jax.ShapeDtypeStruct((2, T, N), BF16),
        compiler_params=_params(("parallel", "parallel")), name="glu_bwd")(*ins)


def _to_state_tiles(x_ref, s, val):
    tc = val.shape[0]
    for j in range(S5_W // 128):
        x_ref[s, pl.ds(j, tc, stride=8), :] = val[:, 128 * j:128 * (j + 1)]


def _from_state_tiles(x_ref, s, tc):
    return jnp.concatenate([x_ref[s, pl.ds(j, tc, stride=8), :] for j in range(S5_W // 128)], axis=1)


def _s5_scan_fwd(xr_ref, xi_ref, ar_ref, ai_ref, cr_ref, ci_ref, tc, nblk):
    a = [(ar_ref[s], ai_ref[s]) for s in range(nblk)]

    def step(i, carry):
        carry = list(carry)
        for uu in range(S5_UNROLL):
            r0 = pl.multiple_of((i * S5_UNROLL + uu) * 8, 8)
            for s in range(nblk):
                cr, ci = carry[2 * s], carry[2 * s + 1]
                a_r, a_i = a[s]
                xr = a_r * cr - a_i * ci + xr_ref[s, pl.ds(r0, 8), :]
                xi = a_r * ci + a_i * cr + xi_ref[s, pl.ds(r0, 8), :]
                xr_ref[s, pl.ds(r0, 8), :] = xr
                xi_ref[s, pl.ds(r0, 8), :] = xi
                carry[2 * s], carry[2 * s + 1] = xr, xi
        return tuple(carry)

    init = []
    for s in range(nblk):
        init += [cr_ref[s], ci_ref[s]]
    out = lax.fori_loop(0, tc // S5_UNROLL, step, tuple(init))
    for s in range(nblk):
        cr_ref[s] = out[2 * s]
        ci_ref[s] = out[2 * s + 1]


def _s5_scan_bwd(lr_ref, li_ref, xr_ref, xi_ref, h_ref, ar_ref, ai_ref, cr_ref, ci_ref,
                 accr_ref, acci_ref, tc, nblk):
    a = [(ar_ref[s], ai_ref[s]) for s in range(nblk)]

    def one(s, r0, prev_r, prev_i, st):
        c_r, c_i, d_r, d_i = st
        a_r, a_i = a[s]
        l_r = lr_ref[s, pl.ds(r0, 8), :] + a_r * c_r + a_i * c_i
        l_i = li_ref[s, pl.ds(r0, 8), :] + a_r * c_i - a_i * c_r
        lr_ref[s, pl.ds(r0, 8), :] = l_r
        li_ref[s, pl.ds(r0, 8), :] = l_i
        return [l_r, l_i, d_r + l_r * prev_r + l_i * prev_i, d_i - l_r * prev_i + l_i * prev_r]

    def step(i, carry):
        carry = list(carry)
        for uu in range(S5_UNROLL):
            t = tc - 1 - (i * S5_UNROLL + uu)
            r0 = pl.multiple_of(t * 8, 8)
            p0 = pl.multiple_of((t - 1) * 8, 8)
            for s in range(nblk):
                carry[4 * s:4 * s + 4] = one(s, r0, xr_ref[s, pl.ds(p0, 8), :], xi_ref[s, pl.ds(p0, 8), :],
                                             carry[4 * s:4 * s + 4])
        return tuple(carry)

    init = []
    for s in range(nblk):
        init += [cr_ref[s], ci_ref[s], accr_ref[s], acci_ref[s]]
    carry = list(lax.fori_loop(0, tc // S5_UNROLL - 1, step, tuple(init)))
    for t in range(S5_UNROLL - 1, -1, -1):
        for s in range(nblk):
            if t > 0:
                prev_r, prev_i = xr_ref[s, 8 * (t - 1):8 * t, :], xi_ref[s, 8 * (t - 1):8 * t, :]
            else:
                prev_r, prev_i = h_ref[0, s], h_ref[1, s]
            carry[4 * s:4 * s + 4] = one(s, 8 * t, prev_r, prev_i, carry[4 * s:4 * s + 4])
    for s in range(nblk):
        cr_ref[s], ci_ref[s], accr_ref[s], acci_ref[s] = carry[4 * s:4 * s + 4]


def _gelu(y):
    k = math.sqrt(2.0 / math.pi)
    return 0.5 * y * (1.0 + jnp.tanh(k * (y + 0.044715 * (y * y * y))))


def _gelu_grad(y):
    k = math.sqrt(2.0 / math.pi)
    t = jnp.tanh(k * (y + 0.044715 * (y * y * y)))
    return 0.5 * (1.0 + t) + 0.5 * y * (1.0 - t * t) * (k * (1.0 + 3.0 * 0.044715 * (y * y)))


def _s5_specs(tc, nch, sbk, rev):
    def ch(c):
        return nch - 1 - c if rev else c

    return dict(
        act=pl.BlockSpec((tc, sbk * S5_CH), lambda i, c: (ch(c), i)),
        bb=pl.BlockSpec((sbk, S5_CH, S5_W), lambda i, c: (i, 0, 0)),
        cc=pl.BlockSpec((sbk, S5_W, S5_CH), lambda i, c: (i, 0, 0)),
        a=pl.BlockSpec((sbk, 8, 128), lambda i, c: (i, 0, 0)),
        d=pl.BlockSpec((1, sbk * S5_CH), lambda i, c: (0, i)),
        h=pl.BlockSpec((None, 2, sbk, 8, 128), lambda i, c: (ch(c), 0, i, 0, 0)),
    )


def _s5_blocks(nb, pref):
    return max(b for b in range(1, pref + 1) if nb % b == 0)


def _s5_fwd(u, bbr, bbi, ccr, cci, a_re, a_im, d_skip):
    T, D = u.shape
    nb = D // S5_CH
    sbk = _s5_blocks(nb, 4)
    tc = _tile(T, 512, 8)
    nch = T // tc
    sp = _s5_specs(tc, nch, sbk, False)

    def body(u_ref, bbr_ref, bbi_ref, ccr_ref, cci_ref, ar_ref, ai_ref, d_ref, z_ref, h_ref,
             xr, xi, cr, ci):
        @pl.when(pl.program_id(1) == 0)
        def _():
            cr[...] = jnp.zeros_like(cr)
            ci[...] = jnp.zeros_like(ci)

        h_ref[0] = cr[...]
        h_ref[1] = ci[...]
        for s in range(sbk):
            ub = u_ref[:, s * S5_CH:(s + 1) * S5_CH].astype(BF16)
            _to_state_tiles(xr, s, _dot(ub, bbr_ref[s]))
            _to_state_tiles(xi, s, _dot(ub, bbi_ref[s]))
        _s5_scan_fwd(xr, xi, ar_ref, ai_ref, cr, ci, tc, sbk)
        for s in range(sbk):
            cols = slice(s * S5_CH, (s + 1) * S5_CH)
            y = (_dot(_from_state_tiles(xr, s, tc).astype(BF16), ccr_ref[s])
                 - _dot(_from_state_tiles(xi, s, tc).astype(BF16), cci_ref[s])
                 + d_ref[:, cols] * u_ref[:, cols])
            z_ref[:, cols] = _gelu(y).astype(BF16)

    tiles = pltpu.VMEM((sbk, tc * 8, 128), F32)
    carry = pltpu.VMEM((sbk, 8, 128), F32)
    return pl.pallas_call(
        body, grid=(nb // sbk, nch),
        in_specs=[sp["act"], sp["bb"], sp["bb"], sp["cc"], sp["cc"], sp["a"], sp["a"], sp["d"]],
        out_specs=[sp["act"], sp["h"]],
        out_shape=[jax.ShapeDtypeStruct((T, D), BF16), jax.ShapeDtypeStruct((nch, 2, nb, 8, 128), F32)],
        scratch_shapes=[tiles, tiles, carry, carry],
        compiler_params=_params(("parallel", "arbitrary")), name="s5_fwd",
    )(u, bbr, bbi, ccr, cci, a_re, a_im, d_skip)


def _s5_bwd(u, dz, h0, bbr, bbi, ccr, cci, a_re, a_im, d_skip):
    T, D = u.shape
    nb = D // S5_CH
    sbk = _s5_blocks(nb, 2)
    tc = _tile(T, 512, 8)
    nch = T // tc
    sp = _s5_specs(tc, nch, sbk, True)

    def body(u_ref, dz_ref, h_ref, bbr_ref, bbi_ref, ccr_ref, cci_ref, ar_ref, ai_ref, d_ref,
             du_ref, dd_ref, dar_ref, dai_ref, dbbr_ref, dbbi_ref, dccr_ref, dcci_ref,
             xr, xi, lr, li, fr, fi, br, bi, accr, acci):
        c = pl.program_id(1)

        @pl.when(c == 0)
        def _():
            for ref in (br, bi, accr, acci, dd_ref, dbbr_ref, dbbi_ref, dccr_ref, dcci_ref):
                ref[...] = jnp.zeros_like(ref)

        for s in range(sbk):
            ub = u_ref[:, s * S5_CH:(s + 1) * S5_CH].astype(BF16)
            _to_state_tiles(xr, s, _dot(ub, bbr_ref[s]))
            _to_state_tiles(xi, s, _dot(ub, bbi_ref[s]))
        fr[...] = h_ref[0]
        fi[...] = h_ref[1]
        _s5_scan_fwd(xr, xi, ar_ref, ai_ref, fr, fi, tc, sbk)
        for s in range(sbk):
            cols = slice(s * S5_CH, (s + 1) * S5_CH)
            uv = u_ref[:, cols]
            xrb = _from_state_tiles(xr, s, tc).astype(BF16)
            xib = _from_state_tiles(xi, s, tc).astype(BF16)
            dsk = d_ref[:, cols]
            y = _dot(xrb, ccr_ref[s]) - _dot(xib, cci_ref[s]) + dsk * uv
            dy = dz_ref[:, cols] * _gelu_grad(y)
            dd_ref[:, cols] += jnp.sum(dy * uv, axis=0, keepdims=True)
            dyb = dy.astype(BF16)
            dccr_ref[s] += _dot(xrb, dyb, TN)
            dcci_ref[s] += _dot(xib, dyb, TN)
            _to_state_tiles(lr, s, _dot(dyb, ccr_ref[s], NT))
            _to_state_tiles(li, s, -_dot(dyb, cci_ref[s], NT))
            du_ref[:, cols] = dy * dsk
        _s5_scan_bwd(lr, li, xr, xi, h_ref, ar_ref, ai_ref, br, bi, accr, acci, tc, sbk)
        for s in range(sbk):
            cols = slice(s * S5_CH, (s + 1) * S5_CH)
            ub = u_ref[:, cols].astype(BF16)
            lrb = _from_state_tiles(lr, s, tc).astype(BF16)
            lib = _from_state_tiles(li, s, tc).astype(BF16)
            dbbr_ref[s] += _dot(ub, lrb, TN)
            dbbi_ref[s] += _dot(ub, lib, TN)
            du_ref[:, cols] += _dot(lrb, bbr_ref[s], NT) + _dot(lib, bbi_ref[s], NT)

        @pl.when(c == nch - 1)
        def _():
            dar_ref[...] = accr[...]
            dai_ref[...] = acci[...]

    tiles = pltpu.VMEM((sbk, tc * 8, 128), F32)
    carry = pltpu.VMEM((sbk, 8, 128), F32)
    return pl.pallas_call(
        body, grid=(nb // sbk, nch),
        in_specs=[sp["act"], sp["act"], sp["h"], sp["bb"], sp["bb"], sp["cc"], sp["cc"],
                  sp["a"], sp["a"], sp["d"]],
        out_specs=[sp["act"], sp["d"], sp["a"], sp["a"], sp["bb"], sp["bb"], sp["cc"], sp["cc"]],
        out_shape=[jax.ShapeDtypeStruct((T, D), F32), jax.ShapeDtypeStruct((1, D), F32),
                   jax.ShapeDtypeStruct((nb, 8, 128), F32), jax.ShapeDtypeStruct((nb, 8, 128), F32),
                   jax.ShapeDtypeStruct((nb, S5_CH, S5_W), F32), jax.ShapeDtypeStruct((nb, S5_CH, S5_W), F32),
                   jax.ShapeDtypeStruct((nb, S5_W, S5_CH), F32), jax.ShapeDtypeStruct((nb, S5_W, S5_CH), F32)],
        scratch_shapes=[tiles, tiles, tiles, tiles, carry, carry, carry, carry, carry, carry],
        compiler_params=_params(("parallel", "arbitrary")), name="s5_bwd",
    )(u, dz, h0, bbr, bbi, ccr, cci, a_re, a_im, d_skip)


def _s5_disc(lr, li, ldt):
    dt = jnp.exp(ldt)
    mag = jnp.exp(lr * dt)
    ang = li * dt
    cs, sn = jnp.cos(ang), jnp.sin(ang)
    lbr, lbi = mag * cs, mag * sn
    nr = lbr - 1.0
    den = lr * lr + li * li
    f_re = (nr * lr + lbi * li) / den
    f_im = (lbi * lr - nr * li) / den
    return dt, mag, cs, sn, lbr, lbi, nr, den, f_re, f_im


def _s5_param_fwd(lr, li, ldt, bt_re, bt_im):
    c, g, p = bt_re.shape

    def body(lr_ref, li_ref, ldt_ref, br_ref, bi_ref, lbr_ref, lbi_ref, bbr_ref, bbi_ref):
        _, _, _, _, lbr, lbi, _, _, f_re, f_im = _s5_disc(lr_ref[...], li_ref[...], ldt_ref[...])
        lbr_ref[...] = lbr
        lbi_ref[...] = lbi
        for ch in range(c):
            b_r, b_i = br_ref[ch], bi_ref[ch]
            bbr_ref[ch] = f_re * b_r - f_im * b_i
            bbi_ref[ch] = f_re * b_i + f_im * b_r

    gp = jax.ShapeDtypeStruct((g, p), F32)
    cgp = jax.ShapeDtypeStruct((c, g, p), F32)
    return pl.pallas_call(body, out_shape=[gp, gp, cgp, cgp], name="s5_param_fwd")(lr, li, ldt, bt_re, bt_im)


def _s5_param_bwd(lr, li, ldt, bt_re, bt_im, dlbr, dlbi, dbbr, dbbi):
    c, g, p = bt_re.shape

    def body(lr_ref, li_ref, ldt_ref, br_ref, bi_ref, dlbr_ref, dlbi_ref, dbbr_ref, dbbi_ref,
             dlr_ref, dli_ref, dldt_ref, dbr_ref, dbi_ref):
        l_r, l_i = lr_ref[...], li_ref[...]
        dt, mag, cs, sn, lbr, lbi, nr, den, f_re, f_im = _s5_disc(l_r, l_i, ldt_ref[...])
        dfr = jnp.zeros_like(l_r)
        dfi = jnp.zeros_like(l_r)
        for ch in range(c):
            b_r, b_i = br_ref[ch], bi_ref[ch]
            g_r, g_i = dbbr_ref[ch], dbbi_ref[ch]
            dbr_ref[ch] = f_re * g_r + f_im * g_i
            dbi_ref[ch] = f_re * g_i - f_im * g_r
            dfr = dfr + g_r * b_r + g_i * b_i
            dfi = dfi + g_i * b_r - g_r * b_i
        inv = 1.0 / den
        d_nr = (dfr * l_r - dfi * l_i) * inv
        d_lbi = (dfr * l_i + dfi * l_r) * inv + dlbi_ref[...]
        d_lbr = d_nr + dlbr_ref[...]
        d_den = -(dfr * f_re + dfi * f_im) * inv
        d_mag = d_lbr * cs + d_lbi * sn
        d_ang = d_lbi * lbr - d_lbr * lbi
        dlr_ref[...] = (dfr * nr + dfi * lbi) * inv + 2.0 * d_den * l_r + d_mag * mag * dt
        dli_ref[...] = (dfr * lbi - dfi * nr) * inv + 2.0 * d_den * l_i + d_ang * dt
        dldt_ref[...] = jnp.sum(d_mag * mag * l_r + d_ang * l_i, axis=1, keepdims=True) * dt

    gp = jax.ShapeDtypeStruct((g, p), F32)
    cgp = jax.ShapeDtypeStruct((c, g, p), F32)
    return pl.pallas_call(body, out_shape=[gp, gp, jax.ShapeDtypeStruct((g, 1), F32), cgp, cgp],
                          name="s5_param_bwd")(lr, li, ldt, bt_re, bt_im, dlbr, dlbi, dbbr, dbbi)


def _block_diag(t, inner_first):
    g, c, p = t.shape
    nb = g // S5_GB
    t4 = t.reshape(nb, S5_GB, c, p)
    eye = jnp.eye(S5_GB, dtype=t.dtype)
    if inner_first:
        e = t4[:, :, :, None, :] * eye[None, :, None, :, None]
        return e.reshape(nb, S5_GB * c, S5_GB * p)
    t4 = t4.transpose(0, 1, 3, 2)
    e = t4[:, :, :, None, :] * eye[None, :, None, :, None]
    return e.reshape(nb, S5_GB * p, S5_GB * c)


def _block_diag_extract(e, inner_first, c, p):
    nb = e.shape[0]
    eye = jnp.eye(S5_GB, dtype=e.dtype)
    if inner_first:
        e5 = e.reshape(nb, S5_GB, c, S5_GB, p)
        return (e5 * eye[None, :, None, :, None]).sum(3).reshape(nb * S5_GB, c, p)
    e5 = e.reshape(nb, S5_GB, p, S5_GB, c)
    return (e5 * eye[None, :, None, :, None]).sum(3).transpose(0, 1, 3, 2).reshape(nb * S5_GB, c, p)


def _att_masks(rep, gb):
    rows = rep * ATT_BLK
    qi = lax.broadcasted_iota(jnp.int32, (rows, 2 * ATT_BLK), 0) % ATT_BLK
    si = lax.broadcasted_iota(jnp.int32, (rows, 2 * ATT_BLK), 1)
    prev = (si < ATT_BLK) & (si >= qi) & (gb > 0)
    cur = (si >= ATT_BLK) & (si - ATT_BLK <= qi)
    return prev | cur


def _stack_heads(ref, rows, rep):
    return jnp.concatenate([ref[rows, j * HEAD_DIM:(j + 1) * HEAD_DIM] for j in range(rep)], axis=0)


def _att_geometry(T, H, dil):
    n = T // dil
    rep = H // N_KV_HEADS
    rc = _tile(n, 1024, ATT_BLK)
    return n, rep, rc, n // rc, rc // ATT_BLK


def _att_fwd(q, k, v, grp, dil):
    T = q.shape[0]
    H = q.shape[1] // HEAD_DIM // len(DILATIONS)
    n, rep, rc, nrc, nblk = _att_geometry(T, H, dil)
    hw = rep * HEAD_DIM
    scale = HEAD_DIM ** -0.5
    q2 = q.reshape(n, dil * q.shape[1])
    k2 = k.reshape(n, dil * k.shape[1])
    v2 = v.reshape(n, dil * v.shape[1])
    qcols = q.shape[1] // hw
    ocols = H * HEAD_DIM // hw

    def body(q_ref, k_ref, v_ref, o_ref, l_ref):
        ch = pl.program_id(2)

        def blk(b, _):
            gb = ch * nblk + b
            rows = pl.ds(pl.multiple_of(b * ATT_BLK, ATT_BLK), ATT_BLK)
            kc = pl.ds(pl.multiple_of(gb * ATT_BLK, ATT_BLK), ATT_BLK)
            kp = pl.ds(pl.multiple_of(jnp.maximum(gb - 1, 0) * ATT_BLK, ATT_BLK), ATT_BLK)
            qs = _stack_heads(q_ref, rows, rep)
            kcat = jnp.concatenate([k_ref[kp, :], k_ref[kc, :]], axis=0)
            vcat = jnp.concatenate([v_ref[kp, :], v_ref[kc, :]], axis=0)
            s = jnp.where(_att_masks(rep, gb), _dot(qs, kcat, NT) * scale, NEG_INF)
            m = jnp.max(s, axis=-1, keepdims=True)
            p = jnp.exp(s - m)
            l = jnp.sum(p, axis=-1, keepdims=True)
            o = _dot(p.astype(BF16), vcat) / l
            lse = jnp.broadcast_to(m + jnp.log(l), (rep * ATT_BLK, HEAD_DIM))
            for j in range(rep):
                o_ref[rows, j * HEAD_DIM:(j + 1) * HEAD_DIM] = o[j * ATT_BLK:(j + 1) * ATT_BLK]
                l_ref[rows, j * HEAD_DIM:(j + 1) * HEAD_DIM] = lse[j * ATT_BLK:(j + 1) * ATT_BLK]
            return 0

        lax.fori_loop(0, nblk, blk, 0)

    qspec = pl.BlockSpec((rc, hw), lambda r, h, c: (c, r * qcols + grp * N_KV_HEADS + h))
    kspec = pl.BlockSpec((n, HEAD_DIM), lambda r, h, c: (0, r * N_KV_HEADS + h))
    ospec = pl.BlockSpec((rc, hw), lambda r, h, c: (c, r * ocols + h))
    o, l = pl.pallas_call(
        body, grid=(dil, N_KV_HEADS, nrc), in_specs=[qspec, kspec, kspec], out_specs=[ospec, ospec],
        out_shape=[jax.ShapeDtypeStruct((n, dil * H * HEAD_DIM), F32)] * 2,
        compiler_params=_params(("parallel", "parallel", "arbitrary")), name=f"att_fwd_d{dil}",
    )(q2, k2, v2)
    return o.reshape(T, H * HEAD_DIM), l.reshape(T, H * HEAD_DIM)


def _att_combine(outs, lses):
    T, W = outs[0].shape
    ng = len(outs)
    tr, tcol = _tile(T, 512, 8), _tile(W, 512)

    def body(*refs):
        o_refs, l_refs = refs[:ng], refs[ng:2 * ng]
        ob_ref, lse_ref = refs[2 * ng:]
        ls = [r[...] for r in l_refs]
        m = functools.reduce(jnp.maximum, ls)
        es = [jnp.exp(l - m) for l in ls]
        den = functools.reduce(lambda a, b: a + b, es)
        num = functools.reduce(lambda a, b: a + b, [e * o[...] for e, o in zip(es, o_refs)])
        ob_ref[...] = (num / den).astype(BF16)
        lse_ref[...] = m + jnp.log(den)

    spec = pl.BlockSpec((tr, tcol), lambda i, j: (i, j))
    return pl.pallas_call(
        body, grid=(T // tr, W // tcol), in_specs=[spec] * (2 * ng), out_specs=[spec, spec],
        out_shape=[jax.ShapeDtypeStruct((T, W), BF16), jax.ShapeDtypeStruct((T, W), F32)],
        compiler_params=_params(("parallel", "parallel")), name="att_combine")(*outs, *lses)


def _att_bwd(q, k, v, o, do, lse, grp, dil):
    T = q.shape[0]
    H = o.shape[1] // HEAD_DIM
    n, rep, rc, nrc, nblk = _att_geometry(T, H, dil)
    hw = rep * HEAD_DIM
    scale = HEAD_DIM ** -0.5
    q2 = q.reshape(n, dil * q.shape[1])
    k2 = k.reshape(n, dil * k.shape[1])
    v2 = v.reshape(n, dil * v.shape[1])
    o2 = o.reshape(n, dil * o.shape[1])
    do2 = do.reshape(n, dil * do.shape[1])
    l2 = lse.reshape(n, dil * lse.shape[1])
    qcols = q.shape[1] // hw
    ocols = H * HEAD_DIM // hw

    def body(q_ref, k_ref, v_ref, o_ref, do_ref, l_ref, dq_ref, dk_ref, dv_ref):
        ch = pl.program_id(2)

        @pl.when(ch == 0)
        def _():
            dk_ref[...] = jnp.zeros_like(dk_ref)
            dv_ref[...] = jnp.zeros_like(dv_ref)

        def blk(b, _):
            gb = ch * nblk + b
            rows = pl.ds(pl.multiple_of(b * ATT_BLK, ATT_BLK), ATT_BLK)
            kc = pl.ds(pl.multiple_of(gb * ATT_BLK, ATT_BLK), ATT_BLK)
            kp = pl.ds(pl.multiple_of(jnp.maximum(gb - 1, 0) * ATT_BLK, ATT_BLK), ATT_BLK)
            qs = _stack_heads(q_ref, rows, rep)
            dos = _stack_heads(do_ref, rows, rep)
            os_ = _stack_heads(o_ref, rows, rep)
            lss = _stack_heads(l_ref, rows, rep)
            kcat = jnp.concatenate([k_ref[kp, :], k_ref[kc, :]], axis=0)
            vcat = jnp.concatenate([v_ref[kp, :], v_ref[kc, :]], axis=0)
            s = _dot(qs, kcat, NT) * scale
            p = jnp.where(_att_masks(rep, gb), jnp.exp(s - jnp.concatenate([lss, lss], axis=1)), 0.0)
            delta = jnp.sum(dos.astype(F32) * os_.astype(F32), axis=-1, keepdims=True)
            dp = _dot(dos, vcat, NT)
            ds = (p * (dp - delta) * scale).astype(BF16)
            dvc = _dot(p.astype(BF16), dos, TN)
            dkc = _dot(ds, qs, TN)
            dqs = _dot(ds, kcat)
            for j in range(rep):
                dq_ref[rows, j * HEAD_DIM:(j + 1) * HEAD_DIM] = dqs[j * ATT_BLK:(j + 1) * ATT_BLK].astype(BF16)
            dk_ref[kc, :] += dkc[ATT_BLK:]
            dv_ref[kc, :] += dvc[ATT_BLK:]

            @pl.when(gb > 0)
            def _():
                dk_ref[kp, :] += dkc[:ATT_BLK]
                dv_ref[kp, :] += dvc[:ATT_BLK]

            return 0

        lax.fori_loop(0, nblk, blk, 0)

    qspec = pl.BlockSpec((rc, hw), lambda r, h, c: (c, r * qcols + grp * N_KV_HEADS + h))
    kspec = pl.BlockSpec((n, HEAD_DIM), lambda r, h, c: (0, r * N_KV_HEADS + h))
    ospec = pl.BlockSpec((rc, hw), lambda r, h, c: (c, r * ocols + h))
    dq, dk, dv = pl.pallas_call(
        body, grid=(dil, N_KV_HEADS, nrc), in_specs=[qspec, kspec, kspec, ospec, ospec, ospec],
        out_specs=[ospec, kspec, kspec],
        out_shape=[jax.ShapeDtypeStruct((n, dil * H * HEAD_DIM), BF16),
                   jax.ShapeDtypeStruct((n, dil * N_KV_HEADS * HEAD_DIM), F32),
                   jax.ShapeDtypeStruct((n, dil * N_KV_HEADS * HEAD_DIM), F32)],
        compiler_params=_params(("parallel", "parallel", "arbitrary")), name=f"att_bwd_d{dil}",
    )(q2, k2, v2, o2, do2, l2)
    return (dq.reshape(T, H * HEAD_DIM), dk.reshape(T, N_KV_HEADS * HEAD_DIM),
            dv.reshape(T, N_KV_HEADS * HEAD_DIM))


def _sum_kv(dks, dvs):
    T, W = dks[0].shape
    ng = len(dks)
    tr = _tile(T, 512, 8)

    def body(*refs):
        o_ref = refs[2 * ng]
        o_ref[0] = functools.reduce(lambda a, b: a + b, [r[...] for r in refs[:ng]]).astype(BF16)
        o_ref[1] = functools.reduce(lambda a, b: a + b, [r[...] for r in refs[ng:2 * ng]]).astype(BF16)

    spec = pl.BlockSpec((tr, W), lambda i: (i, 0))
    return pl.pallas_call(
        body, grid=(T // tr,), in_specs=[spec] * (2 * ng),
        out_specs=pl.BlockSpec((2, tr, W), lambda i: (0, i, 0)),
        out_shape=jax.ShapeDtypeStruct((2, T, W), BF16),
        compiler_params=_params(("parallel",)), name="sum_kv")(*dks, *dvs)


def _local_step(x, tgt, w, wts, ready, dep0=None):
    T, D = x.shape
    g = {}

    (u0,) = _rms_fwd("rms_a", x, [w["a_norm"]], [F32], dep=dep0)
    lbr, lbi, bbt_re, bbt_im = _s5_param_fwd(w["lam_re"], w["lam_im"], w["log_dt"], w["bt_re"], w["bt_im"])
    a_re, a_im = lbr.reshape(-1, 8, 128), lbi.reshape(-1, 8, 128)
    bbr = _block_diag(bbt_re.transpose(1, 0, 2), True).astype(BF16)
    bbi = _block_diag(bbt_im.transpose(1, 0, 2), True).astype(BF16)
    ccr = _block_diag(w["c_re"], False).astype(BF16)
    cci = _block_diag(w["c_im"], False).astype(BF16)
    z, h0 = _s5_fwd(u0, bbr, bbi, ccr, cci, a_re, a_im, w["s5_d"])
    w_glu = wts("glu", z)
    x1, vg = _mm_dual_fwd("glu_fwd", z, w_glu, x, "glu")

    def ffn_fwd(xin, layer):
        (nrm,) = _rms_fwd(f"rms_f{layer}", xin, [w["ffn_norm"][layer:layer + 1]], [BF16])
        w_in = wts(f"win{layer}", nrm)
        gu, act = _mm_dual_fwd(f"ffn_in{layer}", nrm, w_in, None, "ffn")
        w_out = wts(f"wout{layer}", act)
        xout = _mm_nn(f"ffn_out{layer}", act, w_out, res=xin)
        return xout, (nrm, gu, act, w_in, w_out)

    x2, saved0 = ffn_fwd(x1, 0)
    kvn, hb = _rms_fwd("rms_b", x2, [w["kv_norm"], w["b_norm"]], [BF16, BF16])
    w_kv, w_q, w_o = wts("wkv", hb), wts("wq", hb), wts("wo", hb)
    k, v = _mm_kv("kv_proj", kvn, w_kv)
    q = _mm_nn("q_proj", hb, w_q, out_dtype=BF16)
    outs, lses = [], []
    for grp, dil in enumerate(DILATIONS):
        o_g, l_g = _att_fwd(q, k, v, grp, dil)
        outs.append(o_g)
        lses.append(l_g)
    o, lse = _att_combine(outs, lses)
    x3 = _mm_nn("o_proj", o, w_o, res=x2)
    x4, saved1 = ffn_fwd(x3, 1)
    loss_blk, dx4, dx4b, g["final_norm"] = _loss_head(x4, w["final_norm"], tgt)

    def ffn_bwd(dx, dxb, xin, saved, layer, dep):
        nrm, gu, act, w_in, w_out = saved
        dgu = _mm_nt_ffn_bwd(f"ffn_dact{layer}", dxb, w_out, gu, dep=dep)
        g_wout = _mm_tn(f"ffn_dwout{layer}", act, dxb)
        g_win = _mm_tn_pair(f"ffn_dwin{layer}", nrm, dgu)
        dn = _mm_nt_pair(f"ffn_dn{layer}", dgu, w_in)
        dxo, dxob, (dgn,) = _rms_bwd(f"rms_f{layer}_bwd", xin, dx, [w["ffn_norm"][layer:layer + 1]], [dn])
        tok = ready({f"win{layer}": g_win, f"wout{layer}": g_wout})
        return dxo, dxob, dgn, tok

    dx3, dx3b, dfn1, tok = ffn_bwd(dx4, dx4b, x3, saved1, 1, None)
    do = _mm_nt("o_proj_dx", [dx3b], w_o, out_dtype=BF16, dep=tok)
    g_wo = _mm_tn("o_proj_dw", o, dx3b)
    dqs, dks, dvs = [], [], []
    for grp, dil in enumerate(DILATIONS):
        dq_g, dk_g, dv_g = _att_bwd(q, k, v, o, do, lse, grp, dil)
        dqs.append(dq_g)
        dks.append(dk_g)
        dvs.append(dv_g)
    dkv = _sum_kv(dks, dvs)
    dhb = _mm_nt("q_proj_dx", dqs, w_q)
    g_wq = _mm_tn("q_proj_dw", hb, jnp.concatenate(dqs, axis=1))
    dkvn = _mm_nt_pair("kv_proj_dx", dkv, w_kv)
    g_wkv = _mm_tn_pair("kv_proj_dw", kvn, dkv)
    dx2, dx2b, (g["kv_norm"], g["b_norm"]) = _rms_bwd(
        "rms_b_bwd", x2, dx3, [w["kv_norm"], w["b_norm"]], [dkvn, dhb])
    tok = ready({"wkv": g_wkv, "wq": g_wq, "wo": g_wo})
    dx1, dx1b, dfn0, tok = ffn_bwd(dx2, dx2b, x1, saved0, 0, tok)
    g["ffn_norm"] = jnp.concatenate([dfn0, dfn1], axis=0)

    dvg = _glu_bwd(dx1, vg, dep=tok)
    dz = _mm_nt_pair("glu_dx", dvg, w_glu)
    ready({"glu": _mm_tn_pair("glu_dw", z, dvg)})
    du, g["s5_d"], da_re, da_im, dbbr, dbbi, dccr, dcci = _s5_bwd(
        u0, dz, h0, bbr, bbi, ccr, cci, a_re, a_im, w["s5_d"])
    C, G, P = w["bt_re"].shape
    g["c_re"] = _block_diag_extract(dccr, False, C, P)
    g["c_im"] = -_block_diag_extract(dcci, False, C, P)
    g["lam_re"], g["lam_im"], g["log_dt"], g["bt_re"], g["bt_im"] = _s5_param_bwd(
        w["lam_re"], w["lam_im"], w["log_dt"], w["bt_re"], w["bt_im"],
        da_re.reshape(G, P), da_im.reshape(G, P),
        _block_diag_extract(dbbr, True, C, P).transpose(1, 0, 2),
        _block_diag_extract(dbbi, True, C, P).transpose(1, 0, 2))
    grad_x, _, (g["a_norm"],) = _rms_bwd("rms_a_bwd", x, dx1, [w["a_norm"]], [du])
    return loss_blk, grad_x, g


def _coords():
    return lax.axis_index("x"), lax.axis_index("y"), lax.axis_index("c")


def _dev_index(dev):
    return 4 * dev[0] + 2 * dev[1] + dev[2]


def _shard_window(ref, axis, width, idx):
    sl = [slice(None)] * len(ref.shape)
    sl[axis] = pl.ds(pl.multiple_of(idx * width, width), width)
    return ref.at[tuple(sl)]


def _all_gather(name, shards, axes):
    na = len(shards)
    widths = [s.shape[ax] for s, ax in zip(shards, axes)]
    out_shapes = []
    for s, ax in zip(shards, axes):
        shp = list(s.shape)
        shp[ax] *= N_DEV
        out_shapes.append(jax.ShapeDtypeStruct(tuple(shp), s.dtype))

    def body(*refs):
        ins, outs = refs[:na], refs[na:2 * na]
        send_sems, recv_sems, local_sems = refs[2 * na:]
        x, y, c = _coords()
        me, sib = (x, y, c), (x, y, 1 - c)
        chips = [(1 - x, y), (x, 1 - y), (1 - x, 1 - y)]

        def blk(a, dev):
            return _shard_window(outs[a], axes[a], widths[a], _dev_index(dev))

        def copy(a, kk, block, to, src=None):
            return pltpu.make_async_remote_copy(
                src_ref=blk(a, block) if src is None else src, dst_ref=blk(a, block),
                send_sem=send_sems.at[a, kk], recv_sem=recv_sems.at[a, kk],
                device_id=to, device_id_type=MESH)

        local = [pltpu.make_async_copy(ins[a], blk(a, me), local_sems.at[a]) for a in range(na)]
        for cp in local:
            cp.start()
        sent = []
        for a in range(na):
            first = [copy(a, 0, me, sib, src=ins[a])]
            first += [copy(a, 1 + j, me, (*chip, c), src=ins[a]) for j, chip in enumerate(chips)]
            for cp in first:
                cp.start()
            sent += first
        for a in range(na):
            for j, chip in enumerate(chips):
                copy(a, 1 + j, (*chip, c), me).wait_recv()
                fwd = copy(a, 4 + j, (*chip, c), sib)
                fwd.start()
                sent.append(fwd)
        for a in range(na):
            copy(a, 0, sib, me).wait_recv()
            for j, chip in enumerate(chips):
                copy(a, 4 + j, (*chip, 1 - c), me).wait_recv()
        for cp in sent:
            cp.wait_send()
        for cp in local:
            cp.wait()

    return pl.pallas_call(
        body, out_shape=out_shapes, in_specs=[ANY] * na, out_specs=[ANY] * na,
        scratch_shapes=[pltpu.SemaphoreType.DMA((na, 7)), pltpu.SemaphoreType.DMA((na, 7)),
                        pltpu.SemaphoreType.DMA((na,))],
        name=name)(*shards)


HBM = pl.BlockSpec(memory_space=pltpu.HBM)
SEM = pl.BlockSpec(memory_space=pltpu.SEMAPHORE)
TOKEN_SPEC = pl.BlockSpec(memory_space=pltpu.VMEM)
TOKEN_SHAPE = jax.ShapeDtypeStruct((8, 128), F32)
SPLIT_PARAMS = pltpu.CompilerParams(has_side_effects=pltpu.SideEffectType.DATAFLOW_SIDE_EFFECTING)


def _hbm(x):
    return pltpu.with_memory_space_constraint(x, pltpu.HBM)


def _hbm_like(x):
    return pltpu.HBM(x.shape, x.dtype)


def _dma_sems(*shape):
    return pltpu.SemaphoreType.DMA(shape)


def _place_own(shards, axes):
    na = len(shards)
    widths = [s.shape[ax] for s, ax in zip(shards, axes)]
    out_shapes = []
    for s, ax in zip(shards, axes):
        shp = list(s.shape)
        shp[ax] *= N_DEV
        out_shapes.append(jax.ShapeDtypeStruct(tuple(shp), s.dtype))

    def body(*refs):
        ins, outs, sems = refs[:na], refs[na:2 * na], refs[2 * na]
        me = _dev_index(_coords())
        cps = [pltpu.make_async_copy(ins[a], _shard_window(outs[a], axes[a], widths[a], me), sems.at[a])
               for a in range(na)]
        for cp in cps:
            cp.start()
        for cp in cps:
            cp.wait()

    return pl.pallas_call(
        body, out_shape=out_shapes, in_specs=[ANY] * na, out_specs=[ANY] * na,
        scratch_shapes=[_dma_sems(na)], name="place_own_shards")(*shards)


def _gather_start(shards, lands, axes, groups):
    na, ng = len(shards), len(groups)
    widths = [s.shape[ax] for s, ax in zip(shards, axes)]

    def body(*refs):
        sh, ld = refs[:na], refs[na:2 * na]
        sems = refs[2 * na:2 * na + 3 * ng]
        token = refs[-1]
        x, y, c = _coords()
        me, sib = (x, y, c), (x, y, 1 - c)
        chips = [(1 - x, y), (x, 1 - y), (1 - x, 1 - y)]
        for gi, grp in enumerate(groups):
            send, r_d2d, r_ici = sems[3 * gi:3 * gi + 3]
            for li, a in enumerate(grp):
                dst = _shard_window(ld[a], axes[a], widths[a], _dev_index(me))
                pltpu.make_async_remote_copy(
                    src_ref=sh[a], dst_ref=dst, send_sem=send.at[4 * li], recv_sem=r_d2d.at[li],
                    device_id=sib, device_id_type=MESH).start()
                for j, chip in enumerate(chips):
                    pltpu.make_async_remote_copy(
                        src_ref=sh[a], dst_ref=dst, send_sem=send.at[4 * li + 1 + j], recv_sem=r_ici.at[3 * li + j],
                        device_id=(*chip, c), device_id_type=MESH).start()
        token[...] = jnp.zeros_like(token)

    out_shape, out_specs = [], []
    for grp in groups:
        out_shape += [_dma_sems(4 * len(grp)), _dma_sems(len(grp)), _dma_sems(3 * len(grp))]
        out_specs += [SEM] * 3
    out_shape += [_hbm_like(s) for s in shards] + [_hbm_like(l) for l in lands] + [TOKEN_SHAPE]
    out_specs += [HBM] * (2 * na) + [TOKEN_SPEC]
    aliases = {a: 3 * ng + a for a in range(2 * na)}
    res = pl.pallas_call(
        body, name="weights_gather_start", out_shape=out_shape, in_specs=[HBM] * (2 * na),
        out_specs=out_specs, input_output_aliases=aliases, compiler_params=SPLIT_PARAMS,
    )(*[_hbm(s) for s in shards], *[_hbm(l) for l in lands])
    sems = [tuple(res[3 * gi:3 * gi + 3]) for gi in range(ng)]
    return sems, list(res[3 * ng:3 * ng + na]), list(res[3 * ng + na:3 * ng + 2 * na]), res[-1]


def _gather_forward(name, lands, axes, r_ici, after):
    n = len(lands)
    widths = [l.shape[ax] // N_DEV for l, ax in zip(lands, axes)]

    def body(*refs):
        ld, r_ici_ref = refs[:n], refs[n]
        f_send, f_recv = refs[n + 2], refs[n + 3]
        x, y, c = _coords()
        sib = (x, y, 1 - c)
        chips = [(1 - x, y), (x, 1 - y), (1 - x, 1 - y)]
        for li in range(n):
            for j, chip in enumerate(chips):
                blk = _shard_window(ld[li], axes[li], widths[li], _dev_index((*chip, c)))
                pltpu.make_async_remote_copy(
                    src_ref=blk, dst_ref=blk, send_sem=f_send.at[3 * li + j], recv_sem=r_ici_ref.at[3 * li + j],
                    device_id=(*chip, c), device_id_type=MESH).wait_recv()
                pltpu.make_async_remote_copy(
                    src_ref=blk, dst_ref=blk, send_sem=f_send.at[3 * li + j], recv_sem=f_recv.at[3 * li + j],
                    device_id=sib, device_id_type=MESH).start()

    res = pl.pallas_call(
        body, name=name, out_shape=[_dma_sems(3 * n), _dma_sems(3 * n)] + [_hbm_like(l) for l in lands],
        in_specs=[HBM] * n + [SEM, ANY], out_specs=[SEM, SEM] + [HBM] * n,
        input_output_aliases={li: 2 + li for li in range(n)}, compiler_params=SPLIT_PARAMS,
    )(*lands, r_ici, after)
    return res[0], res[1], list(res[2:])


def _gather_finish(name, shards, lands, axes, send, r_d2d, f_send, f_recv):
    n = len(lands)
    widths = [l.shape[ax] // N_DEV for l, ax in zip(lands, axes)]

    def body(*refs):
        sh, ld = refs[:n], refs[n:2 * n]
        send_ref, r_d2d_ref, f_send_ref, f_recv_ref = refs[2 * n:2 * n + 4]
        x, y, c = _coords()
        me, sib = (x, y, c), (x, y, 1 - c)
        chips = [(1 - x, y), (x, 1 - y), (1 - x, 1 - y)]

        def blk(li, dev):
            return _shard_window(ld[li], axes[li], widths[li], _dev_index(dev))

        for li in range(n):
            for kk in range(4):
                pltpu.make_async_remote_copy(
                    src_ref=sh[li], dst_ref=blk(li, me), send_sem=send_ref.at[4 * li + kk], recv_sem=r_d2d_ref.at[li],
                    device_id=sib, device_id_type=MESH).wait_send()
            pltpu.make_async_remote_copy(
                src_ref=blk(li, sib), dst_ref=blk(li, sib), send_sem=send_ref.at[4 * li], recv_sem=r_d2d_ref.at[li],
                device_id=sib, device_id_type=MESH).wait_recv()
            for j, chip in enumerate(chips):
                pltpu.make_async_remote_copy(
                    src_ref=blk(li, (*chip, c)), dst_ref=blk(li, (*chip, c)), send_sem=f_send_ref.at[3 * li + j],
                    recv_sem=f_recv_ref.at[3 * li + j], device_id=sib, device_id_type=MESH).wait_send()
                pltpu.make_async_remote_copy(
                    src_ref=blk(li, (*chip, 1 - c)), dst_ref=blk(li, (*chip, 1 - c)), send_sem=f_send_ref.at[3 * li + j],
                    recv_sem=f_recv_ref.at[3 * li + j], device_id=sib, device_id_type=MESH).wait_recv()

    res = pl.pallas_call(
        body, name=name, out_shape=[_hbm_like(s) for s in shards] + [_hbm_like(l) for l in lands],
        in_specs=[HBM] * (2 * n) + [SEM] * 4, out_specs=[HBM] * (2 * n),
        input_output_aliases={i: i for i in range(2 * n)}, compiler_params=SPLIT_PARAMS,
    )(*shards, *lands, send, r_d2d, f_send, f_recv)
    return list(res[n:])


def _chip_exchange_start(name, parts):
    n = len(parts)

    def body(*refs):
        src, ld = refs[:n], refs[n:2 * n]
        send, recv = refs[2 * n], refs[2 * n + 1]
        token = refs[-1]
        x, y, c = _coords()
        chips = [(1 - x, y), (x, 1 - y), (1 - x, 1 - y)]
        for li in range(n):
            for kk, chip in enumerate(chips):
                pltpu.make_async_remote_copy(
                    src_ref=src[li].at[kk], dst_ref=ld[li].at[kk], send_sem=send.at[3 * li + kk],
                    recv_sem=recv.at[3 * li + kk], device_id=(*chip, c), device_id_type=MESH).start()
        token[...] = jnp.zeros_like(token)

    lands = [lax.empty(p.shape, p.dtype) for p in parts]
    res = pl.pallas_call(
        body, name=name,
        out_shape=[_dma_sems(3 * n), _dma_sems(3 * n)] + [_hbm_like(p) for p in parts] * 2 + [TOKEN_SHAPE],
        in_specs=[HBM] * (2 * n), out_specs=[SEM, SEM] + [HBM] * (2 * n) + [TOKEN_SPEC],
        input_output_aliases={i: 2 + i for i in range(2 * n)}, compiler_params=SPLIT_PARAMS,
    )(*[_hbm(p) for p in parts], *[_hbm(l) for l in lands])
    return res[0], res[1], list(res[2:2 + n]), list(res[2 + n:2 + 2 * n]), res[-1]


def _chip_exchange_finish(name, started, after):
    counts = [len(st[2]) for st in started]
    total = sum(counts)
    ns = len(started)

    def body(*refs):
        src, ld = refs[:total], refs[total:2 * total]
        sems = refs[2 * total:2 * total + 2 * ns]
        x, y, c = _coords()
        chips = [(1 - x, y), (x, 1 - y), (1 - x, 1 - y)]
        off = 0
        for si, cnt in enumerate(counts):
            send, recv = sems[2 * si], sems[2 * si + 1]
            for li in range(cnt):
                for kk, chip in enumerate(chips):
                    cp = pltpu.make_async_remote_copy(
                        src_ref=src[off + li].at[kk], dst_ref=ld[off + li].at[kk], send_sem=send.at[3 * li + kk],
                        recv_sem=recv.at[3 * li + kk], device_id=(*chip, c), device_id_type=MESH)
                    cp.wait_send()
                    cp.wait_recv()
            off += cnt

    srcs = [p for st in started for p in st[2]]
    lands = [l for st in started for l in st[3]]
    sems = [s for st in started for s in st[:2]]
    res = pl.pallas_call(
        body, name=name, out_shape=[_hbm_like(p) for p in srcs + lands],
        in_specs=[HBM] * (2 * total) + [SEM] * (2 * ns) + [ANY], out_specs=[HBM] * (2 * total),
        input_output_aliases={i: i for i in range(2 * total)}, compiler_params=SPLIT_PARAMS,
    )(*srcs, *lands, *sems, after)
    out, off = [], total
    for cnt in counts:
        out.append(list(res[off:off + cnt]))
        off += cnt
    return out


def _pair_exchange(name, grads, axes):
    na = len(grads)
    widths = [gr.shape[ax] // N_DEV for gr, ax in zip(grads, axes)]
    out_shapes = []
    for gr, ax, wd in zip(grads, axes, widths):
        shp = list(gr.shape)
        shp[ax] = wd
        out_shapes.append(jax.ShapeDtypeStruct((4, *shp), gr.dtype))

    def body(*refs):
        ins, outs = refs[:na], refs[na:2 * na]
        send_sems, recv_sems = refs[2 * na:]
        x, y, c = _coords()
        sib = (x, y, 1 - c)
        chips = [(x, y), (1 - x, y), (x, 1 - y), (1 - x, 1 - y)]
        cps = []
        for a in range(na):
            for kk, chip in enumerate(chips):
                cp = pltpu.make_async_remote_copy(
                    src_ref=_shard_window(ins[a], axes[a], widths[a], _dev_index((*chip, 1 - c))),
                    dst_ref=outs[a].at[kk], send_sem=send_sems.at[a, kk], recv_sem=recv_sems.at[a, kk],
                    device_id=sib, device_id_type=MESH)
                cp.start()
                cps.append(cp)
        for cp in cps:
            cp.wait()

    return pl.pallas_call(
        body, out_shape=out_shapes, in_specs=[ANY] * na, out_specs=[ANY] * na,
        scratch_shapes=[pltpu.SemaphoreType.DMA((na, 4)), pltpu.SemaphoreType.DMA((na, 4))],
        name=name)(*grads)


def _pair_sum(name, grad, land, axis, pos):
    wd = grad.shape[axis] // N_DEV
    shard_shape = land.shape[1:]
    rows, cols = shard_shape
    tr = _tile(rows, 256, 8)
    nt = rows // tr

    def dev_of(kk, pos_ref):
        return 4 * (pos_ref[0] ^ (kk & 1)) + 2 * (pos_ref[1] ^ (kk >> 1)) + pos_ref[2]

    if axis == 1:
        gspec = pl.BlockSpec((tr, wd), lambda kk, t, p: (t, dev_of(kk, p)))
    else:
        gspec = pl.BlockSpec((tr, cols), lambda kk, t, p: (dev_of(kk, p) * nt + t, 0))
    lspec = pl.BlockSpec((None, tr, cols), lambda kk, t, p: (kk, t, 0))
    p0spec = pl.BlockSpec((tr, cols), lambda kk, t, p: (jnp.where(kk == 0, t, nt - 1), 0))
    pbspec = pl.BlockSpec((None, tr, cols), lambda kk, t, p: (jnp.maximum(kk - 1, 0), jnp.where(kk == 0, 0, t), 0))

    def body(pos_ref, g_ref, l_ref, p0_ref, pb_ref):
        kk = pl.program_id(0)
        s = g_ref[...] + l_ref[...]

        @pl.when(kk == 0)
        def _():
            p0_ref[...] = s

        @pl.when(kk > 0)
        def _():
            pb_ref[...] = s.astype(BF16)

    return pl.pallas_call(
        body,
        grid_spec=pltpu.PrefetchScalarGridSpec(
            num_scalar_prefetch=1, grid=(4, nt), in_specs=[gspec, lspec], out_specs=[p0spec, pbspec]),
        out_shape=[jax.ShapeDtypeStruct(shard_shape, F32), jax.ShapeDtypeStruct((3, *shard_shape), BF16)],
        compiler_params=_params(("arbitrary", "arbitrary")), name=name)(pos, grad, land)


def _adamw(name, parts, w, m, v):
    rows, cols = w.shape
    tr = _tile(rows, 256, 8)
    npart = len(parts)
    c1 = 1.0 - ADAM_B1 ** ADAM_STEP
    c2 = 1.0 - ADAM_B2 ** ADAM_STEP

    def body(*refs):
        p_refs = refs[:npart]
        w_ref, m_ref, v_ref, g_ref, d_ref, nm_ref, nv_ref = refs[npart:]
        g = None
        for r in p_refs:
            if len(r.shape) == 3:
                for i in range(r.shape[0]):
                    t = r[i].astype(F32)
                    g = t if g is None else g + t
            else:
                t = r[...].astype(F32)
                g = t if g is None else g + t
        nm = ADAM_B1 * m_ref[...] + (1.0 - ADAM_B1) * g
        nv = ADAM_B2 * v_ref[...] + (1.0 - ADAM_B2) * (g * g)
        g_ref[...] = g
        nm_ref[...] = nm
        nv_ref[...] = nv
        d_ref[...] = -ADAM_LR * ((nm / c1) / (jnp.sqrt(nv / c2) + ADAM_EPS) + ADAM_WD * w_ref[...])

    spec = pl.BlockSpec((tr, cols), lambda i: (i, 0))
    pspecs = [pl.BlockSpec((p.shape[0], tr, cols), lambda i: (0, i, 0)) if p.ndim == 3 else spec
              for p in parts]
    return pl.pallas_call(
        body, grid=(rows // tr,), in_specs=pspecs + [spec] * 3, out_specs=[spec] * 4,
        out_shape=[jax.ShapeDtypeStruct((rows, cols), F32)] * 4,
        compiler_params=_params(("parallel",)), name=name)(*parts, w, m, v)


SMALL_NAMES = ("s5_lam_re", "s5_lam_im", "s5_log_dt", "s5_b_re", "s5_b_im", "s5_c_re", "s5_c_im",
               "ffn_norm", "b_norm_mix", "kv_norm", "final_norm")
SMALL_PAD = 1024


def _pack(parts):
    flat = []
    for p in parts:
        f = p.reshape(-1)
        pad = (-f.shape[0]) % SMALL_PAD
        if pad:
            f = jnp.concatenate([f, jnp.zeros((pad,), f.dtype)])
        flat.append(f)
    return jnp.concatenate(flat).reshape(-1, 128)


def _unpack(packed, shapes):
    flat = packed.reshape(-1)
    out, off = [], 0
    for shp in shapes:
        size = math.prod(shp)
        out.append(flat[off:off + size].reshape(shp))
        off += size + (-size) % SMALL_PAD
    return out


def kernel(x, s5_lam_re, s5_lam_im, s5_log_dt, s5_b_re, s5_b_im, s5_c_re, s5_c_im, s5_d, s5_w_glu, a_norm_mix, ffn_norm, ffn_w_in, ffn_w_out, b_norm_mix, attn_w_q, attn_w_o, kv_norm, w_kv, final_norm, loss_target, m_s5_lam_re, m_s5_lam_im, m_s5_log_dt, m_s5_b_re, m_s5_b_im, m_s5_c_re, m_s5_c_im, m_s5_d, m_s5_w_glu, m_a_norm_mix, m_ffn_norm, m_ffn_w_in, m_ffn_w_out, m_b_norm_mix, m_attn_w_q, m_attn_w_o, m_kv_norm, m_w_kv, m_final_norm, v_s5_lam_re, v_s5_lam_im, v_s5_log_dt, v_s5_b_re, v_s5_b_im, v_s5_c_re, v_s5_c_im, v_s5_d, v_s5_w_glu, v_a_norm_mix, v_ffn_norm, v_ffn_w_in, v_ffn_w_out, v_b_norm_mix, v_attn_w_q, v_attn_w_o, v_kv_norm, v_w_kv, v_final_norm):
    args = dict(locals())
    T, D = x.shape[1], x.shape[2]
    n_layers = ffn_w_in.shape[0]
    xi_, yi_, ci_ = _coords()
    pos = jnp.stack([xi_, yi_, ci_]).astype(jnp.int32)
    me = 4 * xi_ + 2 * yi_ + ci_

    big_names = ["glu"] + [f"win{l}" for l in range(n_layers)] + [f"wout{l}" for l in range(n_layers)] \
        + ["wkv", "wq", "wo"]
    big_shards = [s5_w_glu[0]] + [ffn_w_in[l] for l in range(n_layers)] \
        + [ffn_w_out[l] for l in range(n_layers)] + [w_kv, attn_w_q[0], attn_w_o[0]]
    big_axes = [1] + [1] * n_layers + [0] * n_layers + [0, 1, 0]
    big_out_names = ["s5_w_glu"] + ["ffn_w_in"] * n_layers + ["ffn_w_out"] * n_layers \
        + ["w_kv", "attn_w_q", "attn_w_o"]
    index_of = {n: i for i, n in enumerate(big_names)}
    vec_shard = jnp.concatenate([s5_d, a_norm_mix], axis=0)
    (vecs,) = _all_gather("vectors_all_gather", [vec_shard], [1])

    gather_groups = [["glu"], ["win0"], ["wout0"], ["wkv", "wq", "wo"], ["win1", "wout1"]]
    group_idx = [[index_of[n] for n in grp] for grp in gather_groups]
    group_of = {n: gi for gi, grp in enumerate(gather_groups) for n in grp}
    big_bf16 = [s.astype(BF16) for s in big_shards]
    gather_sems, shards_thru, lands_thru, start_token = _gather_start(
        big_bf16, _place_own(big_bf16, big_axes), big_axes, group_idx)
    full = {}

    def wts(name, after):
        if name not in full:
            gi = group_of[name]
            idx = group_idx[gi]
            axes = [big_axes[a] for a in idx]
            send, r_d2d, r_ici = gather_sems[gi]
            f_send, f_recv, lands = _gather_forward(
                f"weights_gather_forward{gi}", [lands_thru[a] for a in idx], axes, r_ici, after)
            done = _gather_finish(f"weights_gather_finish{gi}", [shards_thru[a] for a in idx], lands, axes,
                                  send, r_d2d, f_send, f_recv)
            full.update(zip(gather_groups[gi], done))
        return full[name]

    exchanges = []

    def ready(grads):
        names = list(grads)
        axes = [big_axes[index_of[n]] for n in names]
        tag = "_".join(names)
        lands = _pair_exchange(f"rs_pair_exchange_{tag}", [grads[n] for n in names], axes)
        p0s, pbs = [], []
        for n, land, ax in zip(names, lands, axes):
            p0, pb = _pair_sum(f"rs_pair_sum_{n}", grads[n], land, ax, pos)
            p0s.append(p0)
            pbs.append(pb)
        started = _chip_exchange_start(f"rs_chip_exchange_start_{tag}", pbs)
        exchanges.append((names, p0s, started))
        return started[4]

    G, P, C = s5_b_re.shape[1:]
    w = dict(
        a_norm=vecs[1:2], s5_d=vecs[0:1],
        ffn_norm=ffn_norm, b_norm=b_norm_mix, kv_norm=kv_norm.reshape(1, D), final_norm=final_norm.reshape(1, D),
        lam_re=s5_lam_re[0], lam_im=s5_lam_im[0], log_dt=s5_log_dt.reshape(G, 1),
        bt_re=s5_b_re[0].transpose(2, 0, 1), bt_im=s5_b_im[0].transpose(2, 0, 1),
        c_re=s5_c_re[0], c_im=s5_c_im[0],
    )
    loss_blk, grad_x, g = _local_step(x[0], loss_target[0], w, wts, ready, dep0=start_token)
    loss = lax.psum(loss_blk[0, 0], ("x", "y", "c"))

    out = {}

    def put(name, res, shape):
        for kind, r in zip(("grad", "delta", "new_m", "new_v"), res):
            out[f"{kind}_{name}"] = r.reshape(shape)

    layered = {}

    def update(names, p0s, recvd):
        first = None
        for name, p0, rc in zip(names, p0s, recvd):
            oname, shard = big_out_names[index_of[name]], big_shards[index_of[name]]
            if oname in ("ffn_w_in", "ffn_w_out"):
                layer = int(name[-1])
                res = _adamw(f"adamw_{name}", [p0, rc], shard, args["m_" + oname][layer], args["v_" + oname][layer])
                layered.setdefault(oname, {})[layer] = res
            else:
                mm, vv = args["m_" + oname], args["v_" + oname]
                res = _adamw(f"adamw_{name}", [p0, rc], shard, mm.reshape(shard.shape), vv.reshape(shard.shape))
                put(oname, res, args[oname].shape)
            first = res[0] if first is None else first
        return first

    early, last = exchanges[:-1], exchanges[-1]
    landed = _chip_exchange_finish("rs_chip_exchange_finish_early", [e[2] for e in early], grad_x)
    marker = None
    for (names, p0s, _), recvd in zip(early, landed):
        marker = update(names, p0s, recvd)
    (recvd,) = _chip_exchange_finish("rs_chip_exchange_finish_last", [last[2]], marker)
    update(last[0], last[1], recvd)
    for oname, per_layer in layered.items():
        put(oname, [jnp.stack([per_layer[l][i] for l in range(n_layers)]) for i in range(4)], args[oname].shape)

    small_g = dict(
        s5_lam_re=g["lam_re"], s5_lam_im=g["lam_im"], s5_log_dt=g["log_dt"],
        s5_b_re=g["bt_re"].transpose(1, 2, 0), s5_b_im=g["bt_im"].transpose(1, 2, 0),
        s5_c_re=g["c_re"], s5_c_im=g["c_im"], ffn_norm=g["ffn_norm"], b_norm_mix=g["b_norm"],
        kv_norm=g["kv_norm"], final_norm=g["final_norm"])
    packed = _pack([small_g[n] for n in SMALL_NAMES] + [g["s5_d"], g["a_norm"]])
    rows = packed.shape[0]
    (all_parts,) = _all_gather("small_grads_all_gather", [packed], [0])
    all_parts = all_parts.reshape(N_DEV, rows, 128)
    n_rep_rows = _pack([small_g[n] for n in SMALL_NAMES]).shape[0]
    w_pack = _pack([args[n] for n in SMALL_NAMES])
    m_pack = _pack([args["m_" + n] for n in SMALL_NAMES])
    v_pack = _pack([args["v_" + n] for n in SMALL_NAMES])
    res = _adamw("adamw_small", [all_parts[:, :n_rep_rows]], w_pack, m_pack, v_pack)
    shapes = [args[n].shape for n in SMALL_NAMES]
    unpacked = [_unpack(r, shapes) for r in res]
    for i, n in enumerate(SMALL_NAMES):
        put(n, [u[i] for u in unpacked], args[n].shape)
    ws = D // N_DEV
    tail = all_parts[:, n_rep_rows:].reshape(N_DEV, 2, D)
    tail = lax.dynamic_slice_in_dim(tail, me * ws, ws, axis=2)
    res = _adamw("adamw_vec", [tail], vec_shard,
                 jnp.concatenate([m_s5_d, m_a_norm_mix], axis=0), jnp.concatenate([v_s5_d, v_a_norm_mix], axis=0))
    put("s5_d", [r[0:1] for r in res], s5_d.shape)
    put("a_norm_mix", [r[1:2] for r in res], a_norm_mix.shape)

    names = ("s5_lam_re", "s5_lam_im", "s5_log_dt", "s5_b_re", "s5_b_im", "s5_c_re", "s5_c_im", "s5_d",
             "s5_w_glu", "a_norm_mix", "ffn_norm", "ffn_w_in", "ffn_w_out", "b_norm_mix", "attn_w_q",
             "attn_w_o", "kv_norm", "w_kv", "final_norm")
    result = [loss, grad_x.reshape(x.shape)]
    for kind in ("grad", "delta", "new_m", "new_v"):
        result += [out[f"{kind}_{n}"] for n in names]
    return tuple(result)
```

```python
import functools
import math

import jax
import jax.numpy as jnp
from jax import lax
from jax.experimental import pallas as pl
from jax.experimental.pallas import tpu as pltpu

F32 = jnp.float32
BF16 = jnp.bfloat16

EPS = 1e-6
NEG_INF = -1e30
HEAD_DIM = 128
N_KV_HEADS = 4
DILATIONS = (1, 4, 16)
ATT_BLK = 128
S5_C = 16
S5_P = 64
S5_GB = 16
S5_CH = S5_GB * S5_C
S5_W = S5_GB * S5_P
S5_UNROLL = 4
N_DEV = 8

ADAM_LR = 0.001
ADAM_B1 = 0.9
ADAM_B2 = 0.999
ADAM_EPS = 1e-08
ADAM_WD = 0.01
ADAM_STEP = 10

VMEM_LIMIT_BYTES = 56 * 1024 * 1024
MESH = pl.DeviceIdType.MESH
ANY = pl.BlockSpec(memory_space=pl.ANY)


def _tile(n, pref, align=128):
    t = (min(pref, n) // align) * align
    while t >= align:
        if n % t == 0:
            return t
        t -= align
    return n


def _params(sem):
    return pltpu.CompilerParams(dimension_semantics=sem, vmem_limit_bytes=VMEM_LIMIT_BYTES)


def _sigmoid(x):
    return 1.0 / (1.0 + jnp.exp(-x))


NN = (((1,), (0,)), ((), ()))
NT = (((1,), (1,)), ((), ()))
TN = (((0,), (0,)), ((), ()))


def _dot(a, b, dims=NN):
    return lax.dot_general(a, b, dims, preferred_element_type=F32)


def _matmul(name, grid, ins, in_specs, products, dims, out_shapes, out_specs, acc_shapes, epilogue):
    n_in, n_out, nk = len(ins), len(out_shapes), grid[2]

    def body(*refs):
        in_refs = refs[:n_in]
        out_refs = refs[n_in:n_in + n_out]
        acc_refs = refs[n_in + n_out:]

        def prods():
            vals = [None] * len(acc_shapes)
            for ai, bi, ci in products:
                d = _dot(in_refs[ai][...], in_refs[bi][...], dims)
                vals[ci] = d if vals[ci] is None else vals[ci] + d
            return vals

        if nk == 1:
            epilogue(in_refs, out_refs, prods())
        else:
            k = pl.program_id(2)

            @pl.when(k == 0)
            def _():
                for a in acc_refs:
                    a[...] = jnp.zeros_like(a)

            for a, v in zip(acc_refs, prods()):
                a[...] += v

            @pl.when(k == nk - 1)
            def _():
                epilogue(in_refs, out_refs, [a[...] for a in acc_refs])

    scratch = [] if nk == 1 else [pltpu.VMEM(s, F32) for s in acc_shapes]
    return pl.pallas_call(
        body, grid=grid, in_specs=in_specs, out_specs=out_specs, out_shape=out_shapes,
        scratch_shapes=scratch, compiler_params=_params(("parallel", "parallel", "arbitrary")),
        name=name)(*ins)


def _mm_dual_fwd(name, a, w, res, kind):
    T, K = a.shape
    N = w.shape[1] // 2
    tm, tn = _tile(T, 512), _tile(N, 512)
    nj = N // tn
    grid = (T // tm, nj, 1)
    ins = [a, w, w]
    in_specs = [pl.BlockSpec((tm, K), lambda i, j, k: (i, 0)),
                pl.BlockSpec((K, tn), lambda i, j, k: (0, j)),
                pl.BlockSpec((K, tn), lambda i, j, k: (0, j + nj))]
    pair_spec = pl.BlockSpec((2, tm, tn), lambda i, j, k: (0, i, j))
    tile_spec = pl.BlockSpec((tm, tn), lambda i, j, k: (i, j))
    if kind == "glu":
        ins.append(res)
        in_specs.append(tile_spec)

        def epilogue(in_refs, out_refs, accs):
            val, gate = accs
            out_refs[0][...] = in_refs[3][...] + val * _sigmoid(gate)
            out_refs[1][0] = val.astype(BF16)
            out_refs[1][1] = gate.astype(BF16)

        out_shapes = [jax.ShapeDtypeStruct((T, N), F32), jax.ShapeDtypeStruct((2, T, N), BF16)]
        out_specs = [tile_spec, pair_spec]
    else:
        def epilogue(in_refs, out_refs, accs):
            g, u = accs
            out_refs[0][0] = g.astype(BF16)
            out_refs[0][1] = u.astype(BF16)
            out_refs[1][...] = (g * _sigmoid(g) * u).astype(BF16)

        out_shapes = [jax.ShapeDtypeStruct((2, T, N), BF16), jax.ShapeDtypeStruct((T, N), BF16)]
        out_specs = [pair_spec, tile_spec]
    return _matmul(name, grid, ins, in_specs, [(0, 1, 0), (0, 2, 1)], NN, out_shapes, out_specs,
                   [(tm, tn), (tm, tn)], epilogue)


def _mm_kv(name, a, w):
    T, K = a.shape
    N = w.shape[1] // 2
    tm, tn = _tile(T, 512), _tile(N, 512)
    nj = N // tn
    tile_spec = pl.BlockSpec((tm, tn), lambda i, j, k: (i, j))

    def epilogue(in_refs, out_refs, accs):
        out_refs[0][...] = accs[0].astype(BF16)
        out_refs[1][...] = accs[1].astype(BF16)

    return _matmul(name, (T // tm, nj, 1), [a, w, w],
                   [pl.BlockSpec((tm, K), lambda i, j, k: (i, 0)),
                    pl.BlockSpec((K, tn), lambda i, j, k: (0, j)),
                    pl.BlockSpec((K, tn), lambda i, j, k: (0, j + nj))],
                   [(0, 1, 0), (0, 2, 1)], NN,
                   [jax.ShapeDtypeStruct((T, N), BF16)] * 2, [tile_spec, tile_spec],
                   [(tm, tn), (tm, tn)], epilogue)


def _mm_nn(name, a, w, res=None, out_dtype=F32):
    T, K = a.shape
    N = w.shape[1]
    tm, tn, tk = _tile(T, 1024), _tile(N, 1024), _tile(K, 2048 if K <= 2048 else 512)
    grid = (T // tm, N // tn, K // tk)
    tile_spec = pl.BlockSpec((tm, tn), lambda i, j, k: (i, j))
    ins = [a, w]
    in_specs = [pl.BlockSpec((tm, tk), lambda i, j, k: (i, k)),
                pl.BlockSpec((tk, tn), lambda i, j, k: (k, j))]
    if res is not None:
        ins.append(res)
        in_specs.append(tile_spec)

    def epilogue(in_refs, out_refs, accs):
        v = accs[0]
        if res is not None:
            v = v + in_refs[2][...]
        out_refs[0][...] = v.astype(out_dtype)

    return _matmul(name, grid, ins, in_specs, [(0, 1, 0)], NN,
                   [jax.ShapeDtypeStruct((T, N), out_dtype)], [tile_spec], [(tm, tn)], epilogue)[0]


def _dep_operand(ins, in_specs, dep):
    if dep is not None:
        ins.append(dep)
        in_specs.append(pl.BlockSpec((8, 128), lambda *_: (0, 0)))


def _mm_nt(name, a_list, w, out_dtype=F32, dep=None):
    T, Np = a_list[0].shape
    Ko = w.shape[0]
    n_parts = len(a_list)
    tm, tn, tk = _tile(T, 1024), _tile(Ko, 1024), _tile(Np, 512)
    nkp = Np // tk
    grid = (T // tm, Ko // tn, nkp)
    ins = list(a_list) + [w] * n_parts
    in_specs = [pl.BlockSpec((tm, tk), lambda i, j, k: (i, k)) for _ in a_list]
    in_specs += [pl.BlockSpec((tn, tk), functools.partial(lambda i, j, k, p: (j, p * nkp + k), p=p))
                 for p in range(n_parts)]
    products = [(p, n_parts + p, 0) for p in range(n_parts)]
    _dep_operand(ins, in_specs, dep)

    def epilogue(in_refs, out_refs, accs):
        out_refs[0][...] = accs[0].astype(out_dtype)

    return _matmul(name, grid, ins, in_specs, products, NT,
                   [jax.ShapeDtypeStruct((T, Ko), out_dtype)],
                   [pl.BlockSpec((tm, tn), lambda i, j, k: (i, j))], [(tm, tn)], epilogue)[0]


def _mm_nt_pair(name, a3, w):
    _, T, N = a3.shape
    Ko = w.shape[0]
    tm, tn, tk = _tile(T, 1024), _tile(Ko, 1024), _tile(N, 512)
    nkh = N // tk
    grid = (T // tm, Ko // tn, 2 * nkh)

    def epilogue(in_refs, out_refs, accs):
        out_refs[0][...] = accs[0]

    return _matmul(name, grid, [a3, w],
                   [pl.BlockSpec((None, tm, tk), lambda i, j, k: (k // nkh, i, k % nkh)),
                    pl.BlockSpec((tn, tk), lambda i, j, k: (j, k))],
                   [(0, 1, 0)], NT, [jax.ShapeDtypeStruct((T, Ko), F32)],
                   [pl.BlockSpec((tm, tn), lambda i, j, k: (i, j))], [(tm, tn)], epilogue)[0]


def _mm_nt_ffn_bwd(name, dx, w_out, gu, dep=None):
    T, D = dx.shape
    Fh = w_out.shape[0]
    tm, tn = _tile(T, 512), _tile(Fh, 512)
    pair_spec = pl.BlockSpec((2, tm, tn), lambda i, j, k: (0, i, j))

    def epilogue(in_refs, out_refs, accs):
        da = accs[0]
        g = in_refs[2][0].astype(F32)
        u = in_refs[2][1].astype(F32)
        s = _sigmoid(g)
        out_refs[0][0] = (da * u * (s * (1.0 + g * (1.0 - s)))).astype(BF16)
        out_refs[0][1] = (da * (g * s)).astype(BF16)

    ins = [dx, w_out, gu]
    in_specs = [pl.BlockSpec((tm, D), lambda i, j, k: (i, 0)),
                pl.BlockSpec((tn, D), lambda i, j, k: (j, 0)),
                pair_spec]
    _dep_operand(ins, in_specs, dep)
    return _matmul(name, (T // tm, Fh // tn, 1), ins, in_specs,
                   [(0, 1, 0)], NT, [jax.ShapeDtypeStruct((2, T, Fh), BF16)], [pair_spec],
                   [(tm, tn)], epilogue)[0]


def _mm_tn(name, a, d):
    T, Ko = a.shape
    N = d.shape[1]
    to, tn, tk = _tile(Ko, 1024), _tile(N, 1024), _tile(T, 1024)
    grid = (Ko // to, N // tn, T // tk)

    def epilogue(in_refs, out_refs, accs):
        out_refs[0][...] = accs[0].astype(BF16)

    return _matmul(name, grid, [a, d],
                   [pl.BlockSpec((tk, to), lambda i, j, k: (k, i)),
                    pl.BlockSpec((tk, tn), lambda i, j, k: (k, j))],
                   [(0, 1, 0)], TN, [jax.ShapeDtypeStruct((Ko, N), BF16)],
                   [pl.BlockSpec((to, tn), lambda i, j, k: (i, j))], [(to, tn)], epilogue)[0]


def _mm_tn_pair(name, a, d3):
    T, Ko = a.shape
    N = d3.shape[2]
    to, tn, tk = _tile(Ko, 2048), _tile(N, 512), _tile(T, 1024)
    njh = N // tn
    grid = (Ko // to, 2 * njh, T // tk)

    def epilogue(in_refs, out_refs, accs):
        out_refs[0][...] = accs[0].astype(BF16)

    return _matmul(name, grid, [a, d3],
                   [pl.BlockSpec((tk, to), lambda i, j, k: (k, i)),
                    pl.BlockSpec((None, tk, tn), lambda i, j, k: (j // njh, k, j % njh))],
                   [(0, 1, 0)], TN, [jax.ShapeDtypeStruct((Ko, 2 * N), BF16)],
                   [pl.BlockSpec((to, tn), lambda i, j, k: (i, j))], [(to, tn)], epilogue)[0]


def _rms_fwd(name, x, gains, dtypes, dep=None):
    T, D = x.shape
    n = len(gains)
    tr = _tile(T, 512, 8)
    n_dep = 0 if dep is None else 1

    def body(x_ref, *refs):
        xv = x_ref[...]
        xr = xv * lax.rsqrt(jnp.mean(xv * xv, axis=-1, keepdims=True) + EPS)
        for g_ref, o_ref in zip(refs[:n], refs[n + n_dep:]):
            o_ref[...] = (xr * g_ref[...]).astype(o_ref.dtype)

    row = pl.BlockSpec((tr, D), lambda i: (i, 0))
    vec = pl.BlockSpec((1, D), lambda i: (0, 0))
    ins, in_specs = [x, *gains], [row] + [vec] * n
    _dep_operand(ins, in_specs, dep)
    return pl.pallas_call(
        body, grid=(T // tr,), in_specs=in_specs, out_specs=[row] * n,
        out_shape=[jax.ShapeDtypeStruct((T, D), dt) for dt in dtypes],
        compiler_params=_params(("parallel",)), name=name)(*ins)


def _rms_bwd(name, x, dres, gains, dhs):
    T, D = x.shape
    n = len(gains)
    tr = _tile(T, 256, 8)

    def body(x_ref, dres_ref, *refs):
        g_refs, dh_refs = refs[:n], refs[n:2 * n]
        dx_ref, dxb_ref = refs[2 * n], refs[2 * n + 1]
        dg_refs = refs[2 * n + 2:]
        xv = x_ref[...]
        r = lax.rsqrt(jnp.mean(xv * xv, axis=-1, keepdims=True) + EPS)
        xr = xv * r
        w = None
        for g_ref, dh_ref, dg_ref in zip(g_refs, dh_refs, dg_refs):
            dh = dh_ref[...].astype(F32)

            @pl.when(pl.program_id(0) == 0)
            def _():
                dg_ref[...] = jnp.zeros_like(dg_ref)

            dg_ref[...] += jnp.sum(dh * xr, axis=0, keepdims=True)
            wi = dh * g_ref[...]
            w = wi if w is None else w + wi
        dx = dres_ref[...] + r * (w - xr * jnp.mean(w * xr, axis=-1, keepdims=True))
        dx_ref[...] = dx
        dxb_ref[...] = dx.astype(BF16)

    row = pl.BlockSpec((tr, D), lambda i: (i, 0))
    vec = pl.BlockSpec((1, D), lambda i: (0, 0))
    outs = pl.pallas_call(
        body, grid=(T // tr,), in_specs=[row, row] + [vec] * n + [row] * n,
        out_specs=[row, row] + [vec] * n,
        out_shape=[jax.ShapeDtypeStruct((T, D), F32), jax.ShapeDtypeStruct((T, D), BF16)]
        + [jax.ShapeDtypeStruct((1, D), F32)] * n,
        compiler_params=_params(("arbitrary",)), name=name)(x, dres, *gains, *dhs)
    return outs[0], outs[1], outs[2:]


def _loss_head(x, gain, target):
    T, D = x.shape
    tr = _tile(T, 256, 8)

    def body(x_ref, g_ref, t_ref, loss_ref, dx_ref, dxb_ref, dg_ref):
        @pl.when(pl.program_id(0) == 0)
        def _():
            loss_ref[...] = jnp.zeros_like(loss_ref)
            dg_ref[...] = jnp.zeros_like(dg_ref)

        xv = x_ref[...]
        r = lax.rsqrt(jnp.mean(xv * xv, axis=-1, keepdims=True) + EPS)
        xr = xv * r
        err = xr * g_ref[...] - t_ref[...]
        part = jnp.sum(jnp.sum(err * err, axis=-1, keepdims=True), axis=0, keepdims=True) * (0.5 / D)
        loss_ref[...] += jnp.broadcast_to(part, loss_ref.shape)
        dy = err * (1.0 / D)
        dg_ref[...] += jnp.sum(dy * xr, axis=0, keepdims=True)
        w = dy * g_ref[...]
        dx = r * (w - xr * jnp.mean(w * xr, axis=-1, keepdims=True))
        dx_ref[...] = dx
        dxb_ref[...] = dx.astype(BF16)

    row = pl.BlockSpec((tr, D), lambda i: (i, 0))
    vec = pl.BlockSpec((1, D), lambda i: (0, 0))
    return pl.pallas_call(
        body, grid=(T // tr,), in_specs=[row, vec, row],
        out_specs=[pl.BlockSpec((8, 128), lambda i: (0, 0)), row, row, vec],
        out_shape=[jax.ShapeDtypeStruct((8, 128), F32), jax.ShapeDtypeStruct((T, D), F32),
                   jax.ShapeDtypeStruct((T, D), BF16), jax.ShapeDtypeStruct((1, D), F32)],
        compiler_params=_params(("arbitrary",)), name="loss_head")(x, gain, target)


def _glu_bwd(dmix, vg, dep=None):
    T, N = dmix.shape
    tr, tc = _tile(T, 512, 8), _tile(N, 1024)
    n_dep = 0 if dep is None else 1

    def body(d_ref, vg_ref, *refs):
        o_ref = refs[n_dep]
        d = d_ref[...]
        val = vg_ref[0].astype(F32)
        s = _sigmoid(vg_ref[1].astype(F32))
        o_ref[0] = (d * s).astype(BF16)
        o_ref[1] = (d * val * s * (1.0 - s)).astype(BF16)

    pair = pl.BlockSpec((2, tr, tc), lambda i, j: (0, i, j))
    ins, in_specs = [dmix, vg], [pl.BlockSpec((tr, tc), lambda i, j: (i, j)), pair]
    _dep_operand(ins, in_specs, dep)
    return pl.pallas_call(
        body, grid=(T // tr, N // tc), in_specs=in_specs,
        out_specs=pair, out_shape=jax.ShapeDtypeStruct((2, T, N), BF16),
        compiler_params=_params(("parallel", "parallel")), name="glu_bwd")(*ins)


def _to_state_tiles(x_ref, s, val):
    tc = val.shape[0]
    for j in range(S5_W // 128):
        x_ref[s, pl.ds(j, tc, stride=8), :] = val[:, 128 * j:128 * (j + 1)]


def _from_state_tiles(x_ref, s, tc):
    return jnp.concatenate([x_ref[s, pl.ds(j, tc, stride=8), :] for j in range(S5_W // 128)], axis=1)


def _s5_scan_fwd(xr_ref, xi_ref, ar_ref, ai_ref, cr_ref, ci_ref, tc, nblk):
    a = [(ar_ref[s], ai_ref[s]) for s in range(nblk)]

    def step(i, carry):
        carry = list(carry)
        for uu in range(S5_UNROLL):
            r0 = pl.multiple_of((i * S5_UNROLL + uu) * 8, 8)
            for s in range(nblk):
                cr, ci = carry[2 * s], carry[2 * s + 1]
                a_r, a_i = a[s]
                xr = a_r * cr - a_i * ci + xr_ref[s, pl.ds(r0, 8), :]
                xi = a_r * ci + a_i * cr + xi_ref[s, pl.ds(r0, 8), :]
                xr_ref[s, pl.ds(r0, 8), :] = xr
                xi_ref[s, pl.ds(r0, 8), :] = xi
                carry[2 * s], carry[2 * s + 1] = xr, xi
        return tuple(carry)

    init = []
    for s in range(nblk):
        init += [cr_ref[s], ci_ref[s]]
    out = lax.fori_loop(0, tc // S5_UNROLL, step, tuple(init))
    for s in range(nblk):
        cr_ref[s] = out[2 * s]
        ci_ref[s] = out[2 * s + 1]


def _s5_scan_bwd(lr_ref, li_ref, xr_ref, xi_ref, h_ref, ar_ref, ai_ref, cr_ref, ci_ref,
                 accr_ref, acci_ref, tc, nblk):
    a = [(ar_ref[s], ai_ref[s]) for s in range(nblk)]

    def one(s, r0, prev_r, prev_i, st):
        c_r, c_i, d_r, d_i = st
        a_r, a_i = a[s]
        l_r = lr_ref[s, pl.ds(r0, 8), :] + a_r * c_r + a_i * c_i
        l_i = li_ref[s, pl.ds(r0, 8), :] + a_r * c_i - a_i * c_r
        lr_ref[s, pl.ds(r0, 8), :] = l_r
        li_ref[s, pl.ds(r0, 8), :] = l_i
        return [l_r, l_i, d_r + l_r * prev_r + l_i * prev_i, d_i - l_r * prev_i + l_i * prev_r]

    def step(i, carry):
        carry = list(carry)
        for uu in range(S5_UNROLL):
            t = tc - 1 - (i * S5_UNROLL + uu)
            r0 = pl.multiple_of(t * 8, 8)
            p0 = pl.multiple_of((t - 1) * 8, 8)
            for s in range(nblk):
                carry[4 * s:4 * s + 4] = one(s, r0, xr_ref[s, pl.ds(p0, 8), :], xi_ref[s, pl.ds(p0, 8), :],
                                             carry[4 * s:4 * s + 4])
        return tuple(carry)

    init = []
    for s in range(nblk):
        init += [cr_ref[s], ci_ref[s], accr_ref[s], acci_ref[s]]
    carry = list(lax.fori_loop(0, tc // S5_UNROLL - 1, step, tuple(init)))
    for t in range(S5_UNROLL - 1, -1, -1):
        for s in range(nblk):
            if t > 0:
                prev_r, prev_i = xr_ref[s, 8 * (t - 1):8 * t, :], xi_ref[s, 8 * (t - 1):8 * t, :]
            else:
                prev_r, prev_i = h_ref[0, s], h_ref[1, s]
            carry[4 * s:4 * s + 4] = one(s, 8 * t, prev_r, prev_i, carry[4 * s:4 * s + 4])
    for s in range(nblk):
        cr_ref[s], ci_ref[s], accr_ref[s], acci_ref[s] = carry[4 * s:4 * s + 4]


def _gelu(y):
    k = math.sqrt(2.0 / math.pi)
    return 0.5 * y * (1.0 + jnp.tanh(k * (y + 0.044715 * (y * y * y))))


def _gelu_grad(y):
    k = math.sqrt(2.0 / math.pi)
    t = jnp.tanh(k * (y + 0.044715 * (y * y * y)))
    return 0.5 * (1.0 + t) + 0.5 * y * (1.0 - t * t) * (k * (1.0 + 3.0 * 0.044715 * (y * y)))


def _s5_specs(tc, nch, sbk, rev):
    def ch(c):
        return nch - 1 - c if rev else c

    return dict(
        act=pl.BlockSpec((tc, sbk * S5_CH), lambda i, c: (ch(c), i)),
        bb=pl.BlockSpec((sbk, S5_CH, S5_W), lambda i, c: (i, 0, 0)),
        cc=pl.BlockSpec((sbk, S5_W, S5_CH), lambda i, c: (i, 0, 0)),
        a=pl.BlockSpec((sbk, 8, 128), lambda i, c: (i, 0, 0)),
        d=pl.BlockSpec((1, sbk * S5_CH), lambda i, c: (0, i)),
        h=pl.BlockSpec((None, 2, sbk, 8, 128), lambda i, c: (ch(c), 0, i, 0, 0)),
    )


def _s5_blocks(nb, pref):
    return max(b for b in range(1, pref + 1) if nb % b == 0)


def _s5_fwd(u, bbr, bbi, ccr, cci, a_re, a_im, d_skip):
    T, D = u.shape
    nb = D // S5_CH
    sbk = _s5_blocks(nb, 4)
    tc = _tile(T, 512, 8)
    nch = T // tc
    sp = _s5_specs(tc, nch, sbk, False)

    def body(u_ref, bbr_ref, bbi_ref, ccr_ref, cci_ref, ar_ref, ai_ref, d_ref, z_ref, h_ref,
             xr, xi, cr, ci):
        @pl.when(pl.program_id(1) == 0)
        def _():
            cr[...] = jnp.zeros_like(cr)
            ci[...] = jnp.zeros_like(ci)

        h_ref[0] = cr[...]
        h_ref[1] = ci[...]
        for s in range(sbk):
            ub = u_ref[:, s * S5_CH:(s + 1) * S5_CH].astype(BF16)
            _to_state_tiles(xr, s, _dot(ub, bbr_ref[s]))
            _to_state_tiles(xi, s, _dot(ub, bbi_ref[s]))
        _s5_scan_fwd(xr, xi, ar_ref, ai_ref, cr, ci, tc, sbk)
        for s in range(sbk):
            cols = slice(s * S5_CH, (s + 1) * S5_CH)
            y = (_dot(_from_state_tiles(xr, s, tc).astype(BF16), ccr_ref[s])
                 - _dot(_from_state_tiles(xi, s, tc).astype(BF16), cci_ref[s])
                 + d_ref[:, cols] * u_ref[:, cols])
            z_ref[:, cols] = _gelu(y).astype(BF16)

    tiles = pltpu.VMEM((sbk, tc * 8, 128), F32)
    carry = pltpu.VMEM((sbk, 8, 128), F32)
    return pl.pallas_call(
        body, grid=(nb // sbk, nch),
        in_specs=[sp["act"], sp["bb"], sp["bb"], sp["cc"], sp["cc"], sp["a"], sp["a"], sp["d"]],
        out_specs=[sp["act"], sp["h"]],
        out_shape=[jax.ShapeDtypeStruct((T, D), BF16), jax.ShapeDtypeStruct((nch, 2, nb, 8, 128), F32)],
        scratch_shapes=[tiles, tiles, carry, carry],
        compiler_params=_params(("parallel", "arbitrary")), name="s5_fwd",
    )(u, bbr, bbi, ccr, cci, a_re, a_im, d_skip)


def _s5_bwd(u, dz, h0, bbr, bbi, ccr, cci, a_re, a_im, d_skip):
    T, D = u.shape
    nb = D // S5_CH
    sbk = _s5_blocks(nb, 2)
    tc = _tile(T, 512, 8)
    nch = T // tc
    sp = _s5_specs(tc, nch, sbk, True)

    def body(u_ref, dz_ref, h_ref, bbr_ref, bbi_ref, ccr_ref, cci_ref, ar_ref, ai_ref, d_ref,
             du_ref, dd_ref, dar_ref, dai_ref, dbbr_ref, dbbi_ref, dccr_ref, dcci_ref,
             xr, xi, lr, li, fr, fi, br, bi, accr, acci):
        c = pl.program_id(1)

        @pl.when(c == 0)
        def _():
            for ref in (br, bi, accr, acci, dd_ref, dbbr_ref, dbbi_ref, dccr_ref, dcci_ref):
                ref[...] = jnp.zeros_like(ref)

        for s in range(sbk):
            ub = u_ref[:, s * S5_CH:(s + 1) * S5_CH].astype(BF16)
            _to_state_tiles(xr, s, _dot(ub, bbr_ref[s]))
            _to_state_tiles(xi, s, _dot(ub, bbi_ref[s]))
        fr[...] = h_ref[0]
        fi[...] = h_ref[1]
        _s5_scan_fwd(xr, xi, ar_ref, ai_ref, fr, fi, tc, sbk)
        for s in range(sbk):
            cols = slice(s * S5_CH, (s + 1) * S5_CH)
            uv = u_ref[:, cols]
            xrb = _from_state_tiles(xr, s, tc).astype(BF16)
            xib = _from_state_tiles(xi, s, tc).astype(BF16)
            dsk = d_ref[:, cols]
            y = _dot(xrb, ccr_ref[s]) - _dot(xib, cci_ref[s]) + dsk * uv
            dy = dz_ref[:, cols] * _gelu_grad(y)
            dd_ref[:, cols] += jnp.sum(dy * uv, axis=0, keepdims=True)
            dyb = dy.astype(BF16)
            dccr_ref[s] += _dot(xrb, dyb, TN)
            dcci_ref[s] += _dot(xib, dyb, TN)
            _to_state_tiles(lr, s, _dot(dyb, ccr_ref[s], NT))
            _to_state_tiles(li, s, -_dot(dyb, cci_ref[s], NT))
            du_ref[:, cols] = dy * dsk
        _s5_scan_bwd(lr, li, xr, xi, h_ref, ar_ref, ai_ref, br, bi, accr, acci, tc, sbk)
        for s in range(sbk):
            cols = slice(s * S5_CH, (s + 1) * S5_CH)
            ub = u_ref[:, cols].astype(BF16)
            lrb = _from_state_tiles(lr, s, tc).astype(BF16)
            lib = _from_state_tiles(li, s, tc).astype(BF16)
            dbbr_ref[s] += _dot(ub, lrb, TN)
            dbbi_ref[s] += _dot(ub, lib, TN)
            du_ref[:, cols] += _dot(lrb, bbr_ref[s], NT) + _dot(lib, bbi_ref[s], NT)

        @pl.when(c == nch - 1)
        def _():
            dar_ref[...] = accr[...]
            dai_ref[...] = acci[...]

    tiles = pltpu.VMEM((sbk, tc * 8, 128), F32)
    carry = pltpu.VMEM((sbk, 8, 128), F32)
    return pl.pallas_call(
        body, grid=(nb // sbk, nch),
        in_specs=[sp["act"], sp["act"], sp["h"], sp["bb"], sp["bb"], sp["cc"], sp["cc"],
                  sp["a"], sp["a"], sp["d"]],
        out_specs=[sp["act"], sp["d"], sp["a"], sp["a"], sp["bb"], sp["bb"], sp["cc"], sp["cc"]],
        out_shape=[jax.ShapeDtypeStruct((T, D), F32), jax.ShapeDtypeStruct((1, D), F32),
                   jax.ShapeDtypeStruct((nb, 8, 128), F32), jax.ShapeDtypeStruct((nb, 8, 128), F32),
                   jax.ShapeDtypeStruct((nb, S5_CH, S5_W), F32), jax.ShapeDtypeStruct((nb, S5_CH, S5_W), F32),
                   jax.ShapeDtypeStruct((nb, S5_W, S5_CH), F32), jax.ShapeDtypeStruct((nb, S5_W, S5_CH), F32)],
        scratch_shapes=[tiles, tiles, tiles, tiles, carry, carry, carry, carry, carry, carry],
        compiler_params=_params(("parallel", "arbitrary")), name="s5_bwd",
    )(u, dz, h0, bbr, bbi, ccr, cci, a_re, a_im, d_skip)


def _s5_disc(lr, li, ldt):
    dt = jnp.exp(ldt)
    mag = jnp.exp(lr * dt)
    ang = li * dt
    cs, sn = jnp.cos(ang), jnp.sin(ang)
    lbr, lbi = mag * cs, mag * sn
    nr = lbr - 1.0
    den = lr * lr + li * li
    f_re = (nr * lr + lbi * li) / den
    f_im = (lbi * lr - nr * li) / den
    return dt, mag, cs, sn, lbr, lbi, nr, den, f_re, f_im


def _s5_param_fwd(lr, li, ldt, bt_re, bt_im):
    c, g, p = bt_re.shape

    def body(lr_ref, li_ref, ldt_ref, br_ref, bi_ref, lbr_ref, lbi_ref, bbr_ref, bbi_ref):
        _, _, _, _, lbr, lbi, _, _, f_re, f_im = _s5_disc(lr_ref[...], li_ref[...], ldt_ref[...])
        lbr_ref[...] = lbr
        lbi_ref[...] = lbi
        for ch in range(c):
            b_r, b_i = br_ref[ch], bi_ref[ch]
            bbr_ref[ch] = f_re * b_r - f_im * b_i
            bbi_ref[ch] = f_re * b_i + f_im * b_r

    gp = jax.ShapeDtypeStruct((g, p), F32)
    cgp = jax.ShapeDtypeStruct((c, g, p), F32)
    return pl.pallas_call(body, out_shape=[gp, gp, cgp, cgp], name="s5_param_fwd")(lr, li, ldt, bt_re, bt_im)


def _s5_param_bwd(lr, li, ldt, bt_re, bt_im, dlbr, dlbi, dbbr, dbbi):
    c, g, p = bt_re.shape

    def body(lr_ref, li_ref, ldt_ref, br_ref, bi_ref, dlbr_ref, dlbi_ref, dbbr_ref, dbbi_ref,
             dlr_ref, dli_ref, dldt_ref, dbr_ref, dbi_ref):
        l_r, l_i = lr_ref[...], li_ref[...]
        dt, mag, cs, sn, lbr, lbi, nr, den, f_re, f_im = _s5_disc(l_r, l_i, ldt_ref[...])
        dfr = jnp.zeros_like(l_r)
        dfi = jnp.zeros_like(l_r)
        for ch in range(c):
            b_r, b_i = br_ref[ch], bi_ref[ch]
            g_r, g_i = dbbr_ref[ch], dbbi_ref[ch]
            dbr_ref[ch] = f_re * g_r + f_im * g_i
            dbi_ref[ch] = f_re * g_i - f_im * g_r
            dfr = dfr + g_r * b_r + g_i * b_i
            dfi = dfi + g_i * b_r - g_r * b_i
        inv = 1.0 / den
        d_nr = (dfr * l_r - dfi * l_i) * inv
        d_lbi = (dfr * l_i + dfi * l_r) * inv + dlbi_ref[...]
        d_lbr = d_nr + dlbr_ref[...]
        d_den = -(dfr * f_re + dfi * f_im) * inv
        d_mag = d_lbr * cs + d_lbi * sn
        d_ang = d_lbi * lbr - d_lbr * lbi
        dlr_ref[...] = (dfr * nr + dfi * lbi) * inv + 2.0 * d_den * l_r + d_mag * mag * dt
        dli_ref[...] = (dfr * lbi - dfi * nr) * inv + 2.0 * d_den * l_i + d_ang * dt
        dldt_ref[...] = jnp.sum(d_mag * mag * l_r + d_ang * l_i, axis=1, keepdims=True) * dt

    gp = jax.ShapeDtypeStruct((g, p), F32)
    cgp = jax.ShapeDtypeStruct((c, g, p), F32)
    return pl.pallas_call(body, out_shape=[gp, gp, jax.ShapeDtypeStruct((g, 1), F32), cgp, cgp],
                          name="s5_param_bwd")(lr, li, ldt, bt_re, bt_im, dlbr, dlbi, dbbr, dbbi)


def _block_diag(t, inner_first):
    g, c, p = t.shape
    nb = g // S5_GB
    t4 = t.reshape(nb, S5_GB, c, p)
    eye = jnp.eye(S5_GB, dtype=t.dtype)
    if inner_first:
        e = t4[:, :, :, None, :] * eye[None, :, None, :, None]
        return e.reshape(nb, S5_GB * c, S5_GB * p)
    t4 = t4.transpose(0, 1, 3, 2)
    e = t4[:, :, :, None, :] * eye[None, :, None, :, None]
    return e.reshape(nb, S5_GB * p, S5_GB * c)


def _block_diag_extract(e, inner_first, c, p):
    nb = e.shape[0]
    eye = jnp.eye(S5_GB, dtype=e.dtype)
    if inner_first:
        e5 = e.reshape(nb, S5_GB, c, S5_GB, p)
        return (e5 * eye[None, :, None, :, None]).sum(3).reshape(nb * S5_GB, c, p)
    e5 = e.reshape(nb, S5_GB, p, S5_GB, c)
    return (e5 * eye[None, :, None, :, None]).sum(3).transpose(0, 1, 3, 2).reshape(nb * S5_GB, c, p)


def _att_masks(rep, gb):
    rows = rep * ATT_BLK
    qi = lax.broadcasted_iota(jnp.int32, (rows, 2 * ATT_BLK), 0) % ATT_BLK
    si = lax.broadcasted_iota(jnp.int32, (rows, 2 * ATT_BLK), 1)
    prev = (si < ATT_BLK) & (si >= qi) & (gb > 0)
    cur = (si >= ATT_BLK) & (si - ATT_BLK <= qi)
    return prev | cur


def _stack_heads(ref, rows, rep):
    return jnp.concatenate([ref[rows, j * HEAD_DIM:(j + 1) * HEAD_DIM] for j in range(rep)], axis=0)


def _att_geometry(T, H, dil):
    n = T // dil
    rep = H // N_KV_HEADS
    rc = _tile(n, 1024, ATT_BLK)
    return n, rep, rc, n // rc, rc // ATT_BLK


def _att_fwd(q, k, v, grp, dil):
    T = q.shape[0]
    H = q.shape[1] // HEAD_DIM // len(DILATIONS)
    n, rep, rc, nrc, nblk = _att_geometry(T, H, dil)
    hw = rep * HEAD_DIM
    scale = HEAD_DIM ** -0.5
    q2 = q.reshape(n, dil * q.shape[1])
    k2 = k.reshape(n, dil * k.shape[1])
    v2 = v.reshape(n, dil * v.shape[1])
    qcols = q.shape[1] // hw
    ocols = H * HEAD_DIM // hw

    def body(q_ref, k_ref, v_ref, o_ref, l_ref):
        ch = pl.program_id(2)

        def blk(b, _):
            gb = ch * nblk + b
            rows = pl.ds(pl.multiple_of(b * ATT_BLK, ATT_BLK), ATT_BLK)
            kc = pl.ds(pl.multiple_of(gb * ATT_BLK, ATT_BLK), ATT_BLK)
            kp = pl.ds(pl.multiple_of(jnp.maximum(gb - 1, 0) * ATT_BLK, ATT_BLK), ATT_BLK)
            qs = _stack_heads(q_ref, rows, rep)
            kcat = jnp.concatenate([k_ref[kp, :], k_ref[kc, :]], axis=0)
            vcat = jnp.concatenate([v_ref[kp, :], v_ref[kc, :]], axis=0)
            s = jnp.where(_att_masks(rep, gb), _dot(qs, kcat, NT) * scale, NEG_INF)
            m = jnp.max(s, axis=-1, keepdims=True)
            p = jnp.exp(s - m)
            l = jnp.sum(p, axis=-1, keepdims=True)
            o = _dot(p.astype(BF16), vcat) / l
            lse = jnp.broadcast_to(m + jnp.log(l), (rep * ATT_BLK, HEAD_DIM))
            for j in range(rep):
                o_ref[rows, j * HEAD_DIM:(j + 1) * HEAD_DIM] = o[j * ATT_BLK:(j + 1) * ATT_BLK]
                l_ref[rows, j * HEAD_DIM:(j + 1) * HEAD_DIM] = lse[j * ATT_BLK:(j + 1) * ATT_BLK]
            return 0

        lax.fori_loop(0, nblk, blk, 0)

    qspec = pl.BlockSpec((rc, hw), lambda r, h, c: (c, r * qcols + grp * N_KV_HEADS + h))
    kspec = pl.BlockSpec((n, HEAD_DIM), lambda r, h, c: (0, r * N_KV_HEADS + h))
    ospec = pl.BlockSpec((rc, hw), lambda r, h, c: (c, r * ocols + h))
    o, l = pl.pallas_call(
        body, grid=(dil, N_KV_HEADS, nrc), in_specs=[qspec, kspec, kspec], out_specs=[ospec, ospec],
        out_shape=[jax.ShapeDtypeStruct((n, dil * H * HEAD_DIM), F32)] * 2,
        compiler_params=_params(("parallel", "parallel", "arbitrary")), name=f"att_fwd_d{dil}",
    )(q2, k2, v2)
    return o.reshape(T, H * HEAD_DIM), l.reshape(T, H * HEAD_DIM)


def _att_combine(outs, lses):
    T, W = outs[0].shape
    ng = len(outs)
    tr, tcol = _tile(T, 512, 8), _tile(W, 512)

    def body(*refs):
        o_refs, l_refs = refs[:ng], refs[ng:2 * ng]
        ob_ref, lse_ref = refs[2 * ng:]
        ls = [r[...] for r in l_refs]
        m = functools.reduce(jnp.maximum, ls)
        es = [jnp.exp(l - m) for l in ls]
        den = functools.reduce(lambda a, b: a + b, es)
        num = functools.reduce(lambda a, b: a + b, [e * o[...] for e, o in zip(es, o_refs)])
        ob_ref[...] = (num / den).astype(BF16)
        lse_ref[...] = m + jnp.log(den)

    spec = pl.BlockSpec((tr, tcol), lambda i, j: (i, j))
    return pl.pallas_call(
        body, grid=(T // tr, W // tcol), in_specs=[spec] * (2 * ng), out_specs=[spec, spec],
        out_shape=[jax.ShapeDtypeStruct((T, W), BF16), jax.ShapeDtypeStruct((T, W), F32)],
        compiler_params=_params(("parallel", "parallel")), name="att_combine")(*outs, *lses)


def _att_bwd(q, k, v, o, do, lse, grp, dil):
    T = q.shape[0]
    H = o.shape[1] // HEAD_DIM
    n, rep, rc, nrc, nblk = _att_geometry(T, H, dil)
    hw = rep * HEAD_DIM
    scale = HEAD_DIM ** -0.5
    q2 = q.reshape(n, dil * q.shape[1])
    k2 = k.reshape(n, dil * k.shape[1])
    v2 = v.reshape(n, dil * v.shape[1])
    o2 = o.reshape(n, dil * o.shape[1])
    do2 = do.reshape(n, dil * do.shape[1])
    l2 = lse.reshape(n, dil * lse.shape[1])
    qcols = q.shape[1] // hw
    ocols = H * HEAD_DIM // hw

    def body(q_ref, k_ref, v_ref, o_ref, do_ref, l_ref, dq_ref, dk_ref, dv_ref):
        ch = pl.program_id(2)

        @pl.when(ch == 0)
        def _():
            dk_ref[...] = jnp.zeros_like(dk_ref)
            dv_ref[...] = jnp.zeros_like(dv_ref)

        def blk(b, _):
            gb = ch * nblk + b
            rows = pl.ds(pl.multiple_of(b * ATT_BLK, ATT_BLK), ATT_BLK)
            kc = pl.ds(pl.multiple_of(gb * ATT_BLK, ATT_BLK), ATT_BLK)
            kp = pl.ds(pl.multiple_of(jnp.maximum(gb - 1, 0) * ATT_BLK, ATT_BLK), ATT_BLK)
            qs = _stack_heads(q_ref, rows, rep)
            dos = _stack_heads(do_ref, rows, rep)
            os_ = _stack_heads(o_ref, rows, rep)
            lss = _stack_heads(l_ref, rows, rep)
            kcat = jnp.concatenate([k_ref[kp, :], k_ref[kc, :]], axis=0)
            vcat = jnp.concatenate([v_ref[kp, :], v_ref[kc, :]], axis=0)
            s = _dot(qs, kcat, NT) * scale
            p = jnp.where(_att_masks(rep, gb), jnp.exp(s - jnp.concatenate([lss, lss], axis=1)), 0.0)
            delta = jnp.sum(dos.astype(F32) * os_.astype(F32), axis=-1, keepdims=True)
            dp = _dot(dos, vcat, NT)
            ds = (p * (dp - delta) * scale).astype(BF16)
            dvc = _dot(p.astype(BF16), dos, TN)
            dkc = _dot(ds, qs, TN)
            dqs = _dot(ds, kcat)
            for j in range(rep):
                dq_ref[rows, j * HEAD_DIM:(j + 1) * HEAD_DIM] = dqs[j * ATT_BLK:(j + 1) * ATT_BLK].astype(BF16)
            dk_ref[kc, :] += dkc[ATT_BLK:]
            dv_ref[kc, :] += dvc[ATT_BLK:]

            @pl.when(gb > 0)
            def _():
                dk_ref[kp, :] += dkc[:ATT_BLK]
                dv_ref[kp, :] += dvc[:ATT_BLK]

            return 0

        lax.fori_loop(0, nblk, blk, 0)

    qspec = pl.BlockSpec((rc, hw), lambda r, h, c: (c, r * qcols + grp * N_KV_HEADS + h))
    kspec = pl.BlockSpec((n, HEAD_DIM), lambda r, h, c: (0, r * N_KV_HEADS + h))
    ospec = pl.BlockSpec((rc, hw), lambda r, h, c: (c, r * ocols + h))
    dq, dk, dv = pl.pallas_call(
        body, grid=(dil, N_KV_HEADS, nrc), in_specs=[qspec, kspec, kspec, ospec, ospec, ospec],
        out_specs=[ospec, kspec, kspec],
        out_shape=[jax.ShapeDtypeStruct((n, dil * H * HEAD_DIM), BF16),
                   jax.ShapeDtypeStruct((n, dil * N_KV_HEADS * HEAD_DIM), F32),
                   jax.ShapeDtypeStruct((n, dil * N_KV_HEADS * HEAD_DIM), F32)],
        compiler_params=_params(("parallel", "parallel", "arbitrary")), name=f"att_bwd_d{dil}",
    )(q2, k2, v2, o2, do2, l2)
    return (dq.reshape(T, H * HEAD_DIM), dk.reshape(T, N_KV_HEADS * HEAD_DIM),
            dv.reshape(T, N_KV_HEADS * HEAD_DIM))


def _sum_kv(dks, dvs):
    T, W = dks[0].shape
    ng = len(dks)
    tr = _tile(T, 512, 8)

    def body(*refs):
        o_ref = refs[2 * ng]
        o_ref[0] = functools.reduce(lambda a, b: a + b, [r[...] for r in refs[:ng]]).astype(BF16)
        o_ref[1] = functools.reduce(lambda a, b: a + b, [r[...] for r in refs[ng:2 * ng]]).astype(BF16)

    spec = pl.BlockSpec((tr, W), lambda i: (i, 0))
    return pl.pallas_call(
        body, grid=(T // tr,), in_specs=[spec] * (2 * ng),
        out_specs=pl.BlockSpec((2, tr, W), lambda i: (0, i, 0)),
        out_shape=jax.ShapeDtypeStruct((2, T, W), BF16),
        compiler_params=_params(("parallel",)), name="sum_kv")(*dks, *dvs)


def _local_step(x, tgt, w, wts, ready, dep0=None):
    T, D = x.shape
    g = {}

    (u0,) = _rms_fwd("rms_a", x, [w["a_norm"]], [F32], dep=dep0)
    lbr, lbi, bbt_re, bbt_im = _s5_param_fwd(w["lam_re"], w["lam_im"], w["log_dt"], w["bt_re"], w["bt_im"])
    a_re, a_im = lbr.reshape(-1, 8, 128), lbi.reshape(-1, 8, 128)
    bbr = _block_diag(bbt_re.transpose(1, 0, 2), True).astype(BF16)
    bbi = _block_diag(bbt_im.transpose(1, 0, 2), True).astype(BF16)
    ccr = _block_diag(w["c_re"], False).astype(BF16)
    cci = _block_diag(w["c_im"], False).astype(BF16)
    z, h0 = _s5_fwd(u0, bbr, bbi, ccr, cci, a_re, a_im, w["s5_d"])
    w_glu = wts("glu", z)
    x1, vg = _mm_dual_fwd("glu_fwd", z, w_glu, x, "glu")

    def ffn_fwd(xin, layer):
        (nrm,) = _rms_fwd(f"rms_f{layer}", xin, [w["ffn_norm"][layer:layer + 1]], [BF16])
        w_in = wts(f"win{layer}", nrm)
        gu, act = _mm_dual_fwd(f"ffn_in{layer}", nrm, w_in, None, "ffn")
        w_out = wts(f"wout{layer}", act)
        xout = _mm_nn(f"ffn_out{layer}", act, w_out, res=xin)
        return xout, (nrm, gu, act, w_in, w_out)

    x2, saved0 = ffn_fwd(x1, 0)
    kvn, hb = _rms_fwd("rms_b", x2, [w["kv_norm"], w["b_norm"]], [BF16, BF16])
    w_kv, w_q, w_o = wts("wkv", hb), wts("wq", hb), wts("wo", hb)
    k, v = _mm_kv("kv_proj", kvn, w_kv)
    q = _mm_nn("q_proj", hb, w_q, out_dtype=BF16)
    outs, lses = [], []
    for grp, dil in enumerate(DILATIONS):
        o_g, l_g = _att_fwd(q, k, v, grp, dil)
        outs.append(o_g)
        lses.append(l_g)
    o, lse = _att_combine(outs, lses)
    x3 = _mm_nn("o_proj", o, w_o, res=x2)
    x4, saved1 = ffn_fwd(x3, 1)
    loss_blk, dx4, dx4b, g["final_norm"] = _loss_head(x4, w["final_norm"], tgt)

    def ffn_bwd(dx, dxb, xin, saved, layer, dep):
        nrm, gu, act, w_in, w_out = saved
        dgu = _mm_nt_ffn_bwd(f"ffn_dact{layer}", dxb, w_out, gu, dep=dep)
        g_wout = _mm_tn(f"ffn_dwout{layer}", act, dxb)
        g_win = _mm_tn_pair(f"ffn_dwin{layer}", nrm, dgu)
        dn = _mm_nt_pair(f"ffn_dn{layer}", dgu, w_in)
        dxo, dxob, (dgn,) = _rms_bwd(f"rms_f{layer}_bwd", xin, dx, [w["ffn_norm"][layer:layer + 1]], [dn])
        tok = ready({f"win{layer}": g_win, f"wout{layer}": g_wout})
        return dxo, dxob, dgn, tok

    dx3, dx3b, dfn1, tok = ffn_bwd(dx4, dx4b, x3, saved1, 1, None)
    do = _mm_nt("o_proj_dx", [dx3b], w_o, out_dtype=BF16, dep=tok)
    g_wo = _mm_tn("o_proj_dw", o, dx3b)
    dqs, dks, dvs = [], [], []
    for grp, dil in enumerate(DILATIONS):
        dq_g, dk_g, dv_g = _att_bwd(q, k, v, o, do, lse, grp, dil)
        dqs.append(dq_g)
        dks.append(dk_g)
        dvs.append(dv_g)
    dkv = _sum_kv(dks, dvs)
    dhb = _mm_nt("q_proj_dx", dqs, w_q)
    g_wq = _mm_tn("q_proj_dw", hb, jnp.concatenate(dqs, axis=1))
    dkvn = _mm_nt_pair("kv_proj_dx", dkv, w_kv)
    g_wkv = _mm_tn_pair("kv_proj_dw", kvn, dkv)
    dx2, dx2b, (g["kv_norm"], g["b_norm"]) = _rms_bwd(
        "rms_b_bwd", x2, dx3, [w["kv_norm"], w["b_norm"]], [dkvn, dhb])
    tok = ready({"wkv": g_wkv, "wq": g_wq, "wo": g_wo})
    dx1, dx1b, dfn0, tok = ffn_bwd(dx2, dx2b, x1, saved0, 0, tok)
    g["ffn_norm"] = jnp.concatenate([dfn0, dfn1], axis=0)

    dvg = _glu_bwd(dx1, vg, dep=tok)
    dz = _mm_nt_pair("glu_dx", dvg, w_glu)
    ready({"glu": _mm_tn_pair("glu_dw", z, dvg)})
    du, g["s5_d"], da_re, da_im, dbbr, dbbi, dccr, dcci = _s5_bwd(
        u0, dz, h0, bbr, bbi, ccr, cci, a_re, a_im, w["s5_d"])
    C, G, P = w["bt_re"].shape
    g["c_re"] = _block_diag_extract(dccr, False, C, P)
    g["c_im"] = -_block_diag_extract(dcci, False, C, P)
    g["lam_re"], g["lam_im"], g["log_dt"], g["bt_re"], g["bt_im"] = _s5_param_bwd(
        w["lam_re"], w["lam_im"], w["log_dt"], w["bt_re"], w["bt_im"],
        da_re.reshape(G, P), da_im.reshape(G, P),
        _block_diag_extract(dbbr, True, C, P).transpose(1, 0, 2),
        _block_diag_extract(dbbi, True, C, P).transpose(1, 0, 2))
    grad_x, _, (g["a_norm"],) = _rms_bwd("rms_a_bwd", x, dx1, [w["a_norm"]], [du])
    return loss_blk, grad_x, g


def _coords():
    return lax.axis_index("x"), lax.axis_index("y"), lax.axis_index("c")


def _dev_index(dev):
    return 4 * dev[0] + 2 * dev[1] + dev[2]


def _shard_window(ref, axis, width, idx):
    sl = [slice(None)] * len(ref.shape)
    sl[axis] = pl.ds(pl.multiple_of(idx * width, width), width)
    return ref.at[tuple(sl)]


def _all_gather(name, shards, axes):
    na = len(shards)
    widths = [s.shape[ax] for s, ax in zip(shards, axes)]
    out_shapes = []
    for s, ax in zip(shards, axes):
        shp = list(s.shape)
        shp[ax] *= N_DEV
        out_shapes.append(jax.ShapeDtypeStruct(tuple(shp), s.dtype))

    def body(*refs):
        ins, outs = refs[:na], refs[na:2 * na]
        send_sems, recv_sems, local_sems = refs[2 * na:]
        x, y, c = _coords()
        me, sib = (x, y, c), (x, y, 1 - c)
        chips = [(1 - x, y), (x, 1 - y), (1 - x, 1 - y)]

        def blk(a, dev):
            return _shard_window(outs[a], axes[a], widths[a], _dev_index(dev))

        def copy(a, kk, block, to, src=None):
            return pltpu.make_async_remote_copy(
                src_ref=blk(a, block) if src is None else src, dst_ref=blk(a, block),
                send_sem=send_sems.at[a, kk], recv_sem=recv_sems.at[a, kk],
                device_id=to, device_id_type=MESH)

        local = [pltpu.make_async_copy(ins[a], blk(a, me), local_sems.at[a]) for a in range(na)]
        for cp in local:
            cp.start()
        sent = []
        for a in range(na):
            first = [copy(a, 0, me, sib, src=ins[a])]
            first += [copy(a, 1 + j, me, (*chip, c), src=ins[a]) for j, chip in enumerate(chips)]
            for cp in first:
                cp.start()
            sent += first
        for a in range(na):
            for j, chip in enumerate(chips):
                copy(a, 1 + j, (*chip, c), me).wait_recv()
                fwd = copy(a, 4 + j, (*chip, c), sib)
                fwd.start()
                sent.append(fwd)
        for a in range(na):
            copy(a, 0, sib, me).wait_recv()
            for j, chip in enumerate(chips):
                copy(a, 4 + j, (*chip, 1 - c), me).wait_recv()
        for cp in sent:
            cp.wait_send()
        for cp in local:
            cp.wait()

    return pl.pallas_call(
        body, out_shape=out_shapes, in_specs=[ANY] * na, out_specs=[ANY] * na,
        scratch_shapes=[pltpu.SemaphoreType.DMA((na, 7)), pltpu.SemaphoreType.DMA((na, 7)),
                        pltpu.SemaphoreType.DMA((na,))],
        name=name)(*shards)


HBM = pl.BlockSpec(memory_space=pltpu.HBM)
SEM = pl.BlockSpec(memory_space=pltpu.SEMAPHORE)
TOKEN_SPEC = pl.BlockSpec(memory_space=pltpu.VMEM)
TOKEN_SHAPE = jax.ShapeDtypeStruct((8, 128), F32)
SPLIT_PARAMS = pltpu.CompilerParams(has_side_effects=pltpu.SideEffectType.DATAFLOW_SIDE_EFFECTING)


def _hbm(x):
    return pltpu.with_memory_space_constraint(x, pltpu.HBM)


def _hbm_like(x):
    return pltpu.HBM(x.shape, x.dtype)


def _dma_sems(*shape):
    return pltpu.SemaphoreType.DMA(shape)


def _cast_and_place(name, shard, axis, pos):
    rows, cols = shard.shape
    tr = _tile(rows, 256, 16)
    nt = rows // tr
    full = (rows, cols * N_DEV) if axis == 1 else (rows * N_DEV, cols)

    def dev(p):
        return 4 * p[0] + 2 * p[1] + p[2]

    def body(pos_ref, s_ref, b_ref, l_ref):
        v = s_ref[...].astype(BF16)
        b_ref[...] = v
        l_ref[...] = v

    blk = pl.BlockSpec((tr, cols), lambda i, p: (i, 0))
    if axis == 1:
        lspec = pl.BlockSpec((tr, cols), lambda i, p: (i, dev(p)))
    else:
        lspec = pl.BlockSpec((tr, cols), lambda i, p: (dev(p) * nt + i, 0))
    return pl.pallas_call(
        body, grid_spec=pltpu.PrefetchScalarGridSpec(
            num_scalar_prefetch=1, grid=(nt,), in_specs=[blk], out_specs=[blk, lspec]),
        out_shape=[jax.ShapeDtypeStruct((rows, cols), BF16), jax.ShapeDtypeStruct(full, BF16)],
        compiler_params=_params(("parallel",)), name=name)(pos, shard)


def _gather_start(shards, lands, axes, groups):
    na, ng = len(shards), len(groups)
    widths = [s.shape[ax] for s, ax in zip(shards, axes)]

    def body(*refs):
        sh, ld = refs[:na], refs[na:2 * na]
        sems = refs[2 * na:2 * na + 3 * ng]
        token = refs[-1]
        x, y, c = _coords()
        me, sib = (x, y, c), (x, y, 1 - c)
        chips = [(1 - x, y), (x, 1 - y), (1 - x, 1 - y)]
        for gi, grp in enumerate(groups):
            send, r_d2d, r_ici = sems[3 * gi:3 * gi + 3]
            for li, a in enumerate(grp):
                dst = _shard_window(ld[a], axes[a], widths[a], _dev_index(me))
                pltpu.make_async_remote_copy(
                    src_ref=sh[a], dst_ref=dst, send_sem=send.at[4 * li], recv_sem=r_d2d.at[li],
                    device_id=sib, device_id_type=MESH).start()
                for j, chip in enumerate(chips):
                    pltpu.make_async_remote_copy(
                        src_ref=sh[a], dst_ref=dst, send_sem=send.at[4 * li + 1 + j], recv_sem=r_ici.at[3 * li + j],
                        device_id=(*chip, c), device_id_type=MESH).start()
        token[...] = jnp.zeros_like(token)

    out_shape, out_specs = [], []
    for grp in groups:
        out_shape += [_dma_sems(4 * len(grp)), _dma_sems(len(grp)), _dma_sems(3 * len(grp))]
        out_specs += [SEM] * 3
    out_shape += [_hbm_like(s) for s in shards] + [_hbm_like(l) for l in lands] + [TOKEN_SHAPE]
    out_specs += [HBM] * (2 * na) + [TOKEN_SPEC]
    aliases = {a: 3 * ng + a for a in range(2 * na)}
    res = pl.pallas_call(
        body, name="weights_gather_start", out_shape=out_shape, in_specs=[HBM] * (2 * na),
        out_specs=out_specs, input_output_aliases=aliases, compiler_params=SPLIT_PARAMS,
    )(*[_hbm(s) for s in shards], *[_hbm(l) for l in lands])
    sems = [tuple(res[3 * gi:3 * gi + 3]) for gi in range(ng)]
    return sems, list(res[3 * ng:3 * ng + na]), list(res[3 * ng + na:3 * ng + 2 * na]), res[-1]


def _gather_forward(name, lands, axes, r_ici, after):
    n = len(lands)
    widths = [l.shape[ax] // N_DEV for l, ax in zip(lands, axes)]

    def body(*refs):
        ld, r_ici_ref = refs[:n], refs[n]
        f_send, f_recv = refs[n + 2], refs[n + 3]
        x, y, c = _coords()
        sib = (x, y, 1 - c)
        chips = [(1 - x, y), (x, 1 - y), (1 - x, 1 - y)]
        for li in range(n):
            for j, chip in enumerate(chips):
                blk = _shard_window(ld[li], axes[li], widths[li], _dev_index((*chip, c)))
                pltpu.make_async_remote_copy(
                    src_ref=blk, dst_ref=blk, send_sem=f_send.at[3 * li + j], recv_sem=r_ici_ref.at[3 * li + j],
                    device_id=(*chip, c), device_id_type=MESH).wait_recv()
                pltpu.make_async_remote_copy(
                    src_ref=blk, dst_ref=blk, send_sem=f_send.at[3 * li + j], recv_sem=f_recv.at[3 * li + j],
                    device_id=sib, device_id_type=MESH).start()

    res = pl.pallas_call(
        body, name=name, out_shape=[_dma_sems(3 * n), _dma_sems(3 * n)] + [_hbm_like(l) for l in lands],
        in_specs=[HBM] * n + [SEM, ANY], out_specs=[SEM, SEM] + [HBM] * n,
        input_output_aliases={li: 2 + li for li in range(n)}, compiler_params=SPLIT_PARAMS,
    )(*lands, r_ici, after)
    return res[0], res[1], list(res[2:])


def _gather_finish(name, shards, lands, axes, send, r_d2d, f_send, f_recv):
    n = len(lands)
    widths = [l.shape[ax] // N_DEV for l, ax in zip(lands, axes)]

    def body(*refs):
        sh, ld = refs[:n], refs[n:2 * n]
        send_ref, r_d2d_ref, f_send_ref, f_recv_ref = refs[2 * n:2 * n + 4]
        x, y, c = _coords()
        me, sib = (x, y, c), (x, y, 1 - c)
        chips = [(1 - x, y), (x, 1 - y), (1 - x, 1 - y)]

        def blk(li, dev):
            return _shard_window(ld[li], axes[li], widths[li], _dev_index(dev))

        for li in range(n):
            for kk in range(4):
                pltpu.make_async_remote_copy(
                    src_ref=sh[li], dst_ref=blk(li, me), send_sem=send_ref.at[4 * li + kk], recv_sem=r_d2d_ref.at[li],
                    device_id=sib, device_id_type=MESH).wait_send()
            pltpu.make_async_remote_copy(
                src_ref=blk(li, sib), dst_ref=blk(li, sib), send_sem=send_ref.at[4 * li], recv_sem=r_d2d_ref.at[li],
                device_id=sib, device_id_type=MESH).wait_recv()
            for j, chip in enumerate(chips):
                pltpu.make_async_remote_copy(
                    src_ref=blk(li, (*chip, c)), dst_ref=blk(li, (*chip, c)), send_sem=f_send_ref.at[3 * li + j],
                    recv_sem=f_recv_ref.at[3 * li + j], device_id=sib, device_id_type=MESH).wait_send()
                pltpu.make_async_remote_copy(
                    src_ref=blk(li, (*chip, 1 - c)), dst_ref=blk(li, (*chip, 1 - c)), send_sem=f_send_ref.at[3 * li + j],
                    recv_sem=f_recv_ref.at[3 * li + j], device_id=sib, device_id_type=MESH).wait_recv()

    res = pl.pallas_call(
        body, name=name, out_shape=[_hbm_like(s) for s in shards] + [_hbm_like(l) for l in lands],
        in_specs=[HBM] * (2 * n) + [SEM] * 4, out_specs=[HBM] * (2 * n),
        input_output_aliases={i: i for i in range(2 * n)}, compiler_params=SPLIT_PARAMS,
    )(*shards, *lands, send, r_d2d, f_send, f_recv)
    return list(res[n:])


def _chip_exchange_start(name, parts):
    n = len(parts)

    def body(*refs):
        src, ld = refs[:n], refs[n:2 * n]
        send, recv = refs[2 * n], refs[2 * n + 1]
        token = refs[-1]
        x, y, c = _coords()
        chips = [(1 - x, y), (x, 1 - y), (1 - x, 1 - y)]
        for li in range(n):
            for kk, chip in enumerate(chips):
                pltpu.make_async_remote_copy(
                    src_ref=src[li].at[kk], dst_ref=ld[li].at[kk], send_sem=send.at[3 * li + kk],
                    recv_sem=recv.at[3 * li + kk], device_id=(*chip, c), device_id_type=MESH).start()
        token[...] = jnp.zeros_like(token)

    lands = [lax.empty(p.shape, p.dtype) for p in parts]
    res = pl.pallas_call(
        body, name=name,
        out_shape=[_dma_sems(3 * n), _dma_sems(3 * n)] + [_hbm_like(p) for p in parts] * 2 + [TOKEN_SHAPE],
        in_specs=[HBM] * (2 * n), out_specs=[SEM, SEM] + [HBM] * (2 * n) + [TOKEN_SPEC],
        input_output_aliases={i: 2 + i for i in range(2 * n)}, compiler_params=SPLIT_PARAMS,
    )(*[_hbm(p) for p in parts], *[_hbm(l) for l in lands])
    return res[0], res[1], list(res[2:2 + n]), list(res[2 + n:2 + 2 * n]), res[-1]


def _chip_exchange_finish(name, started, after):
    counts = [len(st[2]) for st in started]
    total = sum(counts)
    ns = len(started)

    def body(*refs):
        src, ld = refs[:total], refs[total:2 * total]
        sems = refs[2 * total:2 * total + 2 * ns]
        x, y, c = _coords()
        chips = [(1 - x, y), (x, 1 - y), (1 - x, 1 - y)]
        off = 0
        for si, cnt in enumerate(counts):
            send, recv = sems[2 * si], sems[2 * si + 1]
            for li in range(cnt):
                for kk, chip in enumerate(chips):
                    cp = pltpu.make_async_remote_copy(
                        src_ref=src[off + li].at[kk], dst_ref=ld[off + li].at[kk], send_sem=send.at[3 * li + kk],
                        recv_sem=recv.at[3 * li + kk], device_id=(*chip, c), device_id_type=MESH)
                    cp.wait_send()
                    cp.wait_recv()
            off += cnt

    srcs = [p for st in started for p in st[2]]
    lands = [l for st in started for l in st[3]]
    sems = [s for st in started for s in st[:2]]
    res = pl.pallas_call(
        body, name=name, out_shape=[_hbm_like(p) for p in srcs + lands],
        in_specs=[HBM] * (2 * total) + [SEM] * (2 * ns) + [ANY], out_specs=[HBM] * (2 * total),
        input_output_aliases={i: i for i in range(2 * total)}, compiler_params=SPLIT_PARAMS,
    )(*srcs, *lands, *sems, after)
    out, off = [], total
    for cnt in counts:
        out.append(list(res[off:off + cnt]))
        off += cnt
    return out


def _pair_exchange(name, grads, axes):
    na = len(grads)
    widths = [gr.shape[ax] // N_DEV for gr, ax in zip(grads, axes)]
    out_shapes = []
    for gr, ax, wd in zip(grads, axes, widths):
        shp = list(gr.shape)
        shp[ax] = wd
        out_shapes.append(jax.ShapeDtypeStruct((4, *shp), gr.dtype))

    def body(*refs):
        ins, outs = refs[:na], refs[na:2 * na]
        send_sems, recv_sems = refs[2 * na:]
        x, y, c = _coords()
        sib = (x, y, 1 - c)
        chips = [(x, y), (1 - x, y), (x, 1 - y), (1 - x, 1 - y)]
        cps = []
        for a in range(na):
            for kk, chip in enumerate(chips):
                cp = pltpu.make_async_remote_copy(
                    src_ref=_shard_window(ins[a], axes[a], widths[a], _dev_index((*chip, 1 - c))),
                    dst_ref=outs[a].at[kk], send_sem=send_sems.at[a, kk], recv_sem=recv_sems.at[a, kk],
                    device_id=sib, device_id_type=MESH)
                cp.start()
                cps.append(cp)
        for cp in cps:
            cp.wait()

    return pl.pallas_call(
        body, out_shape=out_shapes, in_specs=[ANY] * na, out_specs=[ANY] * na,
        scratch_shapes=[pltpu.SemaphoreType.DMA((na, 4)), pltpu.SemaphoreType.DMA((na, 4))],
        name=name)(*grads)


def _pair_sum(name, grad, land, axis, pos):
    wd = grad.shape[axis] // N_DEV
    shard_shape = land.shape[1:]
    rows, cols = shard_shape
    tr = _tile(rows, 256, 16)
    nt = rows // tr

    def dev_of(kk, pos_ref):
        return 4 * (pos_ref[0] ^ (kk & 1)) + 2 * (pos_ref[1] ^ (kk >> 1)) + pos_ref[2]

    if axis == 1:
        gspec = pl.BlockSpec((tr, wd), lambda kk, t, p: (t, dev_of(kk, p)))
    else:
        gspec = pl.BlockSpec((tr, cols), lambda kk, t, p: (dev_of(kk, p) * nt + t, 0))
    lspec = pl.BlockSpec((None, tr, cols), lambda kk, t, p: (kk, t, 0))
    p0spec = pl.BlockSpec((tr, cols), lambda kk, t, p: (jnp.where(kk == 0, t, nt - 1), 0))
    pbspec = pl.BlockSpec((None, tr, cols), lambda kk, t, p: (jnp.maximum(kk - 1, 0), jnp.where(kk == 0, 0, t), 0))

    def body(pos_ref, g_ref, l_ref, p0_ref, pb_ref):
        kk = pl.program_id(0)
        s = g_ref[...].astype(F32) + l_ref[...].astype(F32)

        @pl.when(kk == 0)
        def _():
            p0_ref[...] = s

        @pl.when(kk > 0)
        def _():
            pb_ref[...] = s.astype(BF16)

    return pl.pallas_call(
        body,
        grid_spec=pltpu.PrefetchScalarGridSpec(
            num_scalar_prefetch=1, grid=(4, nt), in_specs=[gspec, lspec], out_specs=[p0spec, pbspec]),
        out_shape=[jax.ShapeDtypeStruct(shard_shape, F32), jax.ShapeDtypeStruct((3, *shard_shape), BF16)],
        compiler_params=_params(("arbitrary", "arbitrary")), name=name)(pos, grad, land)


def _adamw(name, parts, w, m, v):
    rows, cols = w.shape
    tr = _tile(rows, 256, 8)
    npart = len(parts)
    c1 = 1.0 - ADAM_B1 ** ADAM_STEP
    c2 = 1.0 - ADAM_B2 ** ADAM_STEP

    def body(*refs):
        p_refs = refs[:npart]
        w_ref, m_ref, v_ref, g_ref, d_ref, nm_ref, nv_ref = refs[npart:]
        g = None
        for r in p_refs:
            if len(r.shape) == 3:
                for i in range(r.shape[0]):
                    t = r[i].astype(F32)
                    g = t if g is None else g + t
            else:
                t = r[...].astype(F32)
                g = t if g is None else g + t
        nm = ADAM_B1 * m_ref[...] + (1.0 - ADAM_B1) * g
        nv = ADAM_B2 * v_ref[...] + (1.0 - ADAM_B2) * (g * g)
        g_ref[...] = g
        nm_ref[...] = nm
        nv_ref[...] = nv
        d_ref[...] = -ADAM_LR * ((nm / c1) / (jnp.sqrt(nv / c2) + ADAM_EPS) + ADAM_WD * w_ref[...])

    spec = pl.BlockSpec((tr, cols), lambda i: (i, 0))
    pspecs = [pl.BlockSpec((p.shape[0], tr, cols), lambda i: (0, i, 0)) if p.ndim == 3 else spec
              for p in parts]
    return pl.pallas_call(
        body, grid=(rows // tr,), in_specs=pspecs + [spec] * 3, out_specs=[spec] * 4,
        out_shape=[jax.ShapeDtypeStruct((rows, cols), F32)] * 4,
        compiler_params=_params(("parallel",)), name=name)(*parts, w, m, v)


SMALL_NAMES = ("s5_lam_re", "s5_lam_im", "s5_log_dt", "s5_b_re", "s5_b_im", "s5_c_re", "s5_c_im",
               "ffn_norm", "b_norm_mix", "kv_norm", "final_norm")
SMALL_PAD = 1024


def _pack(parts):
    flat = []
    for p in parts:
        f = p.reshape(-1)
        pad = (-f.shape[0]) % SMALL_PAD
        if pad:
            f = jnp.concatenate([f, jnp.zeros((pad,), f.dtype)])
        flat.append(f)
    return jnp.concatenate(flat).reshape(-1, 128)


def _unpack(packed, shapes):
    flat = packed.reshape(-1)
    out, off = [], 0
    for shp in shapes:
        size = math.prod(shp)
        out.append(flat[off:off + size].reshape(shp))
        off += size + (-size) % SMALL_PAD
    return out


def kernel(x, s5_lam_re, s5_lam_im, s5_log_dt, s5_b_re, s5_b_im, s5_c_re, s5_c_im, s5_d, s5_w_glu, a_norm_mix, ffn_norm, ffn_w_in, ffn_w_out, b_norm_mix, attn_w_q, attn_w_o, kv_norm, w_kv, final_norm, loss_target, m_s5_lam_re, m_s5_lam_im, m_s5_log_dt, m_s5_b_re, m_s5_b_im, m_s5_c_re, m_s5_c_im, m_s5_d, m_s5_w_glu, m_a_norm_mix, m_ffn_norm, m_ffn_w_in, m_ffn_w_out, m_b_norm_mix, m_attn_w_q, m_attn_w_o, m_kv_norm, m_w_kv, m_final_norm, v_s5_lam_re, v_s5_lam_im, v_s5_log_dt, v_s5_b_re, v_s5_b_im, v_s5_c_re, v_s5_c_im, v_s5_d, v_s5_w_glu, v_a_norm_mix, v_ffn_norm, v_ffn_w_in, v_ffn_w_out, v_b_norm_mix, v_attn_w_q, v_attn_w_o, v_kv_norm, v_w_kv, v_final_norm):
    args = dict(locals())
    T, D = x.shape[1], x.shape[2]
    n_layers = ffn_w_in.shape[0]
    xi_, yi_, ci_ = _coords()
    pos = jnp.stack([xi_, yi_, ci_]).astype(jnp.int32)
    me = 4 * xi_ + 2 * yi_ + ci_

    big_names = ["glu"] + [f"win{l}" for l in range(n_layers)] + [f"wout{l}" for l in range(n_layers)] \
        + ["wkv", "wq", "wo"]
    big_shards = [s5_w_glu[0]] + [ffn_w_in[l] for l in range(n_layers)] \
        + [ffn_w_out[l] for l in range(n_layers)] + [w_kv, attn_w_q[0], attn_w_o[0]]
    big_axes = [1] + [1] * n_layers + [0] * n_layers + [0, 1, 0]
    big_out_names = ["s5_w_glu"] + ["ffn_w_in"] * n_layers + ["ffn_w_out"] * n_layers \
        + ["w_kv", "attn_w_q", "attn_w_o"]
    index_of = {n: i for i, n in enumerate(big_names)}
    vec_shard = jnp.concatenate([s5_d, a_norm_mix], axis=0)
    (vecs,) = _all_gather("vectors_all_gather", [vec_shard], [1])

    gather_groups = [["glu"], ["win0"], ["wout0"], ["wkv", "wq", "wo"], ["win1", "wout1"]]
    group_idx = [[index_of[n] for n in grp] for grp in gather_groups]
    group_of = {n: gi for gi, grp in enumerate(gather_groups) for n in grp}
    placed = [_cast_and_place(f"cast_place_{n}", s, ax, pos) for n, s, ax in zip(big_names, big_shards, big_axes)]
    gather_sems, shards_thru, lands_thru, start_token = _gather_start(
        [p[0] for p in placed], [p[1] for p in placed], big_axes, group_idx)
    full = {}

    def wts(name, after):
        if name not in full:
            gi = group_of[name]
            idx = group_idx[gi]
            axes = [big_axes[a] for a in idx]
            send, r_d2d, r_ici = gather_sems[gi]
            f_send, f_recv, lands = _gather_forward(
                f"weights_gather_forward{gi}", [lands_thru[a] for a in idx], axes, r_ici, after)
            done = _gather_finish(f"weights_gather_finish{gi}", [shards_thru[a] for a in idx], lands, axes,
                                  send, r_d2d, f_send, f_recv)
            full.update(zip(gather_groups[gi], done))
        return full[name]

    exchanges = []

    def ready(grads):
        names = list(grads)
        axes = [big_axes[index_of[n]] for n in names]
        tag = "_".join(names)
        lands = _pair_exchange(f"rs_pair_exchange_{tag}", [grads[n] for n in names], axes)
        p0s, pbs = [], []
        for n, land, ax in zip(names, lands, axes):
            p0, pb = _pair_sum(f"rs_pair_sum_{n}", grads[n], land, ax, pos)
            p0s.append(p0)
            pbs.append(pb)
        started = _chip_exchange_start(f"rs_chip_exchange_start_{tag}", pbs)
        exchanges.append((names, p0s, started))
        return started[4]

    G, P, C = s5_b_re.shape[1:]
    w = dict(
        a_norm=vecs[1:2], s5_d=vecs[0:1],
        ffn_norm=ffn_norm, b_norm=b_norm_mix, kv_norm=kv_norm.reshape(1, D), final_norm=final_norm.reshape(1, D),
        lam_re=s5_lam_re[0], lam_im=s5_lam_im[0], log_dt=s5_log_dt.reshape(G, 1),
        bt_re=s5_b_re[0].transpose(2, 0, 1), bt_im=s5_b_im[0].transpose(2, 0, 1),
        c_re=s5_c_re[0], c_im=s5_c_im[0],
    )
    loss_blk, grad_x, g = _local_step(x[0], loss_target[0], w, wts, ready, dep0=start_token)
    loss = lax.psum(loss_blk[0, 0], ("x", "y", "c"))

    out = {}

    def put(name, res, shape):
        for kind, r in zip(("grad", "delta", "new_m", "new_v"), res):
            out[f"{kind}_{name}"] = r.reshape(shape)

    layered = {}

    def update(names, p0s, recvd):
        first = None
        for name, p0, rc in zip(names, p0s, recvd):
            oname, shard = big_out_names[index_of[name]], big_shards[index_of[name]]
            if oname in ("ffn_w_in", "ffn_w_out"):
                layer = int(name[-1])
                res = _adamw(f"adamw_{name}", [p0, rc], shard, args["m_" + oname][layer], args["v_" + oname][layer])
                layered.setdefault(oname, {})[layer] = res
            else:
                mm, vv = args["m_" + oname], args["v_" + oname]
                res = _adamw(f"adamw_{name}", [p0, rc], shard, mm.reshape(shard.shape), vv.reshape(shard.shape))
                put(oname, res, args[oname].shape)
            first = res[0] if first is None else first
        return first

    early, last = exchanges[:-1], exchanges[-1]
    landed = _chip_exchange_finish("rs_chip_exchange_finish_early", [e[2] for e in early], grad_x)
    marker = None
    for (names, p0s, _), recvd in zip(early, landed):
        marker = update(names, p0s, recvd)
    (recvd,) = _chip_exchange_finish("rs_chip_exchange_finish_last", [last[2]], marker)
    update(last[0], last[1], recvd)
    for oname, per_layer in layered.items():
        put(oname, [jnp.stack([per_layer[l][i] for l in range(n_layers)]) for i in range(4)], args[oname].shape)

    small_g = dict(
        s5_lam_re=g["lam_re"], s5_lam_im=g["lam_im"], s5_log_dt=g["log_dt"],
        s5_b_re=g["bt_re"].transpose(1, 2, 0), s5_b_im=g["bt_im"].transpose(1, 2, 0),
        s5_c_re=g["c_re"], s5_c_im=g["c_im"], ffn_norm=g["ffn_norm"], b_norm_mix=g["b_norm"],
        kv_norm=g["kv_norm"], final_norm=g["final_norm"])
    packed = _pack([small_g[n] for n in SMALL_NAMES] + [g["s5_d"], g["a_norm"]])
    rows = packed.shape[0]
    (all_parts,) = _all_gather("small_grads_all_gather", [packed], [0])
    all_parts = all_parts.reshape(N_DEV, rows, 128)
    n_rep_rows = _pack([small_g[n] for n in SMALL_NAMES]).shape[0]
    w_pack = _pack([args[n] for n in SMALL_NAMES])
    m_pack = _pack([args["m_" + n] for n in SMALL_NAMES])
    v_pack = _pack([args["v_" + n] for n in SMALL_NAMES])
    res = _adamw("adamw_small", [all_parts[:, :n_rep_rows]], w_pack, m_pack, v_pack)
    shapes = [args[n].shape for n in SMALL_NAMES]
    unpacked = [_unpack(r, shapes) for r in res]
    for i, n in enumerate(SMALL_NAMES):
        put(n, [u[i] for u in unpacked], args[n].shape)
    ws = D // N_DEV
    tail = all_parts[:, n_rep_rows:].reshape(N_DEV, 2, D)
    tail = lax.dynamic_slice_in_dim(tail, me * ws, ws, axis=2)
    res = _adamw("adamw_vec", [tail], vec_shard,
                 jnp.concatenate([m_s5_d, m_a_norm_mix], axis=0), jnp.concatenate([v_s5_d, v_a_norm_mix], axis=0))
    put("s5_d", [r[0:1] for r in res], s5_d.shape)
    put("a_norm_mix", [r[1:2] for r in res], a_norm_mix.shape)

    names = ("s5_lam_re", "s5_lam_im", "s5_log_dt", "s5_b_re", "s5_b_im", "s5_c_re", "s5_c_im", "s5_d",
             "s5_w_glu", "a_norm_mix", "ffn_norm", "ffn_w_in", "ffn_w_out", "b_norm_mix", "attn_w_q",
             "attn_w_o", "kv_norm", "w_kv", "final_norm")
    result = [loss, grad_x.reshape(x.shape)]
    for kind in ("grad", "delta", "new_m", "new_v"):
        result += [out[f"{kind}_{n}"] for n in names]
    return tuple(result)
```

```python
import functools
import math

import jax
import jax.numpy as jnp
from jax import lax
from jax.experimental import pallas as pl
from jax.experimental.pallas import tpu as pltpu

F32 = jnp.float32
BF16 = jnp.bfloat16

EPS = 1e-6
NEG_INF = -1e30
HEAD_DIM = 128
N_KV_HEADS = 4
DILATIONS = (1, 4, 16)
ATT_BLK = 128
S5_C = 16
S5_P = 64
S5_GB = 16
S5_CH = S5_GB * S5_C
S5_W = S5_GB * S5_P
S5_UNROLL = 4
N_DEV = 8

ADAM_LR = 0.001
ADAM_B1 = 0.9
ADAM_B2 = 0.999
ADAM_EPS = 1e-08
ADAM_WD = 0.01
ADAM_STEP = 10

VMEM_LIMIT_BYTES = 56 * 1024 * 1024
MESH = pl.DeviceIdType.MESH
ANY = pl.BlockSpec(memory_space=pl.ANY)


def _tile(n, pref, align=128):
    t = (min(pref, n) // align) * align
    while t >= align:
        if n % t == 0:
            return t
        t -= align
    return n


def _params(sem):
    return pltpu.CompilerParams(dimension_semantics=sem, vmem_limit_bytes=VMEM_LIMIT_BYTES)


def _sigmoid(x):
    return 1.0 / (1.0 + jnp.exp(-x))


NN = (((1,), (0,)), ((), ()))
NT = (((1,), (1,)), ((), ()))
TN = (((0,), (0,)), ((), ()))


def _dot(a, b, dims=NN):
    return lax.dot_general(a, b, dims, preferred_element_type=F32)


def _matmul(name, grid, ins, in_specs, products, dims, out_shapes, out_specs, acc_shapes, epilogue):
    n_in, n_out, nk = len(ins), len(out_shapes), grid[2]

    def body(*refs):
        in_refs = refs[:n_in]
        out_refs = refs[n_in:n_in + n_out]
        acc_refs = refs[n_in + n_out:]

        def prods():
            vals = [None] * len(acc_shapes)
            for ai, bi, ci in products:
                d = _dot(in_refs[ai][...].astype(BF16), in_refs[bi][...].astype(BF16), dims)
                vals[ci] = d if vals[ci] is None else vals[ci] + d
            return vals

        if nk == 1:
            epilogue(in_refs, out_refs, prods())
        else:
            k = pl.program_id(2)

            @pl.when(k == 0)
            def _():
                for a in acc_refs:
                    a[...] = jnp.zeros_like(a)

            for a, v in zip(acc_refs, prods()):
                a[...] += v

            @pl.when(k == nk - 1)
            def _():
                epilogue(in_refs, out_refs, [a[...] for a in acc_refs])

    scratch = [] if nk == 1 else [pltpu.VMEM(s, F32) for s in acc_shapes]
    return pl.pallas_call(
        body, grid=grid, in_specs=in_specs, out_specs=out_specs, out_shape=out_shapes,
        scratch_shapes=scratch, compiler_params=_params(("parallel", "parallel", "arbitrary")),
        name=name)(*ins)


def _mm_dual_fwd(name, a, w, res, kind):
    T, K = a.shape
    N = w.shape[1] // 2
    tm, tn = _tile(T, 512), _tile(N, 512)
    nj = N // tn
    grid = (T // tm, nj, 1)
    ins = [a, w, w]
    in_specs = [pl.BlockSpec((tm, K), lambda i, j, k: (i, 0)),
                pl.BlockSpec((K, tn), lambda i, j, k: (0, j)),
                pl.BlockSpec((K, tn), lambda i, j, k: (0, j + nj))]
    pair_spec = pl.BlockSpec((2, tm, tn), lambda i, j, k: (0, i, j))
    tile_spec = pl.BlockSpec((tm, tn), lambda i, j, k: (i, j))
    if kind == "glu":
        ins.append(res)
        in_specs.append(tile_spec)

        def epilogue(in_refs, out_refs, accs):
            val, gate = accs
            out_refs[0][...] = in_refs[3][...] + val * _sigmoid(gate)
            out_refs[1][0] = val.astype(BF16)
            out_refs[1][1] = gate.astype(BF16)

        out_shapes = [jax.ShapeDtypeStruct((T, N), F32), jax.ShapeDtypeStruct((2, T, N), BF16)]
        out_specs = [tile_spec, pair_spec]
    else:
        def epilogue(in_refs, out_refs, accs):
            g, u = accs
            out_refs[0][0] = g.astype(BF16)
            out_refs[0][1] = u.astype(BF16)
            out_refs[1][...] = (g * _sigmoid(g) * u).astype(BF16)

        out_shapes = [jax.ShapeDtypeStruct((2, T, N), BF16), jax.ShapeDtypeStruct((T, N), BF16)]
        out_specs = [pair_spec, tile_spec]
    return _matmul(name, grid, ins, in_specs, [(0, 1, 0), (0, 2, 1)], NN, out_shapes, out_specs,
                   [(tm, tn), (tm, tn)], epilogue)


def _mm_kv(name, a, w):
    T, K = a.shape
    N = w.shape[1] // 2
    tm, tn = _tile(T, 512), _tile(N, 512)
    nj = N // tn
    tile_spec = pl.BlockSpec((tm, tn), lambda i, j, k: (i, j))

    def epilogue(in_refs, out_refs, accs):
        out_refs[0][...] = accs[0]
        out_refs[1][...] = accs[1]

    return _matmul(name, (T // tm, nj, 1), [a, w, w],
                   [pl.BlockSpec((tm, K), lambda i, j, k: (i, 0)),
                    pl.BlockSpec((K, tn), lambda i, j, k: (0, j)),
                    pl.BlockSpec((K, tn), lambda i, j, k: (0, j + nj))],
                   [(0, 1, 0), (0, 2, 1)], NN,
                   [jax.ShapeDtypeStruct((T, N), F32)] * 2, [tile_spec, tile_spec],
                   [(tm, tn), (tm, tn)], epilogue)


def _mm_nn(name, a, w, res=None, out_dtype=F32):
    T, K = a.shape
    N = w.shape[1]
    tm, tn, tk = _tile(T, 1024), _tile(N, 1024), _tile(K, 2048 if K <= 2048 else 512)
    grid = (T // tm, N // tn, K // tk)
    tile_spec = pl.BlockSpec((tm, tn), lambda i, j, k: (i, j))
    ins = [a, w]
    in_specs = [pl.BlockSpec((tm, tk), lambda i, j, k: (i, k)),
                pl.BlockSpec((tk, tn), lambda i, j, k: (k, j))]
    if res is not None:
        ins.append(res)
        in_specs.append(tile_spec)

    def epilogue(in_refs, out_refs, accs):
        v = accs[0]
        if res is not None:
            v = v + in_refs[2][...]
        out_refs[0][...] = v.astype(out_dtype)

    return _matmul(name, grid, ins, in_specs, [(0, 1, 0)], NN,
                   [jax.ShapeDtypeStruct((T, N), out_dtype)], [tile_spec], [(tm, tn)], epilogue)[0]


def _dep_operand(ins, in_specs, dep):
    if dep is not None:
        ins.append(dep)
        in_specs.append(pl.BlockSpec((8, 128), lambda *_: (0, 0)))


def _mm_nt(name, a_list, w, out_dtype=F32, dep=None):
    T, Np = a_list[0].shape
    Ko = w.shape[0]
    n_parts = len(a_list)
    tm, tn, tk = _tile(T, 1024), _tile(Ko, 1024), _tile(Np, 512)
    nkp = Np // tk
    grid = (T // tm, Ko // tn, nkp)
    ins = list(a_list) + [w] * n_parts
    in_specs = [pl.BlockSpec((tm, tk), lambda i, j, k: (i, k)) for _ in a_list]
    in_specs += [pl.BlockSpec((tn, tk), functools.partial(lambda i, j, k, p: (j, p * nkp + k), p=p))
                 for p in range(n_parts)]
    products = [(p, n_parts + p, 0) for p in range(n_parts)]
    _dep_operand(ins, in_specs, dep)

    def epilogue(in_refs, out_refs, accs):
        out_refs[0][...] = accs[0].astype(out_dtype)

    return _matmul(name, grid, ins, in_specs, products, NT,
                   [jax.ShapeDtypeStruct((T, Ko), out_dtype)],
                   [pl.BlockSpec((tm, tn), lambda i, j, k: (i, j))], [(tm, tn)], epilogue)[0]


def _mm_nt_pair(name, a3, w):
    _, T, N = a3.shape
    Ko = w.shape[0]
    tm, tn, tk = _tile(T, 1024), _tile(Ko, 1024), _tile(N, 512)
    nkh = N // tk
    grid = (T // tm, Ko // tn, 2 * nkh)

    def epilogue(in_refs, out_refs, accs):
        out_refs[0][...] = accs[0]

    return _matmul(name, grid, [a3, w],
                   [pl.BlockSpec((None, tm, tk), lambda i, j, k: (k // nkh, i, k % nkh)),
                    pl.BlockSpec((tn, tk), lambda i, j, k: (j, k))],
                   [(0, 1, 0)], NT, [jax.ShapeDtypeStruct((T, Ko), F32)],
                   [pl.BlockSpec((tm, tn), lambda i, j, k: (i, j))], [(tm, tn)], epilogue)[0]


def _mm_nt_ffn_bwd(name, dx, w_out, gu, dep=None):
    T, D = dx.shape
    Fh = w_out.shape[0]
    tm, tn = _tile(T, 512), _tile(Fh, 512)
    pair_spec = pl.BlockSpec((2, tm, tn), lambda i, j, k: (0, i, j))

    def epilogue(in_refs, out_refs, accs):
        da = accs[0]
        g = in_refs[2][0].astype(F32)
        u = in_refs[2][1].astype(F32)
        s = _sigmoid(g)
        out_refs[0][0] = (da * u * (s * (1.0 + g * (1.0 - s)))).astype(BF16)
        out_refs[0][1] = (da * (g * s)).astype(BF16)

    ins = [dx, w_out, gu]
    in_specs = [pl.BlockSpec((tm, D), lambda i, j, k: (i, 0)),
                pl.BlockSpec((tn, D), lambda i, j, k: (j, 0)),
                pair_spec]
    _dep_operand(ins, in_specs, dep)
    return _matmul(name, (T // tm, Fh // tn, 1), ins, in_specs,
                   [(0, 1, 0)], NT, [jax.ShapeDtypeStruct((2, T, Fh), BF16)], [pair_spec],
                   [(tm, tn)], epilogue)[0]


def _mm_tn(name, a, d):
    T, Ko = a.shape
    N = d.shape[1]
    to, tn, tk = _tile(Ko, 1024), _tile(N, 1024), _tile(T, 1024)
    grid = (Ko // to, N // tn, T // tk)

    def epilogue(in_refs, out_refs, accs):
        out_refs[0][...] = accs[0].astype(BF16)

    return _matmul(name, grid, [a, d],
                   [pl.BlockSpec((tk, to), lambda i, j, k: (k, i)),
                    pl.BlockSpec((tk, tn), lambda i, j, k: (k, j))],
                   [(0, 1, 0)], TN, [jax.ShapeDtypeStruct((Ko, N), BF16)],
                   [pl.BlockSpec((to, tn), lambda i, j, k: (i, j))], [(to, tn)], epilogue)[0]


def _mm_tn_pair(name, a, d3):
    T, Ko = a.shape
    N = d3.shape[2]
    to, tn, tk = _tile(Ko, 2048), _tile(N, 512), _tile(T, 1024)
    njh = N // tn
    grid = (Ko // to, 2 * njh, T // tk)

    def epilogue(in_refs, out_refs, accs):
        out_refs[0][...] = accs[0].astype(BF16)

    return _matmul(name, grid, [a, d3],
                   [pl.BlockSpec((tk, to), lambda i, j, k: (k, i)),
                    pl.BlockSpec((None, tk, tn), lambda i, j, k: (j // njh, k, j % njh))],
                   [(0, 1, 0)], TN, [jax.ShapeDtypeStruct((Ko, 2 * N), BF16)],
                   [pl.BlockSpec((to, tn), lambda i, j, k: (i, j))], [(to, tn)], epilogue)[0]


def _rms_fwd(name, x, gains, dtypes, dep=None):
    T, D = x.shape
    n = len(gains)
    tr = _tile(T, 512, 8)
    n_dep = 0 if dep is None else 1

    def body(x_ref, *refs):
        xv = x_ref[...]
        xr = xv * lax.rsqrt(jnp.mean(xv * xv, axis=-1, keepdims=True) + EPS)
        for g_ref, o_ref in zip(refs[:n], refs[n + n_dep:]):
            o_ref[...] = (xr * g_ref[...]).astype(o_ref.dtype)

    row = pl.BlockSpec((tr, D), lambda i: (i, 0))
    vec = pl.BlockSpec((1, D), lambda i: (0, 0))
    ins, in_specs = [x, *gains], [row] + [vec] * n
    _dep_operand(ins, in_specs, dep)
    return pl.pallas_call(
        body, grid=(T // tr,), in_specs=in_specs, out_specs=[row] * n,
        out_shape=[jax.ShapeDtypeStruct((T, D), dt) for dt in dtypes],
        compiler_params=_params(("parallel",)), name=name)(*ins)


def _rms_bwd(name, x, dres, gains, dhs):
    T, D = x.shape
    n = len(gains)
    tr = _tile(T, 256, 8)

    def body(x_ref, dres_ref, *refs):
        g_refs, dh_refs = refs[:n], refs[n:2 * n]
        dx_ref, dxb_ref = refs[2 * n], refs[2 * n + 1]
        dg_refs = refs[2 * n + 2:]
        xv = x_ref[...]
        r = lax.rsqrt(jnp.mean(xv * xv, axis=-1, keepdims=True) + EPS)
        xr = xv * r
        w = None
        for g_ref, dh_ref, dg_ref in zip(g_refs, dh_refs, dg_refs):
            dh = dh_ref[...].astype(F32)

            @pl.when(pl.program_id(0) == 0)
            def _():
                dg_ref[...] = jnp.zeros_like(dg_ref)

            dg_ref[...] += jnp.sum(dh * xr, axis=0, keepdims=True)
            wi = dh * g_ref[...]
            w = wi if w is None else w + wi
        dx = dres_ref[...] + r * (w - xr * jnp.mean(w * xr, axis=-1, keepdims=True))
        dx_ref[...] = dx
        dxb_ref[...] = dx.astype(BF16)

    row = pl.BlockSpec((tr, D), lambda i: (i, 0))
    vec = pl.BlockSpec((1, D), lambda i: (0, 0))
    outs = pl.pallas_call(
        body, grid=(T // tr,), in_specs=[row, row] + [vec] * n + [row] * n,
        out_specs=[row, row] + [vec] * n,
        out_shape=[jax.ShapeDtypeStruct((T, D), F32), jax.ShapeDtypeStruct((T, D), BF16)]
        + [jax.ShapeDtypeStruct((1, D), F32)] * n,
        compiler_params=_params(("arbitrary",)), name=name)(x, dres, *gains, *dhs)
    return outs[0], outs[1], outs[2:]


def _loss_head(x, gain, target):
    T, D = x.shape
    tr = _tile(T, 256, 8)

    def body(x_ref, g_ref, t_ref, loss_ref, dx_ref, dxb_ref, dg_ref):
        @pl.when(pl.program_id(0) == 0)
        def _():
            loss_ref[...] = jnp.zeros_like(loss_ref)
            dg_ref[...] = jnp.zeros_like(dg_ref)

        xv = x_ref[...]
        r = lax.rsqrt(jnp.mean(xv * xv, axis=-1, keepdims=True) + EPS)
        xr = xv * r
        err = xr * g_ref[...] - t_ref[...]
        part = jnp.sum(jnp.sum(err * err, axis=-1, keepdims=True), axis=0, keepdims=True) * (0.5 / D)
        loss_ref[...] += jnp.broadcast_to(part, loss_ref.shape)
        dy = err * (1.0 / D)
        dg_ref[...] += jnp.sum(dy * xr, axis=0, keepdims=True)
        w = dy * g_ref[...]
        dx = r * (w - xr * jnp.mean(w * xr, axis=-1, keepdims=True))
        dx_ref[...] = dx
        dxb_ref[...] = dx.astype(BF16)

    row = pl.BlockSpec((tr, D), lambda i: (i, 0))
    vec = pl.BlockSpec((1, D), lambda i: (0, 0))
    return pl.pallas_call(
        body, grid=(T // tr,), in_specs=[row, vec, row],
        out_specs=[pl.BlockSpec((8, 128), lambda i: (0, 0)), row, row, vec],
        out_shape=[jax.ShapeDtypeStruct((8, 128), F32), jax.ShapeDtypeStruct((T, D), F32),
                   jax.ShapeDtypeStruct((T, D), BF16), jax.ShapeDtypeStruct((1, D), F32)],
        compiler_params=_params(("arbitrary",)), name="loss_head")(x, gain, target)


def _glu_bwd(dmix, vg, dep=None):
    T, N = dmix.shape
    tr, tc = _tile(T, 512, 8), _tile(N, 1024)
    n_dep = 0 if dep is None else 1

    def body(d_ref, vg_ref, *refs):
        o_ref = refs[n_dep]
        d = d_ref[...]
        val = vg_ref[0].astype(F32)
        s = _sigmoid(vg_ref[1].astype(F32))
        o_ref[0] = (d * s).astype(BF16)
        o_ref[1] = (d * val * s * (1.0 - s)).astype(BF16)

    pair = pl.BlockSpec((2, tr, tc), lambda i, j: (0, i, j))
    ins, in_specs = [dmix, vg], [pl.BlockSpec((tr, tc), lambda i, j: (i, j)), pair]
    _dep_operand(ins, in_specs, dep)
    return pl.pallas_call(
        body, grid=(T // tr, N // tc), in_specs=in_specs,
        out_specs=pair, out_shape=jax.ShapeDtypeStruct((2, T, N), BF16),
        compiler_params=_params(("parallel", "parallel")), name="glu_bwd")(*ins)


def _to_state_tiles(x_ref, s, val):
    tc = val.shape[0]
    for j in range(S5_W // 128):
        x_ref[s, pl.ds(j, tc, stride=8), :] = val[:, 128 * j:128 * (j + 1)]


def _from_state_tiles(x_ref, s, tc):
    return jnp.concatenate([x_ref[s, pl.ds(j, tc, stride=8), :] for j in range(S5_W // 128)], axis=1)


def _s5_scan_fwd(xr_ref, xi_ref, ar_ref, ai_ref, cr_ref, ci_ref, tc, nblk):
    a = [(ar_ref[s], ai_ref[s]) for s in range(nblk)]

    def step(i, carry):
        carry = list(carry)
        for uu in range(S5_UNROLL):
            r0 = pl.multiple_of((i * S5_UNROLL + uu) * 8, 8)
            for s in range(nblk):
                cr, ci = carry[2 * s], carry[2 * s + 1]
                a_r, a_i = a[s]
                xr = a_r * cr - a_i * ci + xr_ref[s, pl.ds(r0, 8), :]
                xi = a_r * ci + a_i * cr + xi_ref[s, pl.ds(r0, 8), :]
                xr_ref[s, pl.ds(r0, 8), :] = xr
                xi_ref[s, pl.ds(r0, 8), :] = xi
                carry[2 * s], carry[2 * s + 1] = xr, xi
        return tuple(carry)

    init = []
    for s in range(nblk):
        init += [cr_ref[s], ci_ref[s]]
    out = lax.fori_loop(0, tc // S5_UNROLL, step, tuple(init))
    for s in range(nblk):
        cr_ref[s] = out[2 * s]
        ci_ref[s] = out[2 * s + 1]


def _s5_scan_bwd(lr_ref, li_ref, xr_ref, xi_ref, h_ref, ar_ref, ai_ref, cr_ref, ci_ref,
                 accr_ref, acci_ref, tc, nblk):
    a = [(ar_ref[s], ai_ref[s]) for s in range(nblk)]

    def one(s, r0, prev_r, prev_i, st):
        c_r, c_i, d_r, d_i = st
        a_r, a_i = a[s]
        l_r = lr_ref[s, pl.ds(r0, 8), :] + a_r * c_r + a_i * c_i
        l_i = li_ref[s, pl.ds(r0, 8), :] + a_r * c_i - a_i * c_r
        lr_ref[s, pl.ds(r0, 8), :] = l_r
        li_ref[s, pl.ds(r0, 8), :] = l_i
        return [l_r, l_i, d_r + l_r * prev_r + l_i * prev_i, d_i - l_r * prev_i + l_i * prev_r]

    def step(i, carry):
        carry = list(carry)
        for uu in range(S5_UNROLL):
            t = tc - 1 - (i * S5_UNROLL + uu)
            r0 = pl.multiple_of(t * 8, 8)
            p0 = pl.multiple_of((t - 1) * 8, 8)
            for s in range(nblk):
                carry[4 * s:4 * s + 4] = one(s, r0, xr_ref[s, pl.ds(p0, 8), :], xi_ref[s, pl.ds(p0, 8), :],
                                             carry[4 * s:4 * s + 4])
        return tuple(carry)

    init = []
    for s in range(nblk):
        init += [cr_ref[s], ci_ref[s], accr_ref[s], acci_ref[s]]
    carry = list(lax.fori_loop(0, tc // S5_UNROLL - 1, step, tuple(init)))
    for t in range(S5_UNROLL - 1, -1, -1):
        for s in range(nblk):
            if t > 0:
                prev_r, prev_i = xr_ref[s, 8 * (t - 1):8 * t, :], xi_ref[s, 8 * (t - 1):8 * t, :]
            else:
                prev_r, prev_i = h_ref[0, s], h_ref[1, s]
            carry[4 * s:4 * s + 4] = one(s, 8 * t, prev_r, prev_i, carry[4 * s:4 * s + 4])
    for s in range(nblk):
        cr_ref[s], ci_ref[s], accr_ref[s], acci_ref[s] = carry[4 * s:4 * s + 4]


def _gelu(y):
    k = math.sqrt(2.0 / math.pi)
    return 0.5 * y * (1.0 + jnp.tanh(k * (y + 0.044715 * (y * y * y))))


def _gelu_grad(y):
    k = math.sqrt(2.0 / math.pi)
    t = jnp.tanh(k * (y + 0.044715 * (y * y * y)))
    return 0.5 * (1.0 + t) + 0.5 * y * (1.0 - t * t) * (k * (1.0 + 3.0 * 0.044715 * (y * y)))


def _s5_specs(tc, nch, sbk, rev):
    def ch(c):
        return nch - 1 - c if rev else c

    return dict(
        act=pl.BlockSpec((tc, sbk * S5_CH), lambda i, c: (ch(c), i)),
        bb=pl.BlockSpec((sbk, S5_CH, S5_W), lambda i, c: (i, 0, 0)),
        cc=pl.BlockSpec((sbk, S5_W, S5_CH), lambda i, c: (i, 0, 0)),
        a=pl.BlockSpec((sbk, 8, 128), lambda i, c: (i, 0, 0)),
        d=pl.BlockSpec((1, sbk * S5_CH), lambda i, c: (0, i)),
        h=pl.BlockSpec((None, 2, sbk, 8, 128), lambda i, c: (ch(c), 0, i, 0, 0)),
    )


def _s5_blocks(nb, pref):
    return max(b for b in range(1, pref + 1) if nb % b == 0)


def _s5_fwd(u, bbr, bbi, ccr, cci, a_re, a_im, d_skip):
    T, D = u.shape
    nb = D // S5_CH
    sbk = _s5_blocks(nb, 4)
    tc = _tile(T, 512, 8)
    nch = T // tc
    sp = _s5_specs(tc, nch, sbk, False)

    def body(u_ref, bbr_ref, bbi_ref, ccr_ref, cci_ref, ar_ref, ai_ref, d_ref, z_ref, h_ref,
             xr, xi, cr, ci):
        @pl.when(pl.program_id(1) == 0)
        def _():
            cr[...] = jnp.zeros_like(cr)
            ci[...] = jnp.zeros_like(ci)

        h_ref[0] = cr[...]
        h_ref[1] = ci[...]
        for s in range(sbk):
            ub = u_ref[:, s * S5_CH:(s + 1) * S5_CH].astype(BF16)
            _to_state_tiles(xr, s, _dot(ub, bbr_ref[s]))
            _to_state_tiles(xi, s, _dot(ub, bbi_ref[s]))
        _s5_scan_fwd(xr, xi, ar_ref, ai_ref, cr, ci, tc, sbk)
        for s in range(sbk):
            cols = slice(s * S5_CH, (s + 1) * S5_CH)
            y = (_dot(_from_state_tiles(xr, s, tc).astype(BF16), ccr_ref[s])
                 - _dot(_from_state_tiles(xi, s, tc).astype(BF16), cci_ref[s])
                 + d_ref[:, cols] * u_ref[:, cols])
            z_ref[:, cols] = _gelu(y).astype(BF16)

    tiles = pltpu.VMEM((sbk, tc * 8, 128), F32)
    carry = pltpu.VMEM((sbk, 8, 128), F32)
    return pl.pallas_call(
        body, grid=(nb // sbk, nch),
        in_specs=[sp["act"], sp["bb"], sp["bb"], sp["cc"], sp["cc"], sp["a"], sp["a"], sp["d"]],
        out_specs=[sp["act"], sp["h"]],
        out_shape=[jax.ShapeDtypeStruct((T, D), BF16), jax.ShapeDtypeStruct((nch, 2, nb, 8, 128), F32)],
        scratch_shapes=[tiles, tiles, carry, carry],
        compiler_params=_params(("parallel", "arbitrary")), name="s5_fwd",
    )(u, bbr, bbi, ccr, cci, a_re, a_im, d_skip)


def _s5_bwd(u, dz, h0, bbr, bbi, ccr, cci, a_re, a_im, d_skip):
    T, D = u.shape
    nb = D // S5_CH
    sbk = _s5_blocks(nb, 2)
    tc = _tile(T, 512, 8)
    nch = T // tc
    sp = _s5_specs(tc, nch, sbk, True)

    def body(u_ref, dz_ref, h_ref, bbr_ref, bbi_ref, ccr_ref, cci_ref, ar_ref, ai_ref, d_ref,
             du_ref, dd_ref, dar_ref, dai_ref, dbbr_ref, dbbi_ref, dccr_ref, dcci_ref,
             xr, xi, lr, li, fr, fi, br, bi, accr, acci):
        c = pl.program_id(1)

        @pl.when(c == 0)
        def _():
            for ref in (br, bi, accr, acci, dd_ref, dbbr_ref, dbbi_ref, dccr_ref, dcci_ref):
                ref[...] = jnp.zeros_like(ref)

        for s in range(sbk):
            ub = u_ref[:, s * S5_CH:(s + 1) * S5_CH].astype(BF16)
            _to_state_tiles(xr, s, _dot(ub, bbr_ref[s]))
            _to_state_tiles(xi, s, _dot(ub, bbi_ref[s]))
        fr[...] = h_ref[0]
        fi[...] = h_ref[1]
        _s5_scan_fwd(xr, xi, ar_ref, ai_ref, fr, fi, tc, sbk)
        for s in range(sbk):
            cols = slice(s * S5_CH, (s + 1) * S5_CH)
            uv = u_ref[:, cols]
            xrb = _from_state_tiles(xr, s, tc).astype(BF16)
            xib = _from_state_tiles(xi, s, tc).astype(BF16)
            dsk = d_ref[:, cols]
            y = _dot(xrb, ccr_ref[s]) - _dot(xib, cci_ref[s]) + dsk * uv
            dy = dz_ref[:, cols] * _gelu_grad(y)
            dd_ref[:, cols] += jnp.sum(dy * uv, axis=0, keepdims=True)
            dyb = dy.astype(BF16)
            dccr_ref[s] += _dot(xrb, dyb, TN)
            dcci_ref[s] += _dot(xib, dyb, TN)
            _to_state_tiles(lr, s, _dot(dyb, ccr_ref[s], NT))
            _to_state_tiles(li, s, -_dot(dyb, cci_ref[s], NT))
            du_ref[:, cols] = dy * dsk
        _s5_scan_bwd(lr, li, xr, xi, h_ref, ar_ref, ai_ref, br, bi, accr, acci, tc, sbk)
        for s in range(sbk):
            cols = slice(s * S5_CH, (s + 1) * S5_CH)
            ub = u_ref[:, cols].astype(BF16)
            lrb = _from_state_tiles(lr, s, tc).astype(BF16)
            lib = _from_state_tiles(li, s, tc).astype(BF16)
            dbbr_ref[s] += _dot(ub, lrb, TN)
            dbbi_ref[s] += _dot(ub, lib, TN)
            du_ref[:, cols] += _dot(lrb, bbr_ref[s], NT) + _dot(lib, bbi_ref[s], NT)

        @pl.when(c == nch - 1)
        def _():
            dar_ref[...] = accr[...]
            dai_ref[...] = acci[...]

    tiles = pltpu.VMEM((sbk, tc * 8, 128), F32)
    carry = pltpu.VMEM((sbk, 8, 128), F32)
    return pl.pallas_call(
        body, grid=(nb // sbk, nch),
        in_specs=[sp["act"], sp["act"], sp["h"], sp["bb"], sp["bb"], sp["cc"], sp["cc"],
                  sp["a"], sp["a"], sp["d"]],
        out_specs=[sp["act"], sp["d"], sp["a"], sp["a"], sp["bb"], sp["bb"], sp["cc"], sp["cc"]],
        out_shape=[jax.ShapeDtypeStruct((T, D), F32), jax.ShapeDtypeStruct((1, D), F32),
                   jax.ShapeDtypeStruct((nb, 8, 128), F32), jax.ShapeDtypeStruct((nb, 8, 128), F32),
                   jax.ShapeDtypeStruct((nb, S5_CH, S5_W), F32), jax.ShapeDtypeStruct((nb, S5_CH, S5_W), F32),
                   jax.ShapeDtypeStruct((nb, S5_W, S5_CH), F32), jax.ShapeDtypeStruct((nb, S5_W, S5_CH), F32)],
        scratch_shapes=[tiles, tiles, tiles, tiles, carry, carry, carry, carry, carry, carry],
        compiler_params=_params(("parallel", "arbitrary")), name="s5_bwd",
    )(u, dz, h0, bbr, bbi, ccr, cci, a_re, a_im, d_skip)


def _s5_disc(lr, li, ldt):
    dt = jnp.exp(ldt)
    mag = jnp.exp(lr * dt)
    ang = li * dt
    cs, sn = jnp.cos(ang), jnp.sin(ang)
    lbr, lbi = mag * cs, mag * sn
    nr = lbr - 1.0
    den = lr * lr + li * li
    f_re = (nr * lr + lbi * li) / den
    f_im = (lbi * lr - nr * li) / den
    return dt, mag, cs, sn, lbr, lbi, nr, den, f_re, f_im


def _s5_param_fwd(lr, li, ldt, bt_re, bt_im):
    c, g, p = bt_re.shape

    def body(lr_ref, li_ref, ldt_ref, br_ref, bi_ref, lbr_ref, lbi_ref, bbr_ref, bbi_ref):
        _, _, _, _, lbr, lbi, _, _, f_re, f_im = _s5_disc(lr_ref[...], li_ref[...], ldt_ref[...])
        lbr_ref[...] = lbr
        lbi_ref[...] = lbi
        for ch in range(c):
            b_r, b_i = br_ref[ch], bi_ref[ch]
            bbr_ref[ch] = f_re * b_r - f_im * b_i
            bbi_ref[ch] = f_re * b_i + f_im * b_r

    gp = jax.ShapeDtypeStruct((g, p), F32)
    cgp = jax.ShapeDtypeStruct((c, g, p), F32)
    return pl.pallas_call(body, out_shape=[gp, gp, cgp, cgp], name="s5_param_fwd")(lr, li, ldt, bt_re, bt_im)


def _s5_param_bwd(lr, li, ldt, bt_re, bt_im, dlbr, dlbi, dbbr, dbbi):
    c, g, p = bt_re.shape

    def body(lr_ref, li_ref, ldt_ref, br_ref, bi_ref, dlbr_ref, dlbi_ref, dbbr_ref, dbbi_ref,
             dlr_ref, dli_ref, dldt_ref, dbr_ref, dbi_ref):
        l_r, l_i = lr_ref[...], li_ref[...]
        dt, mag, cs, sn, lbr, lbi, nr, den, f_re, f_im = _s5_disc(l_r, l_i, ldt_ref[...])
        dfr = jnp.zeros_like(l_r)
        dfi = jnp.zeros_like(l_r)
        for ch in range(c):
            b_r, b_i = br_ref[ch], bi_ref[ch]
            g_r, g_i = dbbr_ref[ch], dbbi_ref[ch]
            dbr_ref[ch] = f_re * g_r + f_im * g_i
            dbi_ref[ch] = f_re * g_i - f_im * g_r
            dfr = dfr + g_r * b_r + g_i * b_i
            dfi = dfi + g_i * b_r - g_r * b_i
        inv = 1.0 / den
        d_nr = (dfr * l_r - dfi * l_i) * inv
        d_lbi = (dfr * l_i + dfi * l_r) * inv + dlbi_ref[...]
        d_lbr = d_nr + dlbr_ref[...]
        d_den = -(dfr * f_re + dfi * f_im) * inv
        d_mag = d_lbr * cs + d_lbi * sn
        d_ang = d_lbi * lbr - d_lbr * lbi
        dlr_ref[...] = (dfr * nr + dfi * lbi) * inv + 2.0 * d_den * l_r + d_mag * mag * dt
        dli_ref[...] = (dfr * lbi - dfi * nr) * inv + 2.0 * d_den * l_i + d_ang * dt
        dldt_ref[...] = jnp.sum(d_mag * mag * l_r + d_ang * l_i, axis=1, keepdims=True) * dt

    gp = jax.ShapeDtypeStruct((g, p), F32)
    cgp = jax.ShapeDtypeStruct((c, g, p), F32)
    return pl.pallas_call(body, out_shape=[gp, gp, jax.ShapeDtypeStruct((g, 1), F32), cgp, cgp],
                          name="s5_param_bwd")(lr, li, ldt, bt_re, bt_im, dlbr, dlbi, dbbr, dbbi)


def _block_diag(t, inner_first):
    g, c, p = t.shape
    nb = g // S5_GB
    t4 = t.reshape(nb, S5_GB, c, p)
    eye = jnp.eye(S5_GB, dtype=t.dtype)
    if inner_first:
        e = t4[:, :, :, None, :] * eye[None, :, None, :, None]
        return e.reshape(nb, S5_GB * c, S5_GB * p)
    t4 = t4.transpose(0, 1, 3, 2)
    e = t4[:, :, :, None, :] * eye[None, :, None, :, None]
    return e.reshape(nb, S5_GB * p, S5_GB * c)


def _block_diag_extract(e, inner_first, c, p):
    nb = e.shape[0]
    eye = jnp.eye(S5_GB, dtype=e.dtype)
    if inner_first:
        e5 = e.reshape(nb, S5_GB, c, S5_GB, p)
        return (e5 * eye[None, :, None, :, None]).sum(3).reshape(nb * S5_GB, c, p)
    e5 = e.reshape(nb, S5_GB, p, S5_GB, c)
    return (e5 * eye[None, :, None, :, None]).sum(3).transpose(0, 1, 3, 2).reshape(nb * S5_GB, c, p)


def _att_masks(rep, gb):
    rows = rep * ATT_BLK
    qi = lax.broadcasted_iota(jnp.int32, (rows, 2 * ATT_BLK), 0) % ATT_BLK
    si = lax.broadcasted_iota(jnp.int32, (rows, 2 * ATT_BLK), 1)
    prev = (si < ATT_BLK) & (si >= qi) & (gb > 0)
    cur = (si >= ATT_BLK) & (si - ATT_BLK <= qi)
    return prev | cur


def _att_rows(start, dil):
    return pl.ds(start, ATT_BLK) if dil == 1 else pl.ds(start, ATT_BLK, stride=dil)


def _att_plan(T, dil):
    span = ATT_BLK * dil
    sbr = max(span, min(T, 1024))
    return span, sbr, T // sbr


def _att_block(sb, i, sbr, span, dil):
    loc = (i // dil) * span + i % dil
    cur = sb * sbr + loc
    gb = sb * (sbr // span) + i // dil
    return loc, cur, jnp.where(gb > 0, cur - span, cur), gb


def _att_fwd(q, k, v, grp, dil):
    T = q.shape[0]
    H = q.shape[1] // HEAD_DIM // len(DILATIONS)
    rep = H // N_KV_HEADS
    span, sbr, nsb = _att_plan(T, dil)
    scale = HEAD_DIM ** -0.5

    def body(*refs):
        q_refs = refs[:rep]
        k_ref, v_ref, o_ref, l_ref, o_slab, l_slab = refs[rep:]
        sb = pl.program_id(1)

        def blk(i, _):
            loc, cur, prv, gb = _att_block(sb, i, sbr, span, dil)
            rows = _att_rows(loc, dil)
            qs = jnp.concatenate([r[rows, :] for r in q_refs], axis=0).astype(BF16)
            kcat = jnp.concatenate([k_ref[_att_rows(prv, dil), :], k_ref[_att_rows(cur, dil), :]], axis=0)
            vcat = jnp.concatenate([v_ref[_att_rows(prv, dil), :], v_ref[_att_rows(cur, dil), :]], axis=0)
            s = jnp.where(_att_masks(rep, gb), _dot(qs, kcat.astype(BF16), NT) * scale, NEG_INF)
            m = jnp.max(s, axis=-1, keepdims=True)
            p = jnp.exp(s - m)
            l = jnp.sum(p, axis=-1, keepdims=True)
            o = _dot(p.astype(BF16), vcat.astype(BF16)) / l
            lse = jnp.broadcast_to(m + jnp.log(l), (rep * ATT_BLK, HEAD_DIM))
            for j in range(rep):
                o_slab[j, rows, :] = o[j * ATT_BLK:(j + 1) * ATT_BLK]
                l_slab[j, rows, :] = lse[j * ATT_BLK:(j + 1) * ATT_BLK]
            return 0

        lax.fori_loop(0, sbr // ATT_BLK, blk, 0)
        for j in range(rep):
            o_ref[:, j * HEAD_DIM:(j + 1) * HEAD_DIM] = o_slab[j]
            l_ref[:, j * HEAD_DIM:(j + 1) * HEAD_DIM] = l_slab[j]

    qspecs = [pl.BlockSpec((sbr, HEAD_DIM), functools.partial(lambda h, s, j: (s, grp * H + h * rep + j), j=j))
              for j in range(rep)]
    kspec = pl.BlockSpec((T, HEAD_DIM), lambda h, s: (0, h))
    ospec = pl.BlockSpec((sbr, rep * HEAD_DIM), lambda h, s: (s, h))
    slab = pltpu.VMEM((rep, sbr, HEAD_DIM), F32)
    return pl.pallas_call(
        body, grid=(N_KV_HEADS, nsb), in_specs=qspecs + [kspec, kspec], out_specs=[ospec, ospec],
        out_shape=[jax.ShapeDtypeStruct((T, H * HEAD_DIM), F32)] * 2, scratch_shapes=[slab, slab],
        compiler_params=_params(("parallel", "arbitrary")), name=f"att_fwd_d{dil}",
    )(*([q] * rep), k, v)


def _att_combine(outs, lses):
    T, W = outs[0].shape
    ng = len(outs)
    tr, tcol = _tile(T, 512, 8), _tile(W, 512)

    def body(*refs):
        o_refs, l_refs = refs[:ng], refs[ng:2 * ng]
        ob_ref, lse_ref = refs[2 * ng:]
        ls = [r[...] for r in l_refs]
        m = functools.reduce(jnp.maximum, ls)
        es = [jnp.exp(l - m) for l in ls]
        den = functools.reduce(lambda a, b: a + b, es)
        num = functools.reduce(lambda a, b: a + b, [e * o[...] for e, o in zip(es, o_refs)])
        ob_ref[...] = (num / den).astype(BF16)
        lse_ref[...] = m + jnp.log(den)

    spec = pl.BlockSpec((tr, tcol), lambda i, j: (i, j))
    return pl.pallas_call(
        body, grid=(T // tr, W // tcol), in_specs=[spec] * (2 * ng), out_specs=[spec, spec],
        out_shape=[jax.ShapeDtypeStruct((T, W), BF16), jax.ShapeDtypeStruct((T, W), F32)],
        compiler_params=_params(("parallel", "parallel")), name="att_combine")(*outs, *lses)


STAT_LANE = HEAD_DIM // 2


def _att_stats(lse, o, do):
    T, W = lse.shape
    tr = _tile(T, 512, 16)

    def body(l_ref, o_ref, do_ref, s_ref):
        delta = jnp.sum(do_ref[...] * o_ref[...].astype(F32), axis=-1, keepdims=True)
        lane = lax.broadcasted_iota(jnp.int32, l_ref.shape, 1)
        s_ref[...] = jnp.where(lane < STAT_LANE, l_ref[...], delta)

    spec = pl.BlockSpec((tr, HEAD_DIM), lambda i, j: (i, j))
    return pl.pallas_call(
        body, grid=(T // tr, W // HEAD_DIM), in_specs=[spec] * 3, out_specs=spec,
        out_shape=jax.ShapeDtypeStruct((T, W), F32),
        compiler_params=_params(("parallel", "parallel")), name="att_stats")(lse, o, do)


def _att_bwd(q, k, v, do, stats, dq, grp, dil):
    T = q.shape[0]
    H = do.shape[1] // HEAD_DIM
    rep = H // N_KV_HEADS
    hs = min(rep, 2)
    span, sbr, nsb = _att_plan(T, dil)
    scale = HEAD_DIM ** -0.5

    def body(*refs):
        q_refs, do_refs, st_refs = refs[:hs], refs[hs + 2:2 * hs + 2], refs[2 * hs + 2:3 * hs + 2]
        k_ref, v_ref = refs[hs], refs[hs + 1]
        dq_ref, dk_ref, dv_ref, dq_slab = refs[3 * hs + 3:]
        sb = pl.program_id(2)

        @pl.when((pl.program_id(1) == 0) & (sb == 0))
        def _():
            dk_ref[...] = jnp.zeros_like(dk_ref)
            dv_ref[...] = jnp.zeros_like(dv_ref)

        def blk(i, _):
            loc, cur, prv, gb = _att_block(sb, i, sbr, span, dil)
            rows, kc, kp = _att_rows(loc, dil), _att_rows(cur, dil), _att_rows(prv, dil)
            qs = jnp.concatenate([r[rows, :] for r in q_refs], axis=0).astype(BF16)
            dos = jnp.concatenate([r[rows, :] for r in do_refs], axis=0).astype(BF16)
            st = jnp.concatenate([r[rows, :] for r in st_refs], axis=0)
            kcat = jnp.concatenate([k_ref[kp, :], k_ref[kc, :]], axis=0).astype(BF16)
            vcat = jnp.concatenate([v_ref[kp, :], v_ref[kc, :]], axis=0).astype(BF16)
            s = _dot(qs, kcat, NT) * scale
            p = jnp.where(_att_masks(hs, gb), jnp.exp(s - st[:, 0:1]), 0.0)
            dp = _dot(dos, vcat, NT)
            ds = (p * (dp - st[:, STAT_LANE:STAT_LANE + 1]) * scale).astype(BF16)
            dvc = _dot(p.astype(BF16), dos, TN)
            dkc = _dot(ds, qs, TN)
            dqs = _dot(ds, kcat)
            for j in range(hs):
                dq_slab[j, rows, :] = dqs[j * ATT_BLK:(j + 1) * ATT_BLK]
            dk_ref[kc, :] += dkc[ATT_BLK:]
            dv_ref[kc, :] += dvc[ATT_BLK:]

            @pl.when(gb > 0)
            def _():
                dk_ref[kp, :] += dkc[:ATT_BLK]
                dv_ref[kp, :] += dvc[:ATT_BLK]

            return 0

        lax.fori_loop(0, sbr // ATT_BLK, blk, 0)
        for j in range(hs):
            dq_ref[:, j * HEAD_DIM:(j + 1) * HEAD_DIM] = dq_slab[j]

    def head_specs(col0):
        return [pl.BlockSpec((sbr, HEAD_DIM),
                             functools.partial(lambda h, f, s, j: (s, col0 + h * rep + f * hs + j), j=j))
                for j in range(hs)]

    kspec = pl.BlockSpec((T, HEAD_DIM), lambda h, f, s: (0, h))
    dqspec = pl.BlockSpec((sbr, hs * HEAD_DIM), lambda h, f, s: (s, (grp * H + h * rep) // hs + f))
    n_in = 3 * hs + 3
    return pl.pallas_call(
        body, grid=(N_KV_HEADS, rep // hs, nsb),
        in_specs=head_specs(grp * H) + [kspec, kspec] + head_specs(0) + head_specs(0) + [ANY],
        out_specs=[dqspec, kspec, kspec],
        out_shape=[jax.ShapeDtypeStruct(dq.shape, F32),
                   jax.ShapeDtypeStruct((T, N_KV_HEADS * HEAD_DIM), F32),
                   jax.ShapeDtypeStruct((T, N_KV_HEADS * HEAD_DIM), F32)],
        scratch_shapes=[pltpu.VMEM((hs, sbr, HEAD_DIM), F32)],
        input_output_aliases={n_in - 1: 0},
        compiler_params=_params(("parallel", "arbitrary", "arbitrary")), name=f"att_bwd_d{dil}",
    )(*([q] * hs), k, v, *([do] * hs), *([stats] * hs), dq)


def _sum_kv(dks, dvs):
    T, W = dks[0].shape
    ng = len(dks)
    tr = _tile(T, 512, 8)

    def body(*refs):
        o_ref = refs[2 * ng]
        o_ref[0] = functools.reduce(lambda a, b: a + b, [r[...] for r in refs[:ng]]).astype(BF16)
        o_ref[1] = functools.reduce(lambda a, b: a + b, [r[...] for r in refs[ng:2 * ng]]).astype(BF16)

    spec = pl.BlockSpec((tr, W), lambda i: (i, 0))
    return pl.pallas_call(
        body, grid=(T // tr,), in_specs=[spec] * (2 * ng),
        out_specs=pl.BlockSpec((2, tr, W), lambda i: (0, i, 0)),
        out_shape=jax.ShapeDtypeStruct((2, T, W), BF16),
        compiler_params=_params(("parallel",)), name="sum_kv")(*dks, *dvs)


def _local_step(x, tgt, w, wts, ready, dep0=None):
    T, D = x.shape
    g = {}

    (u0,) = _rms_fwd("rms_a", x, [w["a_norm"]], [F32], dep=dep0)
    lbr, lbi, bbt_re, bbt_im = _s5_param_fwd(w["lam_re"], w["lam_im"], w["log_dt"], w["bt_re"], w["bt_im"])
    a_re, a_im = lbr.reshape(-1, 8, 128), lbi.reshape(-1, 8, 128)
    bbr = _block_diag(bbt_re.transpose(1, 0, 2), True).astype(BF16)
    bbi = _block_diag(bbt_im.transpose(1, 0, 2), True).astype(BF16)
    ccr = _block_diag(w["c_re"], False).astype(BF16)
    cci = _block_diag(w["c_im"], False).astype(BF16)
    z, h0 = _s5_fwd(u0, bbr, bbi, ccr, cci, a_re, a_im, w["s5_d"])
    w_glu = wts("glu", z)
    x1, vg = _mm_dual_fwd("glu_fwd", z, w_glu, x, "glu")

    def ffn_fwd(xin, layer):
        (nrm,) = _rms_fwd(f"rms_f{layer}", xin, [w["ffn_norm"][layer:layer + 1]], [BF16])
        w_in = wts(f"win{layer}", nrm)
        gu, act = _mm_dual_fwd(f"ffn_in{layer}", nrm, w_in, None, "ffn")
        w_out = wts(f"wout{layer}", act)
        xout = _mm_nn(f"ffn_out{layer}", act, w_out, res=xin)
        return xout, (nrm, gu, act, w_in, w_out)

    x2, saved0 = ffn_fwd(x1, 0)
    kvn, hb = _rms_fwd("rms_b", x2, [w["kv_norm"], w["b_norm"]], [BF16, BF16])
    w_kv, w_q, w_o = wts("wkv", hb), wts("wq", hb), wts("wo", hb)
    k, v = _mm_kv("kv_proj", kvn, w_kv)
    q = _mm_nn("q_proj", hb, w_q)
    outs, lses = [], []
    for grp, dil in enumerate(DILATIONS):
        o_g, l_g = _att_fwd(q, k, v, grp, dil)
        outs.append(o_g)
        lses.append(l_g)
    o, lse = _att_combine(outs, lses)
    x3 = _mm_nn("o_proj", o, w_o, res=x2)
    x4, saved1 = ffn_fwd(x3, 1)
    loss_blk, dx4, dx4b, g["final_norm"] = _loss_head(x4, w["final_norm"], tgt)

    def ffn_bwd(dx, dxb, xin, saved, layer, dep):
        nrm, gu, act, w_in, w_out = saved
        dgu = _mm_nt_ffn_bwd(f"ffn_dact{layer}", dxb, w_out, gu, dep=dep)
        g_wout = _mm_tn(f"ffn_dwout{layer}", act, dxb)
        g_win = _mm_tn_pair(f"ffn_dwin{layer}", nrm, dgu)
        dn = _mm_nt_pair(f"ffn_dn{layer}", dgu, w_in)
        dxo, dxob, (dgn,) = _rms_bwd(f"rms_f{layer}_bwd", xin, dx, [w["ffn_norm"][layer:layer + 1]], [dn])
        tok = ready({f"win{layer}": g_win, f"wout{layer}": g_wout})
        return dxo, dxob, dgn, tok

    dx3, dx3b, dfn1, tok = ffn_bwd(dx4, dx4b, x3, saved1, 1, None)
    do = _mm_nt("o_proj_dx", [dx3b], w_o, dep=tok)
    g_wo = _mm_tn("o_proj_dw", o, dx3b)
    stats = _att_stats(lse, o, do)
    dq = lax.empty(q.shape, F32)
    dks, dvs = [], []
    for grp, dil in enumerate(DILATIONS):
        dq, dk_g, dv_g = _att_bwd(q, k, v, do, stats, dq, grp, dil)
        dks.append(dk_g)
        dvs.append(dv_g)
    dkv = _sum_kv(dks, dvs)
    dhb = _mm_nt("q_proj_dx", [dq], w_q)
    g_wq = _mm_tn("q_proj_dw", hb, dq)
    dkvn = _mm_nt_pair("kv_proj_dx", dkv, w_kv)
    g_wkv = _mm_tn_pair("kv_proj_dw", kvn, dkv)
    dx2, dx2b, (g["kv_norm"], g["b_norm"]) = _rms_bwd(
        "rms_b_bwd", x2, dx3, [w["kv_norm"], w["b_norm"]], [dkvn, dhb])
    tok = ready({"wkv": g_wkv, "wq": g_wq, "wo": g_wo})
    dx1, dx1b, dfn0, tok = ffn_bwd(dx2, dx2b, x1, saved0, 0, tok)
    g["ffn_norm"] = jnp.concatenate([dfn0, dfn1], axis=0)

    dvg = _glu_bwd(dx1, vg, dep=tok)
    dz = _mm_nt_pair("glu_dx", dvg, w_glu)
    ready({"glu": _mm_tn_pair("glu_dw", z, dvg)})
    du, g["s5_d"], da_re, da_im, dbbr, dbbi, dccr, dcci = _s5_bwd(
        u0, dz, h0, bbr, bbi, ccr, cci, a_re, a_im, w["s5_d"])
    C, G, P = w["bt_re"].shape
    g["c_re"] = _block_diag_extract(dccr, False, C, P)
    g["c_im"] = -_block_diag_extract(dcci, False, C, P)
    g["lam_re"], g["lam_im"], g["log_dt"], g["bt_re"], g["bt_im"] = _s5_param_bwd(
        w["lam_re"], w["lam_im"], w["log_dt"], w["bt_re"], w["bt_im"],
        da_re.reshape(G, P), da_im.reshape(G, P),
        _block_diag_extract(dbbr, True, C, P).transpose(1, 0, 2),
        _block_diag_extract(dbbi, True, C, P).transpose(1, 0, 2))
    grad_x, _, (g["a_norm"],) = _rms_bwd("rms_a_bwd", x, dx1, [w["a_norm"]], [du])
    return loss_blk, grad_x, g


def _coords():
    return lax.axis_index("x"), lax.axis_index("y"), lax.axis_index("c")


def _dev_index(dev):
    return 4 * dev[0] + 2 * dev[1] + dev[2]


def _shard_window(ref, axis, width, idx):
    sl = [slice(None)] * len(ref.shape)
    sl[axis] = pl.ds(pl.multiple_of(idx * width, width), width)
    return ref.at[tuple(sl)]


def _all_gather(name, shards, axes):
    na = len(shards)
    widths = [s.shape[ax] for s, ax in zip(shards, axes)]
    out_shapes = []
    for s, ax in zip(shards, axes):
        shp = list(s.shape)
        shp[ax] *= N_DEV
        out_shapes.append(jax.ShapeDtypeStruct(tuple(shp), s.dtype))

    def body(*refs):
        ins, outs = refs[:na], refs[na:2 * na]
        send_sems, recv_sems, local_sems = refs[2 * na:]
        x, y, c = _coords()
        me, sib = (x, y, c), (x, y, 1 - c)
        chips = [(1 - x, y), (x, 1 - y), (1 - x, 1 - y)]

        def blk(a, dev):
            return _shard_window(outs[a], axes[a], widths[a], _dev_index(dev))

        def copy(a, kk, block, to, src=None):
            return pltpu.make_async_remote_copy(
                src_ref=blk(a, block) if src is None else src, dst_ref=blk(a, block),
                send_sem=send_sems.at[a, kk], recv_sem=recv_sems.at[a, kk],
                device_id=to, device_id_type=MESH)

        local = [pltpu.make_async_copy(ins[a], blk(a, me), local_sems.at[a]) for a in range(na)]
        for cp in local:
            cp.start()
        sent = []
        for a in range(na):
            first = [copy(a, 0, me, sib, src=ins[a])]
            first += [copy(a, 1 + j, me, (*chip, c), src=ins[a]) for j, chip in enumerate(chips)]
            for cp in first:
                cp.start()
            sent += first
        for a in range(na):
            for j, chip in enumerate(chips):
                copy(a, 1 + j, (*chip, c), me).wait_recv()
                fwd = copy(a, 4 + j, (*chip, c), sib)
                fwd.start()
                sent.append(fwd)
        for a in range(na):
            copy(a, 0, sib, me).wait_recv()
            for j, chip in enumerate(chips):
                copy(a, 4 + j, (*chip, 1 - c), me).wait_recv()
        for cp in sent:
            cp.wait_send()
        for cp in local:
            cp.wait()

    return pl.pallas_call(
        body, out_shape=out_shapes, in_specs=[ANY] * na, out_specs=[ANY] * na,
        scratch_shapes=[pltpu.SemaphoreType.DMA((na, 7)), pltpu.SemaphoreType.DMA((na, 7)),
                        pltpu.SemaphoreType.DMA((na,))],
        name=name)(*shards)


HBM = pl.BlockSpec(memory_space=pltpu.HBM)
SEM = pl.BlockSpec(memory_space=pltpu.SEMAPHORE)
TOKEN_SPEC = pl.BlockSpec(memory_space=pltpu.VMEM)
TOKEN_SHAPE = jax.ShapeDtypeStruct((8, 128), F32)
SPLIT_PARAMS = pltpu.CompilerParams(has_side_effects=pltpu.SideEffectType.DATAFLOW_SIDE_EFFECTING)


def _hbm(x):
    return pltpu.with_memory_space_constraint(x, pltpu.HBM)


def _hbm_like(x):
    return pltpu.HBM(x.shape, x.dtype)


def _dma_sems(*shape):
    return pltpu.SemaphoreType.DMA(shape)


def _cast_and_place(name, shard, axis, pos):
    rows, cols = shard.shape
    tr = _tile(rows, 256, 16)
    nt = rows // tr
    full = (rows, cols * N_DEV) if axis == 1 else (rows * N_DEV, cols)

    def dev(p):
        return 4 * p[0] + 2 * p[1] + p[2]

    def body(pos_ref, s_ref, b_ref, l_ref):
        v = s_ref[...].astype(BF16)
        b_ref[...] = v
        l_ref[...] = v

    blk = pl.BlockSpec((tr, cols), lambda i, p: (i, 0))
    if axis == 1:
        lspec = pl.BlockSpec((tr, cols), lambda i, p: (i, dev(p)))
    else:
        lspec = pl.BlockSpec((tr, cols), lambda i, p: (dev(p) * nt + i, 0))
    return pl.pallas_call(
        body, grid_spec=pltpu.PrefetchScalarGridSpec(
            num_scalar_prefetch=1, grid=(nt,), in_specs=[blk], out_specs=[blk, lspec]),
        out_shape=[jax.ShapeDtypeStruct((rows, cols), BF16), jax.ShapeDtypeStruct(full, BF16)],
        compiler_params=_params(("parallel",)), name=name)(pos, shard)


def _gather_start(shards, lands, axes, groups):
    na, ng = len(shards), len(groups)
    widths = [s.shape[ax] for s, ax in zip(shards, axes)]

    def body(*refs):
        sh, ld = refs[:na], refs[na:2 * na]
        sems = refs[2 * na:2 * na + 3 * ng]
        token = refs[-1]
        x, y, c = _coords()
        me, sib = (x, y, c), (x, y, 1 - c)
        chips = [(1 - x, y), (x, 1 - y), (1 - x, 1 - y)]
        for gi, grp in enumerate(groups):
            send, r_d2d, r_ici = sems[3 * gi:3 * gi + 3]
            for li, a in enumerate(grp):
                dst = _shard_window(ld[a], axes[a], widths[a], _dev_index(me))
                pltpu.make_async_remote_copy(
                    src_ref=sh[a], dst_ref=dst, send_sem=send.at[4 * li], recv_sem=r_d2d.at[li],
                    device_id=sib, device_id_type=MESH).start()
                for j, chip in enumerate(chips):
                    pltpu.make_async_remote_copy(
                        src_ref=sh[a], dst_ref=dst, send_sem=send.at[4 * li + 1 + j], recv_sem=r_ici.at[3 * li + j],
                        device_id=(*chip, c), device_id_type=MESH).start()
        token[...] = jnp.zeros_like(token)

    out_shape, out_specs = [], []
    for grp in groups:
        out_shape += [_dma_sems(4 * len(grp)), _dma_sems(len(grp)), _dma_sems(3 * len(grp))]
        out_specs += [SEM] * 3
    out_shape += [_hbm_like(s) for s in shards] + [_hbm_like(l) for l in lands] + [TOKEN_SHAPE]
    out_specs += [HBM] * (2 * na) + [TOKEN_SPEC]
    aliases = {a: 3 * ng + a for a in range(2 * na)}
    res = pl.pallas_call(
        body, name="weights_gather_start", out_shape=out_shape, in_specs=[HBM] * (2 * na),
        out_specs=out_specs, input_output_aliases=aliases, compiler_params=SPLIT_PARAMS,
    )(*[_hbm(s) for s in shards], *[_hbm(l) for l in lands])
    sems = [tuple(res[3 * gi:3 * gi + 3]) for gi in range(ng)]
    return sems, list(res[3 * ng:3 * ng + na]), list(res[3 * ng + na:3 * ng + 2 * na]), res[-1]


def _gather_forward(name, lands, axes, r_ici, after):
    n = len(lands)
    widths = [l.shape[ax] // N_DEV for l, ax in zip(lands, axes)]

    def body(*refs):
        ld, r_ici_ref = refs[:n], refs[n]
        f_send, f_recv = refs[n + 2], refs[n + 3]
        x, y, c = _coords()
        sib = (x, y, 1 - c)
        chips = [(1 - x, y), (x, 1 - y), (1 - x, 1 - y)]
        for li in range(n):
            for j, chip in enumerate(chips):
                blk = _shard_window(ld[li], axes[li], widths[li], _dev_index((*chip, c)))
                pltpu.make_async_remote_copy(
                    src_ref=blk, dst_ref=blk, send_sem=f_send.at[3 * li + j], recv_sem=r_ici_ref.at[3 * li + j],
                    device_id=(*chip, c), device_id_type=MESH).wait_recv()
                pltpu.make_async_remote_copy(
                    src_ref=blk, dst_ref=blk, send_sem=f_send.at[3 * li + j], recv_sem=f_recv.at[3 * li + j],
                    device_id=sib, device_id_type=MESH).start()

    res = pl.pallas_call(
        body, name=name, out_shape=[_dma_sems(3 * n), _dma_sems(3 * n)] + [_hbm_like(l) for l in lands],
        in_specs=[HBM] * n + [SEM, ANY], out_specs=[SEM, SEM] + [HBM] * n,
        input_output_aliases={li: 2 + li for li in range(n)}, compiler_params=SPLIT_PARAMS,
    )(*lands, r_ici, after)
    return res[0], res[1], list(res[2:])


def _gather_finish(name, shards, lands, axes, send, r_d2d, f_send, f_recv):
    n = len(lands)
    widths = [l.shape[ax] // N_DEV for l, ax in zip(lands, axes)]

    def body(*refs):
        sh, ld = refs[:n], refs[n:2 * n]
        send_ref, r_d2d_ref, f_send_ref, f_recv_ref = refs[2 * n:2 * n + 4]
        x, y, c = _coords()
        me, sib = (x, y, c), (x, y, 1 - c)
        chips = [(1 - x, y), (x, 1 - y), (1 - x, 1 - y)]

        def blk(li, dev):
            return _shard_window(ld[li], axes[li], widths[li], _dev_index(dev))

        for li in range(n):
            for kk in range(4):
                pltpu.make_async_remote_copy(
                    src_ref=sh[li], dst_ref=blk(li, me), send_sem=send_ref.at[4 * li + kk], recv_sem=r_d2d_ref.at[li],
                    device_id=sib, device_id_type=MESH).wait_send()
            pltpu.make_async_remote_copy(
                src_ref=blk(li, sib), dst_ref=blk(li, sib), send_sem=send_ref.at[4 * li], recv_sem=r_d2d_ref.at[li],
                device_id=sib, device_id_type=MESH).wait_recv()
            for j, chip in enumerate(chips):
                pltpu.make_async_remote_copy(
                    src_ref=blk(li, (*chip, c)), dst_ref=blk(li, (*chip, c)), send_sem=f_send_ref.at[3 * li + j],
                    recv_sem=f_recv_ref.at[3 * li + j], device_id=sib, device_id_type=MESH).wait_send()
                pltpu.make_async_remote_copy(
                    src_ref=blk(li, (*chip, 1 - c)), dst_ref=blk(li, (*chip, 1 - c)), send_sem=f_send_ref.at[3 * li + j],
                    recv_sem=f_recv_ref.at[3 * li + j], device_id=sib, device_id_type=MESH).wait_recv()

    res = pl.pallas_call(
        body, name=name, out_shape=[_hbm_like(s) for s in shards] + [_hbm_like(l) for l in lands],
        in_specs=[HBM] * (2 * n) + [SEM] * 4, out_specs=[HBM] * (2 * n),
        input_output_aliases={i: i for i in range(2 * n)}, compiler_params=SPLIT_PARAMS,
    )(*shards, *lands, send, r_d2d, f_send, f_recv)
    return list(res[n:])


def _chip_exchange_start(name, parts):
    n = len(parts)

    def body(*refs):
        src, ld = refs[:n], refs[n:2 * n]
        send, recv = refs[2 * n], refs[2 * n + 1]
        token = refs[-1]
        x, y, c = _coords()
        chips = [(1 - x, y), (x, 1 - y), (1 - x, 1 - y)]
        for li in range(n):
            for kk, chip in enumerate(chips):
                pltpu.make_async_remote_copy(
                    src_ref=src[li].at[kk], dst_ref=ld[li].at[kk], send_sem=send.at[3 * li + kk],
                    recv_sem=recv.at[3 * li + kk], device_id=(*chip, c), device_id_type=MESH).start()
        token[...] = jnp.zeros_like(token)

    lands = [lax.empty(p.shape, p.dtype) for p in parts]
    res = pl.pallas_call(
        body, name=name,
        out_shape=[_dma_sems(3 * n), _dma_sems(3 * n)] + [_hbm_like(p) for p in parts] * 2 + [TOKEN_SHAPE],
        in_specs=[HBM] * (2 * n), out_specs=[SEM, SEM] + [HBM] * (2 * n) + [TOKEN_SPEC],
        input_output_aliases={i: 2 + i for i in range(2 * n)}, compiler_params=SPLIT_PARAMS,
    )(*[_hbm(p) for p in parts], *[_hbm(l) for l in lands])
    return res[0], res[1], list(res[2:2 + n]), list(res[2 + n:2 + 2 * n]), res[-1]


def _chip_exchange_finish(name, started, after):
    counts = [len(st[2]) for st in started]
    total = sum(counts)
    ns = len(started)

    def body(*refs):
        src, ld = refs[:total], refs[total:2 * total]
        sems = refs[2 * total:2 * total + 2 * ns]
        x, y, c = _coords()
        chips = [(1 - x, y), (x, 1 - y), (1 - x, 1 - y)]
        off = 0
        for si, cnt in enumerate(counts):
            send, recv = sems[2 * si], sems[2 * si + 1]
            for li in range(cnt):
                for kk, chip in enumerate(chips):
                    cp = pltpu.make_async_remote_copy(
                        src_ref=src[off + li].at[kk], dst_ref=ld[off + li].at[kk], send_sem=send.at[3 * li + kk],
                        recv_sem=recv.at[3 * li + kk], device_id=(*chip, c), device_id_type=MESH)
                    cp.wait_send()
                    cp.wait_recv()
            off += cnt

    srcs = [p for st in started for p in st[2]]
    lands = [l for st in started for l in st[3]]
    sems = [s for st in started for s in st[:2]]
    res = pl.pallas_call(
        body, name=name, out_shape=[_hbm_like(p) for p in srcs + lands],
        in_specs=[HBM] * (2 * total) + [SEM] * (2 * ns) + [ANY], out_specs=[HBM] * (2 * total),
        input_output_aliases={i: i for i in range(2 * total)}, compiler_params=SPLIT_PARAMS,
    )(*srcs, *lands, *sems, after)
    out, off = [], total
    for cnt in counts:
        out.append(list(res[off:off + cnt]))
        off += cnt
    return out


def _pair_exchange(name, grads, axes):
    na = len(grads)
    widths = [gr.shape[ax] // N_DEV for gr, ax in zip(grads, axes)]
    out_shapes = []
    for gr, ax, wd in zip(grads, axes, widths):
        shp = list(gr.shape)
        shp[ax] = wd
        out_shapes.append(jax.ShapeDtypeStruct((4, *shp), gr.dtype))

    def body(*refs):
        ins, outs = refs[:na], refs[na:2 * na]
        send_sems, recv_sems = refs[2 * na:]
        x, y, c = _coords()
        sib = (x, y, 1 - c)
        chips = [(x, y), (1 - x, y), (x, 1 - y), (1 - x, 1 - y)]
        cps = []
        for a in range(na):
            for kk, chip in enumerate(chips):
                cp = pltpu.make_async_remote_copy(
                    src_ref=_shard_window(ins[a], axes[a], widths[a], _dev_index((*chip, 1 - c))),
                    dst_ref=outs[a].at[kk], send_sem=send_sems.at[a, kk], recv_sem=recv_sems.at[a, kk],
                    device_id=sib, device_id_type=MESH)
                cp.start()
                cps.append(cp)
        for cp in cps:
            cp.wait()

    return pl.pallas_call(
        body, out_shape=out_shapes, in_specs=[ANY] * na, out_specs=[ANY] * na,
        scratch_shapes=[pltpu.SemaphoreType.DMA((na, 4)), pltpu.SemaphoreType.DMA((na, 4))],
        name=name)(*grads)


def _pair_sum(name, grad, land, axis, pos):
    wd = grad.shape[axis] // N_DEV
    shard_shape = land.shape[1:]
    rows, cols = shard_shape
    tr = _tile(rows, 256, 16)
    nt = rows // tr

    def dev_of(kk, pos_ref):
        return 4 * (pos_ref[0] ^ (kk & 1)) + 2 * (pos_ref[1] ^ (kk >> 1)) + pos_ref[2]

    if axis == 1:
        gspec = pl.BlockSpec((tr, wd), lambda kk, t, p: (t, dev_of(kk, p)))
    else:
        gspec = pl.BlockSpec((tr, cols), lambda kk, t, p: (dev_of(kk, p) * nt + t, 0))
    lspec = pl.BlockSpec((None, tr, cols), lambda kk, t, p: (kk, t, 0))
    p0spec = pl.BlockSpec((tr, cols), lambda kk, t, p: (jnp.where(kk == 0, t, nt - 1), 0))
    pbspec = pl.BlockSpec((None, tr, cols), lambda kk, t, p: (jnp.maximum(kk - 1, 0), jnp.where(kk == 0, 0, t), 0))

    def body(pos_ref, g_ref, l_ref, p0_ref, pb_ref):
        kk = pl.program_id(0)
        s = g_ref[...].astype(F32) + l_ref[...].astype(F32)

        @pl.when(kk == 0)
        def _():
            p0_ref[...] = s

        @pl.when(kk > 0)
        def _():
            pb_ref[...] = s.astype(BF16)

    return pl.pallas_call(
        body,
        grid_spec=pltpu.PrefetchScalarGridSpec(
            num_scalar_prefetch=1, grid=(4, nt), in_specs=[gspec, lspec], out_specs=[p0spec, pbspec]),
        out_shape=[jax.ShapeDtypeStruct(shard_shape, F32), jax.ShapeDtypeStruct((3, *shard_shape), BF16)],
        compiler_params=_params(("arbitrary", "arbitrary")), name=name)(pos, grad, land)


def _adamw(name, parts, w, m, v):
    rows, cols = w.shape
    tr = _tile(rows, 256, 8)
    npart = len(parts)
    c1 = 1.0 - ADAM_B1 ** ADAM_STEP
    c2 = 1.0 - ADAM_B2 ** ADAM_STEP

    def body(*refs):
        p_refs = refs[:npart]
        w_ref, m_ref, v_ref, g_ref, d_ref, nm_ref, nv_ref = refs[npart:]
        g = None
        for r in p_refs:
            if len(r.shape) == 3:
                for i in range(r.shape[0]):
                    t = r[i].astype(F32)
                    g = t if g is None else g + t
            else:
                t = r[...].astype(F32)
                g = t if g is None else g + t
        nm = ADAM_B1 * m_ref[...] + (1.0 - ADAM_B1) * g
        nv = ADAM_B2 * v_ref[...] + (1.0 - ADAM_B2) * (g * g)
        g_ref[...] = g
        nm_ref[...] = nm
        nv_ref[...] = nv
        d_ref[...] = -ADAM_LR * ((nm / c1) / (jnp.sqrt(nv / c2) + ADAM_EPS) + ADAM_WD * w_ref[...])

    spec = pl.BlockSpec((tr, cols), lambda i: (i, 0))
    pspecs = [pl.BlockSpec((p.shape[0], tr, cols), lambda i: (0, i, 0)) if p.ndim == 3 else spec
              for p in parts]
    return pl.pallas_call(
        body, grid=(rows // tr,), in_specs=pspecs + [spec] * 3, out_specs=[spec] * 4,
        out_shape=[jax.ShapeDtypeStruct((rows, cols), F32)] * 4,
        compiler_params=_params(("parallel",)), name=name)(*parts, w, m, v)


SMALL_NAMES = ("s5_lam_re", "s5_lam_im", "s5_log_dt", "s5_b_re", "s5_b_im", "s5_c_re", "s5_c_im",
               "ffn_norm", "b_norm_mix", "kv_norm", "final_norm")
SMALL_PAD = 1024


def _pack(parts):
    flat = []
    for p in parts:
        f = p.reshape(-1)
        pad = (-f.shape[0]) % SMALL_PAD
        if pad:
            f = jnp.concatenate([f, jnp.zeros((pad,), f.dtype)])
        flat.append(f)
    return jnp.concatenate(flat).reshape(-1, 128)


def _unpack(packed, shapes):
    flat = packed.reshape(-1)
    out, off = [], 0
    for shp in shapes:
        size = math.prod(shp)
        out.append(flat[off:off + size].reshape(shp))
        off += size + (-size) % SMALL_PAD
    return out


def kernel(x, s5_lam_re, s5_lam_im, s5_log_dt, s5_b_re, s5_b_im, s5_c_re, s5_c_im, s5_d, s5_w_glu, a_norm_mix, ffn_norm, ffn_w_in, ffn_w_out, b_norm_mix, attn_w_q, attn_w_o, kv_norm, w_kv, final_norm, loss_target, m_s5_lam_re, m_s5_lam_im, m_s5_log_dt, m_s5_b_re, m_s5_b_im, m_s5_c_re, m_s5_c_im, m_s5_d, m_s5_w_glu, m_a_norm_mix, m_ffn_norm, m_ffn_w_in, m_ffn_w_out, m_b_norm_mix, m_attn_w_q, m_attn_w_o, m_kv_norm, m_w_kv, m_final_norm, v_s5_lam_re, v_s5_lam_im, v_s5_log_dt, v_s5_b_re, v_s5_b_im, v_s5_c_re, v_s5_c_im, v_s5_d, v_s5_w_glu, v_a_norm_mix, v_ffn_norm, v_ffn_w_in, v_ffn_w_out, v_b_norm_mix, v_attn_w_q, v_attn_w_o, v_kv_norm, v_w_kv, v_final_norm):
    args = dict(locals())
    T, D = x.shape[1], x.shape[2]
    n_layers = ffn_w_in.shape[0]
    xi_, yi_, ci_ = _coords()
    pos = jnp.stack([xi_, yi_, ci_]).astype(jnp.int32)
    me = 4 * xi_ + 2 * yi_ + ci_

    big_names = ["glu"] + [f"win{l}" for l in range(n_layers)] + [f"wout{l}" for l in range(n_layers)] \
        + ["wkv", "wq", "wo"]
    big_shards = [s5_w_glu[0]] + [ffn_w_in[l] for l in range(n_layers)] \
        + [ffn_w_out[l] for l in range(n_layers)] + [w_kv, attn_w_q[0], attn_w_o[0]]
    big_axes = [1] + [1] * n_layers + [0] * n_layers + [0, 1, 0]
    big_out_names = ["s5_w_glu"] + ["ffn_w_in"] * n_layers + ["ffn_w_out"] * n_layers \
        + ["w_kv", "attn_w_q", "attn_w_o"]
    index_of = {n: i for i, n in enumerate(big_names)}
    vec_shard = jnp.concatenate([s5_d, a_norm_mix], axis=0)
    (vecs,) = _all_gather("vectors_all_gather", [vec_shard], [1])

    gather_groups = [["glu"], ["win0"], ["wout0"], ["wkv", "wq", "wo"], ["win1", "wout1"]]
    group_idx = [[index_of[n] for n in grp] for grp in gather_groups]
    group_of = {n: gi for gi, grp in enumerate(gather_groups) for n in grp}
    placed = [_cast_and_place(f"cast_place_{n}", s, ax, pos) for n, s, ax in zip(big_names, big_shards, big_axes)]
    gather_sems, shards_thru, lands_thru, start_token = _gather_start(
        [p[0] for p in placed], [p[1] for p in placed], big_axes, group_idx)
    full = {}

    def wts(name, after):
        if name not in full:
            gi = group_of[name]
            idx = group_idx[gi]
            axes = [big_axes[a] for a in idx]
            send, r_d2d, r_ici = gather_sems[gi]
            f_send, f_recv, lands = _gather_forward(
                f"weights_gather_forward{gi}", [lands_thru[a] for a in idx], axes, r_ici, after)
            done = _gather_finish(f"weights_gather_finish{gi}", [shards_thru[a] for a in idx], lands, axes,
                                  send, r_d2d, f_send, f_recv)
            full.update(zip(gather_groups[gi], done))
        return full[name]

    exchanges = []

    def ready(grads):
        names = list(grads)
        axes = [big_axes[index_of[n]] for n in names]
        tag = "_".join(names)
        lands = _pair_exchange(f"rs_pair_exchange_{tag}", [grads[n] for n in names], axes)
        p0s, pbs = [], []
        for n, land, ax in zip(names, lands, axes):
            p0, pb = _pair_sum(f"rs_pair_sum_{n}", grads[n], land, ax, pos)
            p0s.append(p0)
            pbs.append(pb)
        started = _chip_exchange_start(f"rs_chip_exchange_start_{tag}", pbs)
        exchanges.append((names, p0s, started))
        return started[4]

    G, P, C = s5_b_re.shape[1:]
    w = dict(
        a_norm=vecs[1:2], s5_d=vecs[0:1],
        ffn_norm=ffn_norm, b_norm=b_norm_mix, kv_norm=kv_norm.reshape(1, D), final_norm=final_norm.reshape(1, D),
        lam_re=s5_lam_re[0], lam_im=s5_lam_im[0], log_dt=s5_log_dt.reshape(G, 1),
        bt_re=s5_b_re[0].transpose(2, 0, 1), bt_im=s5_b_im[0].transpose(2, 0, 1),
        c_re=s5_c_re[0], c_im=s5_c_im[0],
    )
    loss_blk, grad_x, g = _local_step(x[0], loss_target[0], w, wts, ready, dep0=start_token)
    loss = lax.psum(loss_blk[0, 0], ("x", "y", "c"))

    out = {}

    def put(name, res, shape):
        for kind, r in zip(("grad", "delta", "new_m", "new_v"), res):
            out[f"{kind}_{name}"] = r.reshape(shape)

    layered = {}

    def update(names, p0s, recvd):
        first = None
        for name, p0, rc in zip(names, p0s, recvd):
            oname, shard = big_out_names[index_of[name]], big_shards[index_of[name]]
            if oname in ("ffn_w_in", "ffn_w_out"):
                layer = int(name[-1])
                res = _adamw(f"adamw_{name}", [p0, rc], shard, args["m_" + oname][layer], args["v_" + oname][layer])
                layered.setdefault(oname, {})[layer] = res
            else:
                mm, vv = args["m_" + oname], args["v_" + oname]
                res = _adamw(f"adamw_{name}", [p0, rc], shard, mm.reshape(shard.shape), vv.reshape(shard.shape))
                put(oname, res, args[oname].shape)
            first = res[0] if first is None else first
        return first

    early, last = exchanges[:-1], exchanges[-1]
    landed = _chip_exchange_finish("rs_chip_exchange_finish_early", [e[2] for e in early], grad_x)
    marker = None
    for (names, p0s, _), recvd in zip(early, landed):
        marker = update(names, p0s, recvd)
    (recvd,) = _chip_exchange_finish("rs_chip_exchange_finish_last", [last[2]], marker)
    update(last[0], last[1], recvd)
    for oname, per_layer in layered.items():
        put(oname, [jnp.stack([per_layer[l][i] for l in range(n_layers)]) for i in range(4)], args[oname].shape)

    small_g = dict(
        s5_lam_re=g["lam_re"], s5_lam_im=g["lam_im"], s5_log_dt=g["log_dt"],
        s5_b_re=g["bt_re"].transpose(1, 2, 0), s5_b_im=g["bt_im"].transpose(1, 2, 0),
        s5_c_re=g["c_re"], s5_c_im=g["c_im"], ffn_norm=g["ffn_norm"], b_norm_mix=g["b_norm"],
        kv_norm=g["kv_norm"], final_norm=g["final_norm"])
    packed = _pack([small_g[n] for n in SMALL_NAMES] + [g["s5_d"], g["a_norm"]])
    rows = packed.shape[0]
    (all_parts,) = _all_gather("small_grads_all_gather", [packed], [0])
    all_parts = all_parts.reshape(N_DEV, rows, 128)
    n_rep_rows = _pack([small_g[n] for n in SMALL_NAMES]).shape[0]
    w_pack = _pack([args[n] for n in SMALL_NAMES])
    m_pack = _pack([args["m_" + n] for n in SMALL_NAMES])
    v_pack = _pack([args["v_" + n] for n in SMALL_NAMES])
    res = _adamw("adamw_small", [all_parts[:, :n_rep_rows]], w_pack, m_pack, v_pack)
    shapes = [args[n].shape for n in SMALL_NAMES]
    unpacked = [_unpack(r, shapes) for r in res]
    for i, n in enumerate(SMALL_NAMES):
        put(n, [u[i] for u in unpacked], args[n].shape)
    ws = D // N_DEV
    tail = all_parts[:, n_rep_rows:].reshape(N_DEV, 2, D)
    tail = lax.dynamic_slice_in_dim(tail, me * ws, ws, axis=2)
    res = _adamw("adamw_vec", [tail], vec_shard,
                 jnp.concatenate([m_s5_d, m_a_norm_mix], axis=0), jnp.concatenate([v_s5_d, v_a_norm_mix], axis=0))
    put("s5_d", [r[0:1] for r in res], s5_d.shape)
    put("a_norm_mix", [r[1:2] for r in res], a_norm_mix.shape)

    names = ("s5_lam_re", "s5_lam_im", "s5_log_dt", "s5_b_re", "s5_b_im", "s5_c_re", "s5_c_im", "s5_d",
             "s5_w_glu", "a_norm_mix", "ffn_norm", "ffn_w_in", "ffn_w_out", "b_norm_mix", "attn_w_q",
             "attn_w_o", "kv_norm", "w_kv", "final_norm")
    result = [loss, grad_x.reshape(x.shape)]
    for kind in ("grad", "delta", "new_m", "new_v"):
        result += [out[f"{kind}_{n}"] for n in names]
    return tuple(result)
```

```python
import functools
import math

import jax
import jax.numpy as jnp
from jax import lax
from jax.experimental import pallas as pl
from jax.experimental.pallas import tpu as pltpu

F32 = jnp.float32
BF16 = jnp.bfloat16

EPS = 1e-6
NEG_INF = -1e30
HEAD_DIM = 128
N_KV_HEADS = 4
DILATIONS = (1, 4, 16)
ATT_BLK = 128
S5_C = 16
S5_P = 64
S5_GB = 16
S5_CH = S5_GB * S5_C
S5_W = S5_GB * S5_P
S5_UNROLL = 4
N_DEV = 8

ADAM_LR = 0.001
ADAM_B1 = 0.9
ADAM_B2 = 0.999
ADAM_EPS = 1e-08
ADAM_WD = 0.01
ADAM_STEP = 10

VMEM_LIMIT_BYTES = 56 * 1024 * 1024
MM_TILE = 1024
MM_TILE_NARROW = 512
MM_DEPTH = 2816
MESH = pl.DeviceIdType.MESH
ANY = pl.BlockSpec(memory_space=pl.ANY)


def _tile(n, pref, align=128):
    t = (min(pref, n) // align) * align
    while t >= align:
        if n % t == 0:
            return t
        t -= align
    return n


def _params(sem):
    return pltpu.CompilerParams(dimension_semantics=sem, vmem_limit_bytes=VMEM_LIMIT_BYTES)


def _sigmoid(x):
    return 1.0 / (1.0 + jnp.exp(-x))


NN = (((1,), (0,)), ((), ()))
NT = (((1,), (1,)), ((), ()))
TN = (((0,), (0,)), ((), ()))


def _dot(a, b, dims=NN):
    return lax.dot_general(a, b, dims, preferred_element_type=F32)


def _matmul(name, grid, ins, in_specs, products, dims, out_shapes, out_specs, acc_shapes, epilogue):
    n_in, n_out, nk = len(ins), len(out_shapes), grid[2]

    def body(*refs):
        in_refs = refs[:n_in]
        out_refs = refs[n_in:n_in + n_out]
        acc_refs = refs[n_in + n_out:]

        def prods():
            vals = [None] * len(acc_shapes)
            for ai, bi, ci in products:
                d = _dot(in_refs[ai][...].astype(BF16), in_refs[bi][...].astype(BF16), dims)
                vals[ci] = d if vals[ci] is None else vals[ci] + d
            return vals

        if nk == 1:
            epilogue(in_refs, out_refs, prods())
        else:
            k = pl.program_id(2)

            @pl.when(k == 0)
            def _():
                for a in acc_refs:
                    a[...] = jnp.zeros_like(a)

            for a, v in zip(acc_refs, prods()):
                a[...] += v

            @pl.when(k == nk - 1)
            def _():
                epilogue(in_refs, out_refs, [a[...] for a in acc_refs])

    scratch = [] if nk == 1 else [pltpu.VMEM(s, F32) for s in acc_shapes]
    return pl.pallas_call(
        body, grid=grid, in_specs=in_specs, out_specs=out_specs, out_shape=out_shapes,
        scratch_shapes=scratch, compiler_params=_params(("parallel", "parallel", "arbitrary")),
        name=name)(*ins)


def _mm_dual_fwd(name, a, w, res, kind):
    T, K = a.shape
    N = w.shape[1] // 2
    tm, tn = _tile(T, MM_TILE), _tile(N, MM_TILE_NARROW)
    nj = N // tn
    grid = (T // tm, nj, 1)
    ins = [a, w, w]
    in_specs = [pl.BlockSpec((tm, K), lambda i, j, k: (i, 0)),
                pl.BlockSpec((K, tn), lambda i, j, k: (0, j)),
                pl.BlockSpec((K, tn), lambda i, j, k: (0, j + nj))]
    pair_spec = pl.BlockSpec((2, tm, tn), lambda i, j, k: (0, i, j))
    tile_spec = pl.BlockSpec((tm, tn), lambda i, j, k: (i, j))
    if kind == "glu":
        ins.append(res)
        in_specs.append(tile_spec)

        def epilogue(in_refs, out_refs, accs):
            val, gate = accs
            out_refs[0][...] = in_refs[3][...] + val * _sigmoid(gate)
            out_refs[1][0] = val.astype(BF16)
            out_refs[1][1] = gate.astype(BF16)

        out_shapes = [jax.ShapeDtypeStruct((T, N), F32), jax.ShapeDtypeStruct((2, T, N), BF16)]
        out_specs = [tile_spec, pair_spec]
    else:
        def epilogue(in_refs, out_refs, accs):
            g, u = accs
            out_refs[0][0] = g.astype(BF16)
            out_refs[0][1] = u.astype(BF16)
            out_refs[1][...] = (g * _sigmoid(g) * u).astype(BF16)

        out_shapes = [jax.ShapeDtypeStruct((2, T, N), BF16), jax.ShapeDtypeStruct((T, N), BF16)]
        out_specs = [pair_spec, tile_spec]
    return _matmul(name, grid, ins, in_specs, [(0, 1, 0), (0, 2, 1)], NN, out_shapes, out_specs,
                   [(tm, tn), (tm, tn)], epilogue)


def _mm_kv(name, a, w):
    T, K = a.shape
    N = w.shape[1] // 2
    tm, tn = _tile(T, MM_TILE), _tile(N, MM_TILE_NARROW)
    nj = N // tn
    tile_spec = pl.BlockSpec((tm, tn), lambda i, j, k: (i, j))

    def epilogue(in_refs, out_refs, accs):
        out_refs[0][...] = accs[0]
        out_refs[1][...] = accs[1]

    return _matmul(name, (T // tm, nj, 1), [a, w, w],
                   [pl.BlockSpec((tm, K), lambda i, j, k: (i, 0)),
                    pl.BlockSpec((K, tn), lambda i, j, k: (0, j)),
                    pl.BlockSpec((K, tn), lambda i, j, k: (0, j + nj))],
                   [(0, 1, 0), (0, 2, 1)], NN,
                   [jax.ShapeDtypeStruct((T, N), F32)] * 2, [tile_spec, tile_spec],
                   [(tm, tn), (tm, tn)], epilogue)


def _mm_nn(name, a, w, res=None, out_dtype=F32):
    T, K = a.shape
    N = w.shape[1]
    tk = _tile(K, MM_DEPTH)
    tm, tn = _tile(T, MM_TILE), _tile(N, MM_TILE if tk == K else MM_TILE_NARROW)
    grid = (T // tm, N // tn, K // tk)
    tile_spec = pl.BlockSpec((tm, tn), lambda i, j, k: (i, j))
    ins = [a, w]
    in_specs = [pl.BlockSpec((tm, tk), lambda i, j, k: (i, k)),
                pl.BlockSpec((tk, tn), lambda i, j, k: (k, j))]
    if res is not None:
        ins.append(res)
        in_specs.append(tile_spec)

    def epilogue(in_refs, out_refs, accs):
        v = accs[0]
        if res is not None:
            v = v + in_refs[2][...]
        out_refs[0][...] = v.astype(out_dtype)

    return _matmul(name, grid, ins, in_specs, [(0, 1, 0)], NN,
                   [jax.ShapeDtypeStruct((T, N), out_dtype)], [tile_spec], [(tm, tn)], epilogue)[0]


def _dep_operand(ins, in_specs, dep):
    if dep is not None:
        ins.append(dep)
        in_specs.append(pl.BlockSpec((8, 128), lambda *_: (0, 0)))


def _mm_nt(name, a_list, w, out_dtype=F32, dep=None):
    T, Np = a_list[0].shape
    Ko = w.shape[0]
    n_parts = len(a_list)
    wide_a = a_list[0].dtype != BF16
    tm, tn, tk = _tile(T, MM_TILE_NARROW if wide_a else MM_TILE), _tile(Ko, MM_TILE), _tile(Np, MM_DEPTH)
    nkp = Np // tk
    grid = (T // tm, Ko // tn, nkp)
    ins = list(a_list) + [w] * n_parts
    in_specs = [pl.BlockSpec((tm, tk), lambda i, j, k: (i, k)) for _ in a_list]
    in_specs += [pl.BlockSpec((tn, tk), functools.partial(lambda i, j, k, p: (j, p * nkp + k), p=p))
                 for p in range(n_parts)]
    products = [(p, n_parts + p, 0) for p in range(n_parts)]
    _dep_operand(ins, in_specs, dep)

    def epilogue(in_refs, out_refs, accs):
        out_refs[0][...] = accs[0].astype(out_dtype)

    return _matmul(name, grid, ins, in_specs, products, NT,
                   [jax.ShapeDtypeStruct((T, Ko), out_dtype)],
                   [pl.BlockSpec((tm, tn), lambda i, j, k: (i, j))], [(tm, tn)], epilogue)[0]


def _mm_nt_pair(name, a3, w):
    _, T, N = a3.shape
    Ko = w.shape[0]
    tm, tn, tk = _tile(T, MM_TILE), _tile(Ko, MM_TILE), _tile(N, MM_DEPTH)
    nkh = N // tk
    grid = (T // tm, Ko // tn, 2 * nkh)

    def epilogue(in_refs, out_refs, accs):
        out_refs[0][...] = accs[0]

    return _matmul(name, grid, [a3, w],
                   [pl.BlockSpec((None, tm, tk), lambda i, j, k: (k // nkh, i, k % nkh)),
                    pl.BlockSpec((tn, tk), lambda i, j, k: (j, k))],
                   [(0, 1, 0)], NT, [jax.ShapeDtypeStruct((T, Ko), F32)],
                   [pl.BlockSpec((tm, tn), lambda i, j, k: (i, j))], [(tm, tn)], epilogue)[0]


def _mm_nt_ffn_bwd(name, dx, w_out, gu, dep=None):
    T, D = dx.shape
    Fh = w_out.shape[0]
    tm, tn = _tile(T, MM_TILE), _tile(Fh, MM_TILE_NARROW)
    pair_spec = pl.BlockSpec((2, tm, tn), lambda i, j, k: (0, i, j))

    def epilogue(in_refs, out_refs, accs):
        da = accs[0]
        g = in_refs[2][0].astype(F32)
        u = in_refs[2][1].astype(F32)
        s = _sigmoid(g)
        out_refs[0][0] = (da * u * (s * (1.0 + g * (1.0 - s)))).astype(BF16)
        out_refs[0][1] = (da * (g * s)).astype(BF16)

    ins = [dx, w_out, gu]
    in_specs = [pl.BlockSpec((tm, D), lambda i, j, k: (i, 0)),
                pl.BlockSpec((tn, D), lambda i, j, k: (j, 0)),
                pair_spec]
    _dep_operand(ins, in_specs, dep)
    return _matmul(name, (T // tm, Fh // tn, 1), ins, in_specs,
                   [(0, 1, 0)], NT, [jax.ShapeDtypeStruct((2, T, Fh), BF16)], [pair_spec],
                   [(tm, tn)], epilogue)[0]


def _mm_tn(name, a, d):
    T, Ko = a.shape
    N = d.shape[1]
    to, tn, tk = _tile(Ko, MM_TILE_NARROW), _tile(N, MM_TILE if d.dtype == BF16 else MM_TILE_NARROW), T
    grid = (Ko // to, N // tn, T // tk)

    def epilogue(in_refs, out_refs, accs):
        out_refs[0][...] = accs[0].astype(BF16)

    return _matmul(name, grid, [a, d],
                   [pl.BlockSpec((tk, to), lambda i, j, k: (k, i)),
                    pl.BlockSpec((tk, tn), lambda i, j, k: (k, j))],
                   [(0, 1, 0)], TN, [jax.ShapeDtypeStruct((Ko, N), BF16)],
                   [pl.BlockSpec((to, tn), lambda i, j, k: (i, j))], [(to, tn)], epilogue)[0]


def _mm_tn_pair(name, a, d3):
    T, Ko = a.shape
    N = d3.shape[2]
    to, tn, tk = _tile(Ko, MM_TILE), _tile(N, MM_TILE_NARROW), T
    njh = N // tn
    grid = (Ko // to, 2 * njh, T // tk)

    def epilogue(in_refs, out_refs, accs):
        out_refs[0][...] = accs[0].astype(BF16)

    return _matmul(name, grid, [a, d3],
                   [pl.BlockSpec((tk, to), lambda i, j, k: (k, i)),
                    pl.BlockSpec((None, tk, tn), lambda i, j, k: (j // njh, k, j % njh))],
                   [(0, 1, 0)], TN, [jax.ShapeDtypeStruct((Ko, 2 * N), BF16)],
                   [pl.BlockSpec((to, tn), lambda i, j, k: (i, j))], [(to, tn)], epilogue)[0]


def _rms_fwd(name, x, gains, dtypes, dep=None):
    T, D = x.shape
    n = len(gains)
    tr = _tile(T, 512, 8)
    n_dep = 0 if dep is None else 1

    def body(x_ref, *refs):
        xv = x_ref[...]
        xr = xv * lax.rsqrt(jnp.mean(xv * xv, axis=-1, keepdims=True) + EPS)
        for g_ref, o_ref in zip(refs[:n], refs[n + n_dep:]):
            o_ref[...] = (xr * g_ref[...]).astype(o_ref.dtype)

    row = pl.BlockSpec((tr, D), lambda i: (i, 0))
    vec = pl.BlockSpec((1, D), lambda i: (0, 0))
    ins, in_specs = [x, *gains], [row] + [vec] * n
    _dep_operand(ins, in_specs, dep)
    return pl.pallas_call(
        body, grid=(T // tr,), in_specs=in_specs, out_specs=[row] * n,
        out_shape=[jax.ShapeDtypeStruct((T, D), dt) for dt in dtypes],
        compiler_params=_params(("parallel",)), name=name)(*ins)


def _rms_bwd(name, x, dres, gains, dhs):
    T, D = x.shape
    n = len(gains)
    tr = _tile(T, 256, 8)

    def body(x_ref, dres_ref, *refs):
        g_refs, dh_refs = refs[:n], refs[n:2 * n]
        dx_ref, dxb_ref = refs[2 * n], refs[2 * n + 1]
        dg_refs = refs[2 * n + 2:]
        xv = x_ref[...]
        r = lax.rsqrt(jnp.mean(xv * xv, axis=-1, keepdims=True) + EPS)
        xr = xv * r
        w = None
        for g_ref, dh_ref, dg_ref in zip(g_refs, dh_refs, dg_refs):
            dh = dh_ref[...].astype(F32)

            @pl.when(pl.program_id(0) == 0)
            def _():
                dg_ref[...] = jnp.zeros_like(dg_ref)

            dg_ref[...] += jnp.sum(dh * xr, axis=0, keepdims=True)
            wi = dh * g_ref[...]
            w = wi if w is None else w + wi
        dx = dres_ref[...] + r * (w - xr * jnp.mean(w * xr, axis=-1, keepdims=True))
        dx_ref[...] = dx
        dxb_ref[...] = dx.astype(BF16)

    row = pl.BlockSpec((tr, D), lambda i: (i, 0))
    vec = pl.BlockSpec((1, D), lambda i: (0, 0))
    outs = pl.pallas_call(
        body, grid=(T // tr,), in_specs=[row, row] + [vec] * n + [row] * n,
        out_specs=[row, row] + [vec] * n,
        out_shape=[jax.ShapeDtypeStruct((T, D), F32), jax.ShapeDtypeStruct((T, D), BF16)]
        + [jax.ShapeDtypeStruct((1, D), F32)] * n,
        compiler_params=_params(("arbitrary",)), name=name)(x, dres, *gains, *dhs)
    return outs[0], outs[1], outs[2:]


def _loss_head(x, gain, target):
    T, D = x.shape
    tr = _tile(T, 256, 8)

    def body(x_ref, g_ref, t_ref, loss_ref, dx_ref, dxb_ref, dg_ref):
        @pl.when(pl.program_id(0) == 0)
        def _():
            loss_ref[...] = jnp.zeros_like(loss_ref)
            dg_ref[...] = jnp.zeros_like(dg_ref)

        xv = x_ref[...]
        r = lax.rsqrt(jnp.mean(xv * xv, axis=-1, keepdims=True) + EPS)
        xr = xv * r
        err = xr * g_ref[...] - t_ref[...]
        part = jnp.sum(jnp.sum(err * err, axis=-1, keepdims=True), axis=0, keepdims=True) * (0.5 / D)
        loss_ref[...] += jnp.broadcast_to(part, loss_ref.shape)
        dy = err * (1.0 / D)
        dg_ref[...] += jnp.sum(dy * xr, axis=0, keepdims=True)
        w = dy * g_ref[...]
        dx = r * (w - xr * jnp.mean(w * xr, axis=-1, keepdims=True))
        dx_ref[...] = dx
        dxb_ref[...] = dx.astype(BF16)

    row = pl.BlockSpec((tr, D), lambda i: (i, 0))
    vec = pl.BlockSpec((1, D), lambda i: (0, 0))
    return pl.pallas_call(
        body, grid=(T // tr,), in_specs=[row, vec, row],
        out_specs=[pl.BlockSpec((8, 128), lambda i: (0, 0)), row, row, vec],
        out_shape=[jax.ShapeDtypeStruct((8, 128), F32), jax.ShapeDtypeStruct((T, D), F32),
                   jax.ShapeDtypeStruct((T, D), BF16), jax.ShapeDtypeStruct((1, D), F32)],
        compiler_params=_params(("arbitrary",)), name="loss_head")(x, gain, target)


def _glu_bwd(dmix, vg, dep=None):
    T, N = dmix.shape
    tr, tc = _tile(T, 512, 8), _tile(N, 1024)
    n_dep = 0 if dep is None else 1

    def body(d_ref, vg_ref, *refs):
        o_ref = refs[n_dep]
        d = d_ref[...]
        val = vg_ref[0].astype(F32)
        s = _sigmoid(vg_ref[1].astype(F32))
        o_ref[0] = (d * s).astype(BF16)
        o_ref[1] = (d * val * s * (1.0 - s)).astype(BF16)

    pair = pl.BlockSpec((2, tr, tc), lambda i, j: (0, i, j))
    ins, in_specs = [dmix, vg], [pl.BlockSpec((tr, tc), lambda i, j: (i, j)), pair]
    _dep_operand(ins, in_specs, dep)
    return pl.pallas_call(
        body, grid=(T // tr, N // tc), in_specs=in_specs,
        out_specs=pair, out_shape=jax.ShapeDtypeStruct((2, T, N), BF16),
        compiler_params=_params(("parallel", "parallel")), name="glu_bwd")(*ins)


def _to_state_tiles(x_ref, s, val):
    tc = val.shape[0]
    for j in range(S5_W // 128):
        x_ref[s, pl.ds(j, tc, stride=8), :] = val[:, 128 * j:128 * (j + 1)]


def _from_state_tiles(x_ref, s, tc):
    return jnp.concatenate([x_ref[s, pl.ds(j, tc, stride=8), :] for j in range(S5_W // 128)], axis=1)


def _s5_scan_fwd(xr_ref, xi_ref, ar_ref, ai_ref, cr_ref, ci_ref, tc, nblk):
    a = [(ar_ref[s], ai_ref[s]) for s in range(nblk)]

    def step(i, carry):
        carry = list(carry)
        for uu in range(S5_UNROLL):
            r0 = pl.multiple_of((i * S5_UNROLL + uu) * 8, 8)
            for s in range(nblk):
                cr, ci = carry[2 * s], carry[2 * s + 1]
                a_r, a_i = a[s]
                xr = a_r * cr - a_i * ci + xr_ref[s, pl.ds(r0, 8), :]
                xi = a_r * ci + a_i * cr + xi_ref[s, pl.ds(r0, 8), :]
                xr_ref[s, pl.ds(r0, 8), :] = xr
                xi_ref[s, pl.ds(r0, 8), :] = xi
                carry[2 * s], carry[2 * s + 1] = xr, xi
        return tuple(carry)

    init = []
    for s in range(nblk):
        init += [cr_ref[s], ci_ref[s]]
    out = lax.fori_loop(0, tc // S5_UNROLL, step, tuple(init))
    for s in range(nblk):
        cr_ref[s] = out[2 * s]
        ci_ref[s] = out[2 * s + 1]


def _s5_scan_bwd(lr_ref, li_ref, xr_ref, xi_ref, h_ref, ar_ref, ai_ref, cr_ref, ci_ref,
                 accr_ref, acci_ref, tc, nblk):
    a = [(ar_ref[s], ai_ref[s]) for s in range(nblk)]

    def one(s, r0, prev_r, prev_i, st):
        c_r, c_i, d_r, d_i = st
        a_r, a_i = a[s]
        l_r = lr_ref[s, pl.ds(r0, 8), :] + a_r * c_r + a_i * c_i
        l_i = li_ref[s, pl.ds(r0, 8), :] + a_r * c_i - a_i * c_r
        lr_ref[s, pl.ds(r0, 8), :] = l_r
        li_ref[s, pl.ds(r0, 8), :] = l_i
        return [l_r, l_i, d_r + l_r * prev_r + l_i * prev_i, d_i - l_r * prev_i + l_i * prev_r]

    def step(i, carry):
        carry = list(carry)
        for uu in range(S5_UNROLL):
            t = tc - 1 - (i * S5_UNROLL + uu)
            r0 = pl.multiple_of(t * 8, 8)
            p0 = pl.multiple_of((t - 1) * 8, 8)
            for s in range(nblk):
                carry[4 * s:4 * s + 4] = one(s, r0, xr_ref[s, pl.ds(p0, 8), :], xi_ref[s, pl.ds(p0, 8), :],
                                             carry[4 * s:4 * s + 4])
        return tuple(carry)

    init = []
    for s in range(nblk):
        init += [cr_ref[s], ci_ref[s], accr_ref[s], acci_ref[s]]
    carry = list(lax.fori_loop(0, tc // S5_UNROLL - 1, step, tuple(init)))
    for t in range(S5_UNROLL - 1, -1, -1):
        for s in range(nblk):
            if t > 0:
                prev_r, prev_i = xr_ref[s, 8 * (t - 1):8 * t, :], xi_ref[s, 8 * (t - 1):8 * t, :]
            else:
                prev_r, prev_i = h_ref[0, s], h_ref[1, s]
            carry[4 * s:4 * s + 4] = one(s, 8 * t, prev_r, prev_i, carry[4 * s:4 * s + 4])
    for s in range(nblk):
        cr_ref[s], ci_ref[s], accr_ref[s], acci_ref[s] = carry[4 * s:4 * s + 4]


def _gelu(y):
    k = math.sqrt(2.0 / math.pi)
    return 0.5 * y * (1.0 + jnp.tanh(k * (y + 0.044715 * (y * y * y))))


def _gelu_grad(y):
    k = math.sqrt(2.0 / math.pi)
    t = jnp.tanh(k * (y + 0.044715 * (y * y * y)))
    return 0.5 * (1.0 + t) + 0.5 * y * (1.0 - t * t) * (k * (1.0 + 3.0 * 0.044715 * (y * y)))


def _s5_specs(tc, nch, sbk, rev):
    def ch(c):
        return nch - 1 - c if rev else c

    return dict(
        act=pl.BlockSpec((tc, sbk * S5_CH), lambda i, c: (ch(c), i)),
        bb=pl.BlockSpec((sbk, S5_CH, S5_W), lambda i, c: (i, 0, 0)),
        cc=pl.BlockSpec((sbk, S5_W, S5_CH), lambda i, c: (i, 0, 0)),
        a=pl.BlockSpec((sbk, 8, 128), lambda i, c: (i, 0, 0)),
        d=pl.BlockSpec((1, sbk * S5_CH), lambda i, c: (0, i)),
        h=pl.BlockSpec((None, 2, sbk, 8, 128), lambda i, c: (ch(c), 0, i, 0, 0)),
    )


def _s5_blocks(nb, pref):
    return max(b for b in range(1, pref + 1) if nb % b == 0)


def _s5_fwd(u, bbr, bbi, ccr, cci, a_re, a_im, d_skip):
    T, D = u.shape
    nb = D // S5_CH
    sbk = _s5_blocks(nb, 4)
    tc = _tile(T, 512, 8)
    nch = T // tc
    sp = _s5_specs(tc, nch, sbk, False)

    def body(u_ref, bbr_ref, bbi_ref, ccr_ref, cci_ref, ar_ref, ai_ref, d_ref, z_ref, h_ref,
             xr, xi, cr, ci):
        @pl.when(pl.program_id(1) == 0)
        def _():
            cr[...] = jnp.zeros_like(cr)
            ci[...] = jnp.zeros_like(ci)

        h_ref[0] = cr[...]
        h_ref[1] = ci[...]
        for s in range(sbk):
            ub = u_ref[:, s * S5_CH:(s + 1) * S5_CH].astype(BF16)
            _to_state_tiles(xr, s, _dot(ub, bbr_ref[s]))
            _to_state_tiles(xi, s, _dot(ub, bbi_ref[s]))
        _s5_scan_fwd(xr, xi, ar_ref, ai_ref, cr, ci, tc, sbk)
        for s in range(sbk):
            cols = slice(s * S5_CH, (s + 1) * S5_CH)
            y = (_dot(_from_state_tiles(xr, s, tc).astype(BF16), ccr_ref[s])
                 - _dot(_from_state_tiles(xi, s, tc).astype(BF16), cci_ref[s])
                 + d_ref[:, cols] * u_ref[:, cols])
            z_ref[:, cols] = _gelu(y).astype(BF16)

    tiles = pltpu.VMEM((sbk, tc * 8, 128), F32)
    carry = pltpu.VMEM((sbk, 8, 128), F32)
    return pl.pallas_call(
        body, grid=(nb // sbk, nch),
        in_specs=[sp["act"], sp["bb"], sp["bb"], sp["cc"], sp["cc"], sp["a"], sp["a"], sp["d"]],
        out_specs=[sp["act"], sp["h"]],
        out_shape=[jax.ShapeDtypeStruct((T, D), BF16), jax.ShapeDtypeStruct((nch, 2, nb, 8, 128), F32)],
        scratch_shapes=[tiles, tiles, carry, carry],
        compiler_params=_params(("parallel", "arbitrary")), name="s5_fwd",
    )(u, bbr, bbi, ccr, cci, a_re, a_im, d_skip)


def _s5_bwd(u, dz, h0, bbr, bbi, ccr, cci, a_re, a_im, d_skip):
    T, D = u.shape
    nb = D // S5_CH
    sbk = _s5_blocks(nb, 2)
    tc = _tile(T, 512, 8)
    nch = T // tc
    sp = _s5_specs(tc, nch, sbk, True)

    def body(u_ref, dz_ref, h_ref, bbr_ref, bbi_ref, ccr_ref, cci_ref, ar_ref, ai_ref, d_ref,
             du_ref, dd_ref, dar_ref, dai_ref, dbbr_ref, dbbi_ref, dccr_ref, dcci_ref,
             xr, xi, lr, li, fr, fi, br, bi, accr, acci):
        c = pl.program_id(1)

        @pl.when(c == 0)
        def _():
            for ref in (br, bi, accr, acci, dd_ref, dbbr_ref, dbbi_ref, dccr_ref, dcci_ref):
                ref[...] = jnp.zeros_like(ref)

        for s in range(sbk):
            ub = u_ref[:, s * S5_CH:(s + 1) * S5_CH].astype(BF16)
            _to_state_tiles(xr, s, _dot(ub, bbr_ref[s]))
            _to_state_tiles(xi, s, _dot(ub, bbi_ref[s]))
        fr[...] = h_ref[0]
        fi[...] = h_ref[1]
        _s5_scan_fwd(xr, xi, ar_ref, ai_ref, fr, fi, tc, sbk)
        for s in range(sbk):
            cols = slice(s * S5_CH, (s + 1) * S5_CH)
            uv = u_ref[:, cols]
            xrb = _from_state_tiles(xr, s, tc).astype(BF16)
            xib = _from_state_tiles(xi, s, tc).astype(BF16)
            dsk = d_ref[:, cols]
            y = _dot(xrb, ccr_ref[s]) - _dot(xib, cci_ref[s]) + dsk * uv
            dy = dz_ref[:, cols] * _gelu_grad(y)
            dd_ref[:, cols] += jnp.sum(dy * uv, axis=0, keepdims=True)
            dyb = dy.astype(BF16)
            dccr_ref[s] += _dot(xrb, dyb, TN)
            dcci_ref[s] += _dot(xib, dyb, TN)
            _to_state_tiles(lr, s, _dot(dyb, ccr_ref[s], NT))
            _to_state_tiles(li, s, -_dot(dyb, cci_ref[s], NT))
            du_ref[:, cols] = dy * dsk
        _s5_scan_bwd(lr, li, xr, xi, h_ref, ar_ref, ai_ref, br, bi, accr, acci, tc, sbk)
        for s in range(sbk):
            cols = slice(s * S5_CH, (s + 1) * S5_CH)
            ub = u_ref[:, cols].astype(BF16)
            lrb = _from_state_tiles(lr, s, tc).astype(BF16)
            lib = _from_state_tiles(li, s, tc).astype(BF16)
            dbbr_ref[s] += _dot(ub, lrb, TN)
            dbbi_ref[s] += _dot(ub, lib, TN)
            du_ref[:, cols] += _dot(lrb, bbr_ref[s], NT) + _dot(lib, bbi_ref[s], NT)

        @pl.when(c == nch - 1)
        def _():
            dar_ref[...] = accr[...]
            dai_ref[...] = acci[...]

    tiles = pltpu.VMEM((sbk, tc * 8, 128), F32)
    carry = pltpu.VMEM((sbk, 8, 128), F32)
    return pl.pallas_call(
        body, grid=(nb // sbk, nch),
        in_specs=[sp["act"], sp["act"], sp["h"], sp["bb"], sp["bb"], sp["cc"], sp["cc"],
                  sp["a"], sp["a"], sp["d"]],
        out_specs=[sp["act"], sp["d"], sp["a"], sp["a"], sp["bb"], sp["bb"], sp["cc"], sp["cc"]],
        out_shape=[jax.ShapeDtypeStruct((T, D), F32), jax.ShapeDtypeStruct((1, D), F32),
                   jax.ShapeDtypeStruct((nb, 8, 128), F32), jax.ShapeDtypeStruct((nb, 8, 128), F32),
                   jax.ShapeDtypeStruct((nb, S5_CH, S5_W), F32), jax.ShapeDtypeStruct((nb, S5_CH, S5_W), F32),
                   jax.ShapeDtypeStruct((nb, S5_W, S5_CH), F32), jax.ShapeDtypeStruct((nb, S5_W, S5_CH), F32)],
        scratch_shapes=[tiles, tiles, tiles, tiles, carry, carry, carry, carry, carry, carry],
        compiler_params=_params(("parallel", "arbitrary")), name="s5_bwd",
    )(u, dz, h0, bbr, bbi, ccr, cci, a_re, a_im, d_skip)


def _s5_disc(lr, li, ldt):
    dt = jnp.exp(ldt)
    mag = jnp.exp(lr * dt)
    ang = li * dt
    cs, sn = jnp.cos(ang), jnp.sin(ang)
    lbr, lbi = mag * cs, mag * sn
    nr = lbr - 1.0
    den = lr * lr + li * li
    f_re = (nr * lr + lbi * li) / den
    f_im = (lbi * lr - nr * li) / den
    return dt, mag, cs, sn, lbr, lbi, nr, den, f_re, f_im


def _s5_param_fwd(lr, li, ldt, bt_re, bt_im):
    c, g, p = bt_re.shape

    def body(lr_ref, li_ref, ldt_ref, br_ref, bi_ref, lbr_ref, lbi_ref, bbr_ref, bbi_ref):
        _, _, _, _, lbr, lbi, _, _, f_re, f_im = _s5_disc(lr_ref[...], li_ref[...], ldt_ref[...])
        lbr_ref[...] = lbr
        lbi_ref[...] = lbi
        for ch in range(c):
            b_r, b_i = br_ref[ch], bi_ref[ch]
            bbr_ref[ch] = f_re * b_r - f_im * b_i
            bbi_ref[ch] = f_re * b_i + f_im * b_r

    gp = jax.ShapeDtypeStruct((g, p), F32)
    cgp = jax.ShapeDtypeStruct((c, g, p), F32)
    return pl.pallas_call(body, out_shape=[gp, gp, cgp, cgp], name="s5_param_fwd")(lr, li, ldt, bt_re, bt_im)


def _s5_param_bwd(lr, li, ldt, bt_re, bt_im, dlbr, dlbi, dbbr, dbbi):
    c, g, p = bt_re.shape

    def body(lr_ref, li_ref, ldt_ref, br_ref, bi_ref, dlbr_ref, dlbi_ref, dbbr_ref, dbbi_ref,
             dlr_ref, dli_ref, dldt_ref, dbr_ref, dbi_ref):
        l_r, l_i = lr_ref[...], li_ref[...]
        dt, mag, cs, sn, lbr, lbi, nr, den, f_re, f_im = _s5_disc(l_r, l_i, ldt_ref[...])
        dfr = jnp.zeros_like(l_r)
        dfi = jnp.zeros_like(l_r)
        for ch in range(c):
            b_r, b_i = br_ref[ch], bi_ref[ch]
            g_r, g_i = dbbr_ref[ch], dbbi_ref[ch]
            dbr_ref[ch] = f_re * g_r + f_im * g_i
            dbi_ref[ch] = f_re * g_i - f_im * g_r
            dfr = dfr + g_r * b_r + g_i * b_i
            dfi = dfi + g_i * b_r - g_r * b_i
        inv = 1.0 / den
        d_nr = (dfr * l_r - dfi * l_i) * inv
        d_lbi = (dfr * l_i + dfi * l_r) * inv + dlbi_ref[...]
        d_lbr = d_nr + dlbr_ref[...]
        d_den = -(dfr * f_re + dfi * f_im) * inv
        d_mag = d_lbr * cs + d_lbi * sn
        d_ang = d_lbi * lbr - d_lbr * lbi
        dlr_ref[...] = (dfr * nr + dfi * lbi) * inv + 2.0 * d_den * l_r + d_mag * mag * dt
        dli_ref[...] = (dfr * lbi - dfi * nr) * inv + 2.0 * d_den * l_i + d_ang * dt
        dldt_ref[...] = jnp.sum(d_mag * mag * l_r + d_ang * l_i, axis=1, keepdims=True) * dt

    gp = jax.ShapeDtypeStruct((g, p), F32)
    cgp = jax.ShapeDtypeStruct((c, g, p), F32)
    return pl.pallas_call(body, out_shape=[gp, gp, jax.ShapeDtypeStruct((g, 1), F32), cgp, cgp],
                          name="s5_param_bwd")(lr, li, ldt, bt_re, bt_im, dlbr, dlbi, dbbr, dbbi)


def _block_diag(t, inner_first):
    g, c, p = t.shape
    nb = g // S5_GB
    t4 = t.reshape(nb, S5_GB, c, p)
    eye = jnp.eye(S5_GB, dtype=t.dtype)
    if inner_first:
        e = t4[:, :, :, None, :] * eye[None, :, None, :, None]
        return e.reshape(nb, S5_GB * c, S5_GB * p)
    t4 = t4.transpose(0, 1, 3, 2)
    e = t4[:, :, :, None, :] * eye[None, :, None, :, None]
    return e.reshape(nb, S5_GB * p, S5_GB * c)


def _block_diag_extract(e, inner_first, c, p):
    nb = e.shape[0]
    eye = jnp.eye(S5_GB, dtype=e.dtype)
    if inner_first:
        e5 = e.reshape(nb, S5_GB, c, S5_GB, p)
        return (e5 * eye[None, :, None, :, None]).sum(3).reshape(nb * S5_GB, c, p)
    e5 = e.reshape(nb, S5_GB, p, S5_GB, c)
    return (e5 * eye[None, :, None, :, None]).sum(3).transpose(0, 1, 3, 2).reshape(nb * S5_GB, c, p)


def _att_masks(rep, gb):
    rows = rep * ATT_BLK
    qi = lax.broadcasted_iota(jnp.int32, (rows, 2 * ATT_BLK), 0) % ATT_BLK
    si = lax.broadcasted_iota(jnp.int32, (rows, 2 * ATT_BLK), 1)
    prev = (si < ATT_BLK) & (si >= qi) & (gb > 0)
    cur = (si >= ATT_BLK) & (si - ATT_BLK <= qi)
    return prev | cur


def _att_rows(start, dil):
    return pl.ds(start, ATT_BLK) if dil == 1 else pl.ds(start, ATT_BLK, stride=dil)


def _att_plan(T, dil):
    span = ATT_BLK * dil
    sbr = max(span, min(T, 1024))
    return span, sbr, T // sbr


def _att_block(sb, i, sbr, span, dil):
    loc = (i // dil) * span + i % dil
    cur = sb * sbr + loc
    gb = sb * (sbr // span) + i // dil
    return loc, cur, jnp.where(gb > 0, cur - span, cur), gb


def _att_fwd(q, k, v, grp, dil):
    T = q.shape[0]
    H = q.shape[1] // HEAD_DIM // len(DILATIONS)
    rep = H // N_KV_HEADS
    span, sbr, nsb = _att_plan(T, dil)
    scale = HEAD_DIM ** -0.5

    def body(*refs):
        q_refs = refs[:rep]
        k_ref, v_ref, o_ref, l_ref, o_slab, l_slab = refs[rep:]
        sb = pl.program_id(1)

        def blk(i, _):
            loc, cur, prv, gb = _att_block(sb, i, sbr, span, dil)
            rows = _att_rows(loc, dil)
            qs = jnp.concatenate([r[rows, :] for r in q_refs], axis=0).astype(BF16)
            kcat = jnp.concatenate([k_ref[_att_rows(prv, dil), :], k_ref[_att_rows(cur, dil), :]], axis=0)
            vcat = jnp.concatenate([v_ref[_att_rows(prv, dil), :], v_ref[_att_rows(cur, dil), :]], axis=0)
            s = jnp.where(_att_masks(rep, gb), _dot(qs, kcat.astype(BF16), NT) * scale, NEG_INF)
            m = jnp.max(s, axis=-1, keepdims=True)
            p = jnp.exp(s - m)
            l = jnp.sum(p, axis=-1, keepdims=True)
            o = _dot(p.astype(BF16), vcat.astype(BF16)) / l
            lse = jnp.broadcast_to(m + jnp.log(l), (rep * ATT_BLK, HEAD_DIM))
            for j in range(rep):
                o_slab[j, rows, :] = o[j * ATT_BLK:(j + 1) * ATT_BLK]
                l_slab[j, rows, :] = lse[j * ATT_BLK:(j + 1) * ATT_BLK]
            return 0

        lax.fori_loop(0, sbr // ATT_BLK, blk, 0)
        for j in range(rep):
            o_ref[:, j * HEAD_DIM:(j + 1) * HEAD_DIM] = o_slab[j]
            l_ref[:, j * HEAD_DIM:(j + 1) * HEAD_DIM] = l_slab[j]

    qspecs = [pl.BlockSpec((sbr, HEAD_DIM), functools.partial(lambda h, s, j: (s, grp * H + h * rep + j), j=j))
              for j in range(rep)]
    kspec = pl.BlockSpec((T, HEAD_DIM), lambda h, s: (0, h))
    ospec = pl.BlockSpec((sbr, rep * HEAD_DIM), lambda h, s: (s, h))
    slab = pltpu.VMEM((rep, sbr, HEAD_DIM), F32)
    return pl.pallas_call(
        body, grid=(N_KV_HEADS, nsb), in_specs=qspecs + [kspec, kspec], out_specs=[ospec, ospec],
        out_shape=[jax.ShapeDtypeStruct((T, H * HEAD_DIM), F32)] * 2, scratch_shapes=[slab, slab],
        compiler_params=_params(("parallel", "arbitrary")), name=f"att_fwd_d{dil}",
    )(*([q] * rep), k, v)


def _att_combine(outs, lses):
    T, W = outs[0].shape
    ng = len(outs)
    tr, tcol = _tile(T, 512, 8), _tile(W, 512)

    def body(*refs):
        o_refs, l_refs = refs[:ng], refs[ng:2 * ng]
        ob_ref, lse_ref = refs[2 * ng:]
        ls = [r[...] for r in l_refs]
        m = functools.reduce(jnp.maximum, ls)
        es = [jnp.exp(l - m) for l in ls]
        den = functools.reduce(lambda a, b: a + b, es)
        num = functools.reduce(lambda a, b: a + b, [e * o[...] for e, o in zip(es, o_refs)])
        ob_ref[...] = (num / den).astype(BF16)
        lse_ref[...] = m + jnp.log(den)

    spec = pl.BlockSpec((tr, tcol), lambda i, j: (i, j))
    return pl.pallas_call(
        body, grid=(T // tr, W // tcol), in_specs=[spec] * (2 * ng), out_specs=[spec, spec],
        out_shape=[jax.ShapeDtypeStruct((T, W), BF16), jax.ShapeDtypeStruct((T, W), F32)],
        compiler_params=_params(("parallel", "parallel")), name="att_combine")(*outs, *lses)


STAT_LANE = HEAD_DIM // 2


def _att_stats(lse, o, do):
    T, W = lse.shape
    tr = _tile(T, 256, 16)

    def body(l_ref, o_ref, do_ref, s_ref):
        lane = lax.broadcasted_iota(jnp.int32, (tr, HEAD_DIM), 1)
        for h in range(W // HEAD_DIM):
            cols = slice(h * HEAD_DIM, (h + 1) * HEAD_DIM)
            delta = jnp.sum(do_ref[:, cols] * o_ref[:, cols].astype(F32), axis=-1, keepdims=True)
            s_ref[:, cols] = jnp.where(lane < STAT_LANE, l_ref[:, cols], delta)

    spec = pl.BlockSpec((tr, W), lambda i: (i, 0))
    return pl.pallas_call(
        body, grid=(T // tr,), in_specs=[spec] * 3, out_specs=spec,
        out_shape=jax.ShapeDtypeStruct((T, W), F32),
        compiler_params=_params(("parallel",)), name="att_stats")(lse, o, do)


def _att_bwd(q, k, v, do, stats, dq, grp, dil):
    T = q.shape[0]
    H = do.shape[1] // HEAD_DIM
    rep = H // N_KV_HEADS
    hs = min(rep, 2)
    span, sbr, nsb = _att_plan(T, dil)
    scale = HEAD_DIM ** -0.5

    def body(*refs):
        q_refs, do_refs, st_refs = refs[:hs], refs[hs + 2:2 * hs + 2], refs[2 * hs + 2:3 * hs + 2]
        k_ref, v_ref = refs[hs], refs[hs + 1]
        dq_ref, dk_ref, dv_ref, dq_slab = refs[3 * hs + 3:]
        sb = pl.program_id(2)

        @pl.when((pl.program_id(1) == 0) & (sb == 0))
        def _():
            dk_ref[...] = jnp.zeros_like(dk_ref)
            dv_ref[...] = jnp.zeros_like(dv_ref)

        def blk(i, _):
            loc, cur, prv, gb = _att_block(sb, i, sbr, span, dil)
            rows, kc, kp = _att_rows(loc, dil), _att_rows(cur, dil), _att_rows(prv, dil)
            qs = jnp.concatenate([r[rows, :] for r in q_refs], axis=0).astype(BF16)
            dos = jnp.concatenate([r[rows, :] for r in do_refs], axis=0).astype(BF16)
            st = jnp.concatenate([r[rows, :] for r in st_refs], axis=0)
            kcat = jnp.concatenate([k_ref[kp, :], k_ref[kc, :]], axis=0).astype(BF16)
            vcat = jnp.concatenate([v_ref[kp, :], v_ref[kc, :]], axis=0).astype(BF16)
            s = _dot(qs, kcat, NT) * scale
            p = jnp.where(_att_masks(hs, gb), jnp.exp(s - st[:, 0:1]), 0.0)
            dp = _dot(dos, vcat, NT)
            ds = (p * (dp - st[:, STAT_LANE:STAT_LANE + 1]) * scale).astype(BF16)
            dvc = _dot(p.astype(BF16), dos, TN)
            dkc = _dot(ds, qs, TN)
            dqs = _dot(ds, kcat)
            for j in range(hs):
                dq_slab[j, rows, :] = dqs[j * ATT_BLK:(j + 1) * ATT_BLK]
            dk_ref[kc, :] += dkc[ATT_BLK:]
            dv_ref[kc, :] += dvc[ATT_BLK:]

            @pl.when(gb > 0)
            def _():
                dk_ref[kp, :] += dkc[:ATT_BLK]
                dv_ref[kp, :] += dvc[:ATT_BLK]

            return 0

        lax.fori_loop(0, sbr // ATT_BLK, blk, 0)
        for j in range(hs):
            dq_ref[:, j * HEAD_DIM:(j + 1) * HEAD_DIM] = dq_slab[j]

    def head_specs(col0):
        return [pl.BlockSpec((sbr, HEAD_DIM),
                             functools.partial(lambda h, f, s, j: (s, col0 + h * rep + f * hs + j), j=j))
                for j in range(hs)]

    kspec = pl.BlockSpec((T, HEAD_DIM), lambda h, f, s: (0, h))
    dqspec = pl.BlockSpec((sbr, hs * HEAD_DIM), lambda h, f, s: (s, (grp * H + h * rep) // hs + f))
    n_in = 3 * hs + 3
    return pl.pallas_call(
        body, grid=(N_KV_HEADS, rep // hs, nsb),
        in_specs=head_specs(grp * H) + [kspec, kspec] + head_specs(0) + head_specs(0) + [ANY],
        out_specs=[dqspec, kspec, kspec],
        out_shape=[jax.ShapeDtypeStruct(dq.shape, F32),
                   jax.ShapeDtypeStruct((T, N_KV_HEADS * HEAD_DIM), F32),
                   jax.ShapeDtypeStruct((T, N_KV_HEADS * HEAD_DIM), F32)],
        scratch_shapes=[pltpu.VMEM((hs, sbr, HEAD_DIM), F32)],
        input_output_aliases={n_in - 1: 0},
        compiler_params=_params(("parallel", "arbitrary", "arbitrary")), name=f"att_bwd_d{dil}",
    )(*([q] * hs), k, v, *([do] * hs), *([stats] * hs), dq)


def _sum_kv(dks, dvs):
    T, W = dks[0].shape
    ng = len(dks)
    tr = _tile(T, 512, 8)

    def body(*refs):
        o_ref = refs[2 * ng]
        o_ref[0] = functools.reduce(lambda a, b: a + b, [r[...] for r in refs[:ng]]).astype(BF16)
        o_ref[1] = functools.reduce(lambda a, b: a + b, [r[...] for r in refs[ng:2 * ng]]).astype(BF16)

    spec = pl.BlockSpec((tr, W), lambda i: (i, 0))
    return pl.pallas_call(
        body, grid=(T // tr,), in_specs=[spec] * (2 * ng),
        out_specs=pl.BlockSpec((2, tr, W), lambda i: (0, i, 0)),
        out_shape=jax.ShapeDtypeStruct((2, T, W), BF16),
        compiler_params=_params(("parallel",)), name="sum_kv")(*dks, *dvs)


def _local_step(x, tgt, w, wts, ready, dep0=None):
    T, D = x.shape
    g = {}

    (u0,) = _rms_fwd("rms_a", x, [w["a_norm"]], [F32], dep=dep0)
    lbr, lbi, bbt_re, bbt_im = _s5_param_fwd(w["lam_re"], w["lam_im"], w["log_dt"], w["bt_re"], w["bt_im"])
    a_re, a_im = lbr.reshape(-1, 8, 128), lbi.reshape(-1, 8, 128)
    bbr = _block_diag(bbt_re.transpose(1, 0, 2), True).astype(BF16)
    bbi = _block_diag(bbt_im.transpose(1, 0, 2), True).astype(BF16)
    ccr = _block_diag(w["c_re"], False).astype(BF16)
    cci = _block_diag(w["c_im"], False).astype(BF16)
    z, h0 = _s5_fwd(u0, bbr, bbi, ccr, cci, a_re, a_im, w["s5_d"])
    w_glu = wts("glu", z)
    x1, vg = _mm_dual_fwd("glu_fwd", z, w_glu, x, "glu")

    def ffn_fwd(xin, layer):
        (nrm,) = _rms_fwd(f"rms_f{layer}", xin, [w["ffn_norm"][layer:layer + 1]], [BF16])
        w_in = wts(f"win{layer}", nrm)
        gu, act = _mm_dual_fwd(f"ffn_in{layer}", nrm, w_in, None, "ffn")
        w_out = wts(f"wout{layer}", act)
        xout = _mm_nn(f"ffn_out{layer}", act, w_out, res=xin)
        return xout, (nrm, gu, act, w_in, w_out)

    x2, saved0 = ffn_fwd(x1, 0)
    kvn, hb = _rms_fwd("rms_b", x2, [w["kv_norm"], w["b_norm"]], [BF16, BF16])
    w_kv, w_q, w_o = wts("wkv", hb), wts("wq", hb), wts("wo", hb)
    k, v = _mm_kv("kv_proj", kvn, w_kv)
    q = _mm_nn("q_proj", hb, w_q)
    outs, lses = [], []
    for grp, dil in enumerate(DILATIONS):
        o_g, l_g = _att_fwd(q, k, v, grp, dil)
        outs.append(o_g)
        lses.append(l_g)
    o, lse = _att_combine(outs, lses)
    x3 = _mm_nn("o_proj", o, w_o, res=x2)
    x4, saved1 = ffn_fwd(x3, 1)
    loss_blk, dx4, dx4b, g["final_norm"] = _loss_head(x4, w["final_norm"], tgt)

    def ffn_bwd(dx, dxb, xin, saved, layer, dep):
        nrm, gu, act, w_in, w_out = saved
        dgu = _mm_nt_ffn_bwd(f"ffn_dact{layer}", dxb, w_out, gu, dep=dep)
        g_wout = _mm_tn(f"ffn_dwout{layer}", act, dxb)
        g_win = _mm_tn_pair(f"ffn_dwin{layer}", nrm, dgu)
        dn = _mm_nt_pair(f"ffn_dn{layer}", dgu, w_in)
        dxo, dxob, (dgn,) = _rms_bwd(f"rms_f{layer}_bwd", xin, dx, [w["ffn_norm"][layer:layer + 1]], [dn])
        tok = ready({f"win{layer}": g_win, f"wout{layer}": g_wout})
        return dxo, dxob, dgn, tok

    dx3, dx3b, dfn1, tok = ffn_bwd(dx4, dx4b, x3, saved1, 1, None)
    do = _mm_nt("o_proj_dx", [dx3b], w_o, dep=tok)
    g_wo = _mm_tn("o_proj_dw", o, dx3b)
    stats = _att_stats(lse, o, do)
    dq = lax.empty(q.shape, F32)
    dks, dvs = [], []
    for grp, dil in enumerate(DILATIONS):
        dq, dk_g, dv_g = _att_bwd(q, k, v, do, stats, dq, grp, dil)
        dks.append(dk_g)
        dvs.append(dv_g)
    dkv = _sum_kv(dks, dvs)
    dhb = _mm_nt("q_proj_dx", [dq], w_q)
    g_wq = _mm_tn("q_proj_dw", hb, dq)
    dkvn = _mm_nt_pair("kv_proj_dx", dkv, w_kv)
    g_wkv = _mm_tn_pair("kv_proj_dw", kvn, dkv)
    dx2, dx2b, (g["kv_norm"], g["b_norm"]) = _rms_bwd(
        "rms_b_bwd", x2, dx3, [w["kv_norm"], w["b_norm"]], [dkvn, dhb])
    tok = ready({"wkv": g_wkv, "wq": g_wq, "wo": g_wo})
    dx1, dx1b, dfn0, tok = ffn_bwd(dx2, dx2b, x1, saved0, 0, tok)
    g["ffn_norm"] = jnp.concatenate([dfn0, dfn1], axis=0)

    dvg = _glu_bwd(dx1, vg, dep=tok)
    dz = _mm_nt_pair("glu_dx", dvg, w_glu)
    ready({"glu": _mm_tn_pair("glu_dw", z, dvg)})
    du, g["s5_d"], da_re, da_im, dbbr, dbbi, dccr, dcci = _s5_bwd(
        u0, dz, h0, bbr, bbi, ccr, cci, a_re, a_im, w["s5_d"])
    C, G, P = w["bt_re"].shape
    g["c_re"] = _block_diag_extract(dccr, False, C, P)
    g["c_im"] = -_block_diag_extract(dcci, False, C, P)
    g["lam_re"], g["lam_im"], g["log_dt"], g["bt_re"], g["bt_im"] = _s5_param_bwd(
        w["lam_re"], w["lam_im"], w["log_dt"], w["bt_re"], w["bt_im"],
        da_re.reshape(G, P), da_im.reshape(G, P),
        _block_diag_extract(dbbr, True, C, P).transpose(1, 0, 2),
        _block_diag_extract(dbbi, True, C, P).transpose(1, 0, 2))
    grad_x, _, (g["a_norm"],) = _rms_bwd("rms_a_bwd", x, dx1, [w["a_norm"]], [du])
    return loss_blk, grad_x, g


def _coords():
    return lax.axis_index("x"), lax.axis_index("y"), lax.axis_index("c")


def _dev_index(dev):
    return 4 * dev[0] + 2 * dev[1] + dev[2]


def _shard_window(ref, axis, width, idx):
    sl = [slice(None)] * len(ref.shape)
    sl[axis] = pl.ds(pl.multiple_of(idx * width, width), width)
    return ref.at[tuple(sl)]


def _all_gather(name, shards, axes):
    na = len(shards)
    widths = [s.shape[ax] for s, ax in zip(shards, axes)]
    out_shapes = []
    for s, ax in zip(shards, axes):
        shp = list(s.shape)
        shp[ax] *= N_DEV
        out_shapes.append(jax.ShapeDtypeStruct(tuple(shp), s.dtype))

    def body(*refs):
        ins, outs = refs[:na], refs[na:2 * na]
        send_sems, recv_sems, local_sems = refs[2 * na:]
        x, y, c = _coords()
        me, sib = (x, y, c), (x, y, 1 - c)
        chips = [(1 - x, y), (x, 1 - y), (1 - x, 1 - y)]

        def blk(a, dev):
            return _shard_window(outs[a], axes[a], widths[a], _dev_index(dev))

        def copy(a, kk, block, to, src=None):
            return pltpu.make_async_remote_copy(
                src_ref=blk(a, block) if src is None else src, dst_ref=blk(a, block),
                send_sem=send_sems.at[a, kk], recv_sem=recv_sems.at[a, kk],
                device_id=to, device_id_type=MESH)

        local = [pltpu.make_async_copy(ins[a], blk(a, me), local_sems.at[a]) for a in range(na)]
        for cp in local:
            cp.start()
        sent = []
        for a in range(na):
            first = [copy(a, 0, me, sib, src=ins[a])]
            first += [copy(a, 1 + j, me, (*chip, c), src=ins[a]) for j, chip in enumerate(chips)]
            for cp in first:
                cp.start()
            sent += first
        for a in range(na):
            for j, chip in enumerate(chips):
                copy(a, 1 + j, (*chip, c), me).wait_recv()
                fwd = copy(a, 4 + j, (*chip, c), sib)
                fwd.start()
                sent.append(fwd)
        for a in range(na):
            copy(a, 0, sib, me).wait_recv()
            for j, chip in enumerate(chips):
                copy(a, 4 + j, (*chip, 1 - c), me).wait_recv()
        for cp in sent:
            cp.wait_send()
        for cp in local:
            cp.wait()

    return pl.pallas_call(
        body, out_shape=out_shapes, in_specs=[ANY] * na, out_specs=[ANY] * na,
        scratch_shapes=[pltpu.SemaphoreType.DMA((na, 7)), pltpu.SemaphoreType.DMA((na, 7)),
                        pltpu.SemaphoreType.DMA((na,))],
        name=name)(*shards)


HBM = pl.BlockSpec(memory_space=pltpu.HBM)
SEM = pl.BlockSpec(memory_space=pltpu.SEMAPHORE)
TOKEN_SPEC = pl.BlockSpec(memory_space=pltpu.VMEM)
TOKEN_SHAPE = jax.ShapeDtypeStruct((8, 128), F32)
SPLIT_PARAMS = pltpu.CompilerParams(has_side_effects=pltpu.SideEffectType.DATAFLOW_SIDE_EFFECTING)


def _hbm(x):
    return pltpu.with_memory_space_constraint(x, pltpu.HBM)


def _hbm_like(x):
    return pltpu.HBM(x.shape, x.dtype)


def _dma_sems(*shape):
    return pltpu.SemaphoreType.DMA(shape)


def _cast_and_place(name, shard, axis, pos):
    rows, cols = shard.shape
    tr = _tile(rows, 256, 16)
    nt = rows // tr
    full = (rows, cols * N_DEV) if axis == 1 else (rows * N_DEV, cols)

    def dev(p):
        return 4 * p[0] + 2 * p[1] + p[2]

    def body(pos_ref, s_ref, b_ref, l_ref):
        v = s_ref[...].astype(BF16)
        b_ref[...] = v
        l_ref[...] = v

    blk = pl.BlockSpec((tr, cols), lambda i, p: (i, 0))
    if axis == 1:
        lspec = pl.BlockSpec((tr, cols), lambda i, p: (i, dev(p)))
    else:
        lspec = pl.BlockSpec((tr, cols), lambda i, p: (dev(p) * nt + i, 0))
    return pl.pallas_call(
        body, grid_spec=pltpu.PrefetchScalarGridSpec(
            num_scalar_prefetch=1, grid=(nt,), in_specs=[blk], out_specs=[blk, lspec]),
        out_shape=[jax.ShapeDtypeStruct((rows, cols), BF16), jax.ShapeDtypeStruct(full, BF16)],
        compiler_params=_params(("parallel",)), name=name)(pos, shard)


def _gather_start(shards, lands, axes, groups):
    na, ng = len(shards), len(groups)
    widths = [s.shape[ax] for s, ax in zip(shards, axes)]

    def body(*refs):
        sh, ld = refs[:na], refs[na:2 * na]
        sems = refs[2 * na:2 * na + 3 * ng]
        token = refs[-1]
        x, y, c = _coords()
        me, sib = (x, y, c), (x, y, 1 - c)
        chips = [(1 - x, y), (x, 1 - y), (1 - x, 1 - y)]
        for gi, grp in enumerate(groups):
            send, r_d2d, r_ici = sems[3 * gi:3 * gi + 3]
            for li, a in enumerate(grp):
                dst = _shard_window(ld[a], axes[a], widths[a], _dev_index(me))
                pltpu.make_async_remote_copy(
                    src_ref=sh[a], dst_ref=dst, send_sem=send.at[4 * li], recv_sem=r_d2d.at[li],
                    device_id=sib, device_id_type=MESH).start()
                for j, chip in enumerate(chips):
                    pltpu.make_async_remote_copy(
                        src_ref=sh[a], dst_ref=dst, send_sem=send.at[4 * li + 1 + j], recv_sem=r_ici.at[3 * li + j],
                        device_id=(*chip, c), device_id_type=MESH).start()
        token[...] = jnp.zeros_like(token)

    out_shape, out_specs = [], []
    for grp in groups:
        out_shape += [_dma_sems(4 * len(grp)), _dma_sems(len(grp)), _dma_sems(3 * len(grp))]
        out_specs += [SEM] * 3
    out_shape += [_hbm_like(s) for s in shards] + [_hbm_like(l) for l in lands] + [TOKEN_SHAPE]
    out_specs += [HBM] * (2 * na) + [TOKEN_SPEC]
    aliases = {a: 3 * ng + a for a in range(2 * na)}
    res = pl.pallas_call(
        body, name="weights_gather_start", out_shape=out_shape, in_specs=[HBM] * (2 * na),
        out_specs=out_specs, input_output_aliases=aliases, compiler_params=SPLIT_PARAMS,
    )(*[_hbm(s) for s in shards], *[_hbm(l) for l in lands])
    sems = [tuple(res[3 * gi:3 * gi + 3]) for gi in range(ng)]
    return sems, list(res[3 * ng:3 * ng + na]), list(res[3 * ng + na:3 * ng + 2 * na]), res[-1]


def _gather_forward(name, lands, axes, r_ici, after):
    n = len(lands)
    widths = [l.shape[ax] // N_DEV for l, ax in zip(lands, axes)]

    def body(*refs):
        ld, r_ici_ref = refs[:n], refs[n]
        f_send, f_recv = refs[n + 2], refs[n + 3]
        x, y, c = _coords()
        sib = (x, y, 1 - c)
        chips = [(1 - x, y), (x, 1 - y), (1 - x, 1 - y)]
        for li in range(n):
            for j, chip in enumerate(chips):
                blk = _shard_window(ld[li], axes[li], widths[li], _dev_index((*chip, c)))
                pltpu.make_async_remote_copy(
                    src_ref=blk, dst_ref=blk, send_sem=f_send.at[3 * li + j], recv_sem=r_ici_ref.at[3 * li + j],
                    device_id=(*chip, c), device_id_type=MESH).wait_recv()
                pltpu.make_async_remote_copy(
                    src_ref=blk, dst_ref=blk, send_sem=f_send.at[3 * li + j], recv_sem=f_recv.at[3 * li + j],
                    device_id=sib, device_id_type=MESH).start()

    res = pl.pallas_call(
        body, name=name, out_shape=[_dma_sems(3 * n), _dma_sems(3 * n)] + [_hbm_like(l) for l in lands],
        in_specs=[HBM] * n + [SEM, ANY], out_specs=[SEM, SEM] + [HBM] * n,
        input_output_aliases={li: 2 + li for li in range(n)}, compiler_params=SPLIT_PARAMS,
    )(*lands, r_ici, after)
    return res[0], res[1], list(res[2:])


def _gather_finish(name, shards, lands, axes, send, r_d2d, f_send, f_recv):
    n = len(lands)
    widths = [l.shape[ax] // N_DEV for l, ax in zip(lands, axes)]

    def body(*refs):
        sh, ld = refs[:n], refs[n:2 * n]
        send_ref, r_d2d_ref, f_send_ref, f_recv_ref = refs[2 * n:2 * n + 4]
        x, y, c = _coords()
        me, sib = (x, y, c), (x, y, 1 - c)
        chips = [(1 - x, y), (x, 1 - y), (1 - x, 1 - y)]

        def blk(li, dev):
            return _shard_window(ld[li], axes[li], widths[li], _dev_index(dev))

        for li in range(n):
            for kk in range(4):
                pltpu.make_async_remote_copy(
                    src_ref=sh[li], dst_ref=blk(li, me), send_sem=send_ref.at[4 * li + kk], recv_sem=r_d2d_ref.at[li],
                    device_id=sib, device_id_type=MESH).wait_send()
            pltpu.make_async_remote_copy(
                src_ref=blk(li, sib), dst_ref=blk(li, sib), send_sem=send_ref.at[4 * li], recv_sem=r_d2d_ref.at[li],
                device_id=sib, device_id_type=MESH).wait_recv()
            for j, chip in enumerate(chips):
                pltpu.make_async_remote_copy(
                    src_ref=blk(li, (*chip, c)), dst_ref=blk(li, (*chip, c)), send_sem=f_send_ref.at[3 * li + j],
                    recv_sem=f_recv_ref.at[3 * li + j], device_id=sib, device_id_type=MESH).wait_send()
                pltpu.make_async_remote_copy(
                    src_ref=blk(li, (*chip, 1 - c)), dst_ref=blk(li, (*chip, 1 - c)), send_sem=f_send_ref.at[3 * li + j],
                    recv_sem=f_recv_ref.at[3 * li + j], device_id=sib, device_id_type=MESH).wait_recv()

    res = pl.pallas_call(
        body, name=name, out_shape=[_hbm_like(s) for s in shards] + [_hbm_like(l) for l in lands],
        in_specs=[HBM] * (2 * n) + [SEM] * 4, out_specs=[HBM] * (2 * n),
        input_output_aliases={i: i for i in range(2 * n)}, compiler_params=SPLIT_PARAMS,
    )(*shards, *lands, send, r_d2d, f_send, f_recv)
    return list(res[n:])


def _chip_exchange_start(name, parts):
    n = len(parts)

    def body(*refs):
        src, ld = refs[:n], refs[n:2 * n]
        send, recv = refs[2 * n], refs[2 * n + 1]
        token = refs[-1]
        x, y, c = _coords()
        chips = [(1 - x, y), (x, 1 - y), (1 - x, 1 - y)]
        for li in range(n):
            for kk, chip in enumerate(chips):
                pltpu.make_async_remote_copy(
                    src_ref=src[li].at[kk], dst_ref=ld[li].at[kk], send_sem=send.at[3 * li + kk],
                    recv_sem=recv.at[3 * li + kk], device_id=(*chip, c), device_id_type=MESH).start()
        token[...] = jnp.zeros_like(token)

    lands = [lax.empty(p.shape, p.dtype) for p in parts]
    res = pl.pallas_call(
        body, name=name,
        out_shape=[_dma_sems(3 * n), _dma_sems(3 * n)] + [_hbm_like(p) for p in parts] * 2 + [TOKEN_SHAPE],
        in_specs=[HBM] * (2 * n), out_specs=[SEM, SEM] + [HBM] * (2 * n) + [TOKEN_SPEC],
        input_output_aliases={i: 2 + i for i in range(2 * n)}, compiler_params=SPLIT_PARAMS,
    )(*[_hbm(p) for p in parts], *[_hbm(l) for l in lands])
    return res[0], res[1], list(res[2:2 + n]), list(res[2 + n:2 + 2 * n]), res[-1]


def _chip_exchange_finish(name, started, after):
    counts = [len(st[2]) for st in started]
    total = sum(counts)
    ns = len(started)

    def body(*refs):
        src, ld = refs[:total], refs[total:2 * total]
        sems = refs[2 * total:2 * total + 2 * ns]
        x, y, c = _coords()
        chips = [(1 - x, y), (x, 1 - y), (1 - x, 1 - y)]
        off = 0
        for si, cnt in enumerate(counts):
            send, recv = sems[2 * si], sems[2 * si + 1]
            for li in range(cnt):
                for kk, chip in enumerate(chips):
                    cp = pltpu.make_async_remote_copy(
                        src_ref=src[off + li].at[kk], dst_ref=ld[off + li].at[kk], send_sem=send.at[3 * li + kk],
                        recv_sem=recv.at[3 * li + kk], device_id=(*chip, c), device_id_type=MESH)
                    cp.wait_send()
                    cp.wait_recv()
            off += cnt

    srcs = [p for st in started for p in st[2]]
    lands = [l for st in started for l in st[3]]
    sems = [s for st in started for s in st[:2]]
    res = pl.pallas_call(
        body, name=name, out_shape=[_hbm_like(p) for p in srcs + lands],
        in_specs=[HBM] * (2 * total) + [SEM] * (2 * ns) + [ANY], out_specs=[HBM] * (2 * total),
        input_output_aliases={i: i for i in range(2 * total)}, compiler_params=SPLIT_PARAMS,
    )(*srcs, *lands, *sems, after)
    out, off = [], total
    for cnt in counts:
        out.append(list(res[off:off + cnt]))
        off += cnt
    return out


def _pair_exchange(name, grads, axes):
    na = len(grads)
    widths = [gr.shape[ax] // N_DEV for gr, ax in zip(grads, axes)]
    out_shapes = []
    for gr, ax, wd in zip(grads, axes, widths):
        shp = list(gr.shape)
        shp[ax] = wd
        out_shapes.append(jax.ShapeDtypeStruct((4, *shp), gr.dtype))

    def body(*refs):
        ins, outs = refs[:na], refs[na:2 * na]
        send_sems, recv_sems = refs[2 * na:]
        x, y, c = _coords()
        sib = (x, y, 1 - c)
        chips = [(x, y), (1 - x, y), (x, 1 - y), (1 - x, 1 - y)]
        cps = []
        for a in range(na):
            for kk, chip in enumerate(chips):
                cp = pltpu.make_async_remote_copy(
                    src_ref=_shard_window(ins[a], axes[a], widths[a], _dev_index((*chip, 1 - c))),
                    dst_ref=outs[a].at[kk], send_sem=send_sems.at[a, kk], recv_sem=recv_sems.at[a, kk],
                    device_id=sib, device_id_type=MESH)
                cp.start()
                cps.append(cp)
        for cp in cps:
            cp.wait()

    return pl.pallas_call(
        body, out_shape=out_shapes, in_specs=[ANY] * na, out_specs=[ANY] * na,
        scratch_shapes=[pltpu.SemaphoreType.DMA((na, 4)), pltpu.SemaphoreType.DMA((na, 4))],
        name=name)(*grads)


def _pair_sum(name, grad, land, axis, pos):
    wd = grad.shape[axis] // N_DEV
    shard_shape = land.shape[1:]
    rows, cols = shard_shape
    tr = _tile(rows, 256, 16)
    nt = rows // tr

    def dev_of(kk, pos_ref):
        return 4 * (pos_ref[0] ^ (kk & 1)) + 2 * (pos_ref[1] ^ (kk >> 1)) + pos_ref[2]

    if axis == 1:
        gspec = pl.BlockSpec((tr, wd), lambda kk, t, p: (t, dev_of(kk, p)))
    else:
        gspec = pl.BlockSpec((tr, cols), lambda kk, t, p: (dev_of(kk, p) * nt + t, 0))
    lspec = pl.BlockSpec((None, tr, cols), lambda kk, t, p: (kk, t, 0))
    p0spec = pl.BlockSpec((tr, cols), lambda kk, t, p: (jnp.where(kk == 0, t, nt - 1), 0))
    pbspec = pl.BlockSpec((None, tr, cols), lambda kk, t, p: (jnp.maximum(kk - 1, 0), jnp.where(kk == 0, 0, t), 0))

    def body(pos_ref, g_ref, l_ref, p0_ref, pb_ref):
        kk = pl.program_id(0)
        s = g_ref[...].astype(F32) + l_ref[...].astype(F32)

        @pl.when(kk == 0)
        def _():
            p0_ref[...] = s

        @pl.when(kk > 0)
        def _():
            pb_ref[...] = s.astype(BF16)

    return pl.pallas_call(
        body,
        grid_spec=pltpu.PrefetchScalarGridSpec(
            num_scalar_prefetch=1, grid=(4, nt), in_specs=[gspec, lspec], out_specs=[p0spec, pbspec]),
        out_shape=[jax.ShapeDtypeStruct(shard_shape, F32), jax.ShapeDtypeStruct((3, *shard_shape), BF16)],
        compiler_params=_params(("arbitrary", "arbitrary")), name=name)(pos, grad, land)


def _adamw(name, parts, w, m, v):
    rows, cols = w.shape
    tr = _tile(rows, 256, 8)
    npart = len(parts)
    c1 = 1.0 - ADAM_B1 ** ADAM_STEP
    c2 = 1.0 - ADAM_B2 ** ADAM_STEP

    def body(*refs):
        p_refs = refs[:npart]
        w_ref, m_ref, v_ref, g_ref, d_ref, nm_ref, nv_ref = refs[npart:]
        g = None
        for r in p_refs:
            if len(r.shape) == 3:
                for i in range(r.shape[0]):
                    t = r[i].astype(F32)
                    g = t if g is None else g + t
            else:
                t = r[...].astype(F32)
                g = t if g is None else g + t
        nm = ADAM_B1 * m_ref[...] + (1.0 - ADAM_B1) * g
        nv = ADAM_B2 * v_ref[...] + (1.0 - ADAM_B2) * (g * g)
        g_ref[...] = g
        nm_ref[...] = nm
        nv_ref[...] = nv
        d_ref[...] = -ADAM_LR * ((nm / c1) / (jnp.sqrt(nv / c2) + ADAM_EPS) + ADAM_WD * w_ref[...])

    spec = pl.BlockSpec((tr, cols), lambda i: (i, 0))
    pspecs = [pl.BlockSpec((p.shape[0], tr, cols), lambda i: (0, i, 0)) if p.ndim == 3 else spec
              for p in parts]
    return pl.pallas_call(
        body, grid=(rows // tr,), in_specs=pspecs + [spec] * 3, out_specs=[spec] * 4,
        out_shape=[jax.ShapeDtypeStruct((rows, cols), F32)] * 4,
        compiler_params=_params(("parallel",)), name=name)(*parts, w, m, v)


SMALL_NAMES = ("s5_lam_re", "s5_lam_im", "s5_log_dt", "s5_b_re", "s5_b_im", "s5_c_re", "s5_c_im",
               "ffn_norm", "b_norm_mix", "kv_norm", "final_norm")
SMALL_PAD = 1024


def _pack(parts):
    flat = []
    for p in parts:
        f = p.reshape(-1)
        pad = (-f.shape[0]) % SMALL_PAD
        if pad:
            f = jnp.concatenate([f, jnp.zeros((pad,), f.dtype)])
        flat.append(f)
    return jnp.concatenate(flat).reshape(-1, 128)


def _unpack(packed, shapes):
    flat = packed.reshape(-1)
    out, off = [], 0
    for shp in shapes:
        size = math.prod(shp)
        out.append(flat[off:off + size].reshape(shp))
        off += size + (-size) % SMALL_PAD
    return out


def kernel(x, s5_lam_re, s5_lam_im, s5_log_dt, s5_b_re, s5_b_im, s5_c_re, s5_c_im, s5_d, s5_w_glu, a_norm_mix, ffn_norm, ffn_w_in, ffn_w_out, b_norm_mix, attn_w_q, attn_w_o, kv_norm, w_kv, final_norm, loss_target, m_s5_lam_re, m_s5_lam_im, m_s5_log_dt, m_s5_b_re, m_s5_b_im, m_s5_c_re, m_s5_c_im, m_s5_d, m_s5_w_glu, m_a_norm_mix, m_ffn_norm, m_ffn_w_in, m_ffn_w_out, m_b_norm_mix, m_attn_w_q, m_attn_w_o, m_kv_norm, m_w_kv, m_final_norm, v_s5_lam_re, v_s5_lam_im, v_s5_log_dt, v_s5_b_re, v_s5_b_im, v_s5_c_re, v_s5_c_im, v_s5_d, v_s5_w_glu, v_a_norm_mix, v_ffn_norm, v_ffn_w_in, v_ffn_w_out, v_b_norm_mix, v_attn_w_q, v_attn_w_o, v_kv_norm, v_w_kv, v_final_norm):
    args = dict(locals())
    T, D = x.shape[1], x.shape[2]
    n_layers = ffn_w_in.shape[0]
    xi_, yi_, ci_ = _coords()
    pos = jnp.stack([xi_, yi_, ci_]).astype(jnp.int32)
    me = 4 * xi_ + 2 * yi_ + ci_

    big_names = ["glu"] + [f"win{l}" for l in range(n_layers)] + [f"wout{l}" for l in range(n_layers)] \
        + ["wkv", "wq", "wo"]
    big_shards = [s5_w_glu[0]] + [ffn_w_in[l] for l in range(n_layers)] \
        + [ffn_w_out[l] for l in range(n_layers)] + [w_kv, attn_w_q[0], attn_w_o[0]]
    big_axes = [1] + [1] * n_layers + [0] * n_layers + [0, 1, 0]
    big_out_names = ["s5_w_glu"] + ["ffn_w_in"] * n_layers + ["ffn_w_out"] * n_layers \
        + ["w_kv", "attn_w_q", "attn_w_o"]
    index_of = {n: i for i, n in enumerate(big_names)}
    vec_shard = jnp.concatenate([s5_d, a_norm_mix], axis=0)
    (vecs,) = _all_gather("vectors_all_gather", [vec_shard], [1])

    gather_groups = [["glu"], ["win0"], ["wout0"], ["wkv", "wq", "wo"], ["win1", "wout1"]]
    group_idx = [[index_of[n] for n in grp] for grp in gather_groups]
    group_of = {n: gi for gi, grp in enumerate(gather_groups) for n in grp}
    placed = [_cast_and_place(f"cast_place_{n}", s, ax, pos) for n, s, ax in zip(big_names, big_shards, big_axes)]
    gather_sems, shards_thru, lands_thru, start_token = _gather_start(
        [p[0] for p in placed], [p[1] for p in placed], big_axes, group_idx)
    full = {}

    def wts(name, after):
        if name not in full:
            gi = group_of[name]
            idx = group_idx[gi]
            axes = [big_axes[a] for a in idx]
            send, r_d2d, r_ici = gather_sems[gi]
            f_send, f_recv, lands = _gather_forward(
                f"weights_gather_forward{gi}", [lands_thru[a] for a in idx], axes, r_ici, after)
            done = _gather_finish(f"weights_gather_finish{gi}", [shards_thru[a] for a in idx], lands, axes,
                                  send, r_d2d, f_send, f_recv)
            full.update(zip(gather_groups[gi], done))
        return full[name]

    exchanges = []

    def ready(grads):
        names = list(grads)
        axes = [big_axes[index_of[n]] for n in names]
        tag = "_".join(names)
        lands = _pair_exchange(f"rs_pair_exchange_{tag}", [grads[n] for n in names], axes)
        p0s, pbs = [], []
        for n, land, ax in zip(names, lands, axes):
            p0, pb = _pair_sum(f"rs_pair_sum_{n}", grads[n], land, ax, pos)
            p0s.append(p0)
            pbs.append(pb)
        started = _chip_exchange_start(f"rs_chip_exchange_start_{tag}", pbs)
        exchanges.append((names, p0s, started))
        return started[4]

    G, P, C = s5_b_re.shape[1:]
    w = dict(
        a_norm=vecs[1:2], s5_d=vecs[0:1],
        ffn_norm=ffn_norm, b_norm=b_norm_mix, kv_norm=kv_norm.reshape(1, D), final_norm=final_norm.reshape(1, D),
        lam_re=s5_lam_re[0], lam_im=s5_lam_im[0], log_dt=s5_log_dt.reshape(G, 1),
        bt_re=s5_b_re[0].transpose(2, 0, 1), bt_im=s5_b_im[0].transpose(2, 0, 1),
        c_re=s5_c_re[0], c_im=s5_c_im[0],
    )
    loss_blk, grad_x, g = _local_step(x[0], loss_target[0], w, wts, ready, dep0=start_token)
    loss = lax.psum(loss_blk[0, 0], ("x", "y", "c"))

    out = {}

    def put(name, res, shape):
        for kind, r in zip(("grad", "delta", "new_m", "new_v"), res):
            out[f"{kind}_{name}"] = r.reshape(shape)

    layered = {}

    def update(names, p0s, recvd):
        first = None
        for name, p0, rc in zip(names, p0s, recvd):
            oname, shard = big_out_names[index_of[name]], big_shards[index_of[name]]
            if oname in ("ffn_w_in", "ffn_w_out"):
                layer = int(name[-1])
                res = _adamw(f"adamw_{name}", [p0, rc], shard, args["m_" + oname][layer], args["v_" + oname][layer])
                layered.setdefault(oname, {})[layer] = res
            else:
                mm, vv = args["m_" + oname], args["v_" + oname]
                res = _adamw(f"adamw_{name}", [p0, rc], shard, mm.reshape(shard.shape), vv.reshape(shard.shape))
                put(oname, res, args[oname].shape)
            first = res[0] if first is None else first
        return first

    early, last = exchanges[:-1], exchanges[-1]
    landed = _chip_exchange_finish("rs_chip_exchange_finish_early", [e[2] for e in early], grad_x)
    marker = None
    for (names, p0s, _), recvd in zip(early, landed):
        marker = update(names, p0s, recvd)
    (recvd,) = _chip_exchange_finish("rs_chip_exchange_finish_last", [last[2]], marker)
    update(last[0], last[1], recvd)
    for oname, per_layer in layered.items():
        put(oname, [jnp.stack([per_layer[l][i] for l in range(n_layers)]) for i in range(4)], args[oname].shape)

    small_g = dict(
        s5_lam_re=g["lam_re"], s5_lam_im=g["lam_im"], s5_log_dt=g["log_dt"],
        s5_b_re=g["bt_re"].transpose(1, 2, 0), s5_b_im=g["bt_im"].transpose(1, 2, 0),
        s5_c_re=g["c_re"], s5_c_im=g["c_im"], ffn_norm=g["ffn_norm"], b_norm_mix=g["b_norm"],
        kv_norm=g["kv_norm"], final_norm=g["final_norm"])
    packed = _pack([small_g[n] for n in SMALL_NAMES] + [g["s5_d"], g["a_norm"]])
    rows = packed.shape[0]
    (all_parts,) = _all_gather("small_grads_all_gather", [packed], [0])
    all_parts = all_parts.reshape(N_DEV, rows, 128)
    n_rep_rows = _pack([small_g[n] for n in SMALL_NAMES]).shape[0]
    w_pack = _pack([args[n] for n in SMALL_NAMES])
    m_pack = _pack([args["m_" + n] for n in SMALL_NAMES])
    v_pack = _pack([args["v_" + n] for n in SMALL_NAMES])
    res = _adamw("adamw_small", [all_parts[:, :n_rep_rows]], w_pack, m_pack, v_pack)
    shapes = [args[n].shape for n in SMALL_NAMES]
    unpacked = [_unpack(r, shapes) for r in res]
    for i, n in enumerate(SMALL_NAMES):
        put(n, [u[i] for u in unpacked], args[n].shape)
    ws = D // N_DEV
    tail = all_parts[:, n_rep_rows:].reshape(N_DEV, 2, D)
    tail = lax.dynamic_slice_in_dim(tail, me * ws, ws, axis=2)
    res = _adamw("adamw_vec", [tail], vec_shard,
                 jnp.concatenate([m_s5_d, m_a_norm_mix], axis=0), jnp.concatenate([v_s5_d, v_a_norm_mix], axis=0))
    put("s5_d", [r[0:1] for r in res], s5_d.shape)
    put("a_norm_mix", [r[1:2] for r in res], a_norm_mix.shape)

    names = ("s5_lam_re", "s5_lam_im", "s5_log_dt", "s5_b_re", "s5_b_im", "s5_c_re", "s5_c_im", "s5_d",
             "s5_w_glu", "a_norm_mix", "ffn_norm", "ffn_w_in", "ffn_w_out", "b_norm_mix", "attn_w_q",
             "attn_w_o", "kv_norm", "w_kv", "final_norm")
    result = [loss, grad_x.reshape(x.shape)]
    for kind in ("grad", "delta", "new_m", "new_v"):
        result += [out[f"{kind}_{n}"] for n in names]
    return tuple(result)
```

```python
import functools
import math

import jax
import jax.numpy as jnp
from jax import lax
from jax.experimental import pallas as pl
from jax.experimental.pallas import tpu as pltpu

F32 = jnp.float32
BF16 = jnp.bfloat16

EPS = 1e-6
NEG_INF = -1e30
HEAD_DIM = 128
N_KV_HEADS = 4
DILATIONS = (1, 4, 16)
ATT_BLK = 128
S5_C = 16
S5_P = 64
S5_GB = 16
S5_CH = S5_GB * S5_C
S5_W = S5_GB * S5_P
S5_UNROLL = 4
N_DEV = 8

ADAM_LR = 0.001
ADAM_B1 = 0.9
ADAM_B2 = 0.999
ADAM_EPS = 1e-08
ADAM_WD = 0.01
ADAM_STEP = 10

VMEM_LIMIT_BYTES = 56 * 1024 * 1024
MM_TILE = 1024
MM_TILE_NARROW = 512
MM_DEPTH = 2816
MESH = pl.DeviceIdType.MESH
ANY = pl.BlockSpec(memory_space=pl.ANY)


def _tile(n, pref, align=128):
    t = (min(pref, n) // align) * align
    while t >= align:
        if n % t == 0:
            return t
        t -= align
    return n


def _params(sem):
    return pltpu.CompilerParams(dimension_semantics=sem, vmem_limit_bytes=VMEM_LIMIT_BYTES)


def _sigmoid(x):
    return 1.0 / (1.0 + jnp.exp(-x))


NN = (((1,), (0,)), ((), ()))
NT = (((1,), (1,)), ((), ()))
TN = (((0,), (0,)), ((), ()))


def _dot(a, b, dims=NN):
    return lax.dot_general(a, b, dims, preferred_element_type=F32)


def _matmul(name, grid, ins, in_specs, products, dims, out_shapes, out_specs, acc_shapes, epilogue):
    n_in, n_out, nk = len(ins), len(out_shapes), grid[2]

    def body(*refs):
        in_refs = refs[:n_in]
        out_refs = refs[n_in:n_in + n_out]
        acc_refs = refs[n_in + n_out:]

        def prods():
            vals = [None] * len(acc_shapes)
            for ai, bi, ci in products:
                d = _dot(in_refs[ai][...].astype(BF16), in_refs[bi][...].astype(BF16), dims)
                vals[ci] = d if vals[ci] is None else vals[ci] + d
            return vals

        if nk == 1:
            epilogue(in_refs, out_refs, prods())
        else:
            k = pl.program_id(2)

            @pl.when(k == 0)
            def _():
                for a in acc_refs:
                    a[...] = jnp.zeros_like(a)

            for a, v in zip(acc_refs, prods()):
                a[...] += v

            @pl.when(k == nk - 1)
            def _():
                epilogue(in_refs, out_refs, [a[...] for a in acc_refs])

    scratch = [] if nk == 1 else [pltpu.VMEM(s, F32) for s in acc_shapes]
    return pl.pallas_call(
        body, grid=grid, in_specs=in_specs, out_specs=out_specs, out_shape=out_shapes,
        scratch_shapes=scratch, compiler_params=_params(("parallel", "parallel", "arbitrary")),
        name=name)(*ins)


def _mm_dual_fwd(name, a, w, res, kind):
    T, K = a.shape
    N = w.shape[1] // 2
    tm, tn = _tile(T, MM_TILE), _tile(N, MM_TILE_NARROW)
    nj = N // tn
    grid = (T // tm, nj, 1)
    ins = [a, w, w]
    in_specs = [pl.BlockSpec((tm, K), lambda i, j, k: (i, 0)),
                pl.BlockSpec((K, tn), lambda i, j, k: (0, j)),
                pl.BlockSpec((K, tn), lambda i, j, k: (0, j + nj))]
    pair_spec = pl.BlockSpec((2, tm, tn), lambda i, j, k: (0, i, j))
    tile_spec = pl.BlockSpec((tm, tn), lambda i, j, k: (i, j))
    if kind == "glu":
        ins.append(res)
        in_specs.append(tile_spec)

        def epilogue(in_refs, out_refs, accs):
            val, gate = accs
            out_refs[0][...] = in_refs[3][...] + val * _sigmoid(gate)
            out_refs[1][0] = val.astype(BF16)
            out_refs[1][1] = gate.astype(BF16)

        out_shapes = [jax.ShapeDtypeStruct((T, N), F32), jax.ShapeDtypeStruct((2, T, N), BF16)]
        out_specs = [tile_spec, pair_spec]
    else:
        def epilogue(in_refs, out_refs, accs):
            g, u = accs
            out_refs[0][0] = g.astype(BF16)
            out_refs[0][1] = u.astype(BF16)
            out_refs[1][...] = (g * _sigmoid(g) * u).astype(BF16)

        out_shapes = [jax.ShapeDtypeStruct((2, T, N), BF16), jax.ShapeDtypeStruct((T, N), BF16)]
        out_specs = [pair_spec, tile_spec]
    return _matmul(name, grid, ins, in_specs, [(0, 1, 0), (0, 2, 1)], NN, out_shapes, out_specs,
                   [(tm, tn), (tm, tn)], epilogue)


def _mm_kv(name, a, w):
    T, K = a.shape
    N = w.shape[1] // 2
    tm, tn = _tile(T, MM_TILE), _tile(N, MM_TILE_NARROW)
    nj = N // tn
    tile_spec = pl.BlockSpec((tm, tn), lambda i, j, k: (i, j))

    def epilogue(in_refs, out_refs, accs):
        out_refs[0][...] = accs[0]
        out_refs[1][...] = accs[1]

    return _matmul(name, (T // tm, nj, 1), [a, w, w],
                   [pl.BlockSpec((tm, K), lambda i, j, k: (i, 0)),
                    pl.BlockSpec((K, tn), lambda i, j, k: (0, j)),
                    pl.BlockSpec((K, tn), lambda i, j, k: (0, j + nj))],
                   [(0, 1, 0), (0, 2, 1)], NN,
                   [jax.ShapeDtypeStruct((T, N), F32)] * 2, [tile_spec, tile_spec],
                   [(tm, tn), (tm, tn)], epilogue)


def _mm_nn(name, a, w, res=None, out_dtype=F32):
    T, K = a.shape
    N = w.shape[1]
    tk = _tile(K, MM_DEPTH)
    tm, tn = _tile(T, MM_TILE), _tile(N, MM_TILE if tk == K else MM_TILE_NARROW)
    grid = (T // tm, N // tn, K // tk)
    tile_spec = pl.BlockSpec((tm, tn), lambda i, j, k: (i, j))
    ins = [a, w]
    in_specs = [pl.BlockSpec((tm, tk), lambda i, j, k: (i, k)),
                pl.BlockSpec((tk, tn), lambda i, j, k: (k, j))]
    if res is not None:
        ins.append(res)
        in_specs.append(tile_spec)

    def epilogue(in_refs, out_refs, accs):
        v = accs[0]
        if res is not None:
            v = v + in_refs[2][...]
        out_refs[0][...] = v.astype(out_dtype)

    return _matmul(name, grid, ins, in_specs, [(0, 1, 0)], NN,
                   [jax.ShapeDtypeStruct((T, N), out_dtype)], [tile_spec], [(tm, tn)], epilogue)[0]


def _dep_operand(ins, in_specs, dep):
    if dep is not None:
        ins.append(dep)
        in_specs.append(pl.BlockSpec((8, 128), lambda *_: (0, 0)))


def _mm_nt(name, a_list, w, out_dtype=F32, dep=None):
    T, Np = a_list[0].shape
    Ko = w.shape[0]
    n_parts = len(a_list)
    wide_a = a_list[0].dtype != BF16
    tm, tn, tk = _tile(T, MM_TILE_NARROW if wide_a else MM_TILE), _tile(Ko, MM_TILE), _tile(Np, MM_DEPTH)
    nkp = Np // tk
    grid = (T // tm, Ko // tn, nkp)
    ins = list(a_list) + [w] * n_parts
    in_specs = [pl.BlockSpec((tm, tk), lambda i, j, k: (i, k)) for _ in a_list]
    in_specs += [pl.BlockSpec((tn, tk), functools.partial(lambda i, j, k, p: (j, p * nkp + k), p=p))
                 for p in range(n_parts)]
    products = [(p, n_parts + p, 0) for p in range(n_parts)]
    _dep_operand(ins, in_specs, dep)

    def epilogue(in_refs, out_refs, accs):
        out_refs[0][...] = accs[0].astype(out_dtype)

    return _matmul(name, grid, ins, in_specs, products, NT,
                   [jax.ShapeDtypeStruct((T, Ko), out_dtype)],
                   [pl.BlockSpec((tm, tn), lambda i, j, k: (i, j))], [(tm, tn)], epilogue)[0]


def _mm_nt_pair(name, a3, w):
    _, T, N = a3.shape
    Ko = w.shape[0]
    tm, tn, tk = _tile(T, MM_TILE), _tile(Ko, MM_TILE), _tile(N, MM_DEPTH)
    nkh = N // tk
    grid = (T // tm, Ko // tn, 2 * nkh)

    def epilogue(in_refs, out_refs, accs):
        out_refs[0][...] = accs[0]

    return _matmul(name, grid, [a3, w],
                   [pl.BlockSpec((None, tm, tk), lambda i, j, k: (k // nkh, i, k % nkh)),
                    pl.BlockSpec((tn, tk), lambda i, j, k: (j, k))],
                   [(0, 1, 0)], NT, [jax.ShapeDtypeStruct((T, Ko), F32)],
                   [pl.BlockSpec((tm, tn), lambda i, j, k: (i, j))], [(tm, tn)], epilogue)[0]


def _mm_nt_ffn_bwd(name, dx, w_out, gu, dep=None):
    T, D = dx.shape
    Fh = w_out.shape[0]
    tm, tn = _tile(T, MM_TILE), _tile(Fh, MM_TILE_NARROW)
    pair_spec = pl.BlockSpec((2, tm, tn), lambda i, j, k: (0, i, j))

    def epilogue(in_refs, out_refs, accs):
        da = accs[0]
        g = in_refs[2][0].astype(F32)
        u = in_refs[2][1].astype(F32)
        s = _sigmoid(g)
        out_refs[0][0] = (da * u * (s * (1.0 + g * (1.0 - s)))).astype(BF16)
        out_refs[0][1] = (da * (g * s)).astype(BF16)

    ins = [dx, w_out, gu]
    in_specs = [pl.BlockSpec((tm, D), lambda i, j, k: (i, 0)),
                pl.BlockSpec((tn, D), lambda i, j, k: (j, 0)),
                pair_spec]
    _dep_operand(ins, in_specs, dep)
    return _matmul(name, (T // tm, Fh // tn, 1), ins, in_specs,
                   [(0, 1, 0)], NT, [jax.ShapeDtypeStruct((2, T, Fh), BF16)], [pair_spec],
                   [(tm, tn)], epilogue)[0]


def _mm_tn(name, a, d):
    T, Ko = a.shape
    N = d.shape[1]
    to, tn, tk = _tile(Ko, MM_TILE_NARROW), _tile(N, MM_TILE if d.dtype == BF16 else MM_TILE_NARROW), T
    grid = (Ko // to, N // tn, T // tk)

    def epilogue(in_refs, out_refs, accs):
        out_refs[0][...] = accs[0].astype(BF16)

    return _matmul(name, grid, [a, d],
                   [pl.BlockSpec((tk, to), lambda i, j, k: (k, i)),
                    pl.BlockSpec((tk, tn), lambda i, j, k: (k, j))],
                   [(0, 1, 0)], TN, [jax.ShapeDtypeStruct((Ko, N), BF16)],
                   [pl.BlockSpec((to, tn), lambda i, j, k: (i, j))], [(to, tn)], epilogue)[0]


def _mm_tn_pair(name, a, d3):
    T, Ko = a.shape
    N = d3.shape[2]
    to, tn, tk = _tile(Ko, MM_TILE), _tile(N, MM_TILE_NARROW), T
    njh = N // tn
    grid = (Ko // to, 2 * njh, T // tk)

    def epilogue(in_refs, out_refs, accs):
        out_refs[0][...] = accs[0].astype(BF16)

    return _matmul(name, grid, [a, d3],
                   [pl.BlockSpec((tk, to), lambda i, j, k: (k, i)),
                    pl.BlockSpec((None, tk, tn), lambda i, j, k: (j // njh, k, j % njh))],
                   [(0, 1, 0)], TN, [jax.ShapeDtypeStruct((Ko, 2 * N), BF16)],
                   [pl.BlockSpec((to, tn), lambda i, j, k: (i, j))], [(to, tn)], epilogue)[0]


def _rms_fwd(name, x, gains, dtypes, dep=None):
    T, D = x.shape
    n = len(gains)
    tr = _tile(T, 512, 8)
    n_dep = 0 if dep is None else 1

    def body(x_ref, *refs):
        xv = x_ref[...]
        xr = xv * lax.rsqrt(jnp.mean(xv * xv, axis=-1, keepdims=True) + EPS)
        for g_ref, o_ref in zip(refs[:n], refs[n + n_dep:]):
            o_ref[...] = (xr * g_ref[...]).astype(o_ref.dtype)

    row = pl.BlockSpec((tr, D), lambda i: (i, 0))
    vec = pl.BlockSpec((1, D), lambda i: (0, 0))
    ins, in_specs = [x, *gains], [row] + [vec] * n
    _dep_operand(ins, in_specs, dep)
    return pl.pallas_call(
        body, grid=(T // tr,), in_specs=in_specs, out_specs=[row] * n,
        out_shape=[jax.ShapeDtypeStruct((T, D), dt) for dt in dtypes],
        compiler_params=_params(("parallel",)), name=name)(*ins)


def _rms_bwd(name, x, dres, gains, dhs):
    T, D = x.shape
    n = len(gains)
    tr = _tile(T, 256, 8)

    def body(x_ref, dres_ref, *refs):
        g_refs, dh_refs = refs[:n], refs[n:2 * n]
        dx_ref, dxb_ref = refs[2 * n], refs[2 * n + 1]
        dg_refs = refs[2 * n + 2:]
        xv = x_ref[...]
        r = lax.rsqrt(jnp.mean(xv * xv, axis=-1, keepdims=True) + EPS)
        xr = xv * r
        w = None
        for g_ref, dh_ref, dg_ref in zip(g_refs, dh_refs, dg_refs):
            dh = dh_ref[...].astype(F32)

            @pl.when(pl.program_id(0) == 0)
            def _():
                dg_ref[...] = jnp.zeros_like(dg_ref)

            dg_ref[...] += jnp.sum(dh * xr, axis=0, keepdims=True)
            wi = dh * g_ref[...]
            w = wi if w is None else w + wi
        dx = dres_ref[...] + r * (w - xr * jnp.mean(w * xr, axis=-1, keepdims=True))
        dx_ref[...] = dx
        dxb_ref[...] = dx.astype(BF16)

    row = pl.BlockSpec((tr, D), lambda i: (i, 0))
    vec = pl.BlockSpec((1, D), lambda i: (0, 0))
    outs = pl.pallas_call(
        body, grid=(T // tr,), in_specs=[row, row] + [vec] * n + [row] * n,
        out_specs=[row, row] + [vec] * n,
        out_shape=[jax.ShapeDtypeStruct((T, D), F32), jax.ShapeDtypeStruct((T, D), BF16)]
        + [jax.ShapeDtypeStruct((1, D), F32)] * n,
        compiler_params=_params(("arbitrary",)), name=name)(x, dres, *gains, *dhs)
    return outs[0], outs[1], outs[2:]


def _loss_head(x, gain, target):
    T, D = x.shape
    tr = _tile(T, 256, 8)

    def body(x_ref, g_ref, t_ref, loss_ref, dx_ref, dxb_ref, dg_ref):
        @pl.when(pl.program_id(0) == 0)
        def _():
            loss_ref[...] = jnp.zeros_like(loss_ref)
            dg_ref[...] = jnp.zeros_like(dg_ref)

        xv = x_ref[...]
        r = lax.rsqrt(jnp.mean(xv * xv, axis=-1, keepdims=True) + EPS)
        xr = xv * r
        err = xr * g_ref[...] - t_ref[...]
        part = jnp.sum(jnp.sum(err * err, axis=-1, keepdims=True), axis=0, keepdims=True) * (0.5 / D)
        loss_ref[...] += jnp.broadcast_to(part, loss_ref.shape)
        dy = err * (1.0 / D)
        dg_ref[...] += jnp.sum(dy * xr, axis=0, keepdims=True)
        w = dy * g_ref[...]
        dx = r * (w - xr * jnp.mean(w * xr, axis=-1, keepdims=True))
        dx_ref[...] = dx
        dxb_ref[...] = dx.astype(BF16)

    row = pl.BlockSpec((tr, D), lambda i: (i, 0))
    vec = pl.BlockSpec((1, D), lambda i: (0, 0))
    return pl.pallas_call(
        body, grid=(T // tr,), in_specs=[row, vec, row],
        out_specs=[pl.BlockSpec((8, 128), lambda i: (0, 0)), row, row, vec],
        out_shape=[jax.ShapeDtypeStruct((8, 128), F32), jax.ShapeDtypeStruct((T, D), F32),
                   jax.ShapeDtypeStruct((T, D), BF16), jax.ShapeDtypeStruct((1, D), F32)],
        compiler_params=_params(("arbitrary",)), name="loss_head")(x, gain, target)


def _glu_bwd(dmix, vg, dep=None):
    T, N = dmix.shape
    tr, tc = _tile(T, 512, 8), _tile(N, 1024)
    n_dep = 0 if dep is None else 1

    def body(d_ref, vg_ref, *refs):
        o_ref = refs[n_dep]
        d = d_ref[...]
        val = vg_ref[0].astype(F32)
        s = _sigmoid(vg_ref[1].astype(F32))
        o_ref[0] = (d * s).astype(BF16)
        o_ref[1] = (d * val * s * (1.0 - s)).astype(BF16)

    pair = pl.BlockSpec((2, tr, tc), lambda i, j: (0, i, j))
    ins, in_specs = [dmix, vg], [pl.BlockSpec((tr, tc), lambda i, j: (i, j)), pair]
    _dep_operand(ins, in_specs, dep)
    return pl.pallas_call(
        body, grid=(T // tr, N // tc), in_specs=in_specs,
        out_specs=pair, out_shape=jax.ShapeDtypeStruct((2, T, N), BF16),
        compiler_params=_params(("parallel", "parallel")), name="glu_bwd")(*ins)


def _to_state_tiles(x_ref, s, val):
    tc = val.shape[0]
    for j in range(S5_W // 128):
        x_ref[s, pl.ds(j, tc, stride=8), :] = val[:, 128 * j:128 * (j + 1)]


def _from_state_tiles(x_ref, s, tc):
    return jnp.concatenate([x_ref[s, pl.ds(j, tc, stride=8), :] for j in range(S5_W // 128)], axis=1)


def _s5_scan_fwd(xr_ref, xi_ref, ar_ref, ai_ref, cr_ref, ci_ref, tc, nblk):
    a = [(ar_ref[s], ai_ref[s]) for s in range(nblk)]

    def step(i, carry):
        carry = list(carry)
        for uu in range(S5_UNROLL):
            r0 = pl.multiple_of((i * S5_UNROLL + uu) * 8, 8)
            for s in range(nblk):
                cr, ci = carry[2 * s], carry[2 * s + 1]
                a_r, a_i = a[s]
                xr = a_r * cr - a_i * ci + xr_ref[s, pl.ds(r0, 8), :]
                xi = a_r * ci + a_i * cr + xi_ref[s, pl.ds(r0, 8), :]
                xr_ref[s, pl.ds(r0, 8), :] = xr
                xi_ref[s, pl.ds(r0, 8), :] = xi
                carry[2 * s], carry[2 * s + 1] = xr, xi
        return tuple(carry)

    init = []
    for s in range(nblk):
        init += [cr_ref[s], ci_ref[s]]
    out = lax.fori_loop(0, tc // S5_UNROLL, step, tuple(init))
    for s in range(nblk):
        cr_ref[s] = out[2 * s]
        ci_ref[s] = out[2 * s + 1]


def _s5_scan_bwd(lr_ref, li_ref, xr_ref, xi_ref, h_ref, ar_ref, ai_ref, cr_ref, ci_ref,
                 accr_ref, acci_ref, tc, nblk):
    a = [(ar_ref[s], ai_ref[s]) for s in range(nblk)]

    def one(s, r0, prev_r, prev_i, st):
        c_r, c_i, d_r, d_i = st
        a_r, a_i = a[s]
        l_r = lr_ref[s, pl.ds(r0, 8), :] + a_r * c_r + a_i * c_i
        l_i = li_ref[s, pl.ds(r0, 8), :] + a_r * c_i - a_i * c_r
        lr_ref[s, pl.ds(r0, 8), :] = l_r
        li_ref[s, pl.ds(r0, 8), :] = l_i
        return [l_r, l_i, d_r + l_r * prev_r + l_i * prev_i, d_i - l_r * prev_i + l_i * prev_r]

    def step(i, carry):
        carry = list(carry)
        for uu in range(S5_UNROLL):
            t = tc - 1 - (i * S5_UNROLL + uu)
            r0 = pl.multiple_of(t * 8, 8)
            p0 = pl.multiple_of((t - 1) * 8, 8)
            for s in range(nblk):
                carry[4 * s:4 * s + 4] = one(s, r0, xr_ref[s, pl.ds(p0, 8), :], xi_ref[s, pl.ds(p0, 8), :],
                                             carry[4 * s:4 * s + 4])
        return tuple(carry)

    init = []
    for s in range(nblk):
        init += [cr_ref[s], ci_ref[s], accr_ref[s], acci_ref[s]]
    carry = list(lax.fori_loop(0, tc // S5_UNROLL - 1, step, tuple(init)))
    for t in range(S5_UNROLL - 1, -1, -1):
        for s in range(nblk):
            if t > 0:
                prev_r, prev_i = xr_ref[s, 8 * (t - 1):8 * t, :], xi_ref[s, 8 * (t - 1):8 * t, :]
            else:
                prev_r, prev_i = h_ref[0, s], h_ref[1, s]
            carry[4 * s:4 * s + 4] = one(s, 8 * t, prev_r, prev_i, carry[4 * s:4 * s + 4])
    for s in range(nblk):
        cr_ref[s], ci_ref[s], accr_ref[s], acci_ref[s] = carry[4 * s:4 * s + 4]


def _gelu(y):
    k = math.sqrt(2.0 / math.pi)
    return 0.5 * y * (1.0 + jnp.tanh(k * (y + 0.044715 * (y * y * y))))


def _gelu_grad(y):
    k = math.sqrt(2.0 / math.pi)
    t = jnp.tanh(k * (y + 0.044715 * (y * y * y)))
    return 0.5 * (1.0 + t) + 0.5 * y * (1.0 - t * t) * (k * (1.0 + 3.0 * 0.044715 * (y * y)))


def _s5_specs(tc, nch, sbk, rev):
    def ch(c):
        return nch - 1 - c if rev else c

    return dict(
        act=pl.BlockSpec((tc, sbk * S5_CH), lambda i, c: (ch(c), i)),
        bb=pl.BlockSpec((sbk, S5_CH, S5_W), lambda i, c: (i, 0, 0)),
        cc=pl.BlockSpec((sbk, S5_W, S5_CH), lambda i, c: (i, 0, 0)),
        a=pl.BlockSpec((sbk, 8, 128), lambda i, c: (i, 0, 0)),
        d=pl.BlockSpec((1, sbk * S5_CH), lambda i, c: (0, i)),
        h=pl.BlockSpec((None, 2, sbk, 8, 128), lambda i, c: (ch(c), 0, i, 0, 0)),
    )


def _s5_blocks(nb, pref):
    return max(b for b in range(1, pref + 1) if nb % b == 0)


def _s5_fwd(u, bbr, bbi, ccr, cci, a_re, a_im, d_skip):
    T, D = u.shape
    nb = D // S5_CH
    sbk = _s5_blocks(nb, 4)
    tc = _tile(T, 512, 8)
    nch = T // tc
    sp = _s5_specs(tc, nch, sbk, False)

    def body(u_ref, bbr_ref, bbi_ref, ccr_ref, cci_ref, ar_ref, ai_ref, d_ref, z_ref, h_ref,
             xr, xi, cr, ci):
        @pl.when(pl.program_id(1) == 0)
        def _():
            cr[...] = jnp.zeros_like(cr)
            ci[...] = jnp.zeros_like(ci)

        h_ref[0] = cr[...]
        h_ref[1] = ci[...]
        for s in range(sbk):
            ub = u_ref[:, s * S5_CH:(s + 1) * S5_CH].astype(BF16)
            _to_state_tiles(xr, s, _dot(ub, bbr_ref[s]))
            _to_state_tiles(xi, s, _dot(ub, bbi_ref[s]))
        _s5_scan_fwd(xr, xi, ar_ref, ai_ref, cr, ci, tc, sbk)
        for s in range(sbk):
            cols = slice(s * S5_CH, (s + 1) * S5_CH)
            y = (_dot(_from_state_tiles(xr, s, tc).astype(BF16), ccr_ref[s])
                 - _dot(_from_state_tiles(xi, s, tc).astype(BF16), cci_ref[s])
                 + d_ref[:, cols] * u_ref[:, cols])
            z_ref[:, cols] = _gelu(y).astype(BF16)

    tiles = pltpu.VMEM((sbk, tc * 8, 128), F32)
    carry = pltpu.VMEM((sbk, 8, 128), F32)
    return pl.pallas_call(
        body, grid=(nb // sbk, nch),
        in_specs=[sp["act"], sp["bb"], sp["bb"], sp["cc"], sp["cc"], sp["a"], sp["a"], sp["d"]],
        out_specs=[sp["act"], sp["h"]],
        out_shape=[jax.ShapeDtypeStruct((T, D), BF16), jax.ShapeDtypeStruct((nch, 2, nb, 8, 128), F32)],
        scratch_shapes=[tiles, tiles, carry, carry],
        compiler_params=_params(("parallel", "arbitrary")), name="s5_fwd",
    )(u, bbr, bbi, ccr, cci, a_re, a_im, d_skip)


def _s5_bwd(u, dz, h0, bbr, bbi, ccr, cci, a_re, a_im, d_skip, dep=None):
    T, D = u.shape
    nb = D // S5_CH
    sbk = _s5_blocks(nb, 2)
    tc = _tile(T, 512, 8)
    nch = T // tc
    sp = _s5_specs(tc, nch, sbk, True)

    n_dep = 0 if dep is None else 1

    def body(u_ref, dz_ref, h_ref, bbr_ref, bbi_ref, ccr_ref, cci_ref, ar_ref, ai_ref, d_ref, *rest):
        (du_ref, dd_ref, dar_ref, dai_ref, dbbr_ref, dbbi_ref, dccr_ref, dcci_ref,
         xr, xi, lr, li, fr, fi, br, bi, accr, acci) = rest[n_dep:]
        c = pl.program_id(1)

        @pl.when(c == 0)
        def _():
            for ref in (br, bi, accr, acci, dd_ref, dbbr_ref, dbbi_ref, dccr_ref, dcci_ref):
                ref[...] = jnp.zeros_like(ref)

        for s in range(sbk):
            ub = u_ref[:, s * S5_CH:(s + 1) * S5_CH].astype(BF16)
            _to_state_tiles(xr, s, _dot(ub, bbr_ref[s]))
            _to_state_tiles(xi, s, _dot(ub, bbi_ref[s]))
        fr[...] = h_ref[0]
        fi[...] = h_ref[1]
        _s5_scan_fwd(xr, xi, ar_ref, ai_ref, fr, fi, tc, sbk)
        for s in range(sbk):
            cols = slice(s * S5_CH, (s + 1) * S5_CH)
            uv = u_ref[:, cols]
            xrb = _from_state_tiles(xr, s, tc).astype(BF16)
            xib = _from_state_tiles(xi, s, tc).astype(BF16)
            dsk = d_ref[:, cols]
            y = _dot(xrb, ccr_ref[s]) - _dot(xib, cci_ref[s]) + dsk * uv
            dy = dz_ref[:, cols] * _gelu_grad(y)
            dd_ref[:, cols] += jnp.sum(dy * uv, axis=0, keepdims=True)
            dyb = dy.astype(BF16)
            dccr_ref[s] += _dot(xrb, dyb, TN)
            dcci_ref[s] += _dot(xib, dyb, TN)
            _to_state_tiles(lr, s, _dot(dyb, ccr_ref[s], NT))
            _to_state_tiles(li, s, -_dot(dyb, cci_ref[s], NT))
            du_ref[:, cols] = dy * dsk
        _s5_scan_bwd(lr, li, xr, xi, h_ref, ar_ref, ai_ref, br, bi, accr, acci, tc, sbk)
        for s in range(sbk):
            cols = slice(s * S5_CH, (s + 1) * S5_CH)
            ub = u_ref[:, cols].astype(BF16)
            lrb = _from_state_tiles(lr, s, tc).astype(BF16)
            lib = _from_state_tiles(li, s, tc).astype(BF16)
            dbbr_ref[s] += _dot(ub, lrb, TN)
            dbbi_ref[s] += _dot(ub, lib, TN)
            du_ref[:, cols] += _dot(lrb, bbr_ref[s], NT) + _dot(lib, bbi_ref[s], NT)

        @pl.when(c == nch - 1)
        def _():
            dar_ref[...] = accr[...]
            dai_ref[...] = acci[...]

    tiles = pltpu.VMEM((sbk, tc * 8, 128), F32)
    carry = pltpu.VMEM((sbk, 8, 128), F32)
    ins = [u, dz, h0, bbr, bbi, ccr, cci, a_re, a_im, d_skip]
    in_specs = [sp["act"], sp["act"], sp["h"], sp["bb"], sp["bb"], sp["cc"], sp["cc"],
                sp["a"], sp["a"], sp["d"]]
    _dep_operand(ins, in_specs, dep)
    return pl.pallas_call(
        body, grid=(nb // sbk, nch),
        in_specs=in_specs,
        out_specs=[sp["act"], sp["d"], sp["a"], sp["a"], sp["bb"], sp["bb"], sp["cc"], sp["cc"]],
        out_shape=[jax.ShapeDtypeStruct((T, D), F32), jax.ShapeDtypeStruct((1, D), F32),
                   jax.ShapeDtypeStruct((nb, 8, 128), F32), jax.ShapeDtypeStruct((nb, 8, 128), F32),
                   jax.ShapeDtypeStruct((nb, S5_CH, S5_W), F32), jax.ShapeDtypeStruct((nb, S5_CH, S5_W), F32),
                   jax.ShapeDtypeStruct((nb, S5_W, S5_CH), F32), jax.ShapeDtypeStruct((nb, S5_W, S5_CH), F32)],
        scratch_shapes=[tiles, tiles, tiles, tiles, carry, carry, carry, carry, carry, carry],
        compiler_params=_params(("parallel", "arbitrary")), name="s5_bwd",
    )(*ins)


def _s5_disc(lr, li, ldt):
    dt = jnp.exp(ldt)
    mag = jnp.exp(lr * dt)
    ang = li * dt
    cs, sn = jnp.cos(ang), jnp.sin(ang)
    lbr, lbi = mag * cs, mag * sn
    nr = lbr - 1.0
    den = lr * lr + li * li
    f_re = (nr * lr + lbi * li) / den
    f_im = (lbi * lr - nr * li) / den
    return dt, mag, cs, sn, lbr, lbi, nr, den, f_re, f_im


def _s5_param_fwd(lr, li, ldt, bt_re, bt_im):
    c, g, p = bt_re.shape

    def body(lr_ref, li_ref, ldt_ref, br_ref, bi_ref, lbr_ref, lbi_ref, bbr_ref, bbi_ref):
        _, _, _, _, lbr, lbi, _, _, f_re, f_im = _s5_disc(lr_ref[...], li_ref[...], ldt_ref[...])
        lbr_ref[...] = lbr
        lbi_ref[...] = lbi
        for ch in range(c):
            b_r, b_i = br_ref[ch], bi_ref[ch]
            bbr_ref[ch] = f_re * b_r - f_im * b_i
            bbi_ref[ch] = f_re * b_i + f_im * b_r

    gp = jax.ShapeDtypeStruct((g, p), F32)
    cgp = jax.ShapeDtypeStruct((c, g, p), F32)
    return pl.pallas_call(body, out_shape=[gp, gp, cgp, cgp], name="s5_param_fwd")(lr, li, ldt, bt_re, bt_im)


def _s5_param_bwd(lr, li, ldt, bt_re, bt_im, dlbr, dlbi, dbbr, dbbi):
    c, g, p = bt_re.shape

    def body(lr_ref, li_ref, ldt_ref, br_ref, bi_ref, dlbr_ref, dlbi_ref, dbbr_ref, dbbi_ref,
             dlr_ref, dli_ref, dldt_ref, dbr_ref, dbi_ref):
        l_r, l_i = lr_ref[...], li_ref[...]
        dt, mag, cs, sn, lbr, lbi, nr, den, f_re, f_im = _s5_disc(l_r, l_i, ldt_ref[...])
        dfr = jnp.zeros_like(l_r)
        dfi = jnp.zeros_like(l_r)
        for ch in range(c):
            b_r, b_i = br_ref[ch], bi_ref[ch]
            g_r, g_i = dbbr_ref[ch], dbbi_ref[ch]
            dbr_ref[ch] = f_re * g_r + f_im * g_i
            dbi_ref[ch] = f_re * g_i - f_im * g_r
            dfr = dfr + g_r * b_r + g_i * b_i
            dfi = dfi + g_i * b_r - g_r * b_i
        inv = 1.0 / den
        d_nr = (dfr * l_r - dfi * l_i) * inv
        d_lbi = (dfr * l_i + dfi * l_r) * inv + dlbi_ref[...]
        d_lbr = d_nr + dlbr_ref[...]
        d_den = -(dfr * f_re + dfi * f_im) * inv
        d_mag = d_lbr * cs + d_lbi * sn
        d_ang = d_lbi * lbr - d_lbr * lbi
        dlr_ref[...] = (dfr * nr + dfi * lbi) * inv + 2.0 * d_den * l_r + d_mag * mag * dt
        dli_ref[...] = (dfr * lbi - dfi * nr) * inv + 2.0 * d_den * l_i + d_ang * dt
        dldt_ref[...] = jnp.sum(d_mag * mag * l_r + d_ang * l_i, axis=1, keepdims=True) * dt

    gp = jax.ShapeDtypeStruct((g, p), F32)
    cgp = jax.ShapeDtypeStruct((c, g, p), F32)
    return pl.pallas_call(body, out_shape=[gp, gp, jax.ShapeDtypeStruct((g, 1), F32), cgp, cgp],
                          name="s5_param_bwd")(lr, li, ldt, bt_re, bt_im, dlbr, dlbi, dbbr, dbbi)


def _block_diag(t, inner_first):
    g, c, p = t.shape
    nb = g // S5_GB
    t4 = t.reshape(nb, S5_GB, c, p)
    eye = jnp.eye(S5_GB, dtype=t.dtype)
    if inner_first:
        e = t4[:, :, :, None, :] * eye[None, :, None, :, None]
        return e.reshape(nb, S5_GB * c, S5_GB * p)
    t4 = t4.transpose(0, 1, 3, 2)
    e = t4[:, :, :, None, :] * eye[None, :, None, :, None]
    return e.reshape(nb, S5_GB * p, S5_GB * c)


def _block_diag_extract(e, inner_first, c, p):
    nb = e.shape[0]
    eye = jnp.eye(S5_GB, dtype=e.dtype)
    if inner_first:
        e5 = e.reshape(nb, S5_GB, c, S5_GB, p)
        return (e5 * eye[None, :, None, :, None]).sum(3).reshape(nb * S5_GB, c, p)
    e5 = e.reshape(nb, S5_GB, p, S5_GB, c)
    return (e5 * eye[None, :, None, :, None]).sum(3).transpose(0, 1, 3, 2).reshape(nb * S5_GB, c, p)


def _att_masks(rep, gb):
    rows = rep * ATT_BLK
    qi = lax.broadcasted_iota(jnp.int32, (rows, 2 * ATT_BLK), 0) % ATT_BLK
    si = lax.broadcasted_iota(jnp.int32, (rows, 2 * ATT_BLK), 1)
    prev = (si < ATT_BLK) & (si >= qi) & (gb > 0)
    cur = (si >= ATT_BLK) & (si - ATT_BLK <= qi)
    return prev | cur


def _att_rows(start, dil):
    return pl.ds(start, ATT_BLK) if dil == 1 else pl.ds(start, ATT_BLK, stride=dil)


def _att_plan(T, dil):
    span = ATT_BLK * dil
    sbr = max(span, min(T, 1024))
    return span, sbr, T // sbr


def _att_block(sb, i, sbr, span, dil):
    loc = (i // dil) * span + i % dil
    cur = sb * sbr + loc
    gb = sb * (sbr // span) + i // dil
    return loc, cur, jnp.where(gb > 0, cur - span, cur), gb


def _att_fwd(q, k, v, grp, dil):
    T = q.shape[0]
    H = q.shape[1] // HEAD_DIM // len(DILATIONS)
    rep = H // N_KV_HEADS
    span, sbr, nsb = _att_plan(T, dil)
    scale = HEAD_DIM ** -0.5

    def body(*refs):
        q_refs = refs[:rep]
        k_ref, v_ref, o_ref, l_ref, o_slab, l_slab = refs[rep:]
        sb = pl.program_id(1)

        def blk(i, _):
            loc, cur, prv, gb = _att_block(sb, i, sbr, span, dil)
            rows = _att_rows(loc, dil)
            qs = jnp.concatenate([r[rows, :] for r in q_refs], axis=0).astype(BF16)
            kcat = jnp.concatenate([k_ref[_att_rows(prv, dil), :], k_ref[_att_rows(cur, dil), :]], axis=0)
            vcat = jnp.concatenate([v_ref[_att_rows(prv, dil), :], v_ref[_att_rows(cur, dil), :]], axis=0)
            s = jnp.where(_att_masks(rep, gb), _dot(qs, kcat.astype(BF16), NT) * scale, NEG_INF)
            m = jnp.max(s, axis=-1, keepdims=True)
            p = jnp.exp(s - m)
            l = jnp.sum(p, axis=-1, keepdims=True)
            o = _dot(p.astype(BF16), vcat.astype(BF16)) / l
            lse = jnp.broadcast_to(m + jnp.log(l), (rep * ATT_BLK, HEAD_DIM))
            for j in range(rep):
                o_slab[j, rows, :] = o[j * ATT_BLK:(j + 1) * ATT_BLK]
                l_slab[j, rows, :] = lse[j * ATT_BLK:(j + 1) * ATT_BLK]
            return 0

        lax.fori_loop(0, sbr // ATT_BLK, blk, 0)
        for j in range(rep):
            o_ref[:, j * HEAD_DIM:(j + 1) * HEAD_DIM] = o_slab[j]
            l_ref[:, j * HEAD_DIM:(j + 1) * HEAD_DIM] = l_slab[j]

    qspecs = [pl.BlockSpec((sbr, HEAD_DIM), functools.partial(lambda h, s, j: (s, grp * H + h * rep + j), j=j))
              for j in range(rep)]
    kspec = pl.BlockSpec((T, HEAD_DIM), lambda h, s: (0, h))
    ospec = pl.BlockSpec((sbr, rep * HEAD_DIM), lambda h, s: (s, h))
    slab = pltpu.VMEM((rep, sbr, HEAD_DIM), F32)
    return pl.pallas_call(
        body, grid=(N_KV_HEADS, nsb), in_specs=qspecs + [kspec, kspec], out_specs=[ospec, ospec],
        out_shape=[jax.ShapeDtypeStruct((T, H * HEAD_DIM), F32)] * 2, scratch_shapes=[slab, slab],
        compiler_params=_params(("parallel", "arbitrary")), name=f"att_fwd_d{dil}",
    )(*([q] * rep), k, v)


def _att_combine(outs, lses):
    T, W = outs[0].shape
    ng = len(outs)
    tr, tcol = _tile(T, 512, 8), _tile(W, 512)

    def body(*refs):
        o_refs, l_refs = refs[:ng], refs[ng:2 * ng]
        ob_ref, lse_ref = refs[2 * ng:]
        ls = [r[...] for r in l_refs]
        m = functools.reduce(jnp.maximum, ls)
        es = [jnp.exp(l - m) for l in ls]
        den = functools.reduce(lambda a, b: a + b, es)
        num = functools.reduce(lambda a, b: a + b, [e * o[...] for e, o in zip(es, o_refs)])
        ob_ref[...] = (num / den).astype(BF16)
        lse_ref[...] = m + jnp.log(den)

    spec = pl.BlockSpec((tr, tcol), lambda i, j: (i, j))
    return pl.pallas_call(
        body, grid=(T // tr, W // tcol), in_specs=[spec] * (2 * ng), out_specs=[spec, spec],
        out_shape=[jax.ShapeDtypeStruct((T, W), BF16), jax.ShapeDtypeStruct((T, W), F32)],
        compiler_params=_params(("parallel", "parallel")), name="att_combine")(*outs, *lses)


STAT_LANE = HEAD_DIM // 2


def _att_stats(lse, o, do):
    T, W = lse.shape
    tr = _tile(T, 256, 16)

    def body(l_ref, o_ref, do_ref, s_ref):
        lane = lax.broadcasted_iota(jnp.int32, (tr, HEAD_DIM), 1)
        for h in range(W // HEAD_DIM):
            cols = slice(h * HEAD_DIM, (h + 1) * HEAD_DIM)
            delta = jnp.sum(do_ref[:, cols] * o_ref[:, cols].astype(F32), axis=-1, keepdims=True)
            s_ref[:, cols] = jnp.where(lane < STAT_LANE, l_ref[:, cols], delta)

    spec = pl.BlockSpec((tr, W), lambda i: (i, 0))
    return pl.pallas_call(
        body, grid=(T // tr,), in_specs=[spec] * 3, out_specs=spec,
        out_shape=jax.ShapeDtypeStruct((T, W), F32),
        compiler_params=_params(("parallel",)), name="att_stats")(lse, o, do)


def _att_bwd(q, k, v, do, stats, dq, grp, dil):
    T = q.shape[0]
    H = do.shape[1] // HEAD_DIM
    rep = H // N_KV_HEADS
    hs = min(rep, 2)
    span, sbr, nsb = _att_plan(T, dil)
    scale = HEAD_DIM ** -0.5

    def body(*refs):
        q_refs, do_refs, st_refs = refs[:hs], refs[hs + 2:2 * hs + 2], refs[2 * hs + 2:3 * hs + 2]
        k_ref, v_ref = refs[hs], refs[hs + 1]
        dq_ref, dk_ref, dv_ref, dq_slab = refs[3 * hs + 3:]
        sb = pl.program_id(2)

        @pl.when((pl.program_id(1) == 0) & (sb == 0))
        def _():
            dk_ref[...] = jnp.zeros_like(dk_ref)
            dv_ref[...] = jnp.zeros_like(dv_ref)

        def blk(i, _):
            loc, cur, prv, gb = _att_block(sb, i, sbr, span, dil)
            rows, kc, kp = _att_rows(loc, dil), _att_rows(cur, dil), _att_rows(prv, dil)
            qs = jnp.concatenate([r[rows, :] for r in q_refs], axis=0).astype(BF16)
            dos = jnp.concatenate([r[rows, :] for r in do_refs], axis=0).astype(BF16)
            st = jnp.concatenate([r[rows, :] for r in st_refs], axis=0)
            kcat = jnp.concatenate([k_ref[kp, :], k_ref[kc, :]], axis=0).astype(BF16)
            vcat = jnp.concatenate([v_ref[kp, :], v_ref[kc, :]], axis=0).astype(BF16)
            s = _dot(qs, kcat, NT) * scale
            p = jnp.where(_att_masks(hs, gb), jnp.exp(s - st[:, 0:1]), 0.0)
            dp = _dot(dos, vcat, NT)
            ds = (p * (dp - st[:, STAT_LANE:STAT_LANE + 1]) * scale).astype(BF16)
            dvc = _dot(p.astype(BF16), dos, TN)
            dkc = _dot(ds, qs, TN)
            dqs = _dot(ds, kcat)
            for j in range(hs):
                dq_slab[j, rows, :] = dqs[j * ATT_BLK:(j + 1) * ATT_BLK]
            dk_ref[kc, :] += dkc[ATT_BLK:]
            dv_ref[kc, :] += dvc[ATT_BLK:]

            @pl.when(gb > 0)
            def _():
                dk_ref[kp, :] += dkc[:ATT_BLK]
                dv_ref[kp, :] += dvc[:ATT_BLK]

            return 0

        lax.fori_loop(0, sbr // ATT_BLK, blk, 0)
        for j in range(hs):
            dq_ref[:, j * HEAD_DIM:(j + 1) * HEAD_DIM] = dq_slab[j]

    def head_specs(col0):
        return [pl.BlockSpec((sbr, HEAD_DIM),
                             functools.partial(lambda h, f, s, j: (s, col0 + h * rep + f * hs + j), j=j))
                for j in range(hs)]

    kspec = pl.BlockSpec((T, HEAD_DIM), lambda h, f, s: (0, h))
    dqspec = pl.BlockSpec((sbr, hs * HEAD_DIM), lambda h, f, s: (s, (grp * H + h * rep) // hs + f))
    n_in = 3 * hs + 3
    return pl.pallas_call(
        body, grid=(N_KV_HEADS, rep // hs, nsb),
        in_specs=head_specs(grp * H) + [kspec, kspec] + head_specs(0) + head_specs(0) + [ANY],
        out_specs=[dqspec, kspec, kspec],
        out_shape=[jax.ShapeDtypeStruct(dq.shape, F32),
                   jax.ShapeDtypeStruct((T, N_KV_HEADS * HEAD_DIM), F32),
                   jax.ShapeDtypeStruct((T, N_KV_HEADS * HEAD_DIM), F32)],
        scratch_shapes=[pltpu.VMEM((hs, sbr, HEAD_DIM), F32)],
        input_output_aliases={n_in - 1: 0},
        compiler_params=_params(("parallel", "arbitrary", "arbitrary")), name=f"att_bwd_d{dil}",
    )(*([q] * hs), k, v, *([do] * hs), *([stats] * hs), dq)


def _sum_kv(dks, dvs):
    T, W = dks[0].shape
    ng = len(dks)
    tr = _tile(T, 512, 8)

    def body(*refs):
        o_ref = refs[2 * ng]
        o_ref[0] = functools.reduce(lambda a, b: a + b, [r[...] for r in refs[:ng]]).astype(BF16)
        o_ref[1] = functools.reduce(lambda a, b: a + b, [r[...] for r in refs[ng:2 * ng]]).astype(BF16)

    spec = pl.BlockSpec((tr, W), lambda i: (i, 0))
    return pl.pallas_call(
        body, grid=(T // tr,), in_specs=[spec] * (2 * ng),
        out_specs=pl.BlockSpec((2, tr, W), lambda i: (0, i, 0)),
        out_shape=jax.ShapeDtypeStruct((2, T, W), BF16),
        compiler_params=_params(("parallel",)), name="sum_kv")(*dks, *dvs)


def _local_step(x, tgt, w, wts, ready, dep0=None):
    T, D = x.shape
    g = {}

    (u0,) = _rms_fwd("rms_a", x, [w["a_norm"]], [F32], dep=dep0)
    lbr, lbi, bbt_re, bbt_im = _s5_param_fwd(w["lam_re"], w["lam_im"], w["log_dt"], w["bt_re"], w["bt_im"])
    a_re, a_im = lbr.reshape(-1, 8, 128), lbi.reshape(-1, 8, 128)
    bbr = _block_diag(bbt_re.transpose(1, 0, 2), True).astype(BF16)
    bbi = _block_diag(bbt_im.transpose(1, 0, 2), True).astype(BF16)
    ccr = _block_diag(w["c_re"], False).astype(BF16)
    cci = _block_diag(w["c_im"], False).astype(BF16)
    z, h0 = _s5_fwd(u0, bbr, bbi, ccr, cci, a_re, a_im, w["s5_d"])
    w_glu = wts("glu", z)
    x1, vg = _mm_dual_fwd("glu_fwd", z, w_glu, x, "glu")

    def ffn_fwd(xin, layer):
        (nrm,) = _rms_fwd(f"rms_f{layer}", xin, [w["ffn_norm"][layer:layer + 1]], [BF16])
        w_in = wts(f"win{layer}", nrm)
        gu, act = _mm_dual_fwd(f"ffn_in{layer}", nrm, w_in, None, "ffn")
        w_out = wts(f"wout{layer}", act)
        xout = _mm_nn(f"ffn_out{layer}", act, w_out, res=xin)
        return xout, (nrm, gu, act, w_in, w_out)

    x2, saved0 = ffn_fwd(x1, 0)
    kvn, hb = _rms_fwd("rms_b", x2, [w["kv_norm"], w["b_norm"]], [BF16, BF16])
    w_kv, w_q, w_o = wts("wkv", hb), wts("wq", hb), wts("wo", hb)
    k, v = _mm_kv("kv_proj", kvn, w_kv)
    q = _mm_nn("q_proj", hb, w_q)
    outs, lses = [], []
    for grp, dil in enumerate(DILATIONS):
        o_g, l_g = _att_fwd(q, k, v, grp, dil)
        outs.append(o_g)
        lses.append(l_g)
    o, lse = _att_combine(outs, lses)
    x3 = _mm_nn("o_proj", o, w_o, res=x2)
    x4, saved1 = ffn_fwd(x3, 1)
    loss_blk, dx4, dx4b, g["final_norm"] = _loss_head(x4, w["final_norm"], tgt)

    def ffn_bwd(dx, dxb, xin, saved, layer, dep):
        nrm, gu, act, w_in, w_out = saved
        dgu = _mm_nt_ffn_bwd(f"ffn_dact{layer}", dxb, w_out, gu, dep=dep)
        g_wout = _mm_tn(f"ffn_dwout{layer}", act, dxb)
        g_win = _mm_tn_pair(f"ffn_dwin{layer}", nrm, dgu)
        dn = _mm_nt_pair(f"ffn_dn{layer}", dgu, w_in)
        dxo, dxob, (dgn,) = _rms_bwd(f"rms_f{layer}_bwd", xin, dx, [w["ffn_norm"][layer:layer + 1]], [dn])
        tok = ready({f"win{layer}": g_win, f"wout{layer}": g_wout})
        return dxo, dxob, dgn, tok

    dx3, dx3b, dfn1, tok = ffn_bwd(dx4, dx4b, x3, saved1, 1, None)
    do = _mm_nt("o_proj_dx", [dx3b], w_o, dep=tok)
    g_wo = _mm_tn("o_proj_dw", o, dx3b)
    stats = _att_stats(lse, o, do)
    dq = lax.empty(q.shape, F32)
    dks, dvs = [], []
    for grp, dil in enumerate(DILATIONS):
        dq, dk_g, dv_g = _att_bwd(q, k, v, do, stats, dq, grp, dil)
        dks.append(dk_g)
        dvs.append(dv_g)
    dkv = _sum_kv(dks, dvs)
    dhb = _mm_nt("q_proj_dx", [dq], w_q)
    g_wq = _mm_tn("q_proj_dw", hb, dq)
    dkvn = _mm_nt_pair("kv_proj_dx", dkv, w_kv)
    g_wkv = _mm_tn_pair("kv_proj_dw", kvn, dkv)
    dx2, dx2b, (g["kv_norm"], g["b_norm"]) = _rms_bwd(
        "rms_b_bwd", x2, dx3, [w["kv_norm"], w["b_norm"]], [dkvn, dhb])
    tok = ready({"wkv": g_wkv, "wq": g_wq, "wo": g_wo})
    dx1, dx1b, dfn0, tok = ffn_bwd(dx2, dx2b, x1, saved0, 0, tok)
    g["ffn_norm"] = jnp.concatenate([dfn0, dfn1], axis=0)

    dvg = _glu_bwd(dx1, vg, dep=tok)
    dz = _mm_nt_pair("glu_dx", dvg, w_glu)
    tok = ready({"glu": _mm_tn_pair("glu_dw", z, dvg)})
    du, g["s5_d"], da_re, da_im, dbbr, dbbi, dccr, dcci = _s5_bwd(
        u0, dz, h0, bbr, bbi, ccr, cci, a_re, a_im, w["s5_d"], dep=tok)
    ready({}, after=du)
    C, G, P = w["bt_re"].shape
    g["c_re"] = _block_diag_extract(dccr, False, C, P)
    g["c_im"] = -_block_diag_extract(dcci, False, C, P)
    g["lam_re"], g["lam_im"], g["log_dt"], g["bt_re"], g["bt_im"] = _s5_param_bwd(
        w["lam_re"], w["lam_im"], w["log_dt"], w["bt_re"], w["bt_im"],
        da_re.reshape(G, P), da_im.reshape(G, P),
        _block_diag_extract(dbbr, True, C, P).transpose(1, 0, 2),
        _block_diag_extract(dbbi, True, C, P).transpose(1, 0, 2))
    grad_x, _, (g["a_norm"],) = _rms_bwd("rms_a_bwd", x, dx1, [w["a_norm"]], [du])
    return loss_blk, grad_x, g


def _coords():
    return lax.axis_index("x"), lax.axis_index("y"), lax.axis_index("c")


def _dev_index(dev):
    return 4 * dev[0] + 2 * dev[1] + dev[2]


def _shard_window(ref, axis, width, idx):
    sl = [slice(None)] * len(ref.shape)
    sl[axis] = pl.ds(pl.multiple_of(idx * width, width), width)
    return ref.at[tuple(sl)]


def _all_gather(name, shards, axes):
    na = len(shards)
    widths = [s.shape[ax] for s, ax in zip(shards, axes)]
    out_shapes = []
    for s, ax in zip(shards, axes):
        shp = list(s.shape)
        shp[ax] *= N_DEV
        out_shapes.append(jax.ShapeDtypeStruct(tuple(shp), s.dtype))

    def body(*refs):
        ins, outs = refs[:na], refs[na:2 * na]
        send_sems, recv_sems, local_sems = refs[2 * na:]
        x, y, c = _coords()
        me, sib = (x, y, c), (x, y, 1 - c)
        chips = [(1 - x, y), (x, 1 - y), (1 - x, 1 - y)]

        def blk(a, dev):
            return _shard_window(outs[a], axes[a], widths[a], _dev_index(dev))

        def copy(a, kk, block, to, src=None):
            return pltpu.make_async_remote_copy(
                src_ref=blk(a, block) if src is None else src, dst_ref=blk(a, block),
                send_sem=send_sems.at[a, kk], recv_sem=recv_sems.at[a, kk],
                device_id=to, device_id_type=MESH)

        local = [pltpu.make_async_copy(ins[a], blk(a, me), local_sems.at[a]) for a in range(na)]
        for cp in local:
            cp.start()
        sent = []
        for a in range(na):
            first = [copy(a, 0, me, sib, src=ins[a])]
            first += [copy(a, 1 + j, me, (*chip, c), src=ins[a]) for j, chip in enumerate(chips)]
            for cp in first:
                cp.start()
            sent += first
        for a in range(na):
            for j, chip in enumerate(chips):
                copy(a, 1 + j, (*chip, c), me).wait_recv()
                fwd = copy(a, 4 + j, (*chip, c), sib)
                fwd.start()
                sent.append(fwd)
        for a in range(na):
            copy(a, 0, sib, me).wait_recv()
            for j, chip in enumerate(chips):
                copy(a, 4 + j, (*chip, 1 - c), me).wait_recv()
        for cp in sent:
            cp.wait_send()
        for cp in local:
            cp.wait()

    return pl.pallas_call(
        body, out_shape=out_shapes, in_specs=[ANY] * na, out_specs=[ANY] * na,
        scratch_shapes=[pltpu.SemaphoreType.DMA((na, 7)), pltpu.SemaphoreType.DMA((na, 7)),
                        pltpu.SemaphoreType.DMA((na,))],
        name=name)(*shards)


HBM = pl.BlockSpec(memory_space=pltpu.HBM)
SEM = pl.BlockSpec(memory_space=pltpu.SEMAPHORE)
TOKEN_SPEC = pl.BlockSpec(memory_space=pltpu.VMEM)
TOKEN_SHAPE = jax.ShapeDtypeStruct((8, 128), F32)
SPLIT_PARAMS = pltpu.CompilerParams(has_side_effects=pltpu.SideEffectType.DATAFLOW_SIDE_EFFECTING)


def _hbm(x):
    return pltpu.with_memory_space_constraint(x, pltpu.HBM)


def _hbm_like(x):
    return pltpu.HBM(x.shape, x.dtype)


def _dma_sems(*shape):
    return pltpu.SemaphoreType.DMA(shape)


def _cast_and_place(name, shard, layer, axis, pos, dtype):
    rows, cols = shard.shape[-2:]
    tr = _tile(rows, 256, 16)
    nt = rows // tr
    full = (rows, cols * N_DEV) if axis == 1 else (rows * N_DEV, cols)

    def dev(p):
        return 4 * p[0] + 2 * p[1] + p[2]

    def body(pos_ref, s_ref, b_ref, l_ref):
        v = s_ref[...].astype(dtype)
        b_ref[...] = v
        l_ref[...] = v

    blk = pl.BlockSpec((tr, cols), lambda i, p: (i, 0))
    if axis == 1:
        lspec = pl.BlockSpec((tr, cols), lambda i, p: (i, dev(p)))
    else:
        lspec = pl.BlockSpec((tr, cols), lambda i, p: (dev(p) * nt + i, 0))
    return pl.pallas_call(
        body, grid_spec=pltpu.PrefetchScalarGridSpec(
            num_scalar_prefetch=1, grid=(nt,),
            in_specs=[pl.BlockSpec((None, tr, cols), lambda i, p: (layer, i, 0))], out_specs=[blk, lspec]),
        out_shape=[jax.ShapeDtypeStruct((rows, cols), dtype), jax.ShapeDtypeStruct(full, dtype)],
        compiler_params=_params(("parallel",)), name=name)(pos, shard)


def _gather_start(name, shards, lands, axes, groups):
    na, ng = len(shards), len(groups)
    widths = [s.shape[ax] for s, ax in zip(shards, axes)]

    def body(*refs):
        sh, ld = refs[:na], refs[na:2 * na]
        sems = refs[2 * na:2 * na + 3 * ng]
        token = refs[-1]
        x, y, c = _coords()
        me, sib = (x, y, c), (x, y, 1 - c)
        chips = [(1 - x, y), (x, 1 - y), (1 - x, 1 - y)]
        for gi, grp in enumerate(groups):
            send, r_d2d, r_ici = sems[3 * gi:3 * gi + 3]
            for li, a in enumerate(grp):
                dst = _shard_window(ld[a], axes[a], widths[a], _dev_index(me))
                pltpu.make_async_remote_copy(
                    src_ref=sh[a], dst_ref=dst, send_sem=send.at[4 * li], recv_sem=r_d2d.at[li],
                    device_id=sib, device_id_type=MESH).start()
                for j, chip in enumerate(chips):
                    pltpu.make_async_remote_copy(
                        src_ref=sh[a], dst_ref=dst, send_sem=send.at[4 * li + 1 + j], recv_sem=r_ici.at[3 * li + j],
                        device_id=(*chip, c), device_id_type=MESH).start()
        token[...] = jnp.zeros_like(token)

    out_shape, out_specs = [], []
    for grp in groups:
        out_shape += [_dma_sems(4 * len(grp)), _dma_sems(len(grp)), _dma_sems(3 * len(grp))]
        out_specs += [SEM] * 3
    out_shape += [_hbm_like(s) for s in shards] + [_hbm_like(l) for l in lands] + [TOKEN_SHAPE]
    out_specs += [HBM] * (2 * na) + [TOKEN_SPEC]
    aliases = {a: 3 * ng + a for a in range(2 * na)}
    res = pl.pallas_call(
        body, name=name, out_shape=out_shape, in_specs=[HBM] * (2 * na),
        out_specs=out_specs, input_output_aliases=aliases, compiler_params=SPLIT_PARAMS,
    )(*[_hbm(s) for s in shards], *[_hbm(l) for l in lands])
    sems = [tuple(res[3 * gi:3 * gi + 3]) for gi in range(ng)]
    return sems, list(res[3 * ng:3 * ng + na]), list(res[3 * ng + na:3 * ng + 2 * na]), res[-1]


def _gather_forward(name, lands, axes, r_ici, after):
    n = len(lands)
    widths = [l.shape[ax] // N_DEV for l, ax in zip(lands, axes)]

    def body(*refs):
        ld, r_ici_ref = refs[:n], refs[n]
        f_send, f_recv = refs[n + 2], refs[n + 3]
        x, y, c = _coords()
        sib = (x, y, 1 - c)
        chips = [(1 - x, y), (x, 1 - y), (1 - x, 1 - y)]
        for li in range(n):
            for j, chip in enumerate(chips):
                blk = _shard_window(ld[li], axes[li], widths[li], _dev_index((*chip, c)))
                pltpu.make_async_remote_copy(
                    src_ref=blk, dst_ref=blk, send_sem=f_send.at[3 * li + j], recv_sem=r_ici_ref.at[3 * li + j],
                    device_id=(*chip, c), device_id_type=MESH).wait_recv()
                pltpu.make_async_remote_copy(
                    src_ref=blk, dst_ref=blk, send_sem=f_send.at[3 * li + j], recv_sem=f_recv.at[3 * li + j],
                    device_id=sib, device_id_type=MESH).start()

    res = pl.pallas_call(
        body, name=name, out_shape=[_dma_sems(3 * n), _dma_sems(3 * n)] + [_hbm_like(l) for l in lands],
        in_specs=[HBM] * n + [SEM, ANY], out_specs=[SEM, SEM] + [HBM] * n,
        input_output_aliases={li: 2 + li for li in range(n)}, compiler_params=SPLIT_PARAMS,
    )(*lands, r_ici, after)
    return res[0], res[1], list(res[2:])


def _gather_finish(name, shards, lands, axes, send, r_d2d, f_send, f_recv):
    n = len(lands)
    widths = [l.shape[ax] // N_DEV for l, ax in zip(lands, axes)]

    def body(*refs):
        sh, ld = refs[:n], refs[n:2 * n]
        send_ref, r_d2d_ref, f_send_ref, f_recv_ref = refs[2 * n:2 * n + 4]
        x, y, c = _coords()
        me, sib = (x, y, c), (x, y, 1 - c)
        chips = [(1 - x, y), (x, 1 - y), (1 - x, 1 - y)]

        def blk(li, dev):
            return _shard_window(ld[li], axes[li], widths[li], _dev_index(dev))

        for li in range(n):
            for kk in range(4):
                pltpu.make_async_remote_copy(
                    src_ref=sh[li], dst_ref=blk(li, me), send_sem=send_ref.at[4 * li + kk], recv_sem=r_d2d_ref.at[li],
                    device_id=sib, device_id_type=MESH).wait_send()
            pltpu.make_async_remote_copy(
                src_ref=blk(li, sib), dst_ref=blk(li, sib), send_sem=send_ref.at[4 * li], recv_sem=r_d2d_ref.at[li],
                device_id=sib, device_id_type=MESH).wait_recv()
            for j, chip in enumerate(chips):
                pltpu.make_async_remote_copy(
                    src_ref=blk(li, (*chip, c)), dst_ref=blk(li, (*chip, c)), send_sem=f_send_ref.at[3 * li + j],
                    recv_sem=f_recv_ref.at[3 * li + j], device_id=sib, device_id_type=MESH).wait_send()
                pltpu.make_async_remote_copy(
                    src_ref=blk(li, (*chip, 1 - c)), dst_ref=blk(li, (*chip, 1 - c)), send_sem=f_send_ref.at[3 * li + j],
                    recv_sem=f_recv_ref.at[3 * li + j], device_id=sib, device_id_type=MESH).wait_recv()

    res = pl.pallas_call(
        body, name=name, out_shape=[_hbm_like(s) for s in shards] + [_hbm_like(l) for l in lands],
        in_specs=[HBM] * (2 * n) + [SEM] * 4, out_specs=[HBM] * (2 * n),
        input_output_aliases={i: i for i in range(2 * n)}, compiler_params=SPLIT_PARAMS,
    )(*shards, *lands, send, r_d2d, f_send, f_recv)
    return list(res[n:])


def _chip_exchange_start(name, parts):
    n = len(parts)

    def body(*refs):
        src, ld = refs[:n], refs[n:2 * n]
        send, recv = refs[2 * n], refs[2 * n + 1]
        token = refs[-1]
        x, y, c = _coords()
        chips = [(1 - x, y), (x, 1 - y), (1 - x, 1 - y)]
        for li in range(n):
            for kk, chip in enumerate(chips):
                pltpu.make_async_remote_copy(
                    src_ref=src[li].at[kk], dst_ref=ld[li].at[kk], send_sem=send.at[3 * li + kk],
                    recv_sem=recv.at[3 * li + kk], device_id=(*chip, c), device_id_type=MESH).start()
        token[...] = jnp.zeros_like(token)

    lands = [lax.empty(p.shape, p.dtype) for p in parts]
    res = pl.pallas_call(
        body, name=name,
        out_shape=[_dma_sems(3 * n), _dma_sems(3 * n)] + [_hbm_like(p) for p in parts] * 2 + [TOKEN_SHAPE],
        in_specs=[HBM] * (2 * n), out_specs=[SEM, SEM] + [HBM] * (2 * n) + [TOKEN_SPEC],
        input_output_aliases={i: 2 + i for i in range(2 * n)}, compiler_params=SPLIT_PARAMS,
    )(*[_hbm(p) for p in parts], *[_hbm(l) for l in lands])
    return res[0], res[1], list(res[2:2 + n]), list(res[2 + n:2 + 2 * n]), res[-1]


def _chip_exchange_finish(name, started, after):
    counts = [len(st[2]) for st in started]
    total = sum(counts)
    ns = len(started)

    def body(*refs):
        src, ld = refs[:total], refs[total:2 * total]
        sems = refs[2 * total:2 * total + 2 * ns]
        x, y, c = _coords()
        chips = [(1 - x, y), (x, 1 - y), (1 - x, 1 - y)]
        off = 0
        for si, cnt in enumerate(counts):
            send, recv = sems[2 * si], sems[2 * si + 1]
            for li in range(cnt):
                for kk, chip in enumerate(chips):
                    cp = pltpu.make_async_remote_copy(
                        src_ref=src[off + li].at[kk], dst_ref=ld[off + li].at[kk], send_sem=send.at[3 * li + kk],
                        recv_sem=recv.at[3 * li + kk], device_id=(*chip, c), device_id_type=MESH)
                    cp.wait_send()
                    cp.wait_recv()
            off += cnt

    srcs = [p for st in started for p in st[2]]
    lands = [l for st in started for l in st[3]]
    sems = [s for st in started for s in st[:2]]
    res = pl.pallas_call(
        body, name=name, out_shape=[_hbm_like(p) for p in srcs + lands],
        in_specs=[HBM] * (2 * total) + [SEM] * (2 * ns) + [ANY], out_specs=[HBM] * (2 * total),
        input_output_aliases={i: i for i in range(2 * total)}, compiler_params=SPLIT_PARAMS,
    )(*srcs, *lands, *sems, after)
    out, off = [], total
    for cnt in counts:
        out.append(list(res[off:off + cnt]))
        off += cnt
    return out


def _pair_copies(src, ld, send, recv, axes, widths):
    x, y, c = _coords()
    chips = [(x, y), (1 - x, y), (x, 1 - y), (1 - x, 1 - y)]
    return [pltpu.make_async_remote_copy(
        src_ref=_shard_window(src[li], axes[li], widths[li], _dev_index((*chip, 1 - c))),
        dst_ref=ld[li].at[kk], send_sem=send.at[4 * li + kk], recv_sem=recv.at[4 * li + kk],
        device_id=(x, y, 1 - c), device_id_type=MESH)
        for li in range(len(src)) for kk, chip in enumerate(chips)]


def _pair_exchange_start(name, grads, axes):
    n = len(grads)
    widths = [gr.shape[ax] // N_DEV for gr, ax in zip(grads, axes)]
    lands = []
    for gr, ax, wd in zip(grads, axes, widths):
        shp = list(gr.shape)
        shp[ax] = wd
        lands.append(lax.empty((4, *shp), gr.dtype))

    def body(*refs):
        for cp in _pair_copies(refs[:n], refs[n:2 * n], refs[2 * n], refs[2 * n + 1], axes, widths):
            cp.start()
        refs[-1][...] = jnp.zeros_like(refs[-1])

    res = pl.pallas_call(
        body, name=name,
        out_shape=[_dma_sems(4 * n), _dma_sems(4 * n)] + [_hbm_like(a) for a in grads + lands] + [TOKEN_SHAPE],
        in_specs=[HBM] * (2 * n), out_specs=[SEM, SEM] + [HBM] * (2 * n) + [TOKEN_SPEC],
        input_output_aliases={i: 2 + i for i in range(2 * n)}, compiler_params=SPLIT_PARAMS,
    )(*[_hbm(a) for a in grads + lands])
    return res[0], res[1], list(res[2:2 + n]), list(res[2 + n:2 + 2 * n]), res[-1]


def _pair_exchange_finish(name, started, axes, after):
    send, recv, grads, lands, _ = started
    n = len(grads)
    widths = [gr.shape[ax] // N_DEV for gr, ax in zip(grads, axes)]

    def body(*refs):
        for cp in _pair_copies(refs[:n], refs[n:2 * n], refs[2 * n], refs[2 * n + 1], axes, widths):
            cp.wait_send()
            cp.wait_recv()

    res = pl.pallas_call(
        body, name=name, out_shape=[_hbm_like(a) for a in grads + lands],
        in_specs=[HBM] * (2 * n) + [SEM, SEM, ANY], out_specs=[HBM] * (2 * n),
        input_output_aliases={i: i for i in range(2 * n)}, compiler_params=SPLIT_PARAMS,
    )(*grads, *lands, send, recv, after)
    return list(res[:n]), list(res[n:])


def _pair_sum(name, grad, land, axis, pos):
    wd = grad.shape[axis] // N_DEV
    shard_shape = land.shape[1:]
    rows, cols = shard_shape
    tr = _tile(rows, 256, 16)
    nt = rows // tr

    def dev_of(kk, pos_ref):
        return 4 * (pos_ref[0] ^ (kk & 1)) + 2 * (pos_ref[1] ^ (kk >> 1)) + pos_ref[2]

    if axis == 1:
        gspec = pl.BlockSpec((tr, wd), lambda kk, t, p: (t, dev_of(kk, p)))
    else:
        gspec = pl.BlockSpec((tr, cols), lambda kk, t, p: (dev_of(kk, p) * nt + t, 0))
    lspec = pl.BlockSpec((None, tr, cols), lambda kk, t, p: (kk, t, 0))
    p0spec = pl.BlockSpec((tr, cols), lambda kk, t, p: (jnp.where(kk == 0, t, nt - 1), 0))
    pbspec = pl.BlockSpec((None, tr, cols), lambda kk, t, p: (jnp.maximum(kk - 1, 0), jnp.where(kk == 0, 0, t), 0))

    def body(pos_ref, g_ref, l_ref, p0_ref, pb_ref):
        kk = pl.program_id(0)
        s = g_ref[...].astype(F32) + l_ref[...].astype(F32)

        @pl.when(kk == 0)
        def _():
            p0_ref[...] = s

        @pl.when(kk > 0)
        def _():
            pb_ref[...] = s.astype(BF16)

    return pl.pallas_call(
        body,
        grid_spec=pltpu.PrefetchScalarGridSpec(
            num_scalar_prefetch=1, grid=(4, nt), in_specs=[gspec, lspec], out_specs=[p0spec, pbspec]),
        out_shape=[jax.ShapeDtypeStruct(shard_shape, F32), jax.ShapeDtypeStruct((3, *shard_shape), BF16)],
        compiler_params=_params(("arbitrary", "arbitrary")), name=name)(pos, grad, land)


def _adamw(name, parts, w, m, v, layer=None, prev=None):
    rows, cols = w.shape[-2:]
    tr = _tile(rows, 256, 8)
    npart = len(parts)
    c1 = 1.0 - ADAM_B1 ** ADAM_STEP
    c2 = 1.0 - ADAM_B2 ** ADAM_STEP

    def body(*refs):
        p_refs = refs[:npart]
        w_ref, m_ref, v_ref = refs[npart:npart + 3]
        g_ref, d_ref, nm_ref, nv_ref = refs[-4:]
        g = None
        for r in p_refs:
            if len(r.shape) == 3:
                for i in range(r.shape[0]):
                    t = r[i].astype(F32)
                    g = t if g is None else g + t
            else:
                t = r[...].astype(F32)
                g = t if g is None else g + t
        nm = ADAM_B1 * m_ref[...] + (1.0 - ADAM_B1) * g
        nv = ADAM_B2 * v_ref[...] + (1.0 - ADAM_B2) * (g * g)
        g_ref[...] = g
        nm_ref[...] = nm
        nv_ref[...] = nv
        d_ref[...] = -ADAM_LR * ((nm / c1) / (jnp.sqrt(nv / c2) + ADAM_EPS) + ADAM_WD * w_ref[...])

    spec = pl.BlockSpec((tr, cols), lambda i: (i, 0))
    wspec = spec if layer is None else pl.BlockSpec((None, tr, cols), lambda i: (layer, i, 0))
    pspecs = [pl.BlockSpec((p.shape[0], tr, cols), lambda i: (0, i, 0)) if p.ndim == 3 else spec
              for p in parts]
    prev = list(prev) if prev else []
    return pl.pallas_call(
        body, grid=(rows // tr,), in_specs=pspecs + [wspec] * 3 + [ANY] * len(prev), out_specs=[wspec] * 4,
        out_shape=[jax.ShapeDtypeStruct(w.shape, F32)] * 4,
        input_output_aliases={npart + 3 + i: i for i in range(len(prev))},
        compiler_params=_params(("parallel",)), name=name)(*parts, w, m, v, *prev)


SMALL_NAMES = ("s5_lam_re", "s5_lam_im", "s5_log_dt", "s5_b_re", "s5_b_im", "s5_c_re", "s5_c_im",
               "ffn_norm", "b_norm_mix", "kv_norm", "final_norm")
SMALL_PAD = 1024


def _pack(parts):
    flat = []
    for p in parts:
        f = p.reshape(-1)
        pad = (-f.shape[0]) % SMALL_PAD
        if pad:
            f = jnp.concatenate([f, jnp.zeros((pad,), f.dtype)])
        flat.append(f)
    return jnp.concatenate(flat).reshape(-1, 128)


def _unpack(packed, shapes):
    flat = packed.reshape(-1)
    out, off = [], 0
    for shp in shapes:
        size = math.prod(shp)
        out.append(flat[off:off + size].reshape(shp))
        off += size + (-size) % SMALL_PAD
    return out


def kernel(x, s5_lam_re, s5_lam_im, s5_log_dt, s5_b_re, s5_b_im, s5_c_re, s5_c_im, s5_d, s5_w_glu, a_norm_mix, ffn_norm, ffn_w_in, ffn_w_out, b_norm_mix, attn_w_q, attn_w_o, kv_norm, w_kv, final_norm, loss_target, m_s5_lam_re, m_s5_lam_im, m_s5_log_dt, m_s5_b_re, m_s5_b_im, m_s5_c_re, m_s5_c_im, m_s5_d, m_s5_w_glu, m_a_norm_mix, m_ffn_norm, m_ffn_w_in, m_ffn_w_out, m_b_norm_mix, m_attn_w_q, m_attn_w_o, m_kv_norm, m_w_kv, m_final_norm, v_s5_lam_re, v_s5_lam_im, v_s5_log_dt, v_s5_b_re, v_s5_b_im, v_s5_c_re, v_s5_c_im, v_s5_d, v_s5_w_glu, v_a_norm_mix, v_ffn_norm, v_ffn_w_in, v_ffn_w_out, v_b_norm_mix, v_attn_w_q, v_attn_w_o, v_kv_norm, v_w_kv, v_final_norm):
    args = dict(locals())
    T, D = x.shape[1], x.shape[2]
    n_layers = ffn_w_in.shape[0]
    xi_, yi_, ci_ = _coords()
    pos = jnp.stack([xi_, yi_, ci_]).astype(jnp.int32)
    me = 4 * xi_ + 2 * yi_ + ci_

    big_names = ["glu"] + [f"win{l}" for l in range(n_layers)] + [f"wout{l}" for l in range(n_layers)] \
        + ["wkv", "wq", "wo"]
    big_shards = [(s5_w_glu, 0)] + [(ffn_w_in, l) for l in range(n_layers)] \
        + [(ffn_w_out, l) for l in range(n_layers)] + [(w_kv[None], 0), (attn_w_q, 0), (attn_w_o, 0)]
    big_axes = [1] + [1] * n_layers + [0] * n_layers + [0, 1, 0]
    big_out_names = ["s5_w_glu"] + ["ffn_w_in"] * n_layers + ["ffn_w_out"] * n_layers \
        + ["w_kv", "attn_w_q", "attn_w_o"]
    index_of = {n: i for i, n in enumerate(big_names)}
    vec_shard = jnp.concatenate([s5_d, a_norm_mix], axis=0)
    (vecs,) = _all_gather("vectors_all_gather", [vec_shard], [1])

    gather_groups = [["glu"], ["win0"], ["wout0"], ["wkv", "wq", "wo"], ["win1", "wout1"]]
    group_idx = [[index_of[n] for n in grp] for grp in gather_groups]
    group_of = {n: gi for gi, grp in enumerate(gather_groups) for n in grp}
    placed = [_cast_and_place(f"cast_place_{n}", s, l, ax, pos, BF16)
              for n, (s, l), ax in zip(big_names, big_shards, big_axes)]
    gather_sems, shards_thru, lands_thru, start_token = _gather_start(
        "weights_gather_start", [p[0] for p in placed], [p[1] for p in placed], big_axes, group_idx)
    full = {}

    def wts(name, after):
        if name not in full:
            gi = group_of[name]
            idx = group_idx[gi]
            axes = [big_axes[a] for a in idx]
            send, r_d2d, r_ici = gather_sems[gi]
            f_send, f_recv, lands = _gather_forward(
                f"weights_gather_forward{gi}", [lands_thru[a] for a in idx], axes, r_ici, after)
            done = _gather_finish(f"weights_gather_finish{gi}", [shards_thru[a] for a in idx], lands, axes,
                                  send, r_d2d, f_send, f_recv)
            full.update(zip(gather_groups[gi], done))
        return full[name]

    exchanges, pending = [], []

    def flush(after):
        names, axes, started = pending.pop()
        tag = "_".join(names)
        grads, lands = _pair_exchange_finish(f"rs_pair_exchange_finish_{tag}", started, axes, after)
        p0s, pbs = [], []
        for n, gr, land, ax in zip(names, grads, lands, axes):
            p0, pb = _pair_sum(f"rs_pair_sum_{n}", gr, land, ax, pos)
            p0s.append(p0)
            pbs.append(pb)
        started = _chip_exchange_start(f"rs_chip_exchange_start_{tag}", pbs)
        exchanges.append((names, p0s, started))
        return started[4]

    def ready(grads, after=None):
        if not grads:
            return flush(after)
        names = list(grads)
        axes = [big_axes[index_of[n]] for n in names]
        started = _pair_exchange_start(f"rs_pair_exchange_start_{'_'.join(names)}", [grads[n] for n in names], axes)
        token = flush(started[4]) if pending else started[4]
        pending.append((names, axes, started))
        return token

    G, P, C = s5_b_re.shape[1:]
    w = dict(
        a_norm=vecs[1:2], s5_d=vecs[0:1],
        ffn_norm=ffn_norm, b_norm=b_norm_mix, kv_norm=kv_norm.reshape(1, D), final_norm=final_norm.reshape(1, D),
        lam_re=s5_lam_re[0], lam_im=s5_lam_im[0], log_dt=s5_log_dt.reshape(G, 1),
        bt_re=s5_b_re[0].transpose(2, 0, 1), bt_im=s5_b_im[0].transpose(2, 0, 1),
        c_re=s5_c_re[0], c_im=s5_c_im[0],
    )
    loss_blk, grad_x, g = _local_step(x[0], loss_target[0], w, wts, ready, dep0=start_token)
    loss = lax.psum(loss_blk[0, 0], ("x", "y", "c"))

    out = {}

    def put(name, res, shape):
        for kind, r in zip(("grad", "delta", "new_m", "new_v"), res):
            out[f"{kind}_{name}"] = r.reshape(shape)

    small_g = dict(
        s5_lam_re=g["lam_re"], s5_lam_im=g["lam_im"], s5_log_dt=g["log_dt"],
        s5_b_re=g["bt_re"].transpose(1, 2, 0), s5_b_im=g["bt_im"].transpose(1, 2, 0),
        s5_c_re=g["c_re"], s5_c_im=g["c_im"], ffn_norm=g["ffn_norm"], b_norm_mix=g["b_norm"],
        kv_norm=g["kv_norm"], final_norm=g["final_norm"])
    packed = _pack([small_g[n] for n in SMALL_NAMES] + [g["s5_d"], g["a_norm"]])
    rows = packed.shape[0]
    small_shard, small_land = _cast_and_place("place_small_grads", packed[None], 0, 0, pos, F32)
    small_sems, small_thru, small_lands, _ = _gather_start(
        "small_grads_gather_start", [small_shard], [small_land], [0], [[0]])

    updated = {}

    def update(names, p0s, recvd):
        for name, p0, rc in zip(names, p0s, recvd):
            oname = big_out_names[index_of[name]]
            w3, layer = big_shards[index_of[name]]
            updated[oname] = _adamw(f"adamw_{name}", [p0, rc], w3, args["m_" + oname].reshape(w3.shape),
                                    args["v_" + oname].reshape(w3.shape), layer=layer, prev=updated.get(oname))
        return updated[big_out_names[index_of[names[0]]]][0]

    early, last = exchanges[:-1], exchanges[-1]
    landed = _chip_exchange_finish("rs_chip_exchange_finish_early", [e[2] for e in early], grad_x)
    marker = update(early[0][0], early[0][1], landed[0])
    send, r_d2d, r_ici = small_sems[0]
    f_send, f_recv, small_lands = _gather_forward("small_grads_gather_forward", small_lands, [0], r_ici, marker)
    for (names, p0s, _), recvd in zip(early[1:], landed[1:]):
        marker = update(names, p0s, recvd)
    (all_parts,) = _gather_finish("small_grads_gather_finish", small_thru, small_lands, [0],
                                  send, r_d2d, f_send, f_recv)
    (recvd,) = _chip_exchange_finish("rs_chip_exchange_finish_last", [last[2]], marker)
    update(last[0], last[1], recvd)
    for oname, res in updated.items():
        put(oname, res, args[oname].shape)
    all_parts = all_parts.reshape(N_DEV, rows, 128)
    n_rep_rows = _pack([small_g[n] for n in SMALL_NAMES]).shape[0]
    w_pack = _pack([args[n] for n in SMALL_NAMES])
    m_pack = _pack([args["m_" + n] for n in SMALL_NAMES])
    v_pack = _pack([args["v_" + n] for n in SMALL_NAMES])
    res = _adamw("adamw_small", [all_parts[:, :n_rep_rows]], w_pack, m_pack, v_pack)
    shapes = [args[n].shape for n in SMALL_NAMES]
    unpacked = [_unpack(r, shapes) for r in res]
    for i, n in enumerate(SMALL_NAMES):
        put(n, [u[i] for u in unpacked], args[n].shape)
    ws = D // N_DEV
    tail = all_parts[:, n_rep_rows:].reshape(N_DEV, 2, D)
    tail = lax.dynamic_slice_in_dim(tail, me * ws, ws, axis=2)
    res = _adamw("adamw_vec", [tail], vec_shard,
                 jnp.concatenate([m_s5_d, m_a_norm_mix], axis=0), jnp.concatenate([v_s5_d, v_a_norm_mix], axis=0))
    put("s5_d", [r[0:1] for r in res], s5_d.shape)
    put("a_norm_mix", [r[1:2] for r in res], a_norm_mix.shape)

    names = ("s5_lam_re", "s5_lam_im", "s5_log_dt", "s5_b_re", "s5_b_im", "s5_c_re", "s5_c_im", "s5_d",
             "s5_w_glu", "a_norm_mix", "ffn_norm", "ffn_w_in", "ffn_w_out", "b_norm_mix", "attn_w_q",
             "attn_w_o", "kv_norm", "w_kv", "final_norm")
    result = [loss, grad_x.reshape(x.shape)]
    for kind in ("grad", "delta", "new_m", "new_v"):
        result += [out[f"{kind}_{n}"] for n in names]
    return tuple(result)
```

```python
import functools
import math

import jax
import jax.numpy as jnp
from jax import lax
from jax.experimental import pallas as pl
from jax.experimental.pallas import tpu as pltpu

F32 = jnp.float32
BF16 = jnp.bfloat16

EPS = 1e-6
NEG_INF = -1e30
HEAD_DIM = 128
N_KV_HEADS = 4
DILATIONS = (1, 4, 16)
ATT_BLK = 128
S5_C = 16
S5_P = 64
S5_GB = 16
S5_CH = S5_GB * S5_C
S5_W = S5_GB * S5_P
S5_UNROLL = 4
N_DEV = 8

ADAM_LR = 0.001
ADAM_B1 = 0.9
ADAM_B2 = 0.999
ADAM_EPS = 1e-08
ADAM_WD = 0.01
ADAM_STEP = 10

VMEM_LIMIT_BYTES = 56 * 1024 * 1024
MM_TILE = 1024
MM_TILE_NARROW = 512
MM_DEPTH = 2816
MESH = pl.DeviceIdType.MESH
ANY = pl.BlockSpec(memory_space=pl.ANY)


def _tile(n, pref, align=128):
    t = (min(pref, n) // align) * align
    while t >= align:
        if n % t == 0:
            return t
        t -= align
    return n


def _params(sem):
    return pltpu.CompilerParams(dimension_semantics=sem, vmem_limit_bytes=VMEM_LIMIT_BYTES)


def _sigmoid(x):
    return 1.0 / (1.0 + jnp.exp(-x))


NN = (((1,), (0,)), ((), ()))
NT = (((1,), (1,)), ((), ()))
TN = (((0,), (0,)), ((), ()))


def _dot(a, b, dims=NN):
    return lax.dot_general(a, b, dims, preferred_element_type=F32)


def _matmul(name, grid, ins, in_specs, products, dims, out_shapes, out_specs, acc_shapes, epilogue):
    n_in, n_out, nk = len(ins), len(out_shapes), grid[2]

    def body(*refs):
        in_refs = refs[:n_in]
        out_refs = refs[n_in:n_in + n_out]
        acc_refs = refs[n_in + n_out:]

        def prods():
            vals = [None] * len(acc_shapes)
            for ai, bi, ci in products:
                d = _dot(in_refs[ai][...].astype(BF16), in_refs[bi][...].astype(BF16), dims)
                vals[ci] = d if vals[ci] is None else vals[ci] + d
            return vals

        if nk == 1:
            epilogue(in_refs, out_refs, prods())
        else:
            k = pl.program_id(2)

            @pl.when(k == 0)
            def _():
                for a in acc_refs:
                    a[...] = jnp.zeros_like(a)

            for a, v in zip(acc_refs, prods()):
                a[...] += v

            @pl.when(k == nk - 1)
            def _():
                epilogue(in_refs, out_refs, [a[...] for a in acc_refs])

    scratch = [] if nk == 1 else [pltpu.VMEM(s, F32) for s in acc_shapes]
    return pl.pallas_call(
        body, grid=grid, in_specs=in_specs, out_specs=out_specs, out_shape=out_shapes,
        scratch_shapes=scratch, compiler_params=_params(("parallel", "parallel", "arbitrary")),
        name=name)(*ins)


def _mm_dual_fwd(name, a, w, res, kind):
    T, K = a.shape
    N = w.shape[1] // 2
    tm, tn = _tile(T, MM_TILE), _tile(N, MM_TILE_NARROW)
    nj = N // tn
    grid = (T // tm, nj, 1)
    ins = [a, w, w]
    in_specs = [pl.BlockSpec((tm, K), lambda i, j, k: (i, 0)),
                pl.BlockSpec((K, tn), lambda i, j, k: (0, j)),
                pl.BlockSpec((K, tn), lambda i, j, k: (0, j + nj))]
    pair_spec = pl.BlockSpec((2, tm, tn), lambda i, j, k: (0, i, j))
    tile_spec = pl.BlockSpec((tm, tn), lambda i, j, k: (i, j))
    if kind == "glu":
        ins.append(res)
        in_specs.append(tile_spec)

        def epilogue(in_refs, out_refs, accs):
            val, gate = accs
            out_refs[0][...] = in_refs[3][...] + val * _sigmoid(gate)
            out_refs[1][0] = val.astype(BF16)
            out_refs[1][1] = gate.astype(BF16)

        out_shapes = [jax.ShapeDtypeStruct((T, N), F32), jax.ShapeDtypeStruct((2, T, N), BF16)]
        out_specs = [tile_spec, pair_spec]
    else:
        def epilogue(in_refs, out_refs, accs):
            g, u = accs
            out_refs[0][0] = g.astype(BF16)
            out_refs[0][1] = u.astype(BF16)
            out_refs[1][...] = (g * _sigmoid(g) * u).astype(BF16)

        out_shapes = [jax.ShapeDtypeStruct((2, T, N), BF16), jax.ShapeDtypeStruct((T, N), BF16)]
        out_specs = [pair_spec, tile_spec]
    return _matmul(name, grid, ins, in_specs, [(0, 1, 0), (0, 2, 1)], NN, out_shapes, out_specs,
                   [(tm, tn), (tm, tn)], epilogue)


def _mm_kv(name, a, w):
    T, K = a.shape
    N = w.shape[1] // 2
    tm, tn = _tile(T, MM_TILE), _tile(N, MM_TILE_NARROW)
    nj = N // tn
    tile_spec = pl.BlockSpec((tm, tn), lambda i, j, k: (i, j))

    def epilogue(in_refs, out_refs, accs):
        out_refs[0][...] = accs[0]
        out_refs[1][...] = accs[1]

    return _matmul(name, (T // tm, nj, 1), [a, w, w],
                   [pl.BlockSpec((tm, K), lambda i, j, k: (i, 0)),
                    pl.BlockSpec((K, tn), lambda i, j, k: (0, j)),
                    pl.BlockSpec((K, tn), lambda i, j, k: (0, j + nj))],
                   [(0, 1, 0), (0, 2, 1)], NN,
                   [jax.ShapeDtypeStruct((T, N), F32)] * 2, [tile_spec, tile_spec],
                   [(tm, tn), (tm, tn)], epilogue)


def _mm_nn(name, a, w, res=None, out_dtype=F32):
    T, K = a.shape
    N = w.shape[1]
    tk = _tile(K, MM_DEPTH)
    tm, tn = _tile(T, MM_TILE), _tile(N, MM_TILE if tk == K else MM_TILE_NARROW)
    grid = (T // tm, N // tn, K // tk)
    tile_spec = pl.BlockSpec((tm, tn), lambda i, j, k: (i, j))
    ins = [a, w]
    in_specs = [pl.BlockSpec((tm, tk), lambda i, j, k: (i, k)),
                pl.BlockSpec((tk, tn), lambda i, j, k: (k, j))]
    if res is not None:
        ins.append(res)
        in_specs.append(tile_spec)

    def epilogue(in_refs, out_refs, accs):
        v = accs[0]
        if res is not None:
            v = v + in_refs[2][...]
        out_refs[0][...] = v.astype(out_dtype)

    return _matmul(name, grid, ins, in_specs, [(0, 1, 0)], NN,
                   [jax.ShapeDtypeStruct((T, N), out_dtype)], [tile_spec], [(tm, tn)], epilogue)[0]


def _dep_operand(ins, in_specs, dep):
    if dep is not None:
        ins.append(dep)
        in_specs.append(pl.BlockSpec((8, 128), lambda *_: (0, 0)))


def _mm_nt(name, a_list, w, out_dtype=F32, dep=None):
    T, Np = a_list[0].shape
    Ko = w.shape[0]
    n_parts = len(a_list)
    wide_a = a_list[0].dtype != BF16
    tm, tn, tk = _tile(T, MM_TILE_NARROW if wide_a else MM_TILE), _tile(Ko, MM_TILE), _tile(Np, MM_DEPTH)
    nkp = Np // tk
    grid = (T // tm, Ko // tn, nkp)
    ins = list(a_list) + [w] * n_parts
    in_specs = [pl.BlockSpec((tm, tk), lambda i, j, k: (i, k)) for _ in a_list]
    in_specs += [pl.BlockSpec((tn, tk), functools.partial(lambda i, j, k, p: (j, p * nkp + k), p=p))
                 for p in range(n_parts)]
    products = [(p, n_parts + p, 0) for p in range(n_parts)]
    _dep_operand(ins, in_specs, dep)

    def epilogue(in_refs, out_refs, accs):
        out_refs[0][...] = accs[0].astype(out_dtype)

    return _matmul(name, grid, ins, in_specs, products, NT,
                   [jax.ShapeDtypeStruct((T, Ko), out_dtype)],
                   [pl.BlockSpec((tm, tn), lambda i, j, k: (i, j))], [(tm, tn)], epilogue)[0]


def _mm_nt_pair(name, a3, w):
    _, T, N = a3.shape
    Ko = w.shape[0]
    tm, tn, tk = _tile(T, MM_TILE), _tile(Ko, MM_TILE), _tile(N, MM_DEPTH)
    nkh = N // tk
    grid = (T // tm, Ko // tn, 2 * nkh)

    def epilogue(in_refs, out_refs, accs):
        out_refs[0][...] = accs[0]

    return _matmul(name, grid, [a3, w],
                   [pl.BlockSpec((None, tm, tk), lambda i, j, k: (k // nkh, i, k % nkh)),
                    pl.BlockSpec((tn, tk), lambda i, j, k: (j, k))],
                   [(0, 1, 0)], NT, [jax.ShapeDtypeStruct((T, Ko), F32)],
                   [pl.BlockSpec((tm, tn), lambda i, j, k: (i, j))], [(tm, tn)], epilogue)[0]


def _mm_nt_ffn_bwd(name, dx, w_out, gu, dep=None):
    T, D = dx.shape
    Fh = w_out.shape[0]
    tm, tn = _tile(T, MM_TILE), _tile(Fh, MM_TILE_NARROW)
    sub = _tile(tn, 256)
    n_dep = 0 if dep is None else 1

    def body(dx_ref, w_ref, gu_ref, *rest):
        out_ref = rest[n_dep]
        a = dx_ref[...]
        for c0 in range(0, tn, sub):
            da = _dot(a, w_ref[c0:c0 + sub, :], NT)
            g = gu_ref[0, :, c0:c0 + sub].astype(F32)
            u = gu_ref[1, :, c0:c0 + sub].astype(F32)
            s = _sigmoid(g)
            out_ref[0, :, c0:c0 + sub] = (da * u * (s * (1.0 + g * (1.0 - s)))).astype(BF16)
            out_ref[1, :, c0:c0 + sub] = (da * (g * s)).astype(BF16)

    pair_spec = pl.BlockSpec((2, tm, tn), lambda i, j: (0, i, j))
    ins = [dx, w_out, gu]
    in_specs = [pl.BlockSpec((tm, D), lambda i, j: (i, 0)),
                pl.BlockSpec((tn, D), lambda i, j: (j, 0)),
                pair_spec]
    _dep_operand(ins, in_specs, dep)
    return pl.pallas_call(
        body, grid=(T // tm, Fh // tn), in_specs=in_specs, out_specs=pair_spec,
        out_shape=jax.ShapeDtypeStruct((2, T, Fh), BF16),
        compiler_params=_params(("parallel", "parallel")), name=name)(*ins)


def _mm_tn(name, a, d):
    T, Ko = a.shape
    N = d.shape[1]
    to, tn, tk = _tile(Ko, MM_TILE_NARROW), _tile(N, MM_TILE if d.dtype == BF16 else MM_TILE_NARROW), T
    grid = (Ko // to, N // tn, T // tk)

    def epilogue(in_refs, out_refs, accs):
        out_refs[0][...] = accs[0].astype(BF16)

    return _matmul(name, grid, [a, d],
                   [pl.BlockSpec((tk, to), lambda i, j, k: (k, i)),
                    pl.BlockSpec((tk, tn), lambda i, j, k: (k, j))],
                   [(0, 1, 0)], TN, [jax.ShapeDtypeStruct((Ko, N), BF16)],
                   [pl.BlockSpec((to, tn), lambda i, j, k: (i, j))], [(to, tn)], epilogue)[0]


def _mm_tn_pair(name, a, d3):
    T, Ko = a.shape
    N = d3.shape[2]
    to, tn, tk = _tile(Ko, MM_TILE), _tile(N, MM_TILE_NARROW), T
    njh = N // tn
    grid = (Ko // to, 2 * njh, T // tk)

    def epilogue(in_refs, out_refs, accs):
        out_refs[0][...] = accs[0].astype(BF16)

    return _matmul(name, grid, [a, d3],
                   [pl.BlockSpec((tk, to), lambda i, j, k: (k, i)),
                    pl.BlockSpec((None, tk, tn), lambda i, j, k: (j // njh, k, j % njh))],
                   [(0, 1, 0)], TN, [jax.ShapeDtypeStruct((Ko, 2 * N), BF16)],
                   [pl.BlockSpec((to, tn), lambda i, j, k: (i, j))], [(to, tn)], epilogue)[0]


def _rms_fwd(name, x, gains, dtypes, dep=None):
    T, D = x.shape
    n = len(gains)
    tr = _tile(T, 512, 8)
    n_dep = 0 if dep is None else 1

    def body(x_ref, *refs):
        xv = x_ref[...]
        xr = xv * lax.rsqrt(jnp.mean(xv * xv, axis=-1, keepdims=True) + EPS)
        for g_ref, o_ref in zip(refs[:n], refs[n + n_dep:]):
            o_ref[...] = (xr * g_ref[...]).astype(o_ref.dtype)

    row = pl.BlockSpec((tr, D), lambda i: (i, 0))
    vec = pl.BlockSpec((1, D), lambda i: (0, 0))
    ins, in_specs = [x, *gains], [row] + [vec] * n
    _dep_operand(ins, in_specs, dep)
    return pl.pallas_call(
        body, grid=(T // tr,), in_specs=in_specs, out_specs=[row] * n,
        out_shape=[jax.ShapeDtypeStruct((T, D), dt) for dt in dtypes],
        compiler_params=_params(("parallel",)), name=name)(*ins)


def _rms_bwd(name, x, dres, gains, dhs):
    T, D = x.shape
    n = len(gains)
    tr = _tile(T, 256, 8)

    def body(x_ref, dres_ref, *refs):
        g_refs, dh_refs = refs[:n], refs[n:2 * n]
        dx_ref, dxb_ref = refs[2 * n], refs[2 * n + 1]
        dg_refs = refs[2 * n + 2:]
        xv = x_ref[...]
        r = lax.rsqrt(jnp.mean(xv * xv, axis=-1, keepdims=True) + EPS)
        xr = xv * r
        w = None
        for g_ref, dh_ref, dg_ref in zip(g_refs, dh_refs, dg_refs):
            dh = dh_ref[...].astype(F32)

            @pl.when(pl.program_id(0) == 0)
            def _():
                dg_ref[...] = jnp.zeros_like(dg_ref)

            dg_ref[...] += jnp.sum(dh * xr, axis=0, keepdims=True)
            wi = dh * g_ref[...]
            w = wi if w is None else w + wi
        dx = dres_ref[...] + r * (w - xr * jnp.mean(w * xr, axis=-1, keepdims=True))
        dx_ref[...] = dx
        dxb_ref[...] = dx.astype(BF16)

    row = pl.BlockSpec((tr, D), lambda i: (i, 0))
    vec = pl.BlockSpec((1, D), lambda i: (0, 0))
    outs = pl.pallas_call(
        body, grid=(T // tr,), in_specs=[row, row] + [vec] * n + [row] * n,
        out_specs=[row, row] + [vec] * n,
        out_shape=[jax.ShapeDtypeStruct((T, D), F32), jax.ShapeDtypeStruct((T, D), BF16)]
        + [jax.ShapeDtypeStruct((1, D), F32)] * n,
        compiler_params=_params(("arbitrary",)), name=name)(x, dres, *gains, *dhs)
    return outs[0], outs[1], outs[2:]


def _loss_head(x, gain, target):
    T, D = x.shape
    tr = _tile(T, 256, 8)

    def body(x_ref, g_ref, t_ref, loss_ref, dx_ref, dxb_ref, dg_ref):
        @pl.when(pl.program_id(0) == 0)
        def _():
            loss_ref[...] = jnp.zeros_like(loss_ref)
            dg_ref[...] = jnp.zeros_like(dg_ref)

        xv = x_ref[...]
        r = lax.rsqrt(jnp.mean(xv * xv, axis=-1, keepdims=True) + EPS)
        xr = xv * r
        err = xr * g_ref[...] - t_ref[...]
        part = jnp.sum(jnp.sum(err * err, axis=-1, keepdims=True), axis=0, keepdims=True) * (0.5 / D)
        loss_ref[...] += jnp.broadcast_to(part, loss_ref.shape)
        dy = err * (1.0 / D)
        dg_ref[...] += jnp.sum(dy * xr, axis=0, keepdims=True)
        w = dy * g_ref[...]
        dx = r * (w - xr * jnp.mean(w * xr, axis=-1, keepdims=True))
        dx_ref[...] = dx
        dxb_ref[...] = dx.astype(BF16)

    row = pl.BlockSpec((tr, D), lambda i: (i, 0))
    vec = pl.BlockSpec((1, D), lambda i: (0, 0))
    return pl.pallas_call(
        body, grid=(T // tr,), in_specs=[row, vec, row],
        out_specs=[pl.BlockSpec((8, 128), lambda i: (0, 0)), row, row, vec],
        out_shape=[jax.ShapeDtypeStruct((8, 128), F32), jax.ShapeDtypeStruct((T, D), F32),
                   jax.ShapeDtypeStruct((T, D), BF16), jax.ShapeDtypeStruct((1, D), F32)],
        compiler_params=_params(("arbitrary",)), name="loss_head")(x, gain, target)


def _glu_bwd(dmix, vg, dep=None):
    T, N = dmix.shape
    tr, tc = _tile(T, 512, 8), _tile(N, 1024)
    n_dep = 0 if dep is None else 1

    def body(d_ref, vg_ref, *refs):
        o_ref = refs[n_dep]
        d = d_ref[...]
        val = vg_ref[0].astype(F32)
        s = _sigmoid(vg_ref[1].astype(F32))
        o_ref[0] = (d * s).astype(BF16)
        o_ref[1] = (d * val * s * (1.0 - s)).astype(BF16)

    pair = pl.BlockSpec((2, tr, tc), lambda i, j: (0, i, j))
    ins, in_specs = [dmix, vg], [pl.BlockSpec((tr, tc), lambda i, j: (i, j)), pair]
    _dep_operand(ins, in_specs, dep)
    return pl.pallas_call(
        body, grid=(T // tr, N // tc), in_specs=in_specs,
        out_specs=pair, out_shape=jax.ShapeDtypeStruct((2, T, N), BF16),
        compiler_params=_params(("parallel", "parallel")), name="glu_bwd")(*ins)


def _to_state_tiles(x_ref, s, val):
    tc = val.shape[0]
    for j in range(S5_W // 128):
        x_ref[s, pl.ds(j, tc, stride=8), :] = val[:, 128 * j:128 * (j + 1)]


def _from_state_tiles(x_ref, s, tc):
    return jnp.concatenate([x_ref[s, pl.ds(j, tc, stride=8), :] for j in range(S5_W // 128)], axis=1)


def _s5_scan_fwd(xr_ref, xi_ref, ar_ref, ai_ref, cr_ref, ci_ref, tc, nblk):
    a = [(ar_ref[s], ai_ref[s]) for s in range(nblk)]

    def step(i, carry):
        carry = list(carry)
        for uu in range(S5_UNROLL):
            r0 = pl.multiple_of((i * S5_UNROLL + uu) * 8, 8)
            for s in range(nblk):
                cr, ci = carry[2 * s], carry[2 * s + 1]
                a_r, a_i = a[s]
                xr = a_r * cr - a_i * ci + xr_ref[s, pl.ds(r0, 8), :]
                xi = a_r * ci + a_i * cr + xi_ref[s, pl.ds(r0, 8), :]
                xr_ref[s, pl.ds(r0, 8), :] = xr
                xi_ref[s, pl.ds(r0, 8), :] = xi
                carry[2 * s], carry[2 * s + 1] = xr, xi
        return tuple(carry)

    init = []
    for s in range(nblk):
        init += [cr_ref[s], ci_ref[s]]
    out = lax.fori_loop(0, tc // S5_UNROLL, step, tuple(init))
    for s in range(nblk):
        cr_ref[s] = out[2 * s]
        ci_ref[s] = out[2 * s + 1]


def _s5_scan_bwd(lr_ref, li_ref, xr_ref, xi_ref, h_ref, ar_ref, ai_ref, cr_ref, ci_ref,
                 accr_ref, acci_ref, tc, nblk):
    a = [(ar_ref[s], ai_ref[s]) for s in range(nblk)]

    def one(s, r0, prev_r, prev_i, st):
        c_r, c_i, d_r, d_i = st
        a_r, a_i = a[s]
        l_r = lr_ref[s, pl.ds(r0, 8), :] + a_r * c_r + a_i * c_i
        l_i = li_ref[s, pl.ds(r0, 8), :] + a_r * c_i - a_i * c_r
        lr_ref[s, pl.ds(r0, 8), :] = l_r
        li_ref[s, pl.ds(r0, 8), :] = l_i
        return [l_r, l_i, d_r + l_r * prev_r + l_i * prev_i, d_i - l_r * prev_i + l_i * prev_r]

    def step(i, carry):
        carry = list(carry)
        for uu in range(S5_UNROLL):
            t = tc - 1 - (i * S5_UNROLL + uu)
            r0 = pl.multiple_of(t * 8, 8)
            p0 = pl.multiple_of((t - 1) * 8, 8)
            for s in range(nblk):
                carry[4 * s:4 * s + 4] = one(s, r0, xr_ref[s, pl.ds(p0, 8), :], xi_ref[s, pl.ds(p0, 8), :],
                                             carry[4 * s:4 * s + 4])
        return tuple(carry)

    init = []
    for s in range(nblk):
        init += [cr_ref[s], ci_ref[s], accr_ref[s], acci_ref[s]]
    carry = list(lax.fori_loop(0, tc // S5_UNROLL - 1, step, tuple(init)))
    for t in range(S5_UNROLL - 1, -1, -1):
        for s in range(nblk):
            if t > 0:
                prev_r, prev_i = xr_ref[s, 8 * (t - 1):8 * t, :], xi_ref[s, 8 * (t - 1):8 * t, :]
            else:
                prev_r, prev_i = h_ref[0, s], h_ref[1, s]
            carry[4 * s:4 * s + 4] = one(s, 8 * t, prev_r, prev_i, carry[4 * s:4 * s + 4])
    for s in range(nblk):
        cr_ref[s], ci_ref[s], accr_ref[s], acci_ref[s] = carry[4 * s:4 * s + 4]


def _gelu(y):
    k = math.sqrt(2.0 / math.pi)
    return 0.5 * y * (1.0 + jnp.tanh(k * (y + 0.044715 * (y * y * y))))


def _gelu_grad(y):
    k = math.sqrt(2.0 / math.pi)
    t = jnp.tanh(k * (y + 0.044715 * (y * y * y)))
    return 0.5 * (1.0 + t) + 0.5 * y * (1.0 - t * t) * (k * (1.0 + 3.0 * 0.044715 * (y * y)))


def _s5_specs(tc, nch, sbk, rev):
    def ch(c):
        return nch - 1 - c if rev else c

    return dict(
        act=pl.BlockSpec((tc, sbk * S5_CH), lambda i, c: (ch(c), i)),
        bb=pl.BlockSpec((2, sbk, S5_CH, 128), lambda i, c: (0, i, 0, 0)),
        cc=pl.BlockSpec((2, sbk, S5_P, S5_CH), lambda i, c: (0, i, 0, 0)),
        a=pl.BlockSpec((sbk, 8, 128), lambda i, c: (i, 0, 0)),
        d=pl.BlockSpec((1, sbk * S5_CH), lambda i, c: (0, i)),
        h=pl.BlockSpec((None, 2, sbk, 8, 128), lambda i, c: (ch(c), 0, i, 0, 0)),
    )


def _s5_blocks(nb, pref):
    return max(b for b in range(1, pref + 1) if nb % b == 0)


def _s5_group_masks():
    rb = lax.broadcasted_iota(jnp.int32, (S5_CH, S5_W), 0) // S5_C
    qb = lax.broadcasted_iota(jnp.int32, (S5_CH, S5_W), 1) // S5_P
    qc = lax.broadcasted_iota(jnp.int32, (S5_W, S5_CH), 0) // S5_P
    rc = lax.broadcasted_iota(jnp.int32, (S5_W, S5_CH), 1) // S5_C
    return rb == qb, qc == rc


def _s5_expand(bb_ref, cc_ref, bbd, ccd, sbk):
    mask_b, mask_c = _s5_group_masks()
    for k in range(2):
        for s in range(sbk):
            bbd[k, s] = jnp.where(mask_b, jnp.tile(bb_ref[k, s], (1, S5_W // 128)), 0).astype(BF16)
            ccd[k, s] = jnp.where(mask_c, jnp.tile(cc_ref[k, s], (S5_GB, 1)), 0).astype(BF16)


def _s5_fwd(u, bb2, cc2, a_re, a_im, d_skip):
    T, D = u.shape
    nb = D // S5_CH
    sbk = _s5_blocks(nb, 4)
    tc = _tile(T, 512, 8)
    nch = T // tc
    sp = _s5_specs(tc, nch, sbk, False)

    def body(u_ref, bb_ref, cc_ref, ar_ref, ai_ref, d_ref, z_ref, h_ref, xr, xi, cr, ci, bbd, ccd):
        @pl.when(pl.program_id(1) == 0)
        def _():
            cr[...] = jnp.zeros_like(cr)
            ci[...] = jnp.zeros_like(ci)
            _s5_expand(bb_ref, cc_ref, bbd, ccd, sbk)

        h_ref[0] = cr[...]
        h_ref[1] = ci[...]
        for s in range(sbk):
            ub = u_ref[:, s * S5_CH:(s + 1) * S5_CH].astype(BF16)
            _to_state_tiles(xr, s, _dot(ub, bbd[0, s]))
            _to_state_tiles(xi, s, _dot(ub, bbd[1, s]))
        _s5_scan_fwd(xr, xi, ar_ref, ai_ref, cr, ci, tc, sbk)
        for s in range(sbk):
            cols = slice(s * S5_CH, (s + 1) * S5_CH)
            y = (_dot(_from_state_tiles(xr, s, tc).astype(BF16), ccd[0, s])
                 - _dot(_from_state_tiles(xi, s, tc).astype(BF16), ccd[1, s])
                 + d_ref[:, cols] * u_ref[:, cols])
            z_ref[:, cols] = _gelu(y).astype(BF16)

    tiles = pltpu.VMEM((sbk, tc * 8, 128), F32)
    carry = pltpu.VMEM((sbk, 8, 128), F32)
    return pl.pallas_call(
        body, grid=(nb // sbk, nch),
        in_specs=[sp["act"], sp["bb"], sp["cc"], sp["a"], sp["a"], sp["d"]],
        out_specs=[sp["act"], sp["h"]],
        out_shape=[jax.ShapeDtypeStruct((T, D), BF16), jax.ShapeDtypeStruct((nch, 2, nb, 8, 128), F32)],
        scratch_shapes=[tiles, tiles, carry, carry, pltpu.VMEM((2, sbk, S5_CH, S5_W), BF16),
                        pltpu.VMEM((2, sbk, S5_W, S5_CH), BF16)],
        compiler_params=_params(("parallel", "arbitrary")), name="s5_fwd",
    )(u, bb2, cc2, a_re, a_im, d_skip)


def _s5_bwd(u, dz, h0, bb2, cc2, a_re, a_im, d_skip, dep=None):
    T, D = u.shape
    nb = D // S5_CH
    sbk = _s5_blocks(nb, 2)
    tc = _tile(T, 512, 8)
    nch = T // tc
    sp = _s5_specs(tc, nch, sbk, True)

    n_dep = 0 if dep is None else 1

    def body(u_ref, dz_ref, h_ref, bb_ref, cc_ref, ar_ref, ai_ref, d_ref, *rest):
        (du_ref, dd_ref, dar_ref, dai_ref, dbb_ref, dcc_ref,
         xr, xi, lr, li, fr, fi, br, bi, accr, acci, bbd, ccd, dbbd, dccd) = rest[n_dep:]
        c = pl.program_id(1)

        @pl.when(c == 0)
        def _():
            for ref in (br, bi, accr, acci, dd_ref, dbbd, dccd):
                ref[...] = jnp.zeros_like(ref)
            _s5_expand(bb_ref, cc_ref, bbd, ccd, sbk)

        for s in range(sbk):
            ub = u_ref[:, s * S5_CH:(s + 1) * S5_CH].astype(BF16)
            _to_state_tiles(xr, s, _dot(ub, bbd[0, s]))
            _to_state_tiles(xi, s, _dot(ub, bbd[1, s]))
        fr[...] = h_ref[0]
        fi[...] = h_ref[1]
        _s5_scan_fwd(xr, xi, ar_ref, ai_ref, fr, fi, tc, sbk)
        for s in range(sbk):
            cols = slice(s * S5_CH, (s + 1) * S5_CH)
            uv = u_ref[:, cols]
            xrb = _from_state_tiles(xr, s, tc).astype(BF16)
            xib = _from_state_tiles(xi, s, tc).astype(BF16)
            dsk = d_ref[:, cols]
            y = _dot(xrb, ccd[0, s]) - _dot(xib, ccd[1, s]) + dsk * uv
            dy = dz_ref[:, cols] * _gelu_grad(y)
            dd_ref[:, cols] += jnp.sum(dy * uv, axis=0, keepdims=True)
            dyb = dy.astype(BF16)
            dccd[0, s] += _dot(xrb, dyb, TN)
            dccd[1, s] += _dot(xib, dyb, TN)
            _to_state_tiles(lr, s, _dot(dyb, ccd[0, s], NT))
            _to_state_tiles(li, s, -_dot(dyb, ccd[1, s], NT))
            du_ref[:, cols] = dy * dsk
        _s5_scan_bwd(lr, li, xr, xi, h_ref, ar_ref, ai_ref, br, bi, accr, acci, tc, sbk)
        for s in range(sbk):
            cols = slice(s * S5_CH, (s + 1) * S5_CH)
            ub = u_ref[:, cols].astype(BF16)
            lrb = _from_state_tiles(lr, s, tc).astype(BF16)
            lib = _from_state_tiles(li, s, tc).astype(BF16)
            dbbd[0, s] += _dot(ub, lrb, TN)
            dbbd[1, s] += _dot(ub, lib, TN)
            du_ref[:, cols] += _dot(lrb, bbd[0, s], NT) + _dot(lib, bbd[1, s], NT)

        @pl.when(c == nch - 1)
        def _():
            dar_ref[...] = accr[...]
            dai_ref[...] = acci[...]
            mask_b, mask_c = _s5_group_masks()
            for k in range(2):
                for s in range(sbk):
                    mb = jnp.where(mask_b, dbbd[k, s], 0.0)
                    fold = functools.reduce(
                        lambda a, b: a + b, [mb[:, 128 * j:128 * (j + 1)] for j in range(S5_W // 128)])
                    dbb_ref[k, s] = fold + pltpu.roll(fold, S5_P, 1)
                    mc = jnp.where(mask_c, dccd[k, s], 0.0)
                    dcc_ref[k, s] = functools.reduce(
                        lambda a, b: a + b, [mc[S5_P * j:S5_P * (j + 1), :] for j in range(S5_GB)])

    tiles = pltpu.VMEM((sbk, tc * 8, 128), F32)
    carry = pltpu.VMEM((sbk, 8, 128), F32)
    ins = [u, dz, h0, bb2, cc2, a_re, a_im, d_skip]
    in_specs = [sp["act"], sp["act"], sp["h"], sp["bb"], sp["cc"], sp["a"], sp["a"], sp["d"]]
    _dep_operand(ins, in_specs, dep)
    return pl.pallas_call(
        body, grid=(nb // sbk, nch),
        in_specs=in_specs,
        out_specs=[sp["act"], sp["d"], sp["a"], sp["a"], sp["bb"], sp["cc"]],
        out_shape=[jax.ShapeDtypeStruct((T, D), F32), jax.ShapeDtypeStruct((1, D), F32),
                   jax.ShapeDtypeStruct((nb, 8, 128), F32), jax.ShapeDtypeStruct((nb, 8, 128), F32),
                   jax.ShapeDtypeStruct((2, nb, S5_CH, 128), F32), jax.ShapeDtypeStruct((2, nb, S5_P, S5_CH), F32)],
        scratch_shapes=[tiles, tiles, tiles, tiles, carry, carry, carry, carry, carry, carry,
                        pltpu.VMEM((2, sbk, S5_CH, S5_W), BF16), pltpu.VMEM((2, sbk, S5_W, S5_CH), BF16),
                        pltpu.VMEM((2, sbk, S5_CH, S5_W), F32), pltpu.VMEM((2, sbk, S5_W, S5_CH), F32)],
        compiler_params=_params(("parallel", "arbitrary")), name="s5_bwd",
    )(*ins)


def _s5_disc(lr, li, ldt):
    dt = jnp.exp(ldt)
    mag = jnp.exp(lr * dt)
    ang = li * dt
    cs, sn = jnp.cos(ang), jnp.sin(ang)
    lbr, lbi = mag * cs, mag * sn
    nr = lbr - 1.0
    den = lr * lr + li * li
    f_re = (nr * lr + lbi * li) / den
    f_im = (lbi * lr - nr * li) / den
    return dt, mag, cs, sn, lbr, lbi, nr, den, f_re, f_im


def _s5_param_fwd(lr, li, ldt, bt_re, bt_im):
    c, g, p = bt_re.shape

    def body(lr_ref, li_ref, ldt_ref, br_ref, bi_ref, lbr_ref, lbi_ref, bbr_ref, bbi_ref):
        _, _, _, _, lbr, lbi, _, _, f_re, f_im = _s5_disc(lr_ref[...], li_ref[...], ldt_ref[...])
        lbr_ref[...] = lbr
        lbi_ref[...] = lbi
        for ch in range(c):
            b_r, b_i = br_ref[ch], bi_ref[ch]
            bbr_ref[ch] = f_re * b_r - f_im * b_i
            bbi_ref[ch] = f_re * b_i + f_im * b_r

    gp = jax.ShapeDtypeStruct((g, p), F32)
    cgp = jax.ShapeDtypeStruct((c, g, p), F32)
    return pl.pallas_call(body, out_shape=[gp, gp, cgp, cgp], name="s5_param_fwd")(lr, li, ldt, bt_re, bt_im)


def _s5_param_bwd(lr, li, ldt, bt_re, bt_im, dlbr, dlbi, dbbr, dbbi):
    c, g, p = bt_re.shape

    def body(lr_ref, li_ref, ldt_ref, br_ref, bi_ref, dlbr_ref, dlbi_ref, dbbr_ref, dbbi_ref,
             dlr_ref, dli_ref, dldt_ref, dbr_ref, dbi_ref):
        l_r, l_i = lr_ref[...], li_ref[...]
        dt, mag, cs, sn, lbr, lbi, nr, den, f_re, f_im = _s5_disc(l_r, l_i, ldt_ref[...])
        dfr = jnp.zeros_like(l_r)
        dfi = jnp.zeros_like(l_r)
        for ch in range(c):
            b_r, b_i = br_ref[ch], bi_ref[ch]
            g_r, g_i = dbbr_ref[ch], dbbi_ref[ch]
            dbr_ref[ch] = f_re * g_r + f_im * g_i
            dbi_ref[ch] = f_re * g_i - f_im * g_r
            dfr = dfr + g_r * b_r + g_i * b_i
            dfi = dfi + g_i * b_r - g_r * b_i
        inv = 1.0 / den
        d_nr = (dfr * l_r - dfi * l_i) * inv
        d_lbi = (dfr * l_i + dfi * l_r) * inv + dlbi_ref[...]
        d_lbr = d_nr + dlbr_ref[...]
        d_den = -(dfr * f_re + dfi * f_im) * inv
        d_mag = d_lbr * cs + d_lbi * sn
        d_ang = d_lbi * lbr - d_lbr * lbi
        dlr_ref[...] = (dfr * nr + dfi * lbi) * inv + 2.0 * d_den * l_r + d_mag * mag * dt
        dli_ref[...] = (dfr * lbi - dfi * nr) * inv + 2.0 * d_den * l_i + d_ang * dt
        dldt_ref[...] = jnp.sum(d_mag * mag * l_r + d_ang * l_i, axis=1, keepdims=True) * dt

    gp = jax.ShapeDtypeStruct((g, p), F32)
    cgp = jax.ShapeDtypeStruct((c, g, p), F32)
    return pl.pallas_call(body, out_shape=[gp, gp, jax.ShapeDtypeStruct((g, 1), F32), cgp, cgp],
                          name="s5_param_bwd")(lr, li, ldt, bt_re, bt_im, dlbr, dlbi, dbbr, dbbi)


def _att_masks(rep, gb):
    rows = rep * ATT_BLK
    qi = lax.broadcasted_iota(jnp.int32, (rows, 2 * ATT_BLK), 0) % ATT_BLK
    si = lax.broadcasted_iota(jnp.int32, (rows, 2 * ATT_BLK), 1)
    prev = (si < ATT_BLK) & (si >= qi) & (gb > 0)
    cur = (si >= ATT_BLK) & (si - ATT_BLK <= qi)
    return prev | cur


def _att_rows(start, dil):
    return pl.ds(start, ATT_BLK) if dil == 1 else pl.ds(start, ATT_BLK, stride=dil)


def _att_plan(T, dil):
    span = ATT_BLK * dil
    sbr = max(span, min(T, 1024))
    return span, sbr, T // sbr


def _att_block(sb, i, sbr, span, dil):
    loc = (i // dil) * span + i % dil
    cur = sb * sbr + loc
    gb = sb * (sbr // span) + i // dil
    return loc, cur, jnp.where(gb > 0, cur - span, cur), gb


def _att_fwd(q, k, v, grp, dil):
    T = q.shape[0]
    H = q.shape[1] // HEAD_DIM // len(DILATIONS)
    rep = H // N_KV_HEADS
    span, sbr, nsb = _att_plan(T, dil)
    scale = HEAD_DIM ** -0.5

    def body(*refs):
        q_refs = refs[:rep]
        k_ref, v_ref, o_ref, l_ref, o_slab, l_slab = refs[rep:]
        sb = pl.program_id(1)

        def blk(i, _):
            loc, cur, prv, gb = _att_block(sb, i, sbr, span, dil)
            rows = _att_rows(loc, dil)
            qs = jnp.concatenate([r[rows, :] for r in q_refs], axis=0).astype(BF16)
            kcat = jnp.concatenate([k_ref[_att_rows(prv, dil), :], k_ref[_att_rows(cur, dil), :]], axis=0)
            vcat = jnp.concatenate([v_ref[_att_rows(prv, dil), :], v_ref[_att_rows(cur, dil), :]], axis=0)
            s = jnp.where(_att_masks(rep, gb), _dot(qs, kcat.astype(BF16), NT) * scale, NEG_INF)
            m = jnp.max(s, axis=-1, keepdims=True)
            p = jnp.exp(s - m)
            l = jnp.sum(p, axis=-1, keepdims=True)
            o = _dot(p.astype(BF16), vcat.astype(BF16)) / l
            lse = jnp.broadcast_to(m + jnp.log(l), (rep * ATT_BLK, HEAD_DIM))
            for j in range(rep):
                o_slab[j, rows, :] = o[j * ATT_BLK:(j + 1) * ATT_BLK]
                l_slab[j, rows, :] = lse[j * ATT_BLK:(j + 1) * ATT_BLK]
            return 0

        lax.fori_loop(0, sbr // ATT_BLK, blk, 0)
        for j in range(rep):
            o_ref[:, j * HEAD_DIM:(j + 1) * HEAD_DIM] = o_slab[j]
            l_ref[:, j * HEAD_DIM:(j + 1) * HEAD_DIM] = l_slab[j]

    qspecs = [pl.BlockSpec((sbr, HEAD_DIM), functools.partial(lambda h, s, j: (s, grp * H + h * rep + j), j=j))
              for j in range(rep)]
    kspec = pl.BlockSpec((T, HEAD_DIM), lambda h, s: (0, h))
    ospec = pl.BlockSpec((sbr, rep * HEAD_DIM), lambda h, s: (s, h))
    slab = pltpu.VMEM((rep, sbr, HEAD_DIM), F32)
    return pl.pallas_call(
        body, grid=(N_KV_HEADS, nsb), in_specs=qspecs + [kspec, kspec], out_specs=[ospec, ospec],
        out_shape=[jax.ShapeDtypeStruct((T, H * HEAD_DIM), F32)] * 2, scratch_shapes=[slab, slab],
        compiler_params=_params(("parallel", "arbitrary")), name=f"att_fwd_d{dil}",
    )(*([q] * rep), k, v)


def _att_combine(outs, lses):
    T, W = outs[0].shape
    ng = len(outs)
    tr, tcol = _tile(T, 512, 8), _tile(W, 512)

    def body(*refs):
        o_refs, l_refs = refs[:ng], refs[ng:2 * ng]
        ob_ref, lse_ref = refs[2 * ng:]
        ls = [r[...] for r in l_refs]
        m = functools.reduce(jnp.maximum, ls)
        es = [jnp.exp(l - m) for l in ls]
        den = functools.reduce(lambda a, b: a + b, es)
        num = functools.reduce(lambda a, b: a + b, [e * o[...] for e, o in zip(es, o_refs)])
        ob_ref[...] = (num / den).astype(BF16)
        lse_ref[...] = m + jnp.log(den)

    spec = pl.BlockSpec((tr, tcol), lambda i, j: (i, j))
    return pl.pallas_call(
        body, grid=(T // tr, W // tcol), in_specs=[spec] * (2 * ng), out_specs=[spec, spec],
        out_shape=[jax.ShapeDtypeStruct((T, W), BF16), jax.ShapeDtypeStruct((T, W), F32)],
        compiler_params=_params(("parallel", "parallel")), name="att_combine")(*outs, *lses)


STAT_LANE = HEAD_DIM // 2


def _att_stats(lse, o, do):
    T, W = lse.shape
    tr = _tile(T, 256, 16)

    def body(l_ref, o_ref, do_ref, s_ref):
        lane = lax.broadcasted_iota(jnp.int32, (tr, HEAD_DIM), 1)
        for h in range(W // HEAD_DIM):
            cols = slice(h * HEAD_DIM, (h + 1) * HEAD_DIM)
            delta = jnp.sum(do_ref[:, cols] * o_ref[:, cols].astype(F32), axis=-1, keepdims=True)
            s_ref[:, cols] = jnp.where(lane < STAT_LANE, l_ref[:, cols], delta)

    spec = pl.BlockSpec((tr, W), lambda i: (i, 0))
    return pl.pallas_call(
        body, grid=(T // tr,), in_specs=[spec] * 3, out_specs=spec,
        out_shape=jax.ShapeDtypeStruct((T, W), F32),
        compiler_params=_params(("parallel",)), name="att_stats")(lse, o, do)


def _att_bwd(q, k, v, do, stats, dq, grp, dil):
    T = q.shape[0]
    H = do.shape[1] // HEAD_DIM
    rep = H // N_KV_HEADS
    hs = min(rep, 2)
    span, sbr, nsb = _att_plan(T, dil)
    scale = HEAD_DIM ** -0.5

    def body(*refs):
        q_refs, do_refs, st_refs = refs[:hs], refs[hs + 2:2 * hs + 2], refs[2 * hs + 2:3 * hs + 2]
        k_ref, v_ref = refs[hs], refs[hs + 1]
        dq_ref, dk_ref, dv_ref, dq_slab = refs[3 * hs + 3:]
        sb = pl.program_id(2)

        @pl.when((pl.program_id(1) == 0) & (sb == 0))
        def _():
            dk_ref[...] = jnp.zeros_like(dk_ref)
            dv_ref[...] = jnp.zeros_like(dv_ref)

        def blk(i, _):
            loc, cur, prv, gb = _att_block(sb, i, sbr, span, dil)
            rows, kc, kp = _att_rows(loc, dil), _att_rows(cur, dil), _att_rows(prv, dil)
            qs = jnp.concatenate([r[rows, :] for r in q_refs], axis=0).astype(BF16)
            dos = jnp.concatenate([r[rows, :] for r in do_refs], axis=0).astype(BF16)
            st = jnp.concatenate([r[rows, :] for r in st_refs], axis=0)
            kcat = jnp.concatenate([k_ref[kp, :], k_ref[kc, :]], axis=0).astype(BF16)
            vcat = jnp.concatenate([v_ref[kp, :], v_ref[kc, :]], axis=0).astype(BF16)
            s = _dot(qs, kcat, NT) * scale
            p = jnp.where(_att_masks(hs, gb), jnp.exp(s - st[:, 0:1]), 0.0)
            dp = _dot(dos, vcat, NT)
            ds = (p * (dp - st[:, STAT_LANE:STAT_LANE + 1]) * scale).astype(BF16)
            dvc = _dot(p.astype(BF16), dos, TN)
            dkc = _dot(ds, qs, TN)
            dqs = _dot(ds, kcat)
            for j in range(hs):
                dq_slab[j, rows, :] = dqs[j * ATT_BLK:(j + 1) * ATT_BLK]
            dk_ref[kc, :] += dkc[ATT_BLK:]
            dv_ref[kc, :] += dvc[ATT_BLK:]

            @pl.when(gb > 0)
            def _():
                dk_ref[kp, :] += dkc[:ATT_BLK]
                dv_ref[kp, :] += dvc[:ATT_BLK]

            return 0

        lax.fori_loop(0, sbr // ATT_BLK, blk, 0)
        for j in range(hs):
            dq_ref[:, j * HEAD_DIM:(j + 1) * HEAD_DIM] = dq_slab[j].astype(BF16)

    def head_specs(col0):
        return [pl.BlockSpec((sbr, HEAD_DIM),
                             functools.partial(lambda h, f, s, j: (s, col0 + h * rep + f * hs + j), j=j))
                for j in range(hs)]

    kspec = pl.BlockSpec((T, HEAD_DIM), lambda h, f, s: (0, h))
    dqspec = pl.BlockSpec((sbr, hs * HEAD_DIM), lambda h, f, s: (s, (grp * H + h * rep) // hs + f))
    n_in = 3 * hs + 3
    return pl.pallas_call(
        body, grid=(N_KV_HEADS, rep // hs, nsb),
        in_specs=head_specs(grp * H) + [kspec, kspec] + head_specs(0) + head_specs(0) + [ANY],
        out_specs=[dqspec, kspec, kspec],
        out_shape=[jax.ShapeDtypeStruct(dq.shape, BF16),
                   jax.ShapeDtypeStruct((T, N_KV_HEADS * HEAD_DIM), F32),
                   jax.ShapeDtypeStruct((T, N_KV_HEADS * HEAD_DIM), F32)],
        scratch_shapes=[pltpu.VMEM((hs, sbr, HEAD_DIM), F32)],
        input_output_aliases={n_in - 1: 0},
        compiler_params=_params(("parallel", "arbitrary", "arbitrary")), name=f"att_bwd_d{dil}",
    )(*([q] * hs), k, v, *([do] * hs), *([stats] * hs), dq)


def _sum_kv(dks, dvs):
    T, W = dks[0].shape
    ng = len(dks)
    tr = _tile(T, 512, 8)

    def body(*refs):
        o_ref = refs[2 * ng]
        o_ref[0] = functools.reduce(lambda a, b: a + b, [r[...] for r in refs[:ng]]).astype(BF16)
        o_ref[1] = functools.reduce(lambda a, b: a + b, [r[...] for r in refs[ng:2 * ng]]).astype(BF16)

    spec = pl.BlockSpec((tr, W), lambda i: (i, 0))
    return pl.pallas_call(
        body, grid=(T // tr,), in_specs=[spec] * (2 * ng),
        out_specs=pl.BlockSpec((2, tr, W), lambda i: (0, i, 0)),
        out_shape=jax.ShapeDtypeStruct((2, T, W), BF16),
        compiler_params=_params(("parallel",)), name="sum_kv")(*dks, *dvs)


def _local_step(x, tgt, w, wts, ready, dep0=None):
    T, D = x.shape
    g = {}

    (u0,) = _rms_fwd("rms_a", x, [w["a_norm"]], [F32], dep=dep0)
    lbr, lbi, bbt_re, bbt_im = _s5_param_fwd(w["lam_re"], w["lam_im"], w["log_dt"], w["bt_re"], w["bt_im"])
    a_re, a_im = lbr.reshape(-1, 8, 128), lbi.reshape(-1, 8, 128)
    C, G, P = w["bt_re"].shape
    nb = G // S5_GB
    bb2 = jnp.stack([bbt_re, bbt_im]).transpose(0, 2, 1, 3).reshape(2, nb, S5_GB * C, P)
    bb2 = jnp.concatenate([bb2, bb2], axis=-1)
    cc2 = jnp.stack([w["c_re"], w["c_im"]]).reshape(2, nb, S5_GB * C, P).transpose(0, 1, 3, 2)
    z, h0 = _s5_fwd(u0, bb2, cc2, a_re, a_im, w["s5_d"])
    w_glu = wts("glu", z)
    x1, vg = _mm_dual_fwd("glu_fwd", z, w_glu, x, "glu")

    def ffn_fwd(xin, layer):
        (nrm,) = _rms_fwd(f"rms_f{layer}", xin, [w["ffn_norm"][layer:layer + 1]], [BF16])
        w_in = wts(f"win{layer}", nrm)
        gu, act = _mm_dual_fwd(f"ffn_in{layer}", nrm, w_in, None, "ffn")
        w_out = wts(f"wout{layer}", act)
        xout = _mm_nn(f"ffn_out{layer}", act, w_out, res=xin)
        return xout, (nrm, gu, act, w_in, w_out)

    x2, saved0 = ffn_fwd(x1, 0)
    kvn, hb = _rms_fwd("rms_b", x2, [w["kv_norm"], w["b_norm"]], [BF16, BF16])
    w_kv, w_q, w_o = wts("wkv", hb), wts("wq", hb), wts("wo", hb)
    k, v = _mm_kv("kv_proj", kvn, w_kv)
    q = _mm_nn("q_proj", hb, w_q)
    outs, lses = [], []
    for grp, dil in enumerate(DILATIONS):
        o_g, l_g = _att_fwd(q, k, v, grp, dil)
        outs.append(o_g)
        lses.append(l_g)
    o, lse = _att_combine(outs, lses)
    x3 = _mm_nn("o_proj", o, w_o, res=x2)
    x4, saved1 = ffn_fwd(x3, 1)
    loss_blk, dx4, dx4b, g["final_norm"] = _loss_head(x4, w["final_norm"], tgt)

    def ffn_bwd(dx, dxb, xin, saved, layer, dep):
        nrm, gu, act, w_in, w_out = saved
        dgu = _mm_nt_ffn_bwd(f"ffn_dact{layer}", dxb, w_out, gu, dep=dep)
        g_wout = _mm_tn(f"ffn_dwout{layer}", act, dxb)
        g_win = _mm_tn_pair(f"ffn_dwin{layer}", nrm, dgu)
        dn = _mm_nt_pair(f"ffn_dn{layer}", dgu, w_in)
        dxo, dxob, (dgn,) = _rms_bwd(f"rms_f{layer}_bwd", xin, dx, [w["ffn_norm"][layer:layer + 1]], [dn])
        tok = ready({f"win{layer}": g_win, f"wout{layer}": g_wout})
        return dxo, dxob, dgn, tok

    dx3, dx3b, dfn1, tok = ffn_bwd(dx4, dx4b, x3, saved1, 1, None)
    do = _mm_nt("o_proj_dx", [dx3b], w_o, dep=tok)
    g_wo = _mm_tn("o_proj_dw", o, dx3b)
    stats = _att_stats(lse, o, do)
    dq = lax.empty(q.shape, BF16)
    dks, dvs = [], []
    for grp, dil in enumerate(DILATIONS):
        dq, dk_g, dv_g = _att_bwd(q, k, v, do, stats, dq, grp, dil)
        dks.append(dk_g)
        dvs.append(dv_g)
    dkv = _sum_kv(dks, dvs)
    dhb = _mm_nt("q_proj_dx", [dq], w_q)
    g_wq = _mm_tn("q_proj_dw", hb, dq)
    dkvn = _mm_nt_pair("kv_proj_dx", dkv, w_kv)
    g_wkv = _mm_tn_pair("kv_proj_dw", kvn, dkv)
    dx2, dx2b, (g["kv_norm"], g["b_norm"]) = _rms_bwd(
        "rms_b_bwd", x2, dx3, [w["kv_norm"], w["b_norm"]], [dkvn, dhb])
    tok = ready({"wkv": g_wkv, "wq": g_wq, "wo": g_wo})
    dx1, dx1b, dfn0, tok = ffn_bwd(dx2, dx2b, x1, saved0, 0, tok)
    g["ffn_norm"] = jnp.concatenate([dfn0, dfn1], axis=0)

    dvg = _glu_bwd(dx1, vg, dep=tok)
    dz = _mm_nt_pair("glu_dx", dvg, w_glu)
    tok = ready({"glu": _mm_tn_pair("glu_dw", z, dvg)})
    tok = ready({}, after=tok)
    du, g["s5_d"], da_re, da_im, dbb2, dcc2 = _s5_bwd(
        u0, dz, h0, bb2, cc2, a_re, a_im, w["s5_d"], dep=tok)
    dcc = dcc2.transpose(0, 1, 3, 2).reshape(2, G, C, P)
    g["c_re"], g["c_im"] = dcc[0], -dcc[1]
    dbbt = dbb2[..., :P].reshape(2, G, C, P).transpose(0, 2, 1, 3)
    g["lam_re"], g["lam_im"], g["log_dt"], g["bt_re"], g["bt_im"] = _s5_param_bwd(
        w["lam_re"], w["lam_im"], w["log_dt"], w["bt_re"], w["bt_im"],
        da_re.reshape(G, P), da_im.reshape(G, P), dbbt[0], dbbt[1])
    grad_x, _, (g["a_norm"],) = _rms_bwd("rms_a_bwd", x, dx1, [w["a_norm"]], [du])
    return loss_blk, grad_x, g


def _coords():
    return lax.axis_index("x"), lax.axis_index("y"), lax.axis_index("c")


def _dev_index(dev):
    return 4 * dev[0] + 2 * dev[1] + dev[2]


def _shard_window(ref, axis, width, idx):
    sl = [slice(None)] * len(ref.shape)
    sl[axis] = pl.ds(pl.multiple_of(idx * width, width), width)
    return ref.at[tuple(sl)]


def _all_gather(name, shards, axes):
    na = len(shards)
    widths = [s.shape[ax] for s, ax in zip(shards, axes)]
    out_shapes = []
    for s, ax in zip(shards, axes):
        shp = list(s.shape)
        shp[ax] *= N_DEV
        out_shapes.append(jax.ShapeDtypeStruct(tuple(shp), s.dtype))

    def body(*refs):
        ins, outs = refs[:na], refs[na:2 * na]
        send_sems, recv_sems, local_sems = refs[2 * na:]
        x, y, c = _coords()
        me, sib = (x, y, c), (x, y, 1 - c)
        chips = [(1 - x, y), (x, 1 - y), (1 - x, 1 - y)]

        def blk(a, dev):
            return _shard_window(outs[a], axes[a], widths[a], _dev_index(dev))

        def copy(a, kk, block, to, src=None):
            return pltpu.make_async_remote_copy(
                src_ref=blk(a, block) if src is None else src, dst_ref=blk(a, block),
                send_sem=send_sems.at[a, kk], recv_sem=recv_sems.at[a, kk],
                device_id=to, device_id_type=MESH)

        local = [pltpu.make_async_copy(ins[a], blk(a, me), local_sems.at[a]) for a in range(na)]
        for cp in local:
            cp.start()
        sent = []
        for a in range(na):
            first = [copy(a, 0, me, sib, src=ins[a])]
            first += [copy(a, 1 + j, me, (*chip, c), src=ins[a]) for j, chip in enumerate(chips)]
            for cp in first:
                cp.start()
            sent += first
        for a in range(na):
            for j, chip in enumerate(chips):
                copy(a, 1 + j, (*chip, c), me).wait_recv()
                fwd = copy(a, 4 + j, (*chip, c), sib)
                fwd.start()
                sent.append(fwd)
        for a in range(na):
            copy(a, 0, sib, me).wait_recv()
            for j, chip in enumerate(chips):
                copy(a, 4 + j, (*chip, 1 - c), me).wait_recv()
        for cp in sent:
            cp.wait_send()
        for cp in local:
            cp.wait()

    return pl.pallas_call(
        body, out_shape=out_shapes, in_specs=[ANY] * na, out_specs=[ANY] * na,
        scratch_shapes=[pltpu.SemaphoreType.DMA((na, 7)), pltpu.SemaphoreType.DMA((na, 7)),
                        pltpu.SemaphoreType.DMA((na,))],
        name=name)(*shards)


HBM = pl.BlockSpec(memory_space=pltpu.HBM)
SEM = pl.BlockSpec(memory_space=pltpu.SEMAPHORE)
TOKEN_SPEC = pl.BlockSpec(memory_space=pltpu.VMEM)
TOKEN_SHAPE = jax.ShapeDtypeStruct((8, 128), F32)
SPLIT_PARAMS = pltpu.CompilerParams(has_side_effects=pltpu.SideEffectType.DATAFLOW_SIDE_EFFECTING)


def _hbm(x):
    return pltpu.with_memory_space_constraint(x, pltpu.HBM)


def _hbm_like(x):
    return pltpu.HBM(x.shape, x.dtype)


def _dma_sems(*shape):
    return pltpu.SemaphoreType.DMA(shape)


def _cast_and_place(name, shard, layer, axis, pos, dtype):
    rows, cols = shard.shape[-2:]
    tr = _tile(rows, 256, 16)
    nt = rows // tr
    full = (rows, cols * N_DEV) if axis == 1 else (rows * N_DEV, cols)

    def dev(p):
        return 4 * p[0] + 2 * p[1] + p[2]

    def body(pos_ref, s_ref, b_ref, l_ref):
        v = s_ref[...].astype(dtype)
        b_ref[...] = v
        l_ref[...] = v

    blk = pl.BlockSpec((tr, cols), lambda i, p: (i, 0))
    if axis == 1:
        lspec = pl.BlockSpec((tr, cols), lambda i, p: (i, dev(p)))
    else:
        lspec = pl.BlockSpec((tr, cols), lambda i, p: (dev(p) * nt + i, 0))
    return pl.pallas_call(
        body, grid_spec=pltpu.PrefetchScalarGridSpec(
            num_scalar_prefetch=1, grid=(nt,),
            in_specs=[pl.BlockSpec((None, tr, cols), lambda i, p: (layer, i, 0))], out_specs=[blk, lspec]),
        out_shape=[jax.ShapeDtypeStruct((rows, cols), dtype), jax.ShapeDtypeStruct(full, dtype)],
        compiler_params=_params(("parallel",)), name=name)(pos, shard)


def _gather_start(name, shards, lands, axes, groups):
    na, ng = len(shards), len(groups)
    widths = [s.shape[ax] for s, ax in zip(shards, axes)]

    def body(*refs):
        sh, ld = refs[:na], refs[na:2 * na]
        sems = refs[2 * na:2 * na + 3 * ng]
        token = refs[-1]
        x, y, c = _coords()
        me, sib = (x, y, c), (x, y, 1 - c)
        chips = [(1 - x, y), (x, 1 - y), (1 - x, 1 - y)]
        for gi, grp in enumerate(groups):
            send, r_d2d, r_ici = sems[3 * gi:3 * gi + 3]
            for li, a in enumerate(grp):
                dst = _shard_window(ld[a], axes[a], widths[a], _dev_index(me))
                pltpu.make_async_remote_copy(
                    src_ref=sh[a], dst_ref=dst, send_sem=send.at[4 * li], recv_sem=r_d2d.at[li],
                    device_id=sib, device_id_type=MESH).start()
                for j, chip in enumerate(chips):
                    pltpu.make_async_remote_copy(
                        src_ref=sh[a], dst_ref=dst, send_sem=send.at[4 * li + 1 + j], recv_sem=r_ici.at[3 * li + j],
                        device_id=(*chip, c), device_id_type=MESH).start()
        token[...] = jnp.zeros_like(token)

    out_shape, out_specs = [], []
    for grp in groups:
        out_shape += [_dma_sems(4 * len(grp)), _dma_sems(len(grp)), _dma_sems(3 * len(grp))]
        out_specs += [SEM] * 3
    out_shape += [_hbm_like(s) for s in shards] + [_hbm_like(l) for l in lands] + [TOKEN_SHAPE]
    out_specs += [HBM] * (2 * na) + [TOKEN_SPEC]
    aliases = {a: 3 * ng + a for a in range(2 * na)}
    res = pl.pallas_call(
        body, name=name, out_shape=out_shape, in_specs=[HBM] * (2 * na),
        out_specs=out_specs, input_output_aliases=aliases, compiler_params=SPLIT_PARAMS,
    )(*[_hbm(s) for s in shards], *[_hbm(l) for l in lands])
    sems = [tuple(res[3 * gi:3 * gi + 3]) for gi in range(ng)]
    return sems, list(res[3 * ng:3 * ng + na]), list(res[3 * ng + na:3 * ng + 2 * na]), res[-1]


def _gather_forward(name, lands, axes, r_ici, after):
    n = len(lands)
    widths = [l.shape[ax] // N_DEV for l, ax in zip(lands, axes)]

    def body(*refs):
        ld, r_ici_ref = refs[:n], refs[n]
        f_send, f_recv = refs[n + 2], refs[n + 3]
        x, y, c = _coords()
        sib = (x, y, 1 - c)
        chips = [(1 - x, y), (x, 1 - y), (1 - x, 1 - y)]
        for li in range(n):
            for j, chip in enumerate(chips):
                blk = _shard_window(ld[li], axes[li], widths[li], _dev_index((*chip, c)))
                pltpu.make_async_remote_copy(
                    src_ref=blk, dst_ref=blk, send_sem=f_send.at[3 * li + j], recv_sem=r_ici_ref.at[3 * li + j],
                    device_id=(*chip, c), device_id_type=MESH).wait_recv()
                pltpu.make_async_remote_copy(
                    src_ref=blk, dst_ref=blk, send_sem=f_send.at[3 * li + j], recv_sem=f_recv.at[3 * li + j],
                    device_id=sib, device_id_type=MESH).start()

    res = pl.pallas_call(
        body, name=name, out_shape=[_dma_sems(3 * n), _dma_sems(3 * n)] + [_hbm_like(l) for l in lands],
        in_specs=[HBM] * n + [SEM, ANY], out_specs=[SEM, SEM] + [HBM] * n,
        input_output_aliases={li: 2 + li for li in range(n)}, compiler_params=SPLIT_PARAMS,
    )(*lands, r_ici, after)
    return res[0], res[1], list(res[2:])


def _gather_finish(name, shards, lands, axes, send, r_d2d, f_send, f_recv):
    n = len(lands)
    widths = [l.shape[ax] // N_DEV for l, ax in zip(lands, axes)]

    def body(*refs):
        sh, ld = refs[:n], refs[n:2 * n]
        send_ref, r_d2d_ref, f_send_ref, f_recv_ref = refs[2 * n:2 * n + 4]
        x, y, c = _coords()
        me, sib = (x, y, c), (x, y, 1 - c)
        chips = [(1 - x, y), (x, 1 - y), (1 - x, 1 - y)]

        def blk(li, dev):
            return _shard_window(ld[li], axes[li], widths[li], _dev_index(dev))

        for li in range(n):
            for kk in range(4):
                pltpu.make_async_remote_copy(
                    src_ref=sh[li], dst_ref=blk(li, me), send_sem=send_ref.at[4 * li + kk], recv_sem=r_d2d_ref.at[li],
                    device_id=sib, device_id_type=MESH).wait_send()
            pltpu.make_async_remote_copy(
                src_ref=blk(li, sib), dst_ref=blk(li, sib), send_sem=send_ref.at[4 * li], recv_sem=r_d2d_ref.at[li],
                device_id=sib, device_id_type=MESH).wait_recv()
            for j, chip in enumerate(chips):
                pltpu.make_async_remote_copy(
                    src_ref=blk(li, (*chip, c)), dst_ref=blk(li, (*chip, c)), send_sem=f_send_ref.at[3 * li + j],
                    recv_sem=f_recv_ref.at[3 * li + j], device_id=sib, device_id_type=MESH).wait_send()
                pltpu.make_async_remote_copy(
                    src_ref=blk(li, (*chip, 1 - c)), dst_ref=blk(li, (*chip, 1 - c)), send_sem=f_send_ref.at[3 * li + j],
                    recv_sem=f_recv_ref.at[3 * li + j], device_id=sib, device_id_type=MESH).wait_recv()

    res = pl.pallas_call(
        body, name=name, out_shape=[_hbm_like(s) for s in shards] + [_hbm_like(l) for l in lands],
        in_specs=[HBM] * (2 * n) + [SEM] * 4, out_specs=[HBM] * (2 * n),
        input_output_aliases={i: i for i in range(2 * n)}, compiler_params=SPLIT_PARAMS,
    )(*shards, *lands, send, r_d2d, f_send, f_recv)
    return list(res[n:])


def _chip_exchange_start(name, parts):
    n = len(parts)

    def body(*refs):
        src, ld = refs[:n], refs[n:2 * n]
        send, recv = refs[2 * n], refs[2 * n + 1]
        token = refs[-1]
        x, y, c = _coords()
        chips = [(1 - x, y), (x, 1 - y), (1 - x, 1 - y)]
        for li in range(n):
            for kk, chip in enumerate(chips):
                pltpu.make_async_remote_copy(
                    src_ref=src[li].at[kk], dst_ref=ld[li].at[kk], send_sem=send.at[3 * li + kk],
                    recv_sem=recv.at[3 * li + kk], device_id=(*chip, c), device_id_type=MESH).start()
        token[...] = jnp.zeros_like(token)

    lands = [lax.empty(p.shape, p.dtype) for p in parts]
    res = pl.pallas_call(
        body, name=name,
        out_shape=[_dma_sems(3 * n), _dma_sems(3 * n)] + [_hbm_like(p) for p in parts] * 2 + [TOKEN_SHAPE],
        in_specs=[HBM] * (2 * n), out_specs=[SEM, SEM] + [HBM] * (2 * n) + [TOKEN_SPEC],
        input_output_aliases={i: 2 + i for i in range(2 * n)}, compiler_params=SPLIT_PARAMS,
    )(*[_hbm(p) for p in parts], *[_hbm(l) for l in lands])
    return res[0], res[1], list(res[2:2 + n]), list(res[2 + n:2 + 2 * n]), res[-1]


def _chip_exchange_finish(name, started, after):
    counts = [len(st[2]) for st in started]
    total = sum(counts)
    ns = len(started)

    def body(*refs):
        src, ld = refs[:total], refs[total:2 * total]
        sems = refs[2 * total:2 * total + 2 * ns]
        x, y, c = _coords()
        chips = [(1 - x, y), (x, 1 - y), (1 - x, 1 - y)]
        off = 0
        for si, cnt in enumerate(counts):
            send, recv = sems[2 * si], sems[2 * si + 1]
            for li in range(cnt):
                for kk, chip in enumerate(chips):
                    cp = pltpu.make_async_remote_copy(
                        src_ref=src[off + li].at[kk], dst_ref=ld[off + li].at[kk], send_sem=send.at[3 * li + kk],
                        recv_sem=recv.at[3 * li + kk], device_id=(*chip, c), device_id_type=MESH)
                    cp.wait_send()
                    cp.wait_recv()
            off += cnt

    srcs = [p for st in started for p in st[2]]
    lands = [l for st in started for l in st[3]]
    sems = [s for st in started for s in st[:2]]
    res = pl.pallas_call(
        body, name=name, out_shape=[_hbm_like(p) for p in srcs + lands],
        in_specs=[HBM] * (2 * total) + [SEM] * (2 * ns) + [ANY], out_specs=[HBM] * (2 * total),
        input_output_aliases={i: i for i in range(2 * total)}, compiler_params=SPLIT_PARAMS,
    )(*srcs, *lands, *sems, after)
    out, off = [], total
    for cnt in counts:
        out.append(list(res[off:off + cnt]))
        off += cnt
    return out


def _pair_copies(src, ld, send, recv, axes, widths):
    x, y, c = _coords()
    chips = [(x, y), (1 - x, y), (x, 1 - y), (1 - x, 1 - y)]
    return [pltpu.make_async_remote_copy(
        src_ref=_shard_window(src[li], axes[li], widths[li], _dev_index((*chip, 1 - c))),
        dst_ref=ld[li].at[kk], send_sem=send.at[4 * li + kk], recv_sem=recv.at[4 * li + kk],
        device_id=(x, y, 1 - c), device_id_type=MESH)
        for li in range(len(src)) for kk, chip in enumerate(chips)]


def _pair_exchange_start(name, grads, axes):
    n = len(grads)
    widths = [gr.shape[ax] // N_DEV for gr, ax in zip(grads, axes)]
    lands = []
    for gr, ax, wd in zip(grads, axes, widths):
        shp = list(gr.shape)
        shp[ax] = wd
        lands.append(lax.empty((4, *shp), gr.dtype))

    def body(*refs):
        for cp in _pair_copies(refs[:n], refs[n:2 * n], refs[2 * n], refs[2 * n + 1], axes, widths):
            cp.start()
        refs[-1][...] = jnp.zeros_like(refs[-1])

    res = pl.pallas_call(
        body, name=name,
        out_shape=[_dma_sems(4 * n), _dma_sems(4 * n)] + [_hbm_like(a) for a in grads + lands] + [TOKEN_SHAPE],
        in_specs=[HBM] * (2 * n), out_specs=[SEM, SEM] + [HBM] * (2 * n) + [TOKEN_SPEC],
        input_output_aliases={i: 2 + i for i in range(2 * n)}, compiler_params=SPLIT_PARAMS,
    )(*[_hbm(a) for a in grads + lands])
    return res[0], res[1], list(res[2:2 + n]), list(res[2 + n:2 + 2 * n]), res[-1]


def _pair_exchange_finish(name, started, axes, after):
    send, recv, grads, lands, _ = started
    n = len(grads)
    widths = [gr.shape[ax] // N_DEV for gr, ax in zip(grads, axes)]

    def body(*refs):
        for cp in _pair_copies(refs[:n], refs[n:2 * n], refs[2 * n], refs[2 * n + 1], axes, widths):
            cp.wait_send()
            cp.wait_recv()

    res = pl.pallas_call(
        body, name=name, out_shape=[_hbm_like(a) for a in grads + lands],
        in_specs=[HBM] * (2 * n) + [SEM, SEM, ANY], out_specs=[HBM] * (2 * n),
        input_output_aliases={i: i for i in range(2 * n)}, compiler_params=SPLIT_PARAMS,
    )(*grads, *lands, send, recv, after)
    return list(res[:n]), list(res[n:])


def _pair_sum(name, grad, land, axis, pos):
    wd = grad.shape[axis] // N_DEV
    shard_shape = land.shape[1:]
    rows, cols = shard_shape
    tr = _tile(rows, 256, 16)
    nt = rows // tr

    def dev_of(kk, pos_ref):
        return 4 * (pos_ref[0] ^ (kk & 1)) + 2 * (pos_ref[1] ^ (kk >> 1)) + pos_ref[2]

    if axis == 1:
        gspec = pl.BlockSpec((tr, wd), lambda kk, t, p: (t, dev_of(kk, p)))
    else:
        gspec = pl.BlockSpec((tr, cols), lambda kk, t, p: (dev_of(kk, p) * nt + t, 0))
    lspec = pl.BlockSpec((None, tr, cols), lambda kk, t, p: (kk, t, 0))
    p0spec = pl.BlockSpec((tr, cols), lambda kk, t, p: (jnp.where(kk == 0, t, nt - 1), 0))
    pbspec = pl.BlockSpec((None, tr, cols), lambda kk, t, p: (jnp.maximum(kk - 1, 0), jnp.where(kk == 0, 0, t), 0))

    def body(pos_ref, g_ref, l_ref, p0_ref, pb_ref):
        kk = pl.program_id(0)
        s = g_ref[...].astype(F32) + l_ref[...].astype(F32)

        @pl.when(kk == 0)
        def _():
            p0_ref[...] = s

        @pl.when(kk > 0)
        def _():
            pb_ref[...] = s.astype(BF16)

    return pl.pallas_call(
        body,
        grid_spec=pltpu.PrefetchScalarGridSpec(
            num_scalar_prefetch=1, grid=(4, nt), in_specs=[gspec, lspec], out_specs=[p0spec, pbspec]),
        out_shape=[jax.ShapeDtypeStruct(shard_shape, F32), jax.ShapeDtypeStruct((3, *shard_shape), BF16)],
        compiler_params=_params(("arbitrary", "arbitrary")), name=name)(pos, grad, land)


def _adamw(name, parts, w, m, v, layer=None, prev=None):
    rows, cols = w.shape[-2:]
    tr = _tile(rows, 256, 8)
    npart = len(parts)
    c1 = 1.0 - ADAM_B1 ** ADAM_STEP
    c2 = 1.0 - ADAM_B2 ** ADAM_STEP

    def body(*refs):
        p_refs = refs[:npart]
        w_ref, m_ref, v_ref = refs[npart:npart + 3]
        g_ref, d_ref, nm_ref, nv_ref = refs[-4:]
        g = None
        for r in p_refs:
            if len(r.shape) == 3:
                for i in range(r.shape[0]):
                    t = r[i].astype(F32)
                    g = t if g is None else g + t
            else:
                t = r[...].astype(F32)
                g = t if g is None else g + t
        nm = ADAM_B1 * m_ref[...] + (1.0 - ADAM_B1) * g
        nv = ADAM_B2 * v_ref[...] + (1.0 - ADAM_B2) * (g * g)
        g_ref[...] = g
        nm_ref[...] = nm
        nv_ref[...] = nv
        d_ref[...] = -ADAM_LR * ((nm / c1) / (jnp.sqrt(nv / c2) + ADAM_EPS) + ADAM_WD * w_ref[...])

    spec = pl.BlockSpec((tr, cols), lambda i: (i, 0))
    wspec = spec if layer is None else pl.BlockSpec((None, tr, cols), lambda i: (layer, i, 0))
    pspecs = [pl.BlockSpec((p.shape[0], tr, cols), lambda i: (0, i, 0)) if p.ndim == 3 else spec
              for p in parts]
    prev = list(prev) if prev else []
    return pl.pallas_call(
        body, grid=(rows // tr,), in_specs=pspecs + [wspec] * 3 + [ANY] * len(prev), out_specs=[wspec] * 4,
        out_shape=[jax.ShapeDtypeStruct(w.shape, F32)] * 4,
        input_output_aliases={npart + 3 + i: i for i in range(len(prev))},
        compiler_params=_params(("parallel",)), name=name)(*parts, w, m, v, *prev)


SMALL_NAMES = ("s5_lam_re", "s5_lam_im", "s5_log_dt", "s5_b_re", "s5_b_im", "s5_c_re", "s5_c_im",
               "ffn_norm", "b_norm_mix", "kv_norm", "final_norm")
SMALL_PAD = 1024


def _pack(parts):
    flat = []
    for p in parts:
        f = p.reshape(-1)
        pad = (-f.shape[0]) % SMALL_PAD
        if pad:
            f = jnp.concatenate([f, jnp.zeros((pad,), f.dtype)])
        flat.append(f)
    return jnp.concatenate(flat).reshape(-1, 128)


def _unpack(packed, shapes):
    flat = packed.reshape(-1)
    out, off = [], 0
    for shp in shapes:
        size = math.prod(shp)
        out.append(flat[off:off + size].reshape(shp))
        off += size + (-size) % SMALL_PAD
    return out


def kernel(x, s5_lam_re, s5_lam_im, s5_log_dt, s5_b_re, s5_b_im, s5_c_re, s5_c_im, s5_d, s5_w_glu, a_norm_mix, ffn_norm, ffn_w_in, ffn_w_out, b_norm_mix, attn_w_q, attn_w_o, kv_norm, w_kv, final_norm, loss_target, m_s5_lam_re, m_s5_lam_im, m_s5_log_dt, m_s5_b_re, m_s5_b_im, m_s5_c_re, m_s5_c_im, m_s5_d, m_s5_w_glu, m_a_norm_mix, m_ffn_norm, m_ffn_w_in, m_ffn_w_out, m_b_norm_mix, m_attn_w_q, m_attn_w_o, m_kv_norm, m_w_kv, m_final_norm, v_s5_lam_re, v_s5_lam_im, v_s5_log_dt, v_s5_b_re, v_s5_b_im, v_s5_c_re, v_s5_c_im, v_s5_d, v_s5_w_glu, v_a_norm_mix, v_ffn_norm, v_ffn_w_in, v_ffn_w_out, v_b_norm_mix, v_attn_w_q, v_attn_w_o, v_kv_norm, v_w_kv, v_final_norm):
    args = dict(locals())
    T, D = x.shape[1], x.shape[2]
    n_layers = ffn_w_in.shape[0]
    xi_, yi_, ci_ = _coords()
    pos = jnp.stack([xi_, yi_, ci_]).astype(jnp.int32)
    me = 4 * xi_ + 2 * yi_ + ci_

    big_names = ["glu"] + [f"win{l}" for l in range(n_layers)] + [f"wout{l}" for l in range(n_layers)] \
        + ["wkv", "wq", "wo"]
    big_shards = [(s5_w_glu, 0)] + [(ffn_w_in, l) for l in range(n_layers)] \
        + [(ffn_w_out, l) for l in range(n_layers)] + [(w_kv[None], 0), (attn_w_q, 0), (attn_w_o, 0)]
    big_axes = [1] + [1] * n_layers + [0] * n_layers + [0, 1, 0]
    big_out_names = ["s5_w_glu"] + ["ffn_w_in"] * n_layers + ["ffn_w_out"] * n_layers \
        + ["w_kv", "attn_w_q", "attn_w_o"]
    index_of = {n: i for i, n in enumerate(big_names)}
    vec_shard = jnp.concatenate([s5_d, a_norm_mix], axis=0)
    (vecs,) = _all_gather("vectors_all_gather", [vec_shard], [1])

    gather_groups = [["glu"], ["win0"], ["wout0"], ["wkv", "wq", "wo"], ["win1", "wout1"]]
    group_idx = [[index_of[n] for n in grp] for grp in gather_groups]
    group_of = {n: gi for gi, grp in enumerate(gather_groups) for n in grp}
    placed = [_cast_and_place(f"cast_place_{n}", s, l, ax, pos, BF16)
              for n, (s, l), ax in zip(big_names, big_shards, big_axes)]
    gather_sems, shards_thru, lands_thru, start_token = _gather_start(
        "weights_gather_start", [p[0] for p in placed], [p[1] for p in placed], big_axes, group_idx)
    full = {}

    def wts(name, after):
        if name not in full:
            gi = group_of[name]
            idx = group_idx[gi]
            axes = [big_axes[a] for a in idx]
            send, r_d2d, r_ici = gather_sems[gi]
            f_send, f_recv, lands = _gather_forward(
                f"weights_gather_forward{gi}", [lands_thru[a] for a in idx], axes, r_ici, after)
            done = _gather_finish(f"weights_gather_finish{gi}", [shards_thru[a] for a in idx], lands, axes,
                                  send, r_d2d, f_send, f_recv)
            full.update(zip(gather_groups[gi], done))
        return full[name]

    exchanges, pending = [], []

    def flush(after):
        names, axes, started = pending.pop()
        tag = "_".join(names)
        grads, lands = _pair_exchange_finish(f"rs_pair_exchange_finish_{tag}", started, axes, after)
        p0s, pbs = [], []
        for n, gr, land, ax in zip(names, grads, lands, axes):
            p0, pb = _pair_sum(f"rs_pair_sum_{n}", gr, land, ax, pos)
            p0s.append(p0)
            pbs.append(pb)
        started = _chip_exchange_start(f"rs_chip_exchange_start_{tag}", pbs)
        exchanges.append((names, p0s, started))
        return started[4]

    def ready(grads, after=None):
        if not grads:
            return flush(after)
        names = list(grads)
        axes = [big_axes[index_of[n]] for n in names]
        started = _pair_exchange_start(f"rs_pair_exchange_start_{'_'.join(names)}", [grads[n] for n in names], axes)
        token = flush(started[4]) if pending else started[4]
        pending.append((names, axes, started))
        return token

    G, P, C = s5_b_re.shape[1:]
    w = dict(
        a_norm=vecs[1:2], s5_d=vecs[0:1],
        ffn_norm=ffn_norm, b_norm=b_norm_mix, kv_norm=kv_norm.reshape(1, D), final_norm=final_norm.reshape(1, D),
        lam_re=s5_lam_re[0], lam_im=s5_lam_im[0], log_dt=s5_log_dt.reshape(G, 1),
        bt_re=s5_b_re[0].transpose(2, 0, 1), bt_im=s5_b_im[0].transpose(2, 0, 1),
        c_re=s5_c_re[0], c_im=s5_c_im[0],
    )
    loss_blk, grad_x, g = _local_step(x[0], loss_target[0], w, wts, ready, dep0=start_token)
    loss = lax.psum(loss_blk[0, 0], ("x", "y", "c"))

    out = {}

    def put(name, res, shape):
        for kind, r in zip(("grad", "delta", "new_m", "new_v"), res):
            out[f"{kind}_{name}"] = r.reshape(shape)

    small_g = dict(
        s5_lam_re=g["lam_re"], s5_lam_im=g["lam_im"], s5_log_dt=g["log_dt"],
        s5_b_re=g["bt_re"].transpose(1, 2, 0), s5_b_im=g["bt_im"].transpose(1, 2, 0),
        s5_c_re=g["c_re"], s5_c_im=g["c_im"], ffn_norm=g["ffn_norm"], b_norm_mix=g["b_norm"],
        kv_norm=g["kv_norm"], final_norm=g["final_norm"])
    packed = _pack([small_g[n] for n in SMALL_NAMES] + [g["s5_d"], g["a_norm"]])
    rows = packed.shape[0]
    small_shard, small_land = _cast_and_place("place_small_grads", packed[None], 0, 0, pos, F32)
    small_sems, small_thru, small_lands, _ = _gather_start(
        "small_grads_gather_start", [small_shard], [small_land], [0], [[0]])

    updated = {}

    def update(names, p0s, recvd):
        for name, p0, rc in zip(names, p0s, recvd):
            oname = big_out_names[index_of[name]]
            w3, layer = big_shards[index_of[name]]
            updated[oname] = _adamw(f"adamw_{name}", [p0, rc], w3, args["m_" + oname].reshape(w3.shape),
                                    args["v_" + oname].reshape(w3.shape), layer=layer, prev=updated.get(oname))
        return updated[big_out_names[index_of[names[0]]]][0]

    early, last = exchanges[:-1], exchanges[-1]
    landed = _chip_exchange_finish("rs_chip_exchange_finish_early", [e[2] for e in early], grad_x)
    for (names, p0s, _), recvd in zip(early[:-1], landed[:-1]):
        marker = update(names, p0s, recvd)
    send, r_d2d, r_ici = small_sems[0]
    f_send, f_recv, small_lands = _gather_forward("small_grads_gather_forward", small_lands, [0], r_ici, marker)
    marker = update(early[-1][0], early[-1][1], landed[-1])
    (all_parts,) = _gather_finish("small_grads_gather_finish", small_thru, small_lands, [0],
                                  send, r_d2d, f_send, f_recv)
    (recvd,) = _chip_exchange_finish("rs_chip_exchange_finish_last", [last[2]], marker)
    update(last[0], last[1], recvd)
    for oname, res in updated.items():
        put(oname, res, args[oname].shape)
    all_parts = all_parts.reshape(N_DEV, rows, 128)
    n_rep_rows = _pack([small_g[n] for n in SMALL_NAMES]).shape[0]
    w_pack = _pack([args[n] for n in SMALL_NAMES])
    m_pack = _pack([args["m_" + n] for n in SMALL_NAMES])
    v_pack = _pack([args["v_" + n] for n in SMALL_NAMES])
    res = _adamw("adamw_small", [all_parts[:, :n_rep_rows]], w_pack, m_pack, v_pack)
    shapes = [args[n].shape for n in SMALL_NAMES]
    unpacked = [_unpack(r, shapes) for r in res]
    for i, n in enumerate(SMALL_NAMES):
        put(n, [u[i] for u in unpacked], args[n].shape)
    ws = D // N_DEV
    tail = all_parts[:, n_rep_rows:].reshape(N_DEV, 2, D)
    tail = lax.dynamic_slice_in_dim(tail, me * ws, ws, axis=2)
    res = _adamw("adamw_vec", [tail], vec_shard,
                 jnp.concatenate([m_s5_d, m_a_norm_mix], axis=0), jnp.concatenate([v_s5_d, v_a_norm_mix], axis=0))
    put("s5_d", [r[0:1] for r in res], s5_d.shape)
    put("a_norm_mix", [r[1:2] for r in res], a_norm_mix.shape)

    names = ("s5_lam_re", "s5_lam_im", "s5_log_dt", "s5_b_re", "s5_b_im", "s5_c_re", "s5_c_im", "s5_d",
             "s5_w_glu", "a_norm_mix", "ffn_norm", "ffn_w_in", "ffn_w_out", "b_norm_mix", "attn_w_q",
             "attn_w_o", "kv_norm", "w_kv", "final_norm")
    result = [loss, grad_x.reshape(x.shape)]
    for kind in ("grad", "delta", "new_m", "new_v"):
        result += [out[f"{kind}_{n}"] for n in names]
    return tuple(result)
```

```python
import functools
import math

import jax
import jax.numpy as jnp
from jax import lax
from jax.experimental import pallas as pl
from jax.experimental.pallas import tpu as pltpu

F32 = jnp.float32
BF16 = jnp.bfloat16

EPS = 1e-6
NEG_INF = -1e30
HEAD_DIM = 128
N_KV_HEADS = 4
DILATIONS = (1, 4, 16)
ATT_BLK = 128
S5_C = 16
S5_P = 64
S5_GB = 16
S5_CH = S5_GB * S5_C
S5_W = S5_GB * S5_P
S5_UNROLL = 4
N_DEV = 8

ADAM_LR = 0.001
ADAM_B1 = 0.9
ADAM_B2 = 0.999
ADAM_EPS = 1e-08
ADAM_WD = 0.01
ADAM_STEP = 10

VMEM_LIMIT_BYTES = 56 * 1024 * 1024
MM_TILE = 1024
MM_TILE_NARROW = 512
MM_DEPTH = 2816
MESH = pl.DeviceIdType.MESH
ANY = pl.BlockSpec(memory_space=pl.ANY)


def _tile(n, pref, align=128):
    t = (min(pref, n) // align) * align
    while t >= align:
        if n % t == 0:
            return t
        t -= align
    return n


def _params(sem):
    return pltpu.CompilerParams(dimension_semantics=sem, vmem_limit_bytes=VMEM_LIMIT_BYTES)


def _sigmoid(x):
    return 1.0 / (1.0 + jnp.exp(-x))


NN = (((1,), (0,)), ((), ()))
NT = (((1,), (1,)), ((), ()))
TN = (((0,), (0,)), ((), ()))


def _dot(a, b, dims=NN):
    return lax.dot_general(a, b, dims, preferred_element_type=F32)


def _matmul(name, grid, ins, in_specs, products, dims, out_shapes, out_specs, acc_shapes, epilogue):
    n_in, n_out, nk = len(ins), len(out_shapes), grid[2]

    def body(*refs):
        in_refs = refs[:n_in]
        out_refs = refs[n_in:n_in + n_out]
        acc_refs = refs[n_in + n_out:]

        def prods():
            vals = [None] * len(acc_shapes)
            for ai, bi, ci in products:
                d = _dot(in_refs[ai][...].astype(BF16), in_refs[bi][...].astype(BF16), dims)
                vals[ci] = d if vals[ci] is None else vals[ci] + d
            return vals

        if nk == 1:
            epilogue(in_refs, out_refs, prods())
        else:
            k = pl.program_id(2)

            @pl.when(k == 0)
            def _():
                for a in acc_refs:
                    a[...] = jnp.zeros_like(a)

            for a, v in zip(acc_refs, prods()):
                a[...] += v

            @pl.when(k == nk - 1)
            def _():
                epilogue(in_refs, out_refs, [a[...] for a in acc_refs])

    scratch = [] if nk == 1 else [pltpu.VMEM(s, F32) for s in acc_shapes]
    return pl.pallas_call(
        body, grid=grid, in_specs=in_specs, out_specs=out_specs, out_shape=out_shapes,
        scratch_shapes=scratch, compiler_params=_params(("parallel", "parallel", "arbitrary")),
        name=name)(*ins)


def _mm_dual_fwd(name, a, w, res, kind):
    T, K = a.shape
    N = w.shape[1] // 2
    tm, tn = _tile(T, MM_TILE), _tile(N, MM_TILE_NARROW)
    nj = N // tn
    grid = (T // tm, nj, 1)
    ins = [a, w, w]
    in_specs = [pl.BlockSpec((tm, K), lambda i, j, k: (i, 0)),
                pl.BlockSpec((K, tn), lambda i, j, k: (0, j)),
                pl.BlockSpec((K, tn), lambda i, j, k: (0, j + nj))]
    pair_spec = pl.BlockSpec((2, tm, tn), lambda i, j, k: (0, i, j))
    tile_spec = pl.BlockSpec((tm, tn), lambda i, j, k: (i, j))
    if kind == "glu":
        ins.append(res)
        in_specs.append(tile_spec)

        def epilogue(in_refs, out_refs, accs):
            val, gate = accs
            s = _sigmoid(gate)
            out_refs[0][...] = in_refs[3][...] + val * s
            out_refs[1][0] = s.astype(BF16)
            out_refs[1][1] = (val * s * (1.0 - s)).astype(BF16)

        out_shapes = [jax.ShapeDtypeStruct((T, N), F32), jax.ShapeDtypeStruct((2, T, N), BF16)]
        out_specs = [tile_spec, pair_spec]
    else:
        def epilogue(in_refs, out_refs, accs):
            g, u = accs
            s = _sigmoid(g)
            silu = g * s
            out_refs[0][0] = (u * (s * (1.0 + g * (1.0 - s)))).astype(BF16)
            out_refs[0][1] = silu.astype(BF16)
            out_refs[1][...] = (silu * u).astype(BF16)

        out_shapes = [jax.ShapeDtypeStruct((2, T, N), BF16), jax.ShapeDtypeStruct((T, N), BF16)]
        out_specs = [pair_spec, tile_spec]
    return _matmul(name, grid, ins, in_specs, [(0, 1, 0), (0, 2, 1)], NN, out_shapes, out_specs,
                   [(tm, tn), (tm, tn)], epilogue)


def _mm_kv(name, a, w):
    T, K = a.shape
    N = w.shape[1] // 2
    tm, tn = _tile(T, MM_TILE), _tile(N, MM_TILE_NARROW)
    nj = N // tn
    tile_spec = pl.BlockSpec((tm, tn), lambda i, j, k: (i, j))

    def epilogue(in_refs, out_refs, accs):
        out_refs[0][...] = accs[0]
        out_refs[1][...] = accs[1]

    return _matmul(name, (T // tm, nj, 1), [a, w, w],
                   [pl.BlockSpec((tm, K), lambda i, j, k: (i, 0)),
                    pl.BlockSpec((K, tn), lambda i, j, k: (0, j)),
                    pl.BlockSpec((K, tn), lambda i, j, k: (0, j + nj))],
                   [(0, 1, 0), (0, 2, 1)], NN,
                   [jax.ShapeDtypeStruct((T, N), F32)] * 2, [tile_spec, tile_spec],
                   [(tm, tn), (tm, tn)], epilogue)


def _mm_nn(name, a, w, res=None, out_dtype=F32):
    T, K = a.shape
    N = w.shape[1]
    tk = K if K <= 2 * MM_DEPTH else _tile(K, MM_DEPTH)
    tm, tn = _tile(T, MM_TILE), _tile(N, MM_TILE if K <= MM_DEPTH else MM_TILE_NARROW)
    grid = (T // tm, N // tn, K // tk)
    tile_spec = pl.BlockSpec((tm, tn), lambda i, j, k: (i, j))
    ins = [a, w]
    in_specs = [pl.BlockSpec((tm, tk), lambda i, j, k: (i, k)),
                pl.BlockSpec((tk, tn), lambda i, j, k: (k, j))]
    if res is not None:
        ins.append(res)
        in_specs.append(tile_spec)

    def epilogue(in_refs, out_refs, accs):
        v = accs[0]
        if res is not None:
            v = v + in_refs[2][...]
        out_refs[0][...] = v.astype(out_dtype)

    return _matmul(name, grid, ins, in_specs, [(0, 1, 0)], NN,
                   [jax.ShapeDtypeStruct((T, N), out_dtype)], [tile_spec], [(tm, tn)], epilogue)[0]


def _dep_operand(ins, in_specs, dep):
    if dep is not None:
        ins.append(dep)
        in_specs.append(pl.BlockSpec((8, 128), lambda *_: (0, 0)))


def _mm_nt(name, a_list, w, out_dtype=F32, dep=None):
    T, Np = a_list[0].shape
    Ko = w.shape[0]
    n_parts = len(a_list)
    wide_a = a_list[0].dtype != BF16
    tm, tn, tk = _tile(T, MM_TILE_NARROW if wide_a else MM_TILE), _tile(Ko, MM_TILE), _tile(Np, MM_DEPTH)
    nkp = Np // tk
    grid = (T // tm, Ko // tn, nkp)
    ins = list(a_list) + [w] * n_parts
    in_specs = [pl.BlockSpec((tm, tk), lambda i, j, k: (i, k)) for _ in a_list]
    in_specs += [pl.BlockSpec((tn, tk), functools.partial(lambda i, j, k, p: (j, p * nkp + k), p=p))
                 for p in range(n_parts)]
    products = [(p, n_parts + p, 0) for p in range(n_parts)]
    _dep_operand(ins, in_specs, dep)

    def epilogue(in_refs, out_refs, accs):
        out_refs[0][...] = accs[0].astype(out_dtype)

    return _matmul(name, grid, ins, in_specs, products, NT,
                   [jax.ShapeDtypeStruct((T, Ko), out_dtype)],
                   [pl.BlockSpec((tm, tn), lambda i, j, k: (i, j))], [(tm, tn)], epilogue)[0]


def _mm_nt_pair(name, a3, w):
    _, T, N = a3.shape
    Ko = w.shape[0]
    tm, tn, tk = _tile(T, MM_TILE), _tile(Ko, MM_TILE), _tile(N, MM_DEPTH)
    nkh = N // tk
    grid = (T // tm, Ko // tn, 2 * nkh)

    def epilogue(in_refs, out_refs, accs):
        out_refs[0][...] = accs[0]

    return _matmul(name, grid, [a3, w],
                   [pl.BlockSpec((None, tm, tk), lambda i, j, k: (k // nkh, i, k % nkh)),
                    pl.BlockSpec((tn, tk), lambda i, j, k: (j, k))],
                   [(0, 1, 0)], NT, [jax.ShapeDtypeStruct((T, Ko), F32)],
                   [pl.BlockSpec((tm, tn), lambda i, j, k: (i, j))], [(tm, tn)], epilogue)[0]


def _mm_nt_ffn_bwd(name, dx, w_out, gu, dep=None):
    T, D = dx.shape
    Fh = w_out.shape[0]
    tm, tn = _tile(T, MM_TILE), _tile(Fh, MM_TILE_NARROW)
    sub = _tile(tn, 256)
    n_dep = 0 if dep is None else 1

    def body(dx_ref, w_ref, gu_ref, *rest):
        out_ref = rest[n_dep]
        a = dx_ref[...]
        for c0 in range(0, tn, sub):
            da = _dot(a, w_ref[c0:c0 + sub, :], NT)
            out_ref[0, :, c0:c0 + sub] = (da * gu_ref[0, :, c0:c0 + sub].astype(F32)).astype(BF16)
            out_ref[1, :, c0:c0 + sub] = (da * gu_ref[1, :, c0:c0 + sub].astype(F32)).astype(BF16)

    pair_spec = pl.BlockSpec((2, tm, tn), lambda i, j: (0, i, j))
    ins = [dx, w_out, gu]
    in_specs = [pl.BlockSpec((tm, D), lambda i, j: (i, 0)),
                pl.BlockSpec((tn, D), lambda i, j: (j, 0)),
                pair_spec]
    _dep_operand(ins, in_specs, dep)
    return pl.pallas_call(
        body, grid=(T // tm, Fh // tn), in_specs=in_specs, out_specs=pair_spec,
        out_shape=jax.ShapeDtypeStruct((2, T, Fh), BF16),
        compiler_params=_params(("parallel", "parallel")), name=name)(*ins)


def _mm_tn(name, a, d):
    T, Ko = a.shape
    N = d.shape[1]
    to, tn, tk = _tile(Ko, MM_TILE_NARROW), _tile(N, MM_TILE if d.dtype == BF16 else MM_TILE_NARROW), T
    grid = (Ko // to, N // tn, T // tk)

    def epilogue(in_refs, out_refs, accs):
        out_refs[0][...] = accs[0].astype(BF16)

    return _matmul(name, grid, [a, d],
                   [pl.BlockSpec((tk, to), lambda i, j, k: (k, i)),
                    pl.BlockSpec((tk, tn), lambda i, j, k: (k, j))],
                   [(0, 1, 0)], TN, [jax.ShapeDtypeStruct((Ko, N), BF16)],
                   [pl.BlockSpec((to, tn), lambda i, j, k: (i, j))], [(to, tn)], epilogue)[0]


def _mm_tn_pair(name, a, d3):
    T, Ko = a.shape
    N = d3.shape[2]
    to, tn, tk = _tile(Ko, MM_TILE), _tile(N, MM_TILE_NARROW), T
    njh = N // tn
    grid = (Ko // to, 2 * njh, T // tk)

    def epilogue(in_refs, out_refs, accs):
        out_refs[0][...] = accs[0].astype(BF16)

    return _matmul(name, grid, [a, d3],
                   [pl.BlockSpec((tk, to), lambda i, j, k: (k, i)),
                    pl.BlockSpec((None, tk, tn), lambda i, j, k: (j // njh, k, j % njh))],
                   [(0, 1, 0)], TN, [jax.ShapeDtypeStruct((Ko, 2 * N), BF16)],
                   [pl.BlockSpec((to, tn), lambda i, j, k: (i, j))], [(to, tn)], epilogue)[0]


def _rms_fwd(name, x, gains, dtypes, dep=None):
    T, D = x.shape
    n = len(gains)
    tr = _tile(T, 512, 8)
    n_dep = 0 if dep is None else 1

    def body(x_ref, *refs):
        xv = x_ref[...]
        xr = xv * lax.rsqrt(jnp.mean(xv * xv, axis=-1, keepdims=True) + EPS)
        for g_ref, o_ref in zip(refs[:n], refs[n + n_dep:]):
            o_ref[...] = (xr * g_ref[...]).astype(o_ref.dtype)

    row = pl.BlockSpec((tr, D), lambda i: (i, 0))
    vec = pl.BlockSpec((1, D), lambda i: (0, 0))
    ins, in_specs = [x, *gains], [row] + [vec] * n
    _dep_operand(ins, in_specs, dep)
    return pl.pallas_call(
        body, grid=(T // tr,), in_specs=in_specs, out_specs=[row] * n,
        out_shape=[jax.ShapeDtypeStruct((T, D), dt) for dt in dtypes],
        compiler_params=_params(("parallel",)), name=name)(*ins)


def _rms_bwd(name, x, dres, gains, dhs, dep=None):
    T, D = x.shape
    n = len(gains)
    tr = _tile(T, 256, 8)
    n_dep = 0 if dep is None else 1

    def body(x_ref, dres_ref, *refs):
        g_refs, dh_refs = refs[:n], refs[n:2 * n]
        dx_ref, dxb_ref = refs[2 * n + n_dep], refs[2 * n + n_dep + 1]
        dg_refs = refs[2 * n + n_dep + 2:]
        xv = x_ref[...]
        r = lax.rsqrt(jnp.mean(xv * xv, axis=-1, keepdims=True) + EPS)
        xr = xv * r
        w = None
        for g_ref, dh_ref, dg_ref in zip(g_refs, dh_refs, dg_refs):
            dh = dh_ref[...].astype(F32)

            @pl.when(pl.program_id(0) == 0)
            def _():
                dg_ref[...] = jnp.zeros_like(dg_ref)

            dg_ref[...] += jnp.sum(dh * xr, axis=0, keepdims=True)
            wi = dh * g_ref[...]
            w = wi if w is None else w + wi
        dx = dres_ref[...] + r * (w - xr * jnp.mean(w * xr, axis=-1, keepdims=True))
        dx_ref[...] = dx
        dxb_ref[...] = dx.astype(BF16)

    row = pl.BlockSpec((tr, D), lambda i: (i, 0))
    vec = pl.BlockSpec((1, D), lambda i: (0, 0))
    ins, in_specs = [x, dres, *gains, *dhs], [row, row] + [vec] * n + [row] * n
    _dep_operand(ins, in_specs, dep)
    outs = pl.pallas_call(
        body, grid=(T // tr,), in_specs=in_specs,
        out_specs=[row, row] + [vec] * n,
        out_shape=[jax.ShapeDtypeStruct((T, D), F32), jax.ShapeDtypeStruct((T, D), BF16)]
        + [jax.ShapeDtypeStruct((1, D), F32)] * n,
        compiler_params=_params(("arbitrary",)), name=name)(*ins)
    return outs[0], outs[1], outs[2:]


def _loss_head(x, gain, target):
    T, D = x.shape
    tr = _tile(T, 256, 8)

    def body(x_ref, g_ref, t_ref, loss_ref, dx_ref, dxb_ref, dg_ref):
        @pl.when(pl.program_id(0) == 0)
        def _():
            loss_ref[...] = jnp.zeros_like(loss_ref)
            dg_ref[...] = jnp.zeros_like(dg_ref)

        xv = x_ref[...]
        r = lax.rsqrt(jnp.mean(xv * xv, axis=-1, keepdims=True) + EPS)
        xr = xv * r
        err = xr * g_ref[...] - t_ref[...]
        part = jnp.sum(jnp.sum(err * err, axis=-1, keepdims=True), axis=0, keepdims=True) * (0.5 / D)
        loss_ref[...] += jnp.broadcast_to(part, loss_ref.shape)
        dy = err * (1.0 / D)
        dg_ref[...] += jnp.sum(dy * xr, axis=0, keepdims=True)
        w = dy * g_ref[...]
        dx = r * (w - xr * jnp.mean(w * xr, axis=-1, keepdims=True))
        dx_ref[...] = dx
        dxb_ref[...] = dx.astype(BF16)

    row = pl.BlockSpec((tr, D), lambda i: (i, 0))
    vec = pl.BlockSpec((1, D), lambda i: (0, 0))
    return pl.pallas_call(
        body, grid=(T // tr,), in_specs=[row, vec, row],
        out_specs=[pl.BlockSpec((8, 128), lambda i: (0, 0)), row, row, vec],
        out_shape=[jax.ShapeDtypeStruct((8, 128), F32), jax.ShapeDtypeStruct((T, D), F32),
                   jax.ShapeDtypeStruct((T, D), BF16), jax.ShapeDtypeStruct((1, D), F32)],
        compiler_params=_params(("arbitrary",)), name="loss_head")(x, gain, target)


def _glu_bwd(dmix, vg, dep=None):
    T, N = dmix.shape
    tr, tc = _tile(T, 512, 8), _tile(N, 1024)
    n_dep = 0 if dep is None else 1

    def body(d_ref, vg_ref, *refs):
        o_ref = refs[n_dep]
        d = d_ref[...]
        o_ref[0] = (d * vg_ref[0].astype(F32)).astype(BF16)
        o_ref[1] = (d * vg_ref[1].astype(F32)).astype(BF16)

    pair = pl.BlockSpec((2, tr, tc), lambda i, j: (0, i, j))
    ins, in_specs = [dmix, vg], [pl.BlockSpec((tr, tc), lambda i, j: (i, j)), pair]
    _dep_operand(ins, in_specs, dep)
    return pl.pallas_call(
        body, grid=(T // tr, N // tc), in_specs=in_specs,
        out_specs=pair, out_shape=jax.ShapeDtypeStruct((2, T, N), BF16),
        compiler_params=_params(("parallel", "parallel")), name="glu_bwd")(*ins)


def _to_state_tiles(x_ref, s, val):
    tc = val.shape[0]
    for j in range(S5_W // 128):
        x_ref[s, pl.ds(j, tc, stride=8), :] = val[:, 128 * j:128 * (j + 1)]


def _from_state_tiles(x_ref, s, tc):
    return jnp.concatenate([x_ref[s, pl.ds(j, tc, stride=8), :] for j in range(S5_W // 128)], axis=1)


def _s5_scan_fwd(xr_ref, xi_ref, ar_ref, ai_ref, cr_ref, ci_ref, tc, nblk):
    a = [(ar_ref[s], ai_ref[s]) for s in range(nblk)]

    def step(i, carry):
        carry = list(carry)
        for uu in range(S5_UNROLL):
            r0 = pl.multiple_of((i * S5_UNROLL + uu) * 8, 8)
            for s in range(nblk):
                cr, ci = carry[2 * s], carry[2 * s + 1]
                a_r, a_i = a[s]
                xr = a_r * cr - a_i * ci + xr_ref[s, pl.ds(r0, 8), :]
                xi = a_r * ci + a_i * cr + xi_ref[s, pl.ds(r0, 8), :]
                xr_ref[s, pl.ds(r0, 8), :] = xr
                xi_ref[s, pl.ds(r0, 8), :] = xi
                carry[2 * s], carry[2 * s + 1] = xr, xi
        return tuple(carry)

    init = []
    for s in range(nblk):
        init += [cr_ref[s], ci_ref[s]]
    out = lax.fori_loop(0, tc // S5_UNROLL, step, tuple(init))
    for s in range(nblk):
        cr_ref[s] = out[2 * s]
        ci_ref[s] = out[2 * s + 1]


def _s5_scan_bwd(lr_ref, li_ref, xr_ref, xi_ref, h_ref, ar_ref, ai_ref, cr_ref, ci_ref,
                 accr_ref, acci_ref, tc, nblk):
    a = [(ar_ref[s], ai_ref[s]) for s in range(nblk)]

    def one(s, r0, prev_r, prev_i, st):
        c_r, c_i, d_r, d_i = st
        a_r, a_i = a[s]
        l_r = lr_ref[s, pl.ds(r0, 8), :] + a_r * c_r + a_i * c_i
        l_i = li_ref[s, pl.ds(r0, 8), :] + a_r * c_i - a_i * c_r
        lr_ref[s, pl.ds(r0, 8), :] = l_r
        li_ref[s, pl.ds(r0, 8), :] = l_i
        return [l_r, l_i, d_r + l_r * prev_r + l_i * prev_i, d_i - l_r * prev_i + l_i * prev_r]

    def step(i, carry):
        carry = list(carry)
        for uu in range(S5_UNROLL):
            t = tc - 1 - (i * S5_UNROLL + uu)
            r0 = pl.multiple_of(t * 8, 8)
            p0 = pl.multiple_of((t - 1) * 8, 8)
            for s in range(nblk):
                carry[4 * s:4 * s + 4] = one(s, r0, xr_ref[s, pl.ds(p0, 8), :], xi_ref[s, pl.ds(p0, 8), :],
                                             carry[4 * s:4 * s + 4])
        return tuple(carry)

    init = []
    for s in range(nblk):
        init += [cr_ref[s], ci_ref[s], accr_ref[s], acci_ref[s]]
    carry = list(lax.fori_loop(0, tc // S5_UNROLL - 1, step, tuple(init)))
    for t in range(S5_UNROLL - 1, -1, -1):
        for s in range(nblk):
            if t > 0:
                prev_r, prev_i = xr_ref[s, 8 * (t - 1):8 * t, :], xi_ref[s, 8 * (t - 1):8 * t, :]
            else:
                prev_r, prev_i = h_ref[0, s], h_ref[1, s]
            carry[4 * s:4 * s + 4] = one(s, 8 * t, prev_r, prev_i, carry[4 * s:4 * s + 4])
    for s in range(nblk):
        cr_ref[s], ci_ref[s], accr_ref[s], acci_ref[s] = carry[4 * s:4 * s + 4]


def _gelu(y):
    k = math.sqrt(2.0 / math.pi)
    return 0.5 * y * (1.0 + jnp.tanh(k * (y + 0.044715 * (y * y * y))))


def _gelu_grad(y):
    k = math.sqrt(2.0 / math.pi)
    t = jnp.tanh(k * (y + 0.044715 * (y * y * y)))
    return 0.5 * (1.0 + t) + 0.5 * y * (1.0 - t * t) * (k * (1.0 + 3.0 * 0.044715 * (y * y)))


def _s5_specs(tc, nch, sbk, rev):
    def ch(c):
        return nch - 1 - c if rev else c

    return dict(
        act=pl.BlockSpec((tc, sbk * S5_CH), lambda i, c: (ch(c), i)),
        bb=pl.BlockSpec((2, sbk, S5_CH, 128), lambda i, c: (0, i, 0, 0)),
        cc=pl.BlockSpec((2, sbk, S5_P, S5_CH), lambda i, c: (0, i, 0, 0)),
        a=pl.BlockSpec((sbk, 8, 128), lambda i, c: (i, 0, 0)),
        d=pl.BlockSpec((1, sbk * S5_CH), lambda i, c: (0, i)),
        h=pl.BlockSpec((None, 2, sbk, 8, 128), lambda i, c: (ch(c), 0, i, 0, 0)),
    )


def _s5_blocks(nb, pref):
    return max(b for b in range(1, pref + 1) if nb % b == 0)


def _s5_group_masks():
    rb = lax.broadcasted_iota(jnp.int32, (S5_CH, S5_W), 0) // S5_C
    qb = lax.broadcasted_iota(jnp.int32, (S5_CH, S5_W), 1) // S5_P
    qc = lax.broadcasted_iota(jnp.int32, (S5_W, S5_CH), 0) // S5_P
    rc = lax.broadcasted_iota(jnp.int32, (S5_W, S5_CH), 1) // S5_C
    return rb == qb, qc == rc


def _s5_expand(bb_ref, cc_ref, bbd, ccd, sbk):
    mask_b, mask_c = _s5_group_masks()
    for k in range(2):
        for s in range(sbk):
            bbd[k, s] = jnp.where(mask_b, jnp.tile(bb_ref[k, s], (1, S5_W // 128)), 0).astype(BF16)
            ccd[k, s] = jnp.where(mask_c, jnp.tile(cc_ref[k, s], (S5_GB, 1)), 0).astype(BF16)


def _s5_fwd(u, bb2, cc2, a_re, a_im, d_skip):
    T, D = u.shape
    nb = D // S5_CH
    sbk = _s5_blocks(nb, 4)
    tc = _tile(T, 512, 8)
    nch = T // tc
    sp = _s5_specs(tc, nch, sbk, False)

    def body(u_ref, bb_ref, cc_ref, ar_ref, ai_ref, d_ref, z_ref, h_ref, xr, xi, cr, ci, bbd, ccd):
        @pl.when(pl.program_id(1) == 0)
        def _():
            cr[...] = jnp.zeros_like(cr)
            ci[...] = jnp.zeros_like(ci)
            _s5_expand(bb_ref, cc_ref, bbd, ccd, sbk)

        h_ref[0] = cr[...]
        h_ref[1] = ci[...]
        for s in range(sbk):
            ub = u_ref[:, s * S5_CH:(s + 1) * S5_CH].astype(BF16)
            _to_state_tiles(xr, s, _dot(ub, bbd[0, s]))
            _to_state_tiles(xi, s, _dot(ub, bbd[1, s]))
        _s5_scan_fwd(xr, xi, ar_ref, ai_ref, cr, ci, tc, sbk)
        for s in range(sbk):
            cols = slice(s * S5_CH, (s + 1) * S5_CH)
            y = (_dot(_from_state_tiles(xr, s, tc).astype(BF16), ccd[0, s])
                 - _dot(_from_state_tiles(xi, s, tc).astype(BF16), ccd[1, s])
                 + d_ref[:, cols] * u_ref[:, cols])
            z_ref[:, cols] = _gelu(y).astype(BF16)

    tiles = pltpu.VMEM((sbk, tc * 8, 128), F32)
    carry = pltpu.VMEM((sbk, 8, 128), F32)
    return pl.pallas_call(
        body, grid=(nb // sbk, nch),
        in_specs=[sp["act"], sp["bb"], sp["cc"], sp["a"], sp["a"], sp["d"]],
        out_specs=[sp["act"], sp["h"]],
        out_shape=[jax.ShapeDtypeStruct((T, D), BF16), jax.ShapeDtypeStruct((nch, 2, nb, 8, 128), F32)],
        scratch_shapes=[tiles, tiles, carry, carry, pltpu.VMEM((2, sbk, S5_CH, S5_W), BF16),
                        pltpu.VMEM((2, sbk, S5_W, S5_CH), BF16)],
        compiler_params=_params(("parallel", "arbitrary")), name="s5_fwd",
    )(u, bb2, cc2, a_re, a_im, d_skip)


def _s5_bwd(u, dz, h0, bb2, cc2, a_re, a_im, d_skip, dep=None):
    T, D = u.shape
    nb = D // S5_CH
    sbk = _s5_blocks(nb, 2)
    tc = _tile(T, 512, 8)
    nch = T // tc
    sp = _s5_specs(tc, nch, sbk, True)

    n_dep = 0 if dep is None else 1

    def body(u_ref, dz_ref, h_ref, bb_ref, cc_ref, ar_ref, ai_ref, d_ref, *rest):
        (du_ref, dd_ref, dar_ref, dai_ref, dbb_ref, dcc_ref,
         xr, xi, lr, li, fr, fi, br, bi, accr, acci, bbd, ccd, dbbd, dccd) = rest[n_dep:]
        c = pl.program_id(1)

        @pl.when(c == 0)
        def _():
            for ref in (br, bi, accr, acci, dd_ref, dbbd, dccd):
                ref[...] = jnp.zeros_like(ref)
            _s5_expand(bb_ref, cc_ref, bbd, ccd, sbk)

        for s in range(sbk):
            ub = u_ref[:, s * S5_CH:(s + 1) * S5_CH].astype(BF16)
            _to_state_tiles(xr, s, _dot(ub, bbd[0, s]))
            _to_state_tiles(xi, s, _dot(ub, bbd[1, s]))
        fr[...] = h_ref[0]
        fi[...] = h_ref[1]
        _s5_scan_fwd(xr, xi, ar_ref, ai_ref, fr, fi, tc, sbk)
        for s in range(sbk):
            cols = slice(s * S5_CH, (s + 1) * S5_CH)
            uv = u_ref[:, cols]
            xrb = _from_state_tiles(xr, s, tc).astype(BF16)
            xib = _from_state_tiles(xi, s, tc).astype(BF16)
            dsk = d_ref[:, cols]
            y = _dot(xrb, ccd[0, s]) - _dot(xib, ccd[1, s]) + dsk * uv
            dy = dz_ref[:, cols] * _gelu_grad(y)
            dd_ref[:, cols] += jnp.sum(dy * uv, axis=0, keepdims=True)
            dyb = dy.astype(BF16)
            dccd[0, s] += _dot(xrb, dyb, TN)
            dccd[1, s] += _dot(xib, dyb, TN)
            _to_state_tiles(lr, s, _dot(dyb, ccd[0, s], NT))
            _to_state_tiles(li, s, -_dot(dyb, ccd[1, s], NT))
            du_ref[:, cols] = dy * dsk
        _s5_scan_bwd(lr, li, xr, xi, h_ref, ar_ref, ai_ref, br, bi, accr, acci, tc, sbk)
        for s in range(sbk):
            cols = slice(s * S5_CH, (s + 1) * S5_CH)
            ub = u_ref[:, cols].astype(BF16)
            lrb = _from_state_tiles(lr, s, tc).astype(BF16)
            lib = _from_state_tiles(li, s, tc).astype(BF16)
            dbbd[0, s] += _dot(ub, lrb, TN)
            dbbd[1, s] += _dot(ub, lib, TN)
            du_ref[:, cols] += _dot(lrb, bbd[0, s], NT) + _dot(lib, bbd[1, s], NT)

        @pl.when(c == nch - 1)
        def _():
            dar_ref[...] = accr[...]
            dai_ref[...] = acci[...]
            mask_b, mask_c = _s5_group_masks()
            for k in range(2):
                for s in range(sbk):
                    mb = jnp.where(mask_b, dbbd[k, s], 0.0)
                    fold = functools.reduce(
                        lambda a, b: a + b, [mb[:, 128 * j:128 * (j + 1)] for j in range(S5_W // 128)])
                    dbb_ref[k, s] = fold + pltpu.roll(fold, S5_P, 1)
                    mc = jnp.where(mask_c, dccd[k, s], 0.0)
                    dcc_ref[k, s] = functools.reduce(
                        lambda a, b: a + b, [mc[S5_P * j:S5_P * (j + 1), :] for j in range(S5_GB)])

    tiles = pltpu.VMEM((sbk, tc * 8, 128), F32)
    carry = pltpu.VMEM((sbk, 8, 128), F32)
    ins = [u, dz, h0, bb2, cc2, a_re, a_im, d_skip]
    in_specs = [sp["act"], sp["act"], sp["h"], sp["bb"], sp["cc"], sp["a"], sp["a"], sp["d"]]
    _dep_operand(ins, in_specs, dep)
    return pl.pallas_call(
        body, grid=(nb // sbk, nch),
        in_specs=in_specs,
        out_specs=[sp["act"], sp["d"], sp["a"], sp["a"], sp["bb"], sp["cc"]],
        out_shape=[jax.ShapeDtypeStruct((T, D), F32), jax.ShapeDtypeStruct((1, D), F32),
                   jax.ShapeDtypeStruct((nb, 8, 128), F32), jax.ShapeDtypeStruct((nb, 8, 128), F32),
                   jax.ShapeDtypeStruct((2, nb, S5_CH, 128), F32), jax.ShapeDtypeStruct((2, nb, S5_P, S5_CH), F32)],
        scratch_shapes=[tiles, tiles, tiles, tiles, carry, carry, carry, carry, carry, carry,
                        pltpu.VMEM((2, sbk, S5_CH, S5_W), BF16), pltpu.VMEM((2, sbk, S5_W, S5_CH), BF16),
                        pltpu.VMEM((2, sbk, S5_CH, S5_W), F32), pltpu.VMEM((2, sbk, S5_W, S5_CH), F32)],
        compiler_params=_params(("parallel", "arbitrary")), name="s5_bwd",
    )(*ins)


def _s5_disc(lr, li, ldt):
    dt = jnp.exp(ldt)
    mag = jnp.exp(lr * dt)
    ang = li * dt
    cs, sn = jnp.cos(ang), jnp.sin(ang)
    lbr, lbi = mag * cs, mag * sn
    nr = lbr - 1.0
    den = lr * lr + li * li
    f_re = (nr * lr + lbi * li) / den
    f_im = (lbi * lr - nr * li) / den
    return dt, mag, cs, sn, lbr, lbi, nr, den, f_re, f_im


def _s5_param_fwd(lr, li, ldt, bt_re, bt_im):
    c, g, p = bt_re.shape

    def body(lr_ref, li_ref, ldt_ref, br_ref, bi_ref, lbr_ref, lbi_ref, bbr_ref, bbi_ref):
        _, _, _, _, lbr, lbi, _, _, f_re, f_im = _s5_disc(lr_ref[...], li_ref[...], ldt_ref[...])
        lbr_ref[...] = lbr
        lbi_ref[...] = lbi
        for ch in range(c):
            b_r, b_i = br_ref[ch], bi_ref[ch]
            bbr_ref[ch] = f_re * b_r - f_im * b_i
            bbi_ref[ch] = f_re * b_i + f_im * b_r

    gp = jax.ShapeDtypeStruct((g, p), F32)
    cgp = jax.ShapeDtypeStruct((c, g, p), F32)
    return pl.pallas_call(body, out_shape=[gp, gp, cgp, cgp], name="s5_param_fwd")(lr, li, ldt, bt_re, bt_im)


def _s5_param_bwd(lr, li, ldt, bt_re, bt_im, dlbr, dlbi, dbbr, dbbi):
    c, g, p = bt_re.shape

    def body(lr_ref, li_ref, ldt_ref, br_ref, bi_ref, dlbr_ref, dlbi_ref, dbbr_ref, dbbi_ref,
             dlr_ref, dli_ref, dldt_ref, dbr_ref, dbi_ref):
        l_r, l_i = lr_ref[...], li_ref[...]
        dt, mag, cs, sn, lbr, lbi, nr, den, f_re, f_im = _s5_disc(l_r, l_i, ldt_ref[...])
        dfr = jnp.zeros_like(l_r)
        dfi = jnp.zeros_like(l_r)
        for ch in range(c):
            b_r, b_i = br_ref[ch], bi_ref[ch]
            g_r, g_i = dbbr_ref[ch], dbbi_ref[ch]
            dbr_ref[ch] = f_re * g_r + f_im * g_i
            dbi_ref[ch] = f_re * g_i - f_im * g_r
            dfr = dfr + g_r * b_r + g_i * b_i
            dfi = dfi + g_i * b_r - g_r * b_i
        inv = 1.0 / den
        d_nr = (dfr * l_r - dfi * l_i) * inv
        d_lbi = (dfr * l_i + dfi * l_r) * inv + dlbi_ref[...]
        d_lbr = d_nr + dlbr_ref[...]
        d_den = -(dfr * f_re + dfi * f_im) * inv
        d_mag = d_lbr * cs + d_lbi * sn
        d_ang = d_lbi * lbr - d_lbr * lbi
        dlr_ref[...] = (dfr * nr + dfi * lbi) * inv + 2.0 * d_den * l_r + d_mag * mag * dt
        dli_ref[...] = (dfr * lbi - dfi * nr) * inv + 2.0 * d_den * l_i + d_ang * dt
        dldt_ref[...] = jnp.sum(d_mag * mag * l_r + d_ang * l_i, axis=1, keepdims=True) * dt

    gp = jax.ShapeDtypeStruct((g, p), F32)
    cgp = jax.ShapeDtypeStruct((c, g, p), F32)
    return pl.pallas_call(body, out_shape=[gp, gp, jax.ShapeDtypeStruct((g, 1), F32), cgp, cgp],
                          name="s5_param_bwd")(lr, li, ldt, bt_re, bt_im, dlbr, dlbi, dbbr, dbbi)


def _att_masks(rep, gb):
    rows = rep * ATT_BLK
    qi = lax.broadcasted_iota(jnp.int32, (rows, 2 * ATT_BLK), 0) % ATT_BLK
    si = lax.broadcasted_iota(jnp.int32, (rows, 2 * ATT_BLK), 1)
    prev = (si < ATT_BLK) & (si >= qi) & (gb > 0)
    cur = (si >= ATT_BLK) & (si - ATT_BLK <= qi)
    return prev | cur


def _att_rows(start, dil):
    return pl.ds(start, ATT_BLK) if dil == 1 else pl.ds(start, ATT_BLK, stride=dil)


def _att_plan(T, dil):
    span = ATT_BLK * dil
    sbr = max(span, min(T, 1024))
    return span, sbr, T // sbr


def _att_block(sb, i, sbr, span, dil):
    loc = (i // dil) * span + i % dil
    cur = sb * sbr + loc
    gb = sb * (sbr // span) + i // dil
    return loc, cur, jnp.where(gb > 0, cur - span, cur), gb


def _att_fwd(q, k, v, grp, dil):
    T = q.shape[0]
    H = q.shape[1] // HEAD_DIM // len(DILATIONS)
    rep = H // N_KV_HEADS
    span, sbr, nsb = _att_plan(T, dil)
    scale = HEAD_DIM ** -0.5

    def body(*refs):
        q_refs = refs[:rep]
        k_ref, v_ref, o_ref, l_ref, o_slab, l_slab = refs[rep:]
        sb = pl.program_id(1)

        def blk(i, _):
            loc, cur, prv, gb = _att_block(sb, i, sbr, span, dil)
            rows = _att_rows(loc, dil)
            qs = jnp.concatenate([r[rows, :] for r in q_refs], axis=0).astype(BF16)
            kcat = jnp.concatenate([k_ref[_att_rows(prv, dil), :], k_ref[_att_rows(cur, dil), :]], axis=0)
            vcat = jnp.concatenate([v_ref[_att_rows(prv, dil), :], v_ref[_att_rows(cur, dil), :]], axis=0)
            s = jnp.where(_att_masks(rep, gb), _dot(qs, kcat.astype(BF16), NT) * scale, NEG_INF)
            m = jnp.max(s, axis=-1, keepdims=True)
            p = jnp.exp(s - m)
            l = jnp.sum(p, axis=-1, keepdims=True)
            o = _dot(p.astype(BF16), vcat.astype(BF16)) / l
            lse = jnp.broadcast_to(m + jnp.log(l), (rep * ATT_BLK, HEAD_DIM))
            for j in range(rep):
                o_slab[j, rows, :] = o[j * ATT_BLK:(j + 1) * ATT_BLK]
                l_slab[j, rows, :] = lse[j * ATT_BLK:(j + 1) * ATT_BLK]
            return 0

        lax.fori_loop(0, sbr // ATT_BLK, blk, 0)
        for j in range(rep):
            o_ref[:, j * HEAD_DIM:(j + 1) * HEAD_DIM] = o_slab[j]
            l_ref[:, j * HEAD_DIM:(j + 1) * HEAD_DIM] = l_slab[j]

    qspecs = [pl.BlockSpec((sbr, HEAD_DIM), functools.partial(lambda h, s, j: (s, grp * H + h * rep + j), j=j))
              for j in range(rep)]
    kspec = pl.BlockSpec((T, HEAD_DIM), lambda h, s: (0, h))
    ospec = pl.BlockSpec((sbr, rep * HEAD_DIM), lambda h, s: (s, h))
    slab = pltpu.VMEM((rep, sbr, HEAD_DIM), F32)
    return pl.pallas_call(
        body, grid=(N_KV_HEADS, nsb), in_specs=qspecs + [kspec, kspec], out_specs=[ospec, ospec],
        out_shape=[jax.ShapeDtypeStruct((T, H * HEAD_DIM), F32)] * 2, scratch_shapes=[slab, slab],
        compiler_params=_params(("parallel", "arbitrary")), name=f"att_fwd_d{dil}",
    )(*([q] * rep), k, v)


def _att_combine(outs, lses):
    T, W = outs[0].shape
    ng = len(outs)
    tr, tcol = _tile(T, 512, 8), _tile(W, 512)

    def body(*refs):
        o_refs, l_refs = refs[:ng], refs[ng:2 * ng]
        ob_ref, lse_ref = refs[2 * ng:]
        ls = [r[...] for r in l_refs]
        m = functools.reduce(jnp.maximum, ls)
        es = [jnp.exp(l - m) for l in ls]
        den = functools.reduce(lambda a, b: a + b, es)
        num = functools.reduce(lambda a, b: a + b, [e * o[...] for e, o in zip(es, o_refs)])
        ob_ref[...] = (num / den).astype(BF16)
        lse_ref[...] = m + jnp.log(den)

    spec = pl.BlockSpec((tr, tcol), lambda i, j: (i, j))
    return pl.pallas_call(
        body, grid=(T // tr, W // tcol), in_specs=[spec] * (2 * ng), out_specs=[spec, spec],
        out_shape=[jax.ShapeDtypeStruct((T, W), BF16), jax.ShapeDtypeStruct((T, W), F32)],
        compiler_params=_params(("parallel", "parallel")), name="att_combine")(*outs, *lses)


STAT_LANE = HEAD_DIM // 2


def _att_stats(lse, o, do):
    T, W = lse.shape
    tr = _tile(T, 256, 16)

    def body(l_ref, o_ref, do_ref, s_ref):
        lane = lax.broadcasted_iota(jnp.int32, (tr, HEAD_DIM), 1)
        for h in range(W // HEAD_DIM):
            cols = slice(h * HEAD_DIM, (h + 1) * HEAD_DIM)
            delta = jnp.sum(do_ref[:, cols] * o_ref[:, cols].astype(F32), axis=-1, keepdims=True)
            s_ref[:, cols] = jnp.where(lane < STAT_LANE, l_ref[:, cols], delta)

    spec = pl.BlockSpec((tr, W), lambda i: (i, 0))
    return pl.pallas_call(
        body, grid=(T // tr,), in_specs=[spec] * 3, out_specs=spec,
        out_shape=jax.ShapeDtypeStruct((T, W), F32),
        compiler_params=_params(("parallel",)), name="att_stats")(lse, o, do)


def _att_bwd(q, k, v, do, stats, dq, grp, dil):
    T = q.shape[0]
    H = do.shape[1] // HEAD_DIM
    rep = H // N_KV_HEADS
    hs = min(rep, 2)
    span, sbr, nsb = _att_plan(T, dil)
    scale = HEAD_DIM ** -0.5

    def body(*refs):
        q_refs, do_refs, st_refs = refs[:hs], refs[hs + 2:2 * hs + 2], refs[2 * hs + 2:3 * hs + 2]
        k_ref, v_ref = refs[hs], refs[hs + 1]
        dq_ref, dk_ref, dv_ref, dq_slab = refs[3 * hs + 3:]
        sb = pl.program_id(2)

        @pl.when((pl.program_id(1) == 0) & (sb == 0))
        def _():
            dk_ref[...] = jnp.zeros_like(dk_ref)
            dv_ref[...] = jnp.zeros_like(dv_ref)

        def blk(i, _):
            loc, cur, prv, gb = _att_block(sb, i, sbr, span, dil)
            rows, kc, kp = _att_rows(loc, dil), _att_rows(cur, dil), _att_rows(prv, dil)
            qs = jnp.concatenate([r[rows, :] for r in q_refs], axis=0).astype(BF16)
            dos = jnp.concatenate([r[rows, :] for r in do_refs], axis=0).astype(BF16)
            st = jnp.concatenate([r[rows, :] for r in st_refs], axis=0)
            kcat = jnp.concatenate([k_ref[kp, :], k_ref[kc, :]], axis=0).astype(BF16)
            vcat = jnp.concatenate([v_ref[kp, :], v_ref[kc, :]], axis=0).astype(BF16)
            s = _dot(qs, kcat, NT) * scale
            p = jnp.where(_att_masks(hs, gb), jnp.exp(s - st[:, 0:1]), 0.0)
            dp = _dot(dos, vcat, NT)
            ds = (p * (dp - st[:, STAT_LANE:STAT_LANE + 1]) * scale).astype(BF16)
            dvc = _dot(p.astype(BF16), dos, TN)
            dkc = _dot(ds, qs, TN)
            dqs = _dot(ds, kcat)
            for j in range(hs):
                dq_slab[j, rows, :] = dqs[j * ATT_BLK:(j + 1) * ATT_BLK]
            dk_ref[kc, :] += dkc[ATT_BLK:]
            dv_ref[kc, :] += dvc[ATT_BLK:]

            @pl.when(gb > 0)
            def _():
                dk_ref[kp, :] += dkc[:ATT_BLK]
                dv_ref[kp, :] += dvc[:ATT_BLK]

            return 0

        lax.fori_loop(0, sbr // ATT_BLK, blk, 0)
        for j in range(hs):
            dq_ref[:, j * HEAD_DIM:(j + 1) * HEAD_DIM] = dq_slab[j].astype(BF16)

    def head_specs(col0):
        return [pl.BlockSpec((sbr, HEAD_DIM),
                             functools.partial(lambda h, f, s, j: (s, col0 + h * rep + f * hs + j), j=j))
                for j in range(hs)]

    kspec = pl.BlockSpec((T, HEAD_DIM), lambda h, f, s: (0, h))
    dqspec = pl.BlockSpec((sbr, hs * HEAD_DIM), lambda h, f, s: (s, (grp * H + h * rep) // hs + f))
    n_in = 3 * hs + 3
    return pl.pallas_call(
        body, grid=(N_KV_HEADS, rep // hs, nsb),
        in_specs=head_specs(grp * H) + [kspec, kspec] + head_specs(0) + head_specs(0) + [ANY],
        out_specs=[dqspec, kspec, kspec],
        out_shape=[jax.ShapeDtypeStruct(dq.shape, BF16),
                   jax.ShapeDtypeStruct((T, N_KV_HEADS * HEAD_DIM), F32),
                   jax.ShapeDtypeStruct((T, N_KV_HEADS * HEAD_DIM), F32)],
        scratch_shapes=[pltpu.VMEM((hs, sbr, HEAD_DIM), F32)],
        input_output_aliases={n_in - 1: 0},
        compiler_params=_params(("parallel", "arbitrary", "arbitrary")), name=f"att_bwd_d{dil}",
    )(*([q] * hs), k, v, *([do] * hs), *([stats] * hs), dq)


def _sum_kv(dks, dvs):
    T, W = dks[0].shape
    ng = len(dks)
    tr = _tile(T, 512, 8)

    def body(*refs):
        o_ref = refs[2 * ng]
        o_ref[0] = functools.reduce(lambda a, b: a + b, [r[...] for r in refs[:ng]]).astype(BF16)
        o_ref[1] = functools.reduce(lambda a, b: a + b, [r[...] for r in refs[ng:2 * ng]]).astype(BF16)

    spec = pl.BlockSpec((tr, W), lambda i: (i, 0))
    return pl.pallas_call(
        body, grid=(T // tr,), in_specs=[spec] * (2 * ng),
        out_specs=pl.BlockSpec((2, tr, W), lambda i: (0, i, 0)),
        out_shape=jax.ShapeDtypeStruct((2, T, W), BF16),
        compiler_params=_params(("parallel",)), name="sum_kv")(*dks, *dvs)


def _local_step(x, tgt, w, wts, ready, dep0=None, small_ready=None):
    T, D = x.shape
    g = {}

    (u0,) = _rms_fwd("rms_a", x, [w["a_norm"]], [F32], dep=dep0)
    lbr, lbi, bbt_re, bbt_im = _s5_param_fwd(w["lam_re"], w["lam_im"], w["log_dt"], w["bt_re"], w["bt_im"])
    a_re, a_im = lbr.reshape(-1, 8, 128), lbi.reshape(-1, 8, 128)
    C, G, P = w["bt_re"].shape
    nb = G // S5_GB
    bb2 = jnp.stack([bbt_re, bbt_im]).transpose(0, 2, 1, 3).reshape(2, nb, S5_GB * C, P)
    bb2 = jnp.concatenate([bb2, bb2], axis=-1)
    cc2 = jnp.stack([w["c_re"], w["c_im"]]).reshape(2, nb, S5_GB * C, P).transpose(0, 1, 3, 2)
    z, h0 = _s5_fwd(u0, bb2, cc2, a_re, a_im, w["s5_d"])
    w_glu = wts("glu", z)
    x1, vg = _mm_dual_fwd("glu_fwd", z, w_glu, x, "glu")

    def ffn_fwd(xin, layer):
        (nrm,) = _rms_fwd(f"rms_f{layer}", xin, [w["ffn_norm"][layer:layer + 1]], [BF16])
        w_in = wts(f"win{layer}", nrm)
        gu, act = _mm_dual_fwd(f"ffn_in{layer}", nrm, w_in, None, "ffn")
        w_out = wts(f"wout{layer}", act)
        xout = _mm_nn(f"ffn_out{layer}", act, w_out, res=xin)
        return xout, (nrm, gu, act, w_in, w_out)

    x2, saved0 = ffn_fwd(x1, 0)
    kvn, hb = _rms_fwd("rms_b", x2, [w["kv_norm"], w["b_norm"]], [BF16, BF16])
    w_kv, w_q, w_o = wts("wkv", hb), wts("wq", hb), wts("wo", hb)
    k, v = _mm_kv("kv_proj", kvn, w_kv)
    q = _mm_nn("q_proj", hb, w_q)
    outs, lses = [], []
    for grp, dil in enumerate(DILATIONS):
        o_g, l_g = _att_fwd(q, k, v, grp, dil)
        outs.append(o_g)
        lses.append(l_g)
    o, lse = _att_combine(outs, lses)
    x3 = _mm_nn("o_proj", o, w_o, res=x2)
    x4, saved1 = ffn_fwd(x3, 1)
    loss_blk, dx4, dx4b, g["final_norm"] = _loss_head(x4, w["final_norm"], tgt)

    def ffn_bwd(dx, dxb, xin, saved, layer, dep):
        nrm, gu, act, w_in, w_out = saved
        dgu = _mm_nt_ffn_bwd(f"ffn_dact{layer}", dxb, w_out, gu, dep=dep)
        g_wout = _mm_tn(f"ffn_dwout{layer}", act, dxb)
        g_win = _mm_tn_pair(f"ffn_dwin{layer}", nrm, dgu)
        dn = _mm_nt_pair(f"ffn_dn{layer}", dgu, w_in)
        dxo, dxob, (dgn,) = _rms_bwd(f"rms_f{layer}_bwd", xin, dx, [w["ffn_norm"][layer:layer + 1]], [dn])
        tok = ready({f"win{layer}": g_win, f"wout{layer}": g_wout})
        return dxo, dxob, dgn, tok

    dx3, dx3b, dfn1, tok = ffn_bwd(dx4, dx4b, x3, saved1, 1, None)
    do = _mm_nt("o_proj_dx", [dx3b], w_o, dep=tok)
    g_wo = _mm_tn("o_proj_dw", o, dx3b)
    stats = _att_stats(lse, o, do)
    dq = lax.empty(q.shape, BF16)
    dks, dvs = [], []
    for grp, dil in enumerate(DILATIONS):
        dq, dk_g, dv_g = _att_bwd(q, k, v, do, stats, dq, grp, dil)
        dks.append(dk_g)
        dvs.append(dv_g)
    dkv = _sum_kv(dks, dvs)
    dhb = _mm_nt("q_proj_dx", [dq], w_q)
    g_wq = _mm_tn("q_proj_dw", hb, dq)
    dkvn = _mm_nt_pair("kv_proj_dx", dkv, w_kv)
    g_wkv = _mm_tn_pair("kv_proj_dw", kvn, dkv)
    dx2, dx2b, (g["kv_norm"], g["b_norm"]) = _rms_bwd(
        "rms_b_bwd", x2, dx3, [w["kv_norm"], w["b_norm"]], [dkvn, dhb])
    tok = ready({"wkv": g_wkv, "wq": g_wq, "wo": g_wo})
    dx1, dx1b, dfn0, tok = ffn_bwd(dx2, dx2b, x1, saved0, 0, tok)
    g["ffn_norm"] = jnp.concatenate([dfn0, dfn1], axis=0)

    dvg = _glu_bwd(dx1, vg, dep=tok)
    dz = _mm_nt_pair("glu_dx", dvg, w_glu)
    tok = ready({"glu": _mm_tn_pair("glu_dw", z, dvg)})
    tok = ready({}, after=tok)
    du, g["s5_d"], da_re, da_im, dbb2, dcc2 = _s5_bwd(
        u0, dz, h0, bb2, cc2, a_re, a_im, w["s5_d"], dep=tok)
    dcc = dcc2.transpose(0, 1, 3, 2).reshape(2, G, C, P)
    g["c_re"], g["c_im"] = dcc[0], -dcc[1]
    dbbt = dbb2[..., :P].reshape(2, G, C, P).transpose(0, 2, 1, 3)
    g["lam_re"], g["lam_im"], g["log_dt"], g["bt_re"], g["bt_im"] = _s5_param_bwd(
        w["lam_re"], w["lam_im"], w["log_dt"], w["bt_re"], w["bt_im"],
        da_re.reshape(G, P), da_im.reshape(G, P), dbbt[0], dbbt[1])
    tok = small_ready(g) if small_ready is not None else None
    grad_x, _, (g["a_norm"],) = _rms_bwd("rms_a_bwd", x, dx1, [w["a_norm"]], [du], dep=tok)
    return loss_blk, grad_x, g


def _coords():
    return lax.axis_index("x"), lax.axis_index("y"), lax.axis_index("c")


def _dev_index(dev):
    return 4 * dev[0] + 2 * dev[1] + dev[2]


def _shard_window(ref, axis, width, idx):
    sl = [slice(None)] * len(ref.shape)
    sl[axis] = pl.ds(pl.multiple_of(idx * width, width), width)
    return ref.at[tuple(sl)]


def _all_gather(name, shards, axes):
    na = len(shards)
    widths = [s.shape[ax] for s, ax in zip(shards, axes)]
    out_shapes = []
    for s, ax in zip(shards, axes):
        shp = list(s.shape)
        shp[ax] *= N_DEV
        out_shapes.append(jax.ShapeDtypeStruct(tuple(shp), s.dtype))

    def body(*refs):
        ins, outs = refs[:na], refs[na:2 * na]
        send_sems, recv_sems, local_sems = refs[2 * na:]
        x, y, c = _coords()
        me, sib = (x, y, c), (x, y, 1 - c)
        chips = [(1 - x, y), (x, 1 - y), (1 - x, 1 - y)]

        def blk(a, dev):
            return _shard_window(outs[a], axes[a], widths[a], _dev_index(dev))

        def copy(a, kk, block, to, src=None):
            return pltpu.make_async_remote_copy(
                src_ref=blk(a, block) if src is None else src, dst_ref=blk(a, block),
                send_sem=send_sems.at[a, kk], recv_sem=recv_sems.at[a, kk],
                device_id=to, device_id_type=MESH)

        local = [pltpu.make_async_copy(ins[a], blk(a, me), local_sems.at[a]) for a in range(na)]
        for cp in local:
            cp.start()
        sent = []
        for a in range(na):
            first = [copy(a, 0, me, sib, src=ins[a])]
            first += [copy(a, 1 + j, me, (*chip, c), src=ins[a]) for j, chip in enumerate(chips)]
            for cp in first:
                cp.start()
            sent += first
        for a in range(na):
            for j, chip in enumerate(chips):
                copy(a, 1 + j, (*chip, c), me).wait_recv()
                fwd = copy(a, 4 + j, (*chip, c), sib)
                fwd.start()
                sent.append(fwd)
        for a in range(na):
            copy(a, 0, sib, me).wait_recv()
            for j, chip in enumerate(chips):
                copy(a, 4 + j, (*chip, 1 - c), me).wait_recv()
        for cp in sent:
            cp.wait_send()
        for cp in local:
            cp.wait()

    return pl.pallas_call(
        body, out_shape=out_shapes, in_specs=[ANY] * na, out_specs=[ANY] * na,
        scratch_shapes=[pltpu.SemaphoreType.DMA((na, 7)), pltpu.SemaphoreType.DMA((na, 7)),
                        pltpu.SemaphoreType.DMA((na,))],
        name=name)(*shards)


HBM = pl.BlockSpec(memory_space=pltpu.HBM)
SEM = pl.BlockSpec(memory_space=pltpu.SEMAPHORE)
TOKEN_SPEC = pl.BlockSpec(memory_space=pltpu.VMEM)
TOKEN_SHAPE = jax.ShapeDtypeStruct((8, 128), F32)
SPLIT_PARAMS = pltpu.CompilerParams(has_side_effects=pltpu.SideEffectType.DATAFLOW_SIDE_EFFECTING)


def _hbm(x):
    return pltpu.with_memory_space_constraint(x, pltpu.HBM)


def _hbm_like(x):
    return pltpu.HBM(x.shape, x.dtype)


def _dma_sems(*shape):
    return pltpu.SemaphoreType.DMA(shape)


def _cast_and_place(name, shard, layer, axis, pos, dtype):
    rows, cols = shard.shape[-2:]
    tr = _tile(rows, 256, 16)
    nt = rows // tr
    full = (rows, cols * N_DEV) if axis == 1 else (rows * N_DEV, cols)

    def dev(p):
        return 4 * p[0] + 2 * p[1] + p[2]

    def body(pos_ref, s_ref, b_ref, l_ref):
        v = s_ref[...].astype(dtype)
        b_ref[...] = v
        l_ref[...] = v

    blk = pl.BlockSpec((tr, cols), lambda i, p: (i, 0))
    if axis == 1:
        lspec = pl.BlockSpec((tr, cols), lambda i, p: (i, dev(p)))
    else:
        lspec = pl.BlockSpec((tr, cols), lambda i, p: (dev(p) * nt + i, 0))
    return pl.pallas_call(
        body, grid_spec=pltpu.PrefetchScalarGridSpec(
            num_scalar_prefetch=1, grid=(nt,),
            in_specs=[pl.BlockSpec((None, tr, cols), lambda i, p: (layer, i, 0))], out_specs=[blk, lspec]),
        out_shape=[jax.ShapeDtypeStruct((rows, cols), dtype), jax.ShapeDtypeStruct(full, dtype)],
        compiler_params=_params(("parallel",)), name=name)(pos, shard)


def _gather_start(name, shards, lands, axes, groups):
    na, ng = len(shards), len(groups)
    widths = [s.shape[ax] for s, ax in zip(shards, axes)]

    def body(*refs):
        sh, ld = refs[:na], refs[na:2 * na]
        sems = refs[2 * na:2 * na + 3 * ng]
        token = refs[-1]
        x, y, c = _coords()
        me, sib = (x, y, c), (x, y, 1 - c)
        chips = [(1 - x, y), (x, 1 - y), (1 - x, 1 - y)]
        for gi, grp in enumerate(groups):
            send, r_d2d, r_ici = sems[3 * gi:3 * gi + 3]
            for li, a in enumerate(grp):
                dst = _shard_window(ld[a], axes[a], widths[a], _dev_index(me))
                pltpu.make_async_remote_copy(
                    src_ref=sh[a], dst_ref=dst, send_sem=send.at[4 * li], recv_sem=r_d2d.at[li],
                    device_id=sib, device_id_type=MESH).start()
                for j, chip in enumerate(chips):
                    pltpu.make_async_remote_copy(
                        src_ref=sh[a], dst_ref=dst, send_sem=send.at[4 * li + 1 + j], recv_sem=r_ici.at[3 * li + j],
                        device_id=(*chip, c), device_id_type=MESH).start()
        token[...] = jnp.zeros_like(token)

    out_shape, out_specs = [], []
    for grp in groups:
        out_shape += [_dma_sems(4 * len(grp)), _dma_sems(len(grp)), _dma_sems(3 * len(grp))]
        out_specs += [SEM] * 3
    out_shape += [_hbm_like(s) for s in shards] + [_hbm_like(l) for l in lands] + [TOKEN_SHAPE]
    out_specs += [HBM] * (2 * na) + [TOKEN_SPEC]
    aliases = {a: 3 * ng + a for a in range(2 * na)}
    res = pl.pallas_call(
        body, name=name, out_shape=out_shape, in_specs=[HBM] * (2 * na),
        out_specs=out_specs, input_output_aliases=aliases, compiler_params=SPLIT_PARAMS,
    )(*[_hbm(s) for s in shards], *[_hbm(l) for l in lands])
    sems = [tuple(res[3 * gi:3 * gi + 3]) for gi in range(ng)]
    return sems, list(res[3 * ng:3 * ng + na]), list(res[3 * ng + na:3 * ng + 2 * na]), res[-1]


def _gather_forward(name, lands, axes, r_ici, after):
    n = len(lands)
    widths = [l.shape[ax] // N_DEV for l, ax in zip(lands, axes)]

    def body(*refs):
        ld, r_ici_ref = refs[:n], refs[n]
        f_send, f_recv = refs[n + 2], refs[n + 3]
        x, y, c = _coords()
        sib = (x, y, 1 - c)
        chips = [(1 - x, y), (x, 1 - y), (1 - x, 1 - y)]
        for li in range(n):
            for j, chip in enumerate(chips):
                blk = _shard_window(ld[li], axes[li], widths[li], _dev_index((*chip, c)))
                pltpu.make_async_remote_copy(
                    src_ref=blk, dst_ref=blk, send_sem=f_send.at[3 * li + j], recv_sem=r_ici_ref.at[3 * li + j],
                    device_id=(*chip, c), device_id_type=MESH).wait_recv()
                pltpu.make_async_remote_copy(
                    src_ref=blk, dst_ref=blk, send_sem=f_send.at[3 * li + j], recv_sem=f_recv.at[3 * li + j],
                    device_id=sib, device_id_type=MESH).start()

    res = pl.pallas_call(
        body, name=name, out_shape=[_dma_sems(3 * n), _dma_sems(3 * n)] + [_hbm_like(l) for l in lands],
        in_specs=[HBM] * n + [SEM, ANY], out_specs=[SEM, SEM] + [HBM] * n,
        input_output_aliases={li: 2 + li for li in range(n)}, compiler_params=SPLIT_PARAMS,
    )(*lands, r_ici, after)
    return res[0], res[1], list(res[2:])


def _gather_finish(name, shards, lands, axes, send, r_d2d, f_send, f_recv):
    n = len(lands)
    widths = [l.shape[ax] // N_DEV for l, ax in zip(lands, axes)]

    def body(*refs):
        sh, ld = refs[:n], refs[n:2 * n]
        send_ref, r_d2d_ref, f_send_ref, f_recv_ref = refs[2 * n:2 * n + 4]
        x, y, c = _coords()
        me, sib = (x, y, c), (x, y, 1 - c)
        chips = [(1 - x, y), (x, 1 - y), (1 - x, 1 - y)]

        def blk(li, dev):
            return _shard_window(ld[li], axes[li], widths[li], _dev_index(dev))

        for li in range(n):
            for kk in range(4):
                pltpu.make_async_remote_copy(
                    src_ref=sh[li], dst_ref=blk(li, me), send_sem=send_ref.at[4 * li + kk], recv_sem=r_d2d_ref.at[li],
                    device_id=sib, device_id_type=MESH).wait_send()
            pltpu.make_async_remote_copy(
                src_ref=blk(li, sib), dst_ref=blk(li, sib), send_sem=send_ref.at[4 * li], recv_sem=r_d2d_ref.at[li],
                device_id=sib, device_id_type=MESH).wait_recv()
            for j, chip in enumerate(chips):
                pltpu.make_async_remote_copy(
                    src_ref=blk(li, (*chip, c)), dst_ref=blk(li, (*chip, c)), send_sem=f_send_ref.at[3 * li + j],
                    recv_sem=f_recv_ref.at[3 * li + j], device_id=sib, device_id_type=MESH).wait_send()
                pltpu.make_async_remote_copy(
                    src_ref=blk(li, (*chip, 1 - c)), dst_ref=blk(li, (*chip, 1 - c)), send_sem=f_send_ref.at[3 * li + j],
                    recv_sem=f_recv_ref.at[3 * li + j], device_id=sib, device_id_type=MESH).wait_recv()

    res = pl.pallas_call(
        body, name=name, out_shape=[_hbm_like(s) for s in shards] + [_hbm_like(l) for l in lands],
        in_specs=[HBM] * (2 * n) + [SEM] * 4, out_specs=[HBM] * (2 * n),
        input_output_aliases={i: i for i in range(2 * n)}, compiler_params=SPLIT_PARAMS,
    )(*shards, *lands, send, r_d2d, f_send, f_recv)
    return list(res[n:])


def _chip_exchange_start(name, parts):
    n = len(parts)

    def body(*refs):
        src, ld = refs[:n], refs[n:2 * n]
        send, recv = refs[2 * n], refs[2 * n + 1]
        token = refs[-1]
        x, y, c = _coords()
        chips = [(1 - x, y), (x, 1 - y), (1 - x, 1 - y)]
        for li in range(n):
            for kk, chip in enumerate(chips):
                pltpu.make_async_remote_copy(
                    src_ref=src[li].at[kk], dst_ref=ld[li].at[kk], send_sem=send.at[3 * li + kk],
                    recv_sem=recv.at[3 * li + kk], device_id=(*chip, c), device_id_type=MESH).start()
        token[...] = jnp.zeros_like(token)

    lands = [lax.empty(p.shape, p.dtype) for p in parts]
    res = pl.pallas_call(
        body, name=name,
        out_shape=[_dma_sems(3 * n), _dma_sems(3 * n)] + [_hbm_like(p) for p in parts] * 2 + [TOKEN_SHAPE],
        in_specs=[HBM] * (2 * n), out_specs=[SEM, SEM] + [HBM] * (2 * n) + [TOKEN_SPEC],
        input_output_aliases={i: 2 + i for i in range(2 * n)}, compiler_params=SPLIT_PARAMS,
    )(*[_hbm(p) for p in parts], *[_hbm(l) for l in lands])
    return res[0], res[1], list(res[2:2 + n]), list(res[2 + n:2 + 2 * n]), res[-1]


def _chip_exchange_finish(name, started, after):
    counts = [len(st[2]) for st in started]
    total = sum(counts)
    ns = len(started)

    def body(*refs):
        src, ld = refs[:total], refs[total:2 * total]
        sems = refs[2 * total:2 * total + 2 * ns]
        x, y, c = _coords()
        chips = [(1 - x, y), (x, 1 - y), (1 - x, 1 - y)]
        off = 0
        for si, cnt in enumerate(counts):
            send, recv = sems[2 * si], sems[2 * si + 1]
            for li in range(cnt):
                for kk, chip in enumerate(chips):
                    cp = pltpu.make_async_remote_copy(
                        src_ref=src[off + li].at[kk], dst_ref=ld[off + li].at[kk], send_sem=send.at[3 * li + kk],
                        recv_sem=recv.at[3 * li + kk], device_id=(*chip, c), device_id_type=MESH)
                    cp.wait_send()
                    cp.wait_recv()
            off += cnt

    srcs = [p for st in started for p in st[2]]
    lands = [l for st in started for l in st[3]]
    sems = [s for st in started for s in st[:2]]
    res = pl.pallas_call(
        body, name=name, out_shape=[_hbm_like(p) for p in srcs + lands],
        in_specs=[HBM] * (2 * total) + [SEM] * (2 * ns) + [ANY], out_specs=[HBM] * (2 * total),
        input_output_aliases={i: i for i in range(2 * total)}, compiler_params=SPLIT_PARAMS,
    )(*srcs, *lands, *sems, after)
    out, off = [], total
    for cnt in counts:
        out.append(list(res[off:off + cnt]))
        off += cnt
    return out


def _pair_copies(src, ld, send, recv, axes, widths):
    x, y, c = _coords()
    chips = [(x, y), (1 - x, y), (x, 1 - y), (1 - x, 1 - y)]
    return [pltpu.make_async_remote_copy(
        src_ref=_shard_window(src[li], axes[li], widths[li], _dev_index((*chip, 1 - c))),
        dst_ref=ld[li].at[kk], send_sem=send.at[4 * li + kk], recv_sem=recv.at[4 * li + kk],
        device_id=(x, y, 1 - c), device_id_type=MESH)
        for li in range(len(src)) for kk, chip in enumerate(chips)]


def _pair_exchange_start(name, grads, axes):
    n = len(grads)
    widths = [gr.shape[ax] // N_DEV for gr, ax in zip(grads, axes)]
    lands = []
    for gr, ax, wd in zip(grads, axes, widths):
        shp = list(gr.shape)
        shp[ax] = wd
        lands.append(lax.empty((4, *shp), gr.dtype))

    def body(*refs):
        for cp in _pair_copies(refs[:n], refs[n:2 * n], refs[2 * n], refs[2 * n + 1], axes, widths):
            cp.start()
        refs[-1][...] = jnp.zeros_like(refs[-1])

    res = pl.pallas_call(
        body, name=name,
        out_shape=[_dma_sems(4 * n), _dma_sems(4 * n)] + [_hbm_like(a) for a in grads + lands] + [TOKEN_SHAPE],
        in_specs=[HBM] * (2 * n), out_specs=[SEM, SEM] + [HBM] * (2 * n) + [TOKEN_SPEC],
        input_output_aliases={i: 2 + i for i in range(2 * n)}, compiler_params=SPLIT_PARAMS,
    )(*[_hbm(a) for a in grads + lands])
    return res[0], res[1], list(res[2:2 + n]), list(res[2 + n:2 + 2 * n]), res[-1]


def _pair_exchange_finish(name, started, axes, after):
    send, recv, grads, lands, _ = started
    n = len(grads)
    widths = [gr.shape[ax] // N_DEV for gr, ax in zip(grads, axes)]

    def body(*refs):
        for cp in _pair_copies(refs[:n], refs[n:2 * n], refs[2 * n], refs[2 * n + 1], axes, widths):
            cp.wait_send()
            cp.wait_recv()

    res = pl.pallas_call(
        body, name=name, out_shape=[_hbm_like(a) for a in grads + lands],
        in_specs=[HBM] * (2 * n) + [SEM, SEM, ANY], out_specs=[HBM] * (2 * n),
        input_output_aliases={i: i for i in range(2 * n)}, compiler_params=SPLIT_PARAMS,
    )(*grads, *lands, send, recv, after)
    return list(res[:n]), list(res[n:])


def _pair_sum(name, grad, land, axis, pos):
    wd = grad.shape[axis] // N_DEV
    shard_shape = land.shape[1:]
    rows, cols = shard_shape
    tr = _tile(rows, 256, 16)
    nt = rows // tr

    def dev_of(kk, pos_ref):
        return 4 * (pos_ref[0] ^ (kk & 1)) + 2 * (pos_ref[1] ^ (kk >> 1)) + pos_ref[2]

    if axis == 1:
        gspec = pl.BlockSpec((tr, wd), lambda kk, t, p: (t, dev_of(kk, p)))
    else:
        gspec = pl.BlockSpec((tr, cols), lambda kk, t, p: (dev_of(kk, p) * nt + t, 0))
    lspec = pl.BlockSpec((None, tr, cols), lambda kk, t, p: (kk, t, 0))
    p0spec = pl.BlockSpec((tr, cols), lambda kk, t, p: (jnp.where(kk == 0, t, nt - 1), 0))
    pbspec = pl.BlockSpec((None, tr, cols), lambda kk, t, p: (jnp.maximum(kk - 1, 0), jnp.where(kk == 0, 0, t), 0))

    def body(pos_ref, g_ref, l_ref, p0_ref, pb_ref):
        kk = pl.program_id(0)
        s = g_ref[...].astype(F32) + l_ref[...].astype(F32)

        @pl.when(kk == 0)
        def _():
            p0_ref[...] = s

        @pl.when(kk > 0)
        def _():
            pb_ref[...] = s.astype(BF16)

    return pl.pallas_call(
        body,
        grid_spec=pltpu.PrefetchScalarGridSpec(
            num_scalar_prefetch=1, grid=(4, nt), in_specs=[gspec, lspec], out_specs=[p0spec, pbspec]),
        out_shape=[jax.ShapeDtypeStruct(shard_shape, F32), jax.ShapeDtypeStruct((3, *shard_shape), BF16)],
        compiler_params=_params(("arbitrary", "arbitrary")), name=name)(pos, grad, land)


def _adamw(name, parts, w, m, v, layer=None, prev=None):
    rows, cols = w.shape[-2:]
    tr = _tile(rows, 256, 8)
    npart = len(parts)
    c1 = 1.0 - ADAM_B1 ** ADAM_STEP
    c2 = 1.0 - ADAM_B2 ** ADAM_STEP

    def body(*refs):
        p_refs = refs[:npart]
        w_ref, m_ref, v_ref = refs[npart:npart + 3]
        g_ref, d_ref, nm_ref, nv_ref = refs[-4:]
        g = None
        for r in p_refs:
            if len(r.shape) == 3:
                for i in range(r.shape[0]):
                    t = r[i].astype(F32)
                    g = t if g is None else g + t
            else:
                t = r[...].astype(F32)
                g = t if g is None else g + t
        nm = ADAM_B1 * m_ref[...] + (1.0 - ADAM_B1) * g
        nv = ADAM_B2 * v_ref[...] + (1.0 - ADAM_B2) * (g * g)
        g_ref[...] = g
        nm_ref[...] = nm
        nv_ref[...] = nv
        d_ref[...] = -ADAM_LR * ((nm / c1) / (jnp.sqrt(nv / c2) + ADAM_EPS) + ADAM_WD * w_ref[...])

    spec = pl.BlockSpec((tr, cols), lambda i: (i, 0))
    wspec = spec if layer is None else pl.BlockSpec((None, tr, cols), lambda i: (layer, i, 0))
    pspecs = [pl.BlockSpec((p.shape[0], tr, cols), lambda i: (0, i, 0)) if p.ndim == 3 else spec
              for p in parts]
    prev = list(prev) if prev else []
    return pl.pallas_call(
        body, grid=(rows // tr,), in_specs=pspecs + [wspec] * 3 + [ANY] * len(prev), out_specs=[wspec] * 4,
        out_shape=[jax.ShapeDtypeStruct(w.shape, F32)] * 4,
        input_output_aliases={npart + 3 + i: i for i in range(len(prev))},
        compiler_params=_params(("parallel",)), name=name)(*parts, w, m, v, *prev)


SMALL_NAMES = ("s5_lam_re", "s5_lam_im", "s5_log_dt", "s5_b_re", "s5_b_im", "s5_c_re", "s5_c_im",
               "ffn_norm", "b_norm_mix", "kv_norm", "final_norm")
SMALL_PAD = 1024


def _pack(parts):
    flat = []
    for p in parts:
        f = p.reshape(-1)
        pad = (-f.shape[0]) % SMALL_PAD
        if pad:
            f = jnp.concatenate([f, jnp.zeros((pad,), f.dtype)])
        flat.append(f)
    return jnp.concatenate(flat).reshape(-1, 128)


def _unpack(packed, shapes):
    flat = packed.reshape(-1)
    out, off = [], 0
    for shp in shapes:
        size = math.prod(shp)
        out.append(flat[off:off + size].reshape(shp))
        off += size + (-size) % SMALL_PAD
    return out


def kernel(x, s5_lam_re, s5_lam_im, s5_log_dt, s5_b_re, s5_b_im, s5_c_re, s5_c_im, s5_d, s5_w_glu, a_norm_mix, ffn_norm, ffn_w_in, ffn_w_out, b_norm_mix, attn_w_q, attn_w_o, kv_norm, w_kv, final_norm, loss_target, m_s5_lam_re, m_s5_lam_im, m_s5_log_dt, m_s5_b_re, m_s5_b_im, m_s5_c_re, m_s5_c_im, m_s5_d, m_s5_w_glu, m_a_norm_mix, m_ffn_norm, m_ffn_w_in, m_ffn_w_out, m_b_norm_mix, m_attn_w_q, m_attn_w_o, m_kv_norm, m_w_kv, m_final_norm, v_s5_lam_re, v_s5_lam_im, v_s5_log_dt, v_s5_b_re, v_s5_b_im, v_s5_c_re, v_s5_c_im, v_s5_d, v_s5_w_glu, v_a_norm_mix, v_ffn_norm, v_ffn_w_in, v_ffn_w_out, v_b_norm_mix, v_attn_w_q, v_attn_w_o, v_kv_norm, v_w_kv, v_final_norm):
    args = dict(locals())
    T, D = x.shape[1], x.shape[2]
    n_layers = ffn_w_in.shape[0]
    xi_, yi_, ci_ = _coords()
    pos = jnp.stack([xi_, yi_, ci_]).astype(jnp.int32)
    me = 4 * xi_ + 2 * yi_ + ci_

    big_names = ["glu"] + [f"win{l}" for l in range(n_layers)] + [f"wout{l}" for l in range(n_layers)] \
        + ["wkv", "wq", "wo"]
    big_shards = [(s5_w_glu, 0)] + [(ffn_w_in, l) for l in range(n_layers)] \
        + [(ffn_w_out, l) for l in range(n_layers)] + [(w_kv[None], 0), (attn_w_q, 0), (attn_w_o, 0)]
    big_axes = [1] + [1] * n_layers + [0] * n_layers + [0, 1, 0]
    big_out_names = ["s5_w_glu"] + ["ffn_w_in"] * n_layers + ["ffn_w_out"] * n_layers \
        + ["w_kv", "attn_w_q", "attn_w_o"]
    index_of = {n: i for i, n in enumerate(big_names)}
    vec_shard = jnp.concatenate([s5_d, a_norm_mix], axis=0)
    (vecs,) = _all_gather("vectors_all_gather", [vec_shard], [1])

    gather_groups = [["glu"], ["win0"], ["wout0"], ["wkv", "wq", "wo"], ["win1", "wout1"]]
    group_idx = [[index_of[n] for n in grp] for grp in gather_groups]
    group_of = {n: gi for gi, grp in enumerate(gather_groups) for n in grp}
    placed = [_cast_and_place(f"cast_place_{n}", s, l, ax, pos, BF16)
              for n, (s, l), ax in zip(big_names, big_shards, big_axes)]
    gather_sems, shards_thru, lands_thru, start_token = _gather_start(
        "weights_gather_start", [p[0] for p in placed], [p[1] for p in placed], big_axes, group_idx)
    full = {}

    def wts(name, after):
        if name not in full:
            gi = group_of[name]
            idx = group_idx[gi]
            axes = [big_axes[a] for a in idx]
            send, r_d2d, r_ici = gather_sems[gi]
            f_send, f_recv, lands = _gather_forward(
                f"weights_gather_forward{gi}", [lands_thru[a] for a in idx], axes, r_ici, after)
            done = _gather_finish(f"weights_gather_finish{gi}", [shards_thru[a] for a in idx], lands, axes,
                                  send, r_d2d, f_send, f_recv)
            full.update(zip(gather_groups[gi], done))
        return full[name]

    exchanges, pending = [], []

    def flush(after):
        names, axes, started = pending.pop()
        tag = "_".join(names)
        grads, lands = _pair_exchange_finish(f"rs_pair_exchange_finish_{tag}", started, axes, after)
        p0s, pbs = [], []
        for n, gr, land, ax in zip(names, grads, lands, axes):
            p0, pb = _pair_sum(f"rs_pair_sum_{n}", gr, land, ax, pos)
            p0s.append(p0)
            pbs.append(pb)
        started = _chip_exchange_start(f"rs_chip_exchange_start_{tag}", pbs)
        exchanges.append((names, p0s, started))
        return started[4]

    def ready(grads, after=None):
        if not grads:
            return flush(after)
        names = list(grads)
        axes = [big_axes[index_of[n]] for n in names]
        started = _pair_exchange_start(f"rs_pair_exchange_start_{'_'.join(names)}", [grads[n] for n in names], axes)
        token = flush(started[4]) if pending else started[4]
        pending.append((names, axes, started))
        return token

    G, P, C = s5_b_re.shape[1:]
    w = dict(
        a_norm=vecs[1:2], s5_d=vecs[0:1],
        ffn_norm=ffn_norm, b_norm=b_norm_mix, kv_norm=kv_norm.reshape(1, D), final_norm=final_norm.reshape(1, D),
        lam_re=s5_lam_re[0], lam_im=s5_lam_im[0], log_dt=s5_log_dt.reshape(G, 1),
        bt_re=s5_b_re[0].transpose(2, 0, 1), bt_im=s5_b_im[0].transpose(2, 0, 1),
        c_re=s5_c_re[0], c_im=s5_c_im[0],
    )
    small = {}

    def small_ready(g):
        small_g = dict(
            s5_lam_re=g["lam_re"], s5_lam_im=g["lam_im"], s5_log_dt=g["log_dt"],
            s5_b_re=g["bt_re"].transpose(1, 2, 0), s5_b_im=g["bt_im"].transpose(1, 2, 0),
            s5_c_re=g["c_re"], s5_c_im=g["c_im"], ffn_norm=g["ffn_norm"], b_norm_mix=g["b_norm"],
            kv_norm=g["kv_norm"], final_norm=g["final_norm"])
        packed = _pack([small_g[n] for n in SMALL_NAMES])
        shard, land = _cast_and_place("place_small_grads", packed[None], 0, 0, pos, F32)
        sems, thru, lands, token = _gather_start("small_grads_gather_start", [shard], [land], [0], [[0]])
        small.update(sems=sems[0], thru=thru, lands=lands, rows=packed.shape[0])
        return token

    loss_blk, grad_x, g = _local_step(x[0], loss_target[0], w, wts, ready, dep0=start_token,
                                      small_ready=small_ready)
    loss = lax.psum(loss_blk[0, 0], ("x", "y", "c"))
    (tail,) = _all_gather("vector_grads_all_gather", [_pack([g["s5_d"], g["a_norm"]])], [0])

    out = {}

    def put(name, res, shape):
        for kind, r in zip(("grad", "delta", "new_m", "new_v"), res):
            out[f"{kind}_{name}"] = r.reshape(shape)

    updated = {}

    def update(names, p0s, recvd):
        for name, p0, rc in zip(names, p0s, recvd):
            oname = big_out_names[index_of[name]]
            w3, layer = big_shards[index_of[name]]
            updated[oname] = _adamw(f"adamw_{name}", [p0, rc], w3, args["m_" + oname].reshape(w3.shape),
                                    args["v_" + oname].reshape(w3.shape), layer=layer, prev=updated.get(oname))
        return updated[big_out_names[index_of[names[0]]]][0]

    early, last = exchanges[:-1], exchanges[-1]
    landed = _chip_exchange_finish("rs_chip_exchange_finish_early", [e[2] for e in early], grad_x)
    for (names, p0s, _), recvd in zip(early[:-1], landed[:-1]):
        marker = update(names, p0s, recvd)
    send, r_d2d, r_ici = small["sems"]
    f_send, f_recv, small_lands = _gather_forward("small_grads_gather_forward", small["lands"], [0], r_ici, marker)
    marker = update(early[-1][0], early[-1][1], landed[-1])
    (all_parts,) = _gather_finish("small_grads_gather_finish", small["thru"], small_lands, [0],
                                  send, r_d2d, f_send, f_recv)
    (recvd,) = _chip_exchange_finish("rs_chip_exchange_finish_last", [last[2]], marker)
    update(last[0], last[1], recvd)
    for oname, res in updated.items():
        put(oname, res, args[oname].shape)
    all_parts = all_parts.reshape(N_DEV, small["rows"], 128)
    w_pack = _pack([args[n] for n in SMALL_NAMES])
    m_pack = _pack([args["m_" + n] for n in SMALL_NAMES])
    v_pack = _pack([args["v_" + n] for n in SMALL_NAMES])
    res = _adamw("adamw_small", [all_parts], w_pack, m_pack, v_pack)
    shapes = [args[n].shape for n in SMALL_NAMES]
    unpacked = [_unpack(r, shapes) for r in res]
    for i, n in enumerate(SMALL_NAMES):
        put(n, [u[i] for u in unpacked], args[n].shape)
    ws = D // N_DEV
    tail = lax.dynamic_slice_in_dim(tail.reshape(N_DEV, 2, D), me * ws, ws, axis=2)
    res = _adamw("adamw_vec", [tail], vec_shard,
                 jnp.concatenate([m_s5_d, m_a_norm_mix], axis=0), jnp.concatenate([v_s5_d, v_a_norm_mix], axis=0))
    put("s5_d", [r[0:1] for r in res], s5_d.shape)
    put("a_norm_mix", [r[1:2] for r in res], a_norm_mix.shape)

    names = ("s5_lam_re", "s5_lam_im", "s5_log_dt", "s5_b_re", "s5_b_im", "s5_c_re", "s5_c_im", "s5_d",
             "s5_w_glu", "a_norm_mix", "ffn_norm", "ffn_w_in", "ffn_w_out", "b_norm_mix", "attn_w_q",
             "attn_w_o", "kv_norm", "w_kv", "final_norm")
    result = [loss, grad_x.reshape(x.shape)]
    for kind in ("grad", "delta", "new_m", "new_v"):
        result += [out[f"{kind}_{n}"] for n in names]
    return tuple(result)
```

```python
import functools
import math

import jax
import jax.numpy as jnp
from jax import lax
from jax.experimental import pallas as pl
from jax.experimental.pallas import tpu as pltpu

F32 = jnp.float32
BF16 = jnp.bfloat16

EPS = 1e-6
NEG_INF = -1e30
HEAD_DIM = 128
N_KV_HEADS = 4
DILATIONS = (1, 4, 16)
ATT_BLK = 128
S5_C = 16
S5_P = 64
S5_GB = 16
S5_CH = S5_GB * S5_C
S5_W = S5_GB * S5_P
S5_UNROLL = 4
N_DEV = 8

ADAM_LR = 0.001
ADAM_B1 = 0.9
ADAM_B2 = 0.999
ADAM_EPS = 1e-08
ADAM_WD = 0.01
ADAM_STEP = 10

VMEM_LIMIT_BYTES = 56 * 1024 * 1024
MM_TILE = 1024
MM_TILE_NARROW = 512
MM_DEPTH = 2816
MESH = pl.DeviceIdType.MESH
ANY = pl.BlockSpec(memory_space=pl.ANY)


def _tile(n, pref, align=128):
    t = (min(pref, n) // align) * align
    while t >= align:
        if n % t == 0:
            return t
        t -= align
    return n


def _params(sem):
    return pltpu.CompilerParams(dimension_semantics=sem, vmem_limit_bytes=VMEM_LIMIT_BYTES)


def _sigmoid(x):
    return 1.0 / (1.0 + jnp.exp(-x))


NN = (((1,), (0,)), ((), ()))
NT = (((1,), (1,)), ((), ()))
TN = (((0,), (0,)), ((), ()))


def _dot(a, b, dims=NN):
    return lax.dot_general(a, b, dims, preferred_element_type=F32)


def _matmul(name, grid, ins, in_specs, products, dims, out_shapes, out_specs, acc_shapes, epilogue):
    n_in, n_out, nk = len(ins), len(out_shapes), grid[2]

    def body(*refs):
        in_refs = refs[:n_in]
        out_refs = refs[n_in:n_in + n_out]
        acc_refs = refs[n_in + n_out:]

        def prods():
            vals = [None] * len(acc_shapes)
            for ai, bi, ci in products:
                d = _dot(in_refs[ai][...].astype(BF16), in_refs[bi][...].astype(BF16), dims)
                vals[ci] = d if vals[ci] is None else vals[ci] + d
            return vals

        if nk == 1:
            epilogue(in_refs, out_refs, prods())
        else:
            k = pl.program_id(2)

            @pl.when(k == 0)
            def _():
                for a in acc_refs:
                    a[...] = jnp.zeros_like(a)

            for a, v in zip(acc_refs, prods()):
                a[...] += v

            @pl.when(k == nk - 1)
            def _():
                epilogue(in_refs, out_refs, [a[...] for a in acc_refs])

    scratch = [] if nk == 1 else [pltpu.VMEM(s, F32) for s in acc_shapes]
    return pl.pallas_call(
        body, grid=grid, in_specs=in_specs, out_specs=out_specs, out_shape=out_shapes,
        scratch_shapes=scratch, compiler_params=_params(("parallel", "parallel", "arbitrary")),
        name=name)(*ins)


def _mm_dual_fwd(name, a, w, res, kind):
    T, K = a.shape
    N = w.shape[1] // 2
    tm, tn = _tile(T, MM_TILE), _tile(N, MM_TILE_NARROW)
    nj = N // tn
    grid = (T // tm, nj, 1)
    ins = [a, w, w]
    in_specs = [pl.BlockSpec((tm, K), lambda i, j, k: (i, 0)),
                pl.BlockSpec((K, tn), lambda i, j, k: (0, j)),
                pl.BlockSpec((K, tn), lambda i, j, k: (0, j + nj))]
    pair_spec = pl.BlockSpec((2, tm, tn), lambda i, j, k: (0, i, j))
    tile_spec = pl.BlockSpec((tm, tn), lambda i, j, k: (i, j))
    if kind == "glu":
        ins.append(res)
        in_specs.append(tile_spec)

        def epilogue(in_refs, out_refs, accs):
            val, gate = accs
            s = _sigmoid(gate)
            out_refs[0][...] = in_refs[3][...] + val * s
            out_refs[1][0] = s.astype(BF16)
            out_refs[1][1] = (val * s * (1.0 - s)).astype(BF16)

        out_shapes = [jax.ShapeDtypeStruct((T, N), F32), jax.ShapeDtypeStruct((2, T, N), BF16)]
        out_specs = [tile_spec, pair_spec]
    else:
        def epilogue(in_refs, out_refs, accs):
            g, u = accs
            s = _sigmoid(g)
            silu = g * s
            out_refs[0][0] = (u * (s * (1.0 + g * (1.0 - s)))).astype(BF16)
            out_refs[0][1] = silu.astype(BF16)
            out_refs[1][...] = (silu * u).astype(BF16)

        out_shapes = [jax.ShapeDtypeStruct((2, T, N), BF16), jax.ShapeDtypeStruct((T, N), BF16)]
        out_specs = [pair_spec, tile_spec]
    return _matmul(name, grid, ins, in_specs, [(0, 1, 0), (0, 2, 1)], NN, out_shapes, out_specs,
                   [(tm, tn), (tm, tn)], epilogue)


def _mm_kv(name, a, w):
    T, K = a.shape
    N = w.shape[1] // 2
    tm, tn = _tile(T, MM_TILE), _tile(N, MM_TILE_NARROW)
    nj = N // tn
    tile_spec = pl.BlockSpec((tm, tn), lambda i, j, k: (i, j))

    def epilogue(in_refs, out_refs, accs):
        out_refs[0][...] = accs[0]
        out_refs[1][...] = accs[1]

    return _matmul(name, (T // tm, nj, 1), [a, w, w],
                   [pl.BlockSpec((tm, K), lambda i, j, k: (i, 0)),
                    pl.BlockSpec((K, tn), lambda i, j, k: (0, j)),
                    pl.BlockSpec((K, tn), lambda i, j, k: (0, j + nj))],
                   [(0, 1, 0), (0, 2, 1)], NN,
                   [jax.ShapeDtypeStruct((T, N), F32)] * 2, [tile_spec, tile_spec],
                   [(tm, tn), (tm, tn)], epilogue)


def _mm_nn(name, a, w, res=None, out_dtype=F32):
    T, K = a.shape
    N = w.shape[1]
    tk = K if K <= 2 * MM_DEPTH else _tile(K, MM_DEPTH)
    tm, tn = _tile(T, MM_TILE), _tile(N, MM_TILE if K <= MM_DEPTH else MM_TILE_NARROW)
    grid = (T // tm, N // tn, K // tk)
    tile_spec = pl.BlockSpec((tm, tn), lambda i, j, k: (i, j))
    ins = [a, w]
    in_specs = [pl.BlockSpec((tm, tk), lambda i, j, k: (i, k)),
                pl.BlockSpec((tk, tn), lambda i, j, k: (k, j))]
    if res is not None:
        ins.append(res)
        in_specs.append(tile_spec)

    def epilogue(in_refs, out_refs, accs):
        v = accs[0]
        if res is not None:
            v = v + in_refs[2][...]
        out_refs[0][...] = v.astype(out_dtype)

    return _matmul(name, grid, ins, in_specs, [(0, 1, 0)], NN,
                   [jax.ShapeDtypeStruct((T, N), out_dtype)], [tile_spec], [(tm, tn)], epilogue)[0]


def _dep_operand(ins, in_specs, dep):
    if dep is not None:
        ins.append(dep)
        in_specs.append(pl.BlockSpec((8, 128), lambda *_: (0, 0)))


def _mm_nt(name, a_list, w, out_dtype=F32, dep=None):
    T, Np = a_list[0].shape
    Ko = w.shape[0]
    n_parts = len(a_list)
    wide_a = a_list[0].dtype != BF16
    tm, tn, tk = _tile(T, MM_TILE_NARROW if wide_a else MM_TILE), _tile(Ko, MM_TILE), _tile(Np, MM_DEPTH)
    nkp = Np // tk
    grid = (T // tm, Ko // tn, nkp)
    ins = list(a_list) + [w] * n_parts
    in_specs = [pl.BlockSpec((tm, tk), lambda i, j, k: (i, k)) for _ in a_list]
    in_specs += [pl.BlockSpec((tn, tk), functools.partial(lambda i, j, k, p: (j, p * nkp + k), p=p))
                 for p in range(n_parts)]
    products = [(p, n_parts + p, 0) for p in range(n_parts)]
    _dep_operand(ins, in_specs, dep)

    def epilogue(in_refs, out_refs, accs):
        out_refs[0][...] = accs[0].astype(out_dtype)

    return _matmul(name, grid, ins, in_specs, products, NT,
                   [jax.ShapeDtypeStruct((T, Ko), out_dtype)],
                   [pl.BlockSpec((tm, tn), lambda i, j, k: (i, j))], [(tm, tn)], epilogue)[0]


def _mm_nt_pair(name, a3, w):
    _, T, N = a3.shape
    Ko = w.shape[0]
    tm, tn, tk = _tile(T, MM_TILE), _tile(Ko, MM_TILE), _tile(N, MM_DEPTH)
    nkh = N // tk
    grid = (T // tm, Ko // tn, 2 * nkh)

    def epilogue(in_refs, out_refs, accs):
        out_refs[0][...] = accs[0]

    return _matmul(name, grid, [a3, w],
                   [pl.BlockSpec((None, tm, tk), lambda i, j, k: (k // nkh, i, k % nkh)),
                    pl.BlockSpec((tn, tk), lambda i, j, k: (j, k))],
                   [(0, 1, 0)], NT, [jax.ShapeDtypeStruct((T, Ko), F32)],
                   [pl.BlockSpec((tm, tn), lambda i, j, k: (i, j))], [(tm, tn)], epilogue)[0]


def _mm_nt_ffn_bwd(name, dx, w_out, gu, dep=None):
    T, D = dx.shape
    Fh = w_out.shape[0]
    tm, tn = _tile(T, MM_TILE), _tile(Fh, MM_TILE_NARROW)
    sub = _tile(tn, 256)
    n_dep = 0 if dep is None else 1

    def body(dx_ref, w_ref, gu_ref, *rest):
        out_ref = rest[n_dep]
        a = dx_ref[...]
        for c0 in range(0, tn, sub):
            da = _dot(a, w_ref[c0:c0 + sub, :], NT)
            out_ref[0, :, c0:c0 + sub] = (da * gu_ref[0, :, c0:c0 + sub].astype(F32)).astype(BF16)
            out_ref[1, :, c0:c0 + sub] = (da * gu_ref[1, :, c0:c0 + sub].astype(F32)).astype(BF16)

    pair_spec = pl.BlockSpec((2, tm, tn), lambda i, j: (0, i, j))
    ins = [dx, w_out, gu]
    in_specs = [pl.BlockSpec((tm, D), lambda i, j: (i, 0)),
                pl.BlockSpec((tn, D), lambda i, j: (j, 0)),
                pair_spec]
    _dep_operand(ins, in_specs, dep)
    return pl.pallas_call(
        body, grid=(T // tm, Fh // tn), in_specs=in_specs, out_specs=pair_spec,
        out_shape=jax.ShapeDtypeStruct((2, T, Fh), BF16),
        compiler_params=_params(("parallel", "parallel")), name=name)(*ins)


def _mm_tn(name, a, d):
    T, Ko = a.shape
    N = d.shape[1]
    to, tn, tk = _tile(Ko, MM_TILE_NARROW), _tile(N, MM_TILE if d.dtype == BF16 else MM_TILE_NARROW), T
    grid = (Ko // to, N // tn, T // tk)

    def epilogue(in_refs, out_refs, accs):
        out_refs[0][...] = accs[0].astype(BF16)

    return _matmul(name, grid, [a, d],
                   [pl.BlockSpec((tk, to), lambda i, j, k: (k, i)),
                    pl.BlockSpec((tk, tn), lambda i, j, k: (k, j))],
                   [(0, 1, 0)], TN, [jax.ShapeDtypeStruct((Ko, N), BF16)],
                   [pl.BlockSpec((to, tn), lambda i, j, k: (i, j))], [(to, tn)], epilogue)[0]


def _mm_tn_pair(name, a, d3):
    T, Ko = a.shape
    N = d3.shape[2]
    to, tn, tk = _tile(Ko, MM_TILE), _tile(N, MM_TILE_NARROW), T
    njh = N // tn
    grid = (Ko // to, 2 * njh, T // tk)

    def epilogue(in_refs, out_refs, accs):
        out_refs[0][...] = accs[0].astype(BF16)

    return _matmul(name, grid, [a, d3],
                   [pl.BlockSpec((tk, to), lambda i, j, k: (k, i)),
                    pl.BlockSpec((None, tk, tn), lambda i, j, k: (j // njh, k, j % njh))],
                   [(0, 1, 0)], TN, [jax.ShapeDtypeStruct((Ko, 2 * N), BF16)],
                   [pl.BlockSpec((to, tn), lambda i, j, k: (i, j))], [(to, tn)], epilogue)[0]


def _rms_fwd(name, x, gains, dtypes, dep=None):
    T, D = x.shape
    n = len(gains)
    tr = _tile(T, 512, 8)
    n_dep = 0 if dep is None else 1

    def body(x_ref, *refs):
        xv = x_ref[...]
        xr = xv * lax.rsqrt(jnp.mean(xv * xv, axis=-1, keepdims=True) + EPS)
        for g_ref, o_ref in zip(refs[:n], refs[n + n_dep:]):
            o_ref[...] = (xr * g_ref[...]).astype(o_ref.dtype)

    row = pl.BlockSpec((tr, D), lambda i: (i, 0))
    vec = pl.BlockSpec((1, D), lambda i: (0, 0))
    ins, in_specs = [x, *gains], [row] + [vec] * n
    _dep_operand(ins, in_specs, dep)
    return pl.pallas_call(
        body, grid=(T // tr,), in_specs=in_specs, out_specs=[row] * n,
        out_shape=[jax.ShapeDtypeStruct((T, D), dt) for dt in dtypes],
        compiler_params=_params(("parallel",)), name=name)(*ins)


def _rms_bwd(name, x, dres, gains, dhs, dep=None):
    T, D = x.shape
    n = len(gains)
    tr = _tile(T, 256, 8)
    n_dep = 0 if dep is None else 1

    def body(x_ref, dres_ref, *refs):
        g_refs, dh_refs = refs[:n], refs[n:2 * n]
        dx_ref, dxb_ref = refs[2 * n + n_dep], refs[2 * n + n_dep + 1]
        dg_refs = refs[2 * n + n_dep + 2:]
        xv = x_ref[...]
        r = lax.rsqrt(jnp.mean(xv * xv, axis=-1, keepdims=True) + EPS)
        xr = xv * r
        w = None
        for g_ref, dh_ref, dg_ref in zip(g_refs, dh_refs, dg_refs):
            dh = dh_ref[...].astype(F32)

            @pl.when(pl.program_id(0) == 0)
            def _():
                dg_ref[...] = jnp.zeros_like(dg_ref)

            dg_ref[...] += jnp.sum(dh * xr, axis=0, keepdims=True)
            wi = dh * g_ref[...]
            w = wi if w is None else w + wi
        dx = dres_ref[...] + r * (w - xr * jnp.mean(w * xr, axis=-1, keepdims=True))
        dx_ref[...] = dx
        dxb_ref[...] = dx.astype(BF16)

    row = pl.BlockSpec((tr, D), lambda i: (i, 0))
    vec = pl.BlockSpec((1, D), lambda i: (0, 0))
    ins, in_specs = [x, dres, *gains, *dhs], [row, row] + [vec] * n + [row] * n
    _dep_operand(ins, in_specs, dep)
    outs = pl.pallas_call(
        body, grid=(T // tr,), in_specs=in_specs,
        out_specs=[row, row] + [vec] * n,
        out_shape=[jax.ShapeDtypeStruct((T, D), F32), jax.ShapeDtypeStruct((T, D), BF16)]
        + [jax.ShapeDtypeStruct((1, D), F32)] * n,
        compiler_params=_params(("arbitrary",)), name=name)(*ins)
    return outs[0], outs[1], outs[2:]


def _loss_head(x, gain, target):
    T, D = x.shape
    tr = _tile(T, 256, 8)

    def body(x_ref, g_ref, t_ref, loss_ref, dx_ref, dxb_ref, dg_ref):
        @pl.when(pl.program_id(0) == 0)
        def _():
            loss_ref[...] = jnp.zeros_like(loss_ref)
            dg_ref[...] = jnp.zeros_like(dg_ref)

        xv = x_ref[...]
        r = lax.rsqrt(jnp.mean(xv * xv, axis=-1, keepdims=True) + EPS)
        xr = xv * r
        err = xr * g_ref[...] - t_ref[...]
        part = jnp.sum(jnp.sum(err * err, axis=-1, keepdims=True), axis=0, keepdims=True) * (0.5 / D)
        loss_ref[...] += jnp.broadcast_to(part, loss_ref.shape)
        dy = err * (1.0 / D)
        dg_ref[...] += jnp.sum(dy * xr, axis=0, keepdims=True)
        w = dy * g_ref[...]
        dx = r * (w - xr * jnp.mean(w * xr, axis=-1, keepdims=True))
        dx_ref[...] = dx
        dxb_ref[...] = dx.astype(BF16)

    row = pl.BlockSpec((tr, D), lambda i: (i, 0))
    vec = pl.BlockSpec((1, D), lambda i: (0, 0))
    return pl.pallas_call(
        body, grid=(T // tr,), in_specs=[row, vec, row],
        out_specs=[pl.BlockSpec((8, 128), lambda i: (0, 0)), row, row, vec],
        out_shape=[jax.ShapeDtypeStruct((8, 128), F32), jax.ShapeDtypeStruct((T, D), F32),
                   jax.ShapeDtypeStruct((T, D), BF16), jax.ShapeDtypeStruct((1, D), F32)],
        compiler_params=_params(("arbitrary",)), name="loss_head")(x, gain, target)


def _glu_bwd(dmix, vg, dep=None):
    T, N = dmix.shape
    tr, tc = _tile(T, 512, 8), _tile(N, 1024)
    n_dep = 0 if dep is None else 1

    def body(d_ref, vg_ref, *refs):
        o_ref = refs[n_dep]
        d = d_ref[...]
        o_ref[0] = (d * vg_ref[0].astype(F32)).astype(BF16)
        o_ref[1] = (d * vg_ref[1].astype(F32)).astype(BF16)

    pair = pl.BlockSpec((2, tr, tc), lambda i, j: (0, i, j))
    ins, in_specs = [dmix, vg], [pl.BlockSpec((tr, tc), lambda i, j: (i, j)), pair]
    _dep_operand(ins, in_specs, dep)
    return pl.pallas_call(
        body, grid=(T // tr, N // tc), in_specs=in_specs,
        out_specs=pair, out_shape=jax.ShapeDtypeStruct((2, T, N), BF16),
        compiler_params=_params(("parallel", "parallel")), name="glu_bwd")(*ins)


def _to_state_tiles(x_ref, s, val):
    tc = val.shape[0]
    for j in range(S5_W // 128):
        x_ref[s, pl.ds(j, tc, stride=8), :] = val[:, 128 * j:128 * (j + 1)]


def _from_state_tiles(x_ref, s, tc):
    return jnp.concatenate([x_ref[s, pl.ds(j, tc, stride=8), :] for j in range(S5_W // 128)], axis=1)


def _s5_scan_fwd(xr_ref, xi_ref, ar_ref, ai_ref, cr_ref, ci_ref, tc, nblk):
    a = [(ar_ref[s], ai_ref[s]) for s in range(nblk)]

    def step(i, carry):
        carry = list(carry)
        for uu in range(S5_UNROLL):
            r0 = pl.multiple_of((i * S5_UNROLL + uu) * 8, 8)
            for s in range(nblk):
                cr, ci = carry[2 * s], carry[2 * s + 1]
                a_r, a_i = a[s]
                xr = a_r * cr - a_i * ci + xr_ref[s, pl.ds(r0, 8), :]
                xi = a_r * ci + a_i * cr + xi_ref[s, pl.ds(r0, 8), :]
                xr_ref[s, pl.ds(r0, 8), :] = xr
                xi_ref[s, pl.ds(r0, 8), :] = xi
                carry[2 * s], carry[2 * s + 1] = xr, xi
        return tuple(carry)

    init = []
    for s in range(nblk):
        init += [cr_ref[s], ci_ref[s]]
    out = lax.fori_loop(0, tc // S5_UNROLL, step, tuple(init))
    for s in range(nblk):
        cr_ref[s] = out[2 * s]
        ci_ref[s] = out[2 * s + 1]


def _s5_scan_bwd(lr_ref, li_ref, xr_ref, xi_ref, h_ref, ar_ref, ai_ref, cr_ref, ci_ref,
                 accr_ref, acci_ref, tc, nblk):
    a = [(ar_ref[s], ai_ref[s]) for s in range(nblk)]

    def one(s, r0, prev_r, prev_i, st):
        c_r, c_i, d_r, d_i = st
        a_r, a_i = a[s]
        l_r = lr_ref[s, pl.ds(r0, 8), :] + a_r * c_r + a_i * c_i
        l_i = li_ref[s, pl.ds(r0, 8), :] + a_r * c_i - a_i * c_r
        lr_ref[s, pl.ds(r0, 8), :] = l_r
        li_ref[s, pl.ds(r0, 8), :] = l_i
        return [l_r, l_i, d_r + l_r * prev_r + l_i * prev_i, d_i - l_r * prev_i + l_i * prev_r]

    def step(i, carry):
        carry = list(carry)
        for uu in range(S5_UNROLL):
            t = tc - 1 - (i * S5_UNROLL + uu)
            r0 = pl.multiple_of(t * 8, 8)
            p0 = pl.multiple_of((t - 1) * 8, 8)
            for s in range(nblk):
                carry[4 * s:4 * s + 4] = one(s, r0, xr_ref[s, pl.ds(p0, 8), :], xi_ref[s, pl.ds(p0, 8), :],
                                             carry[4 * s:4 * s + 4])
        return tuple(carry)

    init = []
    for s in range(nblk):
        init += [cr_ref[s], ci_ref[s], accr_ref[s], acci_ref[s]]
    carry = list(lax.fori_loop(0, tc // S5_UNROLL - 1, step, tuple(init)))
    for t in range(S5_UNROLL - 1, -1, -1):
        for s in range(nblk):
            if t > 0:
                prev_r, prev_i = xr_ref[s, 8 * (t - 1):8 * t, :], xi_ref[s, 8 * (t - 1):8 * t, :]
            else:
                prev_r, prev_i = h_ref[0, s], h_ref[1, s]
            carry[4 * s:4 * s + 4] = one(s, 8 * t, prev_r, prev_i, carry[4 * s:4 * s + 4])
    for s in range(nblk):
        cr_ref[s], ci_ref[s], accr_ref[s], acci_ref[s] = carry[4 * s:4 * s + 4]


def _gelu(y):
    k = math.sqrt(2.0 / math.pi)
    return 0.5 * y * (1.0 + jnp.tanh(k * (y + 0.044715 * (y * y * y))))


def _gelu_grad(y):
    k = math.sqrt(2.0 / math.pi)
    t = jnp.tanh(k * (y + 0.044715 * (y * y * y)))
    return 0.5 * (1.0 + t) + 0.5 * y * (1.0 - t * t) * (k * (1.0 + 3.0 * 0.044715 * (y * y)))


def _s5_specs(tc, nch, sbk, rev):
    def ch(c):
        return nch - 1 - c if rev else c

    return dict(
        act=pl.BlockSpec((tc, sbk * S5_CH), lambda i, c: (ch(c), i)),
        bb=pl.BlockSpec((2, sbk, S5_CH, 128), lambda i, c: (0, i, 0, 0)),
        cc=pl.BlockSpec((2, sbk, S5_P, S5_CH), lambda i, c: (0, i, 0, 0)),
        a=pl.BlockSpec((sbk, 8, 128), lambda i, c: (i, 0, 0)),
        d=pl.BlockSpec((1, sbk * S5_CH), lambda i, c: (0, i)),
        h=pl.BlockSpec((None, 2, sbk, 8, 128), lambda i, c: (ch(c), 0, i, 0, 0)),
    )


def _s5_blocks(nb, pref):
    return max(b for b in range(1, pref + 1) if nb % b == 0)


def _s5_group_masks():
    rb = lax.broadcasted_iota(jnp.int32, (S5_CH, S5_W), 0) // S5_C
    qb = lax.broadcasted_iota(jnp.int32, (S5_CH, S5_W), 1) // S5_P
    qc = lax.broadcasted_iota(jnp.int32, (S5_W, S5_CH), 0) // S5_P
    rc = lax.broadcasted_iota(jnp.int32, (S5_W, S5_CH), 1) // S5_C
    return rb == qb, qc == rc


def _s5_expand(bb_ref, cc_ref, bbd, ccd, sbk):
    mask_b, mask_c = _s5_group_masks()
    for k in range(2):
        for s in range(sbk):
            bbd[k, s] = jnp.where(mask_b, jnp.tile(bb_ref[k, s], (1, S5_W // 128)), 0).astype(BF16)
            ccd[k, s] = jnp.where(mask_c, jnp.tile(cc_ref[k, s], (S5_GB, 1)), 0).astype(BF16)


def _s5_fwd(u, bb2, cc2, a_re, a_im, d_skip):
    T, D = u.shape
    nb = D // S5_CH
    sbk = _s5_blocks(nb, 4)
    tc = _tile(T, 512, 8)
    nch = T // tc
    sp = _s5_specs(tc, nch, sbk, False)

    def body(u_ref, bb_ref, cc_ref, ar_ref, ai_ref, d_ref, z_ref, h_ref, xr, xi, cr, ci, bbd, ccd):
        @pl.when(pl.program_id(1) == 0)
        def _():
            cr[...] = jnp.zeros_like(cr)
            ci[...] = jnp.zeros_like(ci)
            _s5_expand(bb_ref, cc_ref, bbd, ccd, sbk)

        h_ref[0] = cr[...]
        h_ref[1] = ci[...]
        for s in range(sbk):
            ub = u_ref[:, s * S5_CH:(s + 1) * S5_CH].astype(BF16)
            _to_state_tiles(xr, s, _dot(ub, bbd[0, s]))
            _to_state_tiles(xi, s, _dot(ub, bbd[1, s]))
        _s5_scan_fwd(xr, xi, ar_ref, ai_ref, cr, ci, tc, sbk)
        for s in range(sbk):
            cols = slice(s * S5_CH, (s + 1) * S5_CH)
            y = (_dot(_from_state_tiles(xr, s, tc).astype(BF16), ccd[0, s])
                 - _dot(_from_state_tiles(xi, s, tc).astype(BF16), ccd[1, s])
                 + d_ref[:, cols] * u_ref[:, cols])
            z_ref[:, cols] = _gelu(y).astype(BF16)

    tiles = pltpu.VMEM((sbk, tc * 8, 128), F32)
    carry = pltpu.VMEM((sbk, 8, 128), F32)
    return pl.pallas_call(
        body, grid=(nb // sbk, nch),
        in_specs=[sp["act"], sp["bb"], sp["cc"], sp["a"], sp["a"], sp["d"]],
        out_specs=[sp["act"], sp["h"]],
        out_shape=[jax.ShapeDtypeStruct((T, D), BF16), jax.ShapeDtypeStruct((nch, 2, nb, 8, 128), F32)],
        scratch_shapes=[tiles, tiles, carry, carry, pltpu.VMEM((2, sbk, S5_CH, S5_W), BF16),
                        pltpu.VMEM((2, sbk, S5_W, S5_CH), BF16)],
        compiler_params=_params(("parallel", "arbitrary")), name="s5_fwd",
    )(u, bb2, cc2, a_re, a_im, d_skip)


def _s5_bwd(u, dz, h0, bb2, cc2, a_re, a_im, d_skip, dep=None):
    T, D = u.shape
    nb = D // S5_CH
    sbk = _s5_blocks(nb, 2)
    tc = _tile(T, 512, 8)
    nch = T // tc
    sp = _s5_specs(tc, nch, sbk, True)

    n_dep = 0 if dep is None else 1

    def body(u_ref, dz_ref, h_ref, bb_ref, cc_ref, ar_ref, ai_ref, d_ref, *rest):
        (du_ref, dd_ref, dar_ref, dai_ref, dbb_ref, dcc_ref,
         xr, xi, lr, li, fr, fi, br, bi, accr, acci, bbd, ccd, dbbd, dccd) = rest[n_dep:]
        c = pl.program_id(1)

        @pl.when(c == 0)
        def _():
            for ref in (br, bi, accr, acci, dd_ref, dbbd, dccd):
                ref[...] = jnp.zeros_like(ref)
            _s5_expand(bb_ref, cc_ref, bbd, ccd, sbk)

        for s in range(sbk):
            ub = u_ref[:, s * S5_CH:(s + 1) * S5_CH].astype(BF16)
            _to_state_tiles(xr, s, _dot(ub, bbd[0, s]))
            _to_state_tiles(xi, s, _dot(ub, bbd[1, s]))
        fr[...] = h_ref[0]
        fi[...] = h_ref[1]
        _s5_scan_fwd(xr, xi, ar_ref, ai_ref, fr, fi, tc, sbk)
        for s in range(sbk):
            cols = slice(s * S5_CH, (s + 1) * S5_CH)
            uv = u_ref[:, cols]
            xrb = _from_state_tiles(xr, s, tc).astype(BF16)
            xib = _from_state_tiles(xi, s, tc).astype(BF16)
            dsk = d_ref[:, cols]
            y = _dot(xrb, ccd[0, s]) - _dot(xib, ccd[1, s]) + dsk * uv
            dy = dz_ref[:, cols] * _gelu_grad(y)
            dd_ref[:, cols] += jnp.sum(dy * uv, axis=0, keepdims=True)
            dyb = dy.astype(BF16)
            dccd[0, s] += _dot(xrb, dyb, TN)
            dccd[1, s] += _dot(xib, dyb, TN)
            _to_state_tiles(lr, s, _dot(dyb, ccd[0, s], NT))
            _to_state_tiles(li, s, -_dot(dyb, ccd[1, s], NT))
            du_ref[:, cols] = dy * dsk
        _s5_scan_bwd(lr, li, xr, xi, h_ref, ar_ref, ai_ref, br, bi, accr, acci, tc, sbk)
        for s in range(sbk):
            cols = slice(s * S5_CH, (s + 1) * S5_CH)
            ub = u_ref[:, cols].astype(BF16)
            lrb = _from_state_tiles(lr, s, tc).astype(BF16)
            lib = _from_state_tiles(li, s, tc).astype(BF16)
            dbbd[0, s] += _dot(ub, lrb, TN)
            dbbd[1, s] += _dot(ub, lib, TN)
            du_ref[:, cols] += _dot(lrb, bbd[0, s], NT) + _dot(lib, bbd[1, s], NT)

        @pl.when(c == nch - 1)
        def _():
            dar_ref[...] = accr[...]
            dai_ref[...] = acci[...]
            mask_b, mask_c = _s5_group_masks()
            for k in range(2):
                for s in range(sbk):
                    mb = jnp.where(mask_b, dbbd[k, s], 0.0)
                    fold = functools.reduce(
                        lambda a, b: a + b, [mb[:, 128 * j:128 * (j + 1)] for j in range(S5_W // 128)])
                    dbb_ref[k, s] = fold + pltpu.roll(fold, S5_P, 1)
                    mc = jnp.where(mask_c, dccd[k, s], 0.0)
                    dcc_ref[k, s] = functools.reduce(
                        lambda a, b: a + b, [mc[S5_P * j:S5_P * (j + 1), :] for j in range(S5_GB)])

    tiles = pltpu.VMEM((sbk, tc * 8, 128), F32)
    carry = pltpu.VMEM((sbk, 8, 128), F32)
    ins = [u, dz, h0, bb2, cc2, a_re, a_im, d_skip]
    in_specs = [sp["act"], sp["act"], sp["h"], sp["bb"], sp["cc"], sp["a"], sp["a"], sp["d"]]
    _dep_operand(ins, in_specs, dep)
    return pl.pallas_call(
        body, grid=(nb // sbk, nch),
        in_specs=in_specs,
        out_specs=[sp["act"], sp["d"], sp["a"], sp["a"], sp["bb"], sp["cc"]],
        out_shape=[jax.ShapeDtypeStruct((T, D), F32), jax.ShapeDtypeStruct((1, D), F32),
                   jax.ShapeDtypeStruct((nb, 8, 128), F32), jax.ShapeDtypeStruct((nb, 8, 128), F32),
                   jax.ShapeDtypeStruct((2, nb, S5_CH, 128), F32), jax.ShapeDtypeStruct((2, nb, S5_P, S5_CH), F32)],
        scratch_shapes=[tiles, tiles, tiles, tiles, carry, carry, carry, carry, carry, carry,
                        pltpu.VMEM((2, sbk, S5_CH, S5_W), BF16), pltpu.VMEM((2, sbk, S5_W, S5_CH), BF16),
                        pltpu.VMEM((2, sbk, S5_CH, S5_W), F32), pltpu.VMEM((2, sbk, S5_W, S5_CH), F32)],
        compiler_params=_params(("parallel", "arbitrary")), name="s5_bwd",
    )(*ins)


def _s5_disc(lr, li, ldt):
    dt = jnp.exp(ldt)
    mag = jnp.exp(lr * dt)
    ang = li * dt
    cs, sn = jnp.cos(ang), jnp.sin(ang)
    lbr, lbi = mag * cs, mag * sn
    nr = lbr - 1.0
    den = lr * lr + li * li
    f_re = (nr * lr + lbi * li) / den
    f_im = (lbi * lr - nr * li) / den
    return dt, mag, cs, sn, lbr, lbi, nr, den, f_re, f_im


def _s5_param_fwd(lr, li, ldt, bt_re, bt_im):
    c, g, p = bt_re.shape

    def body(lr_ref, li_ref, ldt_ref, br_ref, bi_ref, lbr_ref, lbi_ref, bbr_ref, bbi_ref):
        _, _, _, _, lbr, lbi, _, _, f_re, f_im = _s5_disc(lr_ref[...], li_ref[...], ldt_ref[...])
        lbr_ref[...] = lbr
        lbi_ref[...] = lbi
        for ch in range(c):
            b_r, b_i = br_ref[ch], bi_ref[ch]
            bbr_ref[ch] = f_re * b_r - f_im * b_i
            bbi_ref[ch] = f_re * b_i + f_im * b_r

    gp = jax.ShapeDtypeStruct((g, p), F32)
    cgp = jax.ShapeDtypeStruct((c, g, p), F32)
    return pl.pallas_call(body, out_shape=[gp, gp, cgp, cgp], name="s5_param_fwd")(lr, li, ldt, bt_re, bt_im)


def _s5_param_bwd(lr, li, ldt, bt_re, bt_im, dlbr, dlbi, dbbr, dbbi):
    c, g, p = bt_re.shape

    def body(lr_ref, li_ref, ldt_ref, br_ref, bi_ref, dlbr_ref, dlbi_ref, dbbr_ref, dbbi_ref,
             dlr_ref, dli_ref, dldt_ref, dbr_ref, dbi_ref):
        l_r, l_i = lr_ref[...], li_ref[...]
        dt, mag, cs, sn, lbr, lbi, nr, den, f_re, f_im = _s5_disc(l_r, l_i, ldt_ref[...])
        dfr = jnp.zeros_like(l_r)
        dfi = jnp.zeros_like(l_r)
        for ch in range(c):
            b_r, b_i = br_ref[ch], bi_ref[ch]
            g_r, g_i = dbbr_ref[ch], dbbi_ref[ch]
            dbr_ref[ch] = f_re * g_r + f_im * g_i
            dbi_ref[ch] = f_re * g_i - f_im * g_r
            dfr = dfr + g_r * b_r + g_i * b_i
            dfi = dfi + g_i * b_r - g_r * b_i
        inv = 1.0 / den
        d_nr = (dfr * l_r - dfi * l_i) * inv
        d_lbi = (dfr * l_i + dfi * l_r) * inv + dlbi_ref[...]
        d_lbr = d_nr + dlbr_ref[...]
        d_den = -(dfr * f_re + dfi * f_im) * inv
        d_mag = d_lbr * cs + d_lbi * sn
        d_ang = d_lbi * lbr - d_lbr * lbi
        dlr_ref[...] = (dfr * nr + dfi * lbi) * inv + 2.0 * d_den * l_r + d_mag * mag * dt
        dli_ref[...] = (dfr * lbi - dfi * nr) * inv + 2.0 * d_den * l_i + d_ang * dt
        dldt_ref[...] = jnp.sum(d_mag * mag * l_r + d_ang * l_i, axis=1, keepdims=True) * dt

    gp = jax.ShapeDtypeStruct((g, p), F32)
    cgp = jax.ShapeDtypeStruct((c, g, p), F32)
    return pl.pallas_call(body, out_shape=[gp, gp, jax.ShapeDtypeStruct((g, 1), F32), cgp, cgp],
                          name="s5_param_bwd")(lr, li, ldt, bt_re, bt_im, dlbr, dlbi, dbbr, dbbi)


def _att_masks(rep, gb):
    rows = rep * ATT_BLK
    qi = lax.broadcasted_iota(jnp.int32, (rows, 2 * ATT_BLK), 0) % ATT_BLK
    si = lax.broadcasted_iota(jnp.int32, (rows, 2 * ATT_BLK), 1)
    prev = (si < ATT_BLK) & (si >= qi) & (gb > 0)
    cur = (si >= ATT_BLK) & (si - ATT_BLK <= qi)
    return prev | cur


def _att_rows(start, dil):
    return pl.ds(start, ATT_BLK) if dil == 1 else pl.ds(start, ATT_BLK, stride=dil)


def _att_plan(T, dil):
    span = ATT_BLK * dil
    sbr = max(span, min(T, 1024))
    return span, sbr, T // sbr


def _att_block(sb, i, sbr, span, dil):
    loc = (i // dil) * span + i % dil
    cur = sb * sbr + loc
    gb = sb * (sbr // span) + i // dil
    return loc, cur, jnp.where(gb > 0, cur - span, cur), gb


def _att_fwd(q, k, v, grp, dil):
    T = q.shape[0]
    H = q.shape[1] // HEAD_DIM // len(DILATIONS)
    rep = H // N_KV_HEADS
    span, sbr, nsb = _att_plan(T, dil)
    scale = HEAD_DIM ** -0.5

    def body(*refs):
        q_refs = refs[:rep]
        k_ref, v_ref, o_ref, l_ref, o_slab, l_slab = refs[rep:]
        sb = pl.program_id(1)

        def blk(i, _):
            loc, cur, prv, gb = _att_block(sb, i, sbr, span, dil)
            rows = _att_rows(loc, dil)
            qs = jnp.concatenate([r[rows, :] for r in q_refs], axis=0).astype(BF16)
            kcat = jnp.concatenate([k_ref[_att_rows(prv, dil), :], k_ref[_att_rows(cur, dil), :]], axis=0)
            vcat = jnp.concatenate([v_ref[_att_rows(prv, dil), :], v_ref[_att_rows(cur, dil), :]], axis=0)
            s = jnp.where(_att_masks(rep, gb), _dot(qs, kcat.astype(BF16), NT) * scale, NEG_INF)
            m = jnp.max(s, axis=-1, keepdims=True)
            p = jnp.exp(s - m)
            l = jnp.sum(p, axis=-1, keepdims=True)
            o = _dot(p.astype(BF16), vcat.astype(BF16)) / l
            lse = jnp.broadcast_to(m + jnp.log(l), (rep * ATT_BLK, HEAD_DIM))
            for j in range(rep):
                o_slab[j, rows, :] = o[j * ATT_BLK:(j + 1) * ATT_BLK]
                l_slab[j, rows, :] = lse[j * ATT_BLK:(j + 1) * ATT_BLK]
            return 0

        lax.fori_loop(0, sbr // ATT_BLK, blk, 0)
        for j in range(rep):
            o_ref[:, j * HEAD_DIM:(j + 1) * HEAD_DIM] = o_slab[j]
            l_ref[:, j * HEAD_DIM:(j + 1) * HEAD_DIM] = l_slab[j]

    qspecs = [pl.BlockSpec((sbr, HEAD_DIM), functools.partial(lambda h, s, j: (s, grp * H + h * rep + j), j=j))
              for j in range(rep)]
    kspec = pl.BlockSpec((T, HEAD_DIM), lambda h, s: (0, h))
    ospec = pl.BlockSpec((sbr, rep * HEAD_DIM), lambda h, s: (s, h))
    slab = pltpu.VMEM((rep, sbr, HEAD_DIM), F32)
    return pl.pallas_call(
        body, grid=(N_KV_HEADS, nsb), in_specs=qspecs + [kspec, kspec], out_specs=[ospec, ospec],
        out_shape=[jax.ShapeDtypeStruct((T, H * HEAD_DIM), F32)] * 2, scratch_shapes=[slab, slab],
        compiler_params=_params(("parallel", "arbitrary")), name=f"att_fwd_d{dil}",
    )(*([q] * rep), k, v)


def _att_combine(outs, lses):
    T, W = outs[0].shape
    ng = len(outs)
    tr, tcol = _tile(T, 512, 8), _tile(W, 512)

    def body(*refs):
        o_refs, l_refs = refs[:ng], refs[ng:2 * ng]
        ob_ref, lse_ref = refs[2 * ng:]
        ls = [r[...] for r in l_refs]
        m = functools.reduce(jnp.maximum, ls)
        es = [jnp.exp(l - m) for l in ls]
        den = functools.reduce(lambda a, b: a + b, es)
        num = functools.reduce(lambda a, b: a + b, [e * o[...] for e, o in zip(es, o_refs)])
        ob_ref[...] = (num / den).astype(BF16)
        lse_ref[...] = m + jnp.log(den)

    spec = pl.BlockSpec((tr, tcol), lambda i, j: (i, j))
    return pl.pallas_call(
        body, grid=(T // tr, W // tcol), in_specs=[spec] * (2 * ng), out_specs=[spec, spec],
        out_shape=[jax.ShapeDtypeStruct((T, W), BF16), jax.ShapeDtypeStruct((T, W), F32)],
        compiler_params=_params(("parallel", "parallel")), name="att_combine")(*outs, *lses)


STAT_LANE = HEAD_DIM // 2


def _att_stats(lse, o, do):
    T, W = lse.shape
    tr = _tile(T, 256, 16)

    def body(l_ref, o_ref, do_ref, s_ref):
        lane = lax.broadcasted_iota(jnp.int32, (tr, HEAD_DIM), 1)
        for h in range(W // HEAD_DIM):
            cols = slice(h * HEAD_DIM, (h + 1) * HEAD_DIM)
            delta = jnp.sum(do_ref[:, cols] * o_ref[:, cols].astype(F32), axis=-1, keepdims=True)
            s_ref[:, cols] = jnp.where(lane < STAT_LANE, l_ref[:, cols], delta)

    spec = pl.BlockSpec((tr, W), lambda i: (i, 0))
    return pl.pallas_call(
        body, grid=(T // tr,), in_specs=[spec] * 3, out_specs=spec,
        out_shape=jax.ShapeDtypeStruct((T, W), F32),
        compiler_params=_params(("parallel",)), name="att_stats")(lse, o, do)


def _att_bwd(q, k, v, do, stats, dq, grp, dil):
    T = q.shape[0]
    H = do.shape[1] // HEAD_DIM
    rep = H // N_KV_HEADS
    hs = rep
    span, sbr, nsb = _att_plan(T, dil)
    scale = HEAD_DIM ** -0.5

    def body(*refs):
        q_refs, do_refs, st_refs = refs[:hs], refs[hs + 2:2 * hs + 2], refs[2 * hs + 2:3 * hs + 2]
        k_ref, v_ref = refs[hs], refs[hs + 1]
        dq_ref, dk_ref, dv_ref, dq_slab = refs[3 * hs + 3:]
        sb = pl.program_id(2)

        @pl.when((pl.program_id(1) == 0) & (sb == 0))
        def _():
            dk_ref[...] = jnp.zeros_like(dk_ref)
            dv_ref[...] = jnp.zeros_like(dv_ref)

        def blk(i, _):
            loc, cur, prv, gb = _att_block(sb, i, sbr, span, dil)
            rows, kc, kp = _att_rows(loc, dil), _att_rows(cur, dil), _att_rows(prv, dil)
            qs = jnp.concatenate([r[rows, :] for r in q_refs], axis=0).astype(BF16)
            dos = jnp.concatenate([r[rows, :] for r in do_refs], axis=0).astype(BF16)
            st = jnp.concatenate([r[rows, :] for r in st_refs], axis=0)
            kcat = jnp.concatenate([k_ref[kp, :], k_ref[kc, :]], axis=0).astype(BF16)
            vcat = jnp.concatenate([v_ref[kp, :], v_ref[kc, :]], axis=0).astype(BF16)
            s = _dot(qs, kcat, NT) * scale
            p = jnp.where(_att_masks(hs, gb), jnp.exp(s - st[:, 0:1]), 0.0)
            dp = _dot(dos, vcat, NT)
            ds = (p * (dp - st[:, STAT_LANE:STAT_LANE + 1]) * scale).astype(BF16)
            dvc = _dot(p.astype(BF16), dos, TN)
            dkc = _dot(ds, qs, TN)
            dqs = _dot(ds, kcat)
            for j in range(hs):
                dq_slab[j, rows, :] = dqs[j * ATT_BLK:(j + 1) * ATT_BLK]
            dk_ref[kc, :] += dkc[ATT_BLK:]
            dv_ref[kc, :] += dvc[ATT_BLK:]

            @pl.when(gb > 0)
            def _():
                dk_ref[kp, :] += dkc[:ATT_BLK]
                dv_ref[kp, :] += dvc[:ATT_BLK]

            return 0

        lax.fori_loop(0, sbr // ATT_BLK, blk, 0)
        for j in range(hs):
            dq_ref[:, j * HEAD_DIM:(j + 1) * HEAD_DIM] = dq_slab[j].astype(BF16)

    def head_specs(col0):
        return [pl.BlockSpec((sbr, HEAD_DIM),
                             functools.partial(lambda h, f, s, j: (s, col0 + h * rep + f * hs + j), j=j))
                for j in range(hs)]

    kspec = pl.BlockSpec((T, HEAD_DIM), lambda h, f, s: (0, h))
    dqspec = pl.BlockSpec((sbr, hs * HEAD_DIM), lambda h, f, s: (s, (grp * H + h * rep) // hs + f))
    n_in = 3 * hs + 3
    return pl.pallas_call(
        body, grid=(N_KV_HEADS, rep // hs, nsb),
        in_specs=head_specs(grp * H) + [kspec, kspec] + head_specs(0) + head_specs(0) + [ANY],
        out_specs=[dqspec, kspec, kspec],
        out_shape=[jax.ShapeDtypeStruct(dq.shape, BF16),
                   jax.ShapeDtypeStruct((T, N_KV_HEADS * HEAD_DIM), F32),
                   jax.ShapeDtypeStruct((T, N_KV_HEADS * HEAD_DIM), F32)],
        scratch_shapes=[pltpu.VMEM((hs, sbr, HEAD_DIM), F32)],
        input_output_aliases={n_in - 1: 0},
        compiler_params=_params(("parallel", "arbitrary", "arbitrary")), name=f"att_bwd_d{dil}",
    )(*([q] * hs), k, v, *([do] * hs), *([stats] * hs), dq)


def _sum_kv(dks, dvs):
    T, W = dks[0].shape
    ng = len(dks)
    tr = _tile(T, 512, 8)

    def body(*refs):
        o_ref = refs[2 * ng]
        o_ref[0] = functools.reduce(lambda a, b: a + b, [r[...] for r in refs[:ng]]).astype(BF16)
        o_ref[1] = functools.reduce(lambda a, b: a + b, [r[...] for r in refs[ng:2 * ng]]).astype(BF16)

    spec = pl.BlockSpec((tr, W), lambda i: (i, 0))
    return pl.pallas_call(
        body, grid=(T // tr,), in_specs=[spec] * (2 * ng),
        out_specs=pl.BlockSpec((2, tr, W), lambda i: (0, i, 0)),
        out_shape=jax.ShapeDtypeStruct((2, T, W), BF16),
        compiler_params=_params(("parallel",)), name="sum_kv")(*dks, *dvs)


def _local_step(x, tgt, w, wts, ready, dep0=None, small_ready=None):
    T, D = x.shape
    g = {}

    (u0,) = _rms_fwd("rms_a", x, [w["a_norm"]], [F32], dep=dep0)
    lbr, lbi, bbt_re, bbt_im = _s5_param_fwd(w["lam_re"], w["lam_im"], w["log_dt"], w["bt_re"], w["bt_im"])
    a_re, a_im = lbr.reshape(-1, 8, 128), lbi.reshape(-1, 8, 128)
    C, G, P = w["bt_re"].shape
    nb = G // S5_GB
    bb2 = jnp.stack([bbt_re, bbt_im]).transpose(0, 2, 1, 3).reshape(2, nb, S5_GB * C, P)
    bb2 = jnp.concatenate([bb2, bb2], axis=-1)
    cc2 = jnp.stack([w["c_re"], w["c_im"]]).reshape(2, nb, S5_GB * C, P).transpose(0, 1, 3, 2)
    z, h0 = _s5_fwd(u0, bb2, cc2, a_re, a_im, w["s5_d"])
    w_glu = wts("glu", z)
    x1, vg = _mm_dual_fwd("glu_fwd", z, w_glu, x, "glu")

    def ffn_fwd(xin, layer):
        (nrm,) = _rms_fwd(f"rms_f{layer}", xin, [w["ffn_norm"][layer:layer + 1]], [BF16])
        w_in = wts(f"win{layer}", nrm)
        gu, act = _mm_dual_fwd(f"ffn_in{layer}", nrm, w_in, None, "ffn")
        w_out = wts(f"wout{layer}", act)
        xout = _mm_nn(f"ffn_out{layer}", act, w_out, res=xin)
        return xout, (nrm, gu, act, w_in, w_out)

    x2, saved0 = ffn_fwd(x1, 0)
    kvn, hb = _rms_fwd("rms_b", x2, [w["kv_norm"], w["b_norm"]], [BF16, BF16])
    w_kv, w_q, w_o = wts("wkv", hb), wts("wq", hb), wts("wo", hb)
    k, v = _mm_kv("kv_proj", kvn, w_kv)
    q = _mm_nn("q_proj", hb, w_q)
    outs, lses = [], []
    for grp, dil in enumerate(DILATIONS):
        o_g, l_g = _att_fwd(q, k, v, grp, dil)
        outs.append(o_g)
        lses.append(l_g)
    o, lse = _att_combine(outs, lses)
    x3 = _mm_nn("o_proj", o, w_o, res=x2)
    x4, saved1 = ffn_fwd(x3, 1)
    loss_blk, dx4, dx4b, g["final_norm"] = _loss_head(x4, w["final_norm"], tgt)

    def ffn_bwd(dx, dxb, xin, saved, layer, dep):
        nrm, gu, act, w_in, w_out = saved
        dgu = _mm_nt_ffn_bwd(f"ffn_dact{layer}", dxb, w_out, gu, dep=dep)
        g_wout = _mm_tn(f"ffn_dwout{layer}", act, dxb)
        g_win = _mm_tn_pair(f"ffn_dwin{layer}", nrm, dgu)
        dn = _mm_nt_pair(f"ffn_dn{layer}", dgu, w_in)
        dxo, dxob, (dgn,) = _rms_bwd(f"rms_f{layer}_bwd", xin, dx, [w["ffn_norm"][layer:layer + 1]], [dn])
        tok = ready({f"win{layer}": g_win, f"wout{layer}": g_wout})
        return dxo, dxob, dgn, tok

    dx3, dx3b, dfn1, tok = ffn_bwd(dx4, dx4b, x3, saved1, 1, None)
    do = _mm_nt("o_proj_dx", [dx3b], w_o, dep=tok)
    g_wo = _mm_tn("o_proj_dw", o, dx3b)
    stats = _att_stats(lse, o, do)
    dq = lax.empty(q.shape, BF16)
    dks, dvs = [], []
    for grp, dil in enumerate(DILATIONS):
        dq, dk_g, dv_g = _att_bwd(q, k, v, do, stats, dq, grp, dil)
        dks.append(dk_g)
        dvs.append(dv_g)
    dkv = _sum_kv(dks, dvs)
    dhb = _mm_nt("q_proj_dx", [dq], w_q)
    g_wq = _mm_tn("q_proj_dw", hb, dq)
    dkvn = _mm_nt_pair("kv_proj_dx", dkv, w_kv)
    g_wkv = _mm_tn_pair("kv_proj_dw", kvn, dkv)
    dx2, dx2b, (g["kv_norm"], g["b_norm"]) = _rms_bwd(
        "rms_b_bwd", x2, dx3, [w["kv_norm"], w["b_norm"]], [dkvn, dhb])
    tok = ready({"wkv": g_wkv, "wq": g_wq, "wo": g_wo})
    dx1, dx1b, dfn0, tok = ffn_bwd(dx2, dx2b, x1, saved0, 0, tok)
    g["ffn_norm"] = jnp.concatenate([dfn0, dfn1], axis=0)

    dvg = _glu_bwd(dx1, vg, dep=tok)
    dz = _mm_nt_pair("glu_dx", dvg, w_glu)
    tok = ready({"glu": _mm_tn_pair("glu_dw", z, dvg)})
    tok = ready({}, after=tok)
    du, g["s5_d"], da_re, da_im, dbb2, dcc2 = _s5_bwd(
        u0, dz, h0, bb2, cc2, a_re, a_im, w["s5_d"], dep=tok)
    dcc = dcc2.transpose(0, 1, 3, 2).reshape(2, G, C, P)
    g["c_re"], g["c_im"] = dcc[0], -dcc[1]
    dbbt = dbb2[..., :P].reshape(2, G, C, P).transpose(0, 2, 1, 3)
    g["lam_re"], g["lam_im"], g["log_dt"], g["bt_re"], g["bt_im"] = _s5_param_bwd(
        w["lam_re"], w["lam_im"], w["log_dt"], w["bt_re"], w["bt_im"],
        da_re.reshape(G, P), da_im.reshape(G, P), dbbt[0], dbbt[1])
    tok = small_ready(g) if small_ready is not None else None
    grad_x, _, (g["a_norm"],) = _rms_bwd("rms_a_bwd", x, dx1, [w["a_norm"]], [du], dep=tok)
    return loss_blk, grad_x, g


def _coords():
    return lax.axis_index("x"), lax.axis_index("y"), lax.axis_index("c")


def _dev_index(dev):
    return 4 * dev[0] + 2 * dev[1] + dev[2]


def _shard_window(ref, axis, width, idx):
    sl = [slice(None)] * len(ref.shape)
    sl[axis] = pl.ds(pl.multiple_of(idx * width, width), width)
    return ref.at[tuple(sl)]


def _all_gather(name, shards, axes):
    na = len(shards)
    widths = [s.shape[ax] for s, ax in zip(shards, axes)]
    out_shapes = []
    for s, ax in zip(shards, axes):
        shp = list(s.shape)
        shp[ax] *= N_DEV
        out_shapes.append(jax.ShapeDtypeStruct(tuple(shp), s.dtype))

    def body(*refs):
        ins, outs = refs[:na], refs[na:2 * na]
        send_sems, recv_sems, local_sems = refs[2 * na:]
        x, y, c = _coords()
        me, sib = (x, y, c), (x, y, 1 - c)
        chips = [(1 - x, y), (x, 1 - y), (1 - x, 1 - y)]

        def blk(a, dev):
            return _shard_window(outs[a], axes[a], widths[a], _dev_index(dev))

        def copy(a, kk, block, to, src=None):
            return pltpu.make_async_remote_copy(
                src_ref=blk(a, block) if src is None else src, dst_ref=blk(a, block),
                send_sem=send_sems.at[a, kk], recv_sem=recv_sems.at[a, kk],
                device_id=to, device_id_type=MESH)

        local = [pltpu.make_async_copy(ins[a], blk(a, me), local_sems.at[a]) for a in range(na)]
        for cp in local:
            cp.start()
        sent = []
        for a in range(na):
            first = [copy(a, 0, me, sib, src=ins[a])]
            first += [copy(a, 1 + j, me, (*chip, c), src=ins[a]) for j, chip in enumerate(chips)]
            for cp in first:
                cp.start()
            sent += first
        for a in range(na):
            for j, chip in enumerate(chips):
                copy(a, 1 + j, (*chip, c), me).wait_recv()
                fwd = copy(a, 4 + j, (*chip, c), sib)
                fwd.start()
                sent.append(fwd)
        for a in range(na):
            copy(a, 0, sib, me).wait_recv()
            for j, chip in enumerate(chips):
                copy(a, 4 + j, (*chip, 1 - c), me).wait_recv()
        for cp in sent:
            cp.wait_send()
        for cp in local:
            cp.wait()

    return pl.pallas_call(
        body, out_shape=out_shapes, in_specs=[ANY] * na, out_specs=[ANY] * na,
        scratch_shapes=[pltpu.SemaphoreType.DMA((na, 7)), pltpu.SemaphoreType.DMA((na, 7)),
                        pltpu.SemaphoreType.DMA((na,))],
        name=name)(*shards)


HBM = pl.BlockSpec(memory_space=pltpu.HBM)
SEM = pl.BlockSpec(memory_space=pltpu.SEMAPHORE)
TOKEN_SPEC = pl.BlockSpec(memory_space=pltpu.VMEM)
TOKEN_SHAPE = jax.ShapeDtypeStruct((8, 128), F32)
SPLIT_PARAMS = pltpu.CompilerParams(has_side_effects=pltpu.SideEffectType.DATAFLOW_SIDE_EFFECTING)


def _hbm(x):
    return pltpu.with_memory_space_constraint(x, pltpu.HBM)


def _hbm_like(x):
    return pltpu.HBM(x.shape, x.dtype)


def _dma_sems(*shape):
    return pltpu.SemaphoreType.DMA(shape)


def _cast_and_place(name, shard, layer, axis, pos, dtype):
    rows, cols = shard.shape[-2:]
    tr = _tile(rows, 256, 16)
    nt = rows // tr
    full = (rows, cols * N_DEV) if axis == 1 else (rows * N_DEV, cols)

    def dev(p):
        return 4 * p[0] + 2 * p[1] + p[2]

    def body(pos_ref, s_ref, b_ref, l_ref):
        v = s_ref[...].astype(dtype)
        b_ref[...] = v
        l_ref[...] = v

    blk = pl.BlockSpec((tr, cols), lambda i, p: (i, 0))
    if axis == 1:
        lspec = pl.BlockSpec((tr, cols), lambda i, p: (i, dev(p)))
    else:
        lspec = pl.BlockSpec((tr, cols), lambda i, p: (dev(p) * nt + i, 0))
    return pl.pallas_call(
        body, grid_spec=pltpu.PrefetchScalarGridSpec(
            num_scalar_prefetch=1, grid=(nt,),
            in_specs=[pl.BlockSpec((None, tr, cols), lambda i, p: (layer, i, 0))], out_specs=[blk, lspec]),
        out_shape=[jax.ShapeDtypeStruct((rows, cols), dtype), jax.ShapeDtypeStruct(full, dtype)],
        compiler_params=_params(("parallel",)), name=name)(pos, shard)


def _gather_start(name, shards, lands, axes, groups):
    na, ng = len(shards), len(groups)
    widths = [s.shape[ax] for s, ax in zip(shards, axes)]

    def body(*refs):
        sh, ld = refs[:na], refs[na:2 * na]
        sems = refs[2 * na:2 * na + 3 * ng]
        token = refs[-1]
        x, y, c = _coords()
        me, sib = (x, y, c), (x, y, 1 - c)
        chips = [(1 - x, y), (x, 1 - y), (1 - x, 1 - y)]
        for gi, grp in enumerate(groups):
            send, r_d2d, r_ici = sems[3 * gi:3 * gi + 3]
            for li, a in enumerate(grp):
                dst = _shard_window(ld[a], axes[a], widths[a], _dev_index(me))
                pltpu.make_async_remote_copy(
                    src_ref=sh[a], dst_ref=dst, send_sem=send.at[4 * li], recv_sem=r_d2d.at[li],
                    device_id=sib, device_id_type=MESH).start()
                for j, chip in enumerate(chips):
                    pltpu.make_async_remote_copy(
                        src_ref=sh[a], dst_ref=dst, send_sem=send.at[4 * li + 1 + j], recv_sem=r_ici.at[3 * li + j],
                        device_id=(*chip, c), device_id_type=MESH).start()
        token[...] = jnp.zeros_like(token)

    out_shape, out_specs = [], []
    for grp in groups:
        out_shape += [_dma_sems(4 * len(grp)), _dma_sems(len(grp)), _dma_sems(3 * len(grp))]
        out_specs += [SEM] * 3
    out_shape += [_hbm_like(s) for s in shards] + [_hbm_like(l) for l in lands] + [TOKEN_SHAPE]
    out_specs += [HBM] * (2 * na) + [TOKEN_SPEC]
    aliases = {a: 3 * ng + a for a in range(2 * na)}
    res = pl.pallas_call(
        body, name=name, out_shape=out_shape, in_specs=[HBM] * (2 * na),
        out_specs=out_specs, input_output_aliases=aliases, compiler_params=SPLIT_PARAMS,
    )(*[_hbm(s) for s in shards], *[_hbm(l) for l in lands])
    sems = [tuple(res[3 * gi:3 * gi + 3]) for gi in range(ng)]
    return sems, list(res[3 * ng:3 * ng + na]), list(res[3 * ng + na:3 * ng + 2 * na]), res[-1]


def _gather_forward(name, lands, axes, r_ici, after):
    n = len(lands)
    widths = [l.shape[ax] // N_DEV for l, ax in zip(lands, axes)]

    def body(*refs):
        ld, r_ici_ref = refs[:n], refs[n]
        f_send, f_recv = refs[n + 2], refs[n + 3]
        x, y, c = _coords()
        sib = (x, y, 1 - c)
        chips = [(1 - x, y), (x, 1 - y), (1 - x, 1 - y)]
        for li in range(n):
            for j, chip in enumerate(chips):
                blk = _shard_window(ld[li], axes[li], widths[li], _dev_index((*chip, c)))
                pltpu.make_async_remote_copy(
                    src_ref=blk, dst_ref=blk, send_sem=f_send.at[3 * li + j], recv_sem=r_ici_ref.at[3 * li + j],
                    device_id=(*chip, c), device_id_type=MESH).wait_recv()
                pltpu.make_async_remote_copy(
                    src_ref=blk, dst_ref=blk, send_sem=f_send.at[3 * li + j], recv_sem=f_recv.at[3 * li + j],
                    device_id=sib, device_id_type=MESH).start()

    res = pl.pallas_call(
        body, name=name, out_shape=[_dma_sems(3 * n), _dma_sems(3 * n)] + [_hbm_like(l) for l in lands],
        in_specs=[HBM] * n + [SEM, ANY], out_specs=[SEM, SEM] + [HBM] * n,
        input_output_aliases={li: 2 + li for li in range(n)}, compiler_params=SPLIT_PARAMS,
    )(*lands, r_ici, after)
    return res[0], res[1], list(res[2:])


def _gather_finish(name, shards, lands, axes, send, r_d2d, f_send, f_recv):
    n = len(lands)
    widths = [l.shape[ax] // N_DEV for l, ax in zip(lands, axes)]

    def body(*refs):
        sh, ld = refs[:n], refs[n:2 * n]
        send_ref, r_d2d_ref, f_send_ref, f_recv_ref = refs[2 * n:2 * n + 4]
        x, y, c = _coords()
        me, sib = (x, y, c), (x, y, 1 - c)
        chips = [(1 - x, y), (x, 1 - y), (1 - x, 1 - y)]

        def blk(li, dev):
            return _shard_window(ld[li], axes[li], widths[li], _dev_index(dev))

        for li in range(n):
            for kk in range(4):
                pltpu.make_async_remote_copy(
                    src_ref=sh[li], dst_ref=blk(li, me), send_sem=send_ref.at[4 * li + kk], recv_sem=r_d2d_ref.at[li],
                    device_id=sib, device_id_type=MESH).wait_send()
            pltpu.make_async_remote_copy(
                src_ref=blk(li, sib), dst_ref=blk(li, sib), send_sem=send_ref.at[4 * li], recv_sem=r_d2d_ref.at[li],
                device_id=sib, device_id_type=MESH).wait_recv()
            for j, chip in enumerate(chips):
                pltpu.make_async_remote_copy(
                    src_ref=blk(li, (*chip, c)), dst_ref=blk(li, (*chip, c)), send_sem=f_send_ref.at[3 * li + j],
                    recv_sem=f_recv_ref.at[3 * li + j], device_id=sib, device_id_type=MESH).wait_send()
                pltpu.make_async_remote_copy(
                    src_ref=blk(li, (*chip, 1 - c)), dst_ref=blk(li, (*chip, 1 - c)), send_sem=f_send_ref.at[3 * li + j],
                    recv_sem=f_recv_ref.at[3 * li + j], device_id=sib, device_id_type=MESH).wait_recv()

    res = pl.pallas_call(
        body, name=name, out_shape=[_hbm_like(s) for s in shards] + [_hbm_like(l) for l in lands],
        in_specs=[HBM] * (2 * n) + [SEM] * 4, out_specs=[HBM] * (2 * n),
        input_output_aliases={i: i for i in range(2 * n)}, compiler_params=SPLIT_PARAMS,
    )(*shards, *lands, send, r_d2d, f_send, f_recv)
    return list(res[n:])


def _chip_exchange_start(name, parts):
    n = len(parts)

    def body(*refs):
        src, ld = refs[:n], refs[n:2 * n]
        send, recv = refs[2 * n], refs[2 * n + 1]
        token = refs[-1]
        x, y, c = _coords()
        chips = [(1 - x, y), (x, 1 - y), (1 - x, 1 - y)]
        for li in range(n):
            for kk, chip in enumerate(chips):
                pltpu.make_async_remote_copy(
                    src_ref=src[li].at[kk], dst_ref=ld[li].at[kk], send_sem=send.at[3 * li + kk],
                    recv_sem=recv.at[3 * li + kk], device_id=(*chip, c), device_id_type=MESH).start()
        token[...] = jnp.zeros_like(token)

    lands = [lax.empty(p.shape, p.dtype) for p in parts]
    res = pl.pallas_call(
        body, name=name,
        out_shape=[_dma_sems(3 * n), _dma_sems(3 * n)] + [_hbm_like(p) for p in parts] * 2 + [TOKEN_SHAPE],
        in_specs=[HBM] * (2 * n), out_specs=[SEM, SEM] + [HBM] * (2 * n) + [TOKEN_SPEC],
        input_output_aliases={i: 2 + i for i in range(2 * n)}, compiler_params=SPLIT_PARAMS,
    )(*[_hbm(p) for p in parts], *[_hbm(l) for l in lands])
    return res[0], res[1], list(res[2:2 + n]), list(res[2 + n:2 + 2 * n]), res[-1]


def _chip_exchange_finish(name, started, after):
    counts = [len(st[2]) for st in started]
    total = sum(counts)
    ns = len(started)

    def body(*refs):
        src, ld = refs[:total], refs[total:2 * total]
        sems = refs[2 * total:2 * total + 2 * ns]
        x, y, c = _coords()
        chips = [(1 - x, y), (x, 1 - y), (1 - x, 1 - y)]
        off = 0
        for si, cnt in enumerate(counts):
            send, recv = sems[2 * si], sems[2 * si + 1]
            for li in range(cnt):
                for kk, chip in enumerate(chips):
                    cp = pltpu.make_async_remote_copy(
                        src_ref=src[off + li].at[kk], dst_ref=ld[off + li].at[kk], send_sem=send.at[3 * li + kk],
                        recv_sem=recv.at[3 * li + kk], device_id=(*chip, c), device_id_type=MESH)
                    cp.wait_send()
                    cp.wait_recv()
            off += cnt

    srcs = [p for st in started for p in st[2]]
    lands = [l for st in started for l in st[3]]
    sems = [s for st in started for s in st[:2]]
    res = pl.pallas_call(
        body, name=name, out_shape=[_hbm_like(p) for p in srcs + lands],
        in_specs=[HBM] * (2 * total) + [SEM] * (2 * ns) + [ANY], out_specs=[HBM] * (2 * total),
        input_output_aliases={i: i for i in range(2 * total)}, compiler_params=SPLIT_PARAMS,
    )(*srcs, *lands, *sems, after)
    out, off = [], total
    for cnt in counts:
        out.append(list(res[off:off + cnt]))
        off += cnt
    return out


def _pair_copies(src, ld, send, recv, axes, widths):
    x, y, c = _coords()
    chips = [(x, y), (1 - x, y), (x, 1 - y), (1 - x, 1 - y)]
    return [pltpu.make_async_remote_copy(
        src_ref=_shard_window(src[li], axes[li], widths[li], _dev_index((*chip, 1 - c))),
        dst_ref=ld[li].at[kk], send_sem=send.at[4 * li + kk], recv_sem=recv.at[4 * li + kk],
        device_id=(x, y, 1 - c), device_id_type=MESH)
        for li in range(len(src)) for kk, chip in enumerate(chips)]


def _pair_exchange_start(name, grads, axes):
    n = len(grads)
    widths = [gr.shape[ax] // N_DEV for gr, ax in zip(grads, axes)]
    lands = []
    for gr, ax, wd in zip(grads, axes, widths):
        shp = list(gr.shape)
        shp[ax] = wd
        lands.append(lax.empty((4, *shp), gr.dtype))

    def body(*refs):
        for cp in _pair_copies(refs[:n], refs[n:2 * n], refs[2 * n], refs[2 * n + 1], axes, widths):
            cp.start()
        refs[-1][...] = jnp.zeros_like(refs[-1])

    res = pl.pallas_call(
        body, name=name,
        out_shape=[_dma_sems(4 * n), _dma_sems(4 * n)] + [_hbm_like(a) for a in grads + lands] + [TOKEN_SHAPE],
        in_specs=[HBM] * (2 * n), out_specs=[SEM, SEM] + [HBM] * (2 * n) + [TOKEN_SPEC],
        input_output_aliases={i: 2 + i for i in range(2 * n)}, compiler_params=SPLIT_PARAMS,
    )(*[_hbm(a) for a in grads + lands])
    return res[0], res[1], list(res[2:2 + n]), list(res[2 + n:2 + 2 * n]), res[-1]


def _pair_exchange_finish(name, started, axes, after):
    send, recv, grads, lands, _ = started
    n = len(grads)
    widths = [gr.shape[ax] // N_DEV for gr, ax in zip(grads, axes)]

    def body(*refs):
        for cp in _pair_copies(refs[:n], refs[n:2 * n], refs[2 * n], refs[2 * n + 1], axes, widths):
            cp.wait_send()
            cp.wait_recv()

    res = pl.pallas_call(
        body, name=name, out_shape=[_hbm_like(a) for a in grads + lands],
        in_specs=[HBM] * (2 * n) + [SEM, SEM, ANY], out_specs=[HBM] * (2 * n),
        input_output_aliases={i: i for i in range(2 * n)}, compiler_params=SPLIT_PARAMS,
    )(*grads, *lands, send, recv, after)
    return list(res[:n]), list(res[n:])


def _pair_sum(name, grad, land, axis, pos):
    wd = grad.shape[axis] // N_DEV
    shard_shape = land.shape[1:]
    rows, cols = shard_shape
    tr = _tile(rows, 256, 16)
    nt = rows // tr

    def dev_of(kk, pos_ref):
        return 4 * (pos_ref[0] ^ (kk & 1)) + 2 * (pos_ref[1] ^ (kk >> 1)) + pos_ref[2]

    def gspec(kk):
        if axis == 1:
            return pl.BlockSpec((tr, wd), lambda t, p: (t, dev_of(kk, p)))
        return pl.BlockSpec((tr, cols), lambda t, p: (dev_of(kk, p) * nt + t, 0))

    def body(pos_ref, g0, g1, g2, g3, l_ref, p0_ref, pb_ref):
        p0_ref[...] = g0[...].astype(F32) + l_ref[0].astype(F32)
        for kk, g_ref in enumerate((g1, g2, g3)):
            pb_ref[kk] = (g_ref[...].astype(F32) + l_ref[kk + 1].astype(F32)).astype(BF16)

    return pl.pallas_call(
        body,
        grid_spec=pltpu.PrefetchScalarGridSpec(
            num_scalar_prefetch=1, grid=(nt,),
            in_specs=[gspec(kk) for kk in range(4)] + [pl.BlockSpec((4, tr, cols), lambda t, p: (0, t, 0))],
            out_specs=[pl.BlockSpec((tr, cols), lambda t, p: (t, 0)),
                       pl.BlockSpec((3, tr, cols), lambda t, p: (0, t, 0))]),
        out_shape=[jax.ShapeDtypeStruct(shard_shape, F32), jax.ShapeDtypeStruct((3, *shard_shape), BF16)],
        compiler_params=_params(("parallel",)), name=name)(pos, grad, grad, grad, grad, land)


def _adamw(name, parts, w, m, v, layer=None, prev=None):
    rows, cols = w.shape[-2:]
    tr = _tile(rows, 256, 8)
    npart = len(parts)
    c1 = 1.0 - ADAM_B1 ** ADAM_STEP
    c2 = 1.0 - ADAM_B2 ** ADAM_STEP

    def body(*refs):
        p_refs = refs[:npart]
        w_ref, m_ref, v_ref = refs[npart:npart + 3]
        g_ref, d_ref, nm_ref, nv_ref = refs[-4:]
        g = None
        for r in p_refs:
            if len(r.shape) == 3:
                for i in range(r.shape[0]):
                    t = r[i].astype(F32)
                    g = t if g is None else g + t
            else:
                t = r[...].astype(F32)
                g = t if g is None else g + t
        nm = ADAM_B1 * m_ref[...] + (1.0 - ADAM_B1) * g
        nv = ADAM_B2 * v_ref[...] + (1.0 - ADAM_B2) * (g * g)
        g_ref[...] = g
        nm_ref[...] = nm
        nv_ref[...] = nv
        d_ref[...] = -ADAM_LR * ((nm / c1) / (jnp.sqrt(nv / c2) + ADAM_EPS) + ADAM_WD * w_ref[...])

    spec = pl.BlockSpec((tr, cols), lambda i: (i, 0))
    wspec = spec if layer is None else pl.BlockSpec((None, tr, cols), lambda i: (layer, i, 0))
    pspecs = [pl.BlockSpec((p.shape[0], tr, cols), lambda i: (0, i, 0)) if p.ndim == 3 else spec
              for p in parts]
    prev = list(prev) if prev else []
    return pl.pallas_call(
        body, grid=(rows // tr,), in_specs=pspecs + [wspec] * 3 + [ANY] * len(prev), out_specs=[wspec] * 4,
        out_shape=[jax.ShapeDtypeStruct(w.shape, F32)] * 4,
        input_output_aliases={npart + 3 + i: i for i in range(len(prev))},
        compiler_params=_params(("parallel",)), name=name)(*parts, w, m, v, *prev)


SMALL_NAMES = ("s5_lam_re", "s5_lam_im", "s5_log_dt", "s5_b_re", "s5_b_im", "s5_c_re", "s5_c_im",
               "ffn_norm", "b_norm_mix", "kv_norm", "final_norm")
SMALL_PAD = 1024


def _pack(parts):
    flat = []
    for p in parts:
        f = p.reshape(-1)
        pad = (-f.shape[0]) % SMALL_PAD
        if pad:
            f = jnp.concatenate([f, jnp.zeros((pad,), f.dtype)])
        flat.append(f)
    return jnp.concatenate(flat).reshape(-1, 128)


def _unpack(packed, shapes):
    flat = packed.reshape(-1)
    out, off = [], 0
    for shp in shapes:
        size = math.prod(shp)
        out.append(flat[off:off + size].reshape(shp))
        off += size + (-size) % SMALL_PAD
    return out


def kernel(x, s5_lam_re, s5_lam_im, s5_log_dt, s5_b_re, s5_b_im, s5_c_re, s5_c_im, s5_d, s5_w_glu, a_norm_mix, ffn_norm, ffn_w_in, ffn_w_out, b_norm_mix, attn_w_q, attn_w_o, kv_norm, w_kv, final_norm, loss_target, m_s5_lam_re, m_s5_lam_im, m_s5_log_dt, m_s5_b_re, m_s5_b_im, m_s5_c_re, m_s5_c_im, m_s5_d, m_s5_w_glu, m_a_norm_mix, m_ffn_norm, m_ffn_w_in, m_ffn_w_out, m_b_norm_mix, m_attn_w_q, m_attn_w_o, m_kv_norm, m_w_kv, m_final_norm, v_s5_lam_re, v_s5_lam_im, v_s5_log_dt, v_s5_b_re, v_s5_b_im, v_s5_c_re, v_s5_c_im, v_s5_d, v_s5_w_glu, v_a_norm_mix, v_ffn_norm, v_ffn_w_in, v_ffn_w_out, v_b_norm_mix, v_attn_w_q, v_attn_w_o, v_kv_norm, v_w_kv, v_final_norm):
    args = dict(locals())
    T, D = x.shape[1], x.shape[2]
    n_layers = ffn_w_in.shape[0]
    xi_, yi_, ci_ = _coords()
    pos = jnp.stack([xi_, yi_, ci_]).astype(jnp.int32)
    me = 4 * xi_ + 2 * yi_ + ci_

    big_names = ["glu"] + [f"win{l}" for l in range(n_layers)] + [f"wout{l}" for l in range(n_layers)] \
        + ["wkv", "wq", "wo"]
    big_shards = [(s5_w_glu, 0)] + [(ffn_w_in, l) for l in range(n_layers)] \
        + [(ffn_w_out, l) for l in range(n_layers)] + [(w_kv[None], 0), (attn_w_q, 0), (attn_w_o, 0)]
    big_axes = [1] + [1] * n_layers + [0] * n_layers + [0, 1, 0]
    big_out_names = ["s5_w_glu"] + ["ffn_w_in"] * n_layers + ["ffn_w_out"] * n_layers \
        + ["w_kv", "attn_w_q", "attn_w_o"]
    index_of = {n: i for i, n in enumerate(big_names)}
    vec_shard = jnp.concatenate([s5_d, a_norm_mix], axis=0)
    (vecs,) = _all_gather("vectors_all_gather", [vec_shard], [1])

    gather_groups = [["glu"], ["win0"], ["wout0"], ["wkv", "wq", "wo"], ["win1", "wout1"]]
    group_idx = [[index_of[n] for n in grp] for grp in gather_groups]
    group_of = {n: gi for gi, grp in enumerate(gather_groups) for n in grp}
    placed = [_cast_and_place(f"cast_place_{n}", s, l, ax, pos, BF16)
              for n, (s, l), ax in zip(big_names, big_shards, big_axes)]
    gather_sems, shards_thru, lands_thru, start_token = _gather_start(
        "weights_gather_start", [p[0] for p in placed], [p[1] for p in placed], big_axes, group_idx)
    full = {}

    def wts(name, after):
        if name not in full:
            gi = group_of[name]
            idx = group_idx[gi]
            axes = [big_axes[a] for a in idx]
            send, r_d2d, r_ici = gather_sems[gi]
            f_send, f_recv, lands = _gather_forward(
                f"weights_gather_forward{gi}", [lands_thru[a] for a in idx], axes, r_ici, after)
            done = _gather_finish(f"weights_gather_finish{gi}", [shards_thru[a] for a in idx], lands, axes,
                                  send, r_d2d, f_send, f_recv)
            full.update(zip(gather_groups[gi], done))
        return full[name]

    exchanges, pending = [], []

    def flush(after):
        names, axes, started = pending.pop()
        tag = "_".join(names)
        grads, lands = _pair_exchange_finish(f"rs_pair_exchange_finish_{tag}", started, axes, after)
        p0s, pbs = [], []
        for n, gr, land, ax in zip(names, grads, lands, axes):
            p0, pb = _pair_sum(f"rs_pair_sum_{n}", gr, land, ax, pos)
            p0s.append(p0)
            pbs.append(pb)
        started = _chip_exchange_start(f"rs_chip_exchange_start_{tag}", pbs)
        exchanges.append((names, p0s, started))
        return started[4]

    def ready(grads, after=None):
        if not grads:
            return flush(after)
        names = list(grads)
        axes = [big_axes[index_of[n]] for n in names]
        started = _pair_exchange_start(f"rs_pair_exchange_start_{'_'.join(names)}", [grads[n] for n in names], axes)
        token = flush(started[4]) if pending else started[4]
        pending.append((names, axes, started))
        return token

    G, P, C = s5_b_re.shape[1:]
    w = dict(
        a_norm=vecs[1:2], s5_d=vecs[0:1],
        ffn_norm=ffn_norm, b_norm=b_norm_mix, kv_norm=kv_norm.reshape(1, D), final_norm=final_norm.reshape(1, D),
        lam_re=s5_lam_re[0], lam_im=s5_lam_im[0], log_dt=s5_log_dt.reshape(G, 1),
        bt_re=s5_b_re[0].transpose(2, 0, 1), bt_im=s5_b_im[0].transpose(2, 0, 1),
        c_re=s5_c_re[0], c_im=s5_c_im[0],
    )
    small = {}

    def small_ready(g):
        small_g = dict(
            s5_lam_re=g["lam_re"], s5_lam_im=g["lam_im"], s5_log_dt=g["log_dt"],
            s5_b_re=g["bt_re"].transpose(1, 2, 0), s5_b_im=g["bt_im"].transpose(1, 2, 0),
            s5_c_re=g["c_re"], s5_c_im=g["c_im"], ffn_norm=g["ffn_norm"], b_norm_mix=g["b_norm"],
            kv_norm=g["kv_norm"], final_norm=g["final_norm"])
        packed = _pack([small_g[n] for n in SMALL_NAMES])
        shard, land = _cast_and_place("place_small_grads", packed[None], 0, 0, pos, F32)
        sems, thru, lands, token = _gather_start("small_grads_gather_start", [shard], [land], [0], [[0]])
        small.update(sems=sems[0], thru=thru, lands=lands, rows=packed.shape[0])
        return token

    loss_blk, grad_x, g = _local_step(x[0], loss_target[0], w, wts, ready, dep0=start_token,
                                      small_ready=small_ready)
    loss = lax.psum(loss_blk[0, 0], ("x", "y", "c"))
    (tail,) = _all_gather("vector_grads_all_gather", [_pack([g["s5_d"], g["a_norm"]])], [0])

    out = {}

    def put(name, res, shape):
        for kind, r in zip(("grad", "delta", "new_m", "new_v"), res):
            out[f"{kind}_{name}"] = r.reshape(shape)

    updated = {}

    def update(names, p0s, recvd):
        for name, p0, rc in zip(names, p0s, recvd):
            oname = big_out_names[index_of[name]]
            w3, layer = big_shards[index_of[name]]
            updated[oname] = _adamw(f"adamw_{name}", [p0, rc], w3, args["m_" + oname].reshape(w3.shape),
                                    args["v_" + oname].reshape(w3.shape), layer=layer, prev=updated.get(oname))
        return updated[big_out_names[index_of[names[0]]]][0]

    early, last = exchanges[:-1], exchanges[-1]
    landed = _chip_exchange_finish("rs_chip_exchange_finish_early", [e[2] for e in early], grad_x)
    for (names, p0s, _), recvd in zip(early[:-1], landed[:-1]):
        marker = update(names, p0s, recvd)
    send, r_d2d, r_ici = small["sems"]
    f_send, f_recv, small_lands = _gather_forward("small_grads_gather_forward", small["lands"], [0], r_ici, marker)
    marker = update(early[-1][0], early[-1][1], landed[-1])
    (all_parts,) = _gather_finish("small_grads_gather_finish", small["thru"], small_lands, [0],
                                  send, r_d2d, f_send, f_recv)
    (recvd,) = _chip_exchange_finish("rs_chip_exchange_finish_last", [last[2]], marker)
    update(last[0], last[1], recvd)
    for oname, res in updated.items():
        put(oname, res, args[oname].shape)
    all_parts = all_parts.reshape(N_DEV, small["rows"], 128)
    w_pack = _pack([args[n] for n in SMALL_NAMES])
    m_pack = _pack([args["m_" + n] for n in SMALL_NAMES])
    v_pack = _pack([args["v_" + n] for n in SMALL_NAMES])
    res = _adamw("adamw_small", [all_parts], w_pack, m_pack, v_pack)
    shapes = [args[n].shape for n in SMALL_NAMES]
    unpacked = [_unpack(r, shapes) for r in res]
    for i, n in enumerate(SMALL_NAMES):
        put(n, [u[i] for u in unpacked], args[n].shape)
    ws = D // N_DEV
    tail = lax.dynamic_slice_in_dim(tail.reshape(N_DEV, 2, D), me * ws, ws, axis=2)
    res = _adamw("adamw_vec", [tail], vec_shard,
                 jnp.concatenate([m_s5_d, m_a_norm_mix], axis=0), jnp.concatenate([v_s5_d, v_a_norm_mix], axis=0))
    put("s5_d", [r[0:1] for r in res], s5_d.shape)
    put("a_norm_mix", [r[1:2] for r in res], a_norm_mix.shape)

    names = ("s5_lam_re", "s5_lam_im", "s5_log_dt", "s5_b_re", "s5_b_im", "s5_c_re", "s5_c_im", "s5_d",
             "s5_w_glu", "a_norm_mix", "ffn_norm", "ffn_w_in", "ffn_w_out", "b_norm_mix", "attn_w_q",
             "attn_w_o", "kv_norm", "w_kv", "final_norm")
    result = [loss, grad_x.reshape(x.shape)]
    for kind in ("grad", "delta", "new_m", "new_v"):
        result += [out[f"{kind}_{n}"] for n in names]
    return tuple(result)
```

```python
import functools
import math

import jax
import jax.numpy as jnp
from jax import lax
from jax.experimental import pallas as pl
from jax.experimental.pallas import tpu as pltpu

F32 = jnp.float32
BF16 = jnp.bfloat16

EPS = 1e-6
NEG_INF = -1e30
HEAD_DIM = 128
N_KV_HEADS = 4
DILATIONS = (1, 4, 16)
ATT_BLK = 128
S5_C = 16
S5_P = 64
S5_GB = 16
S5_CH = S5_GB * S5_C
S5_W = S5_GB * S5_P
S5_UNROLL = 4
N_DEV = 8

ADAM_LR = 0.001
ADAM_B1 = 0.9
ADAM_B2 = 0.999
ADAM_EPS = 1e-08
ADAM_WD = 0.01
ADAM_STEP = 10

VMEM_LIMIT_BYTES = 56 * 1024 * 1024
MM_TILE = 1024
MM_TILE_NARROW = 512
MM_DEPTH = 2816
MESH = pl.DeviceIdType.MESH
ANY = pl.BlockSpec(memory_space=pl.ANY)


def _tile(n, pref, align=128):
    t = (min(pref, n) // align) * align
    while t >= align:
        if n % t == 0:
            return t
        t -= align
    return n


def _params(sem):
    return pltpu.CompilerParams(dimension_semantics=sem, vmem_limit_bytes=VMEM_LIMIT_BYTES)


def _sigmoid(x):
    return 1.0 / (1.0 + jnp.exp(-x))


NN = (((1,), (0,)), ((), ()))
NT = (((1,), (1,)), ((), ()))
TN = (((0,), (0,)), ((), ()))


def _dot(a, b, dims=NN):
    return lax.dot_general(a, b, dims, preferred_element_type=F32)


def _matmul(name, grid, ins, in_specs, products, dims, out_shapes, out_specs, acc_shapes, epilogue):
    n_in, n_out, nk = len(ins), len(out_shapes), grid[2]

    def body(*refs):
        in_refs = refs[:n_in]
        out_refs = refs[n_in:n_in + n_out]
        acc_refs = refs[n_in + n_out:]

        def prods():
            vals = [None] * len(acc_shapes)
            for ai, bi, ci in products:
                d = _dot(in_refs[ai][...].astype(BF16), in_refs[bi][...].astype(BF16), dims)
                vals[ci] = d if vals[ci] is None else vals[ci] + d
            return vals

        if nk == 1:
            epilogue(in_refs, out_refs, prods())
        else:
            k = pl.program_id(2)

            @pl.when(k == 0)
            def _():
                for a in acc_refs:
                    a[...] = jnp.zeros_like(a)

            for a, v in zip(acc_refs, prods()):
                a[...] += v

            @pl.when(k == nk - 1)
            def _():
                epilogue(in_refs, out_refs, [a[...] for a in acc_refs])

    scratch = [] if nk == 1 else [pltpu.VMEM(s, F32) for s in acc_shapes]
    return pl.pallas_call(
        body, grid=grid, in_specs=in_specs, out_specs=out_specs, out_shape=out_shapes,
        scratch_shapes=scratch, compiler_params=_params(("parallel", "parallel", "arbitrary")),
        name=name)(*ins)


def _mm_dual_fwd(name, a, w, res, kind):
    T, K = a.shape
    N = w.shape[1] // 2
    tm, tn = _tile(T, MM_TILE), _tile(N, MM_TILE_NARROW)
    nj = N // tn
    grid = (T // tm, nj, 1)
    ins = [a, w, w]
    in_specs = [pl.BlockSpec((tm, K), lambda i, j, k: (i, 0)),
                pl.BlockSpec((K, tn), lambda i, j, k: (0, j)),
                pl.BlockSpec((K, tn), lambda i, j, k: (0, j + nj))]
    pair_spec = pl.BlockSpec((2, tm, tn), lambda i, j, k: (0, i, j))
    tile_spec = pl.BlockSpec((tm, tn), lambda i, j, k: (i, j))
    if kind == "glu":
        ins.append(res)
        in_specs.append(tile_spec)

        def epilogue(in_refs, out_refs, accs):
            val, gate = accs
            s = _sigmoid(gate)
            out_refs[0][...] = in_refs[3][...] + val * s
            out_refs[1][0] = s.astype(BF16)
            out_refs[1][1] = (val * s * (1.0 - s)).astype(BF16)

        out_shapes = [jax.ShapeDtypeStruct((T, N), F32), jax.ShapeDtypeStruct((2, T, N), BF16)]
        out_specs = [tile_spec, pair_spec]
    else:
        def epilogue(in_refs, out_refs, accs):
            g, u = accs
            s = _sigmoid(g)
            silu = g * s
            out_refs[0][0] = (u * (s * (1.0 + g * (1.0 - s)))).astype(BF16)
            out_refs[0][1] = silu.astype(BF16)
            out_refs[1][...] = (silu * u).astype(BF16)

        out_shapes = [jax.ShapeDtypeStruct((2, T, N), BF16), jax.ShapeDtypeStruct((T, N), BF16)]
        out_specs = [pair_spec, tile_spec]
    return _matmul(name, grid, ins, in_specs, [(0, 1, 0), (0, 2, 1)], NN, out_shapes, out_specs,
                   [(tm, tn), (tm, tn)], epilogue)


def _mm_kv(name, a, w):
    T, K = a.shape
    N = w.shape[1] // 2
    tm, tn = _tile(T, MM_TILE), _tile(N, MM_TILE_NARROW)
    nj = N // tn
    tile_spec = pl.BlockSpec((tm, tn), lambda i, j, k: (i, j))

    def epilogue(in_refs, out_refs, accs):
        out_refs[0][...] = accs[0]
        out_refs[1][...] = accs[1]

    return _matmul(name, (T // tm, nj, 1), [a, w, w],
                   [pl.BlockSpec((tm, K), lambda i, j, k: (i, 0)),
                    pl.BlockSpec((K, tn), lambda i, j, k: (0, j)),
                    pl.BlockSpec((K, tn), lambda i, j, k: (0, j + nj))],
                   [(0, 1, 0), (0, 2, 1)], NN,
                   [jax.ShapeDtypeStruct((T, N), F32)] * 2, [tile_spec, tile_spec],
                   [(tm, tn), (tm, tn)], epilogue)


def _mm_nn(name, a, w, res=None, out_dtype=F32):
    T, K = a.shape
    N = w.shape[1]
    tk = K if K <= 2 * MM_DEPTH else _tile(K, MM_DEPTH)
    tm, tn = _tile(T, MM_TILE), _tile(N, MM_TILE if K <= MM_DEPTH else MM_TILE_NARROW)
    grid = (T // tm, N // tn, K // tk)
    tile_spec = pl.BlockSpec((tm, tn), lambda i, j, k: (i, j))
    ins = [a, w]
    in_specs = [pl.BlockSpec((tm, tk), lambda i, j, k: (i, k)),
                pl.BlockSpec((tk, tn), lambda i, j, k: (k, j))]
    if res is not None:
        ins.append(res)
        in_specs.append(tile_spec)

    def epilogue(in_refs, out_refs, accs):
        v = accs[0]
        if res is not None:
            v = v + in_refs[2][...]
        out_refs[0][...] = v.astype(out_dtype)

    return _matmul(name, grid, ins, in_specs, [(0, 1, 0)], NN,
                   [jax.ShapeDtypeStruct((T, N), out_dtype)], [tile_spec], [(tm, tn)], epilogue)[0]


def _dep_operand(ins, in_specs, dep):
    if dep is not None:
        ins.append(dep)
        in_specs.append(pl.BlockSpec((8, 128), lambda *_: (0, 0)))


def _mm_nt(name, a_list, w, out_dtype=F32, dep=None):
    T, Np = a_list[0].shape
    Ko = w.shape[0]
    n_parts = len(a_list)
    wide_a = a_list[0].dtype != BF16
    tm, tn, tk = _tile(T, MM_TILE_NARROW if wide_a else MM_TILE), _tile(Ko, MM_TILE), _tile(Np, MM_DEPTH)
    nkp = Np // tk
    grid = (T // tm, Ko // tn, nkp)
    ins = list(a_list) + [w] * n_parts
    in_specs = [pl.BlockSpec((tm, tk), lambda i, j, k: (i, k)) for _ in a_list]
    in_specs += [pl.BlockSpec((tn, tk), functools.partial(lambda i, j, k, p: (j, p * nkp + k), p=p))
                 for p in range(n_parts)]
    products = [(p, n_parts + p, 0) for p in range(n_parts)]
    _dep_operand(ins, in_specs, dep)

    def epilogue(in_refs, out_refs, accs):
        out_refs[0][...] = accs[0].astype(out_dtype)

    return _matmul(name, grid, ins, in_specs, products, NT,
                   [jax.ShapeDtypeStruct((T, Ko), out_dtype)],
                   [pl.BlockSpec((tm, tn), lambda i, j, k: (i, j))], [(tm, tn)], epilogue)[0]


def _mm_nt_pair(name, a3, w, out_dtype=F32):
    _, T, N = a3.shape
    Ko = w.shape[0]
    tm, tn, tk = _tile(T, MM_TILE), _tile(Ko, MM_TILE), _tile(N, MM_DEPTH)
    nkh = N // tk
    grid = (T // tm, Ko // tn, 2 * nkh)

    def epilogue(in_refs, out_refs, accs):
        out_refs[0][...] = accs[0].astype(out_dtype)

    return _matmul(name, grid, [a3, w],
                   [pl.BlockSpec((None, tm, tk), lambda i, j, k: (k // nkh, i, k % nkh)),
                    pl.BlockSpec((tn, tk), lambda i, j, k: (j, k))],
                   [(0, 1, 0)], NT, [jax.ShapeDtypeStruct((T, Ko), out_dtype)],
                   [pl.BlockSpec((tm, tn), lambda i, j, k: (i, j))], [(tm, tn)], epilogue)[0]


def _mm_nt_ffn_bwd(name, dx, w_out, gu, dep=None):
    T, D = dx.shape
    Fh = w_out.shape[0]
    tm, tn = _tile(T, MM_TILE), _tile(Fh, MM_TILE_NARROW)
    sub = _tile(tn, 256)
    n_dep = 0 if dep is None else 1

    def body(dx_ref, w_ref, gu_ref, *rest):
        out_ref = rest[n_dep]
        a = dx_ref[...]
        for c0 in range(0, tn, sub):
            da = _dot(a, w_ref[c0:c0 + sub, :], NT)
            out_ref[0, :, c0:c0 + sub] = (da * gu_ref[0, :, c0:c0 + sub].astype(F32)).astype(BF16)
            out_ref[1, :, c0:c0 + sub] = (da * gu_ref[1, :, c0:c0 + sub].astype(F32)).astype(BF16)

    pair_spec = pl.BlockSpec((2, tm, tn), lambda i, j: (0, i, j))
    ins = [dx, w_out, gu]
    in_specs = [pl.BlockSpec((tm, D), lambda i, j: (i, 0)),
                pl.BlockSpec((tn, D), lambda i, j: (j, 0)),
                pair_spec]
    _dep_operand(ins, in_specs, dep)
    return pl.pallas_call(
        body, grid=(T // tm, Fh // tn), in_specs=in_specs, out_specs=pair_spec,
        out_shape=jax.ShapeDtypeStruct((2, T, Fh), BF16),
        compiler_params=_params(("parallel", "parallel")), name=name)(*ins)


def _mm_tn(name, a, d):
    T, Ko = a.shape
    N = d.shape[1]
    to, tn, tk = _tile(Ko, MM_TILE_NARROW), _tile(N, MM_TILE if d.dtype == BF16 else MM_TILE_NARROW), T
    grid = (Ko // to, N // tn, T // tk)

    def epilogue(in_refs, out_refs, accs):
        out_refs[0][...] = accs[0].astype(BF16)

    return _matmul(name, grid, [a, d],
                   [pl.BlockSpec((tk, to), lambda i, j, k: (k, i)),
                    pl.BlockSpec((tk, tn), lambda i, j, k: (k, j))],
                   [(0, 1, 0)], TN, [jax.ShapeDtypeStruct((Ko, N), BF16)],
                   [pl.BlockSpec((to, tn), lambda i, j, k: (i, j))], [(to, tn)], epilogue)[0]


def _mm_tn_pair(name, a, d3):
    T, Ko = a.shape
    N = d3.shape[2]
    to, tn, tk = _tile(Ko, MM_TILE), _tile(N, MM_TILE_NARROW), T
    njh = N // tn
    grid = (Ko // to, 2 * njh, T // tk)

    def epilogue(in_refs, out_refs, accs):
        out_refs[0][...] = accs[0].astype(BF16)

    return _matmul(name, grid, [a, d3],
                   [pl.BlockSpec((tk, to), lambda i, j, k: (k, i)),
                    pl.BlockSpec((None, tk, tn), lambda i, j, k: (j // njh, k, j % njh))],
                   [(0, 1, 0)], TN, [jax.ShapeDtypeStruct((Ko, 2 * N), BF16)],
                   [pl.BlockSpec((to, tn), lambda i, j, k: (i, j))], [(to, tn)], epilogue)[0]


def _rms_fwd(name, x, gains, dtypes, dep=None):
    T, D = x.shape
    n = len(gains)
    tr = _tile(T, 512, 8)
    n_dep = 0 if dep is None else 1

    def body(x_ref, *refs):
        xv = x_ref[...]
        xr = xv * lax.rsqrt(jnp.mean(xv * xv, axis=-1, keepdims=True) + EPS)
        for g_ref, o_ref in zip(refs[:n], refs[n + n_dep:]):
            o_ref[...] = (xr * g_ref[...]).astype(o_ref.dtype)

    row = pl.BlockSpec((tr, D), lambda i: (i, 0))
    vec = pl.BlockSpec((1, D), lambda i: (0, 0))
    ins, in_specs = [x, *gains], [row] + [vec] * n
    _dep_operand(ins, in_specs, dep)
    return pl.pallas_call(
        body, grid=(T // tr,), in_specs=in_specs, out_specs=[row] * n,
        out_shape=[jax.ShapeDtypeStruct((T, D), dt) for dt in dtypes],
        compiler_params=_params(("parallel",)), name=name)(*ins)


def _rms_bwd(name, x, dres, gains, dhs, dep=None):
    T, D = x.shape
    n = len(gains)
    tr = _tile(T, 256, 8)
    n_dep = 0 if dep is None else 1

    def body(x_ref, dres_ref, *refs):
        g_refs, dh_refs = refs[:n], refs[n:2 * n]
        dx_ref, dxb_ref = refs[2 * n + n_dep], refs[2 * n + n_dep + 1]
        dg_refs = refs[2 * n + n_dep + 2:]
        xv = x_ref[...]
        r = lax.rsqrt(jnp.mean(xv * xv, axis=-1, keepdims=True) + EPS)
        xr = xv * r
        w = None
        for g_ref, dh_ref, dg_ref in zip(g_refs, dh_refs, dg_refs):
            dh = dh_ref[...].astype(F32)

            @pl.when(pl.program_id(0) == 0)
            def _():
                dg_ref[...] = jnp.zeros_like(dg_ref)

            dg_ref[...] += jnp.sum(dh * xr, axis=0, keepdims=True)
            wi = dh * g_ref[...]
            w = wi if w is None else w + wi
        dx = dres_ref[...] + r * (w - xr * jnp.mean(w * xr, axis=-1, keepdims=True))
        dx_ref[...] = dx
        dxb_ref[...] = dx.astype(BF16)

    row = pl.BlockSpec((tr, D), lambda i: (i, 0))
    vec = pl.BlockSpec((1, D), lambda i: (0, 0))
    ins, in_specs = [x, dres, *gains, *dhs], [row, row] + [vec] * n + [row] * n
    _dep_operand(ins, in_specs, dep)
    outs = pl.pallas_call(
        body, grid=(T // tr,), in_specs=in_specs,
        out_specs=[row, row] + [vec] * n,
        out_shape=[jax.ShapeDtypeStruct((T, D), F32), jax.ShapeDtypeStruct((T, D), BF16)]
        + [jax.ShapeDtypeStruct((1, D), F32)] * n,
        compiler_params=_params(("arbitrary",)), name=name)(*ins)
    return outs[0], outs[1], outs[2:]


def _loss_head(x, gain, target):
    T, D = x.shape
    tr = _tile(T, 256, 8)

    def body(x_ref, g_ref, t_ref, loss_ref, dx_ref, dxb_ref, dg_ref):
        @pl.when(pl.program_id(0) == 0)
        def _():
            loss_ref[...] = jnp.zeros_like(loss_ref)
            dg_ref[...] = jnp.zeros_like(dg_ref)

        xv = x_ref[...]
        r = lax.rsqrt(jnp.mean(xv * xv, axis=-1, keepdims=True) + EPS)
        xr = xv * r
        err = xr * g_ref[...] - t_ref[...]
        part = jnp.sum(jnp.sum(err * err, axis=-1, keepdims=True), axis=0, keepdims=True) * (0.5 / D)
        loss_ref[...] += jnp.broadcast_to(part, loss_ref.shape)
        dy = err * (1.0 / D)
        dg_ref[...] += jnp.sum(dy * xr, axis=0, keepdims=True)
        w = dy * g_ref[...]
        dx = r * (w - xr * jnp.mean(w * xr, axis=-1, keepdims=True))
        dx_ref[...] = dx
        dxb_ref[...] = dx.astype(BF16)

    row = pl.BlockSpec((tr, D), lambda i: (i, 0))
    vec = pl.BlockSpec((1, D), lambda i: (0, 0))
    return pl.pallas_call(
        body, grid=(T // tr,), in_specs=[row, vec, row],
        out_specs=[pl.BlockSpec((8, 128), lambda i: (0, 0)), row, row, vec],
        out_shape=[jax.ShapeDtypeStruct((8, 128), F32), jax.ShapeDtypeStruct((T, D), F32),
                   jax.ShapeDtypeStruct((T, D), BF16), jax.ShapeDtypeStruct((1, D), F32)],
        compiler_params=_params(("arbitrary",)), name="loss_head")(x, gain, target)


def _glu_bwd(dmix, vg, dep=None):
    T, N = dmix.shape
    tr, tc = _tile(T, 512, 8), _tile(N, 1024)
    n_dep = 0 if dep is None else 1

    def body(d_ref, vg_ref, *refs):
        o_ref = refs[n_dep]
        d = d_ref[...]
        o_ref[0] = (d * vg_ref[0].astype(F32)).astype(BF16)
        o_ref[1] = (d * vg_ref[1].astype(F32)).astype(BF16)

    pair = pl.BlockSpec((2, tr, tc), lambda i, j: (0, i, j))
    ins, in_specs = [dmix, vg], [pl.BlockSpec((tr, tc), lambda i, j: (i, j)), pair]
    _dep_operand(ins, in_specs, dep)
    return pl.pallas_call(
        body, grid=(T // tr, N // tc), in_specs=in_specs,
        out_specs=pair, out_shape=jax.ShapeDtypeStruct((2, T, N), BF16),
        compiler_params=_params(("parallel", "parallel")), name="glu_bwd")(*ins)


def _to_state_tiles(x_ref, s, val):
    tc = val.shape[0]
    for j in range(S5_W // 128):
        x_ref[s, pl.ds(j, tc, stride=8), :] = val[:, 128 * j:128 * (j + 1)]


def _from_state_tiles(x_ref, s, tc):
    return jnp.concatenate([x_ref[s, pl.ds(j, tc, stride=8), :] for j in range(S5_W // 128)], axis=1)


def _s5_scan_fwd(xr_ref, xi_ref, ar_ref, ai_ref, cr_ref, ci_ref, tc, nblk):
    a = [(ar_ref[s], ai_ref[s]) for s in range(nblk)]

    def step(i, carry):
        carry = list(carry)
        for uu in range(S5_UNROLL):
            r0 = pl.multiple_of((i * S5_UNROLL + uu) * 8, 8)
            for s in range(nblk):
                cr, ci = carry[2 * s], carry[2 * s + 1]
                a_r, a_i = a[s]
                xr = a_r * cr - a_i * ci + xr_ref[s, pl.ds(r0, 8), :]
                xi = a_r * ci + a_i * cr + xi_ref[s, pl.ds(r0, 8), :]
                xr_ref[s, pl.ds(r0, 8), :] = xr
                xi_ref[s, pl.ds(r0, 8), :] = xi
                carry[2 * s], carry[2 * s + 1] = xr, xi
        return tuple(carry)

    init = []
    for s in range(nblk):
        init += [cr_ref[s], ci_ref[s]]
    out = lax.fori_loop(0, tc // S5_UNROLL, step, tuple(init))
    for s in range(nblk):
        cr_ref[s] = out[2 * s]
        ci_ref[s] = out[2 * s + 1]


def _s5_scan_bwd(lr_ref, li_ref, xr_ref, xi_ref, h_ref, ar_ref, ai_ref, cr_ref, ci_ref,
                 accr_ref, acci_ref, tc, nblk):
    a = [(ar_ref[s], ai_ref[s]) for s in range(nblk)]

    def one(s, r0, prev_r, prev_i, st):
        c_r, c_i, d_r, d_i = st
        a_r, a_i = a[s]
        l_r = lr_ref[s, pl.ds(r0, 8), :] + a_r * c_r + a_i * c_i
        l_i = li_ref[s, pl.ds(r0, 8), :] + a_r * c_i - a_i * c_r
        lr_ref[s, pl.ds(r0, 8), :] = l_r
        li_ref[s, pl.ds(r0, 8), :] = l_i
        return [l_r, l_i, d_r + l_r * prev_r + l_i * prev_i, d_i - l_r * prev_i + l_i * prev_r]

    def step(i, carry):
        carry = list(carry)
        for uu in range(S5_UNROLL):
            t = tc - 1 - (i * S5_UNROLL + uu)
            r0 = pl.multiple_of(t * 8, 8)
            p0 = pl.multiple_of((t - 1) * 8, 8)
            for s in range(nblk):
                carry[4 * s:4 * s + 4] = one(s, r0, xr_ref[s, pl.ds(p0, 8), :], xi_ref[s, pl.ds(p0, 8), :],
                                             carry[4 * s:4 * s + 4])
        return tuple(carry)

    init = []
    for s in range(nblk):
        init += [cr_ref[s], ci_ref[s], accr_ref[s], acci_ref[s]]
    carry = list(lax.fori_loop(0, tc // S5_UNROLL - 1, step, tuple(init)))
    for t in range(S5_UNROLL - 1, -1, -1):
        for s in range(nblk):
            if t > 0:
                prev_r, prev_i = xr_ref[s, 8 * (t - 1):8 * t, :], xi_ref[s, 8 * (t - 1):8 * t, :]
            else:
                prev_r, prev_i = h_ref[0, s], h_ref[1, s]
            carry[4 * s:4 * s + 4] = one(s, 8 * t, prev_r, prev_i, carry[4 * s:4 * s + 4])
    for s in range(nblk):
        cr_ref[s], ci_ref[s], accr_ref[s], acci_ref[s] = carry[4 * s:4 * s + 4]


def _gelu(y):
    k = math.sqrt(2.0 / math.pi)
    return 0.5 * y * (1.0 + jnp.tanh(k * (y + 0.044715 * (y * y * y))))


def _gelu_grad(y):
    k = math.sqrt(2.0 / math.pi)
    t = jnp.tanh(k * (y + 0.044715 * (y * y * y)))
    return 0.5 * (1.0 + t) + 0.5 * y * (1.0 - t * t) * (k * (1.0 + 3.0 * 0.044715 * (y * y)))


def _s5_specs(tc, nch, sbk, rev):
    def ch(c):
        return nch - 1 - c if rev else c

    return dict(
        act=pl.BlockSpec((tc, sbk * S5_CH), lambda i, c: (ch(c), i)),
        bb=pl.BlockSpec((2, sbk, S5_CH, 128), lambda i, c: (0, i, 0, 0)),
        cc=pl.BlockSpec((2, sbk, S5_P, S5_CH), lambda i, c: (0, i, 0, 0)),
        a=pl.BlockSpec((sbk, 8, 128), lambda i, c: (i, 0, 0)),
        d=pl.BlockSpec((1, sbk * S5_CH), lambda i, c: (0, i)),
        h=pl.BlockSpec((None, 2, sbk, 8, 128), lambda i, c: (ch(c), 0, i, 0, 0)),
    )


def _s5_blocks(nb, pref):
    return max(b for b in range(1, pref + 1) if nb % b == 0)


def _s5_group_masks():
    rb = lax.broadcasted_iota(jnp.int32, (S5_CH, S5_W), 0) // S5_C
    qb = lax.broadcasted_iota(jnp.int32, (S5_CH, S5_W), 1) // S5_P
    qc = lax.broadcasted_iota(jnp.int32, (S5_W, S5_CH), 0) // S5_P
    rc = lax.broadcasted_iota(jnp.int32, (S5_W, S5_CH), 1) // S5_C
    return rb == qb, qc == rc


def _s5_expand(bb_ref, cc_ref, bbd, ccd, sbk):
    mask_b, mask_c = _s5_group_masks()
    for k in range(2):
        for s in range(sbk):
            bbd[k, s] = jnp.where(mask_b, jnp.tile(bb_ref[k, s], (1, S5_W // 128)), 0).astype(BF16)
            ccd[k, s] = jnp.where(mask_c, jnp.tile(cc_ref[k, s], (S5_GB, 1)), 0).astype(BF16)


def _s5_fwd(u, bb2, cc2, a_re, a_im, d_skip):
    T, D = u.shape
    nb = D // S5_CH
    sbk = _s5_blocks(nb, 4)
    tc = _tile(T, 512, 8)
    nch = T // tc
    sp = _s5_specs(tc, nch, sbk, False)

    def body(u_ref, bb_ref, cc_ref, ar_ref, ai_ref, d_ref, z_ref, h_ref, xr, xi, cr, ci, bbd, ccd):
        @pl.when(pl.program_id(1) == 0)
        def _():
            cr[...] = jnp.zeros_like(cr)
            ci[...] = jnp.zeros_like(ci)
            _s5_expand(bb_ref, cc_ref, bbd, ccd, sbk)

        h_ref[0] = cr[...]
        h_ref[1] = ci[...]
        for s in range(sbk):
            ub = u_ref[:, s * S5_CH:(s + 1) * S5_CH].astype(BF16)
            _to_state_tiles(xr, s, _dot(ub, bbd[0, s]))
            _to_state_tiles(xi, s, _dot(ub, bbd[1, s]))
        _s5_scan_fwd(xr, xi, ar_ref, ai_ref, cr, ci, tc, sbk)
        for s in range(sbk):
            cols = slice(s * S5_CH, (s + 1) * S5_CH)
            y = (_dot(_from_state_tiles(xr, s, tc).astype(BF16), ccd[0, s])
                 - _dot(_from_state_tiles(xi, s, tc).astype(BF16), ccd[1, s])
                 + d_ref[:, cols] * u_ref[:, cols])
            z_ref[:, cols] = _gelu(y).astype(BF16)

    tiles = pltpu.VMEM((sbk, tc * 8, 128), F32)
    carry = pltpu.VMEM((sbk, 8, 128), F32)
    return pl.pallas_call(
        body, grid=(nb // sbk, nch),
        in_specs=[sp["act"], sp["bb"], sp["cc"], sp["a"], sp["a"], sp["d"]],
        out_specs=[sp["act"], sp["h"]],
        out_shape=[jax.ShapeDtypeStruct((T, D), BF16), jax.ShapeDtypeStruct((nch, 2, nb, 8, 128), F32)],
        scratch_shapes=[tiles, tiles, carry, carry, pltpu.VMEM((2, sbk, S5_CH, S5_W), BF16),
                        pltpu.VMEM((2, sbk, S5_W, S5_CH), BF16)],
        compiler_params=_params(("parallel", "arbitrary")), name="s5_fwd",
    )(u, bb2, cc2, a_re, a_im, d_skip)


def _s5_bwd(u, dz, h0, bb2, cc2, a_re, a_im, d_skip, dep=None):
    T, D = u.shape
    nb = D // S5_CH
    sbk = _s5_blocks(nb, 2)
    tc = _tile(T, 512, 8)
    nch = T // tc
    sp = _s5_specs(tc, nch, sbk, True)

    n_dep = 0 if dep is None else 1

    def body(u_ref, dz_ref, h_ref, bb_ref, cc_ref, ar_ref, ai_ref, d_ref, *rest):
        (du_ref, dd_ref, dar_ref, dai_ref, dbb_ref, dcc_ref,
         xr, xi, lr, li, fr, fi, br, bi, accr, acci, bbd, ccd, dbbd, dccd) = rest[n_dep:]
        c = pl.program_id(1)

        @pl.when(c == 0)
        def _():
            for ref in (br, bi, accr, acci, dd_ref, dbbd, dccd):
                ref[...] = jnp.zeros_like(ref)
            _s5_expand(bb_ref, cc_ref, bbd, ccd, sbk)

        for s in range(sbk):
            ub = u_ref[:, s * S5_CH:(s + 1) * S5_CH].astype(BF16)
            _to_state_tiles(xr, s, _dot(ub, bbd[0, s]))
            _to_state_tiles(xi, s, _dot(ub, bbd[1, s]))
        fr[...] = h_ref[0]
        fi[...] = h_ref[1]
        _s5_scan_fwd(xr, xi, ar_ref, ai_ref, fr, fi, tc, sbk)
        for s in range(sbk):
            cols = slice(s * S5_CH, (s + 1) * S5_CH)
            uv = u_ref[:, cols]
            xrb = _from_state_tiles(xr, s, tc).astype(BF16)
            xib = _from_state_tiles(xi, s, tc).astype(BF16)
            dsk = d_ref[:, cols]
            y = _dot(xrb, ccd[0, s]) - _dot(xib, ccd[1, s]) + dsk * uv
            dy = dz_ref[:, cols] * _gelu_grad(y)
            dd_ref[:, cols] += jnp.sum(dy * uv, axis=0, keepdims=True)
            dyb = dy.astype(BF16)
            dccd[0, s] += _dot(xrb, dyb, TN)
            dccd[1, s] += _dot(xib, dyb, TN)
            _to_state_tiles(lr, s, _dot(dyb, ccd[0, s], NT))
            _to_state_tiles(li, s, -_dot(dyb, ccd[1, s], NT))
            du_ref[:, cols] = dy * dsk
        _s5_scan_bwd(lr, li, xr, xi, h_ref, ar_ref, ai_ref, br, bi, accr, acci, tc, sbk)
        for s in range(sbk):
            cols = slice(s * S5_CH, (s + 1) * S5_CH)
            ub = u_ref[:, cols].astype(BF16)
            lrb = _from_state_tiles(lr, s, tc).astype(BF16)
            lib = _from_state_tiles(li, s, tc).astype(BF16)
            dbbd[0, s] += _dot(ub, lrb, TN)
            dbbd[1, s] += _dot(ub, lib, TN)
            du_ref[:, cols] += _dot(lrb, bbd[0, s], NT) + _dot(lib, bbd[1, s], NT)

        @pl.when(c == nch - 1)
        def _():
            dar_ref[...] = accr[...]
            dai_ref[...] = acci[...]
            mask_b, mask_c = _s5_group_masks()
            for k in range(2):
                for s in range(sbk):
                    mb = jnp.where(mask_b, dbbd[k, s], 0.0)
                    fold = functools.reduce(
                        lambda a, b: a + b, [mb[:, 128 * j:128 * (j + 1)] for j in range(S5_W // 128)])
                    dbb_ref[k, s] = fold + pltpu.roll(fold, S5_P, 1)
                    mc = jnp.where(mask_c, dccd[k, s], 0.0)
                    dcc_ref[k, s] = functools.reduce(
                        lambda a, b: a + b, [mc[S5_P * j:S5_P * (j + 1), :] for j in range(S5_GB)])

    tiles = pltpu.VMEM((sbk, tc * 8, 128), F32)
    carry = pltpu.VMEM((sbk, 8, 128), F32)
    ins = [u, dz, h0, bb2, cc2, a_re, a_im, d_skip]
    in_specs = [sp["act"], sp["act"], sp["h"], sp["bb"], sp["cc"], sp["a"], sp["a"], sp["d"]]
    _dep_operand(ins, in_specs, dep)
    return pl.pallas_call(
        body, grid=(nb // sbk, nch),
        in_specs=in_specs,
        out_specs=[sp["act"], sp["d"], sp["a"], sp["a"], sp["bb"], sp["cc"]],
        out_shape=[jax.ShapeDtypeStruct((T, D), F32), jax.ShapeDtypeStruct((1, D), F32),
                   jax.ShapeDtypeStruct((nb, 8, 128), F32), jax.ShapeDtypeStruct((nb, 8, 128), F32),
                   jax.ShapeDtypeStruct((2, nb, S5_CH, 128), F32), jax.ShapeDtypeStruct((2, nb, S5_P, S5_CH), F32)],
        scratch_shapes=[tiles, tiles, tiles, tiles, carry, carry, carry, carry, carry, carry,
                        pltpu.VMEM((2, sbk, S5_CH, S5_W), BF16), pltpu.VMEM((2, sbk, S5_W, S5_CH), BF16),
                        pltpu.VMEM((2, sbk, S5_CH, S5_W), F32), pltpu.VMEM((2, sbk, S5_W, S5_CH), F32)],
        compiler_params=_params(("parallel", "arbitrary")), name="s5_bwd",
    )(*ins)


def _s5_disc(lr, li, ldt):
    dt = jnp.exp(ldt)
    mag = jnp.exp(lr * dt)
    ang = li * dt
    cs, sn = jnp.cos(ang), jnp.sin(ang)
    lbr, lbi = mag * cs, mag * sn
    nr = lbr - 1.0
    den = lr * lr + li * li
    f_re = (nr * lr + lbi * li) / den
    f_im = (lbi * lr - nr * li) / den
    return dt, mag, cs, sn, lbr, lbi, nr, den, f_re, f_im


def _s5_param_fwd(lr, li, ldt, bt_re, bt_im):
    c, g, p = bt_re.shape

    def body(lr_ref, li_ref, ldt_ref, br_ref, bi_ref, lbr_ref, lbi_ref, bbr_ref, bbi_ref):
        _, _, _, _, lbr, lbi, _, _, f_re, f_im = _s5_disc(lr_ref[...], li_ref[...], ldt_ref[...])
        lbr_ref[...] = lbr
        lbi_ref[...] = lbi
        for ch in range(c):
            b_r, b_i = br_ref[ch], bi_ref[ch]
            bbr_ref[ch] = f_re * b_r - f_im * b_i
            bbi_ref[ch] = f_re * b_i + f_im * b_r

    gp = jax.ShapeDtypeStruct((g, p), F32)
    cgp = jax.ShapeDtypeStruct((c, g, p), F32)
    return pl.pallas_call(body, out_shape=[gp, gp, cgp, cgp], name="s5_param_fwd")(lr, li, ldt, bt_re, bt_im)


def _s5_param_bwd(lr, li, ldt, bt_re, bt_im, dlbr, dlbi, dbbr, dbbi):
    c, g, p = bt_re.shape

    def body(lr_ref, li_ref, ldt_ref, br_ref, bi_ref, dlbr_ref, dlbi_ref, dbbr_ref, dbbi_ref,
             dlr_ref, dli_ref, dldt_ref, dbr_ref, dbi_ref):
        l_r, l_i = lr_ref[...], li_ref[...]
        dt, mag, cs, sn, lbr, lbi, nr, den, f_re, f_im = _s5_disc(l_r, l_i, ldt_ref[...])
        dfr = jnp.zeros_like(l_r)
        dfi = jnp.zeros_like(l_r)
        for ch in range(c):
            b_r, b_i = br_ref[ch], bi_ref[ch]
            g_r, g_i = dbbr_ref[ch], dbbi_ref[ch]
            dbr_ref[ch] = f_re * g_r + f_im * g_i
            dbi_ref[ch] = f_re * g_i - f_im * g_r
            dfr = dfr + g_r * b_r + g_i * b_i
            dfi = dfi + g_i * b_r - g_r * b_i
        inv = 1.0 / den
        d_nr = (dfr * l_r - dfi * l_i) * inv
        d_lbi = (dfr * l_i + dfi * l_r) * inv + dlbi_ref[...]
        d_lbr = d_nr + dlbr_ref[...]
        d_den = -(dfr * f_re + dfi * f_im) * inv
        d_mag = d_lbr * cs + d_lbi * sn
        d_ang = d_lbi * lbr - d_lbr * lbi
        dlr_ref[...] = (dfr * nr + dfi * lbi) * inv + 2.0 * d_den * l_r + d_mag * mag * dt
        dli_ref[...] = (dfr * lbi - dfi * nr) * inv + 2.0 * d_den * l_i + d_ang * dt
        dldt_ref[...] = jnp.sum(d_mag * mag * l_r + d_ang * l_i, axis=1, keepdims=True) * dt

    gp = jax.ShapeDtypeStruct((g, p), F32)
    cgp = jax.ShapeDtypeStruct((c, g, p), F32)
    return pl.pallas_call(body, out_shape=[gp, gp, jax.ShapeDtypeStruct((g, 1), F32), cgp, cgp],
                          name="s5_param_bwd")(lr, li, ldt, bt_re, bt_im, dlbr, dlbi, dbbr, dbbi)


def _att_masks(rep, gb):
    rows = rep * ATT_BLK
    qi = lax.broadcasted_iota(jnp.int32, (rows, 2 * ATT_BLK), 0) % ATT_BLK
    si = lax.broadcasted_iota(jnp.int32, (rows, 2 * ATT_BLK), 1)
    prev = (si < ATT_BLK) & (si >= qi) & (gb > 0)
    cur = (si >= ATT_BLK) & (si - ATT_BLK <= qi)
    return prev | cur


def _att_rows(start, dil):
    return pl.ds(start, ATT_BLK) if dil == 1 else pl.ds(start, ATT_BLK, stride=dil)


def _att_plan(T, dil):
    span = ATT_BLK * dil
    sbr = max(span, min(T, 1024))
    return span, sbr, T // sbr


def _att_block(sb, i, sbr, span, dil):
    loc = (i // dil) * span + i % dil
    cur = sb * sbr + loc
    gb = sb * (sbr // span) + i // dil
    return loc, cur, jnp.where(gb > 0, cur - span, cur), gb


def _att_fwd(q, k, v, grp, dil):
    T = q.shape[0]
    H = q.shape[1] // HEAD_DIM // len(DILATIONS)
    rep = H // N_KV_HEADS
    span, sbr, nsb = _att_plan(T, dil)
    scale = HEAD_DIM ** -0.5

    def body(*refs):
        q_refs = refs[:rep]
        k_ref, v_ref, o_ref, l_ref, o_slab, l_slab = refs[rep:]
        sb = pl.program_id(1)

        def blk(i, _):
            loc, cur, prv, gb = _att_block(sb, i, sbr, span, dil)
            rows = _att_rows(loc, dil)
            qs = jnp.concatenate([r[rows, :] for r in q_refs], axis=0).astype(BF16)
            kcat = jnp.concatenate([k_ref[_att_rows(prv, dil), :], k_ref[_att_rows(cur, dil), :]], axis=0)
            vcat = jnp.concatenate([v_ref[_att_rows(prv, dil), :], v_ref[_att_rows(cur, dil), :]], axis=0)
            s = jnp.where(_att_masks(rep, gb), _dot(qs, kcat.astype(BF16), NT) * scale, NEG_INF)
            m = jnp.max(s, axis=-1, keepdims=True)
            p = jnp.exp(s - m)
            l = jnp.sum(p, axis=-1, keepdims=True)
            o = _dot(p.astype(BF16), vcat.astype(BF16)) / l
            lse = jnp.broadcast_to(m + jnp.log(l), (rep * ATT_BLK, HEAD_DIM))
            for j in range(rep):
                o_slab[j, rows, :] = o[j * ATT_BLK:(j + 1) * ATT_BLK]
                l_slab[j, rows, :] = lse[j * ATT_BLK:(j + 1) * ATT_BLK]
            return 0

        lax.fori_loop(0, sbr // ATT_BLK, blk, 0)
        for j in range(rep):
            o_ref[:, j * HEAD_DIM:(j + 1) * HEAD_DIM] = o_slab[j]
            l_ref[:, j * HEAD_DIM:(j + 1) * HEAD_DIM] = l_slab[j]

    qspecs = [pl.BlockSpec((sbr, HEAD_DIM), functools.partial(lambda h, s, j: (s, grp * H + h * rep + j), j=j))
              for j in range(rep)]
    kspec = pl.BlockSpec((T, HEAD_DIM), lambda h, s: (0, h))
    ospec = pl.BlockSpec((sbr, rep * HEAD_DIM), lambda h, s: (s, h))
    slab = pltpu.VMEM((rep, sbr, HEAD_DIM), F32)
    return pl.pallas_call(
        body, grid=(N_KV_HEADS, nsb), in_specs=qspecs + [kspec, kspec], out_specs=[ospec, ospec],
        out_shape=[jax.ShapeDtypeStruct((T, H * HEAD_DIM), F32)] * 2, scratch_shapes=[slab, slab],
        compiler_params=_params(("parallel", "arbitrary")), name=f"att_fwd_d{dil}",
    )(*([q] * rep), k, v)


def _att_combine(outs, lses):
    T, W = outs[0].shape
    ng = len(outs)
    tr, tcol = _tile(T, 512, 8), _tile(W, 512)

    def body(*refs):
        o_refs, l_refs = refs[:ng], refs[ng:2 * ng]
        ob_ref, lse_ref = refs[2 * ng:]
        ls = [r[...] for r in l_refs]
        m = functools.reduce(jnp.maximum, ls)
        es = [jnp.exp(l - m) for l in ls]
        den = functools.reduce(lambda a, b: a + b, es)
        num = functools.reduce(lambda a, b: a + b, [e * o[...] for e, o in zip(es, o_refs)])
        ob_ref[...] = (num / den).astype(BF16)
        lse_ref[...] = m + jnp.log(den)

    spec = pl.BlockSpec((tr, tcol), lambda i, j: (i, j))
    return pl.pallas_call(
        body, grid=(T // tr, W // tcol), in_specs=[spec] * (2 * ng), out_specs=[spec, spec],
        out_shape=[jax.ShapeDtypeStruct((T, W), BF16), jax.ShapeDtypeStruct((T, W), F32)],
        compiler_params=_params(("parallel", "parallel")), name="att_combine")(*outs, *lses)


STAT_LANE = HEAD_DIM // 2


def _att_stats(lse, o, do):
    T, W = lse.shape
    tr = _tile(T, 256, 16)

    def body(l_ref, o_ref, do_ref, s_ref):
        lane = lax.broadcasted_iota(jnp.int32, (tr, HEAD_DIM), 1)
        for h in range(W // HEAD_DIM):
            cols = slice(h * HEAD_DIM, (h + 1) * HEAD_DIM)
            delta = jnp.sum(do_ref[:, cols] * o_ref[:, cols].astype(F32), axis=-1, keepdims=True)
            s_ref[:, cols] = jnp.where(lane < STAT_LANE, l_ref[:, cols], delta)

    spec = pl.BlockSpec((tr, W), lambda i: (i, 0))
    return pl.pallas_call(
        body, grid=(T // tr,), in_specs=[spec] * 3, out_specs=spec,
        out_shape=jax.ShapeDtypeStruct((T, W), F32),
        compiler_params=_params(("parallel",)), name="att_stats")(lse, o, do)


def _att_bwd(q, k, v, do, stats, dq, grp, dil):
    T = q.shape[0]
    H = do.shape[1] // HEAD_DIM
    rep = H // N_KV_HEADS
    hs = rep
    span, sbr, nsb = _att_plan(T, dil)
    scale = HEAD_DIM ** -0.5

    def body(*refs):
        q_refs, do_refs, st_refs = refs[:hs], refs[hs + 2:2 * hs + 2], refs[2 * hs + 2:3 * hs + 2]
        k_ref, v_ref = refs[hs], refs[hs + 1]
        dq_ref, dk_ref, dv_ref, dq_slab = refs[3 * hs + 3:]
        sb = pl.program_id(2)

        @pl.when((pl.program_id(1) == 0) & (sb == 0))
        def _():
            dk_ref[...] = jnp.zeros_like(dk_ref)
            dv_ref[...] = jnp.zeros_like(dv_ref)

        def blk(i, _):
            loc, cur, prv, gb = _att_block(sb, i, sbr, span, dil)
            rows, kc, kp = _att_rows(loc, dil), _att_rows(cur, dil), _att_rows(prv, dil)
            qs = jnp.concatenate([r[rows, :] for r in q_refs], axis=0).astype(BF16)
            dos = jnp.concatenate([r[rows, :] for r in do_refs], axis=0).astype(BF16)
            st = jnp.concatenate([r[rows, :] for r in st_refs], axis=0)
            kcat = jnp.concatenate([k_ref[kp, :], k_ref[kc, :]], axis=0).astype(BF16)
            vcat = jnp.concatenate([v_ref[kp, :], v_ref[kc, :]], axis=0).astype(BF16)
            s = _dot(qs, kcat, NT) * scale
            p = jnp.where(_att_masks(hs, gb), jnp.exp(s - st[:, 0:1]), 0.0)
            dp = _dot(dos, vcat, NT)
            ds = (p * (dp - st[:, STAT_LANE:STAT_LANE + 1]) * scale).astype(BF16)
            dvc = _dot(p.astype(BF16), dos, TN)
            dkc = _dot(ds, qs, TN)
            dqs = _dot(ds, kcat)
            for j in range(hs):
                dq_slab[j, rows, :] = dqs[j * ATT_BLK:(j + 1) * ATT_BLK]
            dk_ref[kc, :] += dkc[ATT_BLK:]
            dv_ref[kc, :] += dvc[ATT_BLK:]

            @pl.when(gb > 0)
            def _():
                dk_ref[kp, :] += dkc[:ATT_BLK]
                dv_ref[kp, :] += dvc[:ATT_BLK]

            return 0

        lax.fori_loop(0, sbr // ATT_BLK, blk, 0)
        for j in range(hs):
            dq_ref[:, j * HEAD_DIM:(j + 1) * HEAD_DIM] = dq_slab[j].astype(BF16)

    def head_specs(col0):
        return [pl.BlockSpec((sbr, HEAD_DIM),
                             functools.partial(lambda h, f, s, j: (s, col0 + h * rep + f * hs + j), j=j))
                for j in range(hs)]

    kspec = pl.BlockSpec((T, HEAD_DIM), lambda h, f, s: (0, h))
    dqspec = pl.BlockSpec((sbr, hs * HEAD_DIM), lambda h, f, s: (s, (grp * H + h * rep) // hs + f))
    n_in = 3 * hs + 3
    return pl.pallas_call(
        body, grid=(N_KV_HEADS, rep // hs, nsb),
        in_specs=head_specs(grp * H) + [kspec, kspec] + head_specs(0) + head_specs(0) + [ANY],
        out_specs=[dqspec, kspec, kspec],
        out_shape=[jax.ShapeDtypeStruct(dq.shape, BF16),
                   jax.ShapeDtypeStruct((T, N_KV_HEADS * HEAD_DIM), F32),
                   jax.ShapeDtypeStruct((T, N_KV_HEADS * HEAD_DIM), F32)],
        scratch_shapes=[pltpu.VMEM((hs, sbr, HEAD_DIM), F32)],
        input_output_aliases={n_in - 1: 0},
        compiler_params=_params(("parallel", "arbitrary", "arbitrary")), name=f"att_bwd_d{dil}",
    )(*([q] * hs), k, v, *([do] * hs), *([stats] * hs), dq)


def _sum_kv(dks, dvs):
    T, W = dks[0].shape
    ng = len(dks)
    tr = _tile(T, 512, 8)

    def body(*refs):
        o_ref = refs[2 * ng]
        o_ref[0] = functools.reduce(lambda a, b: a + b, [r[...] for r in refs[:ng]]).astype(BF16)
        o_ref[1] = functools.reduce(lambda a, b: a + b, [r[...] for r in refs[ng:2 * ng]]).astype(BF16)

    spec = pl.BlockSpec((tr, W), lambda i: (i, 0))
    return pl.pallas_call(
        body, grid=(T // tr,), in_specs=[spec] * (2 * ng),
        out_specs=pl.BlockSpec((2, tr, W), lambda i: (0, i, 0)),
        out_shape=jax.ShapeDtypeStruct((2, T, W), BF16),
        compiler_params=_params(("parallel",)), name="sum_kv")(*dks, *dvs)


def _local_step(x, tgt, w, wts, ready, dep0=None, small_ready=None):
    T, D = x.shape
    g = {}

    (u0,) = _rms_fwd("rms_a", x, [w["a_norm"]], [F32], dep=dep0)
    lbr, lbi, bbt_re, bbt_im = _s5_param_fwd(w["lam_re"], w["lam_im"], w["log_dt"], w["bt_re"], w["bt_im"])
    a_re, a_im = lbr.reshape(-1, 8, 128), lbi.reshape(-1, 8, 128)
    C, G, P = w["bt_re"].shape
    nb = G // S5_GB
    bb2 = jnp.stack([bbt_re, bbt_im]).transpose(0, 2, 1, 3).reshape(2, nb, S5_GB * C, P)
    bb2 = jnp.concatenate([bb2, bb2], axis=-1)
    cc2 = jnp.stack([w["c_re"], w["c_im"]]).reshape(2, nb, S5_GB * C, P).transpose(0, 1, 3, 2)
    z, h0 = _s5_fwd(u0, bb2, cc2, a_re, a_im, w["s5_d"])
    w_glu = wts("glu", z)
    x1, vg = _mm_dual_fwd("glu_fwd", z, w_glu, x, "glu")

    def ffn_fwd(xin, layer):
        (nrm,) = _rms_fwd(f"rms_f{layer}", xin, [w["ffn_norm"][layer:layer + 1]], [BF16])
        w_in = wts(f"win{layer}", nrm)
        gu, act = _mm_dual_fwd(f"ffn_in{layer}", nrm, w_in, None, "ffn")
        w_out = wts(f"wout{layer}", act)
        xout = _mm_nn(f"ffn_out{layer}", act, w_out, res=xin)
        return xout, (nrm, gu, act, w_in, w_out)

    x2, saved0 = ffn_fwd(x1, 0)
    kvn, hb = _rms_fwd("rms_b", x2, [w["kv_norm"], w["b_norm"]], [BF16, BF16])
    w_kv, w_q, w_o = wts("wkv", hb), wts("wq", hb), wts("wo", hb)
    k, v = _mm_kv("kv_proj", kvn, w_kv)
    q = _mm_nn("q_proj", hb, w_q)
    outs, lses = [], []
    for grp, dil in enumerate(DILATIONS):
        o_g, l_g = _att_fwd(q, k, v, grp, dil)
        outs.append(o_g)
        lses.append(l_g)
    o, lse = _att_combine(outs, lses)
    wts("win1", o, prefetch=True)
    x3 = _mm_nn("o_proj", o, w_o, res=x2)
    x4, saved1 = ffn_fwd(x3, 1)
    loss_blk, dx4, dx4b, g["final_norm"] = _loss_head(x4, w["final_norm"], tgt)

    def ffn_bwd(dx, dxb, xin, saved, layer, dep):
        nrm, gu, act, w_in, w_out = saved
        dgu = _mm_nt_ffn_bwd(f"ffn_dact{layer}", dxb, w_out, gu, dep=dep)
        g_wout = _mm_tn(f"ffn_dwout{layer}", act, dxb)
        g_win = _mm_tn_pair(f"ffn_dwin{layer}", nrm, dgu)
        dn = _mm_nt_pair(f"ffn_dn{layer}", dgu, w_in, out_dtype=BF16)
        dxo, dxob, (dgn,) = _rms_bwd(f"rms_f{layer}_bwd", xin, dx, [w["ffn_norm"][layer:layer + 1]], [dn])
        tok = ready({f"win{layer}": g_win, f"wout{layer}": g_wout})
        return dxo, dxob, dgn, tok

    dx3, dx3b, dfn1, tok = ffn_bwd(dx4, dx4b, x3, saved1, 1, None)
    do = _mm_nt("o_proj_dx", [dx3b], w_o, dep=tok)
    g_wo = _mm_tn("o_proj_dw", o, dx3b)
    stats = _att_stats(lse, o, do)
    dq = lax.empty(q.shape, BF16)
    dks, dvs = [], []
    for grp, dil in enumerate(DILATIONS):
        dq, dk_g, dv_g = _att_bwd(q, k, v, do, stats, dq, grp, dil)
        dks.append(dk_g)
        dvs.append(dv_g)
    dkv = _sum_kv(dks, dvs)
    dhb = _mm_nt("q_proj_dx", [dq], w_q, out_dtype=BF16)
    g_wq = _mm_tn("q_proj_dw", hb, dq)
    dkvn = _mm_nt_pair("kv_proj_dx", dkv, w_kv, out_dtype=BF16)
    g_wkv = _mm_tn_pair("kv_proj_dw", kvn, dkv)
    dx2, dx2b, (g["kv_norm"], g["b_norm"]) = _rms_bwd(
        "rms_b_bwd", x2, dx3, [w["kv_norm"], w["b_norm"]], [dkvn, dhb])
    tok = ready({"wkv": g_wkv, "wq": g_wq, "wo": g_wo})
    dx1, dx1b, dfn0, tok = ffn_bwd(dx2, dx2b, x1, saved0, 0, tok)
    g["ffn_norm"] = jnp.concatenate([dfn0, dfn1], axis=0)

    dvg = _glu_bwd(dx1, vg, dep=tok)
    dz = _mm_nt_pair("glu_dx", dvg, w_glu)
    tok = ready({"glu": _mm_tn_pair("glu_dw", z, dvg)})
    tok = ready({}, after=tok)
    du, g["s5_d"], da_re, da_im, dbb2, dcc2 = _s5_bwd(
        u0, dz, h0, bb2, cc2, a_re, a_im, w["s5_d"], dep=tok)
    dcc = dcc2.transpose(0, 1, 3, 2).reshape(2, G, C, P)
    g["c_re"], g["c_im"] = dcc[0], -dcc[1]
    dbbt = dbb2[..., :P].reshape(2, G, C, P).transpose(0, 2, 1, 3)
    g["lam_re"], g["lam_im"], g["log_dt"], g["bt_re"], g["bt_im"] = _s5_param_bwd(
        w["lam_re"], w["lam_im"], w["log_dt"], w["bt_re"], w["bt_im"],
        da_re.reshape(G, P), da_im.reshape(G, P), dbbt[0], dbbt[1])
    tok = small_ready(g) if small_ready is not None else None
    grad_x, _, (g["a_norm"],) = _rms_bwd("rms_a_bwd", x, dx1, [w["a_norm"]], [du], dep=tok)
    return loss_blk, grad_x, g


def _coords():
    return lax.axis_index("x"), lax.axis_index("y"), lax.axis_index("c")


def _dev_index(dev):
    return 4 * dev[0] + 2 * dev[1] + dev[2]


def _shard_window(ref, axis, width, idx):
    sl = [slice(None)] * len(ref.shape)
    sl[axis] = pl.ds(pl.multiple_of(idx * width, width), width)
    return ref.at[tuple(sl)]


def _all_gather(name, shards, axes):
    na = len(shards)
    widths = [s.shape[ax] for s, ax in zip(shards, axes)]
    out_shapes = []
    for s, ax in zip(shards, axes):
        shp = list(s.shape)
        shp[ax] *= N_DEV
        out_shapes.append(jax.ShapeDtypeStruct(tuple(shp), s.dtype))

    def body(*refs):
        ins, outs = refs[:na], refs[na:2 * na]
        send_sems, recv_sems, local_sems = refs[2 * na:]
        x, y, c = _coords()
        me, sib = (x, y, c), (x, y, 1 - c)
        chips = [(1 - x, y), (x, 1 - y), (1 - x, 1 - y)]

        def blk(a, dev):
            return _shard_window(outs[a], axes[a], widths[a], _dev_index(dev))

        def copy(a, kk, block, to, src=None):
            return pltpu.make_async_remote_copy(
                src_ref=blk(a, block) if src is None else src, dst_ref=blk(a, block),
                send_sem=send_sems.at[a, kk], recv_sem=recv_sems.at[a, kk],
                device_id=to, device_id_type=MESH)

        local = [pltpu.make_async_copy(ins[a], blk(a, me), local_sems.at[a]) for a in range(na)]
        for cp in local:
            cp.start()
        sent = []
        for a in range(na):
            first = [copy(a, 0, me, sib, src=ins[a])]
            first += [copy(a, 1 + j, me, (*chip, c), src=ins[a]) for j, chip in enumerate(chips)]
            for cp in first:
                cp.start()
            sent += first
        for a in range(na):
            for j, chip in enumerate(chips):
                copy(a, 1 + j, (*chip, c), me).wait_recv()
                fwd = copy(a, 4 + j, (*chip, c), sib)
                fwd.start()
                sent.append(fwd)
        for a in range(na):
            copy(a, 0, sib, me).wait_recv()
            for j, chip in enumerate(chips):
                copy(a, 4 + j, (*chip, 1 - c), me).wait_recv()
        for cp in sent:
            cp.wait_send()
        for cp in local:
            cp.wait()

    return pl.pallas_call(
        body, out_shape=out_shapes, in_specs=[ANY] * na, out_specs=[ANY] * na,
        scratch_shapes=[pltpu.SemaphoreType.DMA((na, 7)), pltpu.SemaphoreType.DMA((na, 7)),
                        pltpu.SemaphoreType.DMA((na,))],
        name=name)(*shards)


HBM = pl.BlockSpec(memory_space=pltpu.HBM)
SEM = pl.BlockSpec(memory_space=pltpu.SEMAPHORE)
TOKEN_SPEC = pl.BlockSpec(memory_space=pltpu.VMEM)
TOKEN_SHAPE = jax.ShapeDtypeStruct((8, 128), F32)
SPLIT_PARAMS = pltpu.CompilerParams(has_side_effects=pltpu.SideEffectType.DATAFLOW_SIDE_EFFECTING)


def _hbm(x):
    return pltpu.with_memory_space_constraint(x, pltpu.HBM)


def _hbm_like(x):
    return pltpu.HBM(x.shape, x.dtype)


def _dma_sems(*shape):
    return pltpu.SemaphoreType.DMA(shape)


def _cast_and_place(name, shard, layer, axis, pos, dtype):
    rows, cols = shard.shape[-2:]
    tr = _tile(rows, 256, 16)
    nt = rows // tr
    full = (rows, cols * N_DEV) if axis == 1 else (rows * N_DEV, cols)

    def dev(p):
        return 4 * p[0] + 2 * p[1] + p[2]

    def body(pos_ref, s_ref, b_ref, l_ref):
        v = s_ref[...].astype(dtype)
        b_ref[...] = v
        l_ref[...] = v

    blk = pl.BlockSpec((tr, cols), lambda i, p: (i, 0))
    if axis == 1:
        lspec = pl.BlockSpec((tr, cols), lambda i, p: (i, dev(p)))
    else:
        lspec = pl.BlockSpec((tr, cols), lambda i, p: (dev(p) * nt + i, 0))
    return pl.pallas_call(
        body, grid_spec=pltpu.PrefetchScalarGridSpec(
            num_scalar_prefetch=1, grid=(nt,),
            in_specs=[pl.BlockSpec((None, tr, cols), lambda i, p: (layer, i, 0))], out_specs=[blk, lspec]),
        out_shape=[jax.ShapeDtypeStruct((rows, cols), dtype), jax.ShapeDtypeStruct(full, dtype)],
        compiler_params=_params(("parallel",)), name=name)(pos, shard)


def _gather_start(name, shards, lands, axes, groups):
    na, ng = len(shards), len(groups)
    widths = [s.shape[ax] for s, ax in zip(shards, axes)]

    def body(*refs):
        sh, ld = refs[:na], refs[na:2 * na]
        sems = refs[2 * na:2 * na + 3 * ng]
        token = refs[-1]
        x, y, c = _coords()
        me, sib = (x, y, c), (x, y, 1 - c)
        chips = [(1 - x, y), (x, 1 - y), (1 - x, 1 - y)]
        for gi, grp in enumerate(groups):
            send, r_d2d, r_ici = sems[3 * gi:3 * gi + 3]
            for li, a in enumerate(grp):
                dst = _shard_window(ld[a], axes[a], widths[a], _dev_index(me))
                pltpu.make_async_remote_copy(
                    src_ref=sh[a], dst_ref=dst, send_sem=send.at[4 * li], recv_sem=r_d2d.at[li],
                    device_id=sib, device_id_type=MESH).start()
                for j, chip in enumerate(chips):
                    pltpu.make_async_remote_copy(
                        src_ref=sh[a], dst_ref=dst, send_sem=send.at[4 * li + 1 + j], recv_sem=r_ici.at[3 * li + j],
                        device_id=(*chip, c), device_id_type=MESH).start()
        token[...] = jnp.zeros_like(token)

    out_shape, out_specs = [], []
    for grp in groups:
        out_shape += [_dma_sems(4 * len(grp)), _dma_sems(len(grp)), _dma_sems(3 * len(grp))]
        out_specs += [SEM] * 3
    out_shape += [_hbm_like(s) for s in shards] + [_hbm_like(l) for l in lands] + [TOKEN_SHAPE]
    out_specs += [HBM] * (2 * na) + [TOKEN_SPEC]
    aliases = {a: 3 * ng + a for a in range(2 * na)}
    res = pl.pallas_call(
        body, name=name, out_shape=out_shape, in_specs=[HBM] * (2 * na),
        out_specs=out_specs, input_output_aliases=aliases, compiler_params=SPLIT_PARAMS,
    )(*[_hbm(s) for s in shards], *[_hbm(l) for l in lands])
    sems = [tuple(res[3 * gi:3 * gi + 3]) for gi in range(ng)]
    return sems, list(res[3 * ng:3 * ng + na]), list(res[3 * ng + na:3 * ng + 2 * na]), res[-1]


def _gather_forward(name, lands, axes, r_ici, after):
    n = len(lands)
    widths = [l.shape[ax] // N_DEV for l, ax in zip(lands, axes)]

    def body(*refs):
        ld, r_ici_ref = refs[:n], refs[n]
        f_send, f_recv = refs[n + 2], refs[n + 3]
        x, y, c = _coords()
        sib = (x, y, 1 - c)
        chips = [(1 - x, y), (x, 1 - y), (1 - x, 1 - y)]
        for li in range(n):
            for j, chip in enumerate(chips):
                blk = _shard_window(ld[li], axes[li], widths[li], _dev_index((*chip, c)))
                pltpu.make_async_remote_copy(
                    src_ref=blk, dst_ref=blk, send_sem=f_send.at[3 * li + j], recv_sem=r_ici_ref.at[3 * li + j],
                    device_id=(*chip, c), device_id_type=MESH).wait_recv()
                pltpu.make_async_remote_copy(
                    src_ref=blk, dst_ref=blk, send_sem=f_send.at[3 * li + j], recv_sem=f_recv.at[3 * li + j],
                    device_id=sib, device_id_type=MESH).start()

    res = pl.pallas_call(
        body, name=name, out_shape=[_dma_sems(3 * n), _dma_sems(3 * n)] + [_hbm_like(l) for l in lands],
        in_specs=[HBM] * n + [SEM, ANY], out_specs=[SEM, SEM] + [HBM] * n,
        input_output_aliases={li: 2 + li for li in range(n)}, compiler_params=SPLIT_PARAMS,
    )(*lands, r_ici, after)
    return res[0], res[1], list(res[2:])


def _gather_finish(name, shards, lands, axes, send, r_d2d, f_send, f_recv, after):
    n = len(lands)
    widths = [l.shape[ax] // N_DEV for l, ax in zip(lands, axes)]

    def body(*refs):
        sh, ld = refs[:n], refs[n:2 * n]
        send_ref, r_d2d_ref, f_send_ref, f_recv_ref = refs[2 * n:2 * n + 4]
        x, y, c = _coords()
        me, sib = (x, y, c), (x, y, 1 - c)
        chips = [(1 - x, y), (x, 1 - y), (1 - x, 1 - y)]

        def blk(li, dev):
            return _shard_window(ld[li], axes[li], widths[li], _dev_index(dev))

        for li in range(n):
            for kk in range(4):
                pltpu.make_async_remote_copy(
                    src_ref=sh[li], dst_ref=blk(li, me), send_sem=send_ref.at[4 * li + kk], recv_sem=r_d2d_ref.at[li],
                    device_id=sib, device_id_type=MESH).wait_send()
            pltpu.make_async_remote_copy(
                src_ref=blk(li, sib), dst_ref=blk(li, sib), send_sem=send_ref.at[4 * li], recv_sem=r_d2d_ref.at[li],
                device_id=sib, device_id_type=MESH).wait_recv()
            for j, chip in enumerate(chips):
                pltpu.make_async_remote_copy(
                    src_ref=blk(li, (*chip, c)), dst_ref=blk(li, (*chip, c)), send_sem=f_send_ref.at[3 * li + j],
                    recv_sem=f_recv_ref.at[3 * li + j], device_id=sib, device_id_type=MESH).wait_send()
                pltpu.make_async_remote_copy(
                    src_ref=blk(li, (*chip, 1 - c)), dst_ref=blk(li, (*chip, 1 - c)), send_sem=f_send_ref.at[3 * li + j],
                    recv_sem=f_recv_ref.at[3 * li + j], device_id=sib, device_id_type=MESH).wait_recv()

    res = pl.pallas_call(
        body, name=name, out_shape=[_hbm_like(s) for s in shards] + [_hbm_like(l) for l in lands],
        in_specs=[HBM] * (2 * n) + [SEM] * 4 + [ANY], out_specs=[HBM] * (2 * n),
        input_output_aliases={i: i for i in range(2 * n)}, compiler_params=SPLIT_PARAMS,
    )(*shards, *lands, send, r_d2d, f_send, f_recv, after)
    return list(res[n:])


def _chip_exchange_start(name, parts):
    n = len(parts)

    def body(*refs):
        src, ld = refs[:n], refs[n:2 * n]
        send, recv = refs[2 * n], refs[2 * n + 1]
        token = refs[-1]
        x, y, c = _coords()
        chips = [(1 - x, y), (x, 1 - y), (1 - x, 1 - y)]
        for li in range(n):
            for kk, chip in enumerate(chips):
                pltpu.make_async_remote_copy(
                    src_ref=src[li].at[kk], dst_ref=ld[li].at[kk], send_sem=send.at[3 * li + kk],
                    recv_sem=recv.at[3 * li + kk], device_id=(*chip, c), device_id_type=MESH).start()
        token[...] = jnp.zeros_like(token)

    lands = [lax.empty(p.shape, p.dtype) for p in parts]
    res = pl.pallas_call(
        body, name=name,
        out_shape=[_dma_sems(3 * n), _dma_sems(3 * n)] + [_hbm_like(p) for p in parts] * 2 + [TOKEN_SHAPE],
        in_specs=[HBM] * (2 * n), out_specs=[SEM, SEM] + [HBM] * (2 * n) + [TOKEN_SPEC],
        input_output_aliases={i: 2 + i for i in range(2 * n)}, compiler_params=SPLIT_PARAMS,
    )(*[_hbm(p) for p in parts], *[_hbm(l) for l in lands])
    return res[0], res[1], list(res[2:2 + n]), list(res[2 + n:2 + 2 * n]), res[-1]


def _chip_exchange_finish(name, started, after):
    counts = [len(st[2]) for st in started]
    total = sum(counts)
    ns = len(started)

    def body(*refs):
        src, ld = refs[:total], refs[total:2 * total]
        sems = refs[2 * total:2 * total + 2 * ns]
        x, y, c = _coords()
        chips = [(1 - x, y), (x, 1 - y), (1 - x, 1 - y)]
        off = 0
        for si, cnt in enumerate(counts):
            send, recv = sems[2 * si], sems[2 * si + 1]
            for li in range(cnt):
                for kk, chip in enumerate(chips):
                    cp = pltpu.make_async_remote_copy(
                        src_ref=src[off + li].at[kk], dst_ref=ld[off + li].at[kk], send_sem=send.at[3 * li + kk],
                        recv_sem=recv.at[3 * li + kk], device_id=(*chip, c), device_id_type=MESH)
                    cp.wait_send()
                    cp.wait_recv()
            off += cnt

    srcs = [p for st in started for p in st[2]]
    lands = [l for st in started for l in st[3]]
    sems = [s for st in started for s in st[:2]]
    res = pl.pallas_call(
        body, name=name, out_shape=[_hbm_like(p) for p in srcs + lands],
        in_specs=[HBM] * (2 * total) + [SEM] * (2 * ns) + [ANY], out_specs=[HBM] * (2 * total),
        input_output_aliases={i: i for i in range(2 * total)}, compiler_params=SPLIT_PARAMS,
    )(*srcs, *lands, *sems, after)
    out, off = [], total
    for cnt in counts:
        out.append(list(res[off:off + cnt]))
        off += cnt
    return out


def _pair_copies(src, ld, send, recv, axes, widths):
    x, y, c = _coords()
    chips = [(x, y), (1 - x, y), (x, 1 - y), (1 - x, 1 - y)]
    return [pltpu.make_async_remote_copy(
        src_ref=_shard_window(src[li], axes[li], widths[li], _dev_index((*chip, 1 - c))),
        dst_ref=ld[li].at[kk], send_sem=send.at[4 * li + kk], recv_sem=recv.at[4 * li + kk],
        device_id=(x, y, 1 - c), device_id_type=MESH)
        for li in range(len(src)) for kk, chip in enumerate(chips)]


def _pair_exchange_start(name, grads, axes):
    n = len(grads)
    widths = [gr.shape[ax] // N_DEV for gr, ax in zip(grads, axes)]
    lands = []
    for gr, ax, wd in zip(grads, axes, widths):
        shp = list(gr.shape)
        shp[ax] = wd
        lands.append(lax.empty((4, *shp), gr.dtype))

    def body(*refs):
        for cp in _pair_copies(refs[:n], refs[n:2 * n], refs[2 * n], refs[2 * n + 1], axes, widths):
            cp.start()
        refs[-1][...] = jnp.zeros_like(refs[-1])

    res = pl.pallas_call(
        body, name=name,
        out_shape=[_dma_sems(4 * n), _dma_sems(4 * n)] + [_hbm_like(a) for a in grads + lands] + [TOKEN_SHAPE],
        in_specs=[HBM] * (2 * n), out_specs=[SEM, SEM] + [HBM] * (2 * n) + [TOKEN_SPEC],
        input_output_aliases={i: 2 + i for i in range(2 * n)}, compiler_params=SPLIT_PARAMS,
    )(*[_hbm(a) for a in grads + lands])
    return res[0], res[1], list(res[2:2 + n]), list(res[2 + n:2 + 2 * n]), res[-1]


def _pair_exchange_finish(name, started, axes, after):
    send, recv, grads, lands, _ = started
    n = len(grads)
    widths = [gr.shape[ax] // N_DEV for gr, ax in zip(grads, axes)]

    def body(*refs):
        for cp in _pair_copies(refs[:n], refs[n:2 * n], refs[2 * n], refs[2 * n + 1], axes, widths):
            cp.wait_send()
            cp.wait_recv()

    res = pl.pallas_call(
        body, name=name, out_shape=[_hbm_like(a) for a in grads + lands],
        in_specs=[HBM] * (2 * n) + [SEM, SEM, ANY], out_specs=[HBM] * (2 * n),
        input_output_aliases={i: i for i in range(2 * n)}, compiler_params=SPLIT_PARAMS,
    )(*grads, *lands, send, recv, after)
    return list(res[:n]), list(res[n:])


def _pair_sum(name, grad, land, axis, pos):
    wd = grad.shape[axis] // N_DEV
    shard_shape = land.shape[1:]
    rows, cols = shard_shape
    tr = _tile(rows, 256, 16)
    nt = rows // tr

    def dev_of(kk, pos_ref):
        return 4 * (pos_ref[0] ^ (kk & 1)) + 2 * (pos_ref[1] ^ (kk >> 1)) + pos_ref[2]

    def gspec(kk):
        if axis == 1:
            return pl.BlockSpec((tr, wd), lambda t, p: (t, dev_of(kk, p)))
        return pl.BlockSpec((tr, cols), lambda t, p: (dev_of(kk, p) * nt + t, 0))

    def body(pos_ref, g0, g1, g2, g3, l_ref, p0_ref, pb_ref):
        p0_ref[...] = g0[...].astype(F32) + l_ref[0].astype(F32)
        for kk, g_ref in enumerate((g1, g2, g3)):
            pb_ref[kk] = (g_ref[...].astype(F32) + l_ref[kk + 1].astype(F32)).astype(BF16)

    return pl.pallas_call(
        body,
        grid_spec=pltpu.PrefetchScalarGridSpec(
            num_scalar_prefetch=1, grid=(nt,),
            in_specs=[gspec(kk) for kk in range(4)] + [pl.BlockSpec((4, tr, cols), lambda t, p: (0, t, 0))],
            out_specs=[pl.BlockSpec((tr, cols), lambda t, p: (t, 0)),
                       pl.BlockSpec((3, tr, cols), lambda t, p: (0, t, 0))]),
        out_shape=[jax.ShapeDtypeStruct(shard_shape, F32), jax.ShapeDtypeStruct((3, *shard_shape), BF16)],
        compiler_params=_params(("parallel",)), name=name)(pos, grad, grad, grad, grad, land)


def _adamw(name, parts, w, m, v, layer=None, prev=None, tile_rows=256):
    rows, cols = w.shape[-2:]
    tr = _tile(rows, tile_rows, 8)
    npart = len(parts)
    c1 = 1.0 - ADAM_B1 ** ADAM_STEP
    c2 = 1.0 - ADAM_B2 ** ADAM_STEP

    def body(*refs):
        p_refs = refs[:npart]
        w_ref, m_ref, v_ref = refs[npart:npart + 3]
        g_ref, d_ref, nm_ref, nv_ref = refs[-4:]
        g = None
        for r in p_refs:
            if len(r.shape) == 3:
                for i in range(r.shape[0]):
                    t = r[i].astype(F32)
                    g = t if g is None else g + t
            else:
                t = r[...].astype(F32)
                g = t if g is None else g + t
        nm = ADAM_B1 * m_ref[...] + (1.0 - ADAM_B1) * g
        nv = ADAM_B2 * v_ref[...] + (1.0 - ADAM_B2) * (g * g)
        g_ref[...] = g
        nm_ref[...] = nm
        nv_ref[...] = nv
        d_ref[...] = -ADAM_LR * ((nm / c1) / (jnp.sqrt(nv / c2) + ADAM_EPS) + ADAM_WD * w_ref[...])

    spec = pl.BlockSpec((tr, cols), lambda i: (i, 0))
    wspec = spec if layer is None else pl.BlockSpec((None, tr, cols), lambda i: (layer, i, 0))
    pspecs = [pl.BlockSpec((p.shape[0], tr, cols), lambda i: (0, i, 0)) if p.ndim == 3 else spec
              for p in parts]
    prev = list(prev) if prev else []
    return pl.pallas_call(
        body, grid=(rows // tr,), in_specs=pspecs + [wspec] * 3 + [ANY] * len(prev), out_specs=[wspec] * 4,
        out_shape=[jax.ShapeDtypeStruct(w.shape, F32)] * 4,
        input_output_aliases={npart + 3 + i: i for i in range(len(prev))},
        compiler_params=_params(("parallel",)), name=name)(*parts, w, m, v, *prev)


SMALL_NAMES = ("s5_lam_re", "s5_lam_im", "s5_log_dt", "s5_b_re", "s5_b_im", "s5_c_re", "s5_c_im",
               "ffn_norm", "b_norm_mix", "kv_norm", "final_norm")
SMALL_PAD = 1024


def _pack(parts):
    flat = []
    for p in parts:
        f = p.reshape(-1)
        pad = (-f.shape[0]) % SMALL_PAD
        if pad:
            f = jnp.concatenate([f, jnp.zeros((pad,), f.dtype)])
        flat.append(f)
    return jnp.concatenate(flat).reshape(-1, 128)


def _unpack(packed, shapes):
    flat = packed.reshape(-1)
    out, off = [], 0
    for shp in shapes:
        size = math.prod(shp)
        out.append(flat[off:off + size].reshape(shp))
        off += size + (-size) % SMALL_PAD
    return out


def kernel(x, s5_lam_re, s5_lam_im, s5_log_dt, s5_b_re, s5_b_im, s5_c_re, s5_c_im, s5_d, s5_w_glu, a_norm_mix, ffn_norm, ffn_w_in, ffn_w_out, b_norm_mix, attn_w_q, attn_w_o, kv_norm, w_kv, final_norm, loss_target, m_s5_lam_re, m_s5_lam_im, m_s5_log_dt, m_s5_b_re, m_s5_b_im, m_s5_c_re, m_s5_c_im, m_s5_d, m_s5_w_glu, m_a_norm_mix, m_ffn_norm, m_ffn_w_in, m_ffn_w_out, m_b_norm_mix, m_attn_w_q, m_attn_w_o, m_kv_norm, m_w_kv, m_final_norm, v_s5_lam_re, v_s5_lam_im, v_s5_log_dt, v_s5_b_re, v_s5_b_im, v_s5_c_re, v_s5_c_im, v_s5_d, v_s5_w_glu, v_a_norm_mix, v_ffn_norm, v_ffn_w_in, v_ffn_w_out, v_b_norm_mix, v_attn_w_q, v_attn_w_o, v_kv_norm, v_w_kv, v_final_norm):
    args = dict(locals())
    T, D = x.shape[1], x.shape[2]
    n_layers = ffn_w_in.shape[0]
    xi_, yi_, ci_ = _coords()
    pos = jnp.stack([xi_, yi_, ci_]).astype(jnp.int32)
    me = 4 * xi_ + 2 * yi_ + ci_

    big_names = ["glu"] + [f"win{l}" for l in range(n_layers)] + [f"wout{l}" for l in range(n_layers)] \
        + ["wkv", "wq", "wo"]
    big_shards = [(s5_w_glu, 0)] + [(ffn_w_in, l) for l in range(n_layers)] \
        + [(ffn_w_out, l) for l in range(n_layers)] + [(w_kv[None], 0), (attn_w_q, 0), (attn_w_o, 0)]
    big_axes = [1] + [1] * n_layers + [0] * n_layers + [0, 1, 0]
    big_out_names = ["s5_w_glu"] + ["ffn_w_in"] * n_layers + ["ffn_w_out"] * n_layers \
        + ["w_kv", "attn_w_q", "attn_w_o"]
    index_of = {n: i for i, n in enumerate(big_names)}
    vec_shard = jnp.concatenate([s5_d, a_norm_mix], axis=0)
    (vecs,) = _all_gather("vectors_all_gather", [vec_shard], [1])

    gather_groups = [["glu"], ["win0"], ["wout0"], ["wkv", "wq", "wo"], ["win1", "wout1"]]
    group_idx = [[index_of[n] for n in grp] for grp in gather_groups]
    group_of = {n: gi for gi, grp in enumerate(gather_groups) for n in grp}
    placed = [_cast_and_place(f"cast_place_{n}", s, l, ax, pos, BF16)
              for n, (s, l), ax in zip(big_names, big_shards, big_axes)]
    gather_sems, shards_thru, lands_thru, start_token = _gather_start(
        "weights_gather_start", [p[0] for p in placed], [p[1] for p in placed], big_axes, group_idx)
    full = {}

    forwarded = {}

    def wts(name, after, prefetch=False):
        gi = group_of[name]
        idx = group_idx[gi]
        axes = [big_axes[a] for a in idx]
        send, r_d2d, r_ici = gather_sems[gi]
        if gi not in forwarded:
            forwarded[gi] = _gather_forward(
                f"weights_gather_forward{gi}", [lands_thru[a] for a in idx], axes, r_ici, after)
        if prefetch:
            return None
        if name not in full:
            f_send, f_recv, lands = forwarded[gi]
            done = _gather_finish(f"weights_gather_finish{gi}", [shards_thru[a] for a in idx], lands, axes,
                                  send, r_d2d, f_send, f_recv, after)
            full.update(zip(gather_groups[gi], done))
        return full[name]

    exchanges, pending = [], []

    def flush(after):
        names, axes, started = pending.pop()
        tag = "_".join(names)
        grads, lands = _pair_exchange_finish(f"rs_pair_exchange_finish_{tag}", started, axes, after)
        p0s, pbs = [], []
        for n, gr, land, ax in zip(names, grads, lands, axes):
            p0, pb = _pair_sum(f"rs_pair_sum_{n}", gr, land, ax, pos)
            p0s.append(p0)
            pbs.append(pb)
        started = _chip_exchange_start(f"rs_chip_exchange_start_{tag}", pbs)
        exchanges.append((names, p0s, started))
        return started[4]

    def ready(grads, after=None):
        if not grads:
            return flush(after)
        names = list(grads)
        axes = [big_axes[index_of[n]] for n in names]
        started = _pair_exchange_start(f"rs_pair_exchange_start_{'_'.join(names)}", [grads[n] for n in names], axes)
        token = flush(started[4]) if pending else started[4]
        pending.append((names, axes, started))
        return token

    G, P, C = s5_b_re.shape[1:]
    w = dict(
        a_norm=vecs[1:2], s5_d=vecs[0:1],
        ffn_norm=ffn_norm, b_norm=b_norm_mix, kv_norm=kv_norm.reshape(1, D), final_norm=final_norm.reshape(1, D),
        lam_re=s5_lam_re[0], lam_im=s5_lam_im[0], log_dt=s5_log_dt.reshape(G, 1),
        bt_re=s5_b_re[0].transpose(2, 0, 1), bt_im=s5_b_im[0].transpose(2, 0, 1),
        c_re=s5_c_re[0], c_im=s5_c_im[0],
    )
    small = {}

    def small_ready(g):
        small_g = dict(
            s5_lam_re=g["lam_re"], s5_lam_im=g["lam_im"], s5_log_dt=g["log_dt"],
            s5_b_re=g["bt_re"].transpose(1, 2, 0), s5_b_im=g["bt_im"].transpose(1, 2, 0),
            s5_c_re=g["c_re"], s5_c_im=g["c_im"], ffn_norm=g["ffn_norm"], b_norm_mix=g["b_norm"],
            kv_norm=g["kv_norm"], final_norm=g["final_norm"])
        packed = _pack([small_g[n] for n in SMALL_NAMES])
        shard, land = _cast_and_place("place_small_grads", packed[None], 0, 0, pos, F32)
        sems, thru, lands, token = _gather_start("small_grads_gather_start", [shard], [land], [0], [[0]])
        small.update(sems=sems[0], thru=thru, lands=lands, rows=packed.shape[0])
        return token

    loss_blk, grad_x, g = _local_step(x[0], loss_target[0], w, wts, ready, dep0=start_token,
                                      small_ready=small_ready)
    loss = lax.psum(loss_blk[0, 0], ("x", "y", "c"))
    (tail,) = _all_gather("vector_grads_all_gather", [_pack([g["s5_d"], g["a_norm"]])], [0])

    out = {}

    def put(name, res, shape):
        for kind, r in zip(("grad", "delta", "new_m", "new_v"), res):
            out[f"{kind}_{name}"] = r.reshape(shape)

    updated = {}

    def update(names, p0s, recvd):
        for name, p0, rc in zip(names, p0s, recvd):
            oname = big_out_names[index_of[name]]
            w3, layer = big_shards[index_of[name]]
            updated[oname] = _adamw(f"adamw_{name}", [p0, rc], w3, args["m_" + oname].reshape(w3.shape),
                                    args["v_" + oname].reshape(w3.shape), layer=layer, prev=updated.get(oname))
        return updated[big_out_names[index_of[names[0]]]][0]

    early, last = exchanges[:-1], exchanges[-1]
    landed = _chip_exchange_finish("rs_chip_exchange_finish_early", [e[2] for e in early], grad_x)
    for (names, p0s, _), recvd in zip(early[:-1], landed[:-1]):
        marker = update(names, p0s, recvd)
    send, r_d2d, r_ici = small["sems"]
    f_send, f_recv, small_lands = _gather_forward("small_grads_gather_forward", small["lands"], [0], r_ici, marker)
    marker = update(early[-1][0], early[-1][1], landed[-1])
    (all_parts,) = _gather_finish("small_grads_gather_finish", small["thru"], small_lands, [0],
                                  send, r_d2d, f_send, f_recv, marker)
    (recvd,) = _chip_exchange_finish("rs_chip_exchange_finish_last", [last[2]], marker)
    update(last[0], last[1], recvd)
    for oname, res in updated.items():
        put(oname, res, args[oname].shape)
    all_parts = all_parts.reshape(N_DEV, small["rows"], 128)
    w_pack = _pack([args[n] for n in SMALL_NAMES])
    m_pack = _pack([args["m_" + n] for n in SMALL_NAMES])
    v_pack = _pack([args["v_" + n] for n in SMALL_NAMES])
    res = _adamw("adamw_small", [all_parts], w_pack, m_pack, v_pack, tile_rows=1024)
    shapes = [args[n].shape for n in SMALL_NAMES]
    unpacked = [_unpack(r, shapes) for r in res]
    for i, n in enumerate(SMALL_NAMES):
        put(n, [u[i] for u in unpacked], args[n].shape)
    ws = D // N_DEV
    tail = lax.dynamic_slice_in_dim(tail.reshape(N_DEV, 2, D), me * ws, ws, axis=2)
    res = _adamw("adamw_vec", [tail], vec_shard,
                 jnp.concatenate([m_s5_d, m_a_norm_mix], axis=0), jnp.concatenate([v_s5_d, v_a_norm_mix], axis=0))
    put("s5_d", [r[0:1] for r in res], s5_d.shape)
    put("a_norm_mix", [r[1:2] for r in res], a_norm_mix.shape)

    names = ("s5_lam_re", "s5_lam_im", "s5_log_dt", "s5_b_re", "s5_b_im", "s5_c_re", "s5_c_im", "s5_d",
             "s5_w_glu", "a_norm_mix", "ffn_norm", "ffn_w_in", "ffn_w_out", "b_norm_mix", "attn_w_q",
             "attn_w_o", "kv_norm", "w_kv", "final_norm")
    result = [loss, grad_x.reshape(x.shape)]
    for kind in ("grad", "delta", "new_m", "new_v"):
        result += [out[f"{kind}_{n}"] for n in names]
    return tuple(result)
```

```python
import functools
import math

import jax
import jax.numpy as jnp
from jax import lax
from jax.experimental import pallas as pl
from jax.experimental.pallas import tpu as pltpu

F32 = jnp.float32
BF16 = jnp.bfloat16

EPS = 1e-6
NEG_INF = -1e30
HEAD_DIM = 128
N_KV_HEADS = 4
DILATIONS = (1, 4, 16)
ATT_BLK = 128
S5_C = 16
S5_P = 64
S5_GB = 16
S5_CH = S5_GB * S5_C
S5_W = S5_GB * S5_P
S5_UNROLL = 4
N_DEV = 8

ADAM_LR = 0.001
ADAM_B1 = 0.9
ADAM_B2 = 0.999
ADAM_EPS = 1e-08
ADAM_WD = 0.01
ADAM_STEP = 10

VMEM_LIMIT_BYTES = 56 * 1024 * 1024
MM_TILE = 1024
MM_TILE_NARROW = 512
MM_DEPTH = 2816
MESH = pl.DeviceIdType.MESH
ANY = pl.BlockSpec(memory_space=pl.ANY)


def _tile(n, pref, align=128):
    t = (min(pref, n) // align) * align
    while t >= align:
        if n % t == 0:
            return t
        t -= align
    return n


def _params(sem):
    return pltpu.CompilerParams(dimension_semantics=sem, vmem_limit_bytes=VMEM_LIMIT_BYTES)


def _sigmoid(x):
    return 1.0 / (1.0 + jnp.exp(-x))


NN = (((1,), (0,)), ((), ()))
NT = (((1,), (1,)), ((), ()))
TN = (((0,), (0,)), ((), ()))


def _dot(a, b, dims=NN):
    return lax.dot_general(a, b, dims, preferred_element_type=F32)


def _matmul(name, grid, ins, in_specs, products, dims, out_shapes, out_specs, acc_shapes, epilogue):
    n_in, n_out, nk = len(ins), len(out_shapes), grid[2]

    def body(*refs):
        in_refs = refs[:n_in]
        out_refs = refs[n_in:n_in + n_out]
        acc_refs = refs[n_in + n_out:]

        def prods():
            vals = [None] * len(acc_shapes)
            for ai, bi, ci in products:
                d = _dot(in_refs[ai][...].astype(BF16), in_refs[bi][...].astype(BF16), dims)
                vals[ci] = d if vals[ci] is None else vals[ci] + d
            return vals

        if nk == 1:
            epilogue(in_refs, out_refs, prods())
        else:
            k = pl.program_id(2)

            @pl.when(k == 0)
            def _():
                for a in acc_refs:
                    a[...] = jnp.zeros_like(a)

            for a, v in zip(acc_refs, prods()):
                a[...] += v

            @pl.when(k == nk - 1)
            def _():
                epilogue(in_refs, out_refs, [a[...] for a in acc_refs])

    scratch = [] if nk == 1 else [pltpu.VMEM(s, F32) for s in acc_shapes]
    return pl.pallas_call(
        body, grid=grid, in_specs=in_specs, out_specs=out_specs, out_shape=out_shapes,
        scratch_shapes=scratch, compiler_params=_params(("parallel", "parallel", "arbitrary")),
        name=name)(*ins)


def _mm_dual_fwd(name, a, w, res, kind):
    T, K = a.shape
    N = w.shape[1] // 2
    tm, tn = _tile(T, MM_TILE), _tile(N, MM_TILE_NARROW)
    nj = N // tn
    grid = (T // tm, nj, 1)
    ins = [a, w, w]
    in_specs = [pl.BlockSpec((tm, K), lambda i, j, k: (i, 0)),
                pl.BlockSpec((K, tn), lambda i, j, k: (0, j)),
                pl.BlockSpec((K, tn), lambda i, j, k: (0, j + nj))]
    pair_spec = pl.BlockSpec((2, tm, tn), lambda i, j, k: (0, i, j))
    tile_spec = pl.BlockSpec((tm, tn), lambda i, j, k: (i, j))
    if kind == "glu":
        ins.append(res)
        in_specs.append(tile_spec)

        def epilogue(in_refs, out_refs, accs):
            val, gate = accs
            s = _sigmoid(gate)
            out_refs[0][...] = in_refs[3][...] + val * s
            out_refs[1][0] = s.astype(BF16)
            out_refs[1][1] = (val * s * (1.0 - s)).astype(BF16)

        out_shapes = [jax.ShapeDtypeStruct((T, N), F32), jax.ShapeDtypeStruct((2, T, N), BF16)]
        out_specs = [tile_spec, pair_spec]
    else:
        def epilogue(in_refs, out_refs, accs):
            g, u = accs
            s = _sigmoid(g)
            silu = g * s
            out_refs[0][0] = (u * (s * (1.0 + g * (1.0 - s)))).astype(BF16)
            out_refs[0][1] = silu.astype(BF16)
            out_refs[1][...] = (silu * u).astype(BF16)

        out_shapes = [jax.ShapeDtypeStruct((2, T, N), BF16), jax.ShapeDtypeStruct((T, N), BF16)]
        out_specs = [pair_spec, tile_spec]
    return _matmul(name, grid, ins, in_specs, [(0, 1, 0), (0, 2, 1)], NN, out_shapes, out_specs,
                   [(tm, tn), (tm, tn)], epilogue)


def _mm_kv(name, a, w):
    T, K = a.shape
    N = w.shape[1] // 2
    tm, tn = _tile(T, MM_TILE), _tile(N, MM_TILE_NARROW)
    nj = N // tn
    tile_spec = pl.BlockSpec((tm, tn), lambda i, j, k: (i, j))

    def epilogue(in_refs, out_refs, accs):
        out_refs[0][...] = accs[0]
        out_refs[1][...] = accs[1]

    return _matmul(name, (T // tm, nj, 1), [a, w, w],
                   [pl.BlockSpec((tm, K), lambda i, j, k: (i, 0)),
                    pl.BlockSpec((K, tn), lambda i, j, k: (0, j)),
                    pl.BlockSpec((K, tn), lambda i, j, k: (0, j + nj))],
                   [(0, 1, 0), (0, 2, 1)], NN,
                   [jax.ShapeDtypeStruct((T, N), F32)] * 2, [tile_spec, tile_spec],
                   [(tm, tn), (tm, tn)], epilogue)


def _mm_nn(name, a, w, res=None, out_dtype=F32):
    T, K = a.shape
    N = w.shape[1]
    tk = K if K <= 2 * MM_DEPTH else _tile(K, MM_DEPTH)
    tm, tn = _tile(T, MM_TILE), _tile(N, MM_TILE if K <= MM_DEPTH else MM_TILE_NARROW)
    grid = (T // tm, N // tn, K // tk)
    tile_spec = pl.BlockSpec((tm, tn), lambda i, j, k: (i, j))
    ins = [a, w]
    in_specs = [pl.BlockSpec((tm, tk), lambda i, j, k: (i, k)),
                pl.BlockSpec((tk, tn), lambda i, j, k: (k, j))]
    if res is not None:
        ins.append(res)
        in_specs.append(tile_spec)

    def epilogue(in_refs, out_refs, accs):
        v = accs[0]
        if res is not None:
            v = v + in_refs[2][...]
        out_refs[0][...] = v.astype(out_dtype)

    return _matmul(name, grid, ins, in_specs, [(0, 1, 0)], NN,
                   [jax.ShapeDtypeStruct((T, N), out_dtype)], [tile_spec], [(tm, tn)], epilogue)[0]


def _dep_operand(ins, in_specs, dep):
    if dep is not None:
        ins.append(dep)
        in_specs.append(pl.BlockSpec((8, 128), lambda *_: (0, 0)))


def _mm_nt(name, a_list, w, out_dtype=F32, dep=None):
    T, Np = a_list[0].shape
    Ko = w.shape[0]
    n_parts = len(a_list)
    wide_a = a_list[0].dtype != BF16
    tm, tn, tk = _tile(T, MM_TILE_NARROW if wide_a else MM_TILE), _tile(Ko, MM_TILE), _tile(Np, MM_DEPTH)
    nkp = Np // tk
    grid = (T // tm, Ko // tn, nkp)
    ins = list(a_list) + [w] * n_parts
    in_specs = [pl.BlockSpec((tm, tk), lambda i, j, k: (i, k)) for _ in a_list]
    in_specs += [pl.BlockSpec((tn, tk), functools.partial(lambda i, j, k, p: (j, p * nkp + k), p=p))
                 for p in range(n_parts)]
    products = [(p, n_parts + p, 0) for p in range(n_parts)]
    _dep_operand(ins, in_specs, dep)

    def epilogue(in_refs, out_refs, accs):
        out_refs[0][...] = accs[0].astype(out_dtype)

    return _matmul(name, grid, ins, in_specs, products, NT,
                   [jax.ShapeDtypeStruct((T, Ko), out_dtype)],
                   [pl.BlockSpec((tm, tn), lambda i, j, k: (i, j))], [(tm, tn)], epilogue)[0]


def _mm_nt_pair(name, a3, w, out_dtype=F32):
    _, T, N = a3.shape
    Ko = w.shape[0]
    tm, tn, tk = _tile(T, MM_TILE), _tile(Ko, MM_TILE), _tile(N, MM_DEPTH)
    nkh = N // tk
    grid = (T // tm, Ko // tn, 2 * nkh)

    def epilogue(in_refs, out_refs, accs):
        out_refs[0][...] = accs[0].astype(out_dtype)

    return _matmul(name, grid, [a3, w],
                   [pl.BlockSpec((None, tm, tk), lambda i, j, k: (k // nkh, i, k % nkh)),
                    pl.BlockSpec((tn, tk), lambda i, j, k: (j, k))],
                   [(0, 1, 0)], NT, [jax.ShapeDtypeStruct((T, Ko), out_dtype)],
                   [pl.BlockSpec((tm, tn), lambda i, j, k: (i, j))], [(tm, tn)], epilogue)[0]


def _mm_nt_ffn_bwd(name, dx, w_out, gu, dep=None):
    T, D = dx.shape
    Fh = w_out.shape[0]
    tm, tn = _tile(T, MM_TILE), _tile(Fh, MM_TILE_NARROW)
    n_dep = 0 if dep is None else 1

    def body(dx_ref, w_ref, gu_ref, *rest):
        out_ref = rest[n_dep]
        da = _dot(dx_ref[...], w_ref[...], NT)
        out_ref[0] = (da * gu_ref[0].astype(F32)).astype(BF16)
        out_ref[1] = (da * gu_ref[1].astype(F32)).astype(BF16)

    pair_spec = pl.BlockSpec((2, tm, tn), lambda i, j: (0, i, j))
    ins = [dx, w_out, gu]
    in_specs = [pl.BlockSpec((tm, D), lambda i, j: (i, 0)),
                pl.BlockSpec((tn, D), lambda i, j: (j, 0)),
                pair_spec]
    _dep_operand(ins, in_specs, dep)
    return pl.pallas_call(
        body, grid=(T // tm, Fh // tn), in_specs=in_specs, out_specs=pair_spec,
        out_shape=jax.ShapeDtypeStruct((2, T, Fh), BF16),
        compiler_params=_params(("parallel", "parallel")), name=name)(*ins)


def _mm_tn(name, a, d):
    T, Ko = a.shape
    N = d.shape[1]
    to, tn, tk = _tile(Ko, MM_TILE_NARROW), _tile(N, MM_TILE if d.dtype == BF16 else MM_TILE_NARROW), T
    grid = (Ko // to, N // tn, T // tk)

    def epilogue(in_refs, out_refs, accs):
        out_refs[0][...] = accs[0].astype(BF16)

    return _matmul(name, grid, [a, d],
                   [pl.BlockSpec((tk, to), lambda i, j, k: (k, i)),
                    pl.BlockSpec((tk, tn), lambda i, j, k: (k, j))],
                   [(0, 1, 0)], TN, [jax.ShapeDtypeStruct((Ko, N), BF16)],
                   [pl.BlockSpec((to, tn), lambda i, j, k: (i, j))], [(to, tn)], epilogue)[0]


def _mm_tn_pair(name, a, d3):
    T, Ko = a.shape
    N = d3.shape[2]
    to, tn, tk = _tile(Ko, MM_TILE), _tile(N, MM_TILE_NARROW), T
    njh = N // tn
    grid = (Ko // to, 2 * njh, T // tk)

    def epilogue(in_refs, out_refs, accs):
        out_refs[0][...] = accs[0].astype(BF16)

    return _matmul(name, grid, [a, d3],
                   [pl.BlockSpec((tk, to), lambda i, j, k: (k, i)),
                    pl.BlockSpec((None, tk, tn), lambda i, j, k: (j // njh, k, j % njh))],
                   [(0, 1, 0)], TN, [jax.ShapeDtypeStruct((Ko, 2 * N), BF16)],
                   [pl.BlockSpec((to, tn), lambda i, j, k: (i, j))], [(to, tn)], epilogue)[0]


def _rms_fwd(name, x, gains, dtypes, dep=None):
    T, D = x.shape
    n = len(gains)
    tr = _tile(T, 512, 8)
    n_dep = 0 if dep is None else 1

    def body(x_ref, *refs):
        xv = x_ref[...]
        xr = xv * lax.rsqrt(jnp.mean(xv * xv, axis=-1, keepdims=True) + EPS)
        for g_ref, o_ref in zip(refs[:n], refs[n + n_dep:]):
            o_ref[...] = (xr * g_ref[...]).astype(o_ref.dtype)

    row = pl.BlockSpec((tr, D), lambda i: (i, 0))
    vec = pl.BlockSpec((1, D), lambda i: (0, 0))
    ins, in_specs = [x, *gains], [row] + [vec] * n
    _dep_operand(ins, in_specs, dep)
    return pl.pallas_call(
        body, grid=(T // tr,), in_specs=in_specs, out_specs=[row] * n,
        out_shape=[jax.ShapeDtypeStruct((T, D), dt) for dt in dtypes],
        compiler_params=_params(("parallel",)), name=name)(*ins)


def _rms_bwd(name, x, dres, gains, dhs, dep=None):
    T, D = x.shape
    n = len(gains)
    tr = _tile(T, 256, 8)
    n_dep = 0 if dep is None else 1

    def body(x_ref, dres_ref, *refs):
        g_refs, dh_refs = refs[:n], refs[n:2 * n]
        dx_ref, dxb_ref = refs[2 * n + n_dep], refs[2 * n + n_dep + 1]
        dg_refs = refs[2 * n + n_dep + 2:]
        xv = x_ref[...]
        r = lax.rsqrt(jnp.mean(xv * xv, axis=-1, keepdims=True) + EPS)
        xr = xv * r
        w = None
        for g_ref, dh_ref, dg_ref in zip(g_refs, dh_refs, dg_refs):
            dh = dh_ref[...].astype(F32)

            @pl.when(pl.program_id(0) == 0)
            def _():
                dg_ref[...] = jnp.zeros_like(dg_ref)

            dg_ref[...] += jnp.sum(dh * xr, axis=0, keepdims=True)
            wi = dh * g_ref[...]
            w = wi if w is None else w + wi
        dx = dres_ref[...] + r * (w - xr * jnp.mean(w * xr, axis=-1, keepdims=True))
        dx_ref[...] = dx
        dxb_ref[...] = dx.astype(BF16)

    row = pl.BlockSpec((tr, D), lambda i: (i, 0))
    vec = pl.BlockSpec((1, D), lambda i: (0, 0))
    ins, in_specs = [x, dres, *gains, *dhs], [row, row] + [vec] * n + [row] * n
    _dep_operand(ins, in_specs, dep)
    outs = pl.pallas_call(
        body, grid=(T // tr,), in_specs=in_specs,
        out_specs=[row, row] + [vec] * n,
        out_shape=[jax.ShapeDtypeStruct((T, D), F32), jax.ShapeDtypeStruct((T, D), BF16)]
        + [jax.ShapeDtypeStruct((1, D), F32)] * n,
        compiler_params=_params(("arbitrary",)), name=name)(*ins)
    return outs[0], outs[1], outs[2:]


def _loss_head(x, gain, target):
    T, D = x.shape
    tr = _tile(T, 256, 8)

    def body(x_ref, g_ref, t_ref, loss_ref, dx_ref, dxb_ref, dg_ref):
        @pl.when(pl.program_id(0) == 0)
        def _():
            loss_ref[...] = jnp.zeros_like(loss_ref)
            dg_ref[...] = jnp.zeros_like(dg_ref)

        xv = x_ref[...]
        r = lax.rsqrt(jnp.mean(xv * xv, axis=-1, keepdims=True) + EPS)
        xr = xv * r
        err = xr * g_ref[...] - t_ref[...]
        part = jnp.sum(jnp.sum(err * err, axis=-1, keepdims=True), axis=0, keepdims=True) * (0.5 / D)
        loss_ref[...] += jnp.broadcast_to(part, loss_ref.shape)
        dy = err * (1.0 / D)
        dg_ref[...] += jnp.sum(dy * xr, axis=0, keepdims=True)
        w = dy * g_ref[...]
        dx = r * (w - xr * jnp.mean(w * xr, axis=-1, keepdims=True))
        dx_ref[...] = dx
        dxb_ref[...] = dx.astype(BF16)

    row = pl.BlockSpec((tr, D), lambda i: (i, 0))
    vec = pl.BlockSpec((1, D), lambda i: (0, 0))
    return pl.pallas_call(
        body, grid=(T // tr,), in_specs=[row, vec, row],
        out_specs=[pl.BlockSpec((8, 128), lambda i: (0, 0)), row, row, vec],
        out_shape=[jax.ShapeDtypeStruct((8, 128), F32), jax.ShapeDtypeStruct((T, D), F32),
                   jax.ShapeDtypeStruct((T, D), BF16), jax.ShapeDtypeStruct((1, D), F32)],
        compiler_params=_params(("arbitrary",)), name="loss_head")(x, gain, target)


def _glu_bwd(dmix, vg, dep=None):
    T, N = dmix.shape
    tr, tc = _tile(T, 512, 8), _tile(N, 1024)
    n_dep = 0 if dep is None else 1

    def body(d_ref, vg_ref, *refs):
        o_ref = refs[n_dep]
        d = d_ref[...]
        o_ref[0] = (d * vg_ref[0].astype(F32)).astype(BF16)
        o_ref[1] = (d * vg_ref[1].astype(F32)).astype(BF16)

    pair = pl.BlockSpec((2, tr, tc), lambda i, j: (0, i, j))
    ins, in_specs = [dmix, vg], [pl.BlockSpec((tr, tc), lambda i, j: (i, j)), pair]
    _dep_operand(ins, in_specs, dep)
    return pl.pallas_call(
        body, grid=(T // tr, N // tc), in_specs=in_specs,
        out_specs=pair, out_shape=jax.ShapeDtypeStruct((2, T, N), BF16),
        compiler_params=_params(("parallel", "parallel")), name="glu_bwd")(*ins)


def _to_state_tiles(x_ref, s, val):
    tc = val.shape[0]
    for j in range(S5_W // 128):
        x_ref[s, pl.ds(j, tc, stride=8), :] = val[:, 128 * j:128 * (j + 1)]


def _from_state_tiles(x_ref, s, tc):
    return jnp.concatenate([x_ref[s, pl.ds(j, tc, stride=8), :] for j in range(S5_W // 128)], axis=1)


def _s5_scan_fwd(xr_ref, xi_ref, ar_ref, ai_ref, cr_ref, ci_ref, tc, nblk):
    a = [(ar_ref[s], ai_ref[s]) for s in range(nblk)]

    def step(i, carry):
        carry = list(carry)
        for uu in range(S5_UNROLL):
            r0 = pl.multiple_of((i * S5_UNROLL + uu) * 8, 8)
            for s in range(nblk):
                cr, ci = carry[2 * s], carry[2 * s + 1]
                a_r, a_i = a[s]
                xr = a_r * cr - a_i * ci + xr_ref[s, pl.ds(r0, 8), :]
                xi = a_r * ci + a_i * cr + xi_ref[s, pl.ds(r0, 8), :]
                xr_ref[s, pl.ds(r0, 8), :] = xr
                xi_ref[s, pl.ds(r0, 8), :] = xi
                carry[2 * s], carry[2 * s + 1] = xr, xi
        return tuple(carry)

    init = []
    for s in range(nblk):
        init += [cr_ref[s], ci_ref[s]]
    out = lax.fori_loop(0, tc // S5_UNROLL, step, tuple(init))
    for s in range(nblk):
        cr_ref[s] = out[2 * s]
        ci_ref[s] = out[2 * s + 1]


def _s5_scan_bwd(lr_ref, li_ref, xr_ref, xi_ref, h_ref, ar_ref, ai_ref, cr_ref, ci_ref,
                 accr_ref, acci_ref, tc, nblk):
    a = [(ar_ref[s], ai_ref[s]) for s in range(nblk)]

    def one(s, r0, prev_r, prev_i, st):
        c_r, c_i, d_r, d_i = st
        a_r, a_i = a[s]
        l_r = lr_ref[s, pl.ds(r0, 8), :] + a_r * c_r + a_i * c_i
        l_i = li_ref[s, pl.ds(r0, 8), :] + a_r * c_i - a_i * c_r
        lr_ref[s, pl.ds(r0, 8), :] = l_r
        li_ref[s, pl.ds(r0, 8), :] = l_i
        return [l_r, l_i, d_r + l_r * prev_r + l_i * prev_i, d_i - l_r * prev_i + l_i * prev_r]

    def step(i, carry):
        carry = list(carry)
        for uu in range(S5_UNROLL):
            t = tc - 1 - (i * S5_UNROLL + uu)
            r0 = pl.multiple_of(t * 8, 8)
            p0 = pl.multiple_of((t - 1) * 8, 8)
            for s in range(nblk):
                carry[4 * s:4 * s + 4] = one(s, r0, xr_ref[s, pl.ds(p0, 8), :], xi_ref[s, pl.ds(p0, 8), :],
                                             carry[4 * s:4 * s + 4])
        return tuple(carry)

    init = []
    for s in range(nblk):
        init += [cr_ref[s], ci_ref[s], accr_ref[s], acci_ref[s]]
    carry = list(lax.fori_loop(0, tc // S5_UNROLL - 1, step, tuple(init)))
    for t in range(S5_UNROLL - 1, -1, -1):
        for s in range(nblk):
            if t > 0:
                prev_r, prev_i = xr_ref[s, 8 * (t - 1):8 * t, :], xi_ref[s, 8 * (t - 1):8 * t, :]
            else:
                prev_r, prev_i = h_ref[0, s], h_ref[1, s]
            carry[4 * s:4 * s + 4] = one(s, 8 * t, prev_r, prev_i, carry[4 * s:4 * s + 4])
    for s in range(nblk):
        cr_ref[s], ci_ref[s], accr_ref[s], acci_ref[s] = carry[4 * s:4 * s + 4]


def _gelu(y):
    k = math.sqrt(2.0 / math.pi)
    return 0.5 * y * (1.0 + jnp.tanh(k * (y + 0.044715 * (y * y * y))))


def _gelu_grad(y):
    k = math.sqrt(2.0 / math.pi)
    t = jnp.tanh(k * (y + 0.044715 * (y * y * y)))
    return 0.5 * (1.0 + t) + 0.5 * y * (1.0 - t * t) * (k * (1.0 + 3.0 * 0.044715 * (y * y)))


def _s5_specs(tc, nch, sbk, rev):
    def ch(c):
        return nch - 1 - c if rev else c

    return dict(
        act=pl.BlockSpec((tc, sbk * S5_CH), lambda i, c: (ch(c), i)),
        bb=pl.BlockSpec((2, sbk, S5_CH, 128), lambda i, c: (0, i, 0, 0)),
        cc=pl.BlockSpec((2, sbk, S5_P, S5_CH), lambda i, c: (0, i, 0, 0)),
        a=pl.BlockSpec((sbk, 8, 128), lambda i, c: (i, 0, 0)),
        d=pl.BlockSpec((1, sbk * S5_CH), lambda i, c: (0, i)),
        h=pl.BlockSpec((None, 2, sbk, 8, 128), lambda i, c: (ch(c), 0, i, 0, 0)),
    )


def _s5_blocks(nb, pref):
    return max(b for b in range(1, pref + 1) if nb % b == 0)


def _s5_group_masks():
    rb = lax.broadcasted_iota(jnp.int32, (S5_CH, S5_W), 0) // S5_C
    qb = lax.broadcasted_iota(jnp.int32, (S5_CH, S5_W), 1) // S5_P
    qc = lax.broadcasted_iota(jnp.int32, (S5_W, S5_CH), 0) // S5_P
    rc = lax.broadcasted_iota(jnp.int32, (S5_W, S5_CH), 1) // S5_C
    return rb == qb, qc == rc


def _s5_expand(bb_ref, cc_ref, bbd, ccd, sbk):
    mask_b, mask_c = _s5_group_masks()
    for k in range(2):
        for s in range(sbk):
            bbd[k, s] = jnp.where(mask_b, jnp.tile(bb_ref[k, s], (1, S5_W // 128)), 0).astype(BF16)
            ccd[k, s] = jnp.where(mask_c, jnp.tile(cc_ref[k, s], (S5_GB, 1)), 0).astype(BF16)


def _s5_fwd(u, bb2, cc2, a_re, a_im, d_skip):
    T, D = u.shape
    nb = D // S5_CH
    sbk = _s5_blocks(nb, 4)
    tc = _tile(T, 512, 8)
    nch = T // tc
    sp = _s5_specs(tc, nch, sbk, False)

    def body(u_ref, bb_ref, cc_ref, ar_ref, ai_ref, d_ref, z_ref, h_ref, xr, xi, cr, ci, bbd, ccd):
        @pl.when(pl.program_id(1) == 0)
        def _():
            cr[...] = jnp.zeros_like(cr)
            ci[...] = jnp.zeros_like(ci)
            _s5_expand(bb_ref, cc_ref, bbd, ccd, sbk)

        h_ref[0] = cr[...]
        h_ref[1] = ci[...]
        for s in range(sbk):
            ub = u_ref[:, s * S5_CH:(s + 1) * S5_CH].astype(BF16)
            _to_state_tiles(xr, s, _dot(ub, bbd[0, s]))
            _to_state_tiles(xi, s, _dot(ub, bbd[1, s]))
        _s5_scan_fwd(xr, xi, ar_ref, ai_ref, cr, ci, tc, sbk)
        for s in range(sbk):
            cols = slice(s * S5_CH, (s + 1) * S5_CH)
            y = (_dot(_from_state_tiles(xr, s, tc).astype(BF16), ccd[0, s])
                 - _dot(_from_state_tiles(xi, s, tc).astype(BF16), ccd[1, s])
                 + d_ref[:, cols] * u_ref[:, cols])
            z_ref[:, cols] = _gelu(y).astype(BF16)

    tiles = pltpu.VMEM((sbk, tc * 8, 128), F32)
    carry = pltpu.VMEM((sbk, 8, 128), F32)
    return pl.pallas_call(
        body, grid=(nb // sbk, nch),
        in_specs=[sp["act"], sp["bb"], sp["cc"], sp["a"], sp["a"], sp["d"]],
        out_specs=[sp["act"], sp["h"]],
        out_shape=[jax.ShapeDtypeStruct((T, D), BF16), jax.ShapeDtypeStruct((nch, 2, nb, 8, 128), F32)],
        scratch_shapes=[tiles, tiles, carry, carry, pltpu.VMEM((2, sbk, S5_CH, S5_W), BF16),
                        pltpu.VMEM((2, sbk, S5_W, S5_CH), BF16)],
        compiler_params=_params(("parallel", "arbitrary")), name="s5_fwd",
    )(u, bb2, cc2, a_re, a_im, d_skip)


def _s5_bwd(u, dz, h0, bb2, cc2, a_re, a_im, d_skip, dep=None):
    T, D = u.shape
    nb = D // S5_CH
    sbk = _s5_blocks(nb, 2)
    tc = _tile(T, 512, 8)
    nch = T // tc
    sp = _s5_specs(tc, nch, sbk, True)

    n_dep = 0 if dep is None else 1

    def body(u_ref, dz_ref, h_ref, bb_ref, cc_ref, ar_ref, ai_ref, d_ref, *rest):
        (du_ref, dd_ref, dar_ref, dai_ref, dbb_ref, dcc_ref,
         xr, xi, lr, li, fr, fi, br, bi, accr, acci, bbd, ccd, dbbd, dccd) = rest[n_dep:]
        c = pl.program_id(1)

        @pl.when(c == 0)
        def _():
            for ref in (br, bi, accr, acci, dd_ref, dbbd, dccd):
                ref[...] = jnp.zeros_like(ref)
            _s5_expand(bb_ref, cc_ref, bbd, ccd, sbk)

        for s in range(sbk):
            ub = u_ref[:, s * S5_CH:(s + 1) * S5_CH].astype(BF16)
            _to_state_tiles(xr, s, _dot(ub, bbd[0, s]))
            _to_state_tiles(xi, s, _dot(ub, bbd[1, s]))
        fr[...] = h_ref[0]
        fi[...] = h_ref[1]
        _s5_scan_fwd(xr, xi, ar_ref, ai_ref, fr, fi, tc, sbk)
        for s in range(sbk):
            cols = slice(s * S5_CH, (s + 1) * S5_CH)
            uv = u_ref[:, cols]
            xrb = _from_state_tiles(xr, s, tc).astype(BF16)
            xib = _from_state_tiles(xi, s, tc).astype(BF16)
            dsk = d_ref[:, cols]
            y = _dot(xrb, ccd[0, s]) - _dot(xib, ccd[1, s]) + dsk * uv
            dy = dz_ref[:, cols] * _gelu_grad(y)
            dd_ref[:, cols] += jnp.sum(dy * uv, axis=0, keepdims=True)
            dyb = dy.astype(BF16)
            dccd[0, s] += _dot(xrb, dyb, TN)
            dccd[1, s] += _dot(xib, dyb, TN)
            _to_state_tiles(lr, s, _dot(dyb, ccd[0, s], NT))
            _to_state_tiles(li, s, -_dot(dyb, ccd[1, s], NT))
            du_ref[:, cols] = dy * dsk
        _s5_scan_bwd(lr, li, xr, xi, h_ref, ar_ref, ai_ref, br, bi, accr, acci, tc, sbk)
        for s in range(sbk):
            cols = slice(s * S5_CH, (s + 1) * S5_CH)
            ub = u_ref[:, cols].astype(BF16)
            lrb = _from_state_tiles(lr, s, tc).astype(BF16)
            lib = _from_state_tiles(li, s, tc).astype(BF16)
            dbbd[0, s] += _dot(ub, lrb, TN)
            dbbd[1, s] += _dot(ub, lib, TN)
            du_ref[:, cols] += _dot(lrb, bbd[0, s], NT) + _dot(lib, bbd[1, s], NT)

        @pl.when(c == nch - 1)
        def _():
            dar_ref[...] = accr[...]
            dai_ref[...] = acci[...]
            mask_b, mask_c = _s5_group_masks()
            for k in range(2):
                for s in range(sbk):
                    mb = jnp.where(mask_b, dbbd[k, s], 0.0)
                    fold = functools.reduce(
                        lambda a, b: a + b, [mb[:, 128 * j:128 * (j + 1)] for j in range(S5_W // 128)])
                    dbb_ref[k, s] = fold + pltpu.roll(fold, S5_P, 1)
                    mc = jnp.where(mask_c, dccd[k, s], 0.0)
                    dcc_ref[k, s] = functools.reduce(
                        lambda a, b: a + b, [mc[S5_P * j:S5_P * (j + 1), :] for j in range(S5_GB)])

    tiles = pltpu.VMEM((sbk, tc * 8, 128), F32)
    carry = pltpu.VMEM((sbk, 8, 128), F32)
    ins = [u, dz, h0, bb2, cc2, a_re, a_im, d_skip]
    in_specs = [sp["act"], sp["act"], sp["h"], sp["bb"], sp["cc"], sp["a"], sp["a"], sp["d"]]
    _dep_operand(ins, in_specs, dep)
    return pl.pallas_call(
        body, grid=(nb // sbk, nch),
        in_specs=in_specs,
        out_specs=[sp["act"], sp["d"], sp["a"], sp["a"], sp["bb"], sp["cc"]],
        out_shape=[jax.ShapeDtypeStruct((T, D), F32), jax.ShapeDtypeStruct((1, D), F32),
                   jax.ShapeDtypeStruct((nb, 8, 128), F32), jax.ShapeDtypeStruct((nb, 8, 128), F32),
                   jax.ShapeDtypeStruct((2, nb, S5_CH, 128), F32), jax.ShapeDtypeStruct((2, nb, S5_P, S5_CH), F32)],
        scratch_shapes=[tiles, tiles, tiles, tiles, carry, carry, carry, carry, carry, carry,
                        pltpu.VMEM((2, sbk, S5_CH, S5_W), BF16), pltpu.VMEM((2, sbk, S5_W, S5_CH), BF16),
                        pltpu.VMEM((2, sbk, S5_CH, S5_W), F32), pltpu.VMEM((2, sbk, S5_W, S5_CH), F32)],
        compiler_params=_params(("parallel", "arbitrary")), name="s5_bwd",
    )(*ins)


def _s5_disc(lr, li, ldt):
    dt = jnp.exp(ldt)
    mag = jnp.exp(lr * dt)
    ang = li * dt
    cs, sn = jnp.cos(ang), jnp.sin(ang)
    lbr, lbi = mag * cs, mag * sn
    nr = lbr - 1.0
    den = lr * lr + li * li
    f_re = (nr * lr + lbi * li) / den
    f_im = (lbi * lr - nr * li) / den
    return dt, mag, cs, sn, lbr, lbi, nr, den, f_re, f_im


def _s5_param_fwd(lr, li, ldt, bt_re, bt_im):
    c, g, p = bt_re.shape

    def body(lr_ref, li_ref, ldt_ref, br_ref, bi_ref, lbr_ref, lbi_ref, bbr_ref, bbi_ref):
        _, _, _, _, lbr, lbi, _, _, f_re, f_im = _s5_disc(lr_ref[...], li_ref[...], ldt_ref[...])
        lbr_ref[...] = lbr
        lbi_ref[...] = lbi
        for ch in range(c):
            b_r, b_i = br_ref[ch], bi_ref[ch]
            bbr_ref[ch] = f_re * b_r - f_im * b_i
            bbi_ref[ch] = f_re * b_i + f_im * b_r

    gp = jax.ShapeDtypeStruct((g, p), F32)
    cgp = jax.ShapeDtypeStruct((c, g, p), F32)
    return pl.pallas_call(body, out_shape=[gp, gp, cgp, cgp], name="s5_param_fwd")(lr, li, ldt, bt_re, bt_im)


def _s5_param_bwd(lr, li, ldt, bt_re, bt_im, dlbr, dlbi, dbbr, dbbi):
    c, g, p = bt_re.shape

    def body(lr_ref, li_ref, ldt_ref, br_ref, bi_ref, dlbr_ref, dlbi_ref, dbbr_ref, dbbi_ref,
             dlr_ref, dli_ref, dldt_ref, dbr_ref, dbi_ref):
        l_r, l_i = lr_ref[...], li_ref[...]
        dt, mag, cs, sn, lbr, lbi, nr, den, f_re, f_im = _s5_disc(l_r, l_i, ldt_ref[...])
        dfr = jnp.zeros_like(l_r)
        dfi = jnp.zeros_like(l_r)
        for ch in range(c):
            b_r, b_i = br_ref[ch], bi_ref[ch]
            g_r, g_i = dbbr_ref[ch], dbbi_ref[ch]
            dbr_ref[ch] = f_re * g_r + f_im * g_i
            dbi_ref[ch] = f_re * g_i - f_im * g_r
            dfr = dfr + g_r * b_r + g_i * b_i
            dfi = dfi + g_i * b_r - g_r * b_i
        inv = 1.0 / den
        d_nr = (dfr * l_r - dfi * l_i) * inv
        d_lbi = (dfr * l_i + dfi * l_r) * inv + dlbi_ref[...]
        d_lbr = d_nr + dlbr_ref[...]
        d_den = -(dfr * f_re + dfi * f_im) * inv
        d_mag = d_lbr * cs + d_lbi * sn
        d_ang = d_lbi * lbr - d_lbr * lbi
        dlr_ref[...] = (dfr * nr + dfi * lbi) * inv + 2.0 * d_den * l_r + d_mag * mag * dt
        dli_ref[...] = (dfr * lbi - dfi * nr) * inv + 2.0 * d_den * l_i + d_ang * dt
        dldt_ref[...] = jnp.sum(d_mag * mag * l_r + d_ang * l_i, axis=1, keepdims=True) * dt

    gp = jax.ShapeDtypeStruct((g, p), F32)
    cgp = jax.ShapeDtypeStruct((c, g, p), F32)
    return pl.pallas_call(body, out_shape=[gp, gp, jax.ShapeDtypeStruct((g, 1), F32), cgp, cgp],
                          name="s5_param_bwd")(lr, li, ldt, bt_re, bt_im, dlbr, dlbi, dbbr, dbbi)


def _att_masks(rep, gb):
    rows = rep * ATT_BLK
    qi = lax.broadcasted_iota(jnp.int32, (rows, 2 * ATT_BLK), 0) % ATT_BLK
    si = lax.broadcasted_iota(jnp.int32, (rows, 2 * ATT_BLK), 1)
    prev = (si < ATT_BLK) & (si >= qi) & (gb > 0)
    cur = (si >= ATT_BLK) & (si - ATT_BLK <= qi)
    return prev | cur


def _att_rows(start, dil):
    return pl.ds(start, ATT_BLK) if dil == 1 else pl.ds(start, ATT_BLK, stride=dil)


def _att_plan(T, dil):
    span = ATT_BLK * dil
    sbr = max(span, min(T, 1024))
    return span, sbr, T // sbr


def _att_block(sb, i, sbr, span, dil):
    loc = (i // dil) * span + i % dil
    cur = sb * sbr + loc
    gb = sb * (sbr // span) + i // dil
    return loc, cur, jnp.where(gb > 0, cur - span, cur), gb


def _att_fwd(q, k, v, grp, dil):
    T = q.shape[0]
    H = q.shape[1] // HEAD_DIM // len(DILATIONS)
    rep = H // N_KV_HEADS
    span, sbr, nsb = _att_plan(T, dil)
    scale = HEAD_DIM ** -0.5

    def body(*refs):
        q_refs = refs[:rep]
        k_ref, v_ref, o_ref, l_ref, o_slab, l_slab = refs[rep:]
        sb = pl.program_id(1)

        def blk(i, _):
            loc, cur, prv, gb = _att_block(sb, i, sbr, span, dil)
            rows = _att_rows(loc, dil)
            qs = jnp.concatenate([r[rows, :] for r in q_refs], axis=0).astype(BF16)
            kcat = jnp.concatenate([k_ref[_att_rows(prv, dil), :], k_ref[_att_rows(cur, dil), :]], axis=0)
            vcat = jnp.concatenate([v_ref[_att_rows(prv, dil), :], v_ref[_att_rows(cur, dil), :]], axis=0)
            s = jnp.where(_att_masks(rep, gb), _dot(qs, kcat.astype(BF16), NT) * scale, NEG_INF)
            m = jnp.max(s, axis=-1, keepdims=True)
            p = jnp.exp(s - m)
            l = jnp.sum(p, axis=-1, keepdims=True)
            o = _dot(p.astype(BF16), vcat.astype(BF16)) / l
            lse = jnp.broadcast_to(m + jnp.log(l), (rep * ATT_BLK, HEAD_DIM))
            for j in range(rep):
                o_slab[j, rows, :] = o[j * ATT_BLK:(j + 1) * ATT_BLK]
                l_slab[j, rows, :] = lse[j * ATT_BLK:(j + 1) * ATT_BLK]
            return 0

        lax.fori_loop(0, sbr // ATT_BLK, blk, 0)
        for j in range(rep):
            o_ref[:, j * HEAD_DIM:(j + 1) * HEAD_DIM] = o_slab[j].astype(BF16)
            l_ref[:, j * HEAD_DIM:(j + 1) * HEAD_DIM] = l_slab[j]

    qspecs = [pl.BlockSpec((sbr, HEAD_DIM), functools.partial(lambda h, s, j: (s, grp * H + h * rep + j), j=j))
              for j in range(rep)]
    kspec = pl.BlockSpec((T, HEAD_DIM), lambda h, s: (0, h))
    ospec = pl.BlockSpec((sbr, rep * HEAD_DIM), lambda h, s: (s, h))
    slab = pltpu.VMEM((rep, sbr, HEAD_DIM), F32)
    return pl.pallas_call(
        body, grid=(N_KV_HEADS, nsb), in_specs=qspecs + [kspec, kspec], out_specs=[ospec, ospec],
        out_shape=[jax.ShapeDtypeStruct((T, H * HEAD_DIM), BF16), jax.ShapeDtypeStruct((T, H * HEAD_DIM), F32)],
        scratch_shapes=[slab, slab],
        compiler_params=_params(("parallel", "arbitrary")), name=f"att_fwd_d{dil}",
    )(*([q] * rep), k, v)


def _att_combine(outs, lses):
    T, W = outs[0].shape
    ng = len(outs)
    tr, tcol = _tile(T, 512, 8), _tile(W, 512)

    def body(*refs):
        o_refs, l_refs = refs[:ng], refs[ng:2 * ng]
        ob_ref, lse_ref = refs[2 * ng:]
        ls = [r[...] for r in l_refs]
        m = functools.reduce(jnp.maximum, ls)
        es = [jnp.exp(l - m) for l in ls]
        den = functools.reduce(lambda a, b: a + b, es)
        num = functools.reduce(lambda a, b: a + b, [e * o[...].astype(F32) for e, o in zip(es, o_refs)])
        ob_ref[...] = (num / den).astype(BF16)
        lse_ref[...] = m + jnp.log(den)

    spec = pl.BlockSpec((tr, tcol), lambda i, j: (i, j))
    return pl.pallas_call(
        body, grid=(T // tr, W // tcol), in_specs=[spec] * (2 * ng), out_specs=[spec, spec],
        out_shape=[jax.ShapeDtypeStruct((T, W), BF16), jax.ShapeDtypeStruct((T, W), F32)],
        compiler_params=_params(("parallel", "parallel")), name="att_combine")(*outs, *lses)


STAT_LANE = HEAD_DIM // 2


def _att_stats(lse, o, do):
    T, W = lse.shape
    tr = _tile(T, 256, 16)

    def body(l_ref, o_ref, do_ref, s_ref):
        lane = lax.broadcasted_iota(jnp.int32, (tr, HEAD_DIM), 1)
        for h in range(W // HEAD_DIM):
            cols = slice(h * HEAD_DIM, (h + 1) * HEAD_DIM)
            delta = jnp.sum(do_ref[:, cols] * o_ref[:, cols].astype(F32), axis=-1, keepdims=True)
            s_ref[:, cols] = jnp.where(lane < STAT_LANE, l_ref[:, cols], delta)

    spec = pl.BlockSpec((tr, W), lambda i: (i, 0))
    return pl.pallas_call(
        body, grid=(T // tr,), in_specs=[spec] * 3, out_specs=spec,
        out_shape=jax.ShapeDtypeStruct((T, W), F32),
        compiler_params=_params(("parallel",)), name="att_stats")(lse, o, do)


def _att_bwd(q, k, v, do, stats, dq, grp, dil):
    T = q.shape[0]
    H = do.shape[1] // HEAD_DIM
    rep = H // N_KV_HEADS
    hs = rep
    span, sbr, nsb = _att_plan(T, dil)
    scale = HEAD_DIM ** -0.5

    def body(*refs):
        q_refs, do_refs, st_refs = refs[:hs], refs[hs + 2:2 * hs + 2], refs[2 * hs + 2:3 * hs + 2]
        k_ref, v_ref = refs[hs], refs[hs + 1]
        dq_ref, dk_ref, dv_ref, dq_slab = refs[3 * hs + 3:]
        sb = pl.program_id(2)

        @pl.when((pl.program_id(1) == 0) & (sb == 0))
        def _():
            dk_ref[...] = jnp.zeros_like(dk_ref)
            dv_ref[...] = jnp.zeros_like(dv_ref)

        def blk(i, _):
            loc, cur, prv, gb = _att_block(sb, i, sbr, span, dil)
            rows, kc, kp = _att_rows(loc, dil), _att_rows(cur, dil), _att_rows(prv, dil)
            qs = jnp.concatenate([r[rows, :] for r in q_refs], axis=0).astype(BF16)
            dos = jnp.concatenate([r[rows, :] for r in do_refs], axis=0).astype(BF16)
            st = jnp.concatenate([r[rows, :] for r in st_refs], axis=0)
            kcat = jnp.concatenate([k_ref[kp, :], k_ref[kc, :]], axis=0).astype(BF16)
            vcat = jnp.concatenate([v_ref[kp, :], v_ref[kc, :]], axis=0).astype(BF16)
            s = _dot(qs, kcat, NT) * scale
            p = jnp.where(_att_masks(hs, gb), jnp.exp(s - st[:, 0:1]), 0.0)
            dp = _dot(dos, vcat, NT)
            ds = (p * (dp - st[:, STAT_LANE:STAT_LANE + 1]) * scale).astype(BF16)
            dvc = _dot(p.astype(BF16), dos, TN)
            dkc = _dot(ds, qs, TN)
            dqs = _dot(ds, kcat)
            for j in range(hs):
                dq_slab[j, rows, :] = dqs[j * ATT_BLK:(j + 1) * ATT_BLK]
            dk_ref[kc, :] += dkc[ATT_BLK:]
            dv_ref[kc, :] += dvc[ATT_BLK:]

            @pl.when(gb > 0)
            def _():
                dk_ref[kp, :] += dkc[:ATT_BLK]
                dv_ref[kp, :] += dvc[:ATT_BLK]

            return 0

        lax.fori_loop(0, sbr // ATT_BLK, blk, 0)
        for j in range(hs):
            dq_ref[:, j * HEAD_DIM:(j + 1) * HEAD_DIM] = dq_slab[j].astype(BF16)

    def head_specs(col0):
        return [pl.BlockSpec((sbr, HEAD_DIM),
                             functools.partial(lambda h, f, s, j: (s, col0 + h * rep + f * hs + j), j=j))
                for j in range(hs)]

    kspec = pl.BlockSpec((T, HEAD_DIM), lambda h, f, s: (0, h))
    dqspec = pl.BlockSpec((sbr, hs * HEAD_DIM), lambda h, f, s: (s, (grp * H + h * rep) // hs + f))
    n_in = 3 * hs + 3
    return pl.pallas_call(
        body, grid=(N_KV_HEADS, rep // hs, nsb),
        in_specs=head_specs(grp * H) + [kspec, kspec] + head_specs(0) + head_specs(0) + [ANY],
        out_specs=[dqspec, kspec, kspec],
        out_shape=[jax.ShapeDtypeStruct(dq.shape, BF16),
                   jax.ShapeDtypeStruct((T, N_KV_HEADS * HEAD_DIM), F32),
                   jax.ShapeDtypeStruct((T, N_KV_HEADS * HEAD_DIM), F32)],
        scratch_shapes=[pltpu.VMEM((hs, sbr, HEAD_DIM), F32)],
        input_output_aliases={n_in - 1: 0},
        compiler_params=_params(("parallel", "arbitrary", "arbitrary")), name=f"att_bwd_d{dil}",
    )(*([q] * hs), k, v, *([do] * hs), *([stats] * hs), dq)


def _sum_kv(dks, dvs):
    T, W = dks[0].shape
    ng = len(dks)
    tr = _tile(T, 512, 8)

    def body(*refs):
        o_ref = refs[2 * ng]
        o_ref[0] = functools.reduce(lambda a, b: a + b, [r[...] for r in refs[:ng]]).astype(BF16)
        o_ref[1] = functools.reduce(lambda a, b: a + b, [r[...] for r in refs[ng:2 * ng]]).astype(BF16)

    spec = pl.BlockSpec((tr, W), lambda i: (i, 0))
    return pl.pallas_call(
        body, grid=(T // tr,), in_specs=[spec] * (2 * ng),
        out_specs=pl.BlockSpec((2, tr, W), lambda i: (0, i, 0)),
        out_shape=jax.ShapeDtypeStruct((2, T, W), BF16),
        compiler_params=_params(("parallel",)), name="sum_kv")(*dks, *dvs)


def _local_step(x, tgt, w, wts, ready, dep0=None, small_ready=None):
    T, D = x.shape
    g = {}

    (u0,) = _rms_fwd("rms_a", x, [w["a_norm"]], [F32], dep=dep0)
    lbr, lbi, bbt_re, bbt_im = _s5_param_fwd(w["lam_re"], w["lam_im"], w["log_dt"], w["bt_re"], w["bt_im"])
    a_re, a_im = lbr.reshape(-1, 8, 128), lbi.reshape(-1, 8, 128)
    C, G, P = w["bt_re"].shape
    nb = G // S5_GB
    bb2 = jnp.stack([bbt_re, bbt_im]).transpose(0, 2, 1, 3).reshape(2, nb, S5_GB * C, P)
    bb2 = jnp.concatenate([bb2, bb2], axis=-1)
    cc2 = jnp.stack([w["c_re"], w["c_im"]]).reshape(2, nb, S5_GB * C, P).transpose(0, 1, 3, 2)
    z, h0 = _s5_fwd(u0, bb2, cc2, a_re, a_im, w["s5_d"])
    w_glu = wts("glu", z)
    x1, vg = _mm_dual_fwd("glu_fwd", z, w_glu, x, "glu")

    def ffn_fwd(xin, layer):
        (nrm,) = _rms_fwd(f"rms_f{layer}", xin, [w["ffn_norm"][layer:layer + 1]], [BF16])
        w_in = wts(f"win{layer}", nrm)
        gu, act = _mm_dual_fwd(f"ffn_in{layer}", nrm, w_in, None, "ffn")
        w_out = wts(f"wout{layer}", act)
        xout = _mm_nn(f"ffn_out{layer}", act, w_out, res=xin)
        return xout, (nrm, gu, act, w_in, w_out)

    x2, saved0 = ffn_fwd(x1, 0)
    kvn, hb = _rms_fwd("rms_b", x2, [w["kv_norm"], w["b_norm"]], [BF16, BF16])
    w_kv, w_q, w_o = wts("wkv", hb), wts("wq", hb), wts("wo", hb)
    k, v = _mm_kv("kv_proj", kvn, w_kv)
    q = _mm_nn("q_proj", hb, w_q)
    outs, lses = [], []
    for grp, dil in enumerate(DILATIONS):
        o_g, l_g = _att_fwd(q, k, v, grp, dil)
        outs.append(o_g)
        lses.append(l_g)
    o, lse = _att_combine(outs, lses)
    wts("win1", o, prefetch=True)
    x3 = _mm_nn("o_proj", o, w_o, res=x2)
    x4, saved1 = ffn_fwd(x3, 1)
    loss_blk, dx4, dx4b, g["final_norm"] = _loss_head(x4, w["final_norm"], tgt)

    def ffn_bwd(dx, dxb, xin, saved, layer, dep):
        nrm, gu, act, w_in, w_out = saved
        dgu = _mm_nt_ffn_bwd(f"ffn_dact{layer}", dxb, w_out, gu, dep=dep)
        g_wout = _mm_tn(f"ffn_dwout{layer}", act, dxb)
        g_win = _mm_tn_pair(f"ffn_dwin{layer}", nrm, dgu)
        dn = _mm_nt_pair(f"ffn_dn{layer}", dgu, w_in, out_dtype=BF16)
        dxo, dxob, (dgn,) = _rms_bwd(f"rms_f{layer}_bwd", xin, dx, [w["ffn_norm"][layer:layer + 1]], [dn])
        tok = ready({f"win{layer}": g_win, f"wout{layer}": g_wout})
        return dxo, dxob, dgn, tok

    dx3, dx3b, dfn1, tok = ffn_bwd(dx4, dx4b, x3, saved1, 1, None)
    do = _mm_nt("o_proj_dx", [dx3b], w_o, dep=tok)
    g_wo = _mm_tn("o_proj_dw", o, dx3b)
    stats = _att_stats(lse, o, do)
    dq = lax.empty(q.shape, BF16)
    dks, dvs = [], []
    for grp, dil in enumerate(DILATIONS):
        dq, dk_g, dv_g = _att_bwd(q, k, v, do, stats, dq, grp, dil)
        dks.append(dk_g)
        dvs.append(dv_g)
    dkv = _sum_kv(dks, dvs)
    dhb = _mm_nt("q_proj_dx", [dq], w_q, out_dtype=BF16)
    g_wq = _mm_tn("q_proj_dw", hb, dq)
    dkvn = _mm_nt_pair("kv_proj_dx", dkv, w_kv, out_dtype=BF16)
    g_wkv = _mm_tn_pair("kv_proj_dw", kvn, dkv)
    dx2, dx2b, (g["kv_norm"], g["b_norm"]) = _rms_bwd(
        "rms_b_bwd", x2, dx3, [w["kv_norm"], w["b_norm"]], [dkvn, dhb])
    tok = ready({"wkv": g_wkv, "wq": g_wq, "wo": g_wo})
    dx1, dx1b, dfn0, tok = ffn_bwd(dx2, dx2b, x1, saved0, 0, tok)
    g["ffn_norm"] = jnp.concatenate([dfn0, dfn1], axis=0)

    dvg = _glu_bwd(dx1, vg, dep=tok)
    dz = _mm_nt_pair("glu_dx", dvg, w_glu)
    tok = ready({"glu": _mm_tn_pair("glu_dw", z, dvg)})
    tok = ready({}, after=tok)
    du, g["s5_d"], da_re, da_im, dbb2, dcc2 = _s5_bwd(
        u0, dz, h0, bb2, cc2, a_re, a_im, w["s5_d"], dep=tok)
    dcc = dcc2.transpose(0, 1, 3, 2).reshape(2, G, C, P)
    g["c_re"], g["c_im"] = dcc[0], -dcc[1]
    dbbt = dbb2[..., :P].reshape(2, G, C, P).transpose(0, 2, 1, 3)
    g["lam_re"], g["lam_im"], g["log_dt"], g["bt_re"], g["bt_im"] = _s5_param_bwd(
        w["lam_re"], w["lam_im"], w["log_dt"], w["bt_re"], w["bt_im"],
        da_re.reshape(G, P), da_im.reshape(G, P), dbbt[0], dbbt[1])
    tok = small_ready(g) if small_ready is not None else None
    grad_x, _, (g["a_norm"],) = _rms_bwd("rms_a_bwd", x, dx1, [w["a_norm"]], [du], dep=tok)
    return loss_blk, grad_x, g


def _coords():
    return lax.axis_index("x"), lax.axis_index("y"), lax.axis_index("c")


def _dev_index(dev):
    return 4 * dev[0] + 2 * dev[1] + dev[2]


def _shard_window(ref, axis, width, idx):
    sl = [slice(None)] * len(ref.shape)
    sl[axis] = pl.ds(pl.multiple_of(idx * width, width), width)
    return ref.at[tuple(sl)]


def _all_gather(name, shards, axes):
    na = len(shards)
    widths = [s.shape[ax] for s, ax in zip(shards, axes)]
    out_shapes = []
    for s, ax in zip(shards, axes):
        shp = list(s.shape)
        shp[ax] *= N_DEV
        out_shapes.append(jax.ShapeDtypeStruct(tuple(shp), s.dtype))

    def body(*refs):
        ins, outs = refs[:na], refs[na:2 * na]
        send_sems, recv_sems, local_sems = refs[2 * na:]
        x, y, c = _coords()
        me, sib = (x, y, c), (x, y, 1 - c)
        chips = [(1 - x, y), (x, 1 - y), (1 - x, 1 - y)]

        def blk(a, dev):
            return _shard_window(outs[a], axes[a], widths[a], _dev_index(dev))

        def copy(a, kk, block, to, src=None):
            return pltpu.make_async_remote_copy(
                src_ref=blk(a, block) if src is None else src, dst_ref=blk(a, block),
                send_sem=send_sems.at[a, kk], recv_sem=recv_sems.at[a, kk],
                device_id=to, device_id_type=MESH)

        local = [pltpu.make_async_copy(ins[a], blk(a, me), local_sems.at[a]) for a in range(na)]
        for cp in local:
            cp.start()
        sent = []
        for a in range(na):
            first = [copy(a, 0, me, sib, src=ins[a])]
            first += [copy(a, 1 + j, me, (*chip, c), src=ins[a]) for j, chip in enumerate(chips)]
            for cp in first:
                cp.start()
            sent += first
        for a in range(na):
            for j, chip in enumerate(chips):
                copy(a, 1 + j, (*chip, c), me).wait_recv()
                fwd = copy(a, 4 + j, (*chip, c), sib)
                fwd.start()
                sent.append(fwd)
        for a in range(na):
            copy(a, 0, sib, me).wait_recv()
            for j, chip in enumerate(chips):
                copy(a, 4 + j, (*chip, 1 - c), me).wait_recv()
        for cp in sent:
            cp.wait_send()
        for cp in local:
            cp.wait()

    return pl.pallas_call(
        body, out_shape=out_shapes, in_specs=[ANY] * na, out_specs=[ANY] * na,
        scratch_shapes=[pltpu.SemaphoreType.DMA((na, 7)), pltpu.SemaphoreType.DMA((na, 7)),
                        pltpu.SemaphoreType.DMA((na,))],
        name=name)(*shards)


HBM = pl.BlockSpec(memory_space=pltpu.HBM)
SEM = pl.BlockSpec(memory_space=pltpu.SEMAPHORE)
TOKEN_SPEC = pl.BlockSpec(memory_space=pltpu.VMEM)
TOKEN_SHAPE = jax.ShapeDtypeStruct((8, 128), F32)
SPLIT_PARAMS = pltpu.CompilerParams(has_side_effects=pltpu.SideEffectType.DATAFLOW_SIDE_EFFECTING)


def _hbm(x):
    return pltpu.with_memory_space_constraint(x, pltpu.HBM)


def _hbm_like(x):
    return pltpu.HBM(x.shape, x.dtype)


def _dma_sems(*shape):
    return pltpu.SemaphoreType.DMA(shape)


def _cast_and_place(name, shard, layer, axis, pos, dtype):
    rows, cols = shard.shape[-2:]
    tr = _tile(rows, 256, 16)
    nt = rows // tr
    full = (rows, cols * N_DEV) if axis == 1 else (rows * N_DEV, cols)

    def dev(p):
        return 4 * p[0] + 2 * p[1] + p[2]

    def body(pos_ref, s_ref, b_ref, l_ref):
        v = s_ref[...].astype(dtype)
        b_ref[...] = v
        l_ref[...] = v

    blk = pl.BlockSpec((tr, cols), lambda i, p: (i, 0))
    if axis == 1:
        lspec = pl.BlockSpec((tr, cols), lambda i, p: (i, dev(p)))
    else:
        lspec = pl.BlockSpec((tr, cols), lambda i, p: (dev(p) * nt + i, 0))
    return pl.pallas_call(
        body, grid_spec=pltpu.PrefetchScalarGridSpec(
            num_scalar_prefetch=1, grid=(nt,),
            in_specs=[pl.BlockSpec((None, tr, cols), lambda i, p: (layer, i, 0))], out_specs=[blk, lspec]),
        out_shape=[jax.ShapeDtypeStruct((rows, cols), dtype), jax.ShapeDtypeStruct(full, dtype)],
        compiler_params=_params(("parallel",)), name=name)(pos, shard)


def _gather_start(name, shards, lands, axes, groups):
    na, ng = len(shards), len(groups)
    widths = [s.shape[ax] for s, ax in zip(shards, axes)]

    def body(*refs):
        sh, ld = refs[:na], refs[na:2 * na]
        sems = refs[2 * na:2 * na + 3 * ng]
        token = refs[-1]
        x, y, c = _coords()
        me, sib = (x, y, c), (x, y, 1 - c)
        chips = [(1 - x, y), (x, 1 - y), (1 - x, 1 - y)]
        for gi, grp in enumerate(groups):
            send, r_d2d, r_ici = sems[3 * gi:3 * gi + 3]
            for li, a in enumerate(grp):
                dst = _shard_window(ld[a], axes[a], widths[a], _dev_index(me))
                pltpu.make_async_remote_copy(
                    src_ref=sh[a], dst_ref=dst, send_sem=send.at[4 * li], recv_sem=r_d2d.at[li],
                    device_id=sib, device_id_type=MESH).start()
                for j, chip in enumerate(chips):
                    pltpu.make_async_remote_copy(
                        src_ref=sh[a], dst_ref=dst, send_sem=send.at[4 * li + 1 + j], recv_sem=r_ici.at[3 * li + j],
                        device_id=(*chip, c), device_id_type=MESH).start()
        token[...] = jnp.zeros_like(token)

    out_shape, out_specs = [], []
    for grp in groups:
        out_shape += [_dma_sems(4 * len(grp)), _dma_sems(len(grp)), _dma_sems(3 * len(grp))]
        out_specs += [SEM] * 3
    out_shape += [_hbm_like(s) for s in shards] + [_hbm_like(l) for l in lands] + [TOKEN_SHAPE]
    out_specs += [HBM] * (2 * na) + [TOKEN_SPEC]
    aliases = {a: 3 * ng + a for a in range(2 * na)}
    res = pl.pallas_call(
        body, name=name, out_shape=out_shape, in_specs=[HBM] * (2 * na),
        out_specs=out_specs, input_output_aliases=aliases, compiler_params=SPLIT_PARAMS,
    )(*[_hbm(s) for s in shards], *[_hbm(l) for l in lands])
    sems = [tuple(res[3 * gi:3 * gi + 3]) for gi in range(ng)]
    return sems, list(res[3 * ng:3 * ng + na]), list(res[3 * ng + na:3 * ng + 2 * na]), res[-1]


def _gather_forward(name, lands, axes, r_ici, after):
    n = len(lands)
    widths = [l.shape[ax] // N_DEV for l, ax in zip(lands, axes)]

    def body(*refs):
        ld, r_ici_ref = refs[:n], refs[n]
        f_send, f_recv = refs[n + 2], refs[n + 3]
        x, y, c = _coords()
        sib = (x, y, 1 - c)
        chips = [(1 - x, y), (x, 1 - y), (1 - x, 1 - y)]
        for li in range(n):
            for j, chip in enumerate(chips):
                blk = _shard_window(ld[li], axes[li], widths[li], _dev_index((*chip, c)))
                pltpu.make_async_remote_copy(
                    src_ref=blk, dst_ref=blk, send_sem=f_send.at[3 * li + j], recv_sem=r_ici_ref.at[3 * li + j],
                    device_id=(*chip, c), device_id_type=MESH).wait_recv()
                pltpu.make_async_remote_copy(
                    src_ref=blk, dst_ref=blk, send_sem=f_send.at[3 * li + j], recv_sem=f_recv.at[3 * li + j],
                    device_id=sib, device_id_type=MESH).start()

    res = pl.pallas_call(
        body, name=name, out_shape=[_dma_sems(3 * n), _dma_sems(3 * n)] + [_hbm_like(l) for l in lands],
        in_specs=[HBM] * n + [SEM, ANY], out_specs=[SEM, SEM] + [HBM] * n,
        input_output_aliases={li: 2 + li for li in range(n)}, compiler_params=SPLIT_PARAMS,
    )(*lands, r_ici, after)
    return res[0], res[1], list(res[2:])


def _gather_finish(name, shards, lands, axes, send, r_d2d, f_send, f_recv, after):
    n = len(lands)
    widths = [l.shape[ax] // N_DEV for l, ax in zip(lands, axes)]

    def body(*refs):
        sh, ld = refs[:n], refs[n:2 * n]
        send_ref, r_d2d_ref, f_send_ref, f_recv_ref = refs[2 * n:2 * n + 4]
        x, y, c = _coords()
        me, sib = (x, y, c), (x, y, 1 - c)
        chips = [(1 - x, y), (x, 1 - y), (1 - x, 1 - y)]

        def blk(li, dev):
            return _shard_window(ld[li], axes[li], widths[li], _dev_index(dev))

        for li in range(n):
            for kk in range(4):
                pltpu.make_async_remote_copy(
                    src_ref=sh[li], dst_ref=blk(li, me), send_sem=send_ref.at[4 * li + kk], recv_sem=r_d2d_ref.at[li],
                    device_id=sib, device_id_type=MESH).wait_send()
            pltpu.make_async_remote_copy(
                src_ref=blk(li, sib), dst_ref=blk(li, sib), send_sem=send_ref.at[4 * li], recv_sem=r_d2d_ref.at[li],
                device_id=sib, device_id_type=MESH).wait_recv()
            for j, chip in enumerate(chips):
                pltpu.make_async_remote_copy(
                    src_ref=blk(li, (*chip, c)), dst_ref=blk(li, (*chip, c)), send_sem=f_send_ref.at[3 * li + j],
                    recv_sem=f_recv_ref.at[3 * li + j], device_id=sib, device_id_type=MESH).wait_send()
                pltpu.make_async_remote_copy(
                    src_ref=blk(li, (*chip, 1 - c)), dst_ref=blk(li, (*chip, 1 - c)), send_sem=f_send_ref.at[3 * li + j],
                    recv_sem=f_recv_ref.at[3 * li + j], device_id=sib, device_id_type=MESH).wait_recv()

    res = pl.pallas_call(
        body, name=name, out_shape=[_hbm_like(s) for s in shards] + [_hbm_like(l) for l in lands],
        in_specs=[HBM] * (2 * n) + [SEM] * 4 + [ANY], out_specs=[HBM] * (2 * n),
        input_output_aliases={i: i for i in range(2 * n)}, compiler_params=SPLIT_PARAMS,
    )(*shards, *lands, send, r_d2d, f_send, f_recv, after)
    return list(res[n:])


def _chip_exchange_start(name, parts):
    n = len(parts)

    def body(*refs):
        src, ld = refs[:n], refs[n:2 * n]
        send, recv = refs[2 * n], refs[2 * n + 1]
        token = refs[-1]
        x, y, c = _coords()
        chips = [(1 - x, y), (x, 1 - y), (1 - x, 1 - y)]
        for li in range(n):
            for kk, chip in enumerate(chips):
                pltpu.make_async_remote_copy(
                    src_ref=src[li].at[kk], dst_ref=ld[li].at[kk], send_sem=send.at[3 * li + kk],
                    recv_sem=recv.at[3 * li + kk], device_id=(*chip, c), device_id_type=MESH).start()
        token[...] = jnp.zeros_like(token)

    lands = [lax.empty(p.shape, p.dtype) for p in parts]
    res = pl.pallas_call(
        body, name=name,
        out_shape=[_dma_sems(3 * n), _dma_sems(3 * n)] + [_hbm_like(p) for p in parts] * 2 + [TOKEN_SHAPE],
        in_specs=[HBM] * (2 * n), out_specs=[SEM, SEM] + [HBM] * (2 * n) + [TOKEN_SPEC],
        input_output_aliases={i: 2 + i for i in range(2 * n)}, compiler_params=SPLIT_PARAMS,
    )(*[_hbm(p) for p in parts], *[_hbm(l) for l in lands])
    return res[0], res[1], list(res[2:2 + n]), list(res[2 + n:2 + 2 * n]), res[-1]


def _chip_exchange_finish(name, started, after):
    counts = [len(st[2]) for st in started]
    total = sum(counts)
    ns = len(started)

    def body(*refs):
        src, ld = refs[:total], refs[total:2 * total]
        sems = refs[2 * total:2 * total + 2 * ns]
        x, y, c = _coords()
        chips = [(1 - x, y), (x, 1 - y), (1 - x, 1 - y)]
        off = 0
        for si, cnt in enumerate(counts):
            send, recv = sems[2 * si], sems[2 * si + 1]
            for li in range(cnt):
                for kk, chip in enumerate(chips):
                    cp = pltpu.make_async_remote_copy(
                        src_ref=src[off + li].at[kk], dst_ref=ld[off + li].at[kk], send_sem=send.at[3 * li + kk],
                        recv_sem=recv.at[3 * li + kk], device_id=(*chip, c), device_id_type=MESH)
                    cp.wait_send()
                    cp.wait_recv()
            off += cnt

    srcs = [p for st in started for p in st[2]]
    lands = [l for st in started for l in st[3]]
    sems = [s for st in started for s in st[:2]]
    res = pl.pallas_call(
        body, name=name, out_shape=[_hbm_like(p) for p in srcs + lands],
        in_specs=[HBM] * (2 * total) + [SEM] * (2 * ns) + [ANY], out_specs=[HBM] * (2 * total),
        input_output_aliases={i: i for i in range(2 * total)}, compiler_params=SPLIT_PARAMS,
    )(*srcs, *lands, *sems, after)
    out, off = [], total
    for cnt in counts:
        out.append(list(res[off:off + cnt]))
        off += cnt
    return out


def _pair_copies(src, ld, send, recv, axes, widths):
    x, y, c = _coords()
    chips = [(x, y), (1 - x, y), (x, 1 - y), (1 - x, 1 - y)]
    return [pltpu.make_async_remote_copy(
        src_ref=_shard_window(src[li], axes[li], widths[li], _dev_index((*chip, 1 - c))),
        dst_ref=ld[li].at[kk], send_sem=send.at[4 * li + kk], recv_sem=recv.at[4 * li + kk],
        device_id=(x, y, 1 - c), device_id_type=MESH)
        for li in range(len(src)) for kk, chip in enumerate(chips)]


def _pair_exchange_start(name, grads, axes):
    n = len(grads)
    widths = [gr.shape[ax] // N_DEV for gr, ax in zip(grads, axes)]
    lands = []
    for gr, ax, wd in zip(grads, axes, widths):
        shp = list(gr.shape)
        shp[ax] = wd
        lands.append(lax.empty((4, *shp), gr.dtype))

    def body(*refs):
        for cp in _pair_copies(refs[:n], refs[n:2 * n], refs[2 * n], refs[2 * n + 1], axes, widths):
            cp.start()
        refs[-1][...] = jnp.zeros_like(refs[-1])

    res = pl.pallas_call(
        body, name=name,
        out_shape=[_dma_sems(4 * n), _dma_sems(4 * n)] + [_hbm_like(a) for a in grads + lands] + [TOKEN_SHAPE],
        in_specs=[HBM] * (2 * n), out_specs=[SEM, SEM] + [HBM] * (2 * n) + [TOKEN_SPEC],
        input_output_aliases={i: 2 + i for i in range(2 * n)}, compiler_params=SPLIT_PARAMS,
    )(*[_hbm(a) for a in grads + lands])
    return res[0], res[1], list(res[2:2 + n]), list(res[2 + n:2 + 2 * n]), res[-1]


def _pair_exchange_finish(name, started, axes, after):
    send, recv, grads, lands, _ = started
    n = len(grads)
    widths = [gr.shape[ax] // N_DEV for gr, ax in zip(grads, axes)]

    def body(*refs):
        for cp in _pair_copies(refs[:n], refs[n:2 * n], refs[2 * n], refs[2 * n + 1], axes, widths):
            cp.wait_send()
            cp.wait_recv()

    res = pl.pallas_call(
        body, name=name, out_shape=[_hbm_like(a) for a in grads + lands],
        in_specs=[HBM] * (2 * n) + [SEM, SEM, ANY], out_specs=[HBM] * (2 * n),
        input_output_aliases={i: i for i in range(2 * n)}, compiler_params=SPLIT_PARAMS,
    )(*grads, *lands, send, recv, after)
    return list(res[:n]), list(res[n:])


def _pair_sum(name, grad, land, axis, pos):
    wd = grad.shape[axis] // N_DEV
    shard_shape = land.shape[1:]
    rows, cols = shard_shape
    tr = _tile(rows, 256, 16)
    nt = rows // tr

    def dev_of(kk, pos_ref):
        return 4 * (pos_ref[0] ^ (kk & 1)) + 2 * (pos_ref[1] ^ (kk >> 1)) + pos_ref[2]

    def gspec(kk):
        if axis == 1:
            return pl.BlockSpec((tr, wd), lambda t, p: (t, dev_of(kk, p)))
        return pl.BlockSpec((tr, cols), lambda t, p: (dev_of(kk, p) * nt + t, 0))

    def body(pos_ref, g0, g1, g2, g3, l_ref, p0_ref, pb_ref):
        p0_ref[...] = g0[...].astype(F32) + l_ref[0].astype(F32)
        for kk, g_ref in enumerate((g1, g2, g3)):
            pb_ref[kk] = (g_ref[...].astype(F32) + l_ref[kk + 1].astype(F32)).astype(BF16)

    return pl.pallas_call(
        body,
        grid_spec=pltpu.PrefetchScalarGridSpec(
            num_scalar_prefetch=1, grid=(nt,),
            in_specs=[gspec(kk) for kk in range(4)] + [pl.BlockSpec((4, tr, cols), lambda t, p: (0, t, 0))],
            out_specs=[pl.BlockSpec((tr, cols), lambda t, p: (t, 0)),
                       pl.BlockSpec((3, tr, cols), lambda t, p: (0, t, 0))]),
        out_shape=[jax.ShapeDtypeStruct(shard_shape, F32), jax.ShapeDtypeStruct((3, *shard_shape), BF16)],
        compiler_params=_params(("parallel",)), name=name)(pos, grad, grad, grad, grad, land)


def _adamw(name, parts, w, m, v, layer=None, prev=None, tile_rows=256):
    rows, cols = w.shape[-2:]
    tr = _tile(rows, tile_rows, 8)
    npart = len(parts)
    c1 = 1.0 - ADAM_B1 ** ADAM_STEP
    c2 = 1.0 - ADAM_B2 ** ADAM_STEP

    def body(*refs):
        p_refs = refs[:npart]
        w_ref, m_ref, v_ref = refs[npart:npart + 3]
        g_ref, d_ref, nm_ref, nv_ref = refs[-4:]
        g = None
        for r in p_refs:
            if len(r.shape) == 3:
                for i in range(r.shape[0]):
                    t = r[i].astype(F32)
                    g = t if g is None else g + t
            else:
                t = r[...].astype(F32)
                g = t if g is None else g + t
        nm = ADAM_B1 * m_ref[...] + (1.0 - ADAM_B1) * g
        nv = ADAM_B2 * v_ref[...] + (1.0 - ADAM_B2) * (g * g)
        g_ref[...] = g
        nm_ref[...] = nm
        nv_ref[...] = nv
        d_ref[...] = -ADAM_LR * ((nm / c1) / (jnp.sqrt(nv / c2) + ADAM_EPS) + ADAM_WD * w_ref[...])

    spec = pl.BlockSpec((tr, cols), lambda i: (i, 0))
    wspec = spec if layer is None else pl.BlockSpec((None, tr, cols), lambda i: (layer, i, 0))
    pspecs = [pl.BlockSpec((p.shape[0], tr, cols), lambda i: (0, i, 0)) if p.ndim == 3 else spec
              for p in parts]
    prev = list(prev) if prev else []
    return pl.pallas_call(
        body, grid=(rows // tr,), in_specs=pspecs + [wspec] * 3 + [ANY] * len(prev), out_specs=[wspec] * 4,
        out_shape=[jax.ShapeDtypeStruct(w.shape, F32)] * 4,
        input_output_aliases={npart + 3 + i: i for i in range(len(prev))},
        compiler_params=_params(("parallel",)), name=name)(*parts, w, m, v, *prev)


SMALL_NAMES = ("s5_lam_re", "s5_lam_im", "s5_log_dt", "s5_b_re", "s5_b_im", "s5_c_re", "s5_c_im",
               "ffn_norm", "b_norm_mix", "kv_norm", "final_norm")
SMALL_PAD = 1024


def _pack(parts):
    flat = []
    for p in parts:
        f = p.reshape(-1)
        pad = (-f.shape[0]) % SMALL_PAD
        if pad:
            f = jnp.concatenate([f, jnp.zeros((pad,), f.dtype)])
        flat.append(f)
    return jnp.concatenate(flat).reshape(-1, 128)


def _unpack(packed, shapes):
    flat = packed.reshape(-1)
    out, off = [], 0
    for shp in shapes:
        size = math.prod(shp)
        out.append(flat[off:off + size].reshape(shp))
        off += size + (-size) % SMALL_PAD
    return out


def kernel(x, s5_lam_re, s5_lam_im, s5_log_dt, s5_b_re, s5_b_im, s5_c_re, s5_c_im, s5_d, s5_w_glu, a_norm_mix, ffn_norm, ffn_w_in, ffn_w_out, b_norm_mix, attn_w_q, attn_w_o, kv_norm, w_kv, final_norm, loss_target, m_s5_lam_re, m_s5_lam_im, m_s5_log_dt, m_s5_b_re, m_s5_b_im, m_s5_c_re, m_s5_c_im, m_s5_d, m_s5_w_glu, m_a_norm_mix, m_ffn_norm, m_ffn_w_in, m_ffn_w_out, m_b_norm_mix, m_attn_w_q, m_attn_w_o, m_kv_norm, m_w_kv, m_final_norm, v_s5_lam_re, v_s5_lam_im, v_s5_log_dt, v_s5_b_re, v_s5_b_im, v_s5_c_re, v_s5_c_im, v_s5_d, v_s5_w_glu, v_a_norm_mix, v_ffn_norm, v_ffn_w_in, v_ffn_w_out, v_b_norm_mix, v_attn_w_q, v_attn_w_o, v_kv_norm, v_w_kv, v_final_norm):
    args = dict(locals())
    T, D = x.shape[1], x.shape[2]
    n_layers = ffn_w_in.shape[0]
    xi_, yi_, ci_ = _coords()
    pos = jnp.stack([xi_, yi_, ci_]).astype(jnp.int32)
    me = 4 * xi_ + 2 * yi_ + ci_

    big_names = ["glu"] + [f"win{l}" for l in range(n_layers)] + [f"wout{l}" for l in range(n_layers)] \
        + ["wkv", "wq", "wo"]
    big_shards = [(s5_w_glu, 0)] + [(ffn_w_in, l) for l in range(n_layers)] \
        + [(ffn_w_out, l) for l in range(n_layers)] + [(w_kv[None], 0), (attn_w_q, 0), (attn_w_o, 0)]
    big_axes = [1] + [1] * n_layers + [0] * n_layers + [0, 1, 0]
    big_out_names = ["s5_w_glu"] + ["ffn_w_in"] * n_layers + ["ffn_w_out"] * n_layers \
        + ["w_kv", "attn_w_q", "attn_w_o"]
    index_of = {n: i for i, n in enumerate(big_names)}
    vec_shard = jnp.concatenate([s5_d, a_norm_mix], axis=0)
    (vecs,) = _all_gather("vectors_all_gather", [vec_shard], [1])

    gather_groups = [["glu"], ["win0"], ["wout0"], ["wkv", "wq", "wo"], ["win1", "wout1"]]
    group_idx = [[index_of[n] for n in grp] for grp in gather_groups]
    group_of = {n: gi for gi, grp in enumerate(gather_groups) for n in grp}
    placed = [_cast_and_place(f"cast_place_{n}", s, l, ax, pos, BF16)
              for n, (s, l), ax in zip(big_names, big_shards, big_axes)]
    gather_sems, shards_thru, lands_thru, start_token = _gather_start(
        "weights_gather_start", [p[0] for p in placed], [p[1] for p in placed], big_axes, group_idx)
    full = {}

    forwarded = {}

    def wts(name, after, prefetch=False):
        gi = group_of[name]
        idx = group_idx[gi]
        axes = [big_axes[a] for a in idx]
        send, r_d2d, r_ici = gather_sems[gi]
        if gi not in forwarded:
            forwarded[gi] = _gather_forward(
                f"weights_gather_forward{gi}", [lands_thru[a] for a in idx], axes, r_ici, after)
        if prefetch:
            return None
        if name not in full:
            f_send, f_recv, lands = forwarded[gi]
            done = _gather_finish(f"weights_gather_finish{gi}", [shards_thru[a] for a in idx], lands, axes,
                                  send, r_d2d, f_send, f_recv, after)
            full.update(zip(gather_groups[gi], done))
        return full[name]

    exchanges, pending = [], []

    def flush(after):
        names, axes, started = pending.pop()
        tag = "_".join(names)
        grads, lands = _pair_exchange_finish(f"rs_pair_exchange_finish_{tag}", started, axes, after)
        p0s, pbs = [], []
        for n, gr, land, ax in zip(names, grads, lands, axes):
            p0, pb = _pair_sum(f"rs_pair_sum_{n}", gr, land, ax, pos)
            p0s.append(p0)
            pbs.append(pb)
        started = _chip_exchange_start(f"rs_chip_exchange_start_{tag}", pbs)
        exchanges.append((names, p0s, started))
        return started[4]

    def ready(grads, after=None):
        if not grads:
            return flush(after)
        names = list(grads)
        axes = [big_axes[index_of[n]] for n in names]
        started = _pair_exchange_start(f"rs_pair_exchange_start_{'_'.join(names)}", [grads[n] for n in names], axes)
        token = flush(started[4]) if pending else started[4]
        pending.append((names, axes, started))
        return token

    G, P, C = s5_b_re.shape[1:]
    w = dict(
        a_norm=vecs[1:2], s5_d=vecs[0:1],
        ffn_norm=ffn_norm, b_norm=b_norm_mix, kv_norm=kv_norm.reshape(1, D), final_norm=final_norm.reshape(1, D),
        lam_re=s5_lam_re[0], lam_im=s5_lam_im[0], log_dt=s5_log_dt.reshape(G, 1),
        bt_re=s5_b_re[0].transpose(2, 0, 1), bt_im=s5_b_im[0].transpose(2, 0, 1),
        c_re=s5_c_re[0], c_im=s5_c_im[0],
    )
    small = {}

    def small_ready(g):
        small_g = dict(
            s5_lam_re=g["lam_re"], s5_lam_im=g["lam_im"], s5_log_dt=g["log_dt"],
            s5_b_re=g["bt_re"].transpose(1, 2, 0), s5_b_im=g["bt_im"].transpose(1, 2, 0),
            s5_c_re=g["c_re"], s5_c_im=g["c_im"], ffn_norm=g["ffn_norm"], b_norm_mix=g["b_norm"],
            kv_norm=g["kv_norm"], final_norm=g["final_norm"])
        packed = _pack([small_g[n] for n in SMALL_NAMES])
        shard, land = _cast_and_place("place_small_grads", packed[None], 0, 0, pos, F32)
        sems, thru, lands, token = _gather_start("small_grads_gather_start", [shard], [land], [0], [[0]])
        small.update(sems=sems[0], thru=thru, lands=lands, rows=packed.shape[0])
        return token

    loss_blk, grad_x, g = _local_step(x[0], loss_target[0], w, wts, ready, dep0=start_token,
                                      small_ready=small_ready)
    loss = lax.psum(loss_blk[0, 0], ("x", "y", "c"))
    (tail,) = _all_gather("vector_grads_all_gather", [_pack([g["s5_d"], g["a_norm"]])], [0])

    out = {}

    def put(name, res, shape):
        for kind, r in zip(("grad", "delta", "new_m", "new_v"), res):
            out[f"{kind}_{name}"] = r.reshape(shape)

    updated = {}

    def update(names, p0s, recvd):
        for name, p0, rc in zip(names, p0s, recvd):
            oname = big_out_names[index_of[name]]
            w3, layer = big_shards[index_of[name]]
            updated[oname] = _adamw(f"adamw_{name}", [p0, rc], w3, args["m_" + oname].reshape(w3.shape),
                                    args["v_" + oname].reshape(w3.shape), layer=layer, prev=updated.get(oname))
        return updated[big_out_names[index_of[names[0]]]][0]

    early, last = exchanges[:-1], exchanges[-1]
    landed = _chip_exchange_finish("rs_chip_exchange_finish_early", [e[2] for e in early], grad_x)
    for (names, p0s, _), recvd in zip(early[:-1], landed[:-1]):
        marker = update(names, p0s, recvd)
    send, r_d2d, r_ici = small["sems"]
    f_send, f_recv, small_lands = _gather_forward("small_grads_gather_forward", small["lands"], [0], r_ici, marker)
    marker = update(early[-1][0], early[-1][1], landed[-1])
    (all_parts,) = _gather_finish("small_grads_gather_finish", small["thru"], small_lands, [0],
                                  send, r_d2d, f_send, f_recv, marker)
    (recvd,) = _chip_exchange_finish("rs_chip_exchange_finish_last", [last[2]], marker)
    update(last[0], last[1], recvd)
    for oname, res in updated.items():
        put(oname, res, args[oname].shape)
    all_parts = all_parts.reshape(N_DEV, small["rows"], 128)
    w_pack = _pack([args[n] for n in SMALL_NAMES])
    m_pack = _pack([args["m_" + n] for n in SMALL_NAMES])
    v_pack = _pack([args["v_" + n] for n in SMALL_NAMES])
    res = _adamw("adamw_small", [all_parts], w_pack, m_pack, v_pack, tile_rows=1024)
    shapes = [args[n].shape for n in SMALL_NAMES]
    unpacked = [_unpack(r, shapes) for r in res]
    for i, n in enumerate(SMALL_NAMES):
        put(n, [u[i] for u in unpacked], args[n].shape)
    ws = D // N_DEV
    tail = lax.dynamic_slice_in_dim(tail.reshape(N_DEV, 2, D), me * ws, ws, axis=2)
    res = _adamw("adamw_vec", [tail], vec_shard,
                 jnp.concatenate([m_s5_d, m_a_norm_mix], axis=0), jnp.concatenate([v_s5_d, v_a_norm_mix], axis=0))
    put("s5_d", [r[0:1] for r in res], s5_d.shape)
    put("a_norm_mix", [r[1:2] for r in res], a_norm_mix.shape)

    names = ("s5_lam_re", "s5_lam_im", "s5_log_dt", "s5_b_re", "s5_b_im", "s5_c_re", "s5_c_im", "s5_d",
             "s5_w_glu", "a_norm_mix", "ffn_norm", "ffn_w_in", "ffn_w_out", "b_norm_mix", "attn_w_q",
             "attn_w_o", "kv_norm", "w_kv", "final_norm")
    result = [loss, grad_x.reshape(x.shape)]
    for kind in ("grad", "delta", "new_m", "new_v"):
        result += [out[f"{kind}_{n}"] for n in names]
    return tuple(result)
```

```python
import functools
import math

import jax
import jax.numpy as jnp
from jax import lax
from jax.experimental import pallas as pl
from jax.experimental.pallas import tpu as pltpu

F32 = jnp.float32
BF16 = jnp.bfloat16

EPS = 1e-6
NEG_INF = -1e30
HEAD_DIM = 128
N_KV_HEADS = 4
DILATIONS = (1, 4, 16)
ATT_BLK = 128
S5_C = 16
S5_P = 64
S5_GB = 16
S5_CH = S5_GB * S5_C
S5_W = S5_GB * S5_P
S5_UNROLL = 4
N_DEV = 8

ADAM_LR = 0.001
ADAM_B1 = 0.9
ADAM_B2 = 0.999
ADAM_EPS = 1e-08
ADAM_WD = 0.01
ADAM_STEP = 10

VMEM_LIMIT_BYTES = 56 * 1024 * 1024
MM_TILE = 1024
MM_TILE_NARROW = 512
MM_DEPTH = 2816
MESH = pl.DeviceIdType.MESH
ANY = pl.BlockSpec(memory_space=pl.ANY)


def _tile(n, pref, align=128):
    t = (min(pref, n) // align) * align
    while t >= align:
        if n % t == 0:
            return t
        t -= align
    return n


def _params(sem):
    return pltpu.CompilerParams(dimension_semantics=sem, vmem_limit_bytes=VMEM_LIMIT_BYTES)


def _sigmoid(x):
    return 1.0 / (1.0 + jnp.exp(-x))


NN = (((1,), (0,)), ((), ()))
NT = (((1,), (1,)), ((), ()))
TN = (((0,), (0,)), ((), ()))


def _dot(a, b, dims=NN):
    return lax.dot_general(a, b, dims, preferred_element_type=F32)


def _matmul(name, grid, ins, in_specs, products, dims, out_shapes, out_specs, acc_shapes, epilogue):
    n_in, n_out, nk = len(ins), len(out_shapes), grid[2]

    def body(*refs):
        in_refs = refs[:n_in]
        out_refs = refs[n_in:n_in + n_out]
        acc_refs = refs[n_in + n_out:]

        def prods():
            vals = [None] * len(acc_shapes)
            for ai, bi, ci in products:
                d = _dot(in_refs[ai][...].astype(BF16), in_refs[bi][...].astype(BF16), dims)
                vals[ci] = d if vals[ci] is None else vals[ci] + d
            return vals

        if nk == 1:
            epilogue(in_refs, out_refs, prods())
        else:
            k = pl.program_id(2)

            @pl.when(k == 0)
            def _():
                for a in acc_refs:
                    a[...] = jnp.zeros_like(a)

            for a, v in zip(acc_refs, prods()):
                a[...] += v

            @pl.when(k == nk - 1)
            def _():
                epilogue(in_refs, out_refs, [a[...] for a in acc_refs])

    scratch = [] if nk == 1 else [pltpu.VMEM(s, F32) for s in acc_shapes]
    return pl.pallas_call(
        body, grid=grid, in_specs=in_specs, out_specs=out_specs, out_shape=out_shapes,
        scratch_shapes=scratch, compiler_params=_params(("parallel", "parallel", "arbitrary")),
        name=name)(*ins)


def _mm_dual_fwd(name, a, w, res, kind):
    T, K = a.shape
    N = w.shape[1] // 2
    tm, tn = _tile(T, MM_TILE), _tile(N, MM_TILE_NARROW)
    nj = N // tn
    grid = (T // tm, nj, 1)
    ins = [a, w, w]
    in_specs = [pl.BlockSpec((tm, K), lambda i, j, k: (i, 0)),
                pl.BlockSpec((K, tn), lambda i, j, k: (0, j)),
                pl.BlockSpec((K, tn), lambda i, j, k: (0, j + nj))]
    pair_spec = pl.BlockSpec((2, tm, tn), lambda i, j, k: (0, i, j))
    tile_spec = pl.BlockSpec((tm, tn), lambda i, j, k: (i, j))
    if kind == "glu":
        ins.append(res)
        in_specs.append(tile_spec)

        def epilogue(in_refs, out_refs, accs):
            val, gate = accs
            s = _sigmoid(gate)
            out_refs[0][...] = in_refs[3][...] + val * s
            out_refs[1][0] = s.astype(BF16)
            out_refs[1][1] = (val * s * (1.0 - s)).astype(BF16)

        out_shapes = [jax.ShapeDtypeStruct((T, N), F32), jax.ShapeDtypeStruct((2, T, N), BF16)]
        out_specs = [tile_spec, pair_spec]
    else:
        def epilogue(in_refs, out_refs, accs):
            g, u = accs
            s = _sigmoid(g)
            silu = g * s
            out_refs[0][0] = (u * (s * (1.0 + g * (1.0 - s)))).astype(BF16)
            out_refs[0][1] = silu.astype(BF16)
            out_refs[1][...] = (silu * u).astype(BF16)

        out_shapes = [jax.ShapeDtypeStruct((2, T, N), BF16), jax.ShapeDtypeStruct((T, N), BF16)]
        out_specs = [pair_spec, tile_spec]
    return _matmul(name, grid, ins, in_specs, [(0, 1, 0), (0, 2, 1)], NN, out_shapes, out_specs,
                   [(tm, tn), (tm, tn)], epilogue)


def _mm_kv(name, a, w):
    T, K = a.shape
    N = w.shape[1] // 2
    tm, tn = _tile(T, MM_TILE), _tile(N, MM_TILE_NARROW)
    nj = N // tn
    tile_spec = pl.BlockSpec((tm, tn), lambda i, j, k: (i, j))

    def epilogue(in_refs, out_refs, accs):
        out_refs[0][...] = accs[0]
        out_refs[1][...] = accs[1]

    return _matmul(name, (T // tm, nj, 1), [a, w, w],
                   [pl.BlockSpec((tm, K), lambda i, j, k: (i, 0)),
                    pl.BlockSpec((K, tn), lambda i, j, k: (0, j)),
                    pl.BlockSpec((K, tn), lambda i, j, k: (0, j + nj))],
                   [(0, 1, 0), (0, 2, 1)], NN,
                   [jax.ShapeDtypeStruct((T, N), F32)] * 2, [tile_spec, tile_spec],
                   [(tm, tn), (tm, tn)], epilogue)


def _mm_nn(name, a, w, res=None, out_dtype=F32):
    T, K = a.shape
    N = w.shape[1]
    tk = K if K <= 2 * MM_DEPTH else _tile(K, MM_DEPTH)
    tm, tn = _tile(T, MM_TILE), _tile(N, MM_TILE if K <= MM_DEPTH else MM_TILE_NARROW)
    grid = (T // tm, N // tn, K // tk)
    tile_spec = pl.BlockSpec((tm, tn), lambda i, j, k: (i, j))
    ins = [a, w]
    in_specs = [pl.BlockSpec((tm, tk), lambda i, j, k: (i, k)),
                pl.BlockSpec((tk, tn), lambda i, j, k: (k, j))]
    if res is not None:
        ins.append(res)
        in_specs.append(tile_spec)

    def epilogue(in_refs, out_refs, accs):
        v = accs[0]
        if res is not None:
            v = v + in_refs[2][...]
        out_refs[0][...] = v.astype(out_dtype)

    return _matmul(name, grid, ins, in_specs, [(0, 1, 0)], NN,
                   [jax.ShapeDtypeStruct((T, N), out_dtype)], [tile_spec], [(tm, tn)], epilogue)[0]


def _dep_operand(ins, in_specs, dep):
    if dep is not None:
        ins.append(dep)
        in_specs.append(pl.BlockSpec((8, 128), lambda *_: (0, 0)))


def _mm_nt(name, a_list, w, out_dtype=F32, dep=None):
    T, Np = a_list[0].shape
    Ko = w.shape[0]
    n_parts = len(a_list)
    wide_a = a_list[0].dtype != BF16
    tm, tn, tk = _tile(T, MM_TILE_NARROW if wide_a else MM_TILE), _tile(Ko, MM_TILE), _tile(Np, MM_DEPTH)
    nkp = Np // tk
    grid = (T // tm, Ko // tn, nkp)
    ins = list(a_list) + [w] * n_parts
    in_specs = [pl.BlockSpec((tm, tk), lambda i, j, k: (i, k)) for _ in a_list]
    in_specs += [pl.BlockSpec((tn, tk), functools.partial(lambda i, j, k, p: (j, p * nkp + k), p=p))
                 for p in range(n_parts)]
    products = [(p, n_parts + p, 0) for p in range(n_parts)]
    _dep_operand(ins, in_specs, dep)

    def epilogue(in_refs, out_refs, accs):
        out_refs[0][...] = accs[0].astype(out_dtype)

    return _matmul(name, grid, ins, in_specs, products, NT,
                   [jax.ShapeDtypeStruct((T, Ko), out_dtype)],
                   [pl.BlockSpec((tm, tn), lambda i, j, k: (i, j))], [(tm, tn)], epilogue)[0]


def _mm_nt_pair(name, a3, w, out_dtype=F32):
    _, T, N = a3.shape
    Ko = w.shape[0]
    tm, tn, tk = _tile(T, MM_TILE), _tile(Ko, MM_TILE), _tile(N, MM_DEPTH)
    nkh = N // tk
    grid = (T // tm, Ko // tn, 2 * nkh)

    def epilogue(in_refs, out_refs, accs):
        out_refs[0][...] = accs[0].astype(out_dtype)

    return _matmul(name, grid, [a3, w],
                   [pl.BlockSpec((None, tm, tk), lambda i, j, k: (k // nkh, i, k % nkh)),
                    pl.BlockSpec((tn, tk), lambda i, j, k: (j, k))],
                   [(0, 1, 0)], NT, [jax.ShapeDtypeStruct((T, Ko), out_dtype)],
                   [pl.BlockSpec((tm, tn), lambda i, j, k: (i, j))], [(tm, tn)], epilogue)[0]


def _mm_nt_ffn_bwd(name, dx, w_out, gu, dep=None):
    T, D = dx.shape
    Fh = w_out.shape[0]
    tm, tn = _tile(T, MM_TILE), _tile(Fh, MM_TILE_NARROW)
    n_dep = 0 if dep is None else 1

    def body(dx_ref, w_ref, gu_ref, *rest):
        out_ref = rest[n_dep]
        da = _dot(dx_ref[...], w_ref[...], NT)
        out_ref[0] = (da * gu_ref[0].astype(F32)).astype(BF16)
        out_ref[1] = (da * gu_ref[1].astype(F32)).astype(BF16)

    pair_spec = pl.BlockSpec((2, tm, tn), lambda i, j: (0, i, j))
    ins = [dx, w_out, gu]
    in_specs = [pl.BlockSpec((tm, D), lambda i, j: (i, 0)),
                pl.BlockSpec((tn, D), lambda i, j: (j, 0)),
                pair_spec]
    _dep_operand(ins, in_specs, dep)
    return pl.pallas_call(
        body, grid=(T // tm, Fh // tn), in_specs=in_specs, out_specs=pair_spec,
        out_shape=jax.ShapeDtypeStruct((2, T, Fh), BF16),
        compiler_params=_params(("parallel", "parallel")), name=name)(*ins)


def _mm_tn(name, a, d):
    T, Ko = a.shape
    N = d.shape[1]
    to, tn, tk = _tile(Ko, MM_TILE_NARROW), _tile(N, MM_TILE if d.dtype == BF16 else MM_TILE_NARROW), T
    grid = (Ko // to, N // tn, T // tk)

    def epilogue(in_refs, out_refs, accs):
        out_refs[0][...] = accs[0].astype(BF16)

    return _matmul(name, grid, [a, d],
                   [pl.BlockSpec((tk, to), lambda i, j, k: (k, i)),
                    pl.BlockSpec((tk, tn), lambda i, j, k: (k, j))],
                   [(0, 1, 0)], TN, [jax.ShapeDtypeStruct((Ko, N), BF16)],
                   [pl.BlockSpec((to, tn), lambda i, j, k: (i, j))], [(to, tn)], epilogue)[0]


def _mm_tn_pair(name, a, d3):
    T, Ko = a.shape
    N = d3.shape[2]
    to, tn, tk = _tile(Ko, MM_TILE), _tile(N, MM_TILE_NARROW), T
    njh = N // tn
    grid = (Ko // to, 2 * njh, T // tk)

    def epilogue(in_refs, out_refs, accs):
        out_refs[0][...] = accs[0].astype(BF16)

    return _matmul(name, grid, [a, d3],
                   [pl.BlockSpec((tk, to), lambda i, j, k: (k, i)),
                    pl.BlockSpec((None, tk, tn), lambda i, j, k: (j // njh, k, j % njh))],
                   [(0, 1, 0)], TN, [jax.ShapeDtypeStruct((Ko, 2 * N), BF16)],
                   [pl.BlockSpec((to, tn), lambda i, j, k: (i, j))], [(to, tn)], epilogue)[0]


def _rms_fwd(name, x, gains, dtypes, dep=None):
    T, D = x.shape
    n = len(gains)
    tr = _tile(T, 512, 8)
    n_dep = 0 if dep is None else 1

    def body(x_ref, *refs):
        xv = x_ref[...]
        xr = xv * lax.rsqrt(jnp.mean(xv * xv, axis=-1, keepdims=True) + EPS)
        for g_ref, o_ref in zip(refs[:n], refs[n + n_dep:]):
            o_ref[...] = (xr * g_ref[...]).astype(o_ref.dtype)

    row = pl.BlockSpec((tr, D), lambda i: (i, 0))
    vec = pl.BlockSpec((1, D), lambda i: (0, 0))
    ins, in_specs = [x, *gains], [row] + [vec] * n
    _dep_operand(ins, in_specs, dep)
    return pl.pallas_call(
        body, grid=(T // tr,), in_specs=in_specs, out_specs=[row] * n,
        out_shape=[jax.ShapeDtypeStruct((T, D), dt) for dt in dtypes],
        compiler_params=_params(("parallel",)), name=name)(*ins)


def _rms_bwd(name, x, dres, gains, dhs, dep=None):
    T, D = x.shape
    n = len(gains)
    tr = _tile(T, 256, 8)
    n_dep = 0 if dep is None else 1

    def body(x_ref, dres_ref, *refs):
        g_refs, dh_refs = refs[:n], refs[n:2 * n]
        dx_ref, dxb_ref = refs[2 * n + n_dep], refs[2 * n + n_dep + 1]
        dg_refs = refs[2 * n + n_dep + 2:]
        xv = x_ref[...]
        r = lax.rsqrt(jnp.mean(xv * xv, axis=-1, keepdims=True) + EPS)
        xr = xv * r
        w = None
        for g_ref, dh_ref, dg_ref in zip(g_refs, dh_refs, dg_refs):
            dh = dh_ref[...].astype(F32)

            @pl.when(pl.program_id(0) == 0)
            def _():
                dg_ref[...] = jnp.zeros_like(dg_ref)

            dg_ref[...] += jnp.sum(dh * xr, axis=0, keepdims=True)
            wi = dh * g_ref[...]
            w = wi if w is None else w + wi
        dx = dres_ref[...] + r * (w - xr * jnp.mean(w * xr, axis=-1, keepdims=True))
        dx_ref[...] = dx
        dxb_ref[...] = dx.astype(BF16)

    row = pl.BlockSpec((tr, D), lambda i: (i, 0))
    vec = pl.BlockSpec((1, D), lambda i: (0, 0))
    ins, in_specs = [x, dres, *gains, *dhs], [row, row] + [vec] * n + [row] * n
    _dep_operand(ins, in_specs, dep)
    outs = pl.pallas_call(
        body, grid=(T // tr,), in_specs=in_specs,
        out_specs=[row, row] + [vec] * n,
        out_shape=[jax.ShapeDtypeStruct((T, D), F32), jax.ShapeDtypeStruct((T, D), BF16)]
        + [jax.ShapeDtypeStruct((1, D), F32)] * n,
        compiler_params=_params(("arbitrary",)), name=name)(*ins)
    return outs[0], outs[1], outs[2:]


def _loss_head(x, gain, target):
    T, D = x.shape
    tr = _tile(T, 256, 8)

    def body(x_ref, g_ref, t_ref, loss_ref, dx_ref, dxb_ref, dg_ref):
        @pl.when(pl.program_id(0) == 0)
        def _():
            loss_ref[...] = jnp.zeros_like(loss_ref)
            dg_ref[...] = jnp.zeros_like(dg_ref)

        xv = x_ref[...]
        r = lax.rsqrt(jnp.mean(xv * xv, axis=-1, keepdims=True) + EPS)
        xr = xv * r
        err = xr * g_ref[...] - t_ref[...]
        part = jnp.sum(jnp.sum(err * err, axis=-1, keepdims=True), axis=0, keepdims=True) * (0.5 / D)
        loss_ref[...] += jnp.broadcast_to(part, loss_ref.shape)
        dy = err * (1.0 / D)
        dg_ref[...] += jnp.sum(dy * xr, axis=0, keepdims=True)
        w = dy * g_ref[...]
        dx = r * (w - xr * jnp.mean(w * xr, axis=-1, keepdims=True))
        dx_ref[...] = dx
        dxb_ref[...] = dx.astype(BF16)

    row = pl.BlockSpec((tr, D), lambda i: (i, 0))
    vec = pl.BlockSpec((1, D), lambda i: (0, 0))
    return pl.pallas_call(
        body, grid=(T // tr,), in_specs=[row, vec, row],
        out_specs=[pl.BlockSpec((8, 128), lambda i: (0, 0)), row, row, vec],
        out_shape=[jax.ShapeDtypeStruct((8, 128), F32), jax.ShapeDtypeStruct((T, D), F32),
                   jax.ShapeDtypeStruct((T, D), BF16), jax.ShapeDtypeStruct((1, D), F32)],
        compiler_params=_params(("arbitrary",)), name="loss_head")(x, gain, target)


def _glu_bwd(dmix, vg, dep=None):
    T, N = dmix.shape
    tr, tc = _tile(T, 512, 8), _tile(N, 1024)
    n_dep = 0 if dep is None else 1

    def body(d_ref, vg_ref, *refs):
        o_ref = refs[n_dep]
        d = d_ref[...]
        o_ref[0] = (d * vg_ref[0].astype(F32)).astype(BF16)
        o_ref[1] = (d * vg_ref[1].astype(F32)).astype(BF16)

    pair = pl.BlockSpec((2, tr, tc), lambda i, j: (0, i, j))
    ins, in_specs = [dmix, vg], [pl.BlockSpec((tr, tc), lambda i, j: (i, j)), pair]
    _dep_operand(ins, in_specs, dep)
    return pl.pallas_call(
        body, grid=(T // tr, N // tc), in_specs=in_specs,
        out_specs=pair, out_shape=jax.ShapeDtypeStruct((2, T, N), BF16),
        compiler_params=_params(("parallel", "parallel")), name="glu_bwd")(*ins)


def _to_state_tiles(x_ref, s, val):
    tc = val.shape[0]
    for j in range(S5_W // 128):
        x_ref[s, pl.ds(j, tc, stride=8), :] = val[:, 128 * j:128 * (j + 1)]


def _from_state_tiles(x_ref, s, tc):
    return jnp.concatenate([x_ref[s, pl.ds(j, tc, stride=8), :] for j in range(S5_W // 128)], axis=1)


def _s5_scan_fwd(xr_ref, xi_ref, ar_ref, ai_ref, cr_ref, ci_ref, tc, nblk):
    a = [(ar_ref[s], ai_ref[s]) for s in range(nblk)]

    def step(i, carry):
        carry = list(carry)
        for uu in range(S5_UNROLL):
            r0 = pl.multiple_of((i * S5_UNROLL + uu) * 8, 8)
            for s in range(nblk):
                cr, ci = carry[2 * s], carry[2 * s + 1]
                a_r, a_i = a[s]
                xr = a_r * cr - a_i * ci + xr_ref[s, pl.ds(r0, 8), :]
                xi = a_r * ci + a_i * cr + xi_ref[s, pl.ds(r0, 8), :]
                xr_ref[s, pl.ds(r0, 8), :] = xr
                xi_ref[s, pl.ds(r0, 8), :] = xi
                carry[2 * s], carry[2 * s + 1] = xr, xi
        return tuple(carry)

    init = []
    for s in range(nblk):
        init += [cr_ref[s], ci_ref[s]]
    out = lax.fori_loop(0, tc // S5_UNROLL, step, tuple(init))
    for s in range(nblk):
        cr_ref[s] = out[2 * s]
        ci_ref[s] = out[2 * s + 1]


def _s5_scan_bwd(lr_ref, li_ref, xr_ref, xi_ref, h_ref, ar_ref, ai_ref, cr_ref, ci_ref,
                 accr_ref, acci_ref, tc, nblk):
    a = [(ar_ref[s], ai_ref[s]) for s in range(nblk)]

    def one(s, r0, prev_r, prev_i, st):
        c_r, c_i, d_r, d_i = st
        a_r, a_i = a[s]
        l_r = lr_ref[s, pl.ds(r0, 8), :] + a_r * c_r + a_i * c_i
        l_i = li_ref[s, pl.ds(r0, 8), :] + a_r * c_i - a_i * c_r
        lr_ref[s, pl.ds(r0, 8), :] = l_r
        li_ref[s, pl.ds(r0, 8), :] = l_i
        return [l_r, l_i, d_r + l_r * prev_r + l_i * prev_i, d_i - l_r * prev_i + l_i * prev_r]

    def step(i, carry):
        carry = list(carry)
        for uu in range(S5_UNROLL):
            t = tc - 1 - (i * S5_UNROLL + uu)
            r0 = pl.multiple_of(t * 8, 8)
            p0 = pl.multiple_of((t - 1) * 8, 8)
            for s in range(nblk):
                carry[4 * s:4 * s + 4] = one(s, r0, xr_ref[s, pl.ds(p0, 8), :], xi_ref[s, pl.ds(p0, 8), :],
                                             carry[4 * s:4 * s + 4])
        return tuple(carry)

    init = []
    for s in range(nblk):
        init += [cr_ref[s], ci_ref[s], accr_ref[s], acci_ref[s]]
    carry = list(lax.fori_loop(0, tc // S5_UNROLL - 1, step, tuple(init)))
    for t in range(S5_UNROLL - 1, -1, -1):
        for s in range(nblk):
            if t > 0:
                prev_r, prev_i = xr_ref[s, 8 * (t - 1):8 * t, :], xi_ref[s, 8 * (t - 1):8 * t, :]
            else:
                prev_r, prev_i = h_ref[0, s], h_ref[1, s]
            carry[4 * s:4 * s + 4] = one(s, 8 * t, prev_r, prev_i, carry[4 * s:4 * s + 4])
    for s in range(nblk):
        cr_ref[s], ci_ref[s], accr_ref[s], acci_ref[s] = carry[4 * s:4 * s + 4]


def _gelu(y):
    k = math.sqrt(2.0 / math.pi)
    return 0.5 * y * (1.0 + jnp.tanh(k * (y + 0.044715 * (y * y * y))))


def _gelu_grad(y):
    k = math.sqrt(2.0 / math.pi)
    t = jnp.tanh(k * (y + 0.044715 * (y * y * y)))
    return 0.5 * (1.0 + t) + 0.5 * y * (1.0 - t * t) * (k * (1.0 + 3.0 * 0.044715 * (y * y)))


def _s5_specs(tc, nch, sbk, rev):
    def ch(c):
        return nch - 1 - c if rev else c

    return dict(
        act=pl.BlockSpec((tc, sbk * S5_CH), lambda i, c: (ch(c), i)),
        bb=pl.BlockSpec((2, sbk, S5_CH, 128), lambda i, c: (0, i, 0, 0)),
        cc=pl.BlockSpec((2, sbk, S5_P, S5_CH), lambda i, c: (0, i, 0, 0)),
        a=pl.BlockSpec((sbk, 8, 128), lambda i, c: (i, 0, 0)),
        d=pl.BlockSpec((1, sbk * S5_CH), lambda i, c: (0, i)),
        h=pl.BlockSpec((None, 2, sbk, 8, 128), lambda i, c: (ch(c), 0, i, 0, 0)),
    )


def _s5_blocks(nb, pref):
    return max(b for b in range(1, pref + 1) if nb % b == 0)


def _s5_group_masks():
    rb = lax.broadcasted_iota(jnp.int32, (S5_CH, S5_W), 0) // S5_C
    qb = lax.broadcasted_iota(jnp.int32, (S5_CH, S5_W), 1) // S5_P
    qc = lax.broadcasted_iota(jnp.int32, (S5_W, S5_CH), 0) // S5_P
    rc = lax.broadcasted_iota(jnp.int32, (S5_W, S5_CH), 1) // S5_C
    return rb == qb, qc == rc


def _s5_expand(bb_ref, cc_ref, bbd, ccd, sbk):
    mask_b, mask_c = _s5_group_masks()
    for k in range(2):
        for s in range(sbk):
            bbd[k, s] = jnp.where(mask_b, jnp.tile(bb_ref[k, s], (1, S5_W // 128)), 0).astype(BF16)
            ccd[k, s] = jnp.where(mask_c, jnp.tile(cc_ref[k, s], (S5_GB, 1)), 0).astype(BF16)


def _s5_fwd(u, bb2, cc2, a_re, a_im, d_skip):
    T, D = u.shape
    nb = D // S5_CH
    sbk = _s5_blocks(nb, 4)
    tc = _tile(T, 512, 8)
    nch = T // tc
    sp = _s5_specs(tc, nch, sbk, False)

    def body(u_ref, bb_ref, cc_ref, ar_ref, ai_ref, d_ref, z_ref, h_ref, xr, xi, cr, ci, bbd, ccd):
        @pl.when(pl.program_id(1) == 0)
        def _():
            cr[...] = jnp.zeros_like(cr)
            ci[...] = jnp.zeros_like(ci)
            _s5_expand(bb_ref, cc_ref, bbd, ccd, sbk)

        h_ref[0] = cr[...]
        h_ref[1] = ci[...]
        for s in range(sbk):
            ub = u_ref[:, s * S5_CH:(s + 1) * S5_CH].astype(BF16)
            _to_state_tiles(xr, s, _dot(ub, bbd[0, s]))
            _to_state_tiles(xi, s, _dot(ub, bbd[1, s]))
        _s5_scan_fwd(xr, xi, ar_ref, ai_ref, cr, ci, tc, sbk)
        for s in range(sbk):
            cols = slice(s * S5_CH, (s + 1) * S5_CH)
            y = (_dot(_from_state_tiles(xr, s, tc).astype(BF16), ccd[0, s])
                 - _dot(_from_state_tiles(xi, s, tc).astype(BF16), ccd[1, s])
                 + d_ref[:, cols] * u_ref[:, cols])
            z_ref[:, cols] = _gelu(y).astype(BF16)

    tiles = pltpu.VMEM((sbk, tc * 8, 128), F32)
    carry = pltpu.VMEM((sbk, 8, 128), F32)
    return pl.pallas_call(
        body, grid=(nb // sbk, nch),
        in_specs=[sp["act"], sp["bb"], sp["cc"], sp["a"], sp["a"], sp["d"]],
        out_specs=[sp["act"], sp["h"]],
        out_shape=[jax.ShapeDtypeStruct((T, D), BF16), jax.ShapeDtypeStruct((nch, 2, nb, 8, 128), F32)],
        scratch_shapes=[tiles, tiles, carry, carry, pltpu.VMEM((2, sbk, S5_CH, S5_W), BF16),
                        pltpu.VMEM((2, sbk, S5_W, S5_CH), BF16)],
        compiler_params=_params(("parallel", "arbitrary")), name="s5_fwd",
    )(u, bb2, cc2, a_re, a_im, d_skip)


def _s5_bwd(u, dz, h0, bb2, cc2, a_re, a_im, d_skip, dep=None):
    T, D = u.shape
    nb = D // S5_CH
    sbk = _s5_blocks(nb, 2)
    tc = _tile(T, 512, 8)
    nch = T // tc
    sp = _s5_specs(tc, nch, sbk, True)

    n_dep = 0 if dep is None else 1

    def body(u_ref, dz_ref, h_ref, bb_ref, cc_ref, ar_ref, ai_ref, d_ref, *rest):
        (du_ref, dd_ref, dar_ref, dai_ref, dbb_ref, dcc_ref,
         xr, xi, lr, li, fr, fi, br, bi, accr, acci, bbd, ccd, dbbd, dccd) = rest[n_dep:]
        c = pl.program_id(1)

        @pl.when(c == 0)
        def _():
            for ref in (br, bi, accr, acci, dd_ref, dbbd, dccd):
                ref[...] = jnp.zeros_like(ref)
            _s5_expand(bb_ref, cc_ref, bbd, ccd, sbk)

        for s in range(sbk):
            ub = u_ref[:, s * S5_CH:(s + 1) * S5_CH].astype(BF16)
            _to_state_tiles(xr, s, _dot(ub, bbd[0, s]))
            _to_state_tiles(xi, s, _dot(ub, bbd[1, s]))
        fr[...] = h_ref[0]
        fi[...] = h_ref[1]
        _s5_scan_fwd(xr, xi, ar_ref, ai_ref, fr, fi, tc, sbk)
        for s in range(sbk):
            cols = slice(s * S5_CH, (s + 1) * S5_CH)
            uv = u_ref[:, cols]
            xrb = _from_state_tiles(xr, s, tc).astype(BF16)
            xib = _from_state_tiles(xi, s, tc).astype(BF16)
            dsk = d_ref[:, cols]
            y = _dot(xrb, ccd[0, s]) - _dot(xib, ccd[1, s]) + dsk * uv
            dy = dz_ref[:, cols] * _gelu_grad(y)
            dd_ref[:, cols] += jnp.sum(dy * uv, axis=0, keepdims=True)
            dyb = dy.astype(BF16)
            dccd[0, s] += _dot(xrb, dyb, TN)
            dccd[1, s] += _dot(xib, dyb, TN)
            _to_state_tiles(lr, s, _dot(dyb, ccd[0, s], NT))
            _to_state_tiles(li, s, -_dot(dyb, ccd[1, s], NT))
            du_ref[:, cols] = dy * dsk
        _s5_scan_bwd(lr, li, xr, xi, h_ref, ar_ref, ai_ref, br, bi, accr, acci, tc, sbk)
        for s in range(sbk):
            cols = slice(s * S5_CH, (s + 1) * S5_CH)
            ub = u_ref[:, cols].astype(BF16)
            lrb = _from_state_tiles(lr, s, tc).astype(BF16)
            lib = _from_state_tiles(li, s, tc).astype(BF16)
            dbbd[0, s] += _dot(ub, lrb, TN)
            dbbd[1, s] += _dot(ub, lib, TN)
            du_ref[:, cols] += _dot(lrb, bbd[0, s], NT) + _dot(lib, bbd[1, s], NT)

        @pl.when(c == nch - 1)
        def _():
            dar_ref[...] = accr[...]
            dai_ref[...] = acci[...]
            mask_b, mask_c = _s5_group_masks()
            for k in range(2):
                for s in range(sbk):
                    mb = jnp.where(mask_b, dbbd[k, s], 0.0)
                    fold = functools.reduce(
                        lambda a, b: a + b, [mb[:, 128 * j:128 * (j + 1)] for j in range(S5_W // 128)])
                    dbb_ref[k, s] = fold + pltpu.roll(fold, S5_P, 1)
                    mc = jnp.where(mask_c, dccd[k, s], 0.0)
                    dcc_ref[k, s] = functools.reduce(
                        lambda a, b: a + b, [mc[S5_P * j:S5_P * (j + 1), :] for j in range(S5_GB)])

    tiles = pltpu.VMEM((sbk, tc * 8, 128), F32)
    carry = pltpu.VMEM((sbk, 8, 128), F32)
    ins = [u, dz, h0, bb2, cc2, a_re, a_im, d_skip]
    in_specs = [sp["act"], sp["act"], sp["h"], sp["bb"], sp["cc"], sp["a"], sp["a"], sp["d"]]
    _dep_operand(ins, in_specs, dep)
    return pl.pallas_call(
        body, grid=(nb // sbk, nch),
        in_specs=in_specs,
        out_specs=[sp["act"], sp["d"], sp["a"], sp["a"], sp["bb"], sp["cc"]],
        out_shape=[jax.ShapeDtypeStruct((T, D), F32), jax.ShapeDtypeStruct((1, D), F32),
                   jax.ShapeDtypeStruct((nb, 8, 128), F32), jax.ShapeDtypeStruct((nb, 8, 128), F32),
                   jax.ShapeDtypeStruct((2, nb, S5_CH, 128), F32), jax.ShapeDtypeStruct((2, nb, S5_P, S5_CH), F32)],
        scratch_shapes=[tiles, tiles, tiles, tiles, carry, carry, carry, carry, carry, carry,
                        pltpu.VMEM((2, sbk, S5_CH, S5_W), BF16), pltpu.VMEM((2, sbk, S5_W, S5_CH), BF16),
                        pltpu.VMEM((2, sbk, S5_CH, S5_W), F32), pltpu.VMEM((2, sbk, S5_W, S5_CH), F32)],
        compiler_params=_params(("parallel", "arbitrary")), name="s5_bwd",
    )(*ins)


def _s5_disc(lr, li, ldt):
    dt = jnp.exp(ldt)
    mag = jnp.exp(lr * dt)
    ang = li * dt
    cs, sn = jnp.cos(ang), jnp.sin(ang)
    lbr, lbi = mag * cs, mag * sn
    nr = lbr - 1.0
    den = lr * lr + li * li
    f_re = (nr * lr + lbi * li) / den
    f_im = (lbi * lr - nr * li) / den
    return dt, mag, cs, sn, lbr, lbi, nr, den, f_re, f_im


def _s5_param_fwd(lr, li, ldt, bt_re, bt_im):
    c, g, p = bt_re.shape

    def body(lr_ref, li_ref, ldt_ref, br_ref, bi_ref, lbr_ref, lbi_ref, bbr_ref, bbi_ref):
        _, _, _, _, lbr, lbi, _, _, f_re, f_im = _s5_disc(lr_ref[...], li_ref[...], ldt_ref[...])
        lbr_ref[...] = lbr
        lbi_ref[...] = lbi
        for ch in range(c):
            b_r, b_i = br_ref[ch], bi_ref[ch]
            bbr_ref[ch] = f_re * b_r - f_im * b_i
            bbi_ref[ch] = f_re * b_i + f_im * b_r

    gp = jax.ShapeDtypeStruct((g, p), F32)
    cgp = jax.ShapeDtypeStruct((c, g, p), F32)
    return pl.pallas_call(body, out_shape=[gp, gp, cgp, cgp], name="s5_param_fwd")(lr, li, ldt, bt_re, bt_im)


def _s5_param_bwd(lr, li, ldt, bt_re, bt_im, dlbr, dlbi, dbbr, dbbi):
    c, g, p = bt_re.shape

    def body(lr_ref, li_ref, ldt_ref, br_ref, bi_ref, dlbr_ref, dlbi_ref, dbbr_ref, dbbi_ref,
             dlr_ref, dli_ref, dldt_ref, dbr_ref, dbi_ref):
        l_r, l_i = lr_ref[...], li_ref[...]
        dt, mag, cs, sn, lbr, lbi, nr, den, f_re, f_im = _s5_disc(l_r, l_i, ldt_ref[...])
        dfr = jnp.zeros_like(l_r)
        dfi = jnp.zeros_like(l_r)
        for ch in range(c):
            b_r, b_i = br_ref[ch], bi_ref[ch]
            g_r, g_i = dbbr_ref[ch], dbbi_ref[ch]
            dbr_ref[ch] = f_re * g_r + f_im * g_i
            dbi_ref[ch] = f_re * g_i - f_im * g_r
            dfr = dfr + g_r * b_r + g_i * b_i
            dfi = dfi + g_i * b_r - g_r * b_i
        inv = 1.0 / den
        d_nr = (dfr * l_r - dfi * l_i) * inv
        d_lbi = (dfr * l_i + dfi * l_r) * inv + dlbi_ref[...]
        d_lbr = d_nr + dlbr_ref[...]
        d_den = -(dfr * f_re + dfi * f_im) * inv
        d_mag = d_lbr * cs + d_lbi * sn
        d_ang = d_lbi * lbr - d_lbr * lbi
        dlr_ref[...] = (dfr * nr + dfi * lbi) * inv + 2.0 * d_den * l_r + d_mag * mag * dt
        dli_ref[...] = (dfr * lbi - dfi * nr) * inv + 2.0 * d_den * l_i + d_ang * dt
        dldt_ref[...] = jnp.sum(d_mag * mag * l_r + d_ang * l_i, axis=1, keepdims=True) * dt

    gp = jax.ShapeDtypeStruct((g, p), F32)
    cgp = jax.ShapeDtypeStruct((c, g, p), F32)
    return pl.pallas_call(body, out_shape=[gp, gp, jax.ShapeDtypeStruct((g, 1), F32), cgp, cgp],
                          name="s5_param_bwd")(lr, li, ldt, bt_re, bt_im, dlbr, dlbi, dbbr, dbbi)


def _att_masks(rep, gb):
    rows = rep * ATT_BLK
    qi = lax.broadcasted_iota(jnp.int32, (rows, 2 * ATT_BLK), 0) % ATT_BLK
    si = lax.broadcasted_iota(jnp.int32, (rows, 2 * ATT_BLK), 1)
    prev = (si < ATT_BLK) & (si >= qi) & (gb > 0)
    cur = (si >= ATT_BLK) & (si - ATT_BLK <= qi)
    return prev | cur


def _att_rows(start, dil):
    return pl.ds(start, ATT_BLK) if dil == 1 else pl.ds(start, ATT_BLK, stride=dil)


def _att_plan(T, dil):
    span = ATT_BLK * dil
    sbr = max(span, min(T, 1024))
    return span, sbr, T // sbr


def _att_block(sb, i, sbr, span, dil):
    loc = (i // dil) * span + i % dil
    cur = sb * sbr + loc
    gb = sb * (sbr // span) + i // dil
    return loc, cur, jnp.where(gb > 0, cur - span, cur), gb


def _att_fwd(q, k, v, grp, dil):
    T = q.shape[0]
    H = q.shape[1] // HEAD_DIM // len(DILATIONS)
    rep = H // N_KV_HEADS
    span, sbr, nsb = _att_plan(T, dil)
    scale = HEAD_DIM ** -0.5

    def body(*refs):
        q_refs = refs[:rep]
        k_ref, v_ref, o_ref, l_ref, o_slab, l_slab = refs[rep:]
        sb = pl.program_id(1)

        def blk(i, _):
            loc, cur, prv, gb = _att_block(sb, i, sbr, span, dil)
            rows = _att_rows(loc, dil)
            qs = jnp.concatenate([r[rows, :] for r in q_refs], axis=0).astype(BF16)
            kcat = jnp.concatenate([k_ref[_att_rows(prv, dil), :], k_ref[_att_rows(cur, dil), :]], axis=0)
            vcat = jnp.concatenate([v_ref[_att_rows(prv, dil), :], v_ref[_att_rows(cur, dil), :]], axis=0)
            s = jnp.where(_att_masks(rep, gb), _dot(qs, kcat.astype(BF16), NT) * scale, NEG_INF)
            m = jnp.max(s, axis=-1, keepdims=True)
            p = jnp.exp(s - m)
            l = jnp.sum(p, axis=-1, keepdims=True)
            o = _dot(p.astype(BF16), vcat.astype(BF16)) / l
            lse = jnp.broadcast_to(m + jnp.log(l), (rep * ATT_BLK, HEAD_DIM))
            for j in range(rep):
                o_slab[j, rows, :] = o[j * ATT_BLK:(j + 1) * ATT_BLK]
                l_slab[j, rows, :] = lse[j * ATT_BLK:(j + 1) * ATT_BLK]
            return 0

        lax.fori_loop(0, sbr // ATT_BLK, blk, 0)
        for j in range(rep):
            o_ref[:, j * HEAD_DIM:(j + 1) * HEAD_DIM] = o_slab[j].astype(BF16)
            l_ref[:, j * HEAD_DIM:(j + 1) * HEAD_DIM] = l_slab[j]

    qspecs = [pl.BlockSpec((sbr, HEAD_DIM), functools.partial(lambda h, s, j: (s, grp * H + h * rep + j), j=j))
              for j in range(rep)]
    kspec = pl.BlockSpec((T, HEAD_DIM), lambda h, s: (0, h))
    ospec = pl.BlockSpec((sbr, rep * HEAD_DIM), lambda h, s: (s, h))
    slab = pltpu.VMEM((rep, sbr, HEAD_DIM), F32)
    return pl.pallas_call(
        body, grid=(N_KV_HEADS, nsb), in_specs=qspecs + [kspec, kspec], out_specs=[ospec, ospec],
        out_shape=[jax.ShapeDtypeStruct((T, H * HEAD_DIM), BF16), jax.ShapeDtypeStruct((T, H * HEAD_DIM), F32)],
        scratch_shapes=[slab, slab],
        compiler_params=_params(("parallel", "arbitrary")), name=f"att_fwd_d{dil}",
    )(*([q] * rep), k, v)


def _att_combine(outs, lses):
    T, W = outs[0].shape
    ng = len(outs)
    tr, tcol = _tile(T, 512, 8), _tile(W, 512)

    def body(*refs):
        o_refs, l_refs = refs[:ng], refs[ng:2 * ng]
        ob_ref, lse_ref = refs[2 * ng:]
        ls = [r[...] for r in l_refs]
        m = functools.reduce(jnp.maximum, ls)
        es = [jnp.exp(l - m) for l in ls]
        den = functools.reduce(lambda a, b: a + b, es)
        num = functools.reduce(lambda a, b: a + b, [e * o[...].astype(F32) for e, o in zip(es, o_refs)])
        ob_ref[...] = (num / den).astype(BF16)
        lse_ref[...] = m + jnp.log(den)

    spec = pl.BlockSpec((tr, tcol), lambda i, j: (i, j))
    return pl.pallas_call(
        body, grid=(T // tr, W // tcol), in_specs=[spec] * (2 * ng), out_specs=[spec, spec],
        out_shape=[jax.ShapeDtypeStruct((T, W), BF16), jax.ShapeDtypeStruct((T, W), F32)],
        compiler_params=_params(("parallel", "parallel")), name="att_combine")(*outs, *lses)


STAT_LANE = HEAD_DIM // 2


def _att_stats(lse, o, do):
    T, W = lse.shape
    tr = _tile(T, 256, 16)

    def body(l_ref, o_ref, do_ref, s_ref):
        lane = lax.broadcasted_iota(jnp.int32, (tr, HEAD_DIM), 1)
        for h in range(W // HEAD_DIM):
            cols = slice(h * HEAD_DIM, (h + 1) * HEAD_DIM)
            delta = jnp.sum(do_ref[:, cols] * o_ref[:, cols].astype(F32), axis=-1, keepdims=True)
            s_ref[:, cols] = jnp.where(lane < STAT_LANE, l_ref[:, cols], delta)

    spec = pl.BlockSpec((tr, W), lambda i: (i, 0))
    return pl.pallas_call(
        body, grid=(T // tr,), in_specs=[spec] * 3, out_specs=spec,
        out_shape=jax.ShapeDtypeStruct((T, W), F32),
        compiler_params=_params(("parallel",)), name="att_stats")(lse, o, do)


def _att_bwd(q, k, v, do, stats, dq, grp, dil):
    T = q.shape[0]
    H = do.shape[1] // HEAD_DIM
    rep = H // N_KV_HEADS
    hs = rep
    span, sbr, nsb = _att_plan(T, dil)
    scale = HEAD_DIM ** -0.5

    def body(*refs):
        q_refs, do_refs, st_refs = refs[:hs], refs[hs + 2:2 * hs + 2], refs[2 * hs + 2:3 * hs + 2]
        k_ref, v_ref = refs[hs], refs[hs + 1]
        dq_ref, dk_ref, dv_ref, dq_slab = refs[3 * hs + 3:]
        sb = pl.program_id(2)

        @pl.when((pl.program_id(1) == 0) & (sb == 0))
        def _():
            dk_ref[...] = jnp.zeros_like(dk_ref)
            dv_ref[...] = jnp.zeros_like(dv_ref)

        def blk(i, _):
            loc, cur, prv, gb = _att_block(sb, i, sbr, span, dil)
            rows, kc, kp = _att_rows(loc, dil), _att_rows(cur, dil), _att_rows(prv, dil)
            qs = jnp.concatenate([r[rows, :] for r in q_refs], axis=0).astype(BF16)
            dos = jnp.concatenate([r[rows, :] for r in do_refs], axis=0).astype(BF16)
            st = jnp.concatenate([r[rows, :] for r in st_refs], axis=0)
            kcat = jnp.concatenate([k_ref[kp, :], k_ref[kc, :]], axis=0).astype(BF16)
            vcat = jnp.concatenate([v_ref[kp, :], v_ref[kc, :]], axis=0).astype(BF16)
            s = _dot(qs, kcat, NT) * scale
            p = jnp.where(_att_masks(hs, gb), jnp.exp(s - st[:, 0:1]), 0.0)
            dp = _dot(dos, vcat, NT)
            ds = (p * (dp - st[:, STAT_LANE:STAT_LANE + 1]) * scale).astype(BF16)
            dvc = _dot(p.astype(BF16), dos, TN)
            dkc = _dot(ds, qs, TN)
            dqs = _dot(ds, kcat)
            for j in range(hs):
                dq_slab[j, rows, :] = dqs[j * ATT_BLK:(j + 1) * ATT_BLK]
            dk_ref[kc, :] += dkc[ATT_BLK:]
            dv_ref[kc, :] += dvc[ATT_BLK:]

            @pl.when(gb > 0)
            def _():
                dk_ref[kp, :] += dkc[:ATT_BLK]
                dv_ref[kp, :] += dvc[:ATT_BLK]

            return 0

        lax.fori_loop(0, sbr // ATT_BLK, blk, 0)
        for j in range(hs):
            dq_ref[:, j * HEAD_DIM:(j + 1) * HEAD_DIM] = dq_slab[j].astype(BF16)

    def head_specs(col0):
        return [pl.BlockSpec((sbr, HEAD_DIM),
                             functools.partial(lambda h, f, s, j: (s, col0 + h * rep + f * hs + j), j=j))
                for j in range(hs)]

    kspec = pl.BlockSpec((T, HEAD_DIM), lambda h, f, s: (0, h))
    dqspec = pl.BlockSpec((sbr, hs * HEAD_DIM), lambda h, f, s: (s, (grp * H + h * rep) // hs + f))
    n_in = 3 * hs + 3
    return pl.pallas_call(
        body, grid=(N_KV_HEADS, rep // hs, nsb),
        in_specs=head_specs(grp * H) + [kspec, kspec] + head_specs(0) + head_specs(0) + [ANY],
        out_specs=[dqspec, kspec, kspec],
        out_shape=[jax.ShapeDtypeStruct(dq.shape, BF16),
                   jax.ShapeDtypeStruct((T, N_KV_HEADS * HEAD_DIM), F32),
                   jax.ShapeDtypeStruct((T, N_KV_HEADS * HEAD_DIM), F32)],
        scratch_shapes=[pltpu.VMEM((hs, sbr, HEAD_DIM), F32)],
        input_output_aliases={n_in - 1: 0},
        compiler_params=_params(("parallel", "arbitrary", "arbitrary")), name=f"att_bwd_d{dil}",
    )(*([q] * hs), k, v, *([do] * hs), *([stats] * hs), dq)


def _sum_kv(dks, dvs):
    T, W = dks[0].shape
    ng = len(dks)
    tr = _tile(T, 512, 8)

    def body(*refs):
        o_ref = refs[2 * ng]
        o_ref[0] = functools.reduce(lambda a, b: a + b, [r[...] for r in refs[:ng]]).astype(BF16)
        o_ref[1] = functools.reduce(lambda a, b: a + b, [r[...] for r in refs[ng:2 * ng]]).astype(BF16)

    spec = pl.BlockSpec((tr, W), lambda i: (i, 0))
    return pl.pallas_call(
        body, grid=(T // tr,), in_specs=[spec] * (2 * ng),
        out_specs=pl.BlockSpec((2, tr, W), lambda i: (0, i, 0)),
        out_shape=jax.ShapeDtypeStruct((2, T, W), BF16),
        compiler_params=_params(("parallel",)), name="sum_kv")(*dks, *dvs)


def _local_step(x, tgt, w, wts, ready, dep0=None, small_ready=None):
    T, D = x.shape
    g = {}

    (u0,) = _rms_fwd("rms_a", x, [w["a_norm"]], [F32], dep=dep0)
    lbr, lbi, bbt_re, bbt_im = _s5_param_fwd(w["lam_re"], w["lam_im"], w["log_dt"], w["bt_re"], w["bt_im"])
    a_re, a_im = lbr.reshape(-1, 8, 128), lbi.reshape(-1, 8, 128)
    C, G, P = w["bt_re"].shape
    nb = G // S5_GB
    bb2 = jnp.stack([bbt_re, bbt_im]).transpose(0, 2, 1, 3).reshape(2, nb, S5_GB * C, P)
    bb2 = jnp.concatenate([bb2, bb2], axis=-1)
    cc2 = jnp.stack([w["c_re"], w["c_im"]]).reshape(2, nb, S5_GB * C, P).transpose(0, 1, 3, 2)
    z, h0 = _s5_fwd(u0, bb2, cc2, a_re, a_im, w["s5_d"])
    w_glu = wts("glu", z)
    x1, vg = _mm_dual_fwd("glu_fwd", z, w_glu, x, "glu")

    def ffn_fwd(xin, layer):
        (nrm,) = _rms_fwd(f"rms_f{layer}", xin, [w["ffn_norm"][layer:layer + 1]], [BF16])
        w_in = wts(f"win{layer}", nrm)
        gu, act = _mm_dual_fwd(f"ffn_in{layer}", nrm, w_in, None, "ffn")
        w_out = wts(f"wout{layer}", act)
        xout = _mm_nn(f"ffn_out{layer}", act, w_out, res=xin)
        return xout, (nrm, gu, act, w_in, w_out)

    x2, saved0 = ffn_fwd(x1, 0)
    kvn, hb = _rms_fwd("rms_b", x2, [w["kv_norm"], w["b_norm"]], [BF16, BF16])
    w_kv, w_q, w_o = wts("wkv", hb), wts("wq", hb), wts("wo", hb)
    k, v = _mm_kv("kv_proj", kvn, w_kv)
    q = _mm_nn("q_proj", hb, w_q)
    outs, lses = [], []
    for grp, dil in enumerate(DILATIONS):
        o_g, l_g = _att_fwd(q, k, v, grp, dil)
        outs.append(o_g)
        lses.append(l_g)
    o, lse = _att_combine(outs, lses)
    wts("win1", o, prefetch=True)
    x3 = _mm_nn("o_proj", o, w_o, res=x2)
    x4, saved1 = ffn_fwd(x3, 1)
    loss_blk, dx4, dx4b, g["final_norm"] = _loss_head(x4, w["final_norm"], tgt)

    def ffn_bwd(dx, dxb, xin, saved, layer, dep):
        nrm, gu, act, w_in, w_out = saved
        dgu = _mm_nt_ffn_bwd(f"ffn_dact{layer}", dxb, w_out, gu, dep=dep)
        g_wout = _mm_tn(f"ffn_dwout{layer}", act, dxb)
        g_win = _mm_tn_pair(f"ffn_dwin{layer}", nrm, dgu)
        dn = _mm_nt_pair(f"ffn_dn{layer}", dgu, w_in, out_dtype=BF16)
        dxo, dxob, (dgn,) = _rms_bwd(f"rms_f{layer}_bwd", xin, dx, [w["ffn_norm"][layer:layer + 1]], [dn])
        tok = ready({f"win{layer}": g_win, f"wout{layer}": g_wout})
        return dxo, dxob, dgn, tok

    dx3, dx3b, dfn1, tok = ffn_bwd(dx4, dx4b, x3, saved1, 1, None)
    do = _mm_nt("o_proj_dx", [dx3b], w_o, dep=tok)
    g_wo = _mm_tn("o_proj_dw", o, dx3b)
    stats = _att_stats(lse, o, do)
    dq = lax.empty(q.shape, BF16)
    dks, dvs = [], []
    for grp, dil in enumerate(DILATIONS):
        dq, dk_g, dv_g = _att_bwd(q, k, v, do, stats, dq, grp, dil)
        dks.append(dk_g)
        dvs.append(dv_g)
    dkv = _sum_kv(dks, dvs)
    dhb = _mm_nt("q_proj_dx", [dq], w_q, out_dtype=BF16)
    g_wq = _mm_tn("q_proj_dw", hb, dq)
    dkvn = _mm_nt_pair("kv_proj_dx", dkv, w_kv, out_dtype=BF16)
    g_wkv = _mm_tn_pair("kv_proj_dw", kvn, dkv)
    dx2, dx2b, (g["kv_norm"], g["b_norm"]) = _rms_bwd(
        "rms_b_bwd", x2, dx3, [w["kv_norm"], w["b_norm"]], [dkvn, dhb])
    tok = ready({"wkv": g_wkv, "wq": g_wq, "wo": g_wo})
    dx1, dx1b, dfn0, tok = ffn_bwd(dx2, dx2b, x1, saved0, 0, tok)
    g["ffn_norm"] = jnp.concatenate([dfn0, dfn1], axis=0)

    dvg = _glu_bwd(dx1, vg, dep=tok)
    dz = _mm_nt_pair("glu_dx", dvg, w_glu)
    tok = ready({"glu": _mm_tn_pair("glu_dw", z, dvg)})
    tok = ready({}, after=tok)
    du, g["s5_d"], da_re, da_im, dbb2, dcc2 = _s5_bwd(
        u0, dz, h0, bb2, cc2, a_re, a_im, w["s5_d"], dep=tok)
    dcc = dcc2.transpose(0, 1, 3, 2).reshape(2, G, C, P)
    g["c_re"], g["c_im"] = dcc[0], -dcc[1]
    dbbt = dbb2[..., :P].reshape(2, G, C, P).transpose(0, 2, 1, 3)
    g["lam_re"], g["lam_im"], g["log_dt"], g["bt_re"], g["bt_im"] = _s5_param_bwd(
        w["lam_re"], w["lam_im"], w["log_dt"], w["bt_re"], w["bt_im"],
        da_re.reshape(G, P), da_im.reshape(G, P), dbbt[0], dbbt[1])
    tok = small_ready(g) if small_ready is not None else None
    grad_x, _, (g["a_norm"],) = _rms_bwd("rms_a_bwd", x, dx1, [w["a_norm"]], [du], dep=tok)
    return loss_blk, grad_x, g


def _coords():
    return lax.axis_index("x"), lax.axis_index("y"), lax.axis_index("c")


def _dev_index(dev):
    return 4 * dev[0] + 2 * dev[1] + dev[2]


def _shard_window(ref, axis, width, idx):
    sl = [slice(None)] * len(ref.shape)
    sl[axis] = pl.ds(pl.multiple_of(idx * width, width), width)
    return ref.at[tuple(sl)]


def _all_gather(name, shards, axes):
    na = len(shards)
    widths = [s.shape[ax] for s, ax in zip(shards, axes)]
    out_shapes = []
    for s, ax in zip(shards, axes):
        shp = list(s.shape)
        shp[ax] *= N_DEV
        out_shapes.append(jax.ShapeDtypeStruct(tuple(shp), s.dtype))

    def body(*refs):
        ins, outs = refs[:na], refs[na:2 * na]
        send_sems, recv_sems, local_sems = refs[2 * na:]
        x, y, c = _coords()
        me, sib = (x, y, c), (x, y, 1 - c)
        chips = [(1 - x, y), (x, 1 - y), (1 - x, 1 - y)]

        def blk(a, dev):
            return _shard_window(outs[a], axes[a], widths[a], _dev_index(dev))

        def copy(a, kk, block, to, src=None):
            return pltpu.make_async_remote_copy(
                src_ref=blk(a, block) if src is None else src, dst_ref=blk(a, block),
                send_sem=send_sems.at[a, kk], recv_sem=recv_sems.at[a, kk],
                device_id=to, device_id_type=MESH)

        local = [pltpu.make_async_copy(ins[a], blk(a, me), local_sems.at[a]) for a in range(na)]
        for cp in local:
            cp.start()
        sent = []
        for a in range(na):
            first = [copy(a, 0, me, sib, src=ins[a])]
            first += [copy(a, 1 + j, me, (*chip, c), src=ins[a]) for j, chip in enumerate(chips)]
            for cp in first:
                cp.start()
            sent += first
        for a in range(na):
            for j, chip in enumerate(chips):
                copy(a, 1 + j, (*chip, c), me).wait_recv()
                fwd = copy(a, 4 + j, (*chip, c), sib)
                fwd.start()
                sent.append(fwd)
        for a in range(na):
            copy(a, 0, sib, me).wait_recv()
            for j, chip in enumerate(chips):
                copy(a, 4 + j, (*chip, 1 - c), me).wait_recv()
        for cp in sent:
            cp.wait_send()
        for cp in local:
            cp.wait()

    return pl.pallas_call(
        body, out_shape=out_shapes, in_specs=[ANY] * na, out_specs=[ANY] * na,
        scratch_shapes=[pltpu.SemaphoreType.DMA((na, 7)), pltpu.SemaphoreType.DMA((na, 7)),
                        pltpu.SemaphoreType.DMA((na,))],
        name=name)(*shards)


HBM = pl.BlockSpec(memory_space=pltpu.HBM)
SEM = pl.BlockSpec(memory_space=pltpu.SEMAPHORE)
TOKEN_SPEC = pl.BlockSpec(memory_space=pltpu.VMEM)
TOKEN_SHAPE = jax.ShapeDtypeStruct((8, 128), F32)
SPLIT_PARAMS = pltpu.CompilerParams(has_side_effects=pltpu.SideEffectType.DATAFLOW_SIDE_EFFECTING)


def _hbm(x):
    return pltpu.with_memory_space_constraint(x, pltpu.HBM)


def _hbm_like(x):
    return pltpu.HBM(x.shape, x.dtype)


def _dma_sems(*shape):
    return pltpu.SemaphoreType.DMA(shape)


def _cast_and_place(name, shard, layer, axis, pos, dtype):
    rows, cols = shard.shape[-2:]
    tr = _tile(rows, 256, 16)
    nt = rows // tr
    full = (rows, cols * N_DEV) if axis == 1 else (rows * N_DEV, cols)

    def dev(p):
        return 4 * p[0] + 2 * p[1] + p[2]

    def body(pos_ref, s_ref, b_ref, l_ref):
        v = s_ref[...].astype(dtype)
        b_ref[...] = v
        l_ref[...] = v

    blk = pl.BlockSpec((tr, cols), lambda i, p: (i, 0))
    if axis == 1:
        lspec = pl.BlockSpec((tr, cols), lambda i, p: (i, dev(p)))
    else:
        lspec = pl.BlockSpec((tr, cols), lambda i, p: (dev(p) * nt + i, 0))
    return pl.pallas_call(
        body, grid_spec=pltpu.PrefetchScalarGridSpec(
            num_scalar_prefetch=1, grid=(nt,),
            in_specs=[pl.BlockSpec((None, tr, cols), lambda i, p: (layer, i, 0))], out_specs=[blk, lspec]),
        out_shape=[jax.ShapeDtypeStruct((rows, cols), dtype), jax.ShapeDtypeStruct(full, dtype)],
        compiler_params=_params(("parallel",)), name=name)(pos, shard)


def _gather_start(name, shards, lands, axes, groups):
    na, ng = len(shards), len(groups)
    widths = [s.shape[ax] for s, ax in zip(shards, axes)]

    def body(*refs):
        sh, ld = refs[:na], refs[na:2 * na]
        sems = refs[2 * na:2 * na + 3 * ng]
        token = refs[-1]
        x, y, c = _coords()
        me, sib = (x, y, c), (x, y, 1 - c)
        chips = [(1 - x, y), (x, 1 - y), (1 - x, 1 - y)]
        for gi, grp in enumerate(groups):
            send, r_d2d, r_ici = sems[3 * gi:3 * gi + 3]
            for li, a in enumerate(grp):
                dst = _shard_window(ld[a], axes[a], widths[a], _dev_index(me))
                pltpu.make_async_remote_copy(
                    src_ref=sh[a], dst_ref=dst, send_sem=send.at[4 * li], recv_sem=r_d2d.at[li],
                    device_id=sib, device_id_type=MESH).start()
                for j, chip in enumerate(chips):
                    pltpu.make_async_remote_copy(
                        src_ref=sh[a], dst_ref=dst, send_sem=send.at[4 * li + 1 + j], recv_sem=r_ici.at[3 * li + j],
                        device_id=(*chip, c), device_id_type=MESH).start()
        token[...] = jnp.zeros_like(token)

    out_shape, out_specs = [], []
    for grp in groups:
        out_shape += [_dma_sems(4 * len(grp)), _dma_sems(len(grp)), _dma_sems(3 * len(grp))]
        out_specs += [SEM] * 3
    out_shape += [_hbm_like(s) for s in shards] + [_hbm_like(l) for l in lands] + [TOKEN_SHAPE]
    out_specs += [HBM] * (2 * na) + [TOKEN_SPEC]
    aliases = {a: 3 * ng + a for a in range(2 * na)}
    res = pl.pallas_call(
        body, name=name, out_shape=out_shape, in_specs=[HBM] * (2 * na),
        out_specs=out_specs, input_output_aliases=aliases, compiler_params=SPLIT_PARAMS,
    )(*[_hbm(s) for s in shards], *[_hbm(l) for l in lands])
    sems = [tuple(res[3 * gi:3 * gi + 3]) for gi in range(ng)]
    return sems, list(res[3 * ng:3 * ng + na]), list(res[3 * ng + na:3 * ng + 2 * na]), res[-1]


def _gather_forward(name, lands, axes, r_ici, after):
    n = len(lands)
    widths = [l.shape[ax] // N_DEV for l, ax in zip(lands, axes)]

    def body(*refs):
        ld, r_ici_ref = refs[:n], refs[n]
        f_send, f_recv = refs[n + 2], refs[n + 3]
        x, y, c = _coords()
        sib = (x, y, 1 - c)
        chips = [(1 - x, y), (x, 1 - y), (1 - x, 1 - y)]
        for li in range(n):
            for j, chip in enumerate(chips):
                blk = _shard_window(ld[li], axes[li], widths[li], _dev_index((*chip, c)))
                pltpu.make_async_remote_copy(
                    src_ref=blk, dst_ref=blk, send_sem=f_send.at[3 * li + j], recv_sem=r_ici_ref.at[3 * li + j],
                    device_id=(*chip, c), device_id_type=MESH).wait_recv()
                pltpu.make_async_remote_copy(
                    src_ref=blk, dst_ref=blk, send_sem=f_send.at[3 * li + j], recv_sem=f_recv.at[3 * li + j],
                    device_id=sib, device_id_type=MESH).start()

    res = pl.pallas_call(
        body, name=name, out_shape=[_dma_sems(3 * n), _dma_sems(3 * n)] + [_hbm_like(l) for l in lands],
        in_specs=[HBM] * n + [SEM, ANY], out_specs=[SEM, SEM] + [HBM] * n,
        input_output_aliases={li: 2 + li for li in range(n)}, compiler_params=SPLIT_PARAMS,
    )(*lands, r_ici, after)
    return res[0], res[1], list(res[2:])


def _gather_finish(name, shards, lands, axes, send, r_d2d, f_send, f_recv, after):
    n = len(lands)
    widths = [l.shape[ax] // N_DEV for l, ax in zip(lands, axes)]

    def body(*refs):
        sh, ld = refs[:n], refs[n:2 * n]
        send_ref, r_d2d_ref, f_send_ref, f_recv_ref = refs[2 * n:2 * n + 4]
        x, y, c = _coords()
        me, sib = (x, y, c), (x, y, 1 - c)
        chips = [(1 - x, y), (x, 1 - y), (1 - x, 1 - y)]

        def blk(li, dev):
            return _shard_window(ld[li], axes[li], widths[li], _dev_index(dev))

        for li in range(n):
            for kk in range(4):
                pltpu.make_async_remote_copy(
                    src_ref=sh[li], dst_ref=blk(li, me), send_sem=send_ref.at[4 * li + kk], recv_sem=r_d2d_ref.at[li],
                    device_id=sib, device_id_type=MESH).wait_send()
            pltpu.make_async_remote_copy(
                src_ref=blk(li, sib), dst_ref=blk(li, sib), send_sem=send_ref.at[4 * li], recv_sem=r_d2d_ref.at[li],
                device_id=sib, device_id_type=MESH).wait_recv()
            for j, chip in enumerate(chips):
                pltpu.make_async_remote_copy(
                    src_ref=blk(li, (*chip, c)), dst_ref=blk(li, (*chip, c)), send_sem=f_send_ref.at[3 * li + j],
                    recv_sem=f_recv_ref.at[3 * li + j], device_id=sib, device_id_type=MESH).wait_send()
                pltpu.make_async_remote_copy(
                    src_ref=blk(li, (*chip, 1 - c)), dst_ref=blk(li, (*chip, 1 - c)), send_sem=f_send_ref.at[3 * li + j],
                    recv_sem=f_recv_ref.at[3 * li + j], device_id=sib, device_id_type=MESH).wait_recv()

    res = pl.pallas_call(
        body, name=name, out_shape=[_hbm_like(s) for s in shards] + [_hbm_like(l) for l in lands],
        in_specs=[HBM] * (2 * n) + [SEM] * 4 + [ANY], out_specs=[HBM] * (2 * n),
        input_output_aliases={i: i for i in range(2 * n)}, compiler_params=SPLIT_PARAMS,
    )(*shards, *lands, send, r_d2d, f_send, f_recv, after)
    return list(res[n:])


def _chip_exchange_start(name, parts):
    n = len(parts)

    def body(*refs):
        src, ld = refs[:n], refs[n:2 * n]
        send, recv = refs[2 * n], refs[2 * n + 1]
        token = refs[-1]
        x, y, c = _coords()
        chips = [(1 - x, y), (x, 1 - y), (1 - x, 1 - y)]
        for li in range(n):
            for kk, chip in enumerate(chips):
                pltpu.make_async_remote_copy(
                    src_ref=src[li].at[kk], dst_ref=ld[li].at[kk], send_sem=send.at[3 * li + kk],
                    recv_sem=recv.at[3 * li + kk], device_id=(*chip, c), device_id_type=MESH).start()
        token[...] = jnp.zeros_like(token)

    lands = [lax.empty(p.shape, p.dtype) for p in parts]
    res = pl.pallas_call(
        body, name=name,
        out_shape=[_dma_sems(3 * n), _dma_sems(3 * n)] + [_hbm_like(p) for p in parts] * 2 + [TOKEN_SHAPE],
        in_specs=[HBM] * (2 * n), out_specs=[SEM, SEM] + [HBM] * (2 * n) + [TOKEN_SPEC],
        input_output_aliases={i: 2 + i for i in range(2 * n)}, compiler_params=SPLIT_PARAMS,
    )(*[_hbm(p) for p in parts], *[_hbm(l) for l in lands])
    return res[0], res[1], list(res[2:2 + n]), list(res[2 + n:2 + 2 * n]), res[-1]


def _chip_exchange_finish(name, started, after):
    counts = [len(st[2]) for st in started]
    total = sum(counts)
    ns = len(started)

    def body(*refs):
        src, ld = refs[:total], refs[total:2 * total]
        sems = refs[2 * total:2 * total + 2 * ns]
        x, y, c = _coords()
        chips = [(1 - x, y), (x, 1 - y), (1 - x, 1 - y)]
        off = 0
        for si, cnt in enumerate(counts):
            send, recv = sems[2 * si], sems[2 * si + 1]
            for li in range(cnt):
                for kk, chip in enumerate(chips):
                    cp = pltpu.make_async_remote_copy(
                        src_ref=src[off + li].at[kk], dst_ref=ld[off + li].at[kk], send_sem=send.at[3 * li + kk],
                        recv_sem=recv.at[3 * li + kk], device_id=(*chip, c), device_id_type=MESH)
                    cp.wait_send()
                    cp.wait_recv()
            off += cnt

    srcs = [p for st in started for p in st[2]]
    lands = [l for st in started for l in st[3]]
    sems = [s for st in started for s in st[:2]]
    res = pl.pallas_call(
        body, name=name, out_shape=[_hbm_like(p) for p in srcs + lands],
        in_specs=[HBM] * (2 * total) + [SEM] * (2 * ns) + [ANY], out_specs=[HBM] * (2 * total),
        input_output_aliases={i: i for i in range(2 * total)}, compiler_params=SPLIT_PARAMS,
    )(*srcs, *lands, *sems, after)
    out, off = [], total
    for cnt in counts:
        out.append(list(res[off:off + cnt]))
        off += cnt
    return out


def _pair_copies(src, ld, send, recv, axes, widths):
    x, y, c = _coords()
    chips = [(x, y), (1 - x, y), (x, 1 - y), (1 - x, 1 - y)]
    return [pltpu.make_async_remote_copy(
        src_ref=_shard_window(src[li], axes[li], widths[li], _dev_index((*chip, 1 - c))),
        dst_ref=ld[li].at[kk], send_sem=send.at[4 * li + kk], recv_sem=recv.at[4 * li + kk],
        device_id=(x, y, 1 - c), device_id_type=MESH)
        for li in range(len(src)) for kk, chip in enumerate(chips)]


def _pair_exchange_start(name, grads, axes):
    n = len(grads)
    widths = [gr.shape[ax] // N_DEV for gr, ax in zip(grads, axes)]
    lands = []
    for gr, ax, wd in zip(grads, axes, widths):
        shp = list(gr.shape)
        shp[ax] = wd
        lands.append(lax.empty((4, *shp), gr.dtype))

    def body(*refs):
        for cp in _pair_copies(refs[:n], refs[n:2 * n], refs[2 * n], refs[2 * n + 1], axes, widths):
            cp.start()
        refs[-1][...] = jnp.zeros_like(refs[-1])

    res = pl.pallas_call(
        body, name=name,
        out_shape=[_dma_sems(4 * n), _dma_sems(4 * n)] + [_hbm_like(a) for a in grads + lands] + [TOKEN_SHAPE],
        in_specs=[HBM] * (2 * n), out_specs=[SEM, SEM] + [HBM] * (2 * n) + [TOKEN_SPEC],
        input_output_aliases={i: 2 + i for i in range(2 * n)}, compiler_params=SPLIT_PARAMS,
    )(*[_hbm(a) for a in grads + lands])
    return res[0], res[1], list(res[2:2 + n]), list(res[2 + n:2 + 2 * n]), res[-1]


def _pair_exchange_finish(name, started, axes, after):
    send, recv, grads, lands, _ = started
    n = len(grads)
    widths = [gr.shape[ax] // N_DEV for gr, ax in zip(grads, axes)]

    def body(*refs):
        for cp in _pair_copies(refs[:n], refs[n:2 * n], refs[2 * n], refs[2 * n + 1], axes, widths):
            cp.wait_send()
            cp.wait_recv()

    res = pl.pallas_call(
        body, name=name, out_shape=[_hbm_like(a) for a in grads + lands],
        in_specs=[HBM] * (2 * n) + [SEM, SEM, ANY], out_specs=[HBM] * (2 * n),
        input_output_aliases={i: i for i in range(2 * n)}, compiler_params=SPLIT_PARAMS,
    )(*grads, *lands, send, recv, after)
    return list(res[:n]), list(res[n:])


def _pair_sum(name, grad, land, axis, pos):
    wd = grad.shape[axis] // N_DEV
    shard_shape = land.shape[1:]
    rows, cols = shard_shape
    tr = _tile(rows, 256, 16)
    nt = rows // tr

    def dev_of(kk, pos_ref):
        return 4 * (pos_ref[0] ^ (kk & 1)) + 2 * (pos_ref[1] ^ (kk >> 1)) + pos_ref[2]

    def gspec(kk):
        if axis == 1:
            return pl.BlockSpec((tr, wd), lambda t, p: (t, dev_of(kk, p)))
        return pl.BlockSpec((tr, cols), lambda t, p: (dev_of(kk, p) * nt + t, 0))

    def body(pos_ref, g0, g1, g2, g3, l_ref, p0_ref, pb_ref):
        p0_ref[...] = g0[...].astype(F32) + l_ref[0].astype(F32)
        for kk, g_ref in enumerate((g1, g2, g3)):
            pb_ref[kk] = (g_ref[...].astype(F32) + l_ref[kk + 1].astype(F32)).astype(BF16)

    return pl.pallas_call(
        body,
        grid_spec=pltpu.PrefetchScalarGridSpec(
            num_scalar_prefetch=1, grid=(nt,),
            in_specs=[gspec(kk) for kk in range(4)] + [pl.BlockSpec((4, tr, cols), lambda t, p: (0, t, 0))],
            out_specs=[pl.BlockSpec((tr, cols), lambda t, p: (t, 0)),
                       pl.BlockSpec((3, tr, cols), lambda t, p: (0, t, 0))]),
        out_shape=[jax.ShapeDtypeStruct(shard_shape, F32), jax.ShapeDtypeStruct((3, *shard_shape), BF16)],
        compiler_params=_params(("parallel",)), name=name)(pos, grad, grad, grad, grad, land)


def _adamw_math(g, w, m, v):
    c1 = 1.0 - ADAM_B1 ** ADAM_STEP
    c2 = 1.0 - ADAM_B2 ** ADAM_STEP
    nm = ADAM_B1 * m + (1.0 - ADAM_B1) * g
    nv = ADAM_B2 * v + (1.0 - ADAM_B2) * (g * g)
    return -ADAM_LR * ((nm / c1) / (jnp.sqrt(nv / c2) + ADAM_EPS) + ADAM_WD * w), nm, nv


def _sum_parts(name, parts):
    n, rows, cols = parts.shape
    tr = _tile(rows, 1024, 8)

    def body(p_ref, o_ref):
        o_ref[...] = functools.reduce(lambda a, b: a + b, [p_ref[i] for i in range(n)])

    return pl.pallas_call(
        body, grid=(rows // tr,), in_specs=[pl.BlockSpec((n, tr, cols), lambda i: (0, i, 0))],
        out_specs=pl.BlockSpec((tr, cols), lambda i: (i, 0)), out_shape=jax.ShapeDtypeStruct((rows, cols), F32),
        compiler_params=_params(("parallel",)), name=name)(parts)


def _adamw_multi(name, gs, ws, ms, vs):
    k = len(gs)
    rows, cols = ws[0].shape
    tr = _tile(rows, 1024, 8)

    def body(*refs):
        for i in range(k):
            g_ref, w_ref, m_ref, v_ref = (refs[j * k + i] for j in range(4))
            og_ref, d_ref, nm_ref, nv_ref = (refs[(4 + j) * k + i] for j in range(4))
            g = g_ref[...]
            og_ref[...] = g
            d_ref[...], nm_ref[...], nv_ref[...] = _adamw_math(g, w_ref[...], m_ref[...], v_ref[...])

    spec = pl.BlockSpec((tr, cols), lambda i: (i, 0))
    res = pl.pallas_call(
        body, grid=(rows // tr,), in_specs=[spec] * (4 * k), out_specs=[spec] * (4 * k),
        out_shape=[jax.ShapeDtypeStruct((rows, cols), F32)] * (4 * k),
        compiler_params=_params(("parallel",)), name=name)(*gs, *ws, *ms, *vs)
    return [[res[j * k + i] for j in range(4)] for i in range(k)]


def _adamw(name, parts, w, m, v, layer=None, prev=None):
    rows, cols = w.shape[-2:]
    tr = _tile(rows, 256, 8)
    npart = len(parts)

    def body(*refs):
        p_refs = refs[:npart]
        w_ref, m_ref, v_ref = refs[npart:npart + 3]
        g_ref, d_ref, nm_ref, nv_ref = refs[-4:]
        g = None
        for r in p_refs:
            if len(r.shape) == 3:
                for i in range(r.shape[0]):
                    t = r[i].astype(F32)
                    g = t if g is None else g + t
            else:
                t = r[...].astype(F32)
                g = t if g is None else g + t
        g_ref[...] = g
        d_ref[...], nm_ref[...], nv_ref[...] = _adamw_math(g, w_ref[...], m_ref[...], v_ref[...])

    spec = pl.BlockSpec((tr, cols), lambda i: (i, 0))
    wspec = spec if layer is None else pl.BlockSpec((None, tr, cols), lambda i: (layer, i, 0))
    pspecs = [pl.BlockSpec((p.shape[0], tr, cols), lambda i: (0, i, 0)) if p.ndim == 3 else spec
              for p in parts]
    prev = list(prev) if prev else []
    return pl.pallas_call(
        body, grid=(rows // tr,), in_specs=pspecs + [wspec] * 3 + [ANY] * len(prev), out_specs=[wspec] * 4,
        out_shape=[jax.ShapeDtypeStruct(w.shape, F32)] * 4,
        input_output_aliases={npart + 3 + i: i for i in range(len(prev))},
        compiler_params=_params(("parallel",)), name=name)(*parts, w, m, v, *prev)


SMALL_NAMES = ("s5_lam_re", "s5_lam_im", "s5_log_dt", "s5_b_re", "s5_b_im", "s5_c_re", "s5_c_im",
               "ffn_norm", "b_norm_mix", "kv_norm", "final_norm")
SMALL_PAD = 1024
SMALL_GROUPS = (("s5_lam_re", "s5_lam_im"), ("s5_log_dt",), ("s5_b_re", "s5_b_im"), ("s5_c_re", "s5_c_im"),
                ("ffn_norm",), ("b_norm_mix", "kv_norm", "final_norm"))


def _pack(parts):
    flat = []
    for p in parts:
        f = p.reshape(-1)
        pad = (-f.shape[0]) % SMALL_PAD
        if pad:
            f = jnp.concatenate([f, jnp.zeros((pad,), f.dtype)])
        flat.append(f)
    return jnp.concatenate(flat).reshape(-1, 128)


def _unpack(packed, shapes):
    flat = packed.reshape(-1)
    out, off = [], 0
    for shp in shapes:
        size = math.prod(shp)
        out.append(flat[off:off + size].reshape(shp))
        off += size + (-size) % SMALL_PAD
    return out


def kernel(x, s5_lam_re, s5_lam_im, s5_log_dt, s5_b_re, s5_b_im, s5_c_re, s5_c_im, s5_d, s5_w_glu, a_norm_mix, ffn_norm, ffn_w_in, ffn_w_out, b_norm_mix, attn_w_q, attn_w_o, kv_norm, w_kv, final_norm, loss_target, m_s5_lam_re, m_s5_lam_im, m_s5_log_dt, m_s5_b_re, m_s5_b_im, m_s5_c_re, m_s5_c_im, m_s5_d, m_s5_w_glu, m_a_norm_mix, m_ffn_norm, m_ffn_w_in, m_ffn_w_out, m_b_norm_mix, m_attn_w_q, m_attn_w_o, m_kv_norm, m_w_kv, m_final_norm, v_s5_lam_re, v_s5_lam_im, v_s5_log_dt, v_s5_b_re, v_s5_b_im, v_s5_c_re, v_s5_c_im, v_s5_d, v_s5_w_glu, v_a_norm_mix, v_ffn_norm, v_ffn_w_in, v_ffn_w_out, v_b_norm_mix, v_attn_w_q, v_attn_w_o, v_kv_norm, v_w_kv, v_final_norm):
    args = dict(locals())
    T, D = x.shape[1], x.shape[2]
    n_layers = ffn_w_in.shape[0]
    xi_, yi_, ci_ = _coords()
    pos = jnp.stack([xi_, yi_, ci_]).astype(jnp.int32)
    me = 4 * xi_ + 2 * yi_ + ci_

    big_names = ["glu"] + [f"win{l}" for l in range(n_layers)] + [f"wout{l}" for l in range(n_layers)] \
        + ["wkv", "wq", "wo"]
    big_shards = [(s5_w_glu, 0)] + [(ffn_w_in, l) for l in range(n_layers)] \
        + [(ffn_w_out, l) for l in range(n_layers)] + [(w_kv[None], 0), (attn_w_q, 0), (attn_w_o, 0)]
    big_axes = [1] + [1] * n_layers + [0] * n_layers + [0, 1, 0]
    big_out_names = ["s5_w_glu"] + ["ffn_w_in"] * n_layers + ["ffn_w_out"] * n_layers \
        + ["w_kv", "attn_w_q", "attn_w_o"]
    index_of = {n: i for i, n in enumerate(big_names)}
    vec_shard = jnp.concatenate([s5_d, a_norm_mix], axis=0)
    (vecs,) = _all_gather("vectors_all_gather", [vec_shard], [1])

    gather_groups = [["glu"], ["win0"], ["wout0"], ["wkv", "wq", "wo"], ["win1", "wout1"]]
    group_idx = [[index_of[n] for n in grp] for grp in gather_groups]
    group_of = {n: gi for gi, grp in enumerate(gather_groups) for n in grp}
    placed = [_cast_and_place(f"cast_place_{n}", s, l, ax, pos, BF16)
              for n, (s, l), ax in zip(big_names, big_shards, big_axes)]
    gather_sems, shards_thru, lands_thru, start_token = _gather_start(
        "weights_gather_start", [p[0] for p in placed], [p[1] for p in placed], big_axes, group_idx)
    full = {}

    forwarded = {}

    def wts(name, after, prefetch=False):
        gi = group_of[name]
        idx = group_idx[gi]
        axes = [big_axes[a] for a in idx]
        send, r_d2d, r_ici = gather_sems[gi]
        if gi not in forwarded:
            forwarded[gi] = _gather_forward(
                f"weights_gather_forward{gi}", [lands_thru[a] for a in idx], axes, r_ici, after)
        if prefetch:
            return None
        if name not in full:
            f_send, f_recv, lands = forwarded[gi]
            done = _gather_finish(f"weights_gather_finish{gi}", [shards_thru[a] for a in idx], lands, axes,
                                  send, r_d2d, f_send, f_recv, after)
            full.update(zip(gather_groups[gi], done))
        return full[name]

    exchanges, pending = [], []

    def flush(after):
        names, axes, started = pending.pop()
        tag = "_".join(names)
        grads, lands = _pair_exchange_finish(f"rs_pair_exchange_finish_{tag}", started, axes, after)
        p0s, pbs = [], []
        for n, gr, land, ax in zip(names, grads, lands, axes):
            p0, pb = _pair_sum(f"rs_pair_sum_{n}", gr, land, ax, pos)
            p0s.append(p0)
            pbs.append(pb)
        started = _chip_exchange_start(f"rs_chip_exchange_start_{tag}", pbs)
        exchanges.append((names, p0s, started))
        return started[4]

    def ready(grads, after=None):
        if not grads:
            return flush(after)
        names = list(grads)
        axes = [big_axes[index_of[n]] for n in names]
        started = _pair_exchange_start(f"rs_pair_exchange_start_{'_'.join(names)}", [grads[n] for n in names], axes)
        token = flush(started[4]) if pending else started[4]
        pending.append((names, axes, started))
        return token

    G, P, C = s5_b_re.shape[1:]
    w = dict(
        a_norm=vecs[1:2], s5_d=vecs[0:1],
        ffn_norm=ffn_norm, b_norm=b_norm_mix, kv_norm=kv_norm.reshape(1, D), final_norm=final_norm.reshape(1, D),
        lam_re=s5_lam_re[0], lam_im=s5_lam_im[0], log_dt=s5_log_dt.reshape(G, 1),
        bt_re=s5_b_re[0].transpose(2, 0, 1), bt_im=s5_b_im[0].transpose(2, 0, 1),
        c_re=s5_c_re[0], c_im=s5_c_im[0],
    )
    small = {}

    def small_ready(g):
        small_g = dict(
            s5_lam_re=g["lam_re"], s5_lam_im=g["lam_im"], s5_log_dt=g["log_dt"],
            s5_b_re=g["bt_re"].transpose(1, 2, 0), s5_b_im=g["bt_im"].transpose(1, 2, 0),
            s5_c_re=g["c_re"], s5_c_im=g["c_im"], ffn_norm=g["ffn_norm"], b_norm_mix=g["b_norm"],
            kv_norm=g["kv_norm"], final_norm=g["final_norm"])
        packed = _pack([small_g[n] for n in SMALL_NAMES])
        shard, land = _cast_and_place("place_small_grads", packed[None], 0, 0, pos, F32)
        sems, thru, lands, token = _gather_start("small_grads_gather_start", [shard], [land], [0], [[0]])
        small.update(sems=sems[0], thru=thru, lands=lands, rows=packed.shape[0])
        return token

    loss_blk, grad_x, g = _local_step(x[0], loss_target[0], w, wts, ready, dep0=start_token,
                                      small_ready=small_ready)
    loss = lax.psum(loss_blk[0, 0], ("x", "y", "c"))
    (tail,) = _all_gather("vector_grads_all_gather", [_pack([g["s5_d"], g["a_norm"]])], [0])

    out = {}

    def put(name, res, shape):
        for kind, r in zip(("grad", "delta", "new_m", "new_v"), res):
            out[f"{kind}_{name}"] = r.reshape(shape)

    updated = {}

    def update(names, p0s, recvd):
        for name, p0, rc in zip(names, p0s, recvd):
            oname = big_out_names[index_of[name]]
            w3, layer = big_shards[index_of[name]]
            updated[oname] = _adamw(f"adamw_{name}", [p0, rc], w3, args["m_" + oname].reshape(w3.shape),
                                    args["v_" + oname].reshape(w3.shape), layer=layer, prev=updated.get(oname))
        return updated[big_out_names[index_of[names[0]]]][0]

    early, last = exchanges[:-1], exchanges[-1]
    landed = _chip_exchange_finish("rs_chip_exchange_finish_early", [e[2] for e in early], grad_x)
    for (names, p0s, _), recvd in zip(early[:-1], landed[:-1]):
        marker = update(names, p0s, recvd)
    send, r_d2d, r_ici = small["sems"]
    f_send, f_recv, small_lands = _gather_forward("small_grads_gather_forward", small["lands"], [0], r_ici, marker)
    marker = update(early[-1][0], early[-1][1], landed[-1])
    (all_parts,) = _gather_finish("small_grads_gather_finish", small["thru"], small_lands, [0],
                                  send, r_d2d, f_send, f_recv, marker)
    (recvd,) = _chip_exchange_finish("rs_chip_exchange_finish_last", [last[2]], marker)
    update(last[0], last[1], recvd)
    for oname, res in updated.items():
        put(oname, res, args[oname].shape)
    g_small = _sum_parts("small_grads_sum", all_parts.reshape(N_DEV, small["rows"], 128))
    g_small = dict(zip(SMALL_NAMES, _unpack(g_small, [args[n].shape for n in SMALL_NAMES])))

    def view(a):
        return a.reshape(1, -1) if a.ndim == 1 else a.reshape(-1, a.shape[-1])

    for grp in SMALL_GROUPS:
        res = _adamw_multi(f"adamw_{grp[0]}", [view(g_small[n]) for n in grp], [view(args[n]) for n in grp],
                           [view(args["m_" + n]) for n in grp], [view(args["v_" + n]) for n in grp])
        for n, r in zip(grp, res):
            put(n, r, args[n].shape)
    ws = D // N_DEV
    tail = lax.dynamic_slice_in_dim(tail.reshape(N_DEV, 2, D), me * ws, ws, axis=2)
    res = _adamw("adamw_vec", [tail], vec_shard,
                 jnp.concatenate([m_s5_d, m_a_norm_mix], axis=0), jnp.concatenate([v_s5_d, v_a_norm_mix], axis=0))
    put("s5_d", [r[0:1] for r in res], s5_d.shape)
    put("a_norm_mix", [r[1:2] for r in res], a_norm_mix.shape)

    names = ("s5_lam_re", "s5_lam_im", "s5_log_dt", "s5_b_re", "s5_b_im", "s5_c_re", "s5_c_im", "s5_d",
             "s5_w_glu", "a_norm_mix", "ffn_norm", "ffn_w_in", "ffn_w_out", "b_norm_mix", "attn_w_q",
             "attn_w_o", "kv_norm", "w_kv", "final_norm")
    result = [loss, grad_x.reshape(x.shape)]
    for kind in ("grad", "delta", "new_m", "new_v"):
        result += [out[f"{kind}_{n}"] for n in names]
    return tuple(result)
```

```python
import functools
import math

import jax
import jax.numpy as jnp
from jax import lax
from jax.experimental import pallas as pl
from jax.experimental.pallas import tpu as pltpu

F32 = jnp.float32
BF16 = jnp.bfloat16

EPS = 1e-6
NEG_INF = -1e30
HEAD_DIM = 128
N_KV_HEADS = 4
DILATIONS = (1, 4, 16)
ATT_BLK = 128
S5_C = 16
S5_P = 64
S5_GB = 16
S5_CH = S5_GB * S5_C
S5_W = S5_GB * S5_P
S5_UNROLL = 4
N_DEV = 8

ADAM_LR = 0.001
ADAM_B1 = 0.9
ADAM_B2 = 0.999
ADAM_EPS = 1e-08
ADAM_WD = 0.01
ADAM_STEP = 10

VMEM_LIMIT_BYTES = 56 * 1024 * 1024
MM_TILE = 1024
MM_TILE_NARROW = 512
MM_DEPTH = 2816
MESH = pl.DeviceIdType.MESH
ANY = pl.BlockSpec(memory_space=pl.ANY)


def _tile(n, pref, align=128):
    t = (min(pref, n) // align) * align
    while t >= align:
        if n % t == 0:
            return t
        t -= align
    return n


def _params(sem):
    return pltpu.CompilerParams(dimension_semantics=sem, vmem_limit_bytes=VMEM_LIMIT_BYTES)


def _sigmoid(x):
    return 1.0 / (1.0 + jnp.exp(-x))


NN = (((1,), (0,)), ((), ()))
NT = (((1,), (1,)), ((), ()))
TN = (((0,), (0,)), ((), ()))


def _dot(a, b, dims=NN):
    return lax.dot_general(a, b, dims, preferred_element_type=F32)


def _matmul(name, grid, ins, in_specs, products, dims, out_shapes, out_specs, acc_shapes, epilogue):
    n_in, n_out, nk = len(ins), len(out_shapes), grid[2]

    def body(*refs):
        in_refs = refs[:n_in]
        out_refs = refs[n_in:n_in + n_out]
        acc_refs = refs[n_in + n_out:]

        def prods():
            vals = [None] * len(acc_shapes)
            for ai, bi, ci in products:
                d = _dot(in_refs[ai][...].astype(BF16), in_refs[bi][...].astype(BF16), dims)
                vals[ci] = d if vals[ci] is None else vals[ci] + d
            return vals

        if nk == 1:
            epilogue(in_refs, out_refs, prods())
        else:
            k = pl.program_id(2)

            @pl.when(k == 0)
            def _():
                for a in acc_refs:
                    a[...] = jnp.zeros_like(a)

            for a, v in zip(acc_refs, prods()):
                a[...] += v

            @pl.when(k == nk - 1)
            def _():
                epilogue(in_refs, out_refs, [a[...] for a in acc_refs])

    scratch = [] if nk == 1 else [pltpu.VMEM(s, F32) for s in acc_shapes]
    return pl.pallas_call(
        body, grid=grid, in_specs=in_specs, out_specs=out_specs, out_shape=out_shapes,
        scratch_shapes=scratch, compiler_params=_params(("parallel", "parallel", "arbitrary")),
        name=name)(*ins)


def _mm_dual_fwd(name, a, w, res, kind):
    T, K = a.shape
    N = w.shape[1] // 2
    tm, tn = _tile(T, MM_TILE), _tile(N, MM_TILE_NARROW)
    nj = N // tn
    grid = (T // tm, nj, 1)
    ins = [a, w, w]
    in_specs = [pl.BlockSpec((tm, K), lambda i, j, k: (i, 0)),
                pl.BlockSpec((K, tn), lambda i, j, k: (0, j)),
                pl.BlockSpec((K, tn), lambda i, j, k: (0, j + nj))]
    pair_spec = pl.BlockSpec((2, tm, tn), lambda i, j, k: (0, i, j))
    tile_spec = pl.BlockSpec((tm, tn), lambda i, j, k: (i, j))
    if kind == "glu":
        ins.append(res)
        in_specs.append(tile_spec)

        def epilogue(in_refs, out_refs, accs):
            val, gate = accs
            s = _sigmoid(gate)
            out_refs[0][...] = in_refs[3][...] + val * s
            out_refs[1][0] = s.astype(BF16)
            out_refs[1][1] = (val * s * (1.0 - s)).astype(BF16)

        out_shapes = [jax.ShapeDtypeStruct((T, N), F32), jax.ShapeDtypeStruct((2, T, N), BF16)]
        out_specs = [tile_spec, pair_spec]
    else:
        def epilogue(in_refs, out_refs, accs):
            g, u = accs
            s = _sigmoid(g)
            silu = g * s
            out_refs[0][0] = (u * (s * (1.0 + g * (1.0 - s)))).astype(BF16)
            out_refs[0][1] = silu.astype(BF16)
            out_refs[1][...] = (silu * u).astype(BF16)

        out_shapes = [jax.ShapeDtypeStruct((2, T, N), BF16), jax.ShapeDtypeStruct((T, N), BF16)]
        out_specs = [pair_spec, tile_spec]
    return _matmul(name, grid, ins, in_specs, [(0, 1, 0), (0, 2, 1)], NN, out_shapes, out_specs,
                   [(tm, tn), (tm, tn)], epilogue)


def _mm_kv(name, a, w):
    T, K = a.shape
    N = w.shape[1] // 2
    tm, tn = _tile(T, MM_TILE), _tile(N, MM_TILE_NARROW)
    nj = N // tn
    tile_spec = pl.BlockSpec((tm, tn), lambda i, j, k: (i, j))

    def epilogue(in_refs, out_refs, accs):
        out_refs[0][...] = accs[0]
        out_refs[1][...] = accs[1]

    return _matmul(name, (T // tm, nj, 1), [a, w, w],
                   [pl.BlockSpec((tm, K), lambda i, j, k: (i, 0)),
                    pl.BlockSpec((K, tn), lambda i, j, k: (0, j)),
                    pl.BlockSpec((K, tn), lambda i, j, k: (0, j + nj))],
                   [(0, 1, 0), (0, 2, 1)], NN,
                   [jax.ShapeDtypeStruct((T, N), F32)] * 2, [tile_spec, tile_spec],
                   [(tm, tn), (tm, tn)], epilogue)


def _mm_nn(name, a, w, res=None, out_dtype=F32):
    T, K = a.shape
    N = w.shape[1]
    tk = K if K <= 2 * MM_DEPTH else _tile(K, MM_DEPTH)
    tm, tn = _tile(T, MM_TILE), _tile(N, MM_TILE if K <= MM_DEPTH else MM_TILE_NARROW)
    grid = (T // tm, N // tn, K // tk)
    tile_spec = pl.BlockSpec((tm, tn), lambda i, j, k: (i, j))
    ins = [a, w]
    in_specs = [pl.BlockSpec((tm, tk), lambda i, j, k: (i, k)),
                pl.BlockSpec((tk, tn), lambda i, j, k: (k, j))]
    if res is not None:
        ins.append(res)
        in_specs.append(tile_spec)

    def epilogue(in_refs, out_refs, accs):
        v = accs[0]
        if res is not None:
            v = v + in_refs[2][...]
        out_refs[0][...] = v.astype(out_dtype)

    return _matmul(name, grid, ins, in_specs, [(0, 1, 0)], NN,
                   [jax.ShapeDtypeStruct((T, N), out_dtype)], [tile_spec], [(tm, tn)], epilogue)[0]


def _dep_operand(ins, in_specs, dep):
    if dep is not None:
        ins.append(dep)
        in_specs.append(pl.BlockSpec((8, 128), lambda *_: (0, 0)))


def _mm_nt(name, a_list, w, out_dtype=F32, dep=None):
    T, Np = a_list[0].shape
    Ko = w.shape[0]
    n_parts = len(a_list)
    wide_a = a_list[0].dtype != BF16
    tm, tn, tk = _tile(T, MM_TILE_NARROW if wide_a else MM_TILE), _tile(Ko, MM_TILE), _tile(Np, MM_DEPTH)
    nkp = Np // tk
    grid = (T // tm, Ko // tn, nkp)
    ins = list(a_list) + [w] * n_parts
    in_specs = [pl.BlockSpec((tm, tk), lambda i, j, k: (i, k)) for _ in a_list]
    in_specs += [pl.BlockSpec((tn, tk), functools.partial(lambda i, j, k, p: (j, p * nkp + k), p=p))
                 for p in range(n_parts)]
    products = [(p, n_parts + p, 0) for p in range(n_parts)]
    _dep_operand(ins, in_specs, dep)

    def epilogue(in_refs, out_refs, accs):
        out_refs[0][...] = accs[0].astype(out_dtype)

    return _matmul(name, grid, ins, in_specs, products, NT,
                   [jax.ShapeDtypeStruct((T, Ko), out_dtype)],
                   [pl.BlockSpec((tm, tn), lambda i, j, k: (i, j))], [(tm, tn)], epilogue)[0]


def _mm_nt_pair(name, a3, w, out_dtype=F32):
    _, T, N = a3.shape
    Ko = w.shape[0]
    tm, tn, tk = _tile(T, MM_TILE), _tile(Ko, MM_TILE), _tile(N, MM_DEPTH)
    nkh = N // tk
    grid = (T // tm, Ko // tn, 2 * nkh)

    def epilogue(in_refs, out_refs, accs):
        out_refs[0][...] = accs[0].astype(out_dtype)

    return _matmul(name, grid, [a3, w],
                   [pl.BlockSpec((None, tm, tk), lambda i, j, k: (k // nkh, i, k % nkh)),
                    pl.BlockSpec((tn, tk), lambda i, j, k: (j, k))],
                   [(0, 1, 0)], NT, [jax.ShapeDtypeStruct((T, Ko), out_dtype)],
                   [pl.BlockSpec((tm, tn), lambda i, j, k: (i, j))], [(tm, tn)], epilogue)[0]


def _mm_nt_ffn_bwd(name, dx, w_out, gu, dep=None):
    T, D = dx.shape
    Fh = w_out.shape[0]
    tm, tn = _tile(T, MM_TILE), _tile(Fh, MM_TILE_NARROW)
    n_dep = 0 if dep is None else 1

    def body(dx_ref, w_ref, gu_ref, *rest):
        out_ref = rest[n_dep]
        da = _dot(dx_ref[...], w_ref[...], NT)
        out_ref[0] = (da * gu_ref[0].astype(F32)).astype(BF16)
        out_ref[1] = (da * gu_ref[1].astype(F32)).astype(BF16)

    pair_spec = pl.BlockSpec((2, tm, tn), lambda i, j: (0, i, j))
    ins = [dx, w_out, gu]
    in_specs = [pl.BlockSpec((tm, D), lambda i, j: (i, 0)),
                pl.BlockSpec((tn, D), lambda i, j: (j, 0)),
                pair_spec]
    _dep_operand(ins, in_specs, dep)
    return pl.pallas_call(
        body, grid=(T // tm, Fh // tn), in_specs=in_specs, out_specs=pair_spec,
        out_shape=jax.ShapeDtypeStruct((2, T, Fh), BF16),
        compiler_params=_params(("parallel", "parallel")), name=name)(*ins)


def _mm_tn(name, a, d):
    T, Ko = a.shape
    N = d.shape[1]
    to, tn, tk = _tile(Ko, MM_TILE_NARROW), _tile(N, MM_TILE if d.dtype == BF16 else MM_TILE_NARROW), T
    grid = (Ko // to, N // tn, T // tk)

    def epilogue(in_refs, out_refs, accs):
        out_refs[0][...] = accs[0].astype(BF16)

    return _matmul(name, grid, [a, d],
                   [pl.BlockSpec((tk, to), lambda i, j, k: (k, i)),
                    pl.BlockSpec((tk, tn), lambda i, j, k: (k, j))],
                   [(0, 1, 0)], TN, [jax.ShapeDtypeStruct((Ko, N), BF16)],
                   [pl.BlockSpec((to, tn), lambda i, j, k: (i, j))], [(to, tn)], epilogue)[0]


def _mm_tn_pair(name, a, d3):
    T, Ko = a.shape
    N = d3.shape[2]
    to, tn, tk = _tile(Ko, MM_TILE), _tile(N, MM_TILE_NARROW), T
    njh = N // tn
    grid = (Ko // to, 2 * njh, T // tk)

    def epilogue(in_refs, out_refs, accs):
        out_refs[0][...] = accs[0].astype(BF16)

    return _matmul(name, grid, [a, d3],
                   [pl.BlockSpec((tk, to), lambda i, j, k: (k, i)),
                    pl.BlockSpec((None, tk, tn), lambda i, j, k: (j // njh, k, j % njh))],
                   [(0, 1, 0)], TN, [jax.ShapeDtypeStruct((Ko, 2 * N), BF16)],
                   [pl.BlockSpec((to, tn), lambda i, j, k: (i, j))], [(to, tn)], epilogue)[0]


def _rms_fwd(name, x, gains, dtypes, dep=None):
    T, D = x.shape
    n = len(gains)
    tr = _tile(T, 512, 8)
    n_dep = 0 if dep is None else 1

    def body(x_ref, *refs):
        xv = x_ref[...]
        xr = xv * lax.rsqrt(jnp.mean(xv * xv, axis=-1, keepdims=True) + EPS)
        for g_ref, o_ref in zip(refs[:n], refs[n + n_dep:]):
            o_ref[...] = (xr * g_ref[...]).astype(o_ref.dtype)

    row = pl.BlockSpec((tr, D), lambda i: (i, 0))
    vec = pl.BlockSpec((1, D), lambda i: (0, 0))
    ins, in_specs = [x, *gains], [row] + [vec] * n
    _dep_operand(ins, in_specs, dep)
    return pl.pallas_call(
        body, grid=(T // tr,), in_specs=in_specs, out_specs=[row] * n,
        out_shape=[jax.ShapeDtypeStruct((T, D), dt) for dt in dtypes],
        compiler_params=_params(("parallel",)), name=name)(*ins)


def _rms_bwd(name, x, dres, gains, dhs, dep=None):
    T, D = x.shape
    n = len(gains)
    tr = _tile(T, 256, 8)
    n_dep = 0 if dep is None else 1

    def body(x_ref, dres_ref, *refs):
        g_refs, dh_refs = refs[:n], refs[n:2 * n]
        dx_ref, dxb_ref = refs[2 * n + n_dep], refs[2 * n + n_dep + 1]
        dg_refs = refs[2 * n + n_dep + 2:]
        xv = x_ref[...]
        r = lax.rsqrt(jnp.mean(xv * xv, axis=-1, keepdims=True) + EPS)
        xr = xv * r
        w = None
        for g_ref, dh_ref, dg_ref in zip(g_refs, dh_refs, dg_refs):
            dh = dh_ref[...].astype(F32)

            @pl.when(pl.program_id(0) == 0)
            def _():
                dg_ref[...] = jnp.zeros_like(dg_ref)

            dg_ref[...] += jnp.sum(dh * xr, axis=0, keepdims=True)
            wi = dh * g_ref[...]
            w = wi if w is None else w + wi
        dx = dres_ref[...] + r * (w - xr * jnp.mean(w * xr, axis=-1, keepdims=True))
        dx_ref[...] = dx
        dxb_ref[...] = dx.astype(BF16)

    row = pl.BlockSpec((tr, D), lambda i: (i, 0))
    vec = pl.BlockSpec((1, D), lambda i: (0, 0))
    ins, in_specs = [x, dres, *gains, *dhs], [row, row] + [vec] * n + [row] * n
    _dep_operand(ins, in_specs, dep)
    outs = pl.pallas_call(
        body, grid=(T // tr,), in_specs=in_specs,
        out_specs=[row, row] + [vec] * n,
        out_shape=[jax.ShapeDtypeStruct((T, D), F32), jax.ShapeDtypeStruct((T, D), BF16)]
        + [jax.ShapeDtypeStruct((1, D), F32)] * n,
        compiler_params=_params(("arbitrary",)), name=name)(*ins)
    return outs[0], outs[1], outs[2:]


def _loss_head(x, gain, target):
    T, D = x.shape
    tr = _tile(T, 256, 8)

    def body(x_ref, g_ref, t_ref, loss_ref, dx_ref, dxb_ref, dg_ref):
        @pl.when(pl.program_id(0) == 0)
        def _():
            loss_ref[...] = jnp.zeros_like(loss_ref)
            dg_ref[...] = jnp.zeros_like(dg_ref)

        xv = x_ref[...]
        r = lax.rsqrt(jnp.mean(xv * xv, axis=-1, keepdims=True) + EPS)
        xr = xv * r
        err = xr * g_ref[...] - t_ref[...]
        part = jnp.sum(jnp.sum(err * err, axis=-1, keepdims=True), axis=0, keepdims=True) * (0.5 / D)
        loss_ref[...] += jnp.broadcast_to(part, loss_ref.shape)
        dy = err * (1.0 / D)
        dg_ref[...] += jnp.sum(dy * xr, axis=0, keepdims=True)
        w = dy * g_ref[...]
        dx = r * (w - xr * jnp.mean(w * xr, axis=-1, keepdims=True))
        dx_ref[...] = dx
        dxb_ref[...] = dx.astype(BF16)

    row = pl.BlockSpec((tr, D), lambda i: (i, 0))
    vec = pl.BlockSpec((1, D), lambda i: (0, 0))
    return pl.pallas_call(
        body, grid=(T // tr,), in_specs=[row, vec, row],
        out_specs=[pl.BlockSpec((8, 128), lambda i: (0, 0)), row, row, vec],
        out_shape=[jax.ShapeDtypeStruct((8, 128), F32), jax.ShapeDtypeStruct((T, D), F32),
                   jax.ShapeDtypeStruct((T, D), BF16), jax.ShapeDtypeStruct((1, D), F32)],
        compiler_params=_params(("arbitrary",)), name="loss_head")(x, gain, target)


def _glu_bwd(dmix, vg, dep=None):
    T, N = dmix.shape
    tr, tc = _tile(T, 512, 8), _tile(N, 1024)
    n_dep = 0 if dep is None else 1

    def body(d_ref, vg_ref, *refs):
        o_ref = refs[n_dep]
        d = d_ref[...]
        o_ref[0] = (d * vg_ref[0].astype(F32)).astype(BF16)
        o_ref[1] = (d * vg_ref[1].astype(F32)).astype(BF16)

    pair = pl.BlockSpec((2, tr, tc), lambda i, j: (0, i, j))
    ins, in_specs = [dmix, vg], [pl.BlockSpec((tr, tc), lambda i, j: (i, j)), pair]
    _dep_operand(ins, in_specs, dep)
    return pl.pallas_call(
        body, grid=(T // tr, N // tc), in_specs=in_specs,
        out_specs=pair, out_shape=jax.ShapeDtypeStruct((2, T, N), BF16),
        compiler_params=_params(("parallel", "parallel")), name="glu_bwd")(*ins)


def _to_state_tiles(x_ref, s, val):
    tc = val.shape[0]
    for j in range(S5_W // 128):
        x_ref[s, pl.ds(j, tc, stride=8), :] = val[:, 128 * j:128 * (j + 1)]


def _from_state_tiles(x_ref, s, tc):
    return jnp.concatenate([x_ref[s, pl.ds(j, tc, stride=8), :] for j in range(S5_W // 128)], axis=1)


def _s5_scan_fwd(xr_ref, xi_ref, ar_ref, ai_ref, cr_ref, ci_ref, tc, nblk):
    a = [(ar_ref[s], ai_ref[s]) for s in range(nblk)]

    def step(i, carry):
        carry = list(carry)
        for uu in range(S5_UNROLL):
            r0 = pl.multiple_of((i * S5_UNROLL + uu) * 8, 8)
            for s in range(nblk):
                cr, ci = carry[2 * s], carry[2 * s + 1]
                a_r, a_i = a[s]
                xr = a_r * cr - a_i * ci + xr_ref[s, pl.ds(r0, 8), :]
                xi = a_r * ci + a_i * cr + xi_ref[s, pl.ds(r0, 8), :]
                xr_ref[s, pl.ds(r0, 8), :] = xr
                xi_ref[s, pl.ds(r0, 8), :] = xi
                carry[2 * s], carry[2 * s + 1] = xr, xi
        return tuple(carry)

    init = []
    for s in range(nblk):
        init += [cr_ref[s], ci_ref[s]]
    out = lax.fori_loop(0, tc // S5_UNROLL, step, tuple(init))
    for s in range(nblk):
        cr_ref[s] = out[2 * s]
        ci_ref[s] = out[2 * s + 1]


def _s5_scan_bwd(lr_ref, li_ref, xr_ref, xi_ref, h_ref, ar_ref, ai_ref, cr_ref, ci_ref,
                 accr_ref, acci_ref, tc, nblk):
    a = [(ar_ref[s], ai_ref[s]) for s in range(nblk)]

    def one(s, r0, prev_r, prev_i, st):
        c_r, c_i, d_r, d_i = st
        a_r, a_i = a[s]
        l_r = lr_ref[s, pl.ds(r0, 8), :] + a_r * c_r + a_i * c_i
        l_i = li_ref[s, pl.ds(r0, 8), :] + a_r * c_i - a_i * c_r
        lr_ref[s, pl.ds(r0, 8), :] = l_r
        li_ref[s, pl.ds(r0, 8), :] = l_i
        return [l_r, l_i, d_r + l_r * prev_r + l_i * prev_i, d_i - l_r * prev_i + l_i * prev_r]

    def step(i, carry):
        carry = list(carry)
        for uu in range(S5_UNROLL):
            t = tc - 1 - (i * S5_UNROLL + uu)
            r0 = pl.multiple_of(t * 8, 8)
            p0 = pl.multiple_of((t - 1) * 8, 8)
            for s in range(nblk):
                carry[4 * s:4 * s + 4] = one(s, r0, xr_ref[s, pl.ds(p0, 8), :], xi_ref[s, pl.ds(p0, 8), :],
                                             carry[4 * s:4 * s + 4])
        return tuple(carry)

    init = []
    for s in range(nblk):
        init += [cr_ref[s], ci_ref[s], accr_ref[s], acci_ref[s]]
    carry = list(lax.fori_loop(0, tc // S5_UNROLL - 1, step, tuple(init)))
    for t in range(S5_UNROLL - 1, -1, -1):
        for s in range(nblk):
            if t > 0:
                prev_r, prev_i = xr_ref[s, 8 * (t - 1):8 * t, :], xi_ref[s, 8 * (t - 1):8 * t, :]
            else:
                prev_r, prev_i = h_ref[0, s], h_ref[1, s]
            carry[4 * s:4 * s + 4] = one(s, 8 * t, prev_r, prev_i, carry[4 * s:4 * s + 4])
    for s in range(nblk):
        cr_ref[s], ci_ref[s], accr_ref[s], acci_ref[s] = carry[4 * s:4 * s + 4]


def _gelu(y):
    k = math.sqrt(2.0 / math.pi)
    return 0.5 * y * (1.0 + jnp.tanh(k * (y + 0.044715 * (y * y * y))))


def _gelu_grad(y):
    k = math.sqrt(2.0 / math.pi)
    t = jnp.tanh(k * (y + 0.044715 * (y * y * y)))
    return 0.5 * (1.0 + t) + 0.5 * y * (1.0 - t * t) * (k * (1.0 + 3.0 * 0.044715 * (y * y)))


def _s5_specs(tc, nch, sbk, rev):
    def ch(c):
        return nch - 1 - c if rev else c

    return dict(
        act=pl.BlockSpec((tc, sbk * S5_CH), lambda i, c: (ch(c), i)),
        bb=pl.BlockSpec((2, sbk, S5_CH, 128), lambda i, c: (0, i, 0, 0)),
        cc=pl.BlockSpec((2, sbk, S5_P, S5_CH), lambda i, c: (0, i, 0, 0)),
        a=pl.BlockSpec((sbk, 8, 128), lambda i, c: (i, 0, 0)),
        d=pl.BlockSpec((1, sbk * S5_CH), lambda i, c: (0, i)),
        h=pl.BlockSpec((None, 2, sbk, 8, 128), lambda i, c: (ch(c), 0, i, 0, 0)),
    )


def _s5_blocks(nb, pref):
    return max(b for b in range(1, pref + 1) if nb % b == 0)


def _s5_group_masks():
    rb = lax.broadcasted_iota(jnp.int32, (S5_CH, S5_W), 0) // S5_C
    qb = lax.broadcasted_iota(jnp.int32, (S5_CH, S5_W), 1) // S5_P
    qc = lax.broadcasted_iota(jnp.int32, (S5_W, S5_CH), 0) // S5_P
    rc = lax.broadcasted_iota(jnp.int32, (S5_W, S5_CH), 1) // S5_C
    return rb == qb, qc == rc


def _s5_expand(bb_ref, cc_ref, bbd, ccd, sbk):
    mask_b, mask_c = _s5_group_masks()
    for k in range(2):
        for s in range(sbk):
            bbd[k, s] = jnp.where(mask_b, jnp.tile(bb_ref[k, s], (1, S5_W // 128)), 0).astype(BF16)
            ccd[k, s] = jnp.where(mask_c, jnp.tile(cc_ref[k, s], (S5_GB, 1)), 0).astype(BF16)


def _s5_fwd(u, bb2, cc2, a_re, a_im, d_skip):
    T, D = u.shape
    nb = D // S5_CH
    sbk = _s5_blocks(nb, 4)
    tc = _tile(T, 512, 8)
    nch = T // tc
    sp = _s5_specs(tc, nch, sbk, False)

    def body(u_ref, bb_ref, cc_ref, ar_ref, ai_ref, d_ref, z_ref, gp_ref, h_ref, xr, xi, cr, ci, bbd, ccd):
        @pl.when(pl.program_id(1) == 0)
        def _():
            cr[...] = jnp.zeros_like(cr)
            ci[...] = jnp.zeros_like(ci)
            _s5_expand(bb_ref, cc_ref, bbd, ccd, sbk)

        h_ref[0] = cr[...]
        h_ref[1] = ci[...]
        for s in range(sbk):
            ub = u_ref[:, s * S5_CH:(s + 1) * S5_CH].astype(BF16)
            _to_state_tiles(xr, s, _dot(ub, bbd[0, s]))
            _to_state_tiles(xi, s, _dot(ub, bbd[1, s]))
        _s5_scan_fwd(xr, xi, ar_ref, ai_ref, cr, ci, tc, sbk)
        for s in range(sbk):
            cols = slice(s * S5_CH, (s + 1) * S5_CH)
            y = (_dot(_from_state_tiles(xr, s, tc).astype(BF16), ccd[0, s])
                 - _dot(_from_state_tiles(xi, s, tc).astype(BF16), ccd[1, s])
                 + d_ref[:, cols] * u_ref[:, cols])
            z_ref[:, cols] = _gelu(y).astype(BF16)
            gp_ref[:, cols] = _gelu_grad(y).astype(BF16)

    tiles = pltpu.VMEM((sbk, tc * 8, 128), F32)
    carry = pltpu.VMEM((sbk, 8, 128), F32)
    return pl.pallas_call(
        body, grid=(nb // sbk, nch),
        in_specs=[sp["act"], sp["bb"], sp["cc"], sp["a"], sp["a"], sp["d"]],
        out_specs=[sp["act"], sp["act"], sp["h"]],
        out_shape=[jax.ShapeDtypeStruct((T, D), BF16), jax.ShapeDtypeStruct((T, D), BF16),
                   jax.ShapeDtypeStruct((nch, 2, nb, 8, 128), F32)],
        scratch_shapes=[tiles, tiles, carry, carry, pltpu.VMEM((2, sbk, S5_CH, S5_W), BF16),
                        pltpu.VMEM((2, sbk, S5_W, S5_CH), BF16)],
        compiler_params=_params(("parallel", "arbitrary")), name="s5_fwd",
    )(u, bb2, cc2, a_re, a_im, d_skip)


def _s5_bwd(u, dz, gp, h0, bb2, cc2, a_re, a_im, d_skip, dep=None):
    T, D = u.shape
    nb = D // S5_CH
    sbk = _s5_blocks(nb, 2)
    tc = _tile(T, 512, 8)
    nch = T // tc
    sp = _s5_specs(tc, nch, sbk, True)

    n_dep = 0 if dep is None else 1

    def body(u_ref, dz_ref, gp_ref, h_ref, bb_ref, cc_ref, ar_ref, ai_ref, d_ref, *rest):
        (du_ref, dd_ref, dar_ref, dai_ref, dbb_ref, dcc_ref,
         xr, xi, lr, li, fr, fi, br, bi, accr, acci, bbd, ccd, dbbd, dccd) = rest[n_dep:]
        c = pl.program_id(1)

        @pl.when(c == 0)
        def _():
            for ref in (br, bi, accr, acci, dd_ref, dbbd, dccd):
                ref[...] = jnp.zeros_like(ref)
            _s5_expand(bb_ref, cc_ref, bbd, ccd, sbk)

        for s in range(sbk):
            ub = u_ref[:, s * S5_CH:(s + 1) * S5_CH].astype(BF16)
            _to_state_tiles(xr, s, _dot(ub, bbd[0, s]))
            _to_state_tiles(xi, s, _dot(ub, bbd[1, s]))
        fr[...] = h_ref[0]
        fi[...] = h_ref[1]
        _s5_scan_fwd(xr, xi, ar_ref, ai_ref, fr, fi, tc, sbk)
        for s in range(sbk):
            cols = slice(s * S5_CH, (s + 1) * S5_CH)
            uv = u_ref[:, cols]
            xrb = _from_state_tiles(xr, s, tc).astype(BF16)
            xib = _from_state_tiles(xi, s, tc).astype(BF16)
            dsk = d_ref[:, cols]
            dy = dz_ref[:, cols].astype(F32) * gp_ref[:, cols].astype(F32)
            dd_ref[:, cols] += jnp.sum(dy * uv, axis=0, keepdims=True)
            dyb = dy.astype(BF16)
            dccd[0, s] += _dot(xrb, dyb, TN)
            dccd[1, s] += _dot(xib, dyb, TN)
            _to_state_tiles(lr, s, _dot(dyb, ccd[0, s], NT))
            _to_state_tiles(li, s, -_dot(dyb, ccd[1, s], NT))
            du_ref[:, cols] = dy * dsk
        _s5_scan_bwd(lr, li, xr, xi, h_ref, ar_ref, ai_ref, br, bi, accr, acci, tc, sbk)
        for s in range(sbk):
            cols = slice(s * S5_CH, (s + 1) * S5_CH)
            ub = u_ref[:, cols].astype(BF16)
            lrb = _from_state_tiles(lr, s, tc).astype(BF16)
            lib = _from_state_tiles(li, s, tc).astype(BF16)
            dbbd[0, s] += _dot(ub, lrb, TN)
            dbbd[1, s] += _dot(ub, lib, TN)
            du_ref[:, cols] += _dot(lrb, bbd[0, s], NT) + _dot(lib, bbd[1, s], NT)

        @pl.when(c == nch - 1)
        def _():
            dar_ref[...] = accr[...]
            dai_ref[...] = acci[...]
            mask_b, mask_c = _s5_group_masks()
            for k in range(2):
                for s in range(sbk):
                    mb = jnp.where(mask_b, dbbd[k, s], 0.0)
                    fold = functools.reduce(
                        lambda a, b: a + b, [mb[:, 128 * j:128 * (j + 1)] for j in range(S5_W // 128)])
                    dbb_ref[k, s] = fold + pltpu.roll(fold, S5_P, 1)
                    mc = jnp.where(mask_c, dccd[k, s], 0.0)
                    dcc_ref[k, s] = functools.reduce(
                        lambda a, b: a + b, [mc[S5_P * j:S5_P * (j + 1), :] for j in range(S5_GB)])

    tiles = pltpu.VMEM((sbk, tc * 8, 128), F32)
    carry = pltpu.VMEM((sbk, 8, 128), F32)
    ins = [u, dz, gp, h0, bb2, cc2, a_re, a_im, d_skip]
    in_specs = [sp["act"], sp["act"], sp["act"], sp["h"], sp["bb"], sp["cc"], sp["a"], sp["a"], sp["d"]]
    _dep_operand(ins, in_specs, dep)
    return pl.pallas_call(
        body, grid=(nb // sbk, nch),
        in_specs=in_specs,
        out_specs=[sp["act"], sp["d"], sp["a"], sp["a"], sp["bb"], sp["cc"]],
        out_shape=[jax.ShapeDtypeStruct((T, D), F32), jax.ShapeDtypeStruct((1, D), F32),
                   jax.ShapeDtypeStruct((nb, 8, 128), F32), jax.ShapeDtypeStruct((nb, 8, 128), F32),
                   jax.ShapeDtypeStruct((2, nb, S5_CH, 128), F32), jax.ShapeDtypeStruct((2, nb, S5_P, S5_CH), F32)],
        scratch_shapes=[tiles, tiles, tiles, tiles, carry, carry, carry, carry, carry, carry,
                        pltpu.VMEM((2, sbk, S5_CH, S5_W), BF16), pltpu.VMEM((2, sbk, S5_W, S5_CH), BF16),
                        pltpu.VMEM((2, sbk, S5_CH, S5_W), F32), pltpu.VMEM((2, sbk, S5_W, S5_CH), F32)],
        compiler_params=_params(("parallel", "arbitrary")), name="s5_bwd",
    )(*ins)


def _s5_disc(lr, li, ldt):
    dt = jnp.exp(ldt)
    mag = jnp.exp(lr * dt)
    ang = li * dt
    cs, sn = jnp.cos(ang), jnp.sin(ang)
    lbr, lbi = mag * cs, mag * sn
    nr = lbr - 1.0
    den = lr * lr + li * li
    f_re = (nr * lr + lbi * li) / den
    f_im = (lbi * lr - nr * li) / den
    return dt, mag, cs, sn, lbr, lbi, nr, den, f_re, f_im


def _s5_param_fwd(lr, li, ldt, bt_re, bt_im):
    c, g, p = bt_re.shape

    def body(lr_ref, li_ref, ldt_ref, br_ref, bi_ref, lbr_ref, lbi_ref, bbr_ref, bbi_ref):
        _, _, _, _, lbr, lbi, _, _, f_re, f_im = _s5_disc(lr_ref[...], li_ref[...], ldt_ref[...])
        lbr_ref[...] = lbr
        lbi_ref[...] = lbi
        for ch in range(c):
            b_r, b_i = br_ref[ch], bi_ref[ch]
            bbr_ref[ch] = f_re * b_r - f_im * b_i
            bbi_ref[ch] = f_re * b_i + f_im * b_r

    gp = jax.ShapeDtypeStruct((g, p), F32)
    cgp = jax.ShapeDtypeStruct((c, g, p), F32)
    return pl.pallas_call(body, out_shape=[gp, gp, cgp, cgp], name="s5_param_fwd")(lr, li, ldt, bt_re, bt_im)


def _s5_param_bwd(lr, li, ldt, bt_re, bt_im, dlbr, dlbi, dbbr, dbbi):
    c, g, p = bt_re.shape

    def body(lr_ref, li_ref, ldt_ref, br_ref, bi_ref, dlbr_ref, dlbi_ref, dbbr_ref, dbbi_ref,
             dlr_ref, dli_ref, dldt_ref, dbr_ref, dbi_ref):
        l_r, l_i = lr_ref[...], li_ref[...]
        dt, mag, cs, sn, lbr, lbi, nr, den, f_re, f_im = _s5_disc(l_r, l_i, ldt_ref[...])
        dfr = jnp.zeros_like(l_r)
        dfi = jnp.zeros_like(l_r)
        for ch in range(c):
            b_r, b_i = br_ref[ch], bi_ref[ch]
            g_r, g_i = dbbr_ref[ch], dbbi_ref[ch]
            dbr_ref[ch] = f_re * g_r + f_im * g_i
            dbi_ref[ch] = f_re * g_i - f_im * g_r
            dfr = dfr + g_r * b_r + g_i * b_i
            dfi = dfi + g_i * b_r - g_r * b_i
        inv = 1.0 / den
        d_nr = (dfr * l_r - dfi * l_i) * inv
        d_lbi = (dfr * l_i + dfi * l_r) * inv + dlbi_ref[...]
        d_lbr = d_nr + dlbr_ref[...]
        d_den = -(dfr * f_re + dfi * f_im) * inv
        d_mag = d_lbr * cs + d_lbi * sn
        d_ang = d_lbi * lbr - d_lbr * lbi
        dlr_ref[...] = (dfr * nr + dfi * lbi) * inv + 2.0 * d_den * l_r + d_mag * mag * dt
        dli_ref[...] = (dfr * lbi - dfi * nr) * inv + 2.0 * d_den * l_i + d_ang * dt
        dldt_ref[...] = jnp.sum(d_mag * mag * l_r + d_ang * l_i, axis=1, keepdims=True) * dt

    gp = jax.ShapeDtypeStruct((g, p), F32)
    cgp = jax.ShapeDtypeStruct((c, g, p), F32)
    return pl.pallas_call(body, out_shape=[gp, gp, jax.ShapeDtypeStruct((g, 1), F32), cgp, cgp],
                          name="s5_param_bwd")(lr, li, ldt, bt_re, bt_im, dlbr, dlbi, dbbr, dbbi)


def _att_masks(rep, gb):
    rows = rep * ATT_BLK
    qi = lax.broadcasted_iota(jnp.int32, (rows, 2 * ATT_BLK), 0) % ATT_BLK
    si = lax.broadcasted_iota(jnp.int32, (rows, 2 * ATT_BLK), 1)
    prev = (si < ATT_BLK) & (si >= qi) & (gb > 0)
    cur = (si >= ATT_BLK) & (si - ATT_BLK <= qi)
    return prev | cur


def _att_rows(start, dil):
    return pl.ds(start, ATT_BLK) if dil == 1 else pl.ds(start, ATT_BLK, stride=dil)


def _att_plan(T, dil):
    span = ATT_BLK * dil
    sbr = max(span, min(T, 1024))
    return span, sbr, T // sbr


def _att_block(sb, i, sbr, span, dil):
    loc = (i // dil) * span + i % dil
    cur = sb * sbr + loc
    gb = sb * (sbr // span) + i // dil
    return loc, cur, jnp.where(gb > 0, cur - span, cur), gb


def _att_fwd(q, k, v, grp, dil):
    T = q.shape[0]
    H = q.shape[1] // HEAD_DIM // len(DILATIONS)
    rep = H // N_KV_HEADS
    span, sbr, nsb = _att_plan(T, dil)
    scale = HEAD_DIM ** -0.5

    def body(*refs):
        q_refs = refs[:rep]
        k_ref, v_ref, o_ref, l_ref, o_slab, l_slab = refs[rep:]
        sb = pl.program_id(1)

        def blk(i, _):
            loc, cur, prv, gb = _att_block(sb, i, sbr, span, dil)
            rows = _att_rows(loc, dil)
            qs = jnp.concatenate([r[rows, :] for r in q_refs], axis=0).astype(BF16)
            kcat = jnp.concatenate([k_ref[_att_rows(prv, dil), :], k_ref[_att_rows(cur, dil), :]], axis=0)
            vcat = jnp.concatenate([v_ref[_att_rows(prv, dil), :], v_ref[_att_rows(cur, dil), :]], axis=0)
            s = jnp.where(_att_masks(rep, gb), _dot(qs, kcat.astype(BF16), NT) * scale, NEG_INF)
            m = jnp.max(s, axis=-1, keepdims=True)
            p = jnp.exp(s - m)
            l = jnp.sum(p, axis=-1, keepdims=True)
            o = _dot(p.astype(BF16), vcat.astype(BF16)) / l
            lse = jnp.broadcast_to(m + jnp.log(l), (rep * ATT_BLK, HEAD_DIM))
            for j in range(rep):
                o_slab[j, rows, :] = o[j * ATT_BLK:(j + 1) * ATT_BLK]
                l_slab[j, rows, :] = lse[j * ATT_BLK:(j + 1) * ATT_BLK]
            return 0

        lax.fori_loop(0, sbr // ATT_BLK, blk, 0)
        for j in range(rep):
            o_ref[:, j * HEAD_DIM:(j + 1) * HEAD_DIM] = o_slab[j].astype(BF16)
            l_ref[:, j * HEAD_DIM:(j + 1) * HEAD_DIM] = l_slab[j]

    qspecs = [pl.BlockSpec((sbr, HEAD_DIM), functools.partial(lambda h, s, j: (s, grp * H + h * rep + j), j=j))
              for j in range(rep)]
    kspec = pl.BlockSpec((T, HEAD_DIM), lambda h, s: (0, h))
    ospec = pl.BlockSpec((sbr, rep * HEAD_DIM), lambda h, s: (s, h))
    slab = pltpu.VMEM((rep, sbr, HEAD_DIM), F32)
    return pl.pallas_call(
        body, grid=(N_KV_HEADS, nsb), in_specs=qspecs + [kspec, kspec], out_specs=[ospec, ospec],
        out_shape=[jax.ShapeDtypeStruct((T, H * HEAD_DIM), BF16), jax.ShapeDtypeStruct((T, H * HEAD_DIM), F32)],
        scratch_shapes=[slab, slab],
        compiler_params=_params(("parallel", "arbitrary")), name=f"att_fwd_d{dil}",
    )(*([q] * rep), k, v)


def _att_combine(outs, lses):
    T, W = outs[0].shape
    ng = len(outs)
    tr, tcol = _tile(T, 512, 8), _tile(W, 512)

    def body(*refs):
        o_refs, l_refs = refs[:ng], refs[ng:2 * ng]
        ob_ref, lse_ref = refs[2 * ng:]
        ls = [r[...] for r in l_refs]
        m = functools.reduce(jnp.maximum, ls)
        es = [jnp.exp(l - m) for l in ls]
        den = functools.reduce(lambda a, b: a + b, es)
        num = functools.reduce(lambda a, b: a + b, [e * o[...].astype(F32) for e, o in zip(es, o_refs)])
        ob_ref[...] = (num / den).astype(BF16)
        lse_ref[...] = m + jnp.log(den)

    spec = pl.BlockSpec((tr, tcol), lambda i, j: (i, j))
    return pl.pallas_call(
        body, grid=(T // tr, W // tcol), in_specs=[spec] * (2 * ng), out_specs=[spec, spec],
        out_shape=[jax.ShapeDtypeStruct((T, W), BF16), jax.ShapeDtypeStruct((T, W), F32)],
        compiler_params=_params(("parallel", "parallel")), name="att_combine")(*outs, *lses)


STAT_LANE = HEAD_DIM // 2


def _att_stats(lse, o, do):
    T, W = lse.shape
    tr = _tile(T, 256, 16)

    def body(l_ref, o_ref, do_ref, s_ref):
        lane = lax.broadcasted_iota(jnp.int32, (tr, HEAD_DIM), 1)
        for h in range(W // HEAD_DIM):
            cols = slice(h * HEAD_DIM, (h + 1) * HEAD_DIM)
            delta = jnp.sum(do_ref[:, cols] * o_ref[:, cols].astype(F32), axis=-1, keepdims=True)
            s_ref[:, cols] = jnp.where(lane < STAT_LANE, l_ref[:, cols], delta)

    spec = pl.BlockSpec((tr, W), lambda i: (i, 0))
    return pl.pallas_call(
        body, grid=(T // tr,), in_specs=[spec] * 3, out_specs=spec,
        out_shape=jax.ShapeDtypeStruct((T, W), F32),
        compiler_params=_params(("parallel",)), name="att_stats")(lse, o, do)


def _att_bwd(q, k, v, do, stats, dq, grp, dil):
    T = q.shape[0]
    H = do.shape[1] // HEAD_DIM
    rep = H // N_KV_HEADS
    hs = rep
    span, sbr, nsb = _att_plan(T, dil)
    scale = HEAD_DIM ** -0.5

    def body(*refs):
        q_refs, do_refs, st_refs = refs[:hs], refs[hs + 2:2 * hs + 2], refs[2 * hs + 2:3 * hs + 2]
        k_ref, v_ref = refs[hs], refs[hs + 1]
        dq_ref, dk_ref, dv_ref, dq_slab = refs[3 * hs + 3:]
        sb = pl.program_id(2)

        @pl.when((pl.program_id(1) == 0) & (sb == 0))
        def _():
            dk_ref[...] = jnp.zeros_like(dk_ref)
            dv_ref[...] = jnp.zeros_like(dv_ref)

        def blk(i, _):
            loc, cur, prv, gb = _att_block(sb, i, sbr, span, dil)
            rows, kc, kp = _att_rows(loc, dil), _att_rows(cur, dil), _att_rows(prv, dil)
            qs = jnp.concatenate([r[rows, :] for r in q_refs], axis=0).astype(BF16)
            dos = jnp.concatenate([r[rows, :] for r in do_refs], axis=0).astype(BF16)
            st = jnp.concatenate([r[rows, :] for r in st_refs], axis=0)
            kcat = jnp.concatenate([k_ref[kp, :], k_ref[kc, :]], axis=0).astype(BF16)
            vcat = jnp.concatenate([v_ref[kp, :], v_ref[kc, :]], axis=0).astype(BF16)
            s = _dot(qs, kcat, NT) * scale
            p = jnp.where(_att_masks(hs, gb), jnp.exp(s - st[:, 0:1]), 0.0)
            dp = _dot(dos, vcat, NT)
            ds = (p * (dp - st[:, STAT_LANE:STAT_LANE + 1]) * scale).astype(BF16)
            dvc = _dot(p.astype(BF16), dos, TN)
            dkc = _dot(ds, qs, TN)
            dqs = _dot(ds, kcat)
            for j in range(hs):
                dq_slab[j, rows, :] = dqs[j * ATT_BLK:(j + 1) * ATT_BLK]
            dk_ref[kc, :] += dkc[ATT_BLK:]
            dv_ref[kc, :] += dvc[ATT_BLK:]

            @pl.when(gb > 0)
            def _():
                dk_ref[kp, :] += dkc[:ATT_BLK]
                dv_ref[kp, :] += dvc[:ATT_BLK]

            return 0

        lax.fori_loop(0, sbr // ATT_BLK, blk, 0)
        for j in range(hs):
            dq_ref[:, j * HEAD_DIM:(j + 1) * HEAD_DIM] = dq_slab[j].astype(BF16)

    def head_specs(col0):
        return [pl.BlockSpec((sbr, HEAD_DIM),
                             functools.partial(lambda h, f, s, j: (s, col0 + h * rep + f * hs + j), j=j))
                for j in range(hs)]

    kspec = pl.BlockSpec((T, HEAD_DIM), lambda h, f, s: (0, h))
    dqspec = pl.BlockSpec((sbr, hs * HEAD_DIM), lambda h, f, s: (s, (grp * H + h * rep) // hs + f))
    n_in = 3 * hs + 3
    return pl.pallas_call(
        body, grid=(N_KV_HEADS, rep // hs, nsb),
        in_specs=head_specs(grp * H) + [kspec, kspec] + head_specs(0) + head_specs(0) + [ANY],
        out_specs=[dqspec, kspec, kspec],
        out_shape=[jax.ShapeDtypeStruct(dq.shape, BF16),
                   jax.ShapeDtypeStruct((T, N_KV_HEADS * HEAD_DIM), F32),
                   jax.ShapeDtypeStruct((T, N_KV_HEADS * HEAD_DIM), F32)],
        scratch_shapes=[pltpu.VMEM((hs, sbr, HEAD_DIM), F32)],
        input_output_aliases={n_in - 1: 0},
        compiler_params=_params(("parallel", "arbitrary", "arbitrary")), name=f"att_bwd_d{dil}",
    )(*([q] * hs), k, v, *([do] * hs), *([stats] * hs), dq)


def _sum_kv(dks, dvs):
    T, W = dks[0].shape
    ng = len(dks)
    tr = _tile(T, 512, 8)

    def body(*refs):
        o_ref = refs[2 * ng]
        o_ref[0] = functools.reduce(lambda a, b: a + b, [r[...] for r in refs[:ng]]).astype(BF16)
        o_ref[1] = functools.reduce(lambda a, b: a + b, [r[...] for r in refs[ng:2 * ng]]).astype(BF16)

    spec = pl.BlockSpec((tr, W), lambda i: (i, 0))
    return pl.pallas_call(
        body, grid=(T // tr,), in_specs=[spec] * (2 * ng),
        out_specs=pl.BlockSpec((2, tr, W), lambda i: (0, i, 0)),
        out_shape=jax.ShapeDtypeStruct((2, T, W), BF16),
        compiler_params=_params(("parallel",)), name="sum_kv")(*dks, *dvs)


def _local_step(x, tgt, w, wts, ready, dep0=None, small_ready=None):
    T, D = x.shape
    g = {}

    (u0,) = _rms_fwd("rms_a", x, [w["a_norm"]], [F32], dep=dep0)
    lbr, lbi, bbt_re, bbt_im = _s5_param_fwd(w["lam_re"], w["lam_im"], w["log_dt"], w["bt_re"], w["bt_im"])
    a_re, a_im = lbr.reshape(-1, 8, 128), lbi.reshape(-1, 8, 128)
    C, G, P = w["bt_re"].shape
    nb = G // S5_GB
    bb2 = jnp.stack([bbt_re, bbt_im]).transpose(0, 2, 1, 3).reshape(2, nb, S5_GB * C, P)
    bb2 = jnp.concatenate([bb2, bb2], axis=-1)
    cc2 = jnp.stack([w["c_re"], w["c_im"]]).reshape(2, nb, S5_GB * C, P).transpose(0, 1, 3, 2)
    z, gp, h0 = _s5_fwd(u0, bb2, cc2, a_re, a_im, w["s5_d"])
    w_glu = wts("glu", z)
    x1, vg = _mm_dual_fwd("glu_fwd", z, w_glu, x, "glu")

    def ffn_fwd(xin, layer):
        (nrm,) = _rms_fwd(f"rms_f{layer}", xin, [w["ffn_norm"][layer:layer + 1]], [BF16])
        w_in = wts(f"win{layer}", nrm)
        gu, act = _mm_dual_fwd(f"ffn_in{layer}", nrm, w_in, None, "ffn")
        w_out = wts(f"wout{layer}", act)
        xout = _mm_nn(f"ffn_out{layer}", act, w_out, res=xin)
        return xout, (nrm, gu, act, w_in, w_out)

    x2, saved0 = ffn_fwd(x1, 0)
    kvn, hb = _rms_fwd("rms_b", x2, [w["kv_norm"], w["b_norm"]], [BF16, BF16])
    w_kv, w_q, w_o = wts("wkv", hb), wts("wq", hb), wts("wo", hb)
    k, v = _mm_kv("kv_proj", kvn, w_kv)
    q = _mm_nn("q_proj", hb, w_q)
    outs, lses = [], []
    for grp, dil in enumerate(DILATIONS):
        o_g, l_g = _att_fwd(q, k, v, grp, dil)
        outs.append(o_g)
        lses.append(l_g)
    o, lse = _att_combine(outs, lses)
    wts("win1", o, prefetch=True)
    x3 = _mm_nn("o_proj", o, w_o, res=x2)
    x4, saved1 = ffn_fwd(x3, 1)
    loss_blk, dx4, dx4b, g["final_norm"] = _loss_head(x4, w["final_norm"], tgt)

    def ffn_bwd(dx, dxb, xin, saved, layer, dep):
        nrm, gu, act, w_in, w_out = saved
        dgu = _mm_nt_ffn_bwd(f"ffn_dact{layer}", dxb, w_out, gu, dep=dep)
        g_wout = _mm_tn(f"ffn_dwout{layer}", act, dxb)
        g_win = _mm_tn_pair(f"ffn_dwin{layer}", nrm, dgu)
        dn = _mm_nt_pair(f"ffn_dn{layer}", dgu, w_in, out_dtype=BF16)
        dxo, dxob, (dgn,) = _rms_bwd(f"rms_f{layer}_bwd", xin, dx, [w["ffn_norm"][layer:layer + 1]], [dn])
        tok = ready({f"win{layer}": g_win, f"wout{layer}": g_wout})
        return dxo, dxob, dgn, tok

    dx3, dx3b, dfn1, tok = ffn_bwd(dx4, dx4b, x3, saved1, 1, None)
    do = _mm_nt("o_proj_dx", [dx3b], w_o, dep=tok)
    g_wo = _mm_tn("o_proj_dw", o, dx3b)
    stats = _att_stats(lse, o, do)
    dq = lax.empty(q.shape, BF16)
    dks, dvs = [], []
    for grp, dil in enumerate(DILATIONS):
        dq, dk_g, dv_g = _att_bwd(q, k, v, do, stats, dq, grp, dil)
        dks.append(dk_g)
        dvs.append(dv_g)
    dkv = _sum_kv(dks, dvs)
    dhb = _mm_nt("q_proj_dx", [dq], w_q, out_dtype=BF16)
    g_wq = _mm_tn("q_proj_dw", hb, dq)
    dkvn = _mm_nt_pair("kv_proj_dx", dkv, w_kv, out_dtype=BF16)
    g_wkv = _mm_tn_pair("kv_proj_dw", kvn, dkv)
    dx2, dx2b, (g["kv_norm"], g["b_norm"]) = _rms_bwd(
        "rms_b_bwd", x2, dx3, [w["kv_norm"], w["b_norm"]], [dkvn, dhb])
    tok = ready({"wkv": g_wkv, "wq": g_wq, "wo": g_wo})
    dx1, dx1b, dfn0, tok = ffn_bwd(dx2, dx2b, x1, saved0, 0, tok)
    g["ffn_norm"] = jnp.concatenate([dfn0, dfn1], axis=0)

    dvg = _glu_bwd(dx1, vg, dep=tok)
    dz = _mm_nt_pair("glu_dx", dvg, w_glu, out_dtype=BF16)
    tok = ready({"glu": _mm_tn_pair("glu_dw", z, dvg)})
    tok = ready({}, after=tok)
    du, g["s5_d"], da_re, da_im, dbb2, dcc2 = _s5_bwd(
        u0, dz, gp, h0, bb2, cc2, a_re, a_im, w["s5_d"], dep=tok)
    dcc = dcc2.transpose(0, 1, 3, 2).reshape(2, G, C, P)
    g["c_re"], g["c_im"] = dcc[0], -dcc[1]
    dbbt = dbb2[..., :P].reshape(2, G, C, P).transpose(0, 2, 1, 3)
    g["lam_re"], g["lam_im"], g["log_dt"], g["bt_re"], g["bt_im"] = _s5_param_bwd(
        w["lam_re"], w["lam_im"], w["log_dt"], w["bt_re"], w["bt_im"],
        da_re.reshape(G, P), da_im.reshape(G, P), dbbt[0], dbbt[1])
    tok = small_ready(g) if small_ready is not None else None
    grad_x, _, (g["a_norm"],) = _rms_bwd("rms_a_bwd", x, dx1, [w["a_norm"]], [du], dep=tok)
    return loss_blk, grad_x, g


def _coords():
    return lax.axis_index("x"), lax.axis_index("y"), lax.axis_index("c")


def _dev_index(dev):
    return 4 * dev[0] + 2 * dev[1] + dev[2]


def _shard_window(ref, axis, width, idx):
    sl = [slice(None)] * len(ref.shape)
    sl[axis] = pl.ds(pl.multiple_of(idx * width, width), width)
    return ref.at[tuple(sl)]


def _all_gather(name, shards, axes):
    na = len(shards)
    widths = [s.shape[ax] for s, ax in zip(shards, axes)]
    out_shapes = []
    for s, ax in zip(shards, axes):
        shp = list(s.shape)
        shp[ax] *= N_DEV
        out_shapes.append(jax.ShapeDtypeStruct(tuple(shp), s.dtype))

    def body(*refs):
        ins, outs = refs[:na], refs[na:2 * na]
        send_sems, recv_sems, local_sems = refs[2 * na:]
        x, y, c = _coords()
        me, sib = (x, y, c), (x, y, 1 - c)
        chips = [(1 - x, y), (x, 1 - y), (1 - x, 1 - y)]

        def blk(a, dev):
            return _shard_window(outs[a], axes[a], widths[a], _dev_index(dev))

        def copy(a, kk, block, to, src=None):
            return pltpu.make_async_remote_copy(
                src_ref=blk(a, block) if src is None else src, dst_ref=blk(a, block),
                send_sem=send_sems.at[a, kk], recv_sem=recv_sems.at[a, kk],
                device_id=to, device_id_type=MESH)

        local = [pltpu.make_async_copy(ins[a], blk(a, me), local_sems.at[a]) for a in range(na)]
        for cp in local:
            cp.start()
        sent = []
        for a in range(na):
            first = [copy(a, 0, me, sib, src=ins[a])]
            first += [copy(a, 1 + j, me, (*chip, c), src=ins[a]) for j, chip in enumerate(chips)]
            for cp in first:
                cp.start()
            sent += first
        for a in range(na):
            for j, chip in enumerate(chips):
                copy(a, 1 + j, (*chip, c), me).wait_recv()
                fwd = copy(a, 4 + j, (*chip, c), sib)
                fwd.start()
                sent.append(fwd)
        for a in range(na):
            copy(a, 0, sib, me).wait_recv()
            for j, chip in enumerate(chips):
                copy(a, 4 + j, (*chip, 1 - c), me).wait_recv()
        for cp in sent:
            cp.wait_send()
        for cp in local:
            cp.wait()

    return pl.pallas_call(
        body, out_shape=out_shapes, in_specs=[ANY] * na, out_specs=[ANY] * na,
        scratch_shapes=[pltpu.SemaphoreType.DMA((na, 7)), pltpu.SemaphoreType.DMA((na, 7)),
                        pltpu.SemaphoreType.DMA((na,))],
        name=name)(*shards)


HBM = pl.BlockSpec(memory_space=pltpu.HBM)
SEM = pl.BlockSpec(memory_space=pltpu.SEMAPHORE)
TOKEN_SPEC = pl.BlockSpec(memory_space=pltpu.VMEM)
TOKEN_SHAPE = jax.ShapeDtypeStruct((8, 128), F32)
SPLIT_PARAMS = pltpu.CompilerParams(has_side_effects=pltpu.SideEffectType.DATAFLOW_SIDE_EFFECTING)


def _hbm(x):
    return pltpu.with_memory_space_constraint(x, pltpu.HBM)


def _hbm_like(x):
    return pltpu.HBM(x.shape, x.dtype)


def _dma_sems(*shape):
    return pltpu.SemaphoreType.DMA(shape)


def _cast_and_place(name, shard, layer, axis, pos, dtype):
    rows, cols = shard.shape[-2:]
    tr = _tile(rows, 256, 16)
    nt = rows // tr
    full = (rows, cols * N_DEV) if axis == 1 else (rows * N_DEV, cols)

    def dev(p):
        return 4 * p[0] + 2 * p[1] + p[2]

    def body(pos_ref, s_ref, b_ref, l_ref):
        v = s_ref[...].astype(dtype)
        b_ref[...] = v
        l_ref[...] = v

    blk = pl.BlockSpec((tr, cols), lambda i, p: (i, 0))
    if axis == 1:
        lspec = pl.BlockSpec((tr, cols), lambda i, p: (i, dev(p)))
    else:
        lspec = pl.BlockSpec((tr, cols), lambda i, p: (dev(p) * nt + i, 0))
    return pl.pallas_call(
        body, grid_spec=pltpu.PrefetchScalarGridSpec(
            num_scalar_prefetch=1, grid=(nt,),
            in_specs=[pl.BlockSpec((None, tr, cols), lambda i, p: (layer, i, 0))], out_specs=[blk, lspec]),
        out_shape=[jax.ShapeDtypeStruct((rows, cols), dtype), jax.ShapeDtypeStruct(full, dtype)],
        compiler_params=_params(("parallel",)), name=name)(pos, shard)


def _gather_start(name, shards, lands, axes, groups):
    na, ng = len(shards), len(groups)
    widths = [s.shape[ax] for s, ax in zip(shards, axes)]

    def body(*refs):
        sh, ld = refs[:na], refs[na:2 * na]
        sems = refs[2 * na:2 * na + 3 * ng]
        token = refs[-1]
        x, y, c = _coords()
        me, sib = (x, y, c), (x, y, 1 - c)
        chips = [(1 - x, y), (x, 1 - y), (1 - x, 1 - y)]
        for gi, grp in enumerate(groups):
            send, r_d2d, r_ici = sems[3 * gi:3 * gi + 3]
            for li, a in enumerate(grp):
                dst = _shard_window(ld[a], axes[a], widths[a], _dev_index(me))
                pltpu.make_async_remote_copy(
                    src_ref=sh[a], dst_ref=dst, send_sem=send.at[4 * li], recv_sem=r_d2d.at[li],
                    device_id=sib, device_id_type=MESH).start()
                for j, chip in enumerate(chips):
                    pltpu.make_async_remote_copy(
                        src_ref=sh[a], dst_ref=dst, send_sem=send.at[4 * li + 1 + j], recv_sem=r_ici.at[3 * li + j],
                        device_id=(*chip, c), device_id_type=MESH).start()
        token[...] = jnp.zeros_like(token)

    out_shape, out_specs = [], []
    for grp in groups:
        out_shape += [_dma_sems(4 * len(grp)), _dma_sems(len(grp)), _dma_sems(3 * len(grp))]
        out_specs += [SEM] * 3
    out_shape += [_hbm_like(s) for s in shards] + [_hbm_like(l) for l in lands] + [TOKEN_SHAPE]
    out_specs += [HBM] * (2 * na) + [TOKEN_SPEC]
    aliases = {a: 3 * ng + a for a in range(2 * na)}
    res = pl.pallas_call(
        body, name=name, out_shape=out_shape, in_specs=[HBM] * (2 * na),
        out_specs=out_specs, input_output_aliases=aliases, compiler_params=SPLIT_PARAMS,
    )(*[_hbm(s) for s in shards], *[_hbm(l) for l in lands])
    sems = [tuple(res[3 * gi:3 * gi + 3]) for gi in range(ng)]
    return sems, list(res[3 * ng:3 * ng + na]), list(res[3 * ng + na:3 * ng + 2 * na]), res[-1]


def _gather_forward(name, lands, axes, r_ici, after):
    n = len(lands)
    widths = [l.shape[ax] // N_DEV for l, ax in zip(lands, axes)]

    def body(*refs):
        ld, r_ici_ref = refs[:n], refs[n]
        f_send, f_recv = refs[n + 2], refs[n + 3]
        x, y, c = _coords()
        sib = (x, y, 1 - c)
        chips = [(1 - x, y), (x, 1 - y), (1 - x, 1 - y)]
        for li in range(n):
            for j, chip in enumerate(chips):
                blk = _shard_window(ld[li], axes[li], widths[li], _dev_index((*chip, c)))
                pltpu.make_async_remote_copy(
                    src_ref=blk, dst_ref=blk, send_sem=f_send.at[3 * li + j], recv_sem=r_ici_ref.at[3 * li + j],
                    device_id=(*chip, c), device_id_type=MESH).wait_recv()
                pltpu.make_async_remote_copy(
                    src_ref=blk, dst_ref=blk, send_sem=f_send.at[3 * li + j], recv_sem=f_recv.at[3 * li + j],
                    device_id=sib, device_id_type=MESH).start()

    res = pl.pallas_call(
        body, name=name, out_shape=[_dma_sems(3 * n), _dma_sems(3 * n)] + [_hbm_like(l) for l in lands],
        in_specs=[HBM] * n + [SEM, ANY], out_specs=[SEM, SEM] + [HBM] * n,
        input_output_aliases={li: 2 + li for li in range(n)}, compiler_params=SPLIT_PARAMS,
    )(*lands, r_ici, after)
    return res[0], res[1], list(res[2:])


def _gather_finish(name, shards, lands, axes, send, r_d2d, f_send, f_recv, after):
    n = len(lands)
    widths = [l.shape[ax] // N_DEV for l, ax in zip(lands, axes)]

    def body(*refs):
        sh, ld = refs[:n], refs[n:2 * n]
        send_ref, r_d2d_ref, f_send_ref, f_recv_ref = refs[2 * n:2 * n + 4]
        x, y, c = _coords()
        me, sib = (x, y, c), (x, y, 1 - c)
        chips = [(1 - x, y), (x, 1 - y), (1 - x, 1 - y)]

        def blk(li, dev):
            return _shard_window(ld[li], axes[li], widths[li], _dev_index(dev))

        for li in range(n):
            for kk in range(4):
                pltpu.make_async_remote_copy(
                    src_ref=sh[li], dst_ref=blk(li, me), send_sem=send_ref.at[4 * li + kk], recv_sem=r_d2d_ref.at[li],
                    device_id=sib, device_id_type=MESH).wait_send()
            pltpu.make_async_remote_copy(
                src_ref=blk(li, sib), dst_ref=blk(li, sib), send_sem=send_ref.at[4 * li], recv_sem=r_d2d_ref.at[li],
                device_id=sib, device_id_type=MESH).wait_recv()
            for j, chip in enumerate(chips):
                pltpu.make_async_remote_copy(
                    src_ref=blk(li, (*chip, c)), dst_ref=blk(li, (*chip, c)), send_sem=f_send_ref.at[3 * li + j],
                    recv_sem=f_recv_ref.at[3 * li + j], device_id=sib, device_id_type=MESH).wait_send()
                pltpu.make_async_remote_copy(
                    src_ref=blk(li, (*chip, 1 - c)), dst_ref=blk(li, (*chip, 1 - c)), send_sem=f_send_ref.at[3 * li + j],
                    recv_sem=f_recv_ref.at[3 * li + j], device_id=sib, device_id_type=MESH).wait_recv()

    res = pl.pallas_call(
        body, name=name, out_shape=[_hbm_like(s) for s in shards] + [_hbm_like(l) for l in lands],
        in_specs=[HBM] * (2 * n) + [SEM] * 4 + [ANY], out_specs=[HBM] * (2 * n),
        input_output_aliases={i: i for i in range(2 * n)}, compiler_params=SPLIT_PARAMS,
    )(*shards, *lands, send, r_d2d, f_send, f_recv, after)
    return list(res[n:])


def _chip_exchange_start(name, parts):
    n = len(parts)

    def body(*refs):
        src, ld = refs[:n], refs[n:2 * n]
        send, recv = refs[2 * n], refs[2 * n + 1]
        token = refs[-1]
        x, y, c = _coords()
        chips = [(1 - x, y), (x, 1 - y), (1 - x, 1 - y)]
        for li in range(n):
            for kk, chip in enumerate(chips):
                pltpu.make_async_remote_copy(
                    src_ref=src[li].at[kk], dst_ref=ld[li].at[kk], send_sem=send.at[3 * li + kk],
                    recv_sem=recv.at[3 * li + kk], device_id=(*chip, c), device_id_type=MESH).start()
        token[...] = jnp.zeros_like(token)

    lands = [lax.empty(p.shape, p.dtype) for p in parts]
    res = pl.pallas_call(
        body, name=name,
        out_shape=[_dma_sems(3 * n), _dma_sems(3 * n)] + [_hbm_like(p) for p in parts] * 2 + [TOKEN_SHAPE],
        in_specs=[HBM] * (2 * n), out_specs=[SEM, SEM] + [HBM] * (2 * n) + [TOKEN_SPEC],
        input_output_aliases={i: 2 + i for i in range(2 * n)}, compiler_params=SPLIT_PARAMS,
    )(*[_hbm(p) for p in parts], *[_hbm(l) for l in lands])
    return res[0], res[1], list(res[2:2 + n]), list(res[2 + n:2 + 2 * n]), res[-1]


def _chip_exchange_finish(name, started, after):
    counts = [len(st[2]) for st in started]
    total = sum(counts)
    ns = len(started)

    def body(*refs):
        src, ld = refs[:total], refs[total:2 * total]
        sems = refs[2 * total:2 * total + 2 * ns]
        x, y, c = _coords()
        chips = [(1 - x, y), (x, 1 - y), (1 - x, 1 - y)]
        off = 0
        for si, cnt in enumerate(counts):
            send, recv = sems[2 * si], sems[2 * si + 1]
            for li in range(cnt):
                for kk, chip in enumerate(chips):
                    cp = pltpu.make_async_remote_copy(
                        src_ref=src[off + li].at[kk], dst_ref=ld[off + li].at[kk], send_sem=send.at[3 * li + kk],
                        recv_sem=recv.at[3 * li + kk], device_id=(*chip, c), device_id_type=MESH)
                    cp.wait_send()
                    cp.wait_recv()
            off += cnt

    srcs = [p for st in started for p in st[2]]
    lands = [l for st in started for l in st[3]]
    sems = [s for st in started for s in st[:2]]
    res = pl.pallas_call(
        body, name=name, out_shape=[_hbm_like(p) for p in srcs + lands],
        in_specs=[HBM] * (2 * total) + [SEM] * (2 * ns) + [ANY], out_specs=[HBM] * (2 * total),
        input_output_aliases={i: i for i in range(2 * total)}, compiler_params=SPLIT_PARAMS,
    )(*srcs, *lands, *sems, after)
    out, off = [], total
    for cnt in counts:
        out.append(list(res[off:off + cnt]))
        off += cnt
    return out


def _pair_copies(src, ld, send, recv, axes, widths):
    x, y, c = _coords()
    chips = [(x, y), (1 - x, y), (x, 1 - y), (1 - x, 1 - y)]
    return [pltpu.make_async_remote_copy(
        src_ref=_shard_window(src[li], axes[li], widths[li], _dev_index((*chip, 1 - c))),
        dst_ref=ld[li].at[kk], send_sem=send.at[4 * li + kk], recv_sem=recv.at[4 * li + kk],
        device_id=(x, y, 1 - c), device_id_type=MESH)
        for li in range(len(src)) for kk, chip in enumerate(chips)]


def _pair_exchange_start(name, grads, axes):
    n = len(grads)
    widths = [gr.shape[ax] // N_DEV for gr, ax in zip(grads, axes)]
    lands = []
    for gr, ax, wd in zip(grads, axes, widths):
        shp = list(gr.shape)
        shp[ax] = wd
        lands.append(lax.empty((4, *shp), gr.dtype))

    def body(*refs):
        for cp in _pair_copies(refs[:n], refs[n:2 * n], refs[2 * n], refs[2 * n + 1], axes, widths):
            cp.start()
        refs[-1][...] = jnp.zeros_like(refs[-1])

    res = pl.pallas_call(
        body, name=name,
        out_shape=[_dma_sems(4 * n), _dma_sems(4 * n)] + [_hbm_like(a) for a in grads + lands] + [TOKEN_SHAPE],
        in_specs=[HBM] * (2 * n), out_specs=[SEM, SEM] + [HBM] * (2 * n) + [TOKEN_SPEC],
        input_output_aliases={i: 2 + i for i in range(2 * n)}, compiler_params=SPLIT_PARAMS,
    )(*[_hbm(a) for a in grads + lands])
    return res[0], res[1], list(res[2:2 + n]), list(res[2 + n:2 + 2 * n]), res[-1]


def _pair_exchange_finish(name, started, axes, after):
    send, recv, grads, lands, _ = started
    n = len(grads)
    widths = [gr.shape[ax] // N_DEV for gr, ax in zip(grads, axes)]

    def body(*refs):
        for cp in _pair_copies(refs[:n], refs[n:2 * n], refs[2 * n], refs[2 * n + 1], axes, widths):
            cp.wait_send()
            cp.wait_recv()

    res = pl.pallas_call(
        body, name=name, out_shape=[_hbm_like(a) for a in grads + lands],
        in_specs=[HBM] * (2 * n) + [SEM, SEM, ANY], out_specs=[HBM] * (2 * n),
        input_output_aliases={i: i for i in range(2 * n)}, compiler_params=SPLIT_PARAMS,
    )(*grads, *lands, send, recv, after)
    return list(res[:n]), list(res[n:])


def _pair_sum(name, grad, land, axis, pos):
    wd = grad.shape[axis] // N_DEV
    shard_shape = land.shape[1:]
    rows, cols = shard_shape
    tr = _tile(rows, 256, 16)
    nt = rows // tr

    def dev_of(kk, pos_ref):
        return 4 * (pos_ref[0] ^ (kk & 1)) + 2 * (pos_ref[1] ^ (kk >> 1)) + pos_ref[2]

    def gspec(kk):
        if axis == 1:
            return pl.BlockSpec((tr, wd), lambda t, p: (t, dev_of(kk, p)))
        return pl.BlockSpec((tr, cols), lambda t, p: (dev_of(kk, p) * nt + t, 0))

    def body(pos_ref, g0, g1, g2, g3, l_ref, p0_ref, pb_ref):
        p0_ref[...] = g0[...].astype(F32) + l_ref[0].astype(F32)
        for kk, g_ref in enumerate((g1, g2, g3)):
            pb_ref[kk] = (g_ref[...].astype(F32) + l_ref[kk + 1].astype(F32)).astype(BF16)

    return pl.pallas_call(
        body,
        grid_spec=pltpu.PrefetchScalarGridSpec(
            num_scalar_prefetch=1, grid=(nt,),
            in_specs=[gspec(kk) for kk in range(4)] + [pl.BlockSpec((4, tr, cols), lambda t, p: (0, t, 0))],
            out_specs=[pl.BlockSpec((tr, cols), lambda t, p: (t, 0)),
                       pl.BlockSpec((3, tr, cols), lambda t, p: (0, t, 0))]),
        out_shape=[jax.ShapeDtypeStruct(shard_shape, F32), jax.ShapeDtypeStruct((3, *shard_shape), BF16)],
        compiler_params=_params(("parallel",)), name=name)(pos, grad, grad, grad, grad, land)


def _adamw_math(g, w, m, v):
    c1 = 1.0 - ADAM_B1 ** ADAM_STEP
    c2 = 1.0 - ADAM_B2 ** ADAM_STEP
    nm = ADAM_B1 * m + (1.0 - ADAM_B1) * g
    nv = ADAM_B2 * v + (1.0 - ADAM_B2) * (g * g)
    return -ADAM_LR * ((nm / c1) / (jnp.sqrt(nv / c2) + ADAM_EPS) + ADAM_WD * w), nm, nv


def _sum_parts(name, parts):
    n, rows, cols = parts.shape
    tr = _tile(rows, 1024, 8)

    def body(p_ref, o_ref):
        o_ref[...] = functools.reduce(lambda a, b: a + b, [p_ref[i] for i in range(n)])

    return pl.pallas_call(
        body, grid=(rows // tr,), in_specs=[pl.BlockSpec((n, tr, cols), lambda i: (0, i, 0))],
        out_specs=pl.BlockSpec((tr, cols), lambda i: (i, 0)), out_shape=jax.ShapeDtypeStruct((rows, cols), F32),
        compiler_params=_params(("parallel",)), name=name)(parts)


def _adamw_multi(name, gs, ws, ms, vs):
    k = len(gs)
    rows, cols = ws[0].shape
    tr = _tile(rows, 1024, 8)

    def body(*refs):
        for i in range(k):
            g_ref, w_ref, m_ref, v_ref = (refs[j * k + i] for j in range(4))
            og_ref, d_ref, nm_ref, nv_ref = (refs[(4 + j) * k + i] for j in range(4))
            g = g_ref[...]
            og_ref[...] = g
            d_ref[...], nm_ref[...], nv_ref[...] = _adamw_math(g, w_ref[...], m_ref[...], v_ref[...])

    spec = pl.BlockSpec((tr, cols), lambda i: (i, 0))
    res = pl.pallas_call(
        body, grid=(rows // tr,), in_specs=[spec] * (4 * k), out_specs=[spec] * (4 * k),
        out_shape=[jax.ShapeDtypeStruct((rows, cols), F32)] * (4 * k),
        compiler_params=_params(("parallel",)), name=name)(*gs, *ws, *ms, *vs)
    return [[res[j * k + i] for j in range(4)] for i in range(k)]


def _adamw(name, parts, w, m, v, layer=None, prev=None):
    rows, cols = w.shape[-2:]
    tr = _tile(rows, 256, 8)
    npart = len(parts)

    def body(*refs):
        p_refs = refs[:npart]
        w_ref, m_ref, v_ref = refs[npart:npart + 3]
        g_ref, d_ref, nm_ref, nv_ref = refs[-4:]
        g = None
        for r in p_refs:
            if len(r.shape) == 3:
                for i in range(r.shape[0]):
                    t = r[i].astype(F32)
                    g = t if g is None else g + t
            else:
                t = r[...].astype(F32)
                g = t if g is None else g + t
        g_ref[...] = g
        d_ref[...], nm_ref[...], nv_ref[...] = _adamw_math(g, w_ref[...], m_ref[...], v_ref[...])

    spec = pl.BlockSpec((tr, cols), lambda i: (i, 0))
    wspec = spec if layer is None else pl.BlockSpec((None, tr, cols), lambda i: (layer, i, 0))
    pspecs = [pl.BlockSpec((p.shape[0], tr, cols), lambda i: (0, i, 0)) if p.ndim == 3 else spec
              for p in parts]
    prev = list(prev) if prev else []
    return pl.pallas_call(
        body, grid=(rows // tr,), in_specs=pspecs + [wspec] * 3 + [ANY] * len(prev), out_specs=[wspec] * 4,
        out_shape=[jax.ShapeDtypeStruct(w.shape, F32)] * 4,
        input_output_aliases={npart + 3 + i: i for i in range(len(prev))},
        compiler_params=_params(("parallel",)), name=name)(*parts, w, m, v, *prev)


SMALL_NAMES = ("s5_lam_re", "s5_lam_im", "s5_log_dt", "s5_b_re", "s5_b_im", "s5_c_re", "s5_c_im",
               "ffn_norm", "b_norm_mix", "kv_norm", "final_norm")
SMALL_PAD = 1024
SMALL_GROUPS = (("s5_lam_re", "s5_lam_im"), ("s5_log_dt",), ("s5_b_re", "s5_b_im"), ("s5_c_re", "s5_c_im"),
                ("ffn_norm",), ("b_norm_mix", "kv_norm", "final_norm"))


def _pack(parts):
    flat = []
    for p in parts:
        f = p.reshape(-1)
        pad = (-f.shape[0]) % SMALL_PAD
        if pad:
            f = jnp.concatenate([f, jnp.zeros((pad,), f.dtype)])
        flat.append(f)
    return jnp.concatenate(flat).reshape(-1, 128)


def _unpack(packed, shapes):
    flat = packed.reshape(-1)
    out, off = [], 0
    for shp in shapes:
        size = math.prod(shp)
        out.append(flat[off:off + size].reshape(shp))
        off += size + (-size) % SMALL_PAD
    return out


def kernel(x, s5_lam_re, s5_lam_im, s5_log_dt, s5_b_re, s5_b_im, s5_c_re, s5_c_im, s5_d, s5_w_glu, a_norm_mix, ffn_norm, ffn_w_in, ffn_w_out, b_norm_mix, attn_w_q, attn_w_o, kv_norm, w_kv, final_norm, loss_target, m_s5_lam_re, m_s5_lam_im, m_s5_log_dt, m_s5_b_re, m_s5_b_im, m_s5_c_re, m_s5_c_im, m_s5_d, m_s5_w_glu, m_a_norm_mix, m_ffn_norm, m_ffn_w_in, m_ffn_w_out, m_b_norm_mix, m_attn_w_q, m_attn_w_o, m_kv_norm, m_w_kv, m_final_norm, v_s5_lam_re, v_s5_lam_im, v_s5_log_dt, v_s5_b_re, v_s5_b_im, v_s5_c_re, v_s5_c_im, v_s5_d, v_s5_w_glu, v_a_norm_mix, v_ffn_norm, v_ffn_w_in, v_ffn_w_out, v_b_norm_mix, v_attn_w_q, v_attn_w_o, v_kv_norm, v_w_kv, v_final_norm):
    args = dict(locals())
    T, D = x.shape[1], x.shape[2]
    n_layers = ffn_w_in.shape[0]
    xi_, yi_, ci_ = _coords()
    pos = jnp.stack([xi_, yi_, ci_]).astype(jnp.int32)
    me = 4 * xi_ + 2 * yi_ + ci_

    big_names = ["glu"] + [f"win{l}" for l in range(n_layers)] + [f"wout{l}" for l in range(n_layers)] \
        + ["wkv", "wq", "wo"]
    big_shards = [(s5_w_glu, 0)] + [(ffn_w_in, l) for l in range(n_layers)] \
        + [(ffn_w_out, l) for l in range(n_layers)] + [(w_kv[None], 0), (attn_w_q, 0), (attn_w_o, 0)]
    big_axes = [1] + [1] * n_layers + [0] * n_layers + [0, 1, 0]
    big_out_names = ["s5_w_glu"] + ["ffn_w_in"] * n_layers + ["ffn_w_out"] * n_layers \
        + ["w_kv", "attn_w_q", "attn_w_o"]
    index_of = {n: i for i, n in enumerate(big_names)}
    vec_shard = jnp.concatenate([s5_d, a_norm_mix], axis=0)
    (vecs,) = _all_gather("vectors_all_gather", [vec_shard], [1])

    gather_groups = [["glu"], ["win0"], ["wout0"], ["wkv", "wq", "wo"], ["win1", "wout1"]]
    group_idx = [[index_of[n] for n in grp] for grp in gather_groups]
    group_of = {n: gi for gi, grp in enumerate(gather_groups) for n in grp}
    placed = [_cast_and_place(f"cast_place_{n}", s, l, ax, pos, BF16)
              for n, (s, l), ax in zip(big_names, big_shards, big_axes)]
    gather_sems, shards_thru, lands_thru, start_token = _gather_start(
        "weights_gather_start", [p[0] for p in placed], [p[1] for p in placed], big_axes, group_idx)
    full = {}

    forwarded = {}

    def wts(name, after, prefetch=False):
        gi = group_of[name]
        idx = group_idx[gi]
        axes = [big_axes[a] for a in idx]
        send, r_d2d, r_ici = gather_sems[gi]
        if gi not in forwarded:
            forwarded[gi] = _gather_forward(
                f"weights_gather_forward{gi}", [lands_thru[a] for a in idx], axes, r_ici, after)
        if prefetch:
            return None
        if name not in full:
            f_send, f_recv, lands = forwarded[gi]
            done = _gather_finish(f"weights_gather_finish{gi}", [shards_thru[a] for a in idx], lands, axes,
                                  send, r_d2d, f_send, f_recv, after)
            full.update(zip(gather_groups[gi], done))
        return full[name]

    exchanges, pending = [], []

    def flush(after):
        names, axes, started = pending.pop()
        tag = "_".join(names)
        grads, lands = _pair_exchange_finish(f"rs_pair_exchange_finish_{tag}", started, axes, after)
        p0s, pbs = [], []
        for n, gr, land, ax in zip(names, grads, lands, axes):
            p0, pb = _pair_sum(f"rs_pair_sum_{n}", gr, land, ax, pos)
            p0s.append(p0)
            pbs.append(pb)
        started = _chip_exchange_start(f"rs_chip_exchange_start_{tag}", pbs)
        exchanges.append((names, p0s, started))
        return started[4]

    def ready(grads, after=None):
        if not grads:
            return flush(after)
        names = list(grads)
        axes = [big_axes[index_of[n]] for n in names]
        started = _pair_exchange_start(f"rs_pair_exchange_start_{'_'.join(names)}", [grads[n] for n in names], axes)
        token = flush(started[4]) if pending else started[4]
        pending.append((names, axes, started))
        return token

    G, P, C = s5_b_re.shape[1:]
    w = dict(
        a_norm=vecs[1:2], s5_d=vecs[0:1],
        ffn_norm=ffn_norm, b_norm=b_norm_mix, kv_norm=kv_norm.reshape(1, D), final_norm=final_norm.reshape(1, D),
        lam_re=s5_lam_re[0], lam_im=s5_lam_im[0], log_dt=s5_log_dt.reshape(G, 1),
        bt_re=s5_b_re[0].transpose(2, 0, 1), bt_im=s5_b_im[0].transpose(2, 0, 1),
        c_re=s5_c_re[0], c_im=s5_c_im[0],
    )
    small = {}

    def small_ready(g):
        small_g = dict(
            s5_lam_re=g["lam_re"], s5_lam_im=g["lam_im"], s5_log_dt=g["log_dt"],
            s5_b_re=g["bt_re"].transpose(1, 2, 0), s5_b_im=g["bt_im"].transpose(1, 2, 0),
            s5_c_re=g["c_re"], s5_c_im=g["c_im"], ffn_norm=g["ffn_norm"], b_norm_mix=g["b_norm"],
            kv_norm=g["kv_norm"], final_norm=g["final_norm"])
        packed = _pack([small_g[n] for n in SMALL_NAMES])
        shard, land = _cast_and_place("place_small_grads", packed[None], 0, 0, pos, F32)
        sems, thru, lands, token = _gather_start("small_grads_gather_start", [shard], [land], [0], [[0]])
        small.update(sems=sems[0], thru=thru, lands=lands, rows=packed.shape[0])
        return token

    loss_blk, grad_x, g = _local_step(x[0], loss_target[0], w, wts, ready, dep0=start_token,
                                      small_ready=small_ready)
    loss = lax.psum(loss_blk[0, 0], ("x", "y", "c"))
    (tail,) = _all_gather("vector_grads_all_gather", [_pack([g["s5_d"], g["a_norm"]])], [0])

    out = {}

    def put(name, res, shape):
        for kind, r in zip(("grad", "delta", "new_m", "new_v"), res):
            out[f"{kind}_{name}"] = r.reshape(shape)

    updated = {}

    def update(names, p0s, recvd):
        for name, p0, rc in zip(names, p0s, recvd):
            oname = big_out_names[index_of[name]]
            w3, layer = big_shards[index_of[name]]
            updated[oname] = _adamw(f"adamw_{name}", [p0, rc], w3, args["m_" + oname].reshape(w3.shape),
                                    args["v_" + oname].reshape(w3.shape), layer=layer, prev=updated.get(oname))
        return updated[big_out_names[index_of[names[0]]]][0]

    early, last = exchanges[:-1], exchanges[-1]
    landed = _chip_exchange_finish("rs_chip_exchange_finish_early", [e[2] for e in early], grad_x)
    for (names, p0s, _), recvd in zip(early[:-1], landed[:-1]):
        marker = update(names, p0s, recvd)
    send, r_d2d, r_ici = small["sems"]
    f_send, f_recv, small_lands = _gather_forward("small_grads_gather_forward", small["lands"], [0], r_ici, marker)
    marker = update(early[-1][0], early[-1][1], landed[-1])
    (all_parts,) = _gather_finish("small_grads_gather_finish", small["thru"], small_lands, [0],
                                  send, r_d2d, f_send, f_recv, marker)
    (recvd,) = _chip_exchange_finish("rs_chip_exchange_finish_last", [last[2]], marker)
    update(last[0], last[1], recvd)
    for oname, res in updated.items():
        put(oname, res, args[oname].shape)
    g_small = _sum_parts("small_grads_sum", all_parts.reshape(N_DEV, small["rows"], 128))
    g_small = dict(zip(SMALL_NAMES, _unpack(g_small, [args[n].shape for n in SMALL_NAMES])))

    def view(a):
        return a.reshape(1, -1) if a.ndim == 1 else a.reshape(-1, a.shape[-1])

    for grp in SMALL_GROUPS:
        res = _adamw_multi(f"adamw_{grp[0]}", [view(g_small[n]) for n in grp], [view(args[n]) for n in grp],
                           [view(args["m_" + n]) for n in grp], [view(args["v_" + n]) for n in grp])
        for n, r in zip(grp, res):
            put(n, r, args[n].shape)
    ws = D // N_DEV
    tail = lax.dynamic_slice_in_dim(tail.reshape(N_DEV, 2, D), me * ws, ws, axis=2)
    res = _adamw("adamw_vec", [tail], vec_shard,
                 jnp.concatenate([m_s5_d, m_a_norm_mix], axis=0), jnp.concatenate([v_s5_d, v_a_norm_mix], axis=0))
    put("s5_d", [r[0:1] for r in res], s5_d.shape)
    put("a_norm_mix", [r[1:2] for r in res], a_norm_mix.shape)

    names = ("s5_lam_re", "s5_lam_im", "s5_log_dt", "s5_b_re", "s5_b_im", "s5_c_re", "s5_c_im", "s5_d",
             "s5_w_glu", "a_norm_mix", "ffn_norm", "ffn_w_in", "ffn_w_out", "b_norm_mix", "attn_w_q",
             "attn_w_o", "kv_norm", "w_kv", "final_norm")
    result = [loss, grad_x.reshape(x.shape)]
    for kind in ("grad", "delta", "new_m", "new_v"):
        result += [out[f"{kind}_{n}"] for n in names]
    return tuple(result)
```

```python
import functools
import math

import jax
import jax.numpy as jnp
from jax import lax
from jax.experimental import pallas as pl
from jax.experimental.pallas import tpu as pltpu

F32 = jnp.float32
BF16 = jnp.bfloat16

EPS = 1e-6
NEG_INF = -1e30
HEAD_DIM = 128
N_KV_HEADS = 4
DILATIONS = (1, 4, 16)
ATT_BLK = 128
S5_C = 16
S5_P = 64
S5_GB = 16
S5_CH = S5_GB * S5_C
S5_W = S5_GB * S5_P
S5_UNROLL = 4
N_DEV = 8

ADAM_LR = 0.001
ADAM_B1 = 0.9
ADAM_B2 = 0.999
ADAM_EPS = 1e-08
ADAM_WD = 0.01
ADAM_STEP = 10

VMEM_LIMIT_BYTES = 56 * 1024 * 1024
MM_TILE = 1024
MM_TILE_NARROW = 512
MM_DEPTH = 2816
MESH = pl.DeviceIdType.MESH
ANY = pl.BlockSpec(memory_space=pl.ANY)


def _tile(n, pref, align=128):
    t = (min(pref, n) // align) * align
    while t >= align:
        if n % t == 0:
            return t
        t -= align
    return n


def _params(sem):
    return pltpu.CompilerParams(dimension_semantics=sem, vmem_limit_bytes=VMEM_LIMIT_BYTES)


def _sigmoid(x):
    return 1.0 / (1.0 + jnp.exp(-x))


NN = (((1,), (0,)), ((), ()))
NT = (((1,), (1,)), ((), ()))
TN = (((0,), (0,)), ((), ()))


def _dot(a, b, dims=NN):
    return lax.dot_general(a, b, dims, preferred_element_type=F32)


def _matmul(name, grid, ins, in_specs, products, dims, out_shapes, out_specs, acc_shapes, epilogue):
    n_in, n_out, nk = len(ins), len(out_shapes), grid[2]

    def body(*refs):
        in_refs = refs[:n_in]
        out_refs = refs[n_in:n_in + n_out]
        acc_refs = refs[n_in + n_out:]

        def prods():
            vals = [None] * len(acc_shapes)
            for ai, bi, ci in products:
                d = _dot(in_refs[ai][...].astype(BF16), in_refs[bi][...].astype(BF16), dims)
                vals[ci] = d if vals[ci] is None else vals[ci] + d
            return vals

        if nk == 1:
            epilogue(in_refs, out_refs, prods())
        else:
            k = pl.program_id(2)

            @pl.when(k == 0)
            def _():
                for a in acc_refs:
                    a[...] = jnp.zeros_like(a)

            for a, v in zip(acc_refs, prods()):
                a[...] += v

            @pl.when(k == nk - 1)
            def _():
                epilogue(in_refs, out_refs, [a[...] for a in acc_refs])

    scratch = [] if nk == 1 else [pltpu.VMEM(s, F32) for s in acc_shapes]
    return pl.pallas_call(
        body, grid=grid, in_specs=in_specs, out_specs=out_specs, out_shape=out_shapes,
        scratch_shapes=scratch, compiler_params=_params(("parallel", "parallel", "arbitrary")),
        name=name)(*ins)


def _mm_dual_fwd(name, a, w, res, kind):
    T, K = a.shape
    N = w.shape[1] // 2
    tm, tn = _tile(T, MM_TILE), _tile(N, MM_TILE_NARROW)
    nj = N // tn
    grid = (T // tm, nj, 1)
    ins = [a, w, w]
    in_specs = [pl.BlockSpec((tm, K), lambda i, j, k: (i, 0)),
                pl.BlockSpec((K, tn), lambda i, j, k: (0, j)),
                pl.BlockSpec((K, tn), lambda i, j, k: (0, j + nj))]
    pair_spec = pl.BlockSpec((2, tm, tn), lambda i, j, k: (0, i, j))
    tile_spec = pl.BlockSpec((tm, tn), lambda i, j, k: (i, j))
    if kind == "glu":
        ins.append(res)
        in_specs.append(tile_spec)

        def epilogue(in_refs, out_refs, accs):
            val, gate = accs
            s = _sigmoid(gate)
            out_refs[0][...] = in_refs[3][...] + val * s
            out_refs[1][0] = s.astype(BF16)
            out_refs[1][1] = (val * s * (1.0 - s)).astype(BF16)

        out_shapes = [jax.ShapeDtypeStruct((T, N), F32), jax.ShapeDtypeStruct((2, T, N), BF16)]
        out_specs = [tile_spec, pair_spec]
    else:
        def epilogue(in_refs, out_refs, accs):
            g, u = accs
            s = _sigmoid(g)
            silu = g * s
            out_refs[0][0] = (u * (s * (1.0 + g * (1.0 - s)))).astype(BF16)
            out_refs[0][1] = silu.astype(BF16)
            out_refs[1][...] = (silu * u).astype(BF16)

        out_shapes = [jax.ShapeDtypeStruct((2, T, N), BF16), jax.ShapeDtypeStruct((T, N), BF16)]
        out_specs = [pair_spec, tile_spec]
    return _matmul(name, grid, ins, in_specs, [(0, 1, 0), (0, 2, 1)], NN, out_shapes, out_specs,
                   [(tm, tn), (tm, tn)], epilogue)


def _mm_kv(name, a, w):
    T, K = a.shape
    N = w.shape[1] // 2
    tm, tn = _tile(T, MM_TILE), _tile(N, MM_TILE_NARROW)
    nj = N // tn
    tile_spec = pl.BlockSpec((tm, tn), lambda i, j, k: (i, j))

    def epilogue(in_refs, out_refs, accs):
        out_refs[0][...] = accs[0]
        out_refs[1][...] = accs[1]

    return _matmul(name, (T // tm, nj, 1), [a, w, w],
                   [pl.BlockSpec((tm, K), lambda i, j, k: (i, 0)),
                    pl.BlockSpec((K, tn), lambda i, j, k: (0, j)),
                    pl.BlockSpec((K, tn), lambda i, j, k: (0, j + nj))],
                   [(0, 1, 0), (0, 2, 1)], NN,
                   [jax.ShapeDtypeStruct((T, N), F32)] * 2, [tile_spec, tile_spec],
                   [(tm, tn), (tm, tn)], epilogue)


def _mm_nn(name, a, w, res=None, out_dtype=F32):
    T, K = a.shape
    N = w.shape[1]
    tk = K if K <= 2 * MM_DEPTH else _tile(K, MM_DEPTH)
    tm, tn = _tile(T, MM_TILE), _tile(N, MM_TILE if K <= MM_DEPTH else MM_TILE_NARROW)
    grid = (T // tm, N // tn, K // tk)
    tile_spec = pl.BlockSpec((tm, tn), lambda i, j, k: (i, j))
    ins = [a, w]
    in_specs = [pl.BlockSpec((tm, tk), lambda i, j, k: (i, k)),
                pl.BlockSpec((tk, tn), lambda i, j, k: (k, j))]
    if res is not None:
        ins.append(res)
        in_specs.append(tile_spec)

    def epilogue(in_refs, out_refs, accs):
        v = accs[0]
        if res is not None:
            v = v + in_refs[2][...]
        out_refs[0][...] = v.astype(out_dtype)

    return _matmul(name, grid, ins, in_specs, [(0, 1, 0)], NN,
                   [jax.ShapeDtypeStruct((T, N), out_dtype)], [tile_spec], [(tm, tn)], epilogue)[0]


def _dep_operand(ins, in_specs, dep):
    if dep is not None:
        ins.append(dep)
        in_specs.append(pl.BlockSpec((8, 128), lambda *_: (0, 0)))


def _mm_nt(name, a_list, w, out_dtype=F32, dep=None):
    T, Np = a_list[0].shape
    Ko = w.shape[0]
    n_parts = len(a_list)
    wide_a = a_list[0].dtype != BF16
    tm, tn, tk = _tile(T, MM_TILE_NARROW if wide_a else MM_TILE), _tile(Ko, MM_TILE), _tile(Np, MM_DEPTH)
    nkp = Np // tk
    grid = (T // tm, Ko // tn, nkp)
    ins = list(a_list) + [w] * n_parts
    in_specs = [pl.BlockSpec((tm, tk), lambda i, j, k: (i, k)) for _ in a_list]
    in_specs += [pl.BlockSpec((tn, tk), functools.partial(lambda i, j, k, p: (j, p * nkp + k), p=p))
                 for p in range(n_parts)]
    products = [(p, n_parts + p, 0) for p in range(n_parts)]
    _dep_operand(ins, in_specs, dep)

    def epilogue(in_refs, out_refs, accs):
        out_refs[0][...] = accs[0].astype(out_dtype)

    return _matmul(name, grid, ins, in_specs, products, NT,
                   [jax.ShapeDtypeStruct((T, Ko), out_dtype)],
                   [pl.BlockSpec((tm, tn), lambda i, j, k: (i, j))], [(tm, tn)], epilogue)[0]


def _mm_nt_pair(name, a3, w, out_dtype=F32):
    _, T, N = a3.shape
    Ko = w.shape[0]
    tm, tn, tk = _tile(T, MM_TILE), _tile(Ko, MM_TILE), _tile(N, MM_DEPTH)
    nkh = N // tk
    grid = (T // tm, Ko // tn, 2 * nkh)

    def epilogue(in_refs, out_refs, accs):
        out_refs[0][...] = accs[0].astype(out_dtype)

    return _matmul(name, grid, [a3, w],
                   [pl.BlockSpec((None, tm, tk), lambda i, j, k: (k // nkh, i, k % nkh)),
                    pl.BlockSpec((tn, tk), lambda i, j, k: (j, k))],
                   [(0, 1, 0)], NT, [jax.ShapeDtypeStruct((T, Ko), out_dtype)],
                   [pl.BlockSpec((tm, tn), lambda i, j, k: (i, j))], [(tm, tn)], epilogue)[0]


def _mm_nt_ffn_bwd(name, dx, w_out, gu, dep=None):
    T, D = dx.shape
    Fh = w_out.shape[0]
    tm, tn = _tile(T, MM_TILE), _tile(Fh, MM_TILE_NARROW)
    n_dep = 0 if dep is None else 1

    def body(dx_ref, w_ref, gu_ref, *rest):
        out_ref = rest[n_dep]
        da = _dot(dx_ref[...], w_ref[...], NT)
        out_ref[0] = (da * gu_ref[0].astype(F32)).astype(BF16)
        out_ref[1] = (da * gu_ref[1].astype(F32)).astype(BF16)

    pair_spec = pl.BlockSpec((2, tm, tn), lambda i, j: (0, i, j))
    ins = [dx, w_out, gu]
    in_specs = [pl.BlockSpec((tm, D), lambda i, j: (i, 0)),
                pl.BlockSpec((tn, D), lambda i, j: (j, 0)),
                pair_spec]
    _dep_operand(ins, in_specs, dep)
    return pl.pallas_call(
        body, grid=(T // tm, Fh // tn), in_specs=in_specs, out_specs=pair_spec,
        out_shape=jax.ShapeDtypeStruct((2, T, Fh), BF16),
        compiler_params=_params(("parallel", "parallel")), name=name)(*ins)


def _mm_tn(name, a, d):
    T, Ko = a.shape
    N = d.shape[1]
    to, tn, tk = _tile(Ko, MM_TILE_NARROW), _tile(N, MM_TILE if d.dtype == BF16 else MM_TILE_NARROW), T
    grid = (Ko // to, N // tn, T // tk)

    def epilogue(in_refs, out_refs, accs):
        out_refs[0][...] = accs[0].astype(BF16)

    return _matmul(name, grid, [a, d],
                   [pl.BlockSpec((tk, to), lambda i, j, k: (k, i)),
                    pl.BlockSpec((tk, tn), lambda i, j, k: (k, j))],
                   [(0, 1, 0)], TN, [jax.ShapeDtypeStruct((Ko, N), BF16)],
                   [pl.BlockSpec((to, tn), lambda i, j, k: (i, j))], [(to, tn)], epilogue)[0]


def _mm_tn_pair(name, a, d3):
    T, Ko = a.shape
    N = d3.shape[2]
    to, tn, tk = _tile(Ko, MM_TILE), _tile(N, MM_TILE_NARROW), T
    njh = N // tn
    grid = (Ko // to, 2 * njh, T // tk)

    def epilogue(in_refs, out_refs, accs):
        out_refs[0][...] = accs[0].astype(BF16)

    return _matmul(name, grid, [a, d3],
                   [pl.BlockSpec((tk, to), lambda i, j, k: (k, i)),
                    pl.BlockSpec((None, tk, tn), lambda i, j, k: (j // njh, k, j % njh))],
                   [(0, 1, 0)], TN, [jax.ShapeDtypeStruct((Ko, 2 * N), BF16)],
                   [pl.BlockSpec((to, tn), lambda i, j, k: (i, j))], [(to, tn)], epilogue)[0]


def _rms_fwd(name, x, gains, dtypes, dep=None):
    T, D = x.shape
    n = len(gains)
    tr = _tile(T, 512, 8)
    n_dep = 0 if dep is None else 1

    def body(x_ref, *refs):
        xv = x_ref[...]
        xr = xv * lax.rsqrt(jnp.mean(xv * xv, axis=-1, keepdims=True) + EPS)
        for g_ref, o_ref in zip(refs[:n], refs[n + n_dep:]):
            o_ref[...] = (xr * g_ref[...]).astype(o_ref.dtype)

    row = pl.BlockSpec((tr, D), lambda i: (i, 0))
    vec = pl.BlockSpec((1, D), lambda i: (0, 0))
    ins, in_specs = [x, *gains], [row] + [vec] * n
    _dep_operand(ins, in_specs, dep)
    return pl.pallas_call(
        body, grid=(T // tr,), in_specs=in_specs, out_specs=[row] * n,
        out_shape=[jax.ShapeDtypeStruct((T, D), dt) for dt in dtypes],
        compiler_params=_params(("parallel",)), name=name)(*ins)


def _rms_bwd(name, x, dres, gains, dhs, dep=None):
    T, D = x.shape
    n = len(gains)
    tr = _tile(T, 256, 8)
    n_dep = 0 if dep is None else 1

    def body(x_ref, dres_ref, *refs):
        g_refs, dh_refs = refs[:n], refs[n:2 * n]
        dx_ref, dxb_ref = refs[2 * n + n_dep], refs[2 * n + n_dep + 1]
        dg_refs = refs[2 * n + n_dep + 2:]
        xv = x_ref[...]
        r = lax.rsqrt(jnp.mean(xv * xv, axis=-1, keepdims=True) + EPS)
        xr = xv * r
        w = None
        for g_ref, dh_ref, dg_ref in zip(g_refs, dh_refs, dg_refs):
            dh = dh_ref[...].astype(F32)

            @pl.when(pl.program_id(0) == 0)
            def _():
                dg_ref[...] = jnp.zeros_like(dg_ref)

            dg_ref[...] += jnp.sum(dh * xr, axis=0, keepdims=True)
            wi = dh * g_ref[...]
            w = wi if w is None else w + wi
        dx = dres_ref[...] + r * (w - xr * jnp.mean(w * xr, axis=-1, keepdims=True))
        dx_ref[...] = dx
        dxb_ref[...] = dx.astype(BF16)

    row = pl.BlockSpec((tr, D), lambda i: (i, 0))
    vec = pl.BlockSpec((1, D), lambda i: (0, 0))
    ins, in_specs = [x, dres, *gains, *dhs], [row, row] + [vec] * n + [row] * n
    _dep_operand(ins, in_specs, dep)
    outs = pl.pallas_call(
        body, grid=(T // tr,), in_specs=in_specs,
        out_specs=[row, row] + [vec] * n,
        out_shape=[jax.ShapeDtypeStruct((T, D), F32), jax.ShapeDtypeStruct((T, D), BF16)]
        + [jax.ShapeDtypeStruct((1, D), F32)] * n,
        compiler_params=_params(("arbitrary",)), name=name)(*ins)
    return outs[0], outs[1], outs[2:]


def _loss_head(x, gain, target):
    T, D = x.shape
    tr = _tile(T, 256, 8)

    def body(x_ref, g_ref, t_ref, loss_ref, dx_ref, dxb_ref, dg_ref):
        @pl.when(pl.program_id(0) == 0)
        def _():
            loss_ref[...] = jnp.zeros_like(loss_ref)
            dg_ref[...] = jnp.zeros_like(dg_ref)

        xv = x_ref[...]
        r = lax.rsqrt(jnp.mean(xv * xv, axis=-1, keepdims=True) + EPS)
        xr = xv * r
        err = xr * g_ref[...] - t_ref[...]
        part = jnp.sum(jnp.sum(err * err, axis=-1, keepdims=True), axis=0, keepdims=True) * (0.5 / D)
        loss_ref[...] += jnp.broadcast_to(part, loss_ref.shape)
        dy = err * (1.0 / D)
        dg_ref[...] += jnp.sum(dy * xr, axis=0, keepdims=True)
        w = dy * g_ref[...]
        dx = r * (w - xr * jnp.mean(w * xr, axis=-1, keepdims=True))
        dx_ref[...] = dx
        dxb_ref[...] = dx.astype(BF16)

    row = pl.BlockSpec((tr, D), lambda i: (i, 0))
    vec = pl.BlockSpec((1, D), lambda i: (0, 0))
    return pl.pallas_call(
        body, grid=(T // tr,), in_specs=[row, vec, row],
        out_specs=[pl.BlockSpec((8, 128), lambda i: (0, 0)), row, row, vec],
        out_shape=[jax.ShapeDtypeStruct((8, 128), F32), jax.ShapeDtypeStruct((T, D), F32),
                   jax.ShapeDtypeStruct((T, D), BF16), jax.ShapeDtypeStruct((1, D), F32)],
        compiler_params=_params(("arbitrary",)), name="loss_head")(x, gain, target)


def _glu_bwd(dmix, vg, dep=None):
    T, N = dmix.shape
    tr, tc = _tile(T, 512, 8), _tile(N, 1024)
    n_dep = 0 if dep is None else 1

    def body(d_ref, vg_ref, *refs):
        o_ref = refs[n_dep]
        d = d_ref[...]
        o_ref[0] = (d * vg_ref[0].astype(F32)).astype(BF16)
        o_ref[1] = (d * vg_ref[1].astype(F32)).astype(BF16)

    pair = pl.BlockSpec((2, tr, tc), lambda i, j: (0, i, j))
    ins, in_specs = [dmix, vg], [pl.BlockSpec((tr, tc), lambda i, j: (i, j)), pair]
    _dep_operand(ins, in_specs, dep)
    return pl.pallas_call(
        body, grid=(T // tr, N // tc), in_specs=in_specs,
        out_specs=pair, out_shape=jax.ShapeDtypeStruct((2, T, N), BF16),
        compiler_params=_params(("parallel", "parallel")), name="glu_bwd")(*ins)


def _to_state_tiles(x_ref, s, val):
    tc = val.shape[0]
    for j in range(S5_W // 128):
        x_ref[s, pl.ds(j, tc, stride=8), :] = val[:, 128 * j:128 * (j + 1)]


def _from_state_tiles(x_ref, s, tc):
    return jnp.concatenate([x_ref[s, pl.ds(j, tc, stride=8), :] for j in range(S5_W // 128)], axis=1)


def _s5_scan_fwd(xr_ref, xi_ref, ar_ref, ai_ref, cr_ref, ci_ref, tc, nblk):
    a = [(ar_ref[s], ai_ref[s]) for s in range(nblk)]

    def step(i, carry):
        carry = list(carry)
        for uu in range(S5_UNROLL):
            r0 = pl.multiple_of((i * S5_UNROLL + uu) * 8, 8)
            for s in range(nblk):
                cr, ci = carry[2 * s], carry[2 * s + 1]
                a_r, a_i = a[s]
                xr = a_r * cr - a_i * ci + xr_ref[s, pl.ds(r0, 8), :]
                xi = a_r * ci + a_i * cr + xi_ref[s, pl.ds(r0, 8), :]
                xr_ref[s, pl.ds(r0, 8), :] = xr
                xi_ref[s, pl.ds(r0, 8), :] = xi
                carry[2 * s], carry[2 * s + 1] = xr, xi
        return tuple(carry)

    init = []
    for s in range(nblk):
        init += [cr_ref[s], ci_ref[s]]
    out = lax.fori_loop(0, tc // S5_UNROLL, step, tuple(init))
    for s in range(nblk):
        cr_ref[s] = out[2 * s]
        ci_ref[s] = out[2 * s + 1]


def _s5_scan_bwd(lr_ref, li_ref, xr_ref, xi_ref, h_ref, ar_ref, ai_ref, cr_ref, ci_ref,
                 accr_ref, acci_ref, tc, nblk):
    a = [(ar_ref[s], ai_ref[s]) for s in range(nblk)]

    def one(s, r0, prev_r, prev_i, st):
        c_r, c_i, d_r, d_i = st
        a_r, a_i = a[s]
        l_r = lr_ref[s, pl.ds(r0, 8), :] + a_r * c_r + a_i * c_i
        l_i = li_ref[s, pl.ds(r0, 8), :] + a_r * c_i - a_i * c_r
        lr_ref[s, pl.ds(r0, 8), :] = l_r
        li_ref[s, pl.ds(r0, 8), :] = l_i
        return [l_r, l_i, d_r + l_r * prev_r + l_i * prev_i, d_i - l_r * prev_i + l_i * prev_r]

    def step(i, carry):
        carry = list(carry)
        for uu in range(S5_UNROLL):
            t = tc - 1 - (i * S5_UNROLL + uu)
            r0 = pl.multiple_of(t * 8, 8)
            p0 = pl.multiple_of((t - 1) * 8, 8)
            for s in range(nblk):
                carry[4 * s:4 * s + 4] = one(s, r0, xr_ref[s, pl.ds(p0, 8), :], xi_ref[s, pl.ds(p0, 8), :],
                                             carry[4 * s:4 * s + 4])
        return tuple(carry)

    init = []
    for s in range(nblk):
        init += [cr_ref[s], ci_ref[s], accr_ref[s], acci_ref[s]]
    carry = list(lax.fori_loop(0, tc // S5_UNROLL - 1, step, tuple(init)))
    for t in range(S5_UNROLL - 1, -1, -1):
        for s in range(nblk):
            if t > 0:
                prev_r, prev_i = xr_ref[s, 8 * (t - 1):8 * t, :], xi_ref[s, 8 * (t - 1):8 * t, :]
            else:
                prev_r, prev_i = h_ref[0, s], h_ref[1, s]
            carry[4 * s:4 * s + 4] = one(s, 8 * t, prev_r, prev_i, carry[4 * s:4 * s + 4])
    for s in range(nblk):
        cr_ref[s], ci_ref[s], accr_ref[s], acci_ref[s] = carry[4 * s:4 * s + 4]


def _gelu(y):
    k = math.sqrt(2.0 / math.pi)
    return 0.5 * y * (1.0 + jnp.tanh(k * (y + 0.044715 * (y * y * y))))


def _gelu_grad(y):
    k = math.sqrt(2.0 / math.pi)
    t = jnp.tanh(k * (y + 0.044715 * (y * y * y)))
    return 0.5 * (1.0 + t) + 0.5 * y * (1.0 - t * t) * (k * (1.0 + 3.0 * 0.044715 * (y * y)))


def _s5_specs(tc, nch, sbk, rev):
    def ch(c):
        return nch - 1 - c if rev else c

    return dict(
        act=pl.BlockSpec((tc, sbk * S5_CH), lambda i, c: (ch(c), i)),
        bb=pl.BlockSpec((2, sbk, S5_CH, 128), lambda i, c: (0, i, 0, 0)),
        cc=pl.BlockSpec((2, sbk, S5_P, S5_CH), lambda i, c: (0, i, 0, 0)),
        a=pl.BlockSpec((sbk, 8, 128), lambda i, c: (i, 0, 0)),
        d=pl.BlockSpec((1, sbk * S5_CH), lambda i, c: (0, i)),
        h=pl.BlockSpec((None, 2, sbk, 8, 128), lambda i, c: (ch(c), 0, i, 0, 0)),
    )


def _s5_blocks(nb, pref):
    return max(b for b in range(1, pref + 1) if nb % b == 0)


def _s5_group_masks():
    rb = lax.broadcasted_iota(jnp.int32, (S5_CH, S5_W), 0) // S5_C
    qb = lax.broadcasted_iota(jnp.int32, (S5_CH, S5_W), 1) // S5_P
    qc = lax.broadcasted_iota(jnp.int32, (S5_W, S5_CH), 0) // S5_P
    rc = lax.broadcasted_iota(jnp.int32, (S5_W, S5_CH), 1) // S5_C
    return rb == qb, qc == rc


def _s5_expand(bb_ref, cc_ref, bbd, ccd, sbk):
    mask_b, mask_c = _s5_group_masks()
    for k in range(2):
        for s in range(sbk):
            bbd[k, s] = jnp.where(mask_b, jnp.tile(bb_ref[k, s], (1, S5_W // 128)), 0).astype(BF16)
            ccd[k, s] = jnp.where(mask_c, jnp.tile(cc_ref[k, s], (S5_GB, 1)), 0).astype(BF16)


def _s5_fwd(u, bb2, cc2, a_re, a_im, d_skip):
    T, D = u.shape
    nb = D // S5_CH
    sbk = _s5_blocks(nb, 4)
    tc = _tile(T, 512, 8)
    nch = T // tc
    sp = _s5_specs(tc, nch, sbk, False)

    def body(u_ref, bb_ref, cc_ref, ar_ref, ai_ref, d_ref, z_ref, gp_ref, h_ref, xr, xi, cr, ci, bbd, ccd):
        @pl.when(pl.program_id(1) == 0)
        def _():
            cr[...] = jnp.zeros_like(cr)
            ci[...] = jnp.zeros_like(ci)
            _s5_expand(bb_ref, cc_ref, bbd, ccd, sbk)

        h_ref[0] = cr[...]
        h_ref[1] = ci[...]
        for s in range(sbk):
            ub = u_ref[:, s * S5_CH:(s + 1) * S5_CH].astype(BF16)
            _to_state_tiles(xr, s, _dot(ub, bbd[0, s]))
            _to_state_tiles(xi, s, _dot(ub, bbd[1, s]))
        _s5_scan_fwd(xr, xi, ar_ref, ai_ref, cr, ci, tc, sbk)
        for s in range(sbk):
            cols = slice(s * S5_CH, (s + 1) * S5_CH)
            y = (_dot(_from_state_tiles(xr, s, tc).astype(BF16), ccd[0, s])
                 - _dot(_from_state_tiles(xi, s, tc).astype(BF16), ccd[1, s])
                 + d_ref[:, cols] * u_ref[:, cols])
            z_ref[:, cols] = _gelu(y).astype(BF16)
            gp_ref[:, cols] = _gelu_grad(y).astype(BF16)

    tiles = pltpu.VMEM((sbk, tc * 8, 128), F32)
    carry = pltpu.VMEM((sbk, 8, 128), F32)
    return pl.pallas_call(
        body, grid=(nb // sbk, nch),
        in_specs=[sp["act"], sp["bb"], sp["cc"], sp["a"], sp["a"], sp["d"]],
        out_specs=[sp["act"], sp["act"], sp["h"]],
        out_shape=[jax.ShapeDtypeStruct((T, D), BF16), jax.ShapeDtypeStruct((T, D), BF16),
                   jax.ShapeDtypeStruct((nch, 2, nb, 8, 128), F32)],
        scratch_shapes=[tiles, tiles, carry, carry, pltpu.VMEM((2, sbk, S5_CH, S5_W), BF16),
                        pltpu.VMEM((2, sbk, S5_W, S5_CH), BF16)],
        compiler_params=_params(("parallel", "arbitrary")), name="s5_fwd",
    )(u, bb2, cc2, a_re, a_im, d_skip)


def _s5_bwd(u, dz, gp, h0, bb2, cc2, a_re, a_im, d_skip, dep=None):
    T, D = u.shape
    nb = D // S5_CH
    sbk = _s5_blocks(nb, 2)
    tc = _tile(T, 512, 8)
    nch = T // tc
    sp = _s5_specs(tc, nch, sbk, True)

    n_dep = 0 if dep is None else 1

    def body(u_ref, dz_ref, gp_ref, h_ref, bb_ref, cc_ref, ar_ref, ai_ref, d_ref, *rest):
        (du_ref, dd_ref, dar_ref, dai_ref, dbb_ref, dcc_ref,
         xr, xi, lr, li, fr, fi, br, bi, accr, acci, bbd, ccd, dbbd, dccd) = rest[n_dep:]
        c = pl.program_id(1)

        @pl.when(c == 0)
        def _():
            for ref in (br, bi, accr, acci, dd_ref, dbbd, dccd):
                ref[...] = jnp.zeros_like(ref)
            _s5_expand(bb_ref, cc_ref, bbd, ccd, sbk)

        for s in range(sbk):
            ub = u_ref[:, s * S5_CH:(s + 1) * S5_CH].astype(BF16)
            _to_state_tiles(xr, s, _dot(ub, bbd[0, s]))
            _to_state_tiles(xi, s, _dot(ub, bbd[1, s]))
        fr[...] = h_ref[0]
        fi[...] = h_ref[1]
        _s5_scan_fwd(xr, xi, ar_ref, ai_ref, fr, fi, tc, sbk)
        for s in range(sbk):
            cols = slice(s * S5_CH, (s + 1) * S5_CH)
            uv = u_ref[:, cols]
            xrb = _from_state_tiles(xr, s, tc).astype(BF16)
            xib = _from_state_tiles(xi, s, tc).astype(BF16)
            dsk = d_ref[:, cols]
            dy = dz_ref[:, cols].astype(F32) * gp_ref[:, cols].astype(F32)
            dd_ref[:, cols] += jnp.sum(dy * uv, axis=0, keepdims=True)
            dyb = dy.astype(BF16)
            dccd[0, s] += _dot(xrb, dyb, TN)
            dccd[1, s] += _dot(xib, dyb, TN)
            _to_state_tiles(lr, s, _dot(dyb, ccd[0, s], NT))
            _to_state_tiles(li, s, -_dot(dyb, ccd[1, s], NT))
            du_ref[:, cols] = dy * dsk
        _s5_scan_bwd(lr, li, xr, xi, h_ref, ar_ref, ai_ref, br, bi, accr, acci, tc, sbk)
        for s in range(sbk):
            cols = slice(s * S5_CH, (s + 1) * S5_CH)
            ub = u_ref[:, cols].astype(BF16)
            lrb = _from_state_tiles(lr, s, tc).astype(BF16)
            lib = _from_state_tiles(li, s, tc).astype(BF16)
            dbbd[0, s] += _dot(ub, lrb, TN)
            dbbd[1, s] += _dot(ub, lib, TN)
            du_ref[:, cols] += _dot(lrb, bbd[0, s], NT) + _dot(lib, bbd[1, s], NT)

        @pl.when(c == nch - 1)
        def _():
            dar_ref[...] = accr[...]
            dai_ref[...] = acci[...]
            mask_b, mask_c = _s5_group_masks()
            for k in range(2):
                for s in range(sbk):
                    mb = jnp.where(mask_b, dbbd[k, s], 0.0)
                    fold = functools.reduce(
                        lambda a, b: a + b, [mb[:, 128 * j:128 * (j + 1)] for j in range(S5_W // 128)])
                    dbb_ref[k, s] = fold + pltpu.roll(fold, S5_P, 1)
                    mc = jnp.where(mask_c, dccd[k, s], 0.0)
                    dcc_ref[k, s] = functools.reduce(
                        lambda a, b: a + b, [mc[S5_P * j:S5_P * (j + 1), :] for j in range(S5_GB)])

    tiles = pltpu.VMEM((sbk, tc * 8, 128), F32)
    carry = pltpu.VMEM((sbk, 8, 128), F32)
    ins = [u, dz, gp, h0, bb2, cc2, a_re, a_im, d_skip]
    in_specs = [sp["act"], sp["act"], sp["act"], sp["h"], sp["bb"], sp["cc"], sp["a"], sp["a"], sp["d"]]
    _dep_operand(ins, in_specs, dep)
    return pl.pallas_call(
        body, grid=(nb // sbk, nch),
        in_specs=in_specs,
        out_specs=[sp["act"], sp["d"], sp["a"], sp["a"], sp["bb"], sp["cc"]],
        out_shape=[jax.ShapeDtypeStruct((T, D), F32), jax.ShapeDtypeStruct((1, D), F32),
                   jax.ShapeDtypeStruct((nb, 8, 128), F32), jax.ShapeDtypeStruct((nb, 8, 128), F32),
                   jax.ShapeDtypeStruct((2, nb, S5_CH, 128), F32), jax.ShapeDtypeStruct((2, nb, S5_P, S5_CH), F32)],
        scratch_shapes=[tiles, tiles, tiles, tiles, carry, carry, carry, carry, carry, carry,
                        pltpu.VMEM((2, sbk, S5_CH, S5_W), BF16), pltpu.VMEM((2, sbk, S5_W, S5_CH), BF16),
                        pltpu.VMEM((2, sbk, S5_CH, S5_W), F32), pltpu.VMEM((2, sbk, S5_W, S5_CH), F32)],
        compiler_params=_params(("parallel", "arbitrary")), name="s5_bwd",
    )(*ins)


def _s5_disc(lr, li, ldt):
    dt = jnp.exp(ldt)
    mag = jnp.exp(lr * dt)
    ang = li * dt
    cs, sn = jnp.cos(ang), jnp.sin(ang)
    lbr, lbi = mag * cs, mag * sn
    nr = lbr - 1.0
    den = lr * lr + li * li
    f_re = (nr * lr + lbi * li) / den
    f_im = (lbi * lr - nr * li) / den
    return dt, mag, cs, sn, lbr, lbi, nr, den, f_re, f_im


def _s5_param_fwd(lr, li, ldt, bt_re, bt_im):
    c, g, p = bt_re.shape

    def body(lr_ref, li_ref, ldt_ref, br_ref, bi_ref, lbr_ref, lbi_ref, bbr_ref, bbi_ref):
        _, _, _, _, lbr, lbi, _, _, f_re, f_im = _s5_disc(lr_ref[...], li_ref[...], ldt_ref[...])
        lbr_ref[...] = lbr
        lbi_ref[...] = lbi
        for ch in range(c):
            b_r, b_i = br_ref[ch], bi_ref[ch]
            bbr_ref[ch] = f_re * b_r - f_im * b_i
            bbi_ref[ch] = f_re * b_i + f_im * b_r

    gp = jax.ShapeDtypeStruct((g, p), F32)
    cgp = jax.ShapeDtypeStruct((c, g, p), F32)
    return pl.pallas_call(body, out_shape=[gp, gp, cgp, cgp], name="s5_param_fwd")(lr, li, ldt, bt_re, bt_im)


def _s5_param_bwd(lr, li, ldt, bt_re, bt_im, dlbr, dlbi, dbbr, dbbi):
    c, g, p = bt_re.shape

    def body(lr_ref, li_ref, ldt_ref, br_ref, bi_ref, dlbr_ref, dlbi_ref, dbbr_ref, dbbi_ref,
             dlr_ref, dli_ref, dldt_ref, dbr_ref, dbi_ref):
        l_r, l_i = lr_ref[...], li_ref[...]
        dt, mag, cs, sn, lbr, lbi, nr, den, f_re, f_im = _s5_disc(l_r, l_i, ldt_ref[...])
        dfr = jnp.zeros_like(l_r)
        dfi = jnp.zeros_like(l_r)
        for ch in range(c):
            b_r, b_i = br_ref[ch], bi_ref[ch]
            g_r, g_i = dbbr_ref[ch], dbbi_ref[ch]
            dbr_ref[ch] = f_re * g_r + f_im * g_i
            dbi_ref[ch] = f_re * g_i - f_im * g_r
            dfr = dfr + g_r * b_r + g_i * b_i
            dfi = dfi + g_i * b_r - g_r * b_i
        inv = 1.0 / den
        d_nr = (dfr * l_r - dfi * l_i) * inv
        d_lbi = (dfr * l_i + dfi * l_r) * inv + dlbi_ref[...]
        d_lbr = d_nr + dlbr_ref[...]
        d_den = -(dfr * f_re + dfi * f_im) * inv
        d_mag = d_lbr * cs + d_lbi * sn
        d_ang = d_lbi * lbr - d_lbr * lbi
        dlr_ref[...] = (dfr * nr + dfi * lbi) * inv + 2.0 * d_den * l_r + d_mag * mag * dt
        dli_ref[...] = (dfr * lbi - dfi * nr) * inv + 2.0 * d_den * l_i + d_ang * dt
        dldt_ref[...] = jnp.sum(d_mag * mag * l_r + d_ang * l_i, axis=1, keepdims=True) * dt

    gp = jax.ShapeDtypeStruct((g, p), F32)
    cgp = jax.ShapeDtypeStruct((c, g, p), F32)
    return pl.pallas_call(body, out_shape=[gp, gp, jax.ShapeDtypeStruct((g, 1), F32), cgp, cgp],
                          name="s5_param_bwd")(lr, li, ldt, bt_re, bt_im, dlbr, dlbi, dbbr, dbbi)


def _att_masks(rep, gb):
    rows = rep * ATT_BLK
    qi = lax.broadcasted_iota(jnp.int32, (rows, 2 * ATT_BLK), 0) % ATT_BLK
    si = lax.broadcasted_iota(jnp.int32, (rows, 2 * ATT_BLK), 1)
    prev = (si < ATT_BLK) & (si >= qi) & (gb > 0)
    cur = (si >= ATT_BLK) & (si - ATT_BLK <= qi)
    return prev | cur


def _att_rows(start, dil):
    return pl.ds(start, ATT_BLK) if dil == 1 else pl.ds(start, ATT_BLK, stride=dil)


def _att_plan(T, dil):
    span = ATT_BLK * dil
    sbr = max(span, min(T, 1024))
    return span, sbr, T // sbr


def _att_block(sb, i, sbr, span, dil):
    loc = (i // dil) * span + i % dil
    cur = sb * sbr + loc
    gb = sb * (sbr // span) + i // dil
    return loc, cur, jnp.where(gb > 0, cur - span, cur), gb


def _att_fwd(q, k, v, grp, dil):
    T = q.shape[0]
    H = q.shape[1] // HEAD_DIM // len(DILATIONS)
    rep = H // N_KV_HEADS
    span, sbr, nsb = _att_plan(T, dil)
    scale = HEAD_DIM ** -0.5

    def body(*refs):
        q_refs = refs[:rep]
        k_ref, v_ref, o_ref, l_ref = refs[rep:rep + 4]
        slabs = refs[rep + 4:]
        sb = pl.program_id(1)

        def blk(i, _):
            loc, cur, prv, gb = _att_block(sb, i, sbr, span, dil)
            rows = _att_rows(loc, dil)
            qs = jnp.concatenate([r[rows, :] for r in q_refs], axis=0).astype(BF16)
            kcat = jnp.concatenate([k_ref[_att_rows(prv, dil), :], k_ref[_att_rows(cur, dil), :]], axis=0)
            vcat = jnp.concatenate([v_ref[_att_rows(prv, dil), :], v_ref[_att_rows(cur, dil), :]], axis=0)
            s = jnp.where(_att_masks(rep, gb), _dot(qs, kcat.astype(BF16), NT) * scale, NEG_INF)
            m = jnp.max(s, axis=-1, keepdims=True)
            p = jnp.exp(s - m)
            l = jnp.sum(p, axis=-1, keepdims=True)
            o = _dot(p.astype(BF16), vcat.astype(BF16)) / l
            lse = jnp.broadcast_to(m + jnp.log(l), (rep * ATT_BLK, HEAD_DIM))
            for j in range(rep):
                part, cols = slice(j * ATT_BLK, (j + 1) * ATT_BLK), slice(j * HEAD_DIM, (j + 1) * HEAD_DIM)
                if dil == 1:
                    dst = pl.ds(pl.multiple_of(loc, ATT_BLK), ATT_BLK)
                    o_ref[dst, cols] = o[part].astype(BF16)
                    l_ref[dst, cols] = lse[part]
                else:
                    slabs[0][j, rows, :] = o[part]
                    slabs[1][j, rows, :] = lse[part]
            return 0

        lax.fori_loop(0, sbr // ATT_BLK, blk, 0)
        if dil > 1:
            for j in range(rep):
                o_ref[:, j * HEAD_DIM:(j + 1) * HEAD_DIM] = slabs[0][j].astype(BF16)
                l_ref[:, j * HEAD_DIM:(j + 1) * HEAD_DIM] = slabs[1][j]

    qspecs = [pl.BlockSpec((sbr, HEAD_DIM), functools.partial(lambda h, s, j: (s, grp * H + h * rep + j), j=j))
              for j in range(rep)]
    kspec = pl.BlockSpec((T, HEAD_DIM), lambda h, s: (0, h))
    ospec = pl.BlockSpec((sbr, rep * HEAD_DIM), lambda h, s: (s, h))
    slab = pltpu.VMEM((rep, sbr, HEAD_DIM), F32)
    return pl.pallas_call(
        body, grid=(N_KV_HEADS, nsb), in_specs=qspecs + [kspec, kspec], out_specs=[ospec, ospec],
        out_shape=[jax.ShapeDtypeStruct((T, H * HEAD_DIM), BF16), jax.ShapeDtypeStruct((T, H * HEAD_DIM), F32)],
        scratch_shapes=[] if dil == 1 else [slab, slab],
        compiler_params=_params(("parallel", "arbitrary")), name=f"att_fwd_d{dil}",
    )(*([q] * rep), k, v)


def _att_combine(outs, lses):
    T, W = outs[0].shape
    ng = len(outs)
    tr, tcol = _tile(T, 512, 8), _tile(W, 512)

    def body(*refs):
        o_refs, l_refs = refs[:ng], refs[ng:2 * ng]
        ob_ref, lse_ref = refs[2 * ng:]
        ls = [r[...] for r in l_refs]
        m = functools.reduce(jnp.maximum, ls)
        es = [jnp.exp(l - m) for l in ls]
        den = functools.reduce(lambda a, b: a + b, es)
        num = functools.reduce(lambda a, b: a + b, [e * o[...].astype(F32) for e, o in zip(es, o_refs)])
        ob_ref[...] = (num / den).astype(BF16)
        lse_ref[...] = m + jnp.log(den)

    spec = pl.BlockSpec((tr, tcol), lambda i, j: (i, j))
    return pl.pallas_call(
        body, grid=(T // tr, W // tcol), in_specs=[spec] * (2 * ng), out_specs=[spec, spec],
        out_shape=[jax.ShapeDtypeStruct((T, W), BF16), jax.ShapeDtypeStruct((T, W), F32)],
        compiler_params=_params(("parallel", "parallel")), name="att_combine")(*outs, *lses)


STAT_LANE = HEAD_DIM // 2


def _att_stats(lse, o, do):
    T, W = lse.shape
    tr = _tile(T, 256, 16)

    def body(l_ref, o_ref, do_ref, s_ref):
        lane = lax.broadcasted_iota(jnp.int32, (tr, HEAD_DIM), 1)
        for h in range(W // HEAD_DIM):
            cols = slice(h * HEAD_DIM, (h + 1) * HEAD_DIM)
            delta = jnp.sum(do_ref[:, cols] * o_ref[:, cols].astype(F32), axis=-1, keepdims=True)
            s_ref[:, cols] = jnp.where(lane < STAT_LANE, l_ref[:, cols], delta)

    spec = pl.BlockSpec((tr, W), lambda i: (i, 0))
    return pl.pallas_call(
        body, grid=(T // tr,), in_specs=[spec] * 3, out_specs=spec,
        out_shape=jax.ShapeDtypeStruct((T, W), F32),
        compiler_params=_params(("parallel",)), name="att_stats")(lse, o, do)


def _att_bwd(q, k, v, do, stats, dq, grp, dil):
    T = q.shape[0]
    H = do.shape[1] // HEAD_DIM
    rep = H // N_KV_HEADS
    hs = rep
    span, sbr, nsb = _att_plan(T, dil)
    scale = HEAD_DIM ** -0.5

    def body(*refs):
        q_refs, do_refs, st_refs = refs[:hs], refs[hs + 2:2 * hs + 2], refs[2 * hs + 2:3 * hs + 2]
        k_ref, v_ref = refs[hs], refs[hs + 1]
        dq_ref, dk_ref, dv_ref, dq_slab = refs[3 * hs + 3:]
        sb = pl.program_id(2)

        @pl.when((pl.program_id(1) == 0) & (sb == 0))
        def _():
            dk_ref[...] = jnp.zeros_like(dk_ref)
            dv_ref[...] = jnp.zeros_like(dv_ref)

        def blk(i, _):
            loc, cur, prv, gb = _att_block(sb, i, sbr, span, dil)
            rows, kc, kp = _att_rows(loc, dil), _att_rows(cur, dil), _att_rows(prv, dil)
            qs = jnp.concatenate([r[rows, :] for r in q_refs], axis=0).astype(BF16)
            dos = jnp.concatenate([r[rows, :] for r in do_refs], axis=0).astype(BF16)
            st = jnp.concatenate([r[rows, :] for r in st_refs], axis=0)
            kcat = jnp.concatenate([k_ref[kp, :], k_ref[kc, :]], axis=0).astype(BF16)
            vcat = jnp.concatenate([v_ref[kp, :], v_ref[kc, :]], axis=0).astype(BF16)
            s = _dot(qs, kcat, NT) * scale
            p = jnp.where(_att_masks(hs, gb), jnp.exp(s - st[:, 0:1]), 0.0)
            dp = _dot(dos, vcat, NT)
            ds = (p * (dp - st[:, STAT_LANE:STAT_LANE + 1]) * scale).astype(BF16)
            dvc = _dot(p.astype(BF16), dos, TN)
            dkc = _dot(ds, qs, TN)
            dqs = _dot(ds, kcat)
            for j in range(hs):
                dq_slab[j, rows, :] = dqs[j * ATT_BLK:(j + 1) * ATT_BLK]
            dk_ref[kc, :] += dkc[ATT_BLK:]
            dv_ref[kc, :] += dvc[ATT_BLK:]

            @pl.when(gb > 0)
            def _():
                dk_ref[kp, :] += dkc[:ATT_BLK]
                dv_ref[kp, :] += dvc[:ATT_BLK]

            return 0

        lax.fori_loop(0, sbr // ATT_BLK, blk, 0)
        for j in range(hs):
            dq_ref[:, j * HEAD_DIM:(j + 1) * HEAD_DIM] = dq_slab[j].astype(BF16)

    def head_specs(col0):
        return [pl.BlockSpec((sbr, HEAD_DIM),
                             functools.partial(lambda h, f, s, j: (s, col0 + h * rep + f * hs + j), j=j))
                for j in range(hs)]

    kspec = pl.BlockSpec((T, HEAD_DIM), lambda h, f, s: (0, h))
    dqspec = pl.BlockSpec((sbr, hs * HEAD_DIM), lambda h, f, s: (s, (grp * H + h * rep) // hs + f))
    n_in = 3 * hs + 3
    return pl.pallas_call(
        body, grid=(N_KV_HEADS, rep // hs, nsb),
        in_specs=head_specs(grp * H) + [kspec, kspec] + head_specs(0) + head_specs(0) + [ANY],
        out_specs=[dqspec, kspec, kspec],
        out_shape=[jax.ShapeDtypeStruct(dq.shape, BF16),
                   jax.ShapeDtypeStruct((T, N_KV_HEADS * HEAD_DIM), F32),
                   jax.ShapeDtypeStruct((T, N_KV_HEADS * HEAD_DIM), F32)],
        scratch_shapes=[pltpu.VMEM((hs, sbr, HEAD_DIM), F32)],
        input_output_aliases={n_in - 1: 0},
        compiler_params=_params(("parallel", "arbitrary", "arbitrary")), name=f"att_bwd_d{dil}",
    )(*([q] * hs), k, v, *([do] * hs), *([stats] * hs), dq)


def _sum_kv(dks, dvs):
    T, W = dks[0].shape
    ng = len(dks)
    tr = _tile(T, 512, 8)

    def body(*refs):
        o_ref = refs[2 * ng]
        o_ref[0] = functools.reduce(lambda a, b: a + b, [r[...] for r in refs[:ng]]).astype(BF16)
        o_ref[1] = functools.reduce(lambda a, b: a + b, [r[...] for r in refs[ng:2 * ng]]).astype(BF16)

    spec = pl.BlockSpec((tr, W), lambda i: (i, 0))
    return pl.pallas_call(
        body, grid=(T // tr,), in_specs=[spec] * (2 * ng),
        out_specs=pl.BlockSpec((2, tr, W), lambda i: (0, i, 0)),
        out_shape=jax.ShapeDtypeStruct((2, T, W), BF16),
        compiler_params=_params(("parallel",)), name="sum_kv")(*dks, *dvs)


def _local_step(x, tgt, w, wts, ready, dep0=None, small_ready=None):
    T, D = x.shape
    g = {}

    (u0,) = _rms_fwd("rms_a", x, [w["a_norm"]], [F32], dep=dep0)
    lbr, lbi, bbt_re, bbt_im = _s5_param_fwd(w["lam_re"], w["lam_im"], w["log_dt"], w["bt_re"], w["bt_im"])
    a_re, a_im = lbr.reshape(-1, 8, 128), lbi.reshape(-1, 8, 128)
    C, G, P = w["bt_re"].shape
    nb = G // S5_GB
    bb2 = jnp.stack([bbt_re, bbt_im]).transpose(0, 2, 1, 3).reshape(2, nb, S5_GB * C, P)
    bb2 = jnp.concatenate([bb2, bb2], axis=-1)
    cc2 = jnp.stack([w["c_re"], w["c_im"]]).reshape(2, nb, S5_GB * C, P).transpose(0, 1, 3, 2)
    z, gp, h0 = _s5_fwd(u0, bb2, cc2, a_re, a_im, w["s5_d"])
    w_glu = wts("glu", z)
    x1, vg = _mm_dual_fwd("glu_fwd", z, w_glu, x, "glu")

    def ffn_fwd(xin, layer):
        (nrm,) = _rms_fwd(f"rms_f{layer}", xin, [w["ffn_norm"][layer:layer + 1]], [BF16])
        w_in = wts(f"win{layer}", nrm)
        gu, act = _mm_dual_fwd(f"ffn_in{layer}", nrm, w_in, None, "ffn")
        w_out = wts(f"wout{layer}", act)
        xout = _mm_nn(f"ffn_out{layer}", act, w_out, res=xin)
        return xout, (nrm, gu, act, w_in, w_out)

    x2, saved0 = ffn_fwd(x1, 0)
    kvn, hb = _rms_fwd("rms_b", x2, [w["kv_norm"], w["b_norm"]], [BF16, BF16])
    w_kv, w_q, w_o = wts("wkv", hb), wts("wq", hb), wts("wo", hb)
    k, v = _mm_kv("kv_proj", kvn, w_kv)
    q = _mm_nn("q_proj", hb, w_q)
    outs, lses = [], []
    for grp, dil in enumerate(DILATIONS):
        o_g, l_g = _att_fwd(q, k, v, grp, dil)
        outs.append(o_g)
        lses.append(l_g)
    o, lse = _att_combine(outs, lses)
    wts("win1", o, prefetch=True)
    x3 = _mm_nn("o_proj", o, w_o, res=x2)
    x4, saved1 = ffn_fwd(x3, 1)
    loss_blk, dx4, dx4b, g["final_norm"] = _loss_head(x4, w["final_norm"], tgt)

    def ffn_bwd(dx, dxb, xin, saved, layer, dep):
        nrm, gu, act, w_in, w_out = saved
        dgu = _mm_nt_ffn_bwd(f"ffn_dact{layer}", dxb, w_out, gu, dep=dep)
        g_wout = _mm_tn(f"ffn_dwout{layer}", act, dxb)
        g_win = _mm_tn_pair(f"ffn_dwin{layer}", nrm, dgu)
        dn = _mm_nt_pair(f"ffn_dn{layer}", dgu, w_in, out_dtype=BF16)
        dxo, dxob, (dgn,) = _rms_bwd(f"rms_f{layer}_bwd", xin, dx, [w["ffn_norm"][layer:layer + 1]], [dn])
        tok = ready({f"win{layer}": g_win, f"wout{layer}": g_wout})
        return dxo, dxob, dgn, tok

    dx3, dx3b, dfn1, tok = ffn_bwd(dx4, dx4b, x3, saved1, 1, None)
    do = _mm_nt("o_proj_dx", [dx3b], w_o, dep=tok)
    g_wo = _mm_tn("o_proj_dw", o, dx3b)
    stats = _att_stats(lse, o, do)
    dq = lax.empty(q.shape, BF16)
    dks, dvs = [], []
    for grp, dil in enumerate(DILATIONS):
        dq, dk_g, dv_g = _att_bwd(q, k, v, do, stats, dq, grp, dil)
        dks.append(dk_g)
        dvs.append(dv_g)
    dkv = _sum_kv(dks, dvs)
    dhb = _mm_nt("q_proj_dx", [dq], w_q, out_dtype=BF16)
    g_wq = _mm_tn("q_proj_dw", hb, dq)
    dkvn = _mm_nt_pair("kv_proj_dx", dkv, w_kv, out_dtype=BF16)
    g_wkv = _mm_tn_pair("kv_proj_dw", kvn, dkv)
    dx2, dx2b, (g["kv_norm"], g["b_norm"]) = _rms_bwd(
        "rms_b_bwd", x2, dx3, [w["kv_norm"], w["b_norm"]], [dkvn, dhb])
    tok = ready({"wkv": g_wkv, "wq": g_wq, "wo": g_wo})
    dx1, dx1b, dfn0, tok = ffn_bwd(dx2, dx2b, x1, saved0, 0, tok)
    g["ffn_norm"] = jnp.concatenate([dfn0, dfn1], axis=0)

    dvg = _glu_bwd(dx1, vg, dep=tok)
    dz = _mm_nt_pair("glu_dx", dvg, w_glu, out_dtype=BF16)
    tok = ready({"glu": _mm_tn_pair("glu_dw", z, dvg)})
    tok = ready({}, after=tok)
    du, g["s5_d"], da_re, da_im, dbb2, dcc2 = _s5_bwd(
        u0, dz, gp, h0, bb2, cc2, a_re, a_im, w["s5_d"], dep=tok)
    dcc = dcc2.transpose(0, 1, 3, 2).reshape(2, G, C, P)
    g["c_re"], g["c_im"] = dcc[0], -dcc[1]
    dbbt = dbb2[..., :P].reshape(2, G, C, P).transpose(0, 2, 1, 3)
    g["lam_re"], g["lam_im"], g["log_dt"], g["bt_re"], g["bt_im"] = _s5_param_bwd(
        w["lam_re"], w["lam_im"], w["log_dt"], w["bt_re"], w["bt_im"],
        da_re.reshape(G, P), da_im.reshape(G, P), dbbt[0], dbbt[1])
    tok = small_ready(g) if small_ready is not None else None
    grad_x, _, (g["a_norm"],) = _rms_bwd("rms_a_bwd", x, dx1, [w["a_norm"]], [du], dep=tok)
    return loss_blk, grad_x, g


def _coords():
    return lax.axis_index("x"), lax.axis_index("y"), lax.axis_index("c")


def _dev_index(dev):
    return 4 * dev[0] + 2 * dev[1] + dev[2]


def _shard_window(ref, axis, width, idx):
    sl = [slice(None)] * len(ref.shape)
    sl[axis] = pl.ds(pl.multiple_of(idx * width, width), width)
    return ref.at[tuple(sl)]


def _all_gather(name, shards, axes):
    na = len(shards)
    widths = [s.shape[ax] for s, ax in zip(shards, axes)]
    out_shapes = []
    for s, ax in zip(shards, axes):
        shp = list(s.shape)
        shp[ax] *= N_DEV
        out_shapes.append(jax.ShapeDtypeStruct(tuple(shp), s.dtype))

    def body(*refs):
        ins, outs = refs[:na], refs[na:2 * na]
        send_sems, recv_sems, local_sems = refs[2 * na:]
        x, y, c = _coords()
        me, sib = (x, y, c), (x, y, 1 - c)
        chips = [(1 - x, y), (x, 1 - y), (1 - x, 1 - y)]

        def blk(a, dev):
            return _shard_window(outs[a], axes[a], widths[a], _dev_index(dev))

        def copy(a, kk, block, to, src=None):
            return pltpu.make_async_remote_copy(
                src_ref=blk(a, block) if src is None else src, dst_ref=blk(a, block),
                send_sem=send_sems.at[a, kk], recv_sem=recv_sems.at[a, kk],
                device_id=to, device_id_type=MESH)

        local = [pltpu.make_async_copy(ins[a], blk(a, me), local_sems.at[a]) for a in range(na)]
        for cp in local:
            cp.start()
        sent = []
        for a in range(na):
            first = [copy(a, 0, me, sib, src=ins[a])]
            first += [copy(a, 1 + j, me, (*chip, c), src=ins[a]) for j, chip in enumerate(chips)]
            for cp in first:
                cp.start()
            sent += first
        for a in range(na):
            for j, chip in enumerate(chips):
                copy(a, 1 + j, (*chip, c), me).wait_recv()
                fwd = copy(a, 4 + j, (*chip, c), sib)
                fwd.start()
                sent.append(fwd)
        for a in range(na):
            copy(a, 0, sib, me).wait_recv()
            for j, chip in enumerate(chips):
                copy(a, 4 + j, (*chip, 1 - c), me).wait_recv()
        for cp in sent:
            cp.wait_send()
        for cp in local:
            cp.wait()

    return pl.pallas_call(
        body, out_shape=out_shapes, in_specs=[ANY] * na, out_specs=[ANY] * na,
        scratch_shapes=[pltpu.SemaphoreType.DMA((na, 7)), pltpu.SemaphoreType.DMA((na, 7)),
                        pltpu.SemaphoreType.DMA((na,))],
        name=name)(*shards)


HBM = pl.BlockSpec(memory_space=pltpu.HBM)
SEM = pl.BlockSpec(memory_space=pltpu.SEMAPHORE)
TOKEN_SPEC = pl.BlockSpec(memory_space=pltpu.VMEM)
TOKEN_SHAPE = jax.ShapeDtypeStruct((8, 128), F32)
SPLIT_PARAMS = pltpu.CompilerParams(has_side_effects=pltpu.SideEffectType.DATAFLOW_SIDE_EFFECTING)


def _hbm(x):
    return pltpu.with_memory_space_constraint(x, pltpu.HBM)


def _hbm_like(x):
    return pltpu.HBM(x.shape, x.dtype)


def _dma_sems(*shape):
    return pltpu.SemaphoreType.DMA(shape)


def _cast_and_place(name, shard, layer, axis, pos, dtype, dep=None):
    n_dep = 0 if dep is None else 1
    rows, cols = shard.shape[-2:]
    tr = _tile(rows, 256, 16)
    nt = rows // tr
    full = (rows, cols * N_DEV) if axis == 1 else (rows * N_DEV, cols)

    def dev(p):
        return 4 * p[0] + 2 * p[1] + p[2]

    def body(pos_ref, s_ref, *refs):
        b_ref, l_ref = refs[n_dep:]
        v = s_ref[...].astype(dtype)
        b_ref[...] = v
        l_ref[...] = v

    blk = pl.BlockSpec((tr, cols), lambda i, p: (i, 0))
    if axis == 1:
        lspec = pl.BlockSpec((tr, cols), lambda i, p: (i, dev(p)))
    else:
        lspec = pl.BlockSpec((tr, cols), lambda i, p: (dev(p) * nt + i, 0))
    ins, in_specs = [shard], [pl.BlockSpec((None, tr, cols), lambda i, p: (layer, i, 0))]
    _dep_operand(ins, in_specs, dep)
    return pl.pallas_call(
        body, grid_spec=pltpu.PrefetchScalarGridSpec(
            num_scalar_prefetch=1, grid=(nt,), in_specs=in_specs, out_specs=[blk, lspec]),
        out_shape=[jax.ShapeDtypeStruct((rows, cols), dtype), jax.ShapeDtypeStruct(full, dtype)],
        compiler_params=_params(("parallel",)), name=name)(pos, *ins)


def _gather_start(name, shards, lands, axes, groups):
    na, ng = len(shards), len(groups)
    widths = [s.shape[ax] for s, ax in zip(shards, axes)]

    def body(*refs):
        sh, ld = refs[:na], refs[na:2 * na]
        sems = refs[2 * na:2 * na + 3 * ng]
        token = refs[-1]
        x, y, c = _coords()
        me, sib = (x, y, c), (x, y, 1 - c)
        chips = [(1 - x, y), (x, 1 - y), (1 - x, 1 - y)]
        for gi, grp in enumerate(groups):
            send, r_d2d, r_ici = sems[3 * gi:3 * gi + 3]
            for li, a in enumerate(grp):
                dst = _shard_window(ld[a], axes[a], widths[a], _dev_index(me))
                pltpu.make_async_remote_copy(
                    src_ref=sh[a], dst_ref=dst, send_sem=send.at[4 * li], recv_sem=r_d2d.at[li],
                    device_id=sib, device_id_type=MESH).start()
                for j, chip in enumerate(chips):
                    pltpu.make_async_remote_copy(
                        src_ref=sh[a], dst_ref=dst, send_sem=send.at[4 * li + 1 + j], recv_sem=r_ici.at[3 * li + j],
                        device_id=(*chip, c), device_id_type=MESH).start()
        token[...] = jnp.zeros_like(token)

    out_shape, out_specs = [], []
    for grp in groups:
        out_shape += [_dma_sems(4 * len(grp)), _dma_sems(len(grp)), _dma_sems(3 * len(grp))]
        out_specs += [SEM] * 3
    out_shape += [_hbm_like(s) for s in shards] + [_hbm_like(l) for l in lands] + [TOKEN_SHAPE]
    out_specs += [HBM] * (2 * na) + [TOKEN_SPEC]
    aliases = {a: 3 * ng + a for a in range(2 * na)}
    res = pl.pallas_call(
        body, name=name, out_shape=out_shape, in_specs=[HBM] * (2 * na),
        out_specs=out_specs, input_output_aliases=aliases, compiler_params=SPLIT_PARAMS,
    )(*[_hbm(s) for s in shards], *[_hbm(l) for l in lands])
    sems = [tuple(res[3 * gi:3 * gi + 3]) for gi in range(ng)]
    return sems, list(res[3 * ng:3 * ng + na]), list(res[3 * ng + na:3 * ng + 2 * na]), res[-1]


def _gather_forward(name, lands, axes, r_ici, after):
    n = len(lands)
    widths = [l.shape[ax] // N_DEV for l, ax in zip(lands, axes)]

    def body(*refs):
        ld, r_ici_ref = refs[:n], refs[n]
        f_send, f_recv = refs[n + 2], refs[n + 3]
        x, y, c = _coords()
        sib = (x, y, 1 - c)
        chips = [(1 - x, y), (x, 1 - y), (1 - x, 1 - y)]
        for li in range(n):
            for j, chip in enumerate(chips):
                blk = _shard_window(ld[li], axes[li], widths[li], _dev_index((*chip, c)))
                pltpu.make_async_remote_copy(
                    src_ref=blk, dst_ref=blk, send_sem=f_send.at[3 * li + j], recv_sem=r_ici_ref.at[3 * li + j],
                    device_id=(*chip, c), device_id_type=MESH).wait_recv()
                pltpu.make_async_remote_copy(
                    src_ref=blk, dst_ref=blk, send_sem=f_send.at[3 * li + j], recv_sem=f_recv.at[3 * li + j],
                    device_id=sib, device_id_type=MESH).start()

    res = pl.pallas_call(
        body, name=name, out_shape=[_dma_sems(3 * n), _dma_sems(3 * n)] + [_hbm_like(l) for l in lands],
        in_specs=[HBM] * n + [SEM, ANY], out_specs=[SEM, SEM] + [HBM] * n,
        input_output_aliases={li: 2 + li for li in range(n)}, compiler_params=SPLIT_PARAMS,
    )(*lands, r_ici, after)
    return res[0], res[1], list(res[2:])


def _gather_finish(name, shards, lands, axes, send, r_d2d, f_send, f_recv, after):
    n = len(lands)
    widths = [l.shape[ax] // N_DEV for l, ax in zip(lands, axes)]

    def body(*refs):
        sh, ld = refs[:n], refs[n:2 * n]
        send_ref, r_d2d_ref, f_send_ref, f_recv_ref = refs[2 * n:2 * n + 4]
        x, y, c = _coords()
        me, sib = (x, y, c), (x, y, 1 - c)
        chips = [(1 - x, y), (x, 1 - y), (1 - x, 1 - y)]

        def blk(li, dev):
            return _shard_window(ld[li], axes[li], widths[li], _dev_index(dev))

        for li in range(n):
            for kk in range(4):
                pltpu.make_async_remote_copy(
                    src_ref=sh[li], dst_ref=blk(li, me), send_sem=send_ref.at[4 * li + kk], recv_sem=r_d2d_ref.at[li],
                    device_id=sib, device_id_type=MESH).wait_send()
            pltpu.make_async_remote_copy(
                src_ref=blk(li, sib), dst_ref=blk(li, sib), send_sem=send_ref.at[4 * li], recv_sem=r_d2d_ref.at[li],
                device_id=sib, device_id_type=MESH).wait_recv()
            for j, chip in enumerate(chips):
                pltpu.make_async_remote_copy(
                    src_ref=blk(li, (*chip, c)), dst_ref=blk(li, (*chip, c)), send_sem=f_send_ref.at[3 * li + j],
                    recv_sem=f_recv_ref.at[3 * li + j], device_id=sib, device_id_type=MESH).wait_send()
                pltpu.make_async_remote_copy(
                    src_ref=blk(li, (*chip, 1 - c)), dst_ref=blk(li, (*chip, 1 - c)), send_sem=f_send_ref.at[3 * li + j],
                    recv_sem=f_recv_ref.at[3 * li + j], device_id=sib, device_id_type=MESH).wait_recv()

    res = pl.pallas_call(
        body, name=name, out_shape=[_hbm_like(s) for s in shards] + [_hbm_like(l) for l in lands],
        in_specs=[HBM] * (2 * n) + [SEM] * 4 + [ANY], out_specs=[HBM] * (2 * n),
        input_output_aliases={i: i for i in range(2 * n)}, compiler_params=SPLIT_PARAMS,
    )(*shards, *lands, send, r_d2d, f_send, f_recv, after)
    return list(res[n:])


def _chip_exchange_start(name, parts):
    n = len(parts)

    def body(*refs):
        src, ld = refs[:n], refs[n:2 * n]
        send, recv = refs[2 * n], refs[2 * n + 1]
        token = refs[-1]
        x, y, c = _coords()
        chips = [(1 - x, y), (x, 1 - y), (1 - x, 1 - y)]
        for li in range(n):
            for kk, chip in enumerate(chips):
                pltpu.make_async_remote_copy(
                    src_ref=src[li].at[kk], dst_ref=ld[li].at[kk], send_sem=send.at[3 * li + kk],
                    recv_sem=recv.at[3 * li + kk], device_id=(*chip, c), device_id_type=MESH).start()
        token[...] = jnp.zeros_like(token)

    lands = [lax.empty(p.shape, p.dtype) for p in parts]
    res = pl.pallas_call(
        body, name=name,
        out_shape=[_dma_sems(3 * n), _dma_sems(3 * n)] + [_hbm_like(p) for p in parts] * 2 + [TOKEN_SHAPE],
        in_specs=[HBM] * (2 * n), out_specs=[SEM, SEM] + [HBM] * (2 * n) + [TOKEN_SPEC],
        input_output_aliases={i: 2 + i for i in range(2 * n)}, compiler_params=SPLIT_PARAMS,
    )(*[_hbm(p) for p in parts], *[_hbm(l) for l in lands])
    return res[0], res[1], list(res[2:2 + n]), list(res[2 + n:2 + 2 * n]), res[-1]


def _chip_exchange_finish(name, started, after):
    counts = [len(st[2]) for st in started]
    total = sum(counts)
    ns = len(started)

    def body(*refs):
        src, ld = refs[:total], refs[total:2 * total]
        sems = refs[2 * total:2 * total + 2 * ns]
        x, y, c = _coords()
        chips = [(1 - x, y), (x, 1 - y), (1 - x, 1 - y)]
        off = 0
        for si, cnt in enumerate(counts):
            send, recv = sems[2 * si], sems[2 * si + 1]
            for li in range(cnt):
                for kk, chip in enumerate(chips):
                    cp = pltpu.make_async_remote_copy(
                        src_ref=src[off + li].at[kk], dst_ref=ld[off + li].at[kk], send_sem=send.at[3 * li + kk],
                        recv_sem=recv.at[3 * li + kk], device_id=(*chip, c), device_id_type=MESH)
                    cp.wait_send()
                    cp.wait_recv()
            off += cnt

    srcs = [p for st in started for p in st[2]]
    lands = [l for st in started for l in st[3]]
    sems = [s for st in started for s in st[:2]]
    res = pl.pallas_call(
        body, name=name, out_shape=[_hbm_like(p) for p in srcs + lands],
        in_specs=[HBM] * (2 * total) + [SEM] * (2 * ns) + [ANY], out_specs=[HBM] * (2 * total),
        input_output_aliases={i: i for i in range(2 * total)}, compiler_params=SPLIT_PARAMS,
    )(*srcs, *lands, *sems, after)
    out, off = [], total
    for cnt in counts:
        out.append(list(res[off:off + cnt]))
        off += cnt
    return out


def _pair_copies(src, ld, send, recv, axes, widths):
    x, y, c = _coords()
    chips = [(x, y), (1 - x, y), (x, 1 - y), (1 - x, 1 - y)]
    return [pltpu.make_async_remote_copy(
        src_ref=_shard_window(src[li], axes[li], widths[li], _dev_index((*chip, 1 - c))),
        dst_ref=ld[li].at[kk], send_sem=send.at[4 * li + kk], recv_sem=recv.at[4 * li + kk],
        device_id=(x, y, 1 - c), device_id_type=MESH)
        for li in range(len(src)) for kk, chip in enumerate(chips)]


def _pair_exchange_start(name, grads, axes):
    n = len(grads)
    widths = [gr.shape[ax] // N_DEV for gr, ax in zip(grads, axes)]
    lands = []
    for gr, ax, wd in zip(grads, axes, widths):
        shp = list(gr.shape)
        shp[ax] = wd
        lands.append(lax.empty((4, *shp), gr.dtype))

    def body(*refs):
        for cp in _pair_copies(refs[:n], refs[n:2 * n], refs[2 * n], refs[2 * n + 1], axes, widths):
            cp.start()
        refs[-1][...] = jnp.zeros_like(refs[-1])

    res = pl.pallas_call(
        body, name=name,
        out_shape=[_dma_sems(4 * n), _dma_sems(4 * n)] + [_hbm_like(a) for a in grads + lands] + [TOKEN_SHAPE],
        in_specs=[HBM] * (2 * n), out_specs=[SEM, SEM] + [HBM] * (2 * n) + [TOKEN_SPEC],
        input_output_aliases={i: 2 + i for i in range(2 * n)}, compiler_params=SPLIT_PARAMS,
    )(*[_hbm(a) for a in grads + lands])
    return res[0], res[1], list(res[2:2 + n]), list(res[2 + n:2 + 2 * n]), res[-1]


def _pair_exchange_finish(name, started, axes, after):
    send, recv, grads, lands, _ = started
    n = len(grads)
    widths = [gr.shape[ax] // N_DEV for gr, ax in zip(grads, axes)]

    def body(*refs):
        for cp in _pair_copies(refs[:n], refs[n:2 * n], refs[2 * n], refs[2 * n + 1], axes, widths):
            cp.wait_send()
            cp.wait_recv()

    res = pl.pallas_call(
        body, name=name, out_shape=[_hbm_like(a) for a in grads + lands],
        in_specs=[HBM] * (2 * n) + [SEM, SEM, ANY], out_specs=[HBM] * (2 * n),
        input_output_aliases={i: i for i in range(2 * n)}, compiler_params=SPLIT_PARAMS,
    )(*grads, *lands, send, recv, after)
    return list(res[:n]), list(res[n:])


def _pair_sum(name, grad, land, axis, pos):
    wd = grad.shape[axis] // N_DEV
    shard_shape = land.shape[1:]
    rows, cols = shard_shape
    tr = _tile(rows, 256, 16)
    nt = rows // tr

    def dev_of(kk, pos_ref):
        return 4 * (pos_ref[0] ^ (kk & 1)) + 2 * (pos_ref[1] ^ (kk >> 1)) + pos_ref[2]

    def gspec(kk):
        if axis == 1:
            return pl.BlockSpec((tr, wd), lambda t, p: (t, dev_of(kk, p)))
        return pl.BlockSpec((tr, cols), lambda t, p: (dev_of(kk, p) * nt + t, 0))

    def body(pos_ref, g0, g1, g2, g3, l_ref, p0_ref, pb_ref):
        p0_ref[...] = g0[...].astype(F32) + l_ref[0].astype(F32)
        for kk, g_ref in enumerate((g1, g2, g3)):
            pb_ref[kk] = (g_ref[...].astype(F32) + l_ref[kk + 1].astype(F32)).astype(BF16)

    return pl.pallas_call(
        body,
        grid_spec=pltpu.PrefetchScalarGridSpec(
            num_scalar_prefetch=1, grid=(nt,),
            in_specs=[gspec(kk) for kk in range(4)] + [pl.BlockSpec((4, tr, cols), lambda t, p: (0, t, 0))],
            out_specs=[pl.BlockSpec((tr, cols), lambda t, p: (t, 0)),
                       pl.BlockSpec((3, tr, cols), lambda t, p: (0, t, 0))]),
        out_shape=[jax.ShapeDtypeStruct(shard_shape, F32), jax.ShapeDtypeStruct((3, *shard_shape), BF16)],
        compiler_params=_params(("parallel",)), name=name)(pos, grad, grad, grad, grad, land)


def _adamw_math(g, w, m, v):
    c1 = 1.0 - ADAM_B1 ** ADAM_STEP
    c2 = 1.0 - ADAM_B2 ** ADAM_STEP
    nm = ADAM_B1 * m + (1.0 - ADAM_B1) * g
    nv = ADAM_B2 * v + (1.0 - ADAM_B2) * (g * g)
    return -ADAM_LR * ((nm / c1) / (jnp.sqrt(nv / c2) + ADAM_EPS) + ADAM_WD * w), nm, nv


def _sum_parts(name, parts):
    n, rows, cols = parts.shape
    tr = _tile(rows, 1024, 8)

    def body(p_ref, o_ref):
        o_ref[...] = functools.reduce(lambda a, b: a + b, [p_ref[i] for i in range(n)])

    return pl.pallas_call(
        body, grid=(rows // tr,), in_specs=[pl.BlockSpec((n, tr, cols), lambda i: (0, i, 0))],
        out_specs=pl.BlockSpec((tr, cols), lambda i: (i, 0)), out_shape=jax.ShapeDtypeStruct((rows, cols), F32),
        compiler_params=_params(("parallel",)), name=name)(parts)


def _adamw_multi(name, gs, ws, ms, vs):
    k = len(gs)
    rows, cols = ws[0].shape
    tr = _tile(rows, 1024, 8)

    def body(*refs):
        for i in range(k):
            g_ref, w_ref, m_ref, v_ref = (refs[j * k + i] for j in range(4))
            og_ref, d_ref, nm_ref, nv_ref = (refs[(4 + j) * k + i] for j in range(4))
            g = g_ref[...]
            og_ref[...] = g
            d_ref[...], nm_ref[...], nv_ref[...] = _adamw_math(g, w_ref[...], m_ref[...], v_ref[...])

    spec = pl.BlockSpec((tr, cols), lambda i: (i, 0))
    res = pl.pallas_call(
        body, grid=(rows // tr,), in_specs=[spec] * (4 * k), out_specs=[spec] * (4 * k),
        out_shape=[jax.ShapeDtypeStruct((rows, cols), F32)] * (4 * k),
        compiler_params=_params(("parallel",)), name=name)(*gs, *ws, *ms, *vs)
    return [[res[j * k + i] for j in range(4)] for i in range(k)]


def _adamw(name, parts, w, m, v, layer=None, prev=None):
    rows, cols = w.shape[-2:]
    tr = _tile(rows, 256, 8)
    npart = len(parts)

    def body(*refs):
        p_refs = refs[:npart]
        w_ref, m_ref, v_ref = refs[npart:npart + 3]
        g_ref, d_ref, nm_ref, nv_ref = refs[-4:]
        g = None
        for r in p_refs:
            if len(r.shape) == 3:
                for i in range(r.shape[0]):
                    t = r[i].astype(F32)
                    g = t if g is None else g + t
            else:
                t = r[...].astype(F32)
                g = t if g is None else g + t
        g_ref[...] = g
        d_ref[...], nm_ref[...], nv_ref[...] = _adamw_math(g, w_ref[...], m_ref[...], v_ref[...])

    spec = pl.BlockSpec((tr, cols), lambda i: (i, 0))
    wspec = spec if layer is None else pl.BlockSpec((None, tr, cols), lambda i: (layer, i, 0))
    pspecs = [pl.BlockSpec((p.shape[0], tr, cols), lambda i: (0, i, 0)) if p.ndim == 3 else spec
              for p in parts]
    prev = list(prev) if prev else []
    return pl.pallas_call(
        body, grid=(rows // tr,), in_specs=pspecs + [wspec] * 3 + [ANY] * len(prev), out_specs=[wspec] * 4,
        out_shape=[jax.ShapeDtypeStruct(w.shape, F32)] * 4,
        input_output_aliases={npart + 3 + i: i for i in range(len(prev))},
        compiler_params=_params(("parallel",)), name=name)(*parts, w, m, v, *prev)


SMALL_NAMES = ("s5_lam_re", "s5_lam_im", "s5_log_dt", "s5_b_re", "s5_b_im", "s5_c_re", "s5_c_im",
               "ffn_norm", "b_norm_mix", "kv_norm", "final_norm")
SMALL_PAD = 1024
SMALL_GROUPS = (("s5_lam_re", "s5_lam_im"), ("s5_log_dt",), ("s5_b_re", "s5_b_im"), ("s5_c_re", "s5_c_im"),
                ("ffn_norm",), ("b_norm_mix", "kv_norm", "final_norm"))


def _pack(parts):
    flat = []
    for p in parts:
        f = p.reshape(-1)
        pad = (-f.shape[0]) % SMALL_PAD
        if pad:
            f = jnp.concatenate([f, jnp.zeros((pad,), f.dtype)])
        flat.append(f)
    return jnp.concatenate(flat).reshape(-1, 128)


def _unpack(packed, shapes):
    flat = packed.reshape(-1)
    out, off = [], 0
    for shp in shapes:
        size = math.prod(shp)
        out.append(flat[off:off + size].reshape(shp))
        off += size + (-size) % SMALL_PAD
    return out


def kernel(x, s5_lam_re, s5_lam_im, s5_log_dt, s5_b_re, s5_b_im, s5_c_re, s5_c_im, s5_d, s5_w_glu, a_norm_mix, ffn_norm, ffn_w_in, ffn_w_out, b_norm_mix, attn_w_q, attn_w_o, kv_norm, w_kv, final_norm, loss_target, m_s5_lam_re, m_s5_lam_im, m_s5_log_dt, m_s5_b_re, m_s5_b_im, m_s5_c_re, m_s5_c_im, m_s5_d, m_s5_w_glu, m_a_norm_mix, m_ffn_norm, m_ffn_w_in, m_ffn_w_out, m_b_norm_mix, m_attn_w_q, m_attn_w_o, m_kv_norm, m_w_kv, m_final_norm, v_s5_lam_re, v_s5_lam_im, v_s5_log_dt, v_s5_b_re, v_s5_b_im, v_s5_c_re, v_s5_c_im, v_s5_d, v_s5_w_glu, v_a_norm_mix, v_ffn_norm, v_ffn_w_in, v_ffn_w_out, v_b_norm_mix, v_attn_w_q, v_attn_w_o, v_kv_norm, v_w_kv, v_final_norm):
    args = dict(locals())
    T, D = x.shape[1], x.shape[2]
    n_layers = ffn_w_in.shape[0]
    xi_, yi_, ci_ = _coords()
    pos = jnp.stack([xi_, yi_, ci_]).astype(jnp.int32)
    me = 4 * xi_ + 2 * yi_ + ci_

    big_names = ["glu"] + [f"win{l}" for l in range(n_layers)] + [f"wout{l}" for l in range(n_layers)] \
        + ["wkv", "wq", "wo"]
    big_shards = [(s5_w_glu, 0)] + [(ffn_w_in, l) for l in range(n_layers)] \
        + [(ffn_w_out, l) for l in range(n_layers)] + [(w_kv[None], 0), (attn_w_q, 0), (attn_w_o, 0)]
    big_axes = [1] + [1] * n_layers + [0] * n_layers + [0, 1, 0]
    big_out_names = ["s5_w_glu"] + ["ffn_w_in"] * n_layers + ["ffn_w_out"] * n_layers \
        + ["w_kv", "attn_w_q", "attn_w_o"]
    index_of = {n: i for i, n in enumerate(big_names)}
    vec_shard = jnp.concatenate([s5_d, a_norm_mix], axis=0)
    (vecs,) = _all_gather("vectors_all_gather", [vec_shard], [1])

    gather_groups = [["glu"], ["win0"], ["wout0"], ["wkv", "wq", "wo"], ["win1", "wout1"]]
    group_idx = [[index_of[n] for n in grp] for grp in gather_groups]
    group_of = {n: gi for gi, grp in enumerate(gather_groups) for n in grp}
    gather_sems = [None] * len(gather_groups)
    shards_thru, lands_thru = [None] * len(big_names), [None] * len(big_names)
    start_token = None
    for tag, gis in (("first", (0, 1)), ("rest", (2, 3, 4))):
        idx = [a for gi in gis for a in group_idx[gi]]
        placed = [_cast_and_place(f"cast_place_{big_names[a]}", *big_shards[a], big_axes[a], pos, BF16,
                                  dep=start_token) for a in idx]
        sems, sent, lands, start_token = _gather_start(
            f"weights_gather_start_{tag}", [p[0] for p in placed], [p[1] for p in placed],
            [big_axes[a] for a in idx], [[idx.index(a) for a in group_idx[gi]] for gi in gis])
        for gi, sem in zip(gis, sems):
            gather_sems[gi] = sem
        for a, sh, ld in zip(idx, sent, lands):
            shards_thru[a], lands_thru[a] = sh, ld
    full = {}

    forwarded = {}

    def wts(name, after, prefetch=False):
        gi = group_of[name]
        idx = group_idx[gi]
        axes = [big_axes[a] for a in idx]
        send, r_d2d, r_ici = gather_sems[gi]
        if gi not in forwarded:
            forwarded[gi] = _gather_forward(
                f"weights_gather_forward{gi}", [lands_thru[a] for a in idx], axes, r_ici, after)
        if prefetch:
            return None
        if name not in full:
            f_send, f_recv, lands = forwarded[gi]
            done = _gather_finish(f"weights_gather_finish{gi}", [shards_thru[a] for a in idx], lands, axes,
                                  send, r_d2d, f_send, f_recv, after)
            full.update(zip(gather_groups[gi], done))
        return full[name]

    exchanges, pending = [], []

    def flush(after):
        names, axes, started = pending.pop()
        tag = "_".join(names)
        grads, lands = _pair_exchange_finish(f"rs_pair_exchange_finish_{tag}", started, axes, after)
        p0s, pbs = [], []
        for n, gr, land, ax in zip(names, grads, lands, axes):
            p0, pb = _pair_sum(f"rs_pair_sum_{n}", gr, land, ax, pos)
            p0s.append(p0)
            pbs.append(pb)
        started = _chip_exchange_start(f"rs_chip_exchange_start_{tag}", pbs)
        exchanges.append((names, p0s, started))
        return started[4]

    def ready(grads, after=None):
        if not grads:
            return flush(after)
        names = list(grads)
        axes = [big_axes[index_of[n]] for n in names]
        started = _pair_exchange_start(f"rs_pair_exchange_start_{'_'.join(names)}", [grads[n] for n in names], axes)
        token = flush(started[4]) if pending else started[4]
        pending.append((names, axes, started))
        return token

    G, P, C = s5_b_re.shape[1:]
    w = dict(
        a_norm=vecs[1:2], s5_d=vecs[0:1],
        ffn_norm=ffn_norm, b_norm=b_norm_mix, kv_norm=kv_norm.reshape(1, D), final_norm=final_norm.reshape(1, D),
        lam_re=s5_lam_re[0], lam_im=s5_lam_im[0], log_dt=s5_log_dt.reshape(G, 1),
        bt_re=s5_b_re[0].transpose(2, 0, 1), bt_im=s5_b_im[0].transpose(2, 0, 1),
        c_re=s5_c_re[0], c_im=s5_c_im[0],
    )
    small = {}

    def small_ready(g):
        small_g = dict(
            s5_lam_re=g["lam_re"], s5_lam_im=g["lam_im"], s5_log_dt=g["log_dt"],
            s5_b_re=g["bt_re"].transpose(1, 2, 0), s5_b_im=g["bt_im"].transpose(1, 2, 0),
            s5_c_re=g["c_re"], s5_c_im=g["c_im"], ffn_norm=g["ffn_norm"], b_norm_mix=g["b_norm"],
            kv_norm=g["kv_norm"], final_norm=g["final_norm"])
        packed = _pack([small_g[n] for n in SMALL_NAMES])
        shard, land = _cast_and_place("place_small_grads", packed[None], 0, 0, pos, F32)
        sems, thru, lands, token = _gather_start("small_grads_gather_start", [shard], [land], [0], [[0]])
        small.update(sems=sems[0], thru=thru, lands=lands, rows=packed.shape[0])
        return token

    loss_blk, grad_x, g = _local_step(x[0], loss_target[0], w, wts, ready, dep0=start_token,
                                      small_ready=small_ready)
    loss = lax.psum(loss_blk[0, 0], ("x", "y", "c"))
    (tail,) = _all_gather("vector_grads_all_gather", [_pack([g["s5_d"], g["a_norm"]])], [0])

    out = {}

    def put(name, res, shape):
        for kind, r in zip(("grad", "delta", "new_m", "new_v"), res):
            out[f"{kind}_{name}"] = r.reshape(shape)

    updated = {}

    def update(names, p0s, recvd):
        for name, p0, rc in zip(names, p0s, recvd):
            oname = big_out_names[index_of[name]]
            w3, layer = big_shards[index_of[name]]
            updated[oname] = _adamw(f"adamw_{name}", [p0, rc], w3, args["m_" + oname].reshape(w3.shape),
                                    args["v_" + oname].reshape(w3.shape), layer=layer, prev=updated.get(oname))
        return updated[big_out_names[index_of[names[0]]]][0]

    early, last = exchanges[:-1], exchanges[-1]
    landed = _chip_exchange_finish("rs_chip_exchange_finish_early", [e[2] for e in early], grad_x)
    for (names, p0s, _), recvd in zip(early[:-1], landed[:-1]):
        marker = update(names, p0s, recvd)
    send, r_d2d, r_ici = small["sems"]
    f_send, f_recv, small_lands = _gather_forward("small_grads_gather_forward", small["lands"], [0], r_ici, marker)
    marker = update(early[-1][0], early[-1][1], landed[-1])
    (all_parts,) = _gather_finish("small_grads_gather_finish", small["thru"], small_lands, [0],
                                  send, r_d2d, f_send, f_recv, marker)
    (recvd,) = _chip_exchange_finish("rs_chip_exchange_finish_last", [last[2]], marker)
    update(last[0], last[1], recvd)
    for oname, res in updated.items():
        put(oname, res, args[oname].shape)
    g_small = _sum_parts("small_grads_sum", all_parts.reshape(N_DEV, small["rows"], 128))
    g_small = dict(zip(SMALL_NAMES, _unpack(g_small, [args[n].shape for n in SMALL_NAMES])))

    def view(a):
        return a.reshape(1, -1) if a.ndim == 1 else a.reshape(-1, a.shape[-1])

    for grp in SMALL_GROUPS:
        res = _adamw_multi(f"adamw_{grp[0]}", [view(g_small[n]) for n in grp], [view(args[n]) for n in grp],
                           [view(args["m_" + n]) for n in grp], [view(args["v_" + n]) for n in grp])
        for n, r in zip(grp, res):
            put(n, r, args[n].shape)
    ws = D // N_DEV
    tail = lax.dynamic_slice_in_dim(tail.reshape(N_DEV, 2, D), me * ws, ws, axis=2)
    res = _adamw("adamw_vec", [tail], vec_shard,
                 jnp.concatenate([m_s5_d, m_a_norm_mix], axis=0), jnp.concatenate([v_s5_d, v_a_norm_mix], axis=0))
    put("s5_d", [r[0:1] for r in res], s5_d.shape)
    put("a_norm_mix", [r[1:2] for r in res], a_norm_mix.shape)

    names = ("s5_lam_re", "s5_lam_im", "s5_log_dt", "s5_b_re", "s5_b_im", "s5_c_re", "s5_c_im", "s5_d",
             "s5_w_glu", "a_norm_mix", "ffn_norm", "ffn_w_in", "ffn_w_out", "b_norm_mix", "attn_w_q",
             "attn_w_o", "kv_norm", "w_kv", "final_norm")
    result = [loss, grad_x.reshape(x.shape)]
    for kind in ("grad", "delta", "new_m", "new_v"):
        result += [out[f"{kind}_{n}"] for n in names]
    return tuple(result)
```

```python
import functools
import math

import jax
import jax.numpy as jnp
from jax import lax
from jax.experimental import pallas as pl
from jax.experimental.pallas import tpu as pltpu

F32 = jnp.float32
BF16 = jnp.bfloat16

EPS = 1e-6
NEG_INF = -1e30
HEAD_DIM = 128
N_KV_HEADS = 4
DILATIONS = (1, 4, 16)
ATT_BLK = 128
S5_C = 16
S5_P = 64
S5_GB = 16
S5_CH = S5_GB * S5_C
S5_W = S5_GB * S5_P
S5_UNROLL = 4
N_DEV = 8

ADAM_LR = 0.001
ADAM_B1 = 0.9
ADAM_B2 = 0.999
ADAM_EPS = 1e-08
ADAM_WD = 0.01
ADAM_STEP = 10

VMEM_LIMIT_BYTES = 56 * 1024 * 1024
MM_TILE = 1024
MM_TILE_NARROW = 512
MM_DEPTH = 2816
MESH = pl.DeviceIdType.MESH
ANY = pl.BlockSpec(memory_space=pl.ANY)


def _tile(n, pref, align=128):
    t = (min(pref, n) // align) * align
    while t >= align:
        if n % t == 0:
            return t
        t -= align
    return n


def _params(sem):
    return pltpu.CompilerParams(dimension_semantics=sem, vmem_limit_bytes=VMEM_LIMIT_BYTES)


def _sigmoid(x):
    return 1.0 / (1.0 + jnp.exp(-x))


NN = (((1,), (0,)), ((), ()))
NT = (((1,), (1,)), ((), ()))
TN = (((0,), (0,)), ((), ()))


def _dot(a, b, dims=NN):
    return lax.dot_general(a, b, dims, preferred_element_type=F32)


def _matmul(name, grid, ins, in_specs, products, dims, out_shapes, out_specs, acc_shapes, epilogue):
    n_in, n_out, nk = len(ins), len(out_shapes), grid[2]

    def body(*refs):
        in_refs = refs[:n_in]
        out_refs = refs[n_in:n_in + n_out]
        acc_refs = refs[n_in + n_out:]

        def prods():
            vals = [None] * len(acc_shapes)
            for ai, bi, ci in products:
                d = _dot(in_refs[ai][...].astype(BF16), in_refs[bi][...].astype(BF16), dims)
                vals[ci] = d if vals[ci] is None else vals[ci] + d
            return vals

        if nk == 1:
            epilogue(in_refs, out_refs, prods())
        else:
            k = pl.program_id(2)

            @pl.when(k == 0)
            def _():
                for a in acc_refs:
                    a[...] = jnp.zeros_like(a)

            for a, v in zip(acc_refs, prods()):
                a[...] += v

            @pl.when(k == nk - 1)
            def _():
                epilogue(in_refs, out_refs, [a[...] for a in acc_refs])

    scratch = [] if nk == 1 else [pltpu.VMEM(s, F32) for s in acc_shapes]
    return pl.pallas_call(
        body, grid=grid, in_specs=in_specs, out_specs=out_specs, out_shape=out_shapes,
        scratch_shapes=scratch, compiler_params=_params(("parallel", "parallel", "arbitrary")),
        name=name)(*ins)


def _mm_dual_fwd(name, a, w, res, kind):
    T, K = a.shape
    N = w.shape[1] // 2
    tm, tn = _tile(T, MM_TILE), _tile(N, MM_TILE_NARROW)
    nj = N // tn
    grid = (T // tm, nj, 1)
    ins = [a, w, w]
    in_specs = [pl.BlockSpec((tm, K), lambda i, j, k: (i, 0)),
                pl.BlockSpec((K, tn), lambda i, j, k: (0, j)),
                pl.BlockSpec((K, tn), lambda i, j, k: (0, j + nj))]
    pair_spec = pl.BlockSpec((2, tm, tn), lambda i, j, k: (0, i, j))
    tile_spec = pl.BlockSpec((tm, tn), lambda i, j, k: (i, j))
    if kind == "glu":
        ins.append(res)
        in_specs.append(tile_spec)

        def epilogue(in_refs, out_refs, accs):
            val, gate = accs
            s = _sigmoid(gate)
            out_refs[0][...] = in_refs[3][...] + val * s
            out_refs[1][0] = s.astype(BF16)
            out_refs[1][1] = (val * s * (1.0 - s)).astype(BF16)

        out_shapes = [jax.ShapeDtypeStruct((T, N), F32), jax.ShapeDtypeStruct((2, T, N), BF16)]
        out_specs = [tile_spec, pair_spec]
    else:
        def epilogue(in_refs, out_refs, accs):
            g, u = accs
            s = _sigmoid(g)
            silu = g * s
            out_refs[0][0] = (u * (s * (1.0 + g * (1.0 - s)))).astype(BF16)
            out_refs[0][1] = silu.astype(BF16)
            out_refs[1][...] = (silu * u).astype(BF16)

        out_shapes = [jax.ShapeDtypeStruct((2, T, N), BF16), jax.ShapeDtypeStruct((T, N), BF16)]
        out_specs = [pair_spec, tile_spec]
    return _matmul(name, grid, ins, in_specs, [(0, 1, 0), (0, 2, 1)], NN, out_shapes, out_specs,
                   [(tm, tn), (tm, tn)], epilogue)


def _mm_kv(name, a, w):
    T, K = a.shape
    N = w.shape[1] // 2
    tm, tn = _tile(T, MM_TILE), _tile(N, MM_TILE_NARROW)
    nj = N // tn
    tile_spec = pl.BlockSpec((tm, tn), lambda i, j, k: (i, j))

    def epilogue(in_refs, out_refs, accs):
        out_refs[0][...] = accs[0]
        out_refs[1][...] = accs[1]

    return _matmul(name, (T // tm, nj, 1), [a, w, w],
                   [pl.BlockSpec((tm, K), lambda i, j, k: (i, 0)),
                    pl.BlockSpec((K, tn), lambda i, j, k: (0, j)),
                    pl.BlockSpec((K, tn), lambda i, j, k: (0, j + nj))],
                   [(0, 1, 0), (0, 2, 1)], NN,
                   [jax.ShapeDtypeStruct((T, N), F32)] * 2, [tile_spec, tile_spec],
                   [(tm, tn), (tm, tn)], epilogue)


def _mm_nn(name, a, w, res=None, out_dtype=F32):
    T, K = a.shape
    N = w.shape[1]
    tk = K if K <= 2 * MM_DEPTH else _tile(K, MM_DEPTH)
    tm, tn = _tile(T, MM_TILE), _tile(N, MM_TILE if K <= MM_DEPTH else MM_TILE_NARROW)
    grid = (T // tm, N // tn, K // tk)
    tile_spec = pl.BlockSpec((tm, tn), lambda i, j, k: (i, j))
    ins = [a, w]
    in_specs = [pl.BlockSpec((tm, tk), lambda i, j, k: (i, k)),
                pl.BlockSpec((tk, tn), lambda i, j, k: (k, j))]
    if res is not None:
        ins.append(res)
        in_specs.append(tile_spec)

    def epilogue(in_refs, out_refs, accs):
        v = accs[0]
        if res is not None:
            v = v + in_refs[2][...]
        out_refs[0][...] = v.astype(out_dtype)

    return _matmul(name, grid, ins, in_specs, [(0, 1, 0)], NN,
                   [jax.ShapeDtypeStruct((T, N), out_dtype)], [tile_spec], [(tm, tn)], epilogue)[0]


def _dep_operand(ins, in_specs, dep):
    if dep is not None:
        ins.append(dep)
        in_specs.append(pl.BlockSpec((8, 128), lambda *_: (0, 0)))


def _mm_nt(name, a_list, w, out_dtype=F32, dep=None):
    T, Np = a_list[0].shape
    Ko = w.shape[0]
    n_parts = len(a_list)
    wide_a = a_list[0].dtype != BF16
    tm, tn, tk = _tile(T, MM_TILE_NARROW if wide_a else MM_TILE), _tile(Ko, MM_TILE), _tile(Np, MM_DEPTH)
    nkp = Np // tk
    grid = (T // tm, Ko // tn, nkp)
    ins = list(a_list) + [w] * n_parts
    in_specs = [pl.BlockSpec((tm, tk), lambda i, j, k: (i, k)) for _ in a_list]
    in_specs += [pl.BlockSpec((tn, tk), functools.partial(lambda i, j, k, p: (j, p * nkp + k), p=p))
                 for p in range(n_parts)]
    products = [(p, n_parts + p, 0) for p in range(n_parts)]
    _dep_operand(ins, in_specs, dep)

    def epilogue(in_refs, out_refs, accs):
        out_refs[0][...] = accs[0].astype(out_dtype)

    return _matmul(name, grid, ins, in_specs, products, NT,
                   [jax.ShapeDtypeStruct((T, Ko), out_dtype)],
                   [pl.BlockSpec((tm, tn), lambda i, j, k: (i, j))], [(tm, tn)], epilogue)[0]


def _mm_nt_pair(name, a3, w, out_dtype=F32):
    _, T, N = a3.shape
    Ko = w.shape[0]
    tm, tn, tk = _tile(T, MM_TILE), _tile(Ko, MM_TILE), _tile(N, MM_DEPTH)
    nkh = N // tk
    grid = (T // tm, Ko // tn, 2 * nkh)

    def epilogue(in_refs, out_refs, accs):
        out_refs[0][...] = accs[0].astype(out_dtype)

    return _matmul(name, grid, [a3, w],
                   [pl.BlockSpec((None, tm, tk), lambda i, j, k: (k // nkh, i, k % nkh)),
                    pl.BlockSpec((tn, tk), lambda i, j, k: (j, k))],
                   [(0, 1, 0)], NT, [jax.ShapeDtypeStruct((T, Ko), out_dtype)],
                   [pl.BlockSpec((tm, tn), lambda i, j, k: (i, j))], [(tm, tn)], epilogue)[0]


def _mm_nt_ffn_bwd(name, dx, w_out, gu, dep=None):
    T, D = dx.shape
    Fh = w_out.shape[0]
    tm, tn = _tile(T, MM_TILE), _tile(Fh, MM_TILE_NARROW)
    n_dep = 0 if dep is None else 1

    def body(dx_ref, w_ref, gu_ref, *rest):
        out_ref = rest[n_dep]
        da = _dot(dx_ref[...], w_ref[...], NT)
        out_ref[0] = (da * gu_ref[0].astype(F32)).astype(BF16)
        out_ref[1] = (da * gu_ref[1].astype(F32)).astype(BF16)

    pair_spec = pl.BlockSpec((2, tm, tn), lambda i, j: (0, i, j))
    ins = [dx, w_out, gu]
    in_specs = [pl.BlockSpec((tm, D), lambda i, j: (i, 0)),
                pl.BlockSpec((tn, D), lambda i, j: (j, 0)),
                pair_spec]
    _dep_operand(ins, in_specs, dep)
    return pl.pallas_call(
        body, grid=(T // tm, Fh // tn), in_specs=in_specs, out_specs=pair_spec,
        out_shape=jax.ShapeDtypeStruct((2, T, Fh), BF16),
        compiler_params=_params(("parallel", "parallel")), name=name)(*ins)


def _mm_tn(name, a, d):
    T, Ko = a.shape
    N = d.shape[1]
    to, tn, tk = _tile(Ko, MM_TILE_NARROW), _tile(N, MM_TILE if d.dtype == BF16 else MM_TILE_NARROW), T
    grid = (Ko // to, N // tn, T // tk)

    def epilogue(in_refs, out_refs, accs):
        out_refs[0][...] = accs[0].astype(BF16)

    return _matmul(name, grid, [a, d],
                   [pl.BlockSpec((tk, to), lambda i, j, k: (k, i)),
                    pl.BlockSpec((tk, tn), lambda i, j, k: (k, j))],
                   [(0, 1, 0)], TN, [jax.ShapeDtypeStruct((Ko, N), BF16)],
                   [pl.BlockSpec((to, tn), lambda i, j, k: (i, j))], [(to, tn)], epilogue)[0]


def _mm_tn_pair(name, a, d3):
    T, Ko = a.shape
    N = d3.shape[2]
    to, tn, tk = _tile(Ko, MM_TILE), _tile(N, MM_TILE_NARROW), T
    njh = N // tn
    grid = (Ko // to, 2 * njh, T // tk)

    def epilogue(in_refs, out_refs, accs):
        out_refs[0][...] = accs[0].astype(BF16)

    return _matmul(name, grid, [a, d3],
                   [pl.BlockSpec((tk, to), lambda i, j, k: (k, i)),
                    pl.BlockSpec((None, tk, tn), lambda i, j, k: (j // njh, k, j % njh))],
                   [(0, 1, 0)], TN, [jax.ShapeDtypeStruct((Ko, 2 * N), BF16)],
                   [pl.BlockSpec((to, tn), lambda i, j, k: (i, j))], [(to, tn)], epilogue)[0]


def _rms_fwd(name, x, gains, dtypes, dep=None):
    T, D = x.shape
    n = len(gains)
    tr = _tile(T, 512, 8)
    n_dep = 0 if dep is None else 1

    def body(x_ref, *refs):
        xv = x_ref[...]
        xr = xv * lax.rsqrt(jnp.mean(xv * xv, axis=-1, keepdims=True) + EPS)
        for g_ref, o_ref in zip(refs[:n], refs[n + n_dep:]):
            o_ref[...] = (xr * g_ref[...]).astype(o_ref.dtype)

    row = pl.BlockSpec((tr, D), lambda i: (i, 0))
    vec = pl.BlockSpec((1, D), lambda i: (0, 0))
    ins, in_specs = [x, *gains], [row] + [vec] * n
    _dep_operand(ins, in_specs, dep)
    return pl.pallas_call(
        body, grid=(T // tr,), in_specs=in_specs, out_specs=[row] * n,
        out_shape=[jax.ShapeDtypeStruct((T, D), dt) for dt in dtypes],
        compiler_params=_params(("parallel",)), name=name)(*ins)


def _rms_bwd(name, x, dres, gains, dhs, dep=None):
    T, D = x.shape
    n = len(gains)
    tr = _tile(T, 256, 8)
    n_dep = 0 if dep is None else 1

    def body(x_ref, dres_ref, *refs):
        g_refs, dh_refs = refs[:n], refs[n:2 * n]
        dx_ref, dxb_ref = refs[2 * n + n_dep], refs[2 * n + n_dep + 1]
        dg_refs = refs[2 * n + n_dep + 2:]
        xv = x_ref[...]
        r = lax.rsqrt(jnp.mean(xv * xv, axis=-1, keepdims=True) + EPS)
        xr = xv * r
        w = None
        for g_ref, dh_ref, dg_ref in zip(g_refs, dh_refs, dg_refs):
            dh = dh_ref[...].astype(F32)

            @pl.when(pl.program_id(0) == 0)
            def _():
                dg_ref[...] = jnp.zeros_like(dg_ref)

            dg_ref[...] += jnp.sum(dh * xr, axis=0, keepdims=True)
            wi = dh * g_ref[...]
            w = wi if w is None else w + wi
        dx = dres_ref[...] + r * (w - xr * jnp.mean(w * xr, axis=-1, keepdims=True))
        dx_ref[...] = dx
        dxb_ref[...] = dx.astype(BF16)

    row = pl.BlockSpec((tr, D), lambda i: (i, 0))
    vec = pl.BlockSpec((1, D), lambda i: (0, 0))
    ins, in_specs = [x, dres, *gains, *dhs], [row, row] + [vec] * n + [row] * n
    _dep_operand(ins, in_specs, dep)
    outs = pl.pallas_call(
        body, grid=(T // tr,), in_specs=in_specs,
        out_specs=[row, row] + [vec] * n,
        out_shape=[jax.ShapeDtypeStruct((T, D), F32), jax.ShapeDtypeStruct((T, D), BF16)]
        + [jax.ShapeDtypeStruct((1, D), F32)] * n,
        compiler_params=_params(("arbitrary",)), name=name)(*ins)
    return outs[0], outs[1], outs[2:]


def _loss_head(x, gain, target):
    T, D = x.shape
    tr = _tile(T, 256, 8)

    def body(x_ref, g_ref, t_ref, loss_ref, dx_ref, dxb_ref, dg_ref):
        @pl.when(pl.program_id(0) == 0)
        def _():
            loss_ref[...] = jnp.zeros_like(loss_ref)
            dg_ref[...] = jnp.zeros_like(dg_ref)

        xv = x_ref[...]
        r = lax.rsqrt(jnp.mean(xv * xv, axis=-1, keepdims=True) + EPS)
        xr = xv * r
        err = xr * g_ref[...] - t_ref[...]
        part = jnp.sum(jnp.sum(err * err, axis=-1, keepdims=True), axis=0, keepdims=True) * (0.5 / D)
        loss_ref[...] += jnp.broadcast_to(part, loss_ref.shape)
        dy = err * (1.0 / D)
        dg_ref[...] += jnp.sum(dy * xr, axis=0, keepdims=True)
        w = dy * g_ref[...]
        dx = r * (w - xr * jnp.mean(w * xr, axis=-1, keepdims=True))
        dx_ref[...] = dx
        dxb_ref[...] = dx.astype(BF16)

    row = pl.BlockSpec((tr, D), lambda i: (i, 0))
    vec = pl.BlockSpec((1, D), lambda i: (0, 0))
    return pl.pallas_call(
        body, grid=(T // tr,), in_specs=[row, vec, row],
        out_specs=[pl.BlockSpec((8, 128), lambda i: (0, 0)), row, row, vec],
        out_shape=[jax.ShapeDtypeStruct((8, 128), F32), jax.ShapeDtypeStruct((T, D), F32),
                   jax.ShapeDtypeStruct((T, D), BF16), jax.ShapeDtypeStruct((1, D), F32)],
        compiler_params=_params(("arbitrary",)), name="loss_head")(x, gain, target)


def _glu_bwd(dmix, vg, dep=None):
    T, N = dmix.shape
    tr, tc = _tile(T, 512, 8), _tile(N, 1024)
    n_dep = 0 if dep is None else 1

    def body(d_ref, vg_ref, *refs):
        o_ref = refs[n_dep]
        d = d_ref[...]
        o_ref[0] = (d * vg_ref[0].astype(F32)).astype(BF16)
        o_ref[1] = (d * vg_ref[1].astype(F32)).astype(BF16)

    pair = pl.BlockSpec((2, tr, tc), lambda i, j: (0, i, j))
    ins, in_specs = [dmix, vg], [pl.BlockSpec((tr, tc), lambda i, j: (i, j)), pair]
    _dep_operand(ins, in_specs, dep)
    return pl.pallas_call(
        body, grid=(T // tr, N // tc), in_specs=in_specs,
        out_specs=pair, out_shape=jax.ShapeDtypeStruct((2, T, N), BF16),
        compiler_params=_params(("parallel", "parallel")), name="glu_bwd")(*ins)


def _to_state_tiles(x_ref, s, val):
    tc = val.shape[0]
    for j in range(S5_W // 128):
        x_ref[s, pl.ds(j, tc, stride=8), :] = val[:, 128 * j:128 * (j + 1)]


def _from_state_tiles(x_ref, s, tc):
    return jnp.concatenate([x_ref[s, pl.ds(j, tc, stride=8), :] for j in range(S5_W // 128)], axis=1)


def _s5_scan_fwd(xr_ref, xi_ref, ar_ref, ai_ref, cr_ref, ci_ref, tc, nblk):
    a = [(ar_ref[s], ai_ref[s]) for s in range(nblk)]

    def step(i, carry):
        carry = list(carry)
        for uu in range(S5_UNROLL):
            r0 = pl.multiple_of((i * S5_UNROLL + uu) * 8, 8)
            for s in range(nblk):
                cr, ci = carry[2 * s], carry[2 * s + 1]
                a_r, a_i = a[s]
                xr = a_r * cr - a_i * ci + xr_ref[s, pl.ds(r0, 8), :]
                xi = a_r * ci + a_i * cr + xi_ref[s, pl.ds(r0, 8), :]
                xr_ref[s, pl.ds(r0, 8), :] = xr
                xi_ref[s, pl.ds(r0, 8), :] = xi
                carry[2 * s], carry[2 * s + 1] = xr, xi
        return tuple(carry)

    init = []
    for s in range(nblk):
        init += [cr_ref[s], ci_ref[s]]
    out = lax.fori_loop(0, tc // S5_UNROLL, step, tuple(init))
    for s in range(nblk):
        cr_ref[s] = out[2 * s]
        ci_ref[s] = out[2 * s + 1]


def _s5_scan_bwd(lr_ref, li_ref, xr_ref, xi_ref, h_ref, ar_ref, ai_ref, cr_ref, ci_ref,
                 accr_ref, acci_ref, tc, nblk):
    a = [(ar_ref[s], ai_ref[s]) for s in range(nblk)]

    def one(s, r0, prev_r, prev_i, st):
        c_r, c_i, d_r, d_i = st
        a_r, a_i = a[s]
        l_r = lr_ref[s, pl.ds(r0, 8), :] + a_r * c_r + a_i * c_i
        l_i = li_ref[s, pl.ds(r0, 8), :] + a_r * c_i - a_i * c_r
        lr_ref[s, pl.ds(r0, 8), :] = l_r
        li_ref[s, pl.ds(r0, 8), :] = l_i
        return [l_r, l_i, d_r + l_r * prev_r + l_i * prev_i, d_i - l_r * prev_i + l_i * prev_r]

    def step(i, carry):
        carry = list(carry)
        for uu in range(S5_UNROLL):
            t = tc - 1 - (i * S5_UNROLL + uu)
            r0 = pl.multiple_of(t * 8, 8)
            p0 = pl.multiple_of((t - 1) * 8, 8)
            for s in range(nblk):
                carry[4 * s:4 * s + 4] = one(s, r0, xr_ref[s, pl.ds(p0, 8), :], xi_ref[s, pl.ds(p0, 8), :],
                                             carry[4 * s:4 * s + 4])
        return tuple(carry)

    init = []
    for s in range(nblk):
        init += [cr_ref[s], ci_ref[s], accr_ref[s], acci_ref[s]]
    carry = list(lax.fori_loop(0, tc // S5_UNROLL - 1, step, tuple(init)))
    for t in range(S5_UNROLL - 1, -1, -1):
        for s in range(nblk):
            if t > 0:
                prev_r, prev_i = xr_ref[s, 8 * (t - 1):8 * t, :], xi_ref[s, 8 * (t - 1):8 * t, :]
            else:
                prev_r, prev_i = h_ref[0, s], h_ref[1, s]
            carry[4 * s:4 * s + 4] = one(s, 8 * t, prev_r, prev_i, carry[4 * s:4 * s + 4])
    for s in range(nblk):
        cr_ref[s], ci_ref[s], accr_ref[s], acci_ref[s] = carry[4 * s:4 * s + 4]


def _gelu_and_grad(y):
    k = math.sqrt(2.0 / math.pi)
    t = jnp.tanh(k * (y + 0.044715 * (y * y * y)))
    return (0.5 * y * (1.0 + t),
            0.5 * (1.0 + t) + 0.5 * y * (1.0 - t * t) * (k * (1.0 + 3.0 * 0.044715 * (y * y))))


def _s5_specs(tc, nch, sbk, rev):
    def ch(c):
        return nch - 1 - c if rev else c

    return dict(
        act=pl.BlockSpec((tc, sbk * S5_CH), lambda i, c: (ch(c), i)),
        bb=pl.BlockSpec((2, sbk, S5_CH, 128), lambda i, c: (0, i, 0, 0)),
        cc=pl.BlockSpec((2, sbk, S5_P, S5_CH), lambda i, c: (0, i, 0, 0)),
        a=pl.BlockSpec((sbk, 8, 128), lambda i, c: (i, 0, 0)),
        d=pl.BlockSpec((1, sbk * S5_CH), lambda i, c: (0, i)),
        h=pl.BlockSpec((None, 2, sbk, 8, 128), lambda i, c: (ch(c), 0, i, 0, 0)),
    )


def _s5_blocks(nb, pref):
    return max(b for b in range(1, pref + 1) if nb % b == 0)


def _s5_group_masks():
    rb = lax.broadcasted_iota(jnp.int32, (S5_CH, S5_W), 0) // S5_C
    qb = lax.broadcasted_iota(jnp.int32, (S5_CH, S5_W), 1) // S5_P
    qc = lax.broadcasted_iota(jnp.int32, (S5_W, S5_CH), 0) // S5_P
    rc = lax.broadcasted_iota(jnp.int32, (S5_W, S5_CH), 1) // S5_C
    return rb == qb, qc == rc


def _s5_expand(bb_ref, cc_ref, bbd, ccd, sbk):
    mask_b, mask_c = _s5_group_masks()
    for k in range(2):
        for s in range(sbk):
            bbd[k, s] = jnp.where(mask_b, jnp.tile(bb_ref[k, s], (1, S5_W // 128)), 0).astype(BF16)
            ccd[k, s] = jnp.where(mask_c, jnp.tile(cc_ref[k, s], (S5_GB, 1)), 0).astype(BF16)


def _s5_fwd(u, bb2, cc2, a_re, a_im, d_skip):
    T, D = u.shape
    nb = D // S5_CH
    sbk = _s5_blocks(nb, 4)
    tc = _tile(T, 512, 8)
    nch = T // tc
    sp = _s5_specs(tc, nch, sbk, False)

    def body(u_ref, bb_ref, cc_ref, ar_ref, ai_ref, d_ref, z_ref, gp_ref, h_ref, xr, xi, cr, ci, bbd, ccd):
        @pl.when(pl.program_id(1) == 0)
        def _():
            cr[...] = jnp.zeros_like(cr)
            ci[...] = jnp.zeros_like(ci)
            _s5_expand(bb_ref, cc_ref, bbd, ccd, sbk)

        h_ref[0] = cr[...]
        h_ref[1] = ci[...]
        for s in range(sbk):
            ub = u_ref[:, s * S5_CH:(s + 1) * S5_CH].astype(BF16)
            _to_state_tiles(xr, s, _dot(ub, bbd[0, s]))
            _to_state_tiles(xi, s, _dot(ub, bbd[1, s]))
        _s5_scan_fwd(xr, xi, ar_ref, ai_ref, cr, ci, tc, sbk)
        for s in range(sbk):
            cols = slice(s * S5_CH, (s + 1) * S5_CH)
            y = (_dot(_from_state_tiles(xr, s, tc).astype(BF16), ccd[0, s])
                 - _dot(_from_state_tiles(xi, s, tc).astype(BF16), ccd[1, s])
                 + d_ref[:, cols] * u_ref[:, cols])
            act, act_grad = _gelu_and_grad(y)
            z_ref[:, cols] = act.astype(BF16)
            gp_ref[:, cols] = act_grad.astype(BF16)

    tiles = pltpu.VMEM((sbk, tc * 8, 128), F32)
    carry = pltpu.VMEM((sbk, 8, 128), F32)
    return pl.pallas_call(
        body, grid=(nb // sbk, nch),
        in_specs=[sp["act"], sp["bb"], sp["cc"], sp["a"], sp["a"], sp["d"]],
        out_specs=[sp["act"], sp["act"], sp["h"]],
        out_shape=[jax.ShapeDtypeStruct((T, D), BF16), jax.ShapeDtypeStruct((T, D), BF16),
                   jax.ShapeDtypeStruct((nch, 2, nb, 8, 128), F32)],
        scratch_shapes=[tiles, tiles, carry, carry, pltpu.VMEM((2, sbk, S5_CH, S5_W), BF16),
                        pltpu.VMEM((2, sbk, S5_W, S5_CH), BF16)],
        compiler_params=_params(("parallel", "arbitrary")), name="s5_fwd",
    )(u, bb2, cc2, a_re, a_im, d_skip)


def _s5_bwd(u, dz, gp, h0, bb2, cc2, a_re, a_im, d_skip, dep=None):
    T, D = u.shape
    nb = D // S5_CH
    sbk = _s5_blocks(nb, 2)
    tc = _tile(T, 512, 8)
    nch = T // tc
    sp = _s5_specs(tc, nch, sbk, True)

    n_dep = 0 if dep is None else 1

    def body(u_ref, dz_ref, gp_ref, h_ref, bb_ref, cc_ref, ar_ref, ai_ref, d_ref, *rest):
        (du_ref, dd_ref, dar_ref, dai_ref, dbb_ref, dcc_ref,
         xr, xi, lr, li, fr, fi, br, bi, accr, acci, bbd, ccd, dbbd, dccd) = rest[n_dep:]
        c = pl.program_id(1)

        @pl.when(c == 0)
        def _():
            for ref in (br, bi, accr, acci, dd_ref, dbbd, dccd):
                ref[...] = jnp.zeros_like(ref)
            _s5_expand(bb_ref, cc_ref, bbd, ccd, sbk)

        for s in range(sbk):
            ub = u_ref[:, s * S5_CH:(s + 1) * S5_CH].astype(BF16)
            _to_state_tiles(xr, s, _dot(ub, bbd[0, s]))
            _to_state_tiles(xi, s, _dot(ub, bbd[1, s]))
        fr[...] = h_ref[0]
        fi[...] = h_ref[1]
        _s5_scan_fwd(xr, xi, ar_ref, ai_ref, fr, fi, tc, sbk)
        for s in range(sbk):
            cols = slice(s * S5_CH, (s + 1) * S5_CH)
            uv = u_ref[:, cols]
            xrb = _from_state_tiles(xr, s, tc).astype(BF16)
            xib = _from_state_tiles(xi, s, tc).astype(BF16)
            dsk = d_ref[:, cols]
            dy = dz_ref[:, cols].astype(F32) * gp_ref[:, cols].astype(F32)
            dd_ref[:, cols] += jnp.sum(dy * uv, axis=0, keepdims=True)
            dyb = dy.astype(BF16)
            dccd[0, s] += _dot(xrb, dyb, TN)
            dccd[1, s] += _dot(xib, dyb, TN)
            _to_state_tiles(lr, s, _dot(dyb, ccd[0, s], NT))
            _to_state_tiles(li, s, -_dot(dyb, ccd[1, s], NT))
            du_ref[:, cols] = dy * dsk
        _s5_scan_bwd(lr, li, xr, xi, h_ref, ar_ref, ai_ref, br, bi, accr, acci, tc, sbk)
        for s in range(sbk):
            cols = slice(s * S5_CH, (s + 1) * S5_CH)
            ub = u_ref[:, cols].astype(BF16)
            lrb = _from_state_tiles(lr, s, tc).astype(BF16)
            lib = _from_state_tiles(li, s, tc).astype(BF16)
            dbbd[0, s] += _dot(ub, lrb, TN)
            dbbd[1, s] += _dot(ub, lib, TN)
            du_ref[:, cols] += _dot(lrb, bbd[0, s], NT) + _dot(lib, bbd[1, s], NT)

        @pl.when(c == nch - 1)
        def _():
            dar_ref[...] = accr[...]
            dai_ref[...] = acci[...]
            mask_b, mask_c = _s5_group_masks()
            for k in range(2):
                for s in range(sbk):
                    mb = jnp.where(mask_b, dbbd[k, s], 0.0)
                    fold = functools.reduce(
                        lambda a, b: a + b, [mb[:, 128 * j:128 * (j + 1)] for j in range(S5_W // 128)])
                    dbb_ref[k, s] = fold + pltpu.roll(fold, S5_P, 1)
                    mc = jnp.where(mask_c, dccd[k, s], 0.0)
                    dcc_ref[k, s] = functools.reduce(
                        lambda a, b: a + b, [mc[S5_P * j:S5_P * (j + 1), :] for j in range(S5_GB)])

    tiles = pltpu.VMEM((sbk, tc * 8, 128), F32)
    carry = pltpu.VMEM((sbk, 8, 128), F32)
    ins = [u, dz, gp, h0, bb2, cc2, a_re, a_im, d_skip]
    in_specs = [sp["act"], sp["act"], sp["act"], sp["h"], sp["bb"], sp["cc"], sp["a"], sp["a"], sp["d"]]
    _dep_operand(ins, in_specs, dep)
    return pl.pallas_call(
        body, grid=(nb // sbk, nch),
        in_specs=in_specs,
        out_specs=[sp["act"], sp["d"], sp["a"], sp["a"], sp["bb"], sp["cc"]],
        out_shape=[jax.ShapeDtypeStruct((T, D), F32), jax.ShapeDtypeStruct((1, D), F32),
                   jax.ShapeDtypeStruct((nb, 8, 128), F32), jax.ShapeDtypeStruct((nb, 8, 128), F32),
                   jax.ShapeDtypeStruct((2, nb, S5_CH, 128), F32), jax.ShapeDtypeStruct((2, nb, S5_P, S5_CH), F32)],
        scratch_shapes=[tiles, tiles, tiles, tiles, carry, carry, carry, carry, carry, carry,
                        pltpu.VMEM((2, sbk, S5_CH, S5_W), BF16), pltpu.VMEM((2, sbk, S5_W, S5_CH), BF16),
                        pltpu.VMEM((2, sbk, S5_CH, S5_W), F32), pltpu.VMEM((2, sbk, S5_W, S5_CH), F32)],
        compiler_params=_params(("parallel", "arbitrary")), name="s5_bwd",
    )(*ins)


def _s5_disc(lr, li, ldt):
    dt = jnp.exp(ldt)
    mag = jnp.exp(lr * dt)
    ang = li * dt
    cs, sn = jnp.cos(ang), jnp.sin(ang)
    lbr, lbi = mag * cs, mag * sn
    nr = lbr - 1.0
    den = lr * lr + li * li
    f_re = (nr * lr + lbi * li) / den
    f_im = (lbi * lr - nr * li) / den
    return dt, mag, cs, sn, lbr, lbi, nr, den, f_re, f_im


def _s5_param_fwd(lr, li, ldt, bt_re, bt_im):
    c, g, p = bt_re.shape

    def body(lr_ref, li_ref, ldt_ref, br_ref, bi_ref, lbr_ref, lbi_ref, bbr_ref, bbi_ref):
        _, _, _, _, lbr, lbi, _, _, f_re, f_im = _s5_disc(lr_ref[...], li_ref[...], ldt_ref[...])
        lbr_ref[...] = lbr
        lbi_ref[...] = lbi
        for ch in range(c):
            b_r, b_i = br_ref[ch], bi_ref[ch]
            bbr_ref[ch] = f_re * b_r - f_im * b_i
            bbi_ref[ch] = f_re * b_i + f_im * b_r

    gp = jax.ShapeDtypeStruct((g, p), F32)
    cgp = jax.ShapeDtypeStruct((c, g, p), F32)
    return pl.pallas_call(body, out_shape=[gp, gp, cgp, cgp], name="s5_param_fwd")(lr, li, ldt, bt_re, bt_im)


def _s5_param_bwd(lr, li, ldt, bt_re, bt_im, dlbr, dlbi, dbbr, dbbi):
    c, g, p = bt_re.shape

    def body(lr_ref, li_ref, ldt_ref, br_ref, bi_ref, dlbr_ref, dlbi_ref, dbbr_ref, dbbi_ref,
             dlr_ref, dli_ref, dldt_ref, dbr_ref, dbi_ref):
        l_r, l_i = lr_ref[...], li_ref[...]
        dt, mag, cs, sn, lbr, lbi, nr, den, f_re, f_im = _s5_disc(l_r, l_i, ldt_ref[...])
        dfr = jnp.zeros_like(l_r)
        dfi = jnp.zeros_like(l_r)
        for ch in range(c):
            b_r, b_i = br_ref[ch], bi_ref[ch]
            g_r, g_i = dbbr_ref[ch], dbbi_ref[ch]
            dbr_ref[ch] = f_re * g_r + f_im * g_i
            dbi_ref[ch] = f_re * g_i - f_im * g_r
            dfr = dfr + g_r * b_r + g_i * b_i
            dfi = dfi + g_i * b_r - g_r * b_i
        inv = 1.0 / den
        d_nr = (dfr * l_r - dfi * l_i) * inv
        d_lbi = (dfr * l_i + dfi * l_r) * inv + dlbi_ref[...]
        d_lbr = d_nr + dlbr_ref[...]
        d_den = -(dfr * f_re + dfi * f_im) * inv
        d_mag = d_lbr * cs + d_lbi * sn
        d_ang = d_lbi * lbr - d_lbr * lbi
        dlr_ref[...] = (dfr * nr + dfi * lbi) * inv + 2.0 * d_den * l_r + d_mag * mag * dt
        dli_ref[...] = (dfr * lbi - dfi * nr) * inv + 2.0 * d_den * l_i + d_ang * dt
        dldt_ref[...] = jnp.sum(d_mag * mag * l_r + d_ang * l_i, axis=1, keepdims=True) * dt

    gp = jax.ShapeDtypeStruct((g, p), F32)
    cgp = jax.ShapeDtypeStruct((c, g, p), F32)
    return pl.pallas_call(body, out_shape=[gp, gp, jax.ShapeDtypeStruct((g, 1), F32), cgp, cgp],
                          name="s5_param_bwd")(lr, li, ldt, bt_re, bt_im, dlbr, dlbi, dbbr, dbbi)


def _att_masks(rep, gb):
    rows = rep * ATT_BLK
    qi = lax.broadcasted_iota(jnp.int32, (rows, 2 * ATT_BLK), 0) % ATT_BLK
    si = lax.broadcasted_iota(jnp.int32, (rows, 2 * ATT_BLK), 1)
    prev = (si < ATT_BLK) & (si >= qi) & (gb > 0)
    cur = (si >= ATT_BLK) & (si - ATT_BLK <= qi)
    return prev | cur


def _att_rows(start, dil):
    return pl.ds(start, ATT_BLK) if dil == 1 else pl.ds(start, ATT_BLK, stride=dil)


def _att_plan(T, dil):
    span = ATT_BLK * dil
    sbr = max(span, min(T, 1024))
    return span, sbr, T // sbr


def _att_block(sb, i, sbr, span, dil):
    loc = (i // dil) * span + i % dil
    cur = sb * sbr + loc
    gb = sb * (sbr // span) + i // dil
    return loc, cur, jnp.where(gb > 0, cur - span, cur), gb


def _att_fwd(q, k, v, grp, dil):
    T = q.shape[0]
    H = q.shape[1] // HEAD_DIM // len(DILATIONS)
    rep = H // N_KV_HEADS
    span, sbr, nsb = _att_plan(T, dil)
    scale = HEAD_DIM ** -0.5

    def body(*refs):
        q_refs = refs[:rep]
        k_ref, v_ref, o_ref, l_ref, o_slab, l_slab = refs[rep:]
        sb = pl.program_id(1)

        def blk(i, _):
            loc, cur, prv, gb = _att_block(sb, i, sbr, span, dil)
            rows = _att_rows(loc, dil)
            qs = jnp.concatenate([r[rows, :] for r in q_refs], axis=0).astype(BF16)
            kcat = jnp.concatenate([k_ref[_att_rows(prv, dil), :], k_ref[_att_rows(cur, dil), :]], axis=0)
            vcat = jnp.concatenate([v_ref[_att_rows(prv, dil), :], v_ref[_att_rows(cur, dil), :]], axis=0)
            s = jnp.where(_att_masks(rep, gb), _dot(qs, kcat.astype(BF16), NT) * scale, NEG_INF)
            m = jnp.max(s, axis=-1, keepdims=True)
            p = jnp.exp(s - m)
            l = jnp.sum(p, axis=-1, keepdims=True)
            o = _dot(p.astype(BF16), vcat.astype(BF16)) / l
            lse = jnp.broadcast_to(m + jnp.log(l), (rep * ATT_BLK, HEAD_DIM))
            for j in range(rep):
                o_slab[j, rows, :] = o[j * ATT_BLK:(j + 1) * ATT_BLK]
                l_slab[j, rows, :] = lse[j * ATT_BLK:(j + 1) * ATT_BLK]
            return 0

        lax.fori_loop(0, sbr // ATT_BLK, blk, 0)
        for j in range(rep):
            o_ref[:, j * HEAD_DIM:(j + 1) * HEAD_DIM] = o_slab[j].astype(BF16)
            l_ref[:, j * HEAD_DIM:(j + 1) * HEAD_DIM] = l_slab[j]

    qspecs = [pl.BlockSpec((sbr, HEAD_DIM), functools.partial(lambda h, s, j: (s, grp * H + h * rep + j), j=j))
              for j in range(rep)]
    kspec = pl.BlockSpec((T, HEAD_DIM), lambda h, s: (0, h))
    ospec = pl.BlockSpec((sbr, rep * HEAD_DIM), lambda h, s: (s, h))
    slab = pltpu.VMEM((rep, sbr, HEAD_DIM), F32)
    return pl.pallas_call(
        body, grid=(N_KV_HEADS, nsb), in_specs=qspecs + [kspec, kspec], out_specs=[ospec, ospec],
        out_shape=[jax.ShapeDtypeStruct((T, H * HEAD_DIM), BF16), jax.ShapeDtypeStruct((T, H * HEAD_DIM), F32)],
        scratch_shapes=[slab, slab],
        compiler_params=_params(("parallel", "arbitrary")), name=f"att_fwd_d{dil}",
    )(*([q] * rep), k, v)


def _att_combine(outs, lses):
    T, W = outs[0].shape
    ng = len(outs)
    tr, tcol = _tile(T, 512, 8), _tile(W, 512)

    def body(*refs):
        o_refs, l_refs = refs[:ng], refs[ng:2 * ng]
        ob_ref, lse_ref = refs[2 * ng:]
        ls = [r[...] for r in l_refs]
        m = functools.reduce(jnp.maximum, ls)
        es = [jnp.exp(l - m) for l in ls]
        den = functools.reduce(lambda a, b: a + b, es)
        num = functools.reduce(lambda a, b: a + b, [e * o[...].astype(F32) for e, o in zip(es, o_refs)])
        ob_ref[...] = (num / den).astype(BF16)
        lse_ref[...] = m + jnp.log(den)

    spec = pl.BlockSpec((tr, tcol), lambda i, j: (i, j))
    return pl.pallas_call(
        body, grid=(T // tr, W // tcol), in_specs=[spec] * (2 * ng), out_specs=[spec, spec],
        out_shape=[jax.ShapeDtypeStruct((T, W), BF16), jax.ShapeDtypeStruct((T, W), F32)],
        compiler_params=_params(("parallel", "parallel")), name="att_combine")(*outs, *lses)


STAT_LANE = HEAD_DIM // 2


def _att_out_bwd(dx, w_o, o, lse, dep=None):
    T, D = dx.shape
    W = w_o.shape[0]
    tm, tn = _tile(T, MM_TILE), _tile(W, MM_TILE)
    n_dep = 0 if dep is None else 1

    def body(dx_ref, w_ref, o_ref, l_ref, *rest):
        do_ref, s_ref = rest[n_dep:]
        do_ref[...] = _dot(dx_ref[...], w_ref[...], NT)
        lane = lax.broadcasted_iota(jnp.int32, (tm, HEAD_DIM), 1)
        for h in range(tn // HEAD_DIM):
            cols = slice(h * HEAD_DIM, (h + 1) * HEAD_DIM)
            delta = jnp.sum(do_ref[:, cols] * o_ref[:, cols].astype(F32), axis=-1, keepdims=True)
            s_ref[:, cols] = jnp.where(lane < STAT_LANE, l_ref[:, cols], delta)

    tile = pl.BlockSpec((tm, tn), lambda i, j: (i, j))
    ins = [dx, w_o, o, lse]
    in_specs = [pl.BlockSpec((tm, D), lambda i, j: (i, 0)), pl.BlockSpec((tn, D), lambda i, j: (j, 0)), tile, tile]
    _dep_operand(ins, in_specs, dep)
    return pl.pallas_call(
        body, grid=(T // tm, W // tn), in_specs=in_specs, out_specs=[tile, tile],
        out_shape=[jax.ShapeDtypeStruct((T, W), F32)] * 2,
        compiler_params=_params(("parallel", "parallel")), name="o_proj_dx")(*ins)


def _att_bwd(q, k, v, do, stats, dq, grp, dil):
    T = q.shape[0]
    H = do.shape[1] // HEAD_DIM
    rep = H // N_KV_HEADS
    hs = rep
    span, sbr, nsb = _att_plan(T, dil)
    scale = HEAD_DIM ** -0.5

    def body(*refs):
        q_refs, do_refs, st_refs = refs[:hs], refs[hs + 2:2 * hs + 2], refs[2 * hs + 2:3 * hs + 2]
        k_ref, v_ref = refs[hs], refs[hs + 1]
        dq_ref, dk_ref, dv_ref, dq_slab = refs[3 * hs + 3:]
        sb = pl.program_id(2)

        @pl.when((pl.program_id(1) == 0) & (sb == 0))
        def _():
            dk_ref[...] = jnp.zeros_like(dk_ref)
            dv_ref[...] = jnp.zeros_like(dv_ref)

        def blk(i, _):
            loc, cur, prv, gb = _att_block(sb, i, sbr, span, dil)
            rows, kc, kp = _att_rows(loc, dil), _att_rows(cur, dil), _att_rows(prv, dil)
            qs = jnp.concatenate([r[rows, :] for r in q_refs], axis=0).astype(BF16)
            dos = jnp.concatenate([r[rows, :] for r in do_refs], axis=0).astype(BF16)
            st = jnp.concatenate([r[rows, :] for r in st_refs], axis=0)
            kcat = jnp.concatenate([k_ref[kp, :], k_ref[kc, :]], axis=0).astype(BF16)
            vcat = jnp.concatenate([v_ref[kp, :], v_ref[kc, :]], axis=0).astype(BF16)
            s = _dot(qs, kcat, NT) * scale
            p = jnp.where(_att_masks(hs, gb), jnp.exp(s - st[:, 0:1]), 0.0)
            dp = _dot(dos, vcat, NT)
            ds = (p * (dp - st[:, STAT_LANE:STAT_LANE + 1]) * scale).astype(BF16)
            dvc = _dot(p.astype(BF16), dos, TN)
            dkc = _dot(ds, qs, TN)
            dqs = _dot(ds, kcat)
            for j in range(hs):
                dq_slab[j, rows, :] = dqs[j * ATT_BLK:(j + 1) * ATT_BLK]
            dk_ref[kc, :] += dkc[ATT_BLK:]
            dv_ref[kc, :] += dvc[ATT_BLK:]

            @pl.when(gb > 0)
            def _():
                dk_ref[kp, :] += dkc[:ATT_BLK]
                dv_ref[kp, :] += dvc[:ATT_BLK]

            return 0

        lax.fori_loop(0, sbr // ATT_BLK, blk, 0)
        for j in range(hs):
            dq_ref[:, j * HEAD_DIM:(j + 1) * HEAD_DIM] = dq_slab[j].astype(BF16)

    def head_specs(col0):
        return [pl.BlockSpec((sbr, HEAD_DIM),
                             functools.partial(lambda h, f, s, j: (s, col0 + h * rep + f * hs + j), j=j))
                for j in range(hs)]

    kspec = pl.BlockSpec((T, HEAD_DIM), lambda h, f, s: (0, h))
    dqspec = pl.BlockSpec((sbr, hs * HEAD_DIM), lambda h, f, s: (s, (grp * H + h * rep) // hs + f))
    n_in = 3 * hs + 3
    return pl.pallas_call(
        body, grid=(N_KV_HEADS, rep // hs, nsb),
        in_specs=head_specs(grp * H) + [kspec, kspec] + head_specs(0) + head_specs(0) + [ANY],
        out_specs=[dqspec, kspec, kspec],
        out_shape=[jax.ShapeDtypeStruct(dq.shape, BF16),
                   jax.ShapeDtypeStruct((T, N_KV_HEADS * HEAD_DIM), F32),
                   jax.ShapeDtypeStruct((T, N_KV_HEADS * HEAD_DIM), F32)],
        scratch_shapes=[pltpu.VMEM((hs, sbr, HEAD_DIM), F32)],
        input_output_aliases={n_in - 1: 0},
        compiler_params=_params(("parallel", "arbitrary", "arbitrary")), name=f"att_bwd_d{dil}",
    )(*([q] * hs), k, v, *([do] * hs), *([stats] * hs), dq)


def _sum_kv(dks, dvs):
    T, W = dks[0].shape
    ng = len(dks)
    tr = _tile(T, 512, 8)

    def body(*refs):
        o_ref = refs[2 * ng]
        o_ref[0] = functools.reduce(lambda a, b: a + b, [r[...] for r in refs[:ng]]).astype(BF16)
        o_ref[1] = functools.reduce(lambda a, b: a + b, [r[...] for r in refs[ng:2 * ng]]).astype(BF16)

    spec = pl.BlockSpec((tr, W), lambda i: (i, 0))
    return pl.pallas_call(
        body, grid=(T // tr,), in_specs=[spec] * (2 * ng),
        out_specs=pl.BlockSpec((2, tr, W), lambda i: (0, i, 0)),
        out_shape=jax.ShapeDtypeStruct((2, T, W), BF16),
        compiler_params=_params(("parallel",)), name="sum_kv")(*dks, *dvs)


def _local_step(x, tgt, w, wts, ready, dep0=None, small_ready=None):
    T, D = x.shape
    g = {}

    (u0,) = _rms_fwd("rms_a", x, [w["a_norm"]], [F32], dep=dep0)
    lbr, lbi, bbt_re, bbt_im = _s5_param_fwd(w["lam_re"], w["lam_im"], w["log_dt"], w["bt_re"], w["bt_im"])
    a_re, a_im = lbr.reshape(-1, 8, 128), lbi.reshape(-1, 8, 128)
    C, G, P = w["bt_re"].shape
    nb = G // S5_GB
    bb2 = jnp.stack([bbt_re, bbt_im]).transpose(0, 2, 1, 3).reshape(2, nb, S5_GB * C, P)
    bb2 = jnp.concatenate([bb2, bb2], axis=-1)
    cc2 = jnp.stack([w["c_re"], w["c_im"]]).reshape(2, nb, S5_GB * C, P).transpose(0, 1, 3, 2)
    z, gp, h0 = _s5_fwd(u0, bb2, cc2, a_re, a_im, w["s5_d"])
    w_glu = wts("glu", z)
    x1, vg = _mm_dual_fwd("glu_fwd", z, w_glu, x, "glu")

    def ffn_fwd(xin, layer):
        (nrm,) = _rms_fwd(f"rms_f{layer}", xin, [w["ffn_norm"][layer:layer + 1]], [BF16])
        w_in = wts(f"win{layer}", nrm)
        gu, act = _mm_dual_fwd(f"ffn_in{layer}", nrm, w_in, None, "ffn")
        w_out = wts(f"wout{layer}", act)
        xout = _mm_nn(f"ffn_out{layer}", act, w_out, res=xin)
        return xout, (nrm, gu, act, w_in, w_out)

    x2, saved0 = ffn_fwd(x1, 0)
    kvn, hb = _rms_fwd("rms_b", x2, [w["kv_norm"], w["b_norm"]], [BF16, BF16])
    w_kv, w_q, w_o = wts("wkv", hb), wts("wq", hb), wts("wo", hb)
    k, v = _mm_kv("kv_proj", kvn, w_kv)
    q = _mm_nn("q_proj", hb, w_q)
    outs, lses = [], []
    for grp, dil in enumerate(DILATIONS):
        o_g, l_g = _att_fwd(q, k, v, grp, dil)
        outs.append(o_g)
        lses.append(l_g)
    o, lse = _att_combine(outs, lses)
    wts("win1", o, prefetch=True)
    x3 = _mm_nn("o_proj", o, w_o, res=x2)
    x4, saved1 = ffn_fwd(x3, 1)
    loss_blk, dx4, dx4b, g["final_norm"] = _loss_head(x4, w["final_norm"], tgt)

    def ffn_bwd(dx, dxb, xin, saved, layer, dep):
        nrm, gu, act, w_in, w_out = saved
        dgu = _mm_nt_ffn_bwd(f"ffn_dact{layer}", dxb, w_out, gu, dep=dep)
        g_wout = _mm_tn(f"ffn_dwout{layer}", act, dxb)
        g_win = _mm_tn_pair(f"ffn_dwin{layer}", nrm, dgu)
        dn = _mm_nt_pair(f"ffn_dn{layer}", dgu, w_in, out_dtype=BF16)
        dxo, dxob, (dgn,) = _rms_bwd(f"rms_f{layer}_bwd", xin, dx, [w["ffn_norm"][layer:layer + 1]], [dn])
        tok = ready({f"win{layer}": g_win, f"wout{layer}": g_wout})
        return dxo, dxob, dgn, tok

    dx3, dx3b, dfn1, tok = ffn_bwd(dx4, dx4b, x3, saved1, 1, None)
    do, stats = _att_out_bwd(dx3b, w_o, o, lse, dep=tok)
    g_wo = _mm_tn("o_proj_dw", o, dx3b)
    dq = lax.empty(q.shape, BF16)
    dks, dvs = [], []
    for grp, dil in enumerate(DILATIONS):
        dq, dk_g, dv_g = _att_bwd(q, k, v, do, stats, dq, grp, dil)
        dks.append(dk_g)
        dvs.append(dv_g)
    dkv = _sum_kv(dks, dvs)
    dhb = _mm_nt("q_proj_dx", [dq], w_q, out_dtype=BF16)
    g_wq = _mm_tn("q_proj_dw", hb, dq)
    dkvn = _mm_nt_pair("kv_proj_dx", dkv, w_kv, out_dtype=BF16)
    g_wkv = _mm_tn_pair("kv_proj_dw", kvn, dkv)
    dx2, dx2b, (g["kv_norm"], g["b_norm"]) = _rms_bwd(
        "rms_b_bwd", x2, dx3, [w["kv_norm"], w["b_norm"]], [dkvn, dhb])
    tok = ready({"wkv": g_wkv, "wq": g_wq, "wo": g_wo})
    dx1, dx1b, dfn0, tok = ffn_bwd(dx2, dx2b, x1, saved0, 0, tok)
    g["ffn_norm"] = jnp.concatenate([dfn0, dfn1], axis=0)

    dvg = _glu_bwd(dx1, vg, dep=tok)
    dz = _mm_nt_pair("glu_dx", dvg, w_glu, out_dtype=BF16)
    tok = ready({"glu": _mm_tn_pair("glu_dw", z, dvg)})
    tok = ready({}, after=tok)
    du, g["s5_d"], da_re, da_im, dbb2, dcc2 = _s5_bwd(
        u0, dz, gp, h0, bb2, cc2, a_re, a_im, w["s5_d"], dep=tok)
    dcc = dcc2.transpose(0, 1, 3, 2).reshape(2, G, C, P)
    g["c_re"], g["c_im"] = dcc[0], -dcc[1]
    dbbt = dbb2[..., :P].reshape(2, G, C, P).transpose(0, 2, 1, 3)
    g["lam_re"], g["lam_im"], g["log_dt"], g["bt_re"], g["bt_im"] = _s5_param_bwd(
        w["lam_re"], w["lam_im"], w["log_dt"], w["bt_re"], w["bt_im"],
        da_re.reshape(G, P), da_im.reshape(G, P), dbbt[0], dbbt[1])
    tok = small_ready(g) if small_ready is not None else None
    grad_x, _, (g["a_norm"],) = _rms_bwd("rms_a_bwd", x, dx1, [w["a_norm"]], [du], dep=tok)
    return loss_blk, grad_x, g


def _coords():
    return lax.axis_index("x"), lax.axis_index("y"), lax.axis_index("c")


def _dev_index(dev):
    return 4 * dev[0] + 2 * dev[1] + dev[2]


def _shard_window(ref, axis, width, idx):
    sl = [slice(None)] * len(ref.shape)
    sl[axis] = pl.ds(pl.multiple_of(idx * width, width), width)
    return ref.at[tuple(sl)]


def _all_gather(name, shards, axes):
    na = len(shards)
    widths = [s.shape[ax] for s, ax in zip(shards, axes)]
    out_shapes = []
    for s, ax in zip(shards, axes):
        shp = list(s.shape)
        shp[ax] *= N_DEV
        out_shapes.append(jax.ShapeDtypeStruct(tuple(shp), s.dtype))

    def body(*refs):
        ins, outs = refs[:na], refs[na:2 * na]
        send_sems, recv_sems, local_sems = refs[2 * na:]
        x, y, c = _coords()
        me, sib = (x, y, c), (x, y, 1 - c)
        chips = [(1 - x, y), (x, 1 - y), (1 - x, 1 - y)]

        def blk(a, dev):
            return _shard_window(outs[a], axes[a], widths[a], _dev_index(dev))

        def copy(a, kk, block, to, src=None):
            return pltpu.make_async_remote_copy(
                src_ref=blk(a, block) if src is None else src, dst_ref=blk(a, block),
                send_sem=send_sems.at[a, kk], recv_sem=recv_sems.at[a, kk],
                device_id=to, device_id_type=MESH)

        local = [pltpu.make_async_copy(ins[a], blk(a, me), local_sems.at[a]) for a in range(na)]
        for cp in local:
            cp.start()
        sent = []
        for a in range(na):
            first = [copy(a, 0, me, sib, src=ins[a])]
            first += [copy(a, 1 + j, me, (*chip, c), src=ins[a]) for j, chip in enumerate(chips)]
            for cp in first:
                cp.start()
            sent += first
        for a in range(na):
            for j, chip in enumerate(chips):
                copy(a, 1 + j, (*chip, c), me).wait_recv()
                fwd = copy(a, 4 + j, (*chip, c), sib)
                fwd.start()
                sent.append(fwd)
        for a in range(na):
            copy(a, 0, sib, me).wait_recv()
            for j, chip in enumerate(chips):
                copy(a, 4 + j, (*chip, 1 - c), me).wait_recv()
        for cp in sent:
            cp.wait_send()
        for cp in local:
            cp.wait()

    return pl.pallas_call(
        body, out_shape=out_shapes, in_specs=[ANY] * na, out_specs=[ANY] * na,
        scratch_shapes=[pltpu.SemaphoreType.DMA((na, 7)), pltpu.SemaphoreType.DMA((na, 7)),
                        pltpu.SemaphoreType.DMA((na,))],
        name=name)(*shards)


HBM = pl.BlockSpec(memory_space=pltpu.HBM)
SEM = pl.BlockSpec(memory_space=pltpu.SEMAPHORE)
TOKEN_SPEC = pl.BlockSpec(memory_space=pltpu.VMEM)
TOKEN_SHAPE = jax.ShapeDtypeStruct((8, 128), F32)
SPLIT_PARAMS = pltpu.CompilerParams(has_side_effects=pltpu.SideEffectType.DATAFLOW_SIDE_EFFECTING)


def _hbm(x):
    return pltpu.with_memory_space_constraint(x, pltpu.HBM)


def _hbm_like(x):
    return pltpu.HBM(x.shape, x.dtype)


def _dma_sems(*shape):
    return pltpu.SemaphoreType.DMA(shape)


def _cast_and_place(name, shard, layer, axis, pos, dtype):
    rows, cols = shard.shape[-2:]
    tr = _tile(rows, 256, 16)
    nt = rows // tr
    full = (rows, cols * N_DEV) if axis == 1 else (rows * N_DEV, cols)

    def dev(p):
        return 4 * p[0] + 2 * p[1] + p[2]

    def body(pos_ref, s_ref, b_ref, l_ref):
        v = s_ref[...].astype(dtype)
        b_ref[...] = v
        l_ref[...] = v

    blk = pl.BlockSpec((tr, cols), lambda i, p: (i, 0))
    if axis == 1:
        lspec = pl.BlockSpec((tr, cols), lambda i, p: (i, dev(p)))
    else:
        lspec = pl.BlockSpec((tr, cols), lambda i, p: (dev(p) * nt + i, 0))
    return pl.pallas_call(
        body, grid_spec=pltpu.PrefetchScalarGridSpec(
            num_scalar_prefetch=1, grid=(nt,),
            in_specs=[pl.BlockSpec((None, tr, cols), lambda i, p: (layer, i, 0))], out_specs=[blk, lspec]),
        out_shape=[jax.ShapeDtypeStruct((rows, cols), dtype), jax.ShapeDtypeStruct(full, dtype)],
        compiler_params=_params(("parallel",)), name=name)(pos, shard)


def _gather_start(name, shards, lands, axes, groups):
    na, ng = len(shards), len(groups)
    widths = [s.shape[ax] for s, ax in zip(shards, axes)]

    def body(*refs):
        sh, ld = refs[:na], refs[na:2 * na]
        sems = refs[2 * na:2 * na + 3 * ng]
        token = refs[-1]
        x, y, c = _coords()
        me, sib = (x, y, c), (x, y, 1 - c)
        chips = [(1 - x, y), (x, 1 - y), (1 - x, 1 - y)]
        for gi, grp in enumerate(groups):
            send, r_d2d, r_ici = sems[3 * gi:3 * gi + 3]
            for li, a in enumerate(grp):
                dst = _shard_window(ld[a], axes[a], widths[a], _dev_index(me))
                pltpu.make_async_remote_copy(
                    src_ref=sh[a], dst_ref=dst, send_sem=send.at[4 * li], recv_sem=r_d2d.at[li],
                    device_id=sib, device_id_type=MESH).start()
                for j, chip in enumerate(chips):
                    pltpu.make_async_remote_copy(
                        src_ref=sh[a], dst_ref=dst, send_sem=send.at[4 * li + 1 + j], recv_sem=r_ici.at[3 * li + j],
                        device_id=(*chip, c), device_id_type=MESH).start()
        token[...] = jnp.zeros_like(token)

    out_shape, out_specs = [], []
    for grp in groups:
        out_shape += [_dma_sems(4 * len(grp)), _dma_sems(len(grp)), _dma_sems(3 * len(grp))]
        out_specs += [SEM] * 3
    out_shape += [_hbm_like(s) for s in shards] + [_hbm_like(l) for l in lands] + [TOKEN_SHAPE]
    out_specs += [HBM] * (2 * na) + [TOKEN_SPEC]
    aliases = {a: 3 * ng + a for a in range(2 * na)}
    res = pl.pallas_call(
        body, name=name, out_shape=out_shape, in_specs=[HBM] * (2 * na),
        out_specs=out_specs, input_output_aliases=aliases, compiler_params=SPLIT_PARAMS,
    )(*[_hbm(s) for s in shards], *[_hbm(l) for l in lands])
    sems = [tuple(res[3 * gi:3 * gi + 3]) for gi in range(ng)]
    return sems, list(res[3 * ng:3 * ng + na]), list(res[3 * ng + na:3 * ng + 2 * na]), res[-1]


def _gather_forward(name, lands, axes, r_ici, after):
    n = len(lands)
    widths = [l.shape[ax] // N_DEV for l, ax in zip(lands, axes)]

    def body(*refs):
        ld, r_ici_ref = refs[:n], refs[n]
        f_send, f_recv = refs[n + 2], refs[n + 3]
        x, y, c = _coords()
        sib = (x, y, 1 - c)
        chips = [(1 - x, y), (x, 1 - y), (1 - x, 1 - y)]
        for li in range(n):
            for j, chip in enumerate(chips):
                blk = _shard_window(ld[li], axes[li], widths[li], _dev_index((*chip, c)))
                pltpu.make_async_remote_copy(
                    src_ref=blk, dst_ref=blk, send_sem=f_send.at[3 * li + j], recv_sem=r_ici_ref.at[3 * li + j],
                    device_id=(*chip, c), device_id_type=MESH).wait_recv()
                pltpu.make_async_remote_copy(
                    src_ref=blk, dst_ref=blk, send_sem=f_send.at[3 * li + j], recv_sem=f_recv.at[3 * li + j],
                    device_id=sib, device_id_type=MESH).start()

    res = pl.pallas_call(
        body, name=name, out_shape=[_dma_sems(3 * n), _dma_sems(3 * n)] + [_hbm_like(l) for l in lands],
        in_specs=[HBM] * n + [SEM, ANY], out_specs=[SEM, SEM] + [HBM] * n,
        input_output_aliases={li: 2 + li for li in range(n)}, compiler_params=SPLIT_PARAMS,
    )(*lands, r_ici, after)
    return res[0], res[1], list(res[2:])


def _gather_finish(name, shards, lands, axes, send, r_d2d, f_send, f_recv, after):
    n = len(lands)
    widths = [l.shape[ax] // N_DEV for l, ax in zip(lands, axes)]

    def body(*refs):
        sh, ld = refs[:n], refs[n:2 * n]
        send_ref, r_d2d_ref, f_send_ref, f_recv_ref = refs[2 * n:2 * n + 4]
        x, y, c = _coords()
        me, sib = (x, y, c), (x, y, 1 - c)
        chips = [(1 - x, y), (x, 1 - y), (1 - x, 1 - y)]

        def blk(li, dev):
            return _shard_window(ld[li], axes[li], widths[li], _dev_index(dev))

        for li in range(n):
            for kk in range(4):
                pltpu.make_async_remote_copy(
                    src_ref=sh[li], dst_ref=blk(li, me), send_sem=send_ref.at[4 * li + kk], recv_sem=r_d2d_ref.at[li],
                    device_id=sib, device_id_type=MESH).wait_send()
            pltpu.make_async_remote_copy(
                src_ref=blk(li, sib), dst_ref=blk(li, sib), send_sem=send_ref.at[4 * li], recv_sem=r_d2d_ref.at[li],
                device_id=sib, device_id_type=MESH).wait_recv()
            for j, chip in enumerate(chips):
                pltpu.make_async_remote_copy(
                    src_ref=blk(li, (*chip, c)), dst_ref=blk(li, (*chip, c)), send_sem=f_send_ref.at[3 * li + j],
                    recv_sem=f_recv_ref.at[3 * li + j], device_id=sib, device_id_type=MESH).wait_send()
                pltpu.make_async_remote_copy(
                    src_ref=blk(li, (*chip, 1 - c)), dst_ref=blk(li, (*chip, 1 - c)), send_sem=f_send_ref.at[3 * li + j],
                    recv_sem=f_recv_ref.at[3 * li + j], device_id=sib, device_id_type=MESH).wait_recv()

    res = pl.pallas_call(
        body, name=name, out_shape=[_hbm_like(s) for s in shards] + [_hbm_like(l) for l in lands],
        in_specs=[HBM] * (2 * n) + [SEM] * 4 + [ANY], out_specs=[HBM] * (2 * n),
        input_output_aliases={i: i for i in range(2 * n)}, compiler_params=SPLIT_PARAMS,
    )(*shards, *lands, send, r_d2d, f_send, f_recv, after)
    return list(res[n:])


def _chip_exchange_start(name, parts):
    n = len(parts)

    def body(*refs):
        src, ld = refs[:n], refs[n:2 * n]
        send, recv = refs[2 * n], refs[2 * n + 1]
        token = refs[-1]
        x, y, c = _coords()
        chips = [(1 - x, y), (x, 1 - y), (1 - x, 1 - y)]
        for li in range(n):
            for kk, chip in enumerate(chips):
                pltpu.make_async_remote_copy(
                    src_ref=src[li].at[kk], dst_ref=ld[li].at[kk], send_sem=send.at[3 * li + kk],
                    recv_sem=recv.at[3 * li + kk], device_id=(*chip, c), device_id_type=MESH).start()
        token[...] = jnp.zeros_like(token)

    lands = [lax.empty(p.shape, p.dtype) for p in parts]
    res = pl.pallas_call(
        body, name=name,
        out_shape=[_dma_sems(3 * n), _dma_sems(3 * n)] + [_hbm_like(p) for p in parts] * 2 + [TOKEN_SHAPE],
        in_specs=[HBM] * (2 * n), out_specs=[SEM, SEM] + [HBM] * (2 * n) + [TOKEN_SPEC],
        input_output_aliases={i: 2 + i for i in range(2 * n)}, compiler_params=SPLIT_PARAMS,
    )(*[_hbm(p) for p in parts], *[_hbm(l) for l in lands])
    return res[0], res[1], list(res[2:2 + n]), list(res[2 + n:2 + 2 * n]), res[-1]


def _chip_exchange_finish(name, started, after):
    counts = [len(st[2]) for st in started]
    total = sum(counts)
    ns = len(started)

    def body(*refs):
        src, ld = refs[:total], refs[total:2 * total]
        sems = refs[2 * total:2 * total + 2 * ns]
        x, y, c = _coords()
        chips = [(1 - x, y), (x, 1 - y), (1 - x, 1 - y)]
        off = 0
        for si, cnt in enumerate(counts):
            send, recv = sems[2 * si], sems[2 * si + 1]
            for li in range(cnt):
                for kk, chip in enumerate(chips):
                    cp = pltpu.make_async_remote_copy(
                        src_ref=src[off + li].at[kk], dst_ref=ld[off + li].at[kk], send_sem=send.at[3 * li + kk],
                        recv_sem=recv.at[3 * li + kk], device_id=(*chip, c), device_id_type=MESH)
                    cp.wait_send()
                    cp.wait_recv()
            off += cnt

    srcs = [p for st in started for p in st[2]]
    lands = [l for st in started for l in st[3]]
    sems = [s for st in started for s in st[:2]]
    res = pl.pallas_call(
        body, name=name, out_shape=[_hbm_like(p) for p in srcs + lands],
        in_specs=[HBM] * (2 * total) + [SEM] * (2 * ns) + [ANY], out_specs=[HBM] * (2 * total),
        input_output_aliases={i: i for i in range(2 * total)}, compiler_params=SPLIT_PARAMS,
    )(*srcs, *lands, *sems, after)
    out, off = [], total
    for cnt in counts:
        out.append(list(res[off:off + cnt]))
        off += cnt
    return out


def _pair_copies(src, ld, send, recv, axes, widths):
    x, y, c = _coords()
    chips = [(x, y), (1 - x, y), (x, 1 - y), (1 - x, 1 - y)]
    return [pltpu.make_async_remote_copy(
        src_ref=_shard_window(src[li], axes[li], widths[li], _dev_index((*chip, 1 - c))),
        dst_ref=ld[li].at[kk], send_sem=send.at[4 * li + kk], recv_sem=recv.at[4 * li + kk],
        device_id=(x, y, 1 - c), device_id_type=MESH)
        for li in range(len(src)) for kk, chip in enumerate(chips)]


def _pair_exchange_start(name, grads, axes):
    n = len(grads)
    widths = [gr.shape[ax] // N_DEV for gr, ax in zip(grads, axes)]
    lands = []
    for gr, ax, wd in zip(grads, axes, widths):
        shp = list(gr.shape)
        shp[ax] = wd
        lands.append(lax.empty((4, *shp), gr.dtype))

    def body(*refs):
        for cp in _pair_copies(refs[:n], refs[n:2 * n], refs[2 * n], refs[2 * n + 1], axes, widths):
            cp.start()
        refs[-1][...] = jnp.zeros_like(refs[-1])

    res = pl.pallas_call(
        body, name=name,
        out_shape=[_dma_sems(4 * n), _dma_sems(4 * n)] + [_hbm_like(a) for a in grads + lands] + [TOKEN_SHAPE],
        in_specs=[HBM] * (2 * n), out_specs=[SEM, SEM] + [HBM] * (2 * n) + [TOKEN_SPEC],
        input_output_aliases={i: 2 + i for i in range(2 * n)}, compiler_params=SPLIT_PARAMS,
    )(*[_hbm(a) for a in grads + lands])
    return res[0], res[1], list(res[2:2 + n]), list(res[2 + n:2 + 2 * n]), res[-1]


def _pair_exchange_finish(name, started, axes, after):
    send, recv, grads, lands, _ = started
    n = len(grads)
    widths = [gr.shape[ax] // N_DEV for gr, ax in zip(grads, axes)]

    def body(*refs):
        for cp in _pair_copies(refs[:n], refs[n:2 * n], refs[2 * n], refs[2 * n + 1], axes, widths):
            cp.wait_send()
            cp.wait_recv()

    res = pl.pallas_call(
        body, name=name, out_shape=[_hbm_like(a) for a in grads + lands],
        in_specs=[HBM] * (2 * n) + [SEM, SEM, ANY], out_specs=[HBM] * (2 * n),
        input_output_aliases={i: i for i in range(2 * n)}, compiler_params=SPLIT_PARAMS,
    )(*grads, *lands, send, recv, after)
    return list(res[:n]), list(res[n:])


def _pair_sum(name, grad, land, axis, pos):
    wd = grad.shape[axis] // N_DEV
    shard_shape = land.shape[1:]
    rows, cols = shard_shape
    tr = _tile(rows, 256, 16)
    nt = rows // tr

    def dev_of(kk, pos_ref):
        return 4 * (pos_ref[0] ^ (kk & 1)) + 2 * (pos_ref[1] ^ (kk >> 1)) + pos_ref[2]

    def gspec(kk):
        if axis == 1:
            return pl.BlockSpec((tr, wd), lambda t, p: (t, dev_of(kk, p)))
        return pl.BlockSpec((tr, cols), lambda t, p: (dev_of(kk, p) * nt + t, 0))

    def body(pos_ref, g0, g1, g2, g3, l_ref, p0_ref, pb_ref):
        p0_ref[...] = g0[...].astype(F32) + l_ref[0].astype(F32)
        for kk, g_ref in enumerate((g1, g2, g3)):
            pb_ref[kk] = (g_ref[...].astype(F32) + l_ref[kk + 1].astype(F32)).astype(BF16)

    return pl.pallas_call(
        body,
        grid_spec=pltpu.PrefetchScalarGridSpec(
            num_scalar_prefetch=1, grid=(nt,),
            in_specs=[gspec(kk) for kk in range(4)] + [pl.BlockSpec((4, tr, cols), lambda t, p: (0, t, 0))],
            out_specs=[pl.BlockSpec((tr, cols), lambda t, p: (t, 0)),
                       pl.BlockSpec((3, tr, cols), lambda t, p: (0, t, 0))]),
        out_shape=[jax.ShapeDtypeStruct(shard_shape, F32), jax.ShapeDtypeStruct((3, *shard_shape), BF16)],
        compiler_params=_params(("parallel",)), name=name)(pos, grad, grad, grad, grad, land)


def _adamw_math(g, w, m, v):
    c1 = 1.0 - ADAM_B1 ** ADAM_STEP
    c2 = 1.0 - ADAM_B2 ** ADAM_STEP
    nm = ADAM_B1 * m + (1.0 - ADAM_B1) * g
    nv = ADAM_B2 * v + (1.0 - ADAM_B2) * (g * g)
    return -ADAM_LR * ((nm / c1) / (jnp.sqrt(nv / c2) + ADAM_EPS) + ADAM_WD * w), nm, nv


def _sum_parts(name, parts):
    n, rows, cols = parts.shape
    tr = _tile(rows, 1024, 8)

    def body(p_ref, o_ref):
        o_ref[...] = functools.reduce(lambda a, b: a + b, [p_ref[i] for i in range(n)])

    return pl.pallas_call(
        body, grid=(rows // tr,), in_specs=[pl.BlockSpec((n, tr, cols), lambda i: (0, i, 0))],
        out_specs=pl.BlockSpec((tr, cols), lambda i: (i, 0)), out_shape=jax.ShapeDtypeStruct((rows, cols), F32),
        compiler_params=_params(("parallel",)), name=name)(parts)


def _adamw_multi(name, gs, ws, ms, vs):
    k = len(gs)
    rows, cols = ws[0].shape
    tr = _tile(rows, 1024, 8)

    def body(*refs):
        for i in range(k):
            g_ref, w_ref, m_ref, v_ref = (refs[j * k + i] for j in range(4))
            og_ref, d_ref, nm_ref, nv_ref = (refs[(4 + j) * k + i] for j in range(4))
            g = g_ref[...]
            og_ref[...] = g
            d_ref[...], nm_ref[...], nv_ref[...] = _adamw_math(g, w_ref[...], m_ref[...], v_ref[...])

    spec = pl.BlockSpec((tr, cols), lambda i: (i, 0))
    res = pl.pallas_call(
        body, grid=(rows // tr,), in_specs=[spec] * (4 * k), out_specs=[spec] * (4 * k),
        out_shape=[jax.ShapeDtypeStruct((rows, cols), F32)] * (4 * k),
        compiler_params=_params(("parallel",)), name=name)(*gs, *ws, *ms, *vs)
    return [[res[j * k + i] for j in range(4)] for i in range(k)]


def _adamw(name, parts, w, m, v, layer=None, prev=None):
    rows, cols = w.shape[-2:]
    tr = _tile(rows, 256, 8)
    npart = len(parts)

    def body(*refs):
        p_refs = refs[:npart]
        w_ref, m_ref, v_ref = refs[npart:npart + 3]
        g_ref, d_ref, nm_ref, nv_ref = refs[-4:]
        g = None
        for r in p_refs:
            if len(r.shape) == 3:
                for i in range(r.shape[0]):
                    t = r[i].astype(F32)
                    g = t if g is None else g + t
            else:
                t = r[...].astype(F32)
                g = t if g is None else g + t
        g_ref[...] = g
        d_ref[...], nm_ref[...], nv_ref[...] = _adamw_math(g, w_ref[...], m_ref[...], v_ref[...])

    spec = pl.BlockSpec((tr, cols), lambda i: (i, 0))
    wspec = spec if layer is None else pl.BlockSpec((None, tr, cols), lambda i: (layer, i, 0))
    pspecs = [pl.BlockSpec((p.shape[0], tr, cols), lambda i: (0, i, 0)) if p.ndim == 3 else spec
              for p in parts]
    prev = list(prev) if prev else []
    return pl.pallas_call(
        body, grid=(rows // tr,), in_specs=pspecs + [wspec] * 3 + [ANY] * len(prev), out_specs=[wspec] * 4,
        out_shape=[jax.ShapeDtypeStruct(w.shape, F32)] * 4,
        input_output_aliases={npart + 3 + i: i for i in range(len(prev))},
        compiler_params=_params(("parallel",)), name=name)(*parts, w, m, v, *prev)


SMALL_NAMES = ("s5_lam_re", "s5_lam_im", "s5_log_dt", "s5_b_re", "s5_b_im", "s5_c_re", "s5_c_im",
               "ffn_norm", "b_norm_mix", "kv_norm", "final_norm")
SMALL_PAD = 1024
SMALL_GROUPS = (("s5_lam_re", "s5_lam_im"), ("s5_log_dt",), ("s5_b_re", "s5_b_im"), ("s5_c_re", "s5_c_im"),
                ("ffn_norm",), ("b_norm_mix", "kv_norm", "final_norm"))


def _pack(parts):
    flat = []
    for p in parts:
        f = p.reshape(-1)
        pad = (-f.shape[0]) % SMALL_PAD
        if pad:
            f = jnp.concatenate([f, jnp.zeros((pad,), f.dtype)])
        flat.append(f)
    return jnp.concatenate(flat).reshape(-1, 128)


def _unpack(packed, shapes):
    flat = packed.reshape(-1)
    out, off = [], 0
    for shp in shapes:
        size = math.prod(shp)
        out.append(flat[off:off + size].reshape(shp))
        off += size + (-size) % SMALL_PAD
    return out


def kernel(x, s5_lam_re, s5_lam_im, s5_log_dt, s5_b_re, s5_b_im, s5_c_re, s5_c_im, s5_d, s5_w_glu, a_norm_mix, ffn_norm, ffn_w_in, ffn_w_out, b_norm_mix, attn_w_q, attn_w_o, kv_norm, w_kv, final_norm, loss_target, m_s5_lam_re, m_s5_lam_im, m_s5_log_dt, m_s5_b_re, m_s5_b_im, m_s5_c_re, m_s5_c_im, m_s5_d, m_s5_w_glu, m_a_norm_mix, m_ffn_norm, m_ffn_w_in, m_ffn_w_out, m_b_norm_mix, m_attn_w_q, m_attn_w_o, m_kv_norm, m_w_kv, m_final_norm, v_s5_lam_re, v_s5_lam_im, v_s5_log_dt, v_s5_b_re, v_s5_b_im, v_s5_c_re, v_s5_c_im, v_s5_d, v_s5_w_glu, v_a_norm_mix, v_ffn_norm, v_ffn_w_in, v_ffn_w_out, v_b_norm_mix, v_attn_w_q, v_attn_w_o, v_kv_norm, v_w_kv, v_final_norm):
    args = dict(locals())
    T, D = x.shape[1], x.shape[2]
    n_layers = ffn_w_in.shape[0]
    xi_, yi_, ci_ = _coords()
    pos = jnp.stack([xi_, yi_, ci_]).astype(jnp.int32)
    me = 4 * xi_ + 2 * yi_ + ci_

    big_names = ["glu"] + [f"win{l}" for l in range(n_layers)] + [f"wout{l}" for l in range(n_layers)] \
        + ["wkv", "wq", "wo"]
    big_shards = [(s5_w_glu, 0)] + [(ffn_w_in, l) for l in range(n_layers)] \
        + [(ffn_w_out, l) for l in range(n_layers)] + [(w_kv[None], 0), (attn_w_q, 0), (attn_w_o, 0)]
    big_axes = [1] + [1] * n_layers + [0] * n_layers + [0, 1, 0]
    big_out_names = ["s5_w_glu"] + ["ffn_w_in"] * n_layers + ["ffn_w_out"] * n_layers \
        + ["w_kv", "attn_w_q", "attn_w_o"]
    index_of = {n: i for i, n in enumerate(big_names)}
    vec_shard = jnp.concatenate([s5_d, a_norm_mix], axis=0)
    (vecs,) = _all_gather("vectors_all_gather", [vec_shard], [1])

    gather_groups = [["glu"], ["win0"], ["wout0"], ["wkv", "wq", "wo"], ["win1", "wout1"]]
    group_idx = [[index_of[n] for n in grp] for grp in gather_groups]
    group_of = {n: gi for gi, grp in enumerate(gather_groups) for n in grp}
    placed = [_cast_and_place(f"cast_place_{n}", s, l, ax, pos, BF16)
              for n, (s, l), ax in zip(big_names, big_shards, big_axes)]
    gather_sems, shards_thru, lands_thru, start_token = _gather_start(
        "weights_gather_start", [p[0] for p in placed], [p[1] for p in placed], big_axes, group_idx)
    full = {}

    forwarded = {}

    def wts(name, after, prefetch=False):
        gi = group_of[name]
        idx = group_idx[gi]
        axes = [big_axes[a] for a in idx]
        send, r_d2d, r_ici = gather_sems[gi]
        if gi not in forwarded:
            forwarded[gi] = _gather_forward(
                f"weights_gather_forward{gi}", [lands_thru[a] for a in idx], axes, r_ici, after)
        if prefetch:
            return None
        if name not in full:
            f_send, f_recv, lands = forwarded[gi]
            done = _gather_finish(f"weights_gather_finish{gi}", [shards_thru[a] for a in idx], lands, axes,
                                  send, r_d2d, f_send, f_recv, after)
            full.update(zip(gather_groups[gi], done))
        return full[name]

    exchanges, pending = [], []

    def flush(after):
        names, axes, started = pending.pop()
        tag = "_".join(names)
        grads, lands = _pair_exchange_finish(f"rs_pair_exchange_finish_{tag}", started, axes, after)
        p0s, pbs = [], []
        for n, gr, land, ax in zip(names, grads, lands, axes):
            p0, pb = _pair_sum(f"rs_pair_sum_{n}", gr, land, ax, pos)
            p0s.append(p0)
            pbs.append(pb)
        started = _chip_exchange_start(f"rs_chip_exchange_start_{tag}", pbs)
        exchanges.append((names, p0s, started))
        return started[4]

    def ready(grads, after=None):
        if not grads:
            return flush(after)
        names = list(grads)
        axes = [big_axes[index_of[n]] for n in names]
        started = _pair_exchange_start(f"rs_pair_exchange_start_{'_'.join(names)}", [grads[n] for n in names], axes)
        token = flush(started[4]) if pending else started[4]
        pending.append((names, axes, started))
        return token

    G, P, C = s5_b_re.shape[1:]
    w = dict(
        a_norm=vecs[1:2], s5_d=vecs[0:1],
        ffn_norm=ffn_norm, b_norm=b_norm_mix, kv_norm=kv_norm.reshape(1, D), final_norm=final_norm.reshape(1, D),
        lam_re=s5_lam_re[0], lam_im=s5_lam_im[0], log_dt=s5_log_dt.reshape(G, 1),
        bt_re=s5_b_re[0].transpose(2, 0, 1), bt_im=s5_b_im[0].transpose(2, 0, 1),
        c_re=s5_c_re[0], c_im=s5_c_im[0],
    )
    small = {}

    def small_ready(g):
        small_g = dict(
            s5_lam_re=g["lam_re"], s5_lam_im=g["lam_im"], s5_log_dt=g["log_dt"],
            s5_b_re=g["bt_re"].transpose(1, 2, 0), s5_b_im=g["bt_im"].transpose(1, 2, 0),
            s5_c_re=g["c_re"], s5_c_im=g["c_im"], ffn_norm=g["ffn_norm"], b_norm_mix=g["b_norm"],
            kv_norm=g["kv_norm"], final_norm=g["final_norm"])
        packed = _pack([small_g[n] for n in SMALL_NAMES])
        shard, land = _cast_and_place("place_small_grads", packed[None], 0, 0, pos, F32)
        sems, thru, lands, token = _gather_start("small_grads_gather_start", [shard], [land], [0], [[0]])
        small.update(sems=sems[0], thru=thru, lands=lands, rows=packed.shape[0])
        return token

    loss_blk, grad_x, g = _local_step(x[0], loss_target[0], w, wts, ready, dep0=start_token,
                                      small_ready=small_ready)
    loss = lax.psum(loss_blk[0, 0], ("x", "y", "c"))
    (tail,) = _all_gather("vector_grads_all_gather", [_pack([g["s5_d"], g["a_norm"]])], [0])

    out = {}

    def put(name, res, shape):
        for kind, r in zip(("grad", "delta", "new_m", "new_v"), res):
            out[f"{kind}_{name}"] = r.reshape(shape)

    updated = {}

    def update(names, p0s, recvd):
        for name, p0, rc in zip(names, p0s, recvd):
            oname = big_out_names[index_of[name]]
            w3, layer = big_shards[index_of[name]]
            updated[oname] = _adamw(f"adamw_{name}", [p0, rc], w3, args["m_" + oname].reshape(w3.shape),
                                    args["v_" + oname].reshape(w3.shape), layer=layer, prev=updated.get(oname))
        return updated[big_out_names[index_of[names[0]]]][0]

    early, last = exchanges[:-1], exchanges[-1]
    landed = _chip_exchange_finish("rs_chip_exchange_finish_early", [e[2] for e in early], grad_x)
    for (names, p0s, _), recvd in zip(early[:-1], landed[:-1]):
        marker = update(names, p0s, recvd)
    send, r_d2d, r_ici = small["sems"]
    f_send, f_recv, small_lands = _gather_forward("small_grads_gather_forward", small["lands"], [0], r_ici, marker)
    marker = update(early[-1][0], early[-1][1], landed[-1])
    (all_parts,) = _gather_finish("small_grads_gather_finish", small["thru"], small_lands, [0],
                                  send, r_d2d, f_send, f_recv, marker)
    (recvd,) = _chip_exchange_finish("rs_chip_exchange_finish_last", [last[2]], marker)
    update(last[0], last[1], recvd)
    for oname, res in updated.items():
        put(oname, res, args[oname].shape)
    g_small = _sum_parts("small_grads_sum", all_parts.reshape(N_DEV, small["rows"], 128))
    g_small = dict(zip(SMALL_NAMES, _unpack(g_small, [args[n].shape for n in SMALL_NAMES])))

    def view(a):
        return a.reshape(1, -1) if a.ndim == 1 else a.reshape(-1, a.shape[-1])

    for grp in SMALL_GROUPS:
        res = _adamw_multi(f"adamw_{grp[0]}", [view(g_small[n]) for n in grp], [view(args[n]) for n in grp],
                           [view(args["m_" + n]) for n in grp], [view(args["v_" + n]) for n in grp])
        for n, r in zip(grp, res):
            put(n, r, args[n].shape)
    ws = D // N_DEV
    tail = lax.dynamic_slice_in_dim(tail.reshape(N_DEV, 2, D), me * ws, ws, axis=2)
    res = _adamw("adamw_vec", [tail], vec_shard,
                 jnp.concatenate([m_s5_d, m_a_norm_mix], axis=0), jnp.concatenate([v_s5_d, v_a_norm_mix], axis=0))
    put("s5_d", [r[0:1] for r in res], s5_d.shape)
    put("a_norm_mix", [r[1:2] for r in res], a_norm_mix.shape)

    names = ("s5_lam_re", "s5_lam_im", "s5_log_dt", "s5_b_re", "s5_b_im", "s5_c_re", "s5_c_im", "s5_d",
             "s5_w_glu", "a_norm_mix", "ffn_norm", "ffn_w_in", "ffn_w_out", "b_norm_mix", "attn_w_q",
             "attn_w_o", "kv_norm", "w_kv", "final_norm")
    result = [loss, grad_x.reshape(x.shape)]
    for kind in ("grad", "delta", "new_m", "new_v"):
        result += [out[f"{kind}_{n}"] for n in names]
    return tuple(result)
```

```python
import functools
import math

import jax
import jax.numpy as jnp
from jax import lax
from jax.experimental import pallas as pl
from jax.experimental.pallas import tpu as pltpu

F32 = jnp.float32
BF16 = jnp.bfloat16

EPS = 1e-6
NEG_INF = -1e30
HEAD_DIM = 128
N_KV_HEADS = 4
DILATIONS = (1, 4, 16)
ATT_BLK = 128
S5_C = 16
S5_P = 64
S5_GB = 16
S5_CH = S5_GB * S5_C
S5_W = S5_GB * S5_P
S5_UNROLL = 4
N_DEV = 8

ADAM_LR = 0.001
ADAM_B1 = 0.9
ADAM_B2 = 0.999
ADAM_EPS = 1e-08
ADAM_WD = 0.01
ADAM_STEP = 10

VMEM_LIMIT_BYTES = 56 * 1024 * 1024
MM_TILE = 1024
MM_TILE_NARROW = 512
MM_DEPTH = 2816
MESH = pl.DeviceIdType.MESH
ANY = pl.BlockSpec(memory_space=pl.ANY)


def _tile(n, pref, align=128):
    t = (min(pref, n) // align) * align
    while t >= align:
        if n % t == 0:
            return t
        t -= align
    return n


def _params(sem):
    return pltpu.CompilerParams(dimension_semantics=sem, vmem_limit_bytes=VMEM_LIMIT_BYTES)


def _sigmoid(x):
    return 1.0 / (1.0 + jnp.exp(-x))


NN = (((1,), (0,)), ((), ()))
NT = (((1,), (1,)), ((), ()))
TN = (((0,), (0,)), ((), ()))


def _dot(a, b, dims=NN):
    return lax.dot_general(a, b, dims, preferred_element_type=F32)


def _matmul(name, grid, ins, in_specs, products, dims, out_shapes, out_specs, acc_shapes, epilogue):
    n_in, n_out, nk = len(ins), len(out_shapes), grid[2]

    def body(*refs):
        in_refs = refs[:n_in]
        out_refs = refs[n_in:n_in + n_out]
        acc_refs = refs[n_in + n_out:]

        def prods():
            vals = [None] * len(acc_shapes)
            for ai, bi, ci in products:
                d = _dot(in_refs[ai][...].astype(BF16), in_refs[bi][...].astype(BF16), dims)
                vals[ci] = d if vals[ci] is None else vals[ci] + d
            return vals

        if nk == 1:
            epilogue(in_refs, out_refs, prods())
        else:
            k = pl.program_id(2)

            @pl.when(k == 0)
            def _():
                for a in acc_refs:
                    a[...] = jnp.zeros_like(a)

            for a, v in zip(acc_refs, prods()):
                a[...] += v

            @pl.when(k == nk - 1)
            def _():
                epilogue(in_refs, out_refs, [a[...] for a in acc_refs])

    scratch = [] if nk == 1 else [pltpu.VMEM(s, F32) for s in acc_shapes]
    return pl.pallas_call(
        body, grid=grid, in_specs=in_specs, out_specs=out_specs, out_shape=out_shapes,
        scratch_shapes=scratch, compiler_params=_params(("parallel", "parallel", "arbitrary")),
        name=name)(*ins)


def _mm_dual_fwd(name, a, w, res, kind):
    T, K = a.shape
    N = w.shape[1] // 2
    tm, tn = _tile(T, MM_TILE), _tile(N, MM_TILE_NARROW)
    nj = N // tn
    grid = (T // tm, nj, 1)
    ins = [a, w, w]
    in_specs = [pl.BlockSpec((tm, K), lambda i, j, k: (i, 0)),
                pl.BlockSpec((K, tn), lambda i, j, k: (0, j)),
                pl.BlockSpec((K, tn), lambda i, j, k: (0, j + nj))]
    pair_spec = pl.BlockSpec((2, tm, tn), lambda i, j, k: (0, i, j))
    tile_spec = pl.BlockSpec((tm, tn), lambda i, j, k: (i, j))
    if kind == "glu":
        ins.append(res)
        in_specs.append(tile_spec)

        def epilogue(in_refs, out_refs, accs):
            val, gate = accs
            s = _sigmoid(gate)
            out_refs[0][...] = in_refs[3][...] + val * s
            out_refs[1][0] = s.astype(BF16)
            out_refs[1][1] = (val * s * (1.0 - s)).astype(BF16)

        out_shapes = [jax.ShapeDtypeStruct((T, N), F32), jax.ShapeDtypeStruct((2, T, N), BF16)]
        out_specs = [tile_spec, pair_spec]
    else:
        def epilogue(in_refs, out_refs, accs):
            g, u = accs
            s = _sigmoid(g)
            silu = g * s
            out_refs[0][0] = (u * (s * (1.0 + g * (1.0 - s)))).astype(BF16)
            out_refs[0][1] = silu.astype(BF16)
            out_refs[1][...] = (silu * u).astype(BF16)

        out_shapes = [jax.ShapeDtypeStruct((2, T, N), BF16), jax.ShapeDtypeStruct((T, N), BF16)]
        out_specs = [pair_spec, tile_spec]
    return _matmul(name, grid, ins, in_specs, [(0, 1, 0), (0, 2, 1)], NN, out_shapes, out_specs,
                   [(tm, tn), (tm, tn)], epilogue)


def _mm_kv(name, a, w):
    T, K = a.shape
    N = w.shape[1] // 2
    tm, tn = _tile(T, MM_TILE), _tile(N, MM_TILE_NARROW)
    nj = N // tn
    tile_spec = pl.BlockSpec((tm, tn), lambda i, j, k: (i, j))

    def epilogue(in_refs, out_refs, accs):
        out_refs[0][...] = accs[0]
        out_refs[1][...] = accs[1]

    return _matmul(name, (T // tm, nj, 1), [a, w, w],
                   [pl.BlockSpec((tm, K), lambda i, j, k: (i, 0)),
                    pl.BlockSpec((K, tn), lambda i, j, k: (0, j)),
                    pl.BlockSpec((K, tn), lambda i, j, k: (0, j + nj))],
                   [(0, 1, 0), (0, 2, 1)], NN,
                   [jax.ShapeDtypeStruct((T, N), F32)] * 2, [tile_spec, tile_spec],
                   [(tm, tn), (tm, tn)], epilogue)


def _mm_nn(name, a, w, res=None, out_dtype=F32):
    T, K = a.shape
    N = w.shape[1]
    tk = K if K <= 2 * MM_DEPTH else _tile(K, MM_DEPTH)
    tm, tn = _tile(T, MM_TILE), _tile(N, MM_TILE if K <= MM_DEPTH else MM_TILE_NARROW)
    grid = (T // tm, N // tn, K // tk)
    tile_spec = pl.BlockSpec((tm, tn), lambda i, j, k: (i, j))
    ins = [a, w]
    in_specs = [pl.BlockSpec((tm, tk), lambda i, j, k: (i, k)),
                pl.BlockSpec((tk, tn), lambda i, j, k: (k, j))]
    if res is not None:
        ins.append(res)
        in_specs.append(tile_spec)

    def epilogue(in_refs, out_refs, accs):
        v = accs[0]
        if res is not None:
            v = v + in_refs[2][...]
        out_refs[0][...] = v.astype(out_dtype)

    return _matmul(name, grid, ins, in_specs, [(0, 1, 0)], NN,
                   [jax.ShapeDtypeStruct((T, N), out_dtype)], [tile_spec], [(tm, tn)], epilogue)[0]


def _dep_operand(ins, in_specs, dep):
    if dep is not None:
        ins.append(dep)
        in_specs.append(pl.BlockSpec((8, 128), lambda *_: (0, 0)))


def _mm_nt(name, a_list, w, out_dtype=F32, dep=None):
    T, Np = a_list[0].shape
    Ko = w.shape[0]
    n_parts = len(a_list)
    wide_a = a_list[0].dtype != BF16
    tm, tn, tk = _tile(T, MM_TILE_NARROW if wide_a else MM_TILE), _tile(Ko, MM_TILE), _tile(Np, MM_DEPTH)
    nkp = Np // tk
    grid = (T // tm, Ko // tn, nkp)
    ins = list(a_list) + [w] * n_parts
    in_specs = [pl.BlockSpec((tm, tk), lambda i, j, k: (i, k)) for _ in a_list]
    in_specs += [pl.BlockSpec((tn, tk), functools.partial(lambda i, j, k, p: (j, p * nkp + k), p=p))
                 for p in range(n_parts)]
    products = [(p, n_parts + p, 0) for p in range(n_parts)]
    _dep_operand(ins, in_specs, dep)

    def epilogue(in_refs, out_refs, accs):
        out_refs[0][...] = accs[0].astype(out_dtype)

    return _matmul(name, grid, ins, in_specs, products, NT,
                   [jax.ShapeDtypeStruct((T, Ko), out_dtype)],
                   [pl.BlockSpec((tm, tn), lambda i, j, k: (i, j))], [(tm, tn)], epilogue)[0]


def _mm_nt_pair(name, a3, w, out_dtype=F32):
    _, T, N = a3.shape
    Ko = w.shape[0]
    tm, tn, tk = _tile(T, MM_TILE), _tile(Ko, MM_TILE), _tile(N, MM_DEPTH)
    nkh = N // tk
    grid = (T // tm, Ko // tn, 2 * nkh)

    def epilogue(in_refs, out_refs, accs):
        out_refs[0][...] = accs[0].astype(out_dtype)

    return _matmul(name, grid, [a3, w],
                   [pl.BlockSpec((None, tm, tk), lambda i, j, k: (k // nkh, i, k % nkh)),
                    pl.BlockSpec((tn, tk), lambda i, j, k: (j, k))],
                   [(0, 1, 0)], NT, [jax.ShapeDtypeStruct((T, Ko), out_dtype)],
                   [pl.BlockSpec((tm, tn), lambda i, j, k: (i, j))], [(tm, tn)], epilogue)[0]


def _mm_nt_ffn_bwd(name, dx, w_out, gu, dep=None):
    T, D = dx.shape
    Fh = w_out.shape[0]
    tm, tn = _tile(T, 2 * MM_TILE), _tile(Fh, MM_TILE_NARROW)
    n_dep = 0 if dep is None else 1

    def body(dx_ref, w_ref, gu_ref, *rest):
        out_ref = rest[n_dep]
        da = _dot(dx_ref[...], w_ref[...], NT)
        out_ref[0] = (da * gu_ref[0].astype(F32)).astype(BF16)
        out_ref[1] = (da * gu_ref[1].astype(F32)).astype(BF16)

    pair_spec = pl.BlockSpec((2, tm, tn), lambda i, j: (0, i, j))
    ins = [dx, w_out, gu]
    in_specs = [pl.BlockSpec((tm, D), lambda i, j: (i, 0)),
                pl.BlockSpec((tn, D), lambda i, j: (j, 0)),
                pair_spec]
    _dep_operand(ins, in_specs, dep)
    return pl.pallas_call(
        body, grid=(T // tm, Fh // tn), in_specs=in_specs, out_specs=pair_spec,
        out_shape=jax.ShapeDtypeStruct((2, T, Fh), BF16),
        compiler_params=_params(("parallel", "parallel")), name=name)(*ins)


def _mm_tn(name, a, d):
    T, Ko = a.shape
    N = d.shape[1]
    to, tn, tk = _tile(Ko, MM_TILE_NARROW), _tile(N, MM_TILE if d.dtype == BF16 else MM_TILE_NARROW), T
    grid = (Ko // to, N // tn, T // tk)

    def epilogue(in_refs, out_refs, accs):
        out_refs[0][...] = accs[0].astype(BF16)

    return _matmul(name, grid, [a, d],
                   [pl.BlockSpec((tk, to), lambda i, j, k: (k, i)),
                    pl.BlockSpec((tk, tn), lambda i, j, k: (k, j))],
                   [(0, 1, 0)], TN, [jax.ShapeDtypeStruct((Ko, N), BF16)],
                   [pl.BlockSpec((to, tn), lambda i, j, k: (i, j))], [(to, tn)], epilogue)[0]


def _mm_tn_pair(name, a, d3):
    T, Ko = a.shape
    N = d3.shape[2]
    to, tn, tk = _tile(Ko, MM_TILE), _tile(N, MM_TILE_NARROW), T
    njh = N // tn
    grid = (Ko // to, 2 * njh, T // tk)

    def epilogue(in_refs, out_refs, accs):
        out_refs[0][...] = accs[0].astype(BF16)

    return _matmul(name, grid, [a, d3],
                   [pl.BlockSpec((tk, to), lambda i, j, k: (k, i)),
                    pl.BlockSpec((None, tk, tn), lambda i, j, k: (j // njh, k, j % njh))],
                   [(0, 1, 0)], TN, [jax.ShapeDtypeStruct((Ko, 2 * N), BF16)],
                   [pl.BlockSpec((to, tn), lambda i, j, k: (i, j))], [(to, tn)], epilogue)[0]


def _rms_fwd(name, x, gains, dtypes, dep=None):
    T, D = x.shape
    n = len(gains)
    tr = _tile(T, 512, 8)
    n_dep = 0 if dep is None else 1

    def body(x_ref, *refs):
        xv = x_ref[...]
        xr = xv * lax.rsqrt(jnp.mean(xv * xv, axis=-1, keepdims=True) + EPS)
        for g_ref, o_ref in zip(refs[:n], refs[n + n_dep:]):
            o_ref[...] = (xr * g_ref[...]).astype(o_ref.dtype)

    row = pl.BlockSpec((tr, D), lambda i: (i, 0))
    vec = pl.BlockSpec((1, D), lambda i: (0, 0))
    ins, in_specs = [x, *gains], [row] + [vec] * n
    _dep_operand(ins, in_specs, dep)
    return pl.pallas_call(
        body, grid=(T // tr,), in_specs=in_specs, out_specs=[row] * n,
        out_shape=[jax.ShapeDtypeStruct((T, D), dt) for dt in dtypes],
        compiler_params=_params(("parallel",)), name=name)(*ins)


def _rms_bwd(name, x, dres, gains, dhs, dep=None):
    T, D = x.shape
    n = len(gains)
    tr = _tile(T, 256, 8)
    n_dep = 0 if dep is None else 1

    def body(x_ref, dres_ref, *refs):
        g_refs, dh_refs = refs[:n], refs[n:2 * n]
        dx_ref, dxb_ref = refs[2 * n + n_dep], refs[2 * n + n_dep + 1]
        dg_refs = refs[2 * n + n_dep + 2:]
        xv = x_ref[...]
        r = lax.rsqrt(jnp.mean(xv * xv, axis=-1, keepdims=True) + EPS)
        xr = xv * r
        w = None
        for g_ref, dh_ref, dg_ref in zip(g_refs, dh_refs, dg_refs):
            dh = dh_ref[...].astype(F32)

            @pl.when(pl.program_id(0) == 0)
            def _():
                dg_ref[...] = jnp.zeros_like(dg_ref)

            dg_ref[...] += jnp.sum(dh * xr, axis=0, keepdims=True)
            wi = dh * g_ref[...]
            w = wi if w is None else w + wi
        dx = dres_ref[...] + r * (w - xr * jnp.mean(w * xr, axis=-1, keepdims=True))
        dx_ref[...] = dx
        dxb_ref[...] = dx.astype(BF16)

    row = pl.BlockSpec((tr, D), lambda i: (i, 0))
    vec = pl.BlockSpec((1, D), lambda i: (0, 0))
    ins, in_specs = [x, dres, *gains, *dhs], [row, row] + [vec] * n + [row] * n
    _dep_operand(ins, in_specs, dep)
    outs = pl.pallas_call(
        body, grid=(T // tr,), in_specs=in_specs,
        out_specs=[row, row] + [vec] * n,
        out_shape=[jax.ShapeDtypeStruct((T, D), F32), jax.ShapeDtypeStruct((T, D), BF16)]
        + [jax.ShapeDtypeStruct((1, D), F32)] * n,
        compiler_params=_params(("arbitrary",)), name=name)(*ins)
    return outs[0], outs[1], outs[2:]


def _loss_head(x, gain, target):
    T, D = x.shape
    tr = _tile(T, 256, 8)

    def body(x_ref, g_ref, t_ref, loss_ref, dx_ref, dxb_ref, dg_ref):
        @pl.when(pl.program_id(0) == 0)
        def _():
            loss_ref[...] = jnp.zeros_like(loss_ref)
            dg_ref[...] = jnp.zeros_like(dg_ref)

        xv = x_ref[...]
        r = lax.rsqrt(jnp.mean(xv * xv, axis=-1, keepdims=True) + EPS)
        xr = xv * r
        err = xr * g_ref[...] - t_ref[...]
        part = jnp.sum(jnp.sum(err * err, axis=-1, keepdims=True), axis=0, keepdims=True) * (0.5 / D)
        loss_ref[...] += jnp.broadcast_to(part, loss_ref.shape)
        dy = err * (1.0 / D)
        dg_ref[...] += jnp.sum(dy * xr, axis=0, keepdims=True)
        w = dy * g_ref[...]
        dx = r * (w - xr * jnp.mean(w * xr, axis=-1, keepdims=True))
        dx_ref[...] = dx
        dxb_ref[...] = dx.astype(BF16)

    row = pl.BlockSpec((tr, D), lambda i: (i, 0))
    vec = pl.BlockSpec((1, D), lambda i: (0, 0))
    return pl.pallas_call(
        body, grid=(T // tr,), in_specs=[row, vec, row],
        out_specs=[pl.BlockSpec((8, 128), lambda i: (0, 0)), row, row, vec],
        out_shape=[jax.ShapeDtypeStruct((8, 128), F32), jax.ShapeDtypeStruct((T, D), F32),
                   jax.ShapeDtypeStruct((T, D), BF16), jax.ShapeDtypeStruct((1, D), F32)],
        compiler_params=_params(("arbitrary",)), name="loss_head")(x, gain, target)


def _glu_bwd(dmix, vg, dep=None):
    T, N = dmix.shape
    tr, tc = _tile(T, 512, 8), _tile(N, 1024)
    n_dep = 0 if dep is None else 1

    def body(d_ref, vg_ref, *refs):
        o_ref = refs[n_dep]
        d = d_ref[...]
        o_ref[0] = (d * vg_ref[0].astype(F32)).astype(BF16)
        o_ref[1] = (d * vg_ref[1].astype(F32)).astype(BF16)

    pair = pl.BlockSpec((2, tr, tc), lambda i, j: (0, i, j))
    ins, in_specs = [dmix, vg], [pl.BlockSpec((tr, tc), lambda i, j: (i, j)), pair]
    _dep_operand(ins, in_specs, dep)
    return pl.pallas_call(
        body, grid=(T // tr, N // tc), in_specs=in_specs,
        out_specs=pair, out_shape=jax.ShapeDtypeStruct((2, T, N), BF16),
        compiler_params=_params(("parallel", "parallel")), name="glu_bwd")(*ins)


def _to_state_tiles(x_ref, s, val):
    tc = val.shape[0]
    for j in range(S5_W // 128):
        x_ref[s, pl.ds(j, tc, stride=8), :] = val[:, 128 * j:128 * (j + 1)]


def _from_state_tiles(x_ref, s, tc):
    return jnp.concatenate([x_ref[s, pl.ds(j, tc, stride=8), :] for j in range(S5_W // 128)], axis=1)


def _s5_scan_fwd(xr_ref, xi_ref, ar_ref, ai_ref, cr_ref, ci_ref, tc, nblk):
    a = [(ar_ref[s], ai_ref[s]) for s in range(nblk)]

    def step(i, carry):
        carry = list(carry)
        for uu in range(S5_UNROLL):
            r0 = pl.multiple_of((i * S5_UNROLL + uu) * 8, 8)
            for s in range(nblk):
                cr, ci = carry[2 * s], carry[2 * s + 1]
                a_r, a_i = a[s]
                xr = a_r * cr - a_i * ci + xr_ref[s, pl.ds(r0, 8), :]
                xi = a_r * ci + a_i * cr + xi_ref[s, pl.ds(r0, 8), :]
                xr_ref[s, pl.ds(r0, 8), :] = xr
                xi_ref[s, pl.ds(r0, 8), :] = xi
                carry[2 * s], carry[2 * s + 1] = xr, xi
        return tuple(carry)

    init = []
    for s in range(nblk):
        init += [cr_ref[s], ci_ref[s]]
    out = lax.fori_loop(0, tc // S5_UNROLL, step, tuple(init))
    for s in range(nblk):
        cr_ref[s] = out[2 * s]
        ci_ref[s] = out[2 * s + 1]


def _s5_scan_bwd(lr_ref, li_ref, xr_ref, xi_ref, h_ref, ar_ref, ai_ref, cr_ref, ci_ref,
                 accr_ref, acci_ref, tc, nblk):
    a = [(ar_ref[s], ai_ref[s]) for s in range(nblk)]

    def one(s, r0, prev_r, prev_i, st):
        c_r, c_i, d_r, d_i = st
        a_r, a_i = a[s]
        l_r = lr_ref[s, pl.ds(r0, 8), :] + a_r * c_r + a_i * c_i
        l_i = li_ref[s, pl.ds(r0, 8), :] + a_r * c_i - a_i * c_r
        lr_ref[s, pl.ds(r0, 8), :] = l_r
        li_ref[s, pl.ds(r0, 8), :] = l_i
        return [l_r, l_i, d_r + l_r * prev_r + l_i * prev_i, d_i - l_r * prev_i + l_i * prev_r]

    def step(i, carry):
        carry = list(carry)
        for uu in range(S5_UNROLL):
            t = tc - 1 - (i * S5_UNROLL + uu)
            r0 = pl.multiple_of(t * 8, 8)
            p0 = pl.multiple_of((t - 1) * 8, 8)
            for s in range(nblk):
                carry[4 * s:4 * s + 4] = one(s, r0, xr_ref[s, pl.ds(p0, 8), :], xi_ref[s, pl.ds(p0, 8), :],
                                             carry[4 * s:4 * s + 4])
        return tuple(carry)

    init = []
    for s in range(nblk):
        init += [cr_ref[s], ci_ref[s], accr_ref[s], acci_ref[s]]
    carry = list(lax.fori_loop(0, tc // S5_UNROLL - 1, step, tuple(init)))
    for t in range(S5_UNROLL - 1, -1, -1):
        for s in range(nblk):
            if t > 0:
                prev_r, prev_i = xr_ref[s, 8 * (t - 1):8 * t, :], xi_ref[s, 8 * (t - 1):8 * t, :]
            else:
                prev_r, prev_i = h_ref[0, s], h_ref[1, s]
            carry[4 * s:4 * s + 4] = one(s, 8 * t, prev_r, prev_i, carry[4 * s:4 * s + 4])
    for s in range(nblk):
        cr_ref[s], ci_ref[s], accr_ref[s], acci_ref[s] = carry[4 * s:4 * s + 4]


def _gelu_and_grad(y):
    k = math.sqrt(2.0 / math.pi)
    t = jnp.tanh(k * (y + 0.044715 * (y * y * y)))
    return (0.5 * y * (1.0 + t),
            0.5 * (1.0 + t) + 0.5 * y * (1.0 - t * t) * (k * (1.0 + 3.0 * 0.044715 * (y * y))))


def _s5_specs(tc, nch, sbk, rev):
    def ch(c):
        return nch - 1 - c if rev else c

    return dict(
        act=pl.BlockSpec((tc, sbk * S5_CH), lambda i, c: (ch(c), i)),
        bb=pl.BlockSpec((2, sbk, S5_CH, 128), lambda i, c: (0, i, 0, 0)),
        cc=pl.BlockSpec((2, sbk, S5_P, S5_CH), lambda i, c: (0, i, 0, 0)),
        a=pl.BlockSpec((sbk, 8, 128), lambda i, c: (i, 0, 0)),
        d=pl.BlockSpec((1, sbk * S5_CH), lambda i, c: (0, i)),
        h=pl.BlockSpec((None, 2, sbk, 8, 128), lambda i, c: (ch(c), 0, i, 0, 0)),
    )


def _s5_blocks(nb, pref):
    return max(b for b in range(1, pref + 1) if nb % b == 0)


def _s5_group_masks():
    rb = lax.broadcasted_iota(jnp.int32, (S5_CH, S5_W), 0) // S5_C
    qb = lax.broadcasted_iota(jnp.int32, (S5_CH, S5_W), 1) // S5_P
    qc = lax.broadcasted_iota(jnp.int32, (S5_W, S5_CH), 0) // S5_P
    rc = lax.broadcasted_iota(jnp.int32, (S5_W, S5_CH), 1) // S5_C
    return rb == qb, qc == rc


def _s5_expand(bb_ref, cc_ref, bbd, ccd, sbk):
    mask_b, mask_c = _s5_group_masks()
    for k in range(2):
        for s in range(sbk):
            bbd[k, s] = jnp.where(mask_b, jnp.tile(bb_ref[k, s], (1, S5_W // 128)), 0).astype(BF16)
            ccd[k, s] = jnp.where(mask_c, jnp.tile(cc_ref[k, s], (S5_GB, 1)), 0).astype(BF16)


def _s5_fwd(u, bb2, cc2, a_re, a_im, d_skip):
    T, D = u.shape
    nb = D // S5_CH
    sbk = _s5_blocks(nb, 4)
    tc = _tile(T, 512, 8)
    nch = T // tc
    sp = _s5_specs(tc, nch, sbk, False)

    def body(u_ref, bb_ref, cc_ref, ar_ref, ai_ref, d_ref, z_ref, gp_ref, h_ref, xr, xi, cr, ci, bbd, ccd):
        @pl.when(pl.program_id(1) == 0)
        def _():
            cr[...] = jnp.zeros_like(cr)
            ci[...] = jnp.zeros_like(ci)
            _s5_expand(bb_ref, cc_ref, bbd, ccd, sbk)

        h_ref[0] = cr[...]
        h_ref[1] = ci[...]
        for s in range(sbk):
            ub = u_ref[:, s * S5_CH:(s + 1) * S5_CH].astype(BF16)
            _to_state_tiles(xr, s, _dot(ub, bbd[0, s]))
            _to_state_tiles(xi, s, _dot(ub, bbd[1, s]))
        _s5_scan_fwd(xr, xi, ar_ref, ai_ref, cr, ci, tc, sbk)
        for s in range(sbk):
            cols = slice(s * S5_CH, (s + 1) * S5_CH)
            y = (_dot(_from_state_tiles(xr, s, tc).astype(BF16), ccd[0, s])
                 - _dot(_from_state_tiles(xi, s, tc).astype(BF16), ccd[1, s])
                 + d_ref[:, cols] * u_ref[:, cols])
            act, act_grad = _gelu_and_grad(y)
            z_ref[:, cols] = act.astype(BF16)
            gp_ref[:, cols] = act_grad.astype(BF16)

    tiles = pltpu.VMEM((sbk, tc * 8, 128), F32)
    carry = pltpu.VMEM((sbk, 8, 128), F32)
    return pl.pallas_call(
        body, grid=(nb // sbk, nch),
        in_specs=[sp["act"], sp["bb"], sp["cc"], sp["a"], sp["a"], sp["d"]],
        out_specs=[sp["act"], sp["act"], sp["h"]],
        out_shape=[jax.ShapeDtypeStruct((T, D), BF16), jax.ShapeDtypeStruct((T, D), BF16),
                   jax.ShapeDtypeStruct((nch, 2, nb, 8, 128), F32)],
        scratch_shapes=[tiles, tiles, carry, carry, pltpu.VMEM((2, sbk, S5_CH, S5_W), BF16),
                        pltpu.VMEM((2, sbk, S5_W, S5_CH), BF16)],
        compiler_params=_params(("parallel", "arbitrary")), name="s5_fwd",
    )(u, bb2, cc2, a_re, a_im, d_skip)


def _s5_bwd(u, dz, gp, h0, bb2, cc2, a_re, a_im, d_skip, dep=None):
    T, D = u.shape
    nb = D // S5_CH
    sbk = _s5_blocks(nb, 2)
    tc = _tile(T, 512, 8)
    nch = T // tc
    sp = _s5_specs(tc, nch, sbk, True)

    n_dep = 0 if dep is None else 1

    def body(u_ref, dz_ref, gp_ref, h_ref, bb_ref, cc_ref, ar_ref, ai_ref, d_ref, *rest):
        (du_ref, dd_ref, dar_ref, dai_ref, dbb_ref, dcc_ref,
         xr, xi, lr, li, fr, fi, br, bi, accr, acci, bbd, ccd, dbbd, dccd) = rest[n_dep:]
        c = pl.program_id(1)

        @pl.when(c == 0)
        def _():
            for ref in (br, bi, accr, acci, dd_ref, dbbd, dccd):
                ref[...] = jnp.zeros_like(ref)
            _s5_expand(bb_ref, cc_ref, bbd, ccd, sbk)

        for s in range(sbk):
            ub = u_ref[:, s * S5_CH:(s + 1) * S5_CH].astype(BF16)
            _to_state_tiles(xr, s, _dot(ub, bbd[0, s]))
            _to_state_tiles(xi, s, _dot(ub, bbd[1, s]))
        fr[...] = h_ref[0]
        fi[...] = h_ref[1]
        _s5_scan_fwd(xr, xi, ar_ref, ai_ref, fr, fi, tc, sbk)
        for s in range(sbk):
            cols = slice(s * S5_CH, (s + 1) * S5_CH)
            uv = u_ref[:, cols]
            xrb = _from_state_tiles(xr, s, tc).astype(BF16)
            xib = _from_state_tiles(xi, s, tc).astype(BF16)
            dsk = d_ref[:, cols]
            dy = dz_ref[:, cols].astype(F32) * gp_ref[:, cols].astype(F32)
            dd_ref[:, cols] += jnp.sum(dy * uv, axis=0, keepdims=True)
            dyb = dy.astype(BF16)
            dccd[0, s] += _dot(xrb, dyb, TN)
            dccd[1, s] += _dot(xib, dyb, TN)
            _to_state_tiles(lr, s, _dot(dyb, ccd[0, s], NT))
            _to_state_tiles(li, s, -_dot(dyb, ccd[1, s], NT))
            du_ref[:, cols] = dy * dsk
        _s5_scan_bwd(lr, li, xr, xi, h_ref, ar_ref, ai_ref, br, bi, accr, acci, tc, sbk)
        for s in range(sbk):
            cols = slice(s * S5_CH, (s + 1) * S5_CH)
            ub = u_ref[:, cols].astype(BF16)
            lrb = _from_state_tiles(lr, s, tc).astype(BF16)
            lib = _from_state_tiles(li, s, tc).astype(BF16)
            dbbd[0, s] += _dot(ub, lrb, TN)
            dbbd[1, s] += _dot(ub, lib, TN)
            du_ref[:, cols] += _dot(lrb, bbd[0, s], NT) + _dot(lib, bbd[1, s], NT)

        @pl.when(c == nch - 1)
        def _():
            dar_ref[...] = accr[...]
            dai_ref[...] = acci[...]
            mask_b, mask_c = _s5_group_masks()
            for k in range(2):
                for s in range(sbk):
                    mb = jnp.where(mask_b, dbbd[k, s], 0.0)
                    fold = functools.reduce(
                        lambda a, b: a + b, [mb[:, 128 * j:128 * (j + 1)] for j in range(S5_W // 128)])
                    dbb_ref[k, s] = fold + pltpu.roll(fold, S5_P, 1)
                    mc = jnp.where(mask_c, dccd[k, s], 0.0)
                    dcc_ref[k, s] = functools.reduce(
                        lambda a, b: a + b, [mc[S5_P * j:S5_P * (j + 1), :] for j in range(S5_GB)])

    tiles = pltpu.VMEM((sbk, tc * 8, 128), F32)
    carry = pltpu.VMEM((sbk, 8, 128), F32)
    ins = [u, dz, gp, h0, bb2, cc2, a_re, a_im, d_skip]
    in_specs = [sp["act"], sp["act"], sp["act"], sp["h"], sp["bb"], sp["cc"], sp["a"], sp["a"], sp["d"]]
    _dep_operand(ins, in_specs, dep)
    return pl.pallas_call(
        body, grid=(nb // sbk, nch),
        in_specs=in_specs,
        out_specs=[sp["act"], sp["d"], sp["a"], sp["a"], sp["bb"], sp["cc"]],
        out_shape=[jax.ShapeDtypeStruct((T, D), F32), jax.ShapeDtypeStruct((1, D), F32),
                   jax.ShapeDtypeStruct((nb, 8, 128), F32), jax.ShapeDtypeStruct((nb, 8, 128), F32),
                   jax.ShapeDtypeStruct((2, nb, S5_CH, 128), F32), jax.ShapeDtypeStruct((2, nb, S5_P, S5_CH), F32)],
        scratch_shapes=[tiles, tiles, tiles, tiles, carry, carry, carry, carry, carry, carry,
                        pltpu.VMEM((2, sbk, S5_CH, S5_W), BF16), pltpu.VMEM((2, sbk, S5_W, S5_CH), BF16),
                        pltpu.VMEM((2, sbk, S5_CH, S5_W), F32), pltpu.VMEM((2, sbk, S5_W, S5_CH), F32)],
        compiler_params=_params(("parallel", "arbitrary")), name="s5_bwd",
    )(*ins)


def _s5_disc(lr, li, ldt):
    dt = jnp.exp(ldt)
    mag = jnp.exp(lr * dt)
    ang = li * dt
    cs, sn = jnp.cos(ang), jnp.sin(ang)
    lbr, lbi = mag * cs, mag * sn
    nr = lbr - 1.0
    den = lr * lr + li * li
    f_re = (nr * lr + lbi * li) / den
    f_im = (lbi * lr - nr * li) / den
    return dt, mag, cs, sn, lbr, lbi, nr, den, f_re, f_im


def _s5_param_fwd(lr, li, ldt, bt_re, bt_im):
    c, g, p = bt_re.shape

    def body(lr_ref, li_ref, ldt_ref, br_ref, bi_ref, lbr_ref, lbi_ref, bbr_ref, bbi_ref):
        _, _, _, _, lbr, lbi, _, _, f_re, f_im = _s5_disc(lr_ref[...], li_ref[...], ldt_ref[...])
        lbr_ref[...] = lbr
        lbi_ref[...] = lbi
        for ch in range(c):
            b_r, b_i = br_ref[ch], bi_ref[ch]
            bbr_ref[ch] = f_re * b_r - f_im * b_i
            bbi_ref[ch] = f_re * b_i + f_im * b_r

    gp = jax.ShapeDtypeStruct((g, p), F32)
    cgp = jax.ShapeDtypeStruct((c, g, p), F32)
    return pl.pallas_call(body, out_shape=[gp, gp, cgp, cgp], name="s5_param_fwd")(lr, li, ldt, bt_re, bt_im)


def _s5_param_bwd(lr, li, ldt, bt_re, bt_im, dlbr, dlbi, dbbr, dbbi):
    c, g, p = bt_re.shape

    def body(lr_ref, li_ref, ldt_ref, br_ref, bi_ref, dlbr_ref, dlbi_ref, dbbr_ref, dbbi_ref,
             dlr_ref, dli_ref, dldt_ref, dbr_ref, dbi_ref):
        l_r, l_i = lr_ref[...], li_ref[...]
        dt, mag, cs, sn, lbr, lbi, nr, den, f_re, f_im = _s5_disc(l_r, l_i, ldt_ref[...])
        dfr = jnp.zeros_like(l_r)
        dfi = jnp.zeros_like(l_r)
        for ch in range(c):
            b_r, b_i = br_ref[ch], bi_ref[ch]
            g_r, g_i = dbbr_ref[ch], dbbi_ref[ch]
            dbr_ref[ch] = f_re * g_r + f_im * g_i
            dbi_ref[ch] = f_re * g_i - f_im * g_r
            dfr = dfr + g_r * b_r + g_i * b_i
            dfi = dfi + g_i * b_r - g_r * b_i
        inv = 1.0 / den
        d_nr = (dfr * l_r - dfi * l_i) * inv
        d_lbi = (dfr * l_i + dfi * l_r) * inv + dlbi_ref[...]
        d_lbr = d_nr + dlbr_ref[...]
        d_den = -(dfr * f_re + dfi * f_im) * inv
        d_mag = d_lbr * cs + d_lbi * sn
        d_ang = d_lbi * lbr - d_lbr * lbi
        dlr_ref[...] = (dfr * nr + dfi * lbi) * inv + 2.0 * d_den * l_r + d_mag * mag * dt
        dli_ref[...] = (dfr * lbi - dfi * nr) * inv + 2.0 * d_den * l_i + d_ang * dt
        dldt_ref[...] = jnp.sum(d_mag * mag * l_r + d_ang * l_i, axis=1, keepdims=True) * dt

    gp = jax.ShapeDtypeStruct((g, p), F32)
    cgp = jax.ShapeDtypeStruct((c, g, p), F32)
    return pl.pallas_call(body, out_shape=[gp, gp, jax.ShapeDtypeStruct((g, 1), F32), cgp, cgp],
                          name="s5_param_bwd")(lr, li, ldt, bt_re, bt_im, dlbr, dlbi, dbbr, dbbi)


def _att_masks(rep, gb):
    rows = rep * ATT_BLK
    qi = lax.broadcasted_iota(jnp.int32, (rows, 2 * ATT_BLK), 0) % ATT_BLK
    si = lax.broadcasted_iota(jnp.int32, (rows, 2 * ATT_BLK), 1)
    prev = (si < ATT_BLK) & (si >= qi) & (gb > 0)
    cur = (si >= ATT_BLK) & (si - ATT_BLK <= qi)
    return prev | cur


def _att_rows(start, dil):
    return pl.ds(start, ATT_BLK) if dil == 1 else pl.ds(start, ATT_BLK, stride=dil)


def _att_plan(T, dil):
    span = ATT_BLK * dil
    sbr = max(span, min(T, 1024))
    return span, sbr, T // sbr


def _att_block(sb, i, sbr, span, dil):
    loc = (i // dil) * span + i % dil
    cur = sb * sbr + loc
    gb = sb * (sbr // span) + i // dil
    return loc, cur, jnp.where(gb > 0, cur - span, cur), gb


def _att_fwd(q, k, v, grp, dil):
    T = q.shape[0]
    H = q.shape[1] // HEAD_DIM // len(DILATIONS)
    rep = H // N_KV_HEADS
    span, sbr, nsb = _att_plan(T, dil)
    scale = HEAD_DIM ** -0.5

    def body(*refs):
        q_refs = refs[:rep]
        k_ref, v_ref, o_ref, l_ref, o_slab, l_slab = refs[rep:]
        sb = pl.program_id(1)

        def blk(i, _):
            loc, cur, prv, gb = _att_block(sb, i, sbr, span, dil)
            rows = _att_rows(loc, dil)
            qs = jnp.concatenate([r[rows, :] for r in q_refs], axis=0).astype(BF16)
            kcat = jnp.concatenate([k_ref[_att_rows(prv, dil), :], k_ref[_att_rows(cur, dil), :]], axis=0)
            vcat = jnp.concatenate([v_ref[_att_rows(prv, dil), :], v_ref[_att_rows(cur, dil), :]], axis=0)
            s = jnp.where(_att_masks(rep, gb), _dot(qs, kcat.astype(BF16), NT) * scale, NEG_INF)
            m = jnp.max(s, axis=-1, keepdims=True)
            p = jnp.exp(s - m)
            l = jnp.sum(p, axis=-1, keepdims=True)
            o = _dot(p.astype(BF16), vcat.astype(BF16)) / l
            lse = jnp.broadcast_to(m + jnp.log(l), (rep * ATT_BLK, HEAD_DIM))
            for j in range(rep):
                o_slab[j, rows, :] = o[j * ATT_BLK:(j + 1) * ATT_BLK]
                l_slab[j, rows, :] = lse[j * ATT_BLK:(j + 1) * ATT_BLK]
            return 0

        lax.fori_loop(0, sbr // ATT_BLK, blk, 0)
        for j in range(rep):
            o_ref[:, j * HEAD_DIM:(j + 1) * HEAD_DIM] = o_slab[j].astype(BF16)
            l_ref[:, j * HEAD_DIM:(j + 1) * HEAD_DIM] = l_slab[j]

    qspecs = [pl.BlockSpec((sbr, HEAD_DIM), functools.partial(lambda h, s, j: (s, grp * H + h * rep + j), j=j))
              for j in range(rep)]
    kspec = pl.BlockSpec((T, HEAD_DIM), lambda h, s: (0, h))
    ospec = pl.BlockSpec((sbr, rep * HEAD_DIM), lambda h, s: (s, h))
    slab = pltpu.VMEM((rep, sbr, HEAD_DIM), F32)
    return pl.pallas_call(
        body, grid=(N_KV_HEADS, nsb), in_specs=qspecs + [kspec, kspec], out_specs=[ospec, ospec],
        out_shape=[jax.ShapeDtypeStruct((T, H * HEAD_DIM), BF16), jax.ShapeDtypeStruct((T, H * HEAD_DIM), F32)],
        scratch_shapes=[slab, slab],
        compiler_params=_params(("parallel", "arbitrary")), name=f"att_fwd_d{dil}",
    )(*([q] * rep), k, v)


def _att_combine(outs, lses):
    T, W = outs[0].shape
    ng = len(outs)
    tr, tcol = _tile(T, 512, 8), _tile(W, 512)

    def body(*refs):
        o_refs, l_refs = refs[:ng], refs[ng:2 * ng]
        ob_ref, lse_ref = refs[2 * ng:]
        ls = [r[...] for r in l_refs]
        m = functools.reduce(jnp.maximum, ls)
        es = [jnp.exp(l - m) for l in ls]
        den = functools.reduce(lambda a, b: a + b, es)
        num = functools.reduce(lambda a, b: a + b, [e * o[...].astype(F32) for e, o in zip(es, o_refs)])
        ob_ref[...] = (num / den).astype(BF16)
        lse_ref[...] = m + jnp.log(den)

    spec = pl.BlockSpec((tr, tcol), lambda i, j: (i, j))
    return pl.pallas_call(
        body, grid=(T // tr, W // tcol), in_specs=[spec] * (2 * ng), out_specs=[spec, spec],
        out_shape=[jax.ShapeDtypeStruct((T, W), BF16), jax.ShapeDtypeStruct((T, W), F32)],
        compiler_params=_params(("parallel", "parallel")), name="att_combine")(*outs, *lses)


STAT_LANE = HEAD_DIM // 2


def _att_out_bwd(dx, w_o, o, lse, dep=None):
    T, D = dx.shape
    W = w_o.shape[0]
    tm, tn = _tile(T, MM_TILE), _tile(W, MM_TILE)
    n_dep = 0 if dep is None else 1

    def body(dx_ref, w_ref, o_ref, l_ref, *rest):
        do_ref, s_ref = rest[n_dep:]
        do_ref[...] = _dot(dx_ref[...], w_ref[...], NT)
        lane = lax.broadcasted_iota(jnp.int32, (tm, HEAD_DIM), 1)
        for h in range(tn // HEAD_DIM):
            cols = slice(h * HEAD_DIM, (h + 1) * HEAD_DIM)
            delta = jnp.sum(do_ref[:, cols] * o_ref[:, cols].astype(F32), axis=-1, keepdims=True)
            s_ref[:, cols] = jnp.where(lane < STAT_LANE, l_ref[:, cols], delta)

    tile = pl.BlockSpec((tm, tn), lambda i, j: (i, j))
    ins = [dx, w_o, o, lse]
    in_specs = [pl.BlockSpec((tm, D), lambda i, j: (i, 0)), pl.BlockSpec((tn, D), lambda i, j: (j, 0)), tile, tile]
    _dep_operand(ins, in_specs, dep)
    return pl.pallas_call(
        body, grid=(T // tm, W // tn), in_specs=in_specs, out_specs=[tile, tile],
        out_shape=[jax.ShapeDtypeStruct((T, W), F32)] * 2,
        compiler_params=_params(("parallel", "parallel")), name="o_proj_dx")(*ins)


def _att_bwd(q, k, v, do, stats, dq, grp, dil):
    T = q.shape[0]
    H = do.shape[1] // HEAD_DIM
    rep = H // N_KV_HEADS
    hs = rep
    span, sbr, nsb = _att_plan(T, dil)
    scale = HEAD_DIM ** -0.5

    def body(*refs):
        q_refs, do_refs, st_refs = refs[:hs], refs[hs + 2:2 * hs + 2], refs[2 * hs + 2:3 * hs + 2]
        k_ref, v_ref = refs[hs], refs[hs + 1]
        dq_ref, dk_ref, dv_ref, dq_slab = refs[3 * hs + 3:]
        sb = pl.program_id(2)

        @pl.when((pl.program_id(1) == 0) & (sb == 0))
        def _():
            dk_ref[...] = jnp.zeros_like(dk_ref)
            dv_ref[...] = jnp.zeros_like(dv_ref)

        def blk(i, _):
            loc, cur, prv, gb = _att_block(sb, i, sbr, span, dil)
            rows, kc, kp = _att_rows(loc, dil), _att_rows(cur, dil), _att_rows(prv, dil)
            qs = jnp.concatenate([r[rows, :] for r in q_refs], axis=0).astype(BF16)
            dos = jnp.concatenate([r[rows, :] for r in do_refs], axis=0).astype(BF16)
            st = jnp.concatenate([r[rows, :] for r in st_refs], axis=0)
            kcat = jnp.concatenate([k_ref[kp, :], k_ref[kc, :]], axis=0).astype(BF16)
            vcat = jnp.concatenate([v_ref[kp, :], v_ref[kc, :]], axis=0).astype(BF16)
            s = _dot(qs, kcat, NT) * scale
            p = jnp.where(_att_masks(hs, gb), jnp.exp(s - st[:, 0:1]), 0.0)
            dp = _dot(dos, vcat, NT)
            ds = (p * (dp - st[:, STAT_LANE:STAT_LANE + 1]) * scale).astype(BF16)
            dvc = _dot(p.astype(BF16), dos, TN)
            dkc = _dot(ds, qs, TN)
            dqs = _dot(ds, kcat)
            for j in range(hs):
                dq_slab[j, rows, :] = dqs[j * ATT_BLK:(j + 1) * ATT_BLK]
            dk_ref[kc, :] += dkc[ATT_BLK:]
            dv_ref[kc, :] += dvc[ATT_BLK:]

            @pl.when(gb > 0)
            def _():
                dk_ref[kp, :] += dkc[:ATT_BLK]
                dv_ref[kp, :] += dvc[:ATT_BLK]

            return 0

        lax.fori_loop(0, sbr // ATT_BLK, blk, 0)
        for j in range(hs):
            dq_ref[:, j * HEAD_DIM:(j + 1) * HEAD_DIM] = dq_slab[j].astype(BF16)

    def head_specs(col0):
        return [pl.BlockSpec((sbr, HEAD_DIM),
                             functools.partial(lambda h, f, s, j: (s, col0 + h * rep + f * hs + j), j=j))
                for j in range(hs)]

    kspec = pl.BlockSpec((T, HEAD_DIM), lambda h, f, s: (0, h))
    dqspec = pl.BlockSpec((sbr, hs * HEAD_DIM), lambda h, f, s: (s, (grp * H + h * rep) // hs + f))
    n_in = 3 * hs + 3
    return pl.pallas_call(
        body, grid=(N_KV_HEADS, rep // hs, nsb),
        in_specs=head_specs(grp * H) + [kspec, kspec] + head_specs(0) + head_specs(0) + [ANY],
        out_specs=[dqspec, kspec, kspec],
        out_shape=[jax.ShapeDtypeStruct(dq.shape, BF16),
                   jax.ShapeDtypeStruct((T, N_KV_HEADS * HEAD_DIM), F32),
                   jax.ShapeDtypeStruct((T, N_KV_HEADS * HEAD_DIM), F32)],
        scratch_shapes=[pltpu.VMEM((hs, sbr, HEAD_DIM), F32)],
        input_output_aliases={n_in - 1: 0},
        compiler_params=_params(("parallel", "arbitrary", "arbitrary")), name=f"att_bwd_d{dil}",
    )(*([q] * hs), k, v, *([do] * hs), *([stats] * hs), dq)


def _sum_kv(dks, dvs):
    T, W = dks[0].shape
    ng = len(dks)
    tr = _tile(T, 512, 8)

    def body(*refs):
        o_ref = refs[2 * ng]
        o_ref[0] = functools.reduce(lambda a, b: a + b, [r[...] for r in refs[:ng]]).astype(BF16)
        o_ref[1] = functools.reduce(lambda a, b: a + b, [r[...] for r in refs[ng:2 * ng]]).astype(BF16)

    spec = pl.BlockSpec((tr, W), lambda i: (i, 0))
    return pl.pallas_call(
        body, grid=(T // tr,), in_specs=[spec] * (2 * ng),
        out_specs=pl.BlockSpec((2, tr, W), lambda i: (0, i, 0)),
        out_shape=jax.ShapeDtypeStruct((2, T, W), BF16),
        compiler_params=_params(("parallel",)), name="sum_kv")(*dks, *dvs)


def _local_step(x, tgt, w, wts, ready, dep0=None, small_ready=None):
    T, D = x.shape
    g = {}

    (u0,) = _rms_fwd("rms_a", x, [w["a_norm"]], [F32], dep=dep0)
    lbr, lbi, bbt_re, bbt_im = _s5_param_fwd(w["lam_re"], w["lam_im"], w["log_dt"], w["bt_re"], w["bt_im"])
    a_re, a_im = lbr.reshape(-1, 8, 128), lbi.reshape(-1, 8, 128)
    C, G, P = w["bt_re"].shape
    nb = G // S5_GB
    bb2 = jnp.stack([bbt_re, bbt_im]).transpose(0, 2, 1, 3).reshape(2, nb, S5_GB * C, P)
    bb2 = jnp.concatenate([bb2, bb2], axis=-1)
    cc2 = jnp.stack([w["c_re"], w["c_im"]]).reshape(2, nb, S5_GB * C, P).transpose(0, 1, 3, 2)
    z, gp, h0 = _s5_fwd(u0, bb2, cc2, a_re, a_im, w["s5_d"])
    w_glu = wts("glu", z)
    x1, vg = _mm_dual_fwd("glu_fwd", z, w_glu, x, "glu")

    def ffn_fwd(xin, layer):
        (nrm,) = _rms_fwd(f"rms_f{layer}", xin, [w["ffn_norm"][layer:layer + 1]], [BF16])
        w_in = wts(f"win{layer}", nrm)
        gu, act = _mm_dual_fwd(f"ffn_in{layer}", nrm, w_in, None, "ffn")
        w_out = wts(f"wout{layer}", act)
        xout = _mm_nn(f"ffn_out{layer}", act, w_out, res=xin)
        return xout, (nrm, gu, act, w_in, w_out)

    x2, saved0 = ffn_fwd(x1, 0)
    kvn, hb = _rms_fwd("rms_b", x2, [w["kv_norm"], w["b_norm"]], [BF16, BF16])
    w_kv, w_q, w_o = wts("wkv", hb), wts("wq", hb), wts("wo", hb)
    k, v = _mm_kv("kv_proj", kvn, w_kv)
    q = _mm_nn("q_proj", hb, w_q)
    outs, lses = [], []
    for grp, dil in enumerate(DILATIONS):
        o_g, l_g = _att_fwd(q, k, v, grp, dil)
        outs.append(o_g)
        lses.append(l_g)
    o, lse = _att_combine(outs, lses)
    wts("win1", o, prefetch=True)
    x3 = _mm_nn("o_proj", o, w_o, res=x2)
    x4, saved1 = ffn_fwd(x3, 1)
    loss_blk, dx4, dx4b, g["final_norm"] = _loss_head(x4, w["final_norm"], tgt)

    def ffn_bwd(dx, dxb, xin, saved, layer, dep):
        nrm, gu, act, w_in, w_out = saved
        dgu = _mm_nt_ffn_bwd(f"ffn_dact{layer}", dxb, w_out, gu, dep=dep)
        g_wout = _mm_tn(f"ffn_dwout{layer}", act, dxb)
        g_win = _mm_tn_pair(f"ffn_dwin{layer}", nrm, dgu)
        dn = _mm_nt_pair(f"ffn_dn{layer}", dgu, w_in, out_dtype=BF16)
        dxo, dxob, (dgn,) = _rms_bwd(f"rms_f{layer}_bwd", xin, dx, [w["ffn_norm"][layer:layer + 1]], [dn])
        tok = ready({f"win{layer}": g_win, f"wout{layer}": g_wout})
        return dxo, dxob, dgn, tok

    dx3, dx3b, dfn1, tok = ffn_bwd(dx4, dx4b, x3, saved1, 1, None)
    do, stats = _att_out_bwd(dx3b, w_o, o, lse, dep=tok)
    g_wo = _mm_tn("o_proj_dw", o, dx3b)
    dq = lax.empty(q.shape, BF16)
    dks, dvs = [], []
    for grp, dil in enumerate(DILATIONS):
        dq, dk_g, dv_g = _att_bwd(q, k, v, do, stats, dq, grp, dil)
        dks.append(dk_g)
        dvs.append(dv_g)
    dkv = _sum_kv(dks, dvs)
    dhb = _mm_nt("q_proj_dx", [dq], w_q, out_dtype=BF16)
    g_wq = _mm_tn("q_proj_dw", hb, dq)
    dkvn = _mm_nt_pair("kv_proj_dx", dkv, w_kv, out_dtype=BF16)
    g_wkv = _mm_tn_pair("kv_proj_dw", kvn, dkv)
    dx2, dx2b, (g["kv_norm"], g["b_norm"]) = _rms_bwd(
        "rms_b_bwd", x2, dx3, [w["kv_norm"], w["b_norm"]], [dkvn, dhb])
    tok = ready({"wkv": g_wkv, "wq": g_wq, "wo": g_wo})
    dx1, dx1b, dfn0, tok = ffn_bwd(dx2, dx2b, x1, saved0, 0, tok)
    g["ffn_norm"] = jnp.concatenate([dfn0, dfn1], axis=0)

    dvg = _glu_bwd(dx1, vg, dep=tok)
    dz = _mm_nt_pair("glu_dx", dvg, w_glu, out_dtype=BF16)
    tok = ready({"glu": _mm_tn_pair("glu_dw", z, dvg)})
    tok = ready({}, after=tok)
    du, g["s5_d"], da_re, da_im, dbb2, dcc2 = _s5_bwd(
        u0, dz, gp, h0, bb2, cc2, a_re, a_im, w["s5_d"], dep=tok)
    dcc = dcc2.transpose(0, 1, 3, 2).reshape(2, G, C, P)
    g["c_re"], g["c_im"] = dcc[0], -dcc[1]
    dbbt = dbb2[..., :P].reshape(2, G, C, P).transpose(0, 2, 1, 3)
    g["lam_re"], g["lam_im"], g["log_dt"], g["bt_re"], g["bt_im"] = _s5_param_bwd(
        w["lam_re"], w["lam_im"], w["log_dt"], w["bt_re"], w["bt_im"],
        da_re.reshape(G, P), da_im.reshape(G, P), dbbt[0], dbbt[1])
    tok = small_ready(g) if small_ready is not None else None
    grad_x, _, (g["a_norm"],) = _rms_bwd("rms_a_bwd", x, dx1, [w["a_norm"]], [du], dep=tok)
    return loss_blk, grad_x, g


def _coords():
    return lax.axis_index("x"), lax.axis_index("y"), lax.axis_index("c")


def _dev_index(dev):
    return 4 * dev[0] + 2 * dev[1] + dev[2]


def _shard_window(ref, axis, width, idx):
    sl = [slice(None)] * len(ref.shape)
    sl[axis] = pl.ds(pl.multiple_of(idx * width, width), width)
    return ref.at[tuple(sl)]


def _all_gather(name, shards, axes):
    na = len(shards)
    widths = [s.shape[ax] for s, ax in zip(shards, axes)]
    out_shapes = []
    for s, ax in zip(shards, axes):
        shp = list(s.shape)
        shp[ax] *= N_DEV
        out_shapes.append(jax.ShapeDtypeStruct(tuple(shp), s.dtype))

    def body(*refs):
        ins, outs = refs[:na], refs[na:2 * na]
        send_sems, recv_sems, local_sems = refs[2 * na:]
        x, y, c = _coords()
        me, sib = (x, y, c), (x, y, 1 - c)
        chips = [(1 - x, y), (x, 1 - y), (1 - x, 1 - y)]

        def blk(a, dev):
            return _shard_window(outs[a], axes[a], widths[a], _dev_index(dev))

        def copy(a, kk, block, to, src=None):
            return pltpu.make_async_remote_copy(
                src_ref=blk(a, block) if src is None else src, dst_ref=blk(a, block),
                send_sem=send_sems.at[a, kk], recv_sem=recv_sems.at[a, kk],
                device_id=to, device_id_type=MESH)

        local = [pltpu.make_async_copy(ins[a], blk(a, me), local_sems.at[a]) for a in range(na)]
        for cp in local:
            cp.start()
        sent = []
        for a in range(na):
            first = [copy(a, 0, me, sib, src=ins[a])]
            first += [copy(a, 1 + j, me, (*chip, c), src=ins[a]) for j, chip in enumerate(chips)]
            for cp in first:
                cp.start()
            sent += first
        for a in range(na):
            for j, chip in enumerate(chips):
                copy(a, 1 + j, (*chip, c), me).wait_recv()
                fwd = copy(a, 4 + j, (*chip, c), sib)
                fwd.start()
                sent.append(fwd)
        for a in range(na):
            copy(a, 0, sib, me).wait_recv()
            for j, chip in enumerate(chips):
                copy(a, 4 + j, (*chip, 1 - c), me).wait_recv()
        for cp in sent:
            cp.wait_send()
        for cp in local:
            cp.wait()

    return pl.pallas_call(
        body, out_shape=out_shapes, in_specs=[ANY] * na, out_specs=[ANY] * na,
        scratch_shapes=[pltpu.SemaphoreType.DMA((na, 7)), pltpu.SemaphoreType.DMA((na, 7)),
                        pltpu.SemaphoreType.DMA((na,))],
        name=name)(*shards)


HBM = pl.BlockSpec(memory_space=pltpu.HBM)
SEM = pl.BlockSpec(memory_space=pltpu.SEMAPHORE)
TOKEN_SPEC = pl.BlockSpec(memory_space=pltpu.VMEM)
TOKEN_SHAPE = jax.ShapeDtypeStruct((8, 128), F32)
SPLIT_PARAMS = pltpu.CompilerParams(has_side_effects=pltpu.SideEffectType.DATAFLOW_SIDE_EFFECTING)


def _hbm(x):
    return pltpu.with_memory_space_constraint(x, pltpu.HBM)


def _hbm_like(x):
    return pltpu.HBM(x.shape, x.dtype)


def _dma_sems(*shape):
    return pltpu.SemaphoreType.DMA(shape)


def _cast_and_place(name, shard, layer, axis, pos, dtype):
    rows, cols = shard.shape[-2:]
    tr = _tile(rows, 256, 16)
    nt = rows // tr
    full = (rows, cols * N_DEV) if axis == 1 else (rows * N_DEV, cols)

    def dev(p):
        return 4 * p[0] + 2 * p[1] + p[2]

    def body(pos_ref, s_ref, b_ref, l_ref):
        v = s_ref[...].astype(dtype)
        b_ref[...] = v
        l_ref[...] = v

    blk = pl.BlockSpec((tr, cols), lambda i, p: (i, 0))
    if axis == 1:
        lspec = pl.BlockSpec((tr, cols), lambda i, p: (i, dev(p)))
    else:
        lspec = pl.BlockSpec((tr, cols), lambda i, p: (dev(p) * nt + i, 0))
    return pl.pallas_call(
        body, grid_spec=pltpu.PrefetchScalarGridSpec(
            num_scalar_prefetch=1, grid=(nt,),
            in_specs=[pl.BlockSpec((None, tr, cols), lambda i, p: (layer, i, 0))], out_specs=[blk, lspec]),
        out_shape=[jax.ShapeDtypeStruct((rows, cols), dtype), jax.ShapeDtypeStruct(full, dtype)],
        compiler_params=_params(("parallel",)), name=name)(pos, shard)


def _gather_start(name, shards, lands, axes, groups):
    na, ng = len(shards), len(groups)
    widths = [s.shape[ax] for s, ax in zip(shards, axes)]

    def body(*refs):
        sh, ld = refs[:na], refs[na:2 * na]
        sems = refs[2 * na:2 * na + 3 * ng]
        token = refs[-1]
        x, y, c = _coords()
        me, sib = (x, y, c), (x, y, 1 - c)
        chips = [(1 - x, y), (x, 1 - y), (1 - x, 1 - y)]
        for gi, grp in enumerate(groups):
            send, r_d2d, r_ici = sems[3 * gi:3 * gi + 3]
            for li, a in enumerate(grp):
                dst = _shard_window(ld[a], axes[a], widths[a], _dev_index(me))
                pltpu.make_async_remote_copy(
                    src_ref=sh[a], dst_ref=dst, send_sem=send.at[4 * li], recv_sem=r_d2d.at[li],
                    device_id=sib, device_id_type=MESH).start()
                for j, chip in enumerate(chips):
                    pltpu.make_async_remote_copy(
                        src_ref=sh[a], dst_ref=dst, send_sem=send.at[4 * li + 1 + j], recv_sem=r_ici.at[3 * li + j],
                        device_id=(*chip, c), device_id_type=MESH).start()
        token[...] = jnp.zeros_like(token)

    out_shape, out_specs = [], []
    for grp in groups:
        out_shape += [_dma_sems(4 * len(grp)), _dma_sems(len(grp)), _dma_sems(3 * len(grp))]
        out_specs += [SEM] * 3
    out_shape += [_hbm_like(s) for s in shards] + [_hbm_like(l) for l in lands] + [TOKEN_SHAPE]
    out_specs += [HBM] * (2 * na) + [TOKEN_SPEC]
    aliases = {a: 3 * ng + a for a in range(2 * na)}
    res = pl.pallas_call(
        body, name=name, out_shape=out_shape, in_specs=[HBM] * (2 * na),
        out_specs=out_specs, input_output_aliases=aliases, compiler_params=SPLIT_PARAMS,
    )(*[_hbm(s) for s in shards], *[_hbm(l) for l in lands])
    sems = [tuple(res[3 * gi:3 * gi + 3]) for gi in range(ng)]
    return sems, list(res[3 * ng:3 * ng + na]), list(res[3 * ng + na:3 * ng + 2 * na]), res[-1]


def _gather_forward(name, lands, axes, r_ici, after):
    n = len(lands)
    widths = [l.shape[ax] // N_DEV for l, ax in zip(lands, axes)]

    def body(*refs):
        ld, r_ici_ref = refs[:n], refs[n]
        f_send, f_recv = refs[n + 2], refs[n + 3]
        x, y, c = _coords()
        sib = (x, y, 1 - c)
        chips = [(1 - x, y), (x, 1 - y), (1 - x, 1 - y)]
        for li in range(n):
            for j, chip in enumerate(chips):
                blk = _shard_window(ld[li], axes[li], widths[li], _dev_index((*chip, c)))
                pltpu.make_async_remote_copy(
                    src_ref=blk, dst_ref=blk, send_sem=f_send.at[3 * li + j], recv_sem=r_ici_ref.at[3 * li + j],
                    device_id=(*chip, c), device_id_type=MESH).wait_recv()
                pltpu.make_async_remote_copy(
                    src_ref=blk, dst_ref=blk, send_sem=f_send.at[3 * li + j], recv_sem=f_recv.at[3 * li + j],
                    device_id=sib, device_id_type=MESH).start()

    res = pl.pallas_call(
        body, name=name, out_shape=[_dma_sems(3 * n), _dma_sems(3 * n)] + [_hbm_like(l) for l in lands],
        in_specs=[HBM] * n + [SEM, ANY], out_specs=[SEM, SEM] + [HBM] * n,
        input_output_aliases={li: 2 + li for li in range(n)}, compiler_params=SPLIT_PARAMS,
    )(*lands, r_ici, after)
    return res[0], res[1], list(res[2:])


def _gather_finish(name, shards, lands, axes, send, r_d2d, f_send, f_recv, after):
    n = len(lands)
    widths = [l.shape[ax] // N_DEV for l, ax in zip(lands, axes)]

    def body(*refs):
        sh, ld = refs[:n], refs[n:2 * n]
        send_ref, r_d2d_ref, f_send_ref, f_recv_ref = refs[2 * n:2 * n + 4]
        x, y, c = _coords()
        me, sib = (x, y, c), (x, y, 1 - c)
        chips = [(1 - x, y), (x, 1 - y), (1 - x, 1 - y)]

        def blk(li, dev):
            return _shard_window(ld[li], axes[li], widths[li], _dev_index(dev))

        for li in range(n):
            for kk in range(4):
                pltpu.make_async_remote_copy(
                    src_ref=sh[li], dst_ref=blk(li, me), send_sem=send_ref.at[4 * li + kk], recv_sem=r_d2d_ref.at[li],
                    device_id=sib, device_id_type=MESH).wait_send()
            pltpu.make_async_remote_copy(
                src_ref=blk(li, sib), dst_ref=blk(li, sib), send_sem=send_ref.at[4 * li], recv_sem=r_d2d_ref.at[li],
                device_id=sib, device_id_type=MESH).wait_recv()
            for j, chip in enumerate(chips):
                pltpu.make_async_remote_copy(
                    src_ref=blk(li, (*chip, c)), dst_ref=blk(li, (*chip, c)), send_sem=f_send_ref.at[3 * li + j],
                    recv_sem=f_recv_ref.at[3 * li + j], device_id=sib, device_id_type=MESH).wait_send()
                pltpu.make_async_remote_copy(
                    src_ref=blk(li, (*chip, 1 - c)), dst_ref=blk(li, (*chip, 1 - c)), send_sem=f_send_ref.at[3 * li + j],
                    recv_sem=f_recv_ref.at[3 * li + j], device_id=sib, device_id_type=MESH).wait_recv()

    res = pl.pallas_call(
        body, name=name, out_shape=[_hbm_like(s) for s in shards] + [_hbm_like(l) for l in lands],
        in_specs=[HBM] * (2 * n) + [SEM] * 4 + [ANY], out_specs=[HBM] * (2 * n),
        input_output_aliases={i: i for i in range(2 * n)}, compiler_params=SPLIT_PARAMS,
    )(*shards, *lands, send, r_d2d, f_send, f_recv, after)
    return list(res[n:])


def _chip_exchange_start(name, parts):
    n = len(parts)

    def body(*refs):
        src, ld = refs[:n], refs[n:2 * n]
        send, recv = refs[2 * n], refs[2 * n + 1]
        token = refs[-1]
        x, y, c = _coords()
        chips = [(1 - x, y), (x, 1 - y), (1 - x, 1 - y)]
        for li in range(n):
            for kk, chip in enumerate(chips):
                pltpu.make_async_remote_copy(
                    src_ref=src[li].at[kk], dst_ref=ld[li].at[kk], send_sem=send.at[3 * li + kk],
                    recv_sem=recv.at[3 * li + kk], device_id=(*chip, c), device_id_type=MESH).start()
        token[...] = jnp.zeros_like(token)

    lands = [lax.empty(p.shape, p.dtype) for p in parts]
    res = pl.pallas_call(
        body, name=name,
        out_shape=[_dma_sems(3 * n), _dma_sems(3 * n)] + [_hbm_like(p) for p in parts] * 2 + [TOKEN_SHAPE],
        in_specs=[HBM] * (2 * n), out_specs=[SEM, SEM] + [HBM] * (2 * n) + [TOKEN_SPEC],
        input_output_aliases={i: 2 + i for i in range(2 * n)}, compiler_params=SPLIT_PARAMS,
    )(*[_hbm(p) for p in parts], *[_hbm(l) for l in lands])
    return res[0], res[1], list(res[2:2 + n]), list(res[2 + n:2 + 2 * n]), res[-1]


def _chip_exchange_finish(name, started, after):
    counts = [len(st[2]) for st in started]
    total = sum(counts)
    ns = len(started)

    def body(*refs):
        src, ld = refs[:total], refs[total:2 * total]
        sems = refs[2 * total:2 * total + 2 * ns]
        x, y, c = _coords()
        chips = [(1 - x, y), (x, 1 - y), (1 - x, 1 - y)]
        off = 0
        for si, cnt in enumerate(counts):
            send, recv = sems[2 * si], sems[2 * si + 1]
            for li in range(cnt):
                for kk, chip in enumerate(chips):
                    cp = pltpu.make_async_remote_copy(
                        src_ref=src[off + li].at[kk], dst_ref=ld[off + li].at[kk], send_sem=send.at[3 * li + kk],
                        recv_sem=recv.at[3 * li + kk], device_id=(*chip, c), device_id_type=MESH)
                    cp.wait_send()
                    cp.wait_recv()
            off += cnt

    srcs = [p for st in started for p in st[2]]
    lands = [l for st in started for l in st[3]]
    sems = [s for st in started for s in st[:2]]
    res = pl.pallas_call(
        body, name=name, out_shape=[_hbm_like(p) for p in srcs + lands],
        in_specs=[HBM] * (2 * total) + [SEM] * (2 * ns) + [ANY], out_specs=[HBM] * (2 * total),
        input_output_aliases={i: i for i in range(2 * total)}, compiler_params=SPLIT_PARAMS,
    )(*srcs, *lands, *sems, after)
    out, off = [], total
    for cnt in counts:
        out.append(list(res[off:off + cnt]))
        off += cnt
    return out


def _pair_copies(src, ld, send, recv, axes, widths):
    x, y, c = _coords()
    chips = [(x, y), (1 - x, y), (x, 1 - y), (1 - x, 1 - y)]
    return [pltpu.make_async_remote_copy(
        src_ref=_shard_window(src[li], axes[li], widths[li], _dev_index((*chip, 1 - c))),
        dst_ref=ld[li].at[kk], send_sem=send.at[4 * li + kk], recv_sem=recv.at[4 * li + kk],
        device_id=(x, y, 1 - c), device_id_type=MESH)
        for li in range(len(src)) for kk, chip in enumerate(chips)]


def _pair_exchange_start(name, grads, axes):
    n = len(grads)
    widths = [gr.shape[ax] // N_DEV for gr, ax in zip(grads, axes)]
    lands = []
    for gr, ax, wd in zip(grads, axes, widths):
        shp = list(gr.shape)
        shp[ax] = wd
        lands.append(lax.empty((4, *shp), gr.dtype))

    def body(*refs):
        for cp in _pair_copies(refs[:n], refs[n:2 * n], refs[2 * n], refs[2 * n + 1], axes, widths):
            cp.start()
        refs[-1][...] = jnp.zeros_like(refs[-1])

    res = pl.pallas_call(
        body, name=name,
        out_shape=[_dma_sems(4 * n), _dma_sems(4 * n)] + [_hbm_like(a) for a in grads + lands] + [TOKEN_SHAPE],
        in_specs=[HBM] * (2 * n), out_specs=[SEM, SEM] + [HBM] * (2 * n) + [TOKEN_SPEC],
        input_output_aliases={i: 2 + i for i in range(2 * n)}, compiler_params=SPLIT_PARAMS,
    )(*[_hbm(a) for a in grads + lands])
    return res[0], res[1], list(res[2:2 + n]), list(res[2 + n:2 + 2 * n]), res[-1]


def _pair_exchange_finish(name, started, axes, after):
    send, recv, grads, lands, _ = started
    n = len(grads)
    widths = [gr.shape[ax] // N_DEV for gr, ax in zip(grads, axes)]

    def body(*refs):
        for cp in _pair_copies(refs[:n], refs[n:2 * n], refs[2 * n], refs[2 * n + 1], axes, widths):
            cp.wait_send()
            cp.wait_recv()

    res = pl.pallas_call(
        body, name=name, out_shape=[_hbm_like(a) for a in grads + lands],
        in_specs=[HBM] * (2 * n) + [SEM, SEM, ANY], out_specs=[HBM] * (2 * n),
        input_output_aliases={i: i for i in range(2 * n)}, compiler_params=SPLIT_PARAMS,
    )(*grads, *lands, send, recv, after)
    return list(res[:n]), list(res[n:])


def _pair_sum(name, grad, land, axis, pos):
    wd = grad.shape[axis] // N_DEV
    shard_shape = land.shape[1:]
    rows, cols = shard_shape
    tr = _tile(rows, 256, 16)
    nt = rows // tr

    def dev_of(kk, pos_ref):
        return 4 * (pos_ref[0] ^ (kk & 1)) + 2 * (pos_ref[1] ^ (kk >> 1)) + pos_ref[2]

    def gspec(kk):
        if axis == 1:
            return pl.BlockSpec((tr, wd), lambda t, p: (t, dev_of(kk, p)))
        return pl.BlockSpec((tr, cols), lambda t, p: (dev_of(kk, p) * nt + t, 0))

    def body(pos_ref, g0, g1, g2, g3, l_ref, p0_ref, pb_ref):
        p0_ref[...] = g0[...].astype(F32) + l_ref[0].astype(F32)
        for kk, g_ref in enumerate((g1, g2, g3)):
            pb_ref[kk] = (g_ref[...].astype(F32) + l_ref[kk + 1].astype(F32)).astype(BF16)

    return pl.pallas_call(
        body,
        grid_spec=pltpu.PrefetchScalarGridSpec(
            num_scalar_prefetch=1, grid=(nt,),
            in_specs=[gspec(kk) for kk in range(4)] + [pl.BlockSpec((4, tr, cols), lambda t, p: (0, t, 0))],
            out_specs=[pl.BlockSpec((tr, cols), lambda t, p: (t, 0)),
                       pl.BlockSpec((3, tr, cols), lambda t, p: (0, t, 0))]),
        out_shape=[jax.ShapeDtypeStruct(shard_shape, F32), jax.ShapeDtypeStruct((3, *shard_shape), BF16)],
        compiler_params=_params(("parallel",)), name=name)(pos, grad, grad, grad, grad, land)


def _adamw_math(g, w, m, v):
    c1 = 1.0 - ADAM_B1 ** ADAM_STEP
    c2 = 1.0 - ADAM_B2 ** ADAM_STEP
    nm = ADAM_B1 * m + (1.0 - ADAM_B1) * g
    nv = ADAM_B2 * v + (1.0 - ADAM_B2) * (g * g)
    return -ADAM_LR * ((nm / c1) / (jnp.sqrt(nv / c2) + ADAM_EPS) + ADAM_WD * w), nm, nv


def _sum_parts(name, parts):
    n, rows, cols = parts.shape
    tr = _tile(rows, 1024, 8)

    def body(p_ref, o_ref):
        o_ref[...] = functools.reduce(lambda a, b: a + b, [p_ref[i] for i in range(n)])

    return pl.pallas_call(
        body, grid=(rows // tr,), in_specs=[pl.BlockSpec((n, tr, cols), lambda i: (0, i, 0))],
        out_specs=pl.BlockSpec((tr, cols), lambda i: (i, 0)), out_shape=jax.ShapeDtypeStruct((rows, cols), F32),
        compiler_params=_params(("parallel",)), name=name)(parts)


def _adamw_multi(name, gs, ws, ms, vs):
    k = len(gs)
    rows, cols = ws[0].shape
    tr = _tile(rows, 1024, 8)

    def body(*refs):
        for i in range(k):
            g_ref, w_ref, m_ref, v_ref = (refs[j * k + i] for j in range(4))
            og_ref, d_ref, nm_ref, nv_ref = (refs[(4 + j) * k + i] for j in range(4))
            g = g_ref[...]
            og_ref[...] = g
            d_ref[...], nm_ref[...], nv_ref[...] = _adamw_math(g, w_ref[...], m_ref[...], v_ref[...])

    spec = pl.BlockSpec((tr, cols), lambda i: (i, 0))
    res = pl.pallas_call(
        body, grid=(rows // tr,), in_specs=[spec] * (4 * k), out_specs=[spec] * (4 * k),
        out_shape=[jax.ShapeDtypeStruct((rows, cols), F32)] * (4 * k),
        compiler_params=_params(("parallel",)), name=name)(*gs, *ws, *ms, *vs)
    return [[res[j * k + i] for j in range(4)] for i in range(k)]


def _adamw(name, parts, w, m, v, layer=None, prev=None):
    rows, cols = w.shape[-2:]
    tr = _tile(rows, 256, 8)
    npart = len(parts)

    def body(*refs):
        p_refs = refs[:npart]
        w_ref, m_ref, v_ref = refs[npart:npart + 3]
        g_ref, d_ref, nm_ref, nv_ref = refs[-4:]
        g = None
        for r in p_refs:
            if len(r.shape) == 3:
                for i in range(r.shape[0]):
                    t = r[i].astype(F32)
                    g = t if g is None else g + t
            else:
                t = r[...].astype(F32)
                g = t if g is None else g + t
        g_ref[...] = g
        d_ref[...], nm_ref[...], nv_ref[...] = _adamw_math(g, w_ref[...], m_ref[...], v_ref[...])

    spec = pl.BlockSpec((tr, cols), lambda i: (i, 0))
    wspec = spec if layer is None else pl.BlockSpec((None, tr, cols), lambda i: (layer, i, 0))
    pspecs = [pl.BlockSpec((p.shape[0], tr, cols), lambda i: (0, i, 0)) if p.ndim == 3 else spec
              for p in parts]
    prev = list(prev) if prev else []
    return pl.pallas_call(
        body, grid=(rows // tr,), in_specs=pspecs + [wspec] * 3 + [ANY] * len(prev), out_specs=[wspec] * 4,
        out_shape=[jax.ShapeDtypeStruct(w.shape, F32)] * 4,
        input_output_aliases={npart + 3 + i: i for i in range(len(prev))},
        compiler_params=_params(("parallel",)), name=name)(*parts, w, m, v, *prev)


SMALL_NAMES = ("s5_lam_re", "s5_lam_im", "s5_log_dt", "s5_b_re", "s5_b_im", "s5_c_re", "s5_c_im",
               "ffn_norm", "b_norm_mix", "kv_norm", "final_norm")
SMALL_PAD = 1024
SMALL_GROUPS = (("s5_lam_re", "s5_lam_im"), ("s5_log_dt",), ("s5_b_re", "s5_b_im"), ("s5_c_re", "s5_c_im"),
                ("ffn_norm",), ("b_norm_mix", "kv_norm", "final_norm"))


def _pack(parts):
    flat = []
    for p in parts:
        f = p.reshape(-1)
        pad = (-f.shape[0]) % SMALL_PAD
        if pad:
            f = jnp.concatenate([f, jnp.zeros((pad,), f.dtype)])
        flat.append(f)
    return jnp.concatenate(flat).reshape(-1, 128)


def _unpack(packed, shapes):
    flat = packed.reshape(-1)
    out, off = [], 0
    for shp in shapes:
        size = math.prod(shp)
        out.append(flat[off:off + size].reshape(shp))
        off += size + (-size) % SMALL_PAD
    return out


def kernel(x, s5_lam_re, s5_lam_im, s5_log_dt, s5_b_re, s5_b_im, s5_c_re, s5_c_im, s5_d, s5_w_glu, a_norm_mix, ffn_norm, ffn_w_in, ffn_w_out, b_norm_mix, attn_w_q, attn_w_o, kv_norm, w_kv, final_norm, loss_target, m_s5_lam_re, m_s5_lam_im, m_s5_log_dt, m_s5_b_re, m_s5_b_im, m_s5_c_re, m_s5_c_im, m_s5_d, m_s5_w_glu, m_a_norm_mix, m_ffn_norm, m_ffn_w_in, m_ffn_w_out, m_b_norm_mix, m_attn_w_q, m_attn_w_o, m_kv_norm, m_w_kv, m_final_norm, v_s5_lam_re, v_s5_lam_im, v_s5_log_dt, v_s5_b_re, v_s5_b_im, v_s5_c_re, v_s5_c_im, v_s5_d, v_s5_w_glu, v_a_norm_mix, v_ffn_norm, v_ffn_w_in, v_ffn_w_out, v_b_norm_mix, v_attn_w_q, v_attn_w_o, v_kv_norm, v_w_kv, v_final_norm):
    args = dict(locals())
    T, D = x.shape[1], x.shape[2]
    n_layers = ffn_w_in.shape[0]
    xi_, yi_, ci_ = _coords()
    pos = jnp.stack([xi_, yi_, ci_]).astype(jnp.int32)
    me = 4 * xi_ + 2 * yi_ + ci_

    big_names = ["glu"] + [f"win{l}" for l in range(n_layers)] + [f"wout{l}" for l in range(n_layers)] \
        + ["wkv", "wq", "wo"]
    big_shards = [(s5_w_glu, 0)] + [(ffn_w_in, l) for l in range(n_layers)] \
        + [(ffn_w_out, l) for l in range(n_layers)] + [(w_kv[None], 0), (attn_w_q, 0), (attn_w_o, 0)]
    big_axes = [1] + [1] * n_layers + [0] * n_layers + [0, 1, 0]
    big_out_names = ["s5_w_glu"] + ["ffn_w_in"] * n_layers + ["ffn_w_out"] * n_layers \
        + ["w_kv", "attn_w_q", "attn_w_o"]
    index_of = {n: i for i, n in enumerate(big_names)}
    vec_shard = jnp.concatenate([s5_d, a_norm_mix], axis=0)
    (vecs,) = _all_gather("vectors_all_gather", [vec_shard], [1])

    gather_groups = [["glu"], ["win0"], ["wout0"], ["wkv", "wq", "wo"], ["win1", "wout1"]]
    group_idx = [[index_of[n] for n in grp] for grp in gather_groups]
    group_of = {n: gi for gi, grp in enumerate(gather_groups) for n in grp}
    placed = [_cast_and_place(f"cast_place_{n}", s, l, ax, pos, BF16)
              for n, (s, l), ax in zip(big_names, big_shards, big_axes)]
    gather_sems, shards_thru, lands_thru, start_token = _gather_start(
        "weights_gather_start", [p[0] for p in placed], [p[1] for p in placed], big_axes, group_idx)
    full = {}

    forwarded = {}

    def wts(name, after, prefetch=False):
        gi = group_of[name]
        idx = group_idx[gi]
        axes = [big_axes[a] for a in idx]
        send, r_d2d, r_ici = gather_sems[gi]
        if gi not in forwarded:
            forwarded[gi] = _gather_forward(
                f"weights_gather_forward{gi}", [lands_thru[a] for a in idx], axes, r_ici, after)
        if prefetch:
            return None
        if name not in full:
            f_send, f_recv, lands = forwarded[gi]
            done = _gather_finish(f"weights_gather_finish{gi}", [shards_thru[a] for a in idx], lands, axes,
                                  send, r_d2d, f_send, f_recv, after)
            full.update(zip(gather_groups[gi], done))
        return full[name]

    exchanges, pending = [], []

    def flush(after):
        names, axes, started = pending.pop()
        tag = "_".join(names)
        grads, lands = _pair_exchange_finish(f"rs_pair_exchange_finish_{tag}", started, axes, after)
        p0s, pbs = [], []
        for n, gr, land, ax in zip(names, grads, lands, axes):
            p0, pb = _pair_sum(f"rs_pair_sum_{n}", gr, land, ax, pos)
            p0s.append(p0)
            pbs.append(pb)
        started = _chip_exchange_start(f"rs_chip_exchange_start_{tag}", pbs)
        exchanges.append((names, p0s, started))
        return started[4]

    def ready(grads, after=None):
        if not grads:
            return flush(after)
        names = list(grads)
        axes = [big_axes[index_of[n]] for n in names]
        started = _pair_exchange_start(f"rs_pair_exchange_start_{'_'.join(names)}", [grads[n] for n in names], axes)
        token = flush(started[4]) if pending else started[4]
        pending.append((names, axes, started))
        return token

    G, P, C = s5_b_re.shape[1:]
    w = dict(
        a_norm=vecs[1:2], s5_d=vecs[0:1],
        ffn_norm=ffn_norm, b_norm=b_norm_mix, kv_norm=kv_norm.reshape(1, D), final_norm=final_norm.reshape(1, D),
        lam_re=s5_lam_re[0], lam_im=s5_lam_im[0], log_dt=s5_log_dt.reshape(G, 1),
        bt_re=s5_b_re[0].transpose(2, 0, 1), bt_im=s5_b_im[0].transpose(2, 0, 1),
        c_re=s5_c_re[0], c_im=s5_c_im[0],
    )
    small = {}

    def small_ready(g):
        small_g = dict(
            s5_lam_re=g["lam_re"], s5_lam_im=g["lam_im"], s5_log_dt=g["log_dt"],
            s5_b_re=g["bt_re"].transpose(1, 2, 0), s5_b_im=g["bt_im"].transpose(1, 2, 0),
            s5_c_re=g["c_re"], s5_c_im=g["c_im"], ffn_norm=g["ffn_norm"], b_norm_mix=g["b_norm"],
            kv_norm=g["kv_norm"], final_norm=g["final_norm"])
        packed = _pack([small_g[n] for n in SMALL_NAMES])
        shard, land = _cast_and_place("place_small_grads", packed[None], 0, 0, pos, F32)
        sems, thru, lands, token = _gather_start("small_grads_gather_start", [shard], [land], [0], [[0]])
        small.update(sems=sems[0], thru=thru, lands=lands, rows=packed.shape[0])
        return token

    loss_blk, grad_x, g = _local_step(x[0], loss_target[0], w, wts, ready, dep0=start_token,
                                      small_ready=small_ready)
    loss = lax.psum(loss_blk[0, 0], ("x", "y", "c"))
    (tail,) = _all_gather("vector_grads_all_gather", [_pack([g["s5_d"], g["a_norm"]])], [0])

    out = {}

    def put(name, res, shape):
        for kind, r in zip(("grad", "delta", "new_m", "new_v"), res):
            out[f"{kind}_{name}"] = r.reshape(shape)

    updated = {}

    def update(names, p0s, recvd):
        for name, p0, rc in zip(names, p0s, recvd):
            oname = big_out_names[index_of[name]]
            w3, layer = big_shards[index_of[name]]
            updated[oname] = _adamw(f"adamw_{name}", [p0, rc], w3, args["m_" + oname].reshape(w3.shape),
                                    args["v_" + oname].reshape(w3.shape), layer=layer, prev=updated.get(oname))
        return updated[big_out_names[index_of[names[0]]]][0]

    early, last = exchanges[:-1], exchanges[-1]
    landed = _chip_exchange_finish("rs_chip_exchange_finish_early", [e[2] for e in early], grad_x)
    for (names, p0s, _), recvd in zip(early[:-1], landed[:-1]):
        marker = update(names, p0s, recvd)
    send, r_d2d, r_ici = small["sems"]
    f_send, f_recv, small_lands = _gather_forward("small_grads_gather_forward", small["lands"], [0], r_ici, marker)
    marker = update(early[-1][0], early[-1][1], landed[-1])
    (all_parts,) = _gather_finish("small_grads_gather_finish", small["thru"], small_lands, [0],
                                  send, r_d2d, f_send, f_recv, marker)
    (recvd,) = _chip_exchange_finish("rs_chip_exchange_finish_last", [last[2]], marker)
    update(last[0], last[1], recvd)
    for oname, res in updated.items():
        put(oname, res, args[oname].shape)
    g_small = _sum_parts("small_grads_sum", all_parts.reshape(N_DEV, small["rows"], 128))
    g_small = dict(zip(SMALL_NAMES, _unpack(g_small, [args[n].shape for n in SMALL_NAMES])))

    def view(a):
        return a.reshape(1, -1) if a.ndim == 1 else a.reshape(-1, a.shape[-1])

    for grp in SMALL_GROUPS:
        res = _adamw_multi(f"adamw_{grp[0]}", [view(g_small[n]) for n in grp], [view(args[n]) for n in grp],
                           [view(args["m_" + n]) for n in grp], [view(args["v_" + n]) for n in grp])
        for n, r in zip(grp, res):
            put(n, r, args[n].shape)
    ws = D // N_DEV
    tail = lax.dynamic_slice_in_dim(tail.reshape(N_DEV, 2, D), me * ws, ws, axis=2)
    res = _adamw("adamw_vec", [tail], vec_shard,
                 jnp.concatenate([m_s5_d, m_a_norm_mix], axis=0), jnp.concatenate([v_s5_d, v_a_norm_mix], axis=0))
    put("s5_d", [r[0:1] for r in res], s5_d.shape)
    put("a_norm_mix", [r[1:2] for r in res], a_norm_mix.shape)

    names = ("s5_lam_re", "s5_lam_im", "s5_log_dt", "s5_b_re", "s5_b_im", "s5_c_re", "s5_c_im", "s5_d",
             "s5_w_glu", "a_norm_mix", "ffn_norm", "ffn_w_in", "ffn_w_out", "b_norm_mix", "attn_w_q",
             "attn_w_o", "kv_norm", "w_kv", "final_norm")
    result = [loss, grad_x.reshape(x.shape)]
    for kind in ("grad", "delta", "new_m", "new_v"):
        result += [out[f"{kind}_{n}"] for n in names]
    return tuple(result)
```

```python
import functools
import math

import jax
import jax.numpy as jnp
from jax import lax
from jax.experimental import pallas as pl
from jax.experimental.pallas import tpu as pltpu

F32 = jnp.float32
BF16 = jnp.bfloat16

EPS = 1e-6
NEG_INF = -1e30
HEAD_DIM = 128
N_KV_HEADS = 4
DILATIONS = (1, 4, 16)
ATT_BLK = 128
S5_C = 16
S5_P = 64
S5_GB = 16
S5_CH = S5_GB * S5_C
S5_W = S5_GB * S5_P
S5_UNROLL = 4
N_DEV = 8

ADAM_LR = 0.001
ADAM_B1 = 0.9
ADAM_B2 = 0.999
ADAM_EPS = 1e-08
ADAM_WD = 0.01
ADAM_STEP = 10

VMEM_LIMIT_BYTES = 56 * 1024 * 1024
MM_TILE = 1024
MM_TILE_NARROW = 512
MM_DEPTH = 2816
GU_RING = 3
MESH = pl.DeviceIdType.MESH
ANY = pl.BlockSpec(memory_space=pl.ANY)


def _tile(n, pref, align=128):
    t = (min(pref, n) // align) * align
    while t >= align:
        if n % t == 0:
            return t
        t -= align
    return n


def _params(sem):
    return pltpu.CompilerParams(dimension_semantics=sem, vmem_limit_bytes=VMEM_LIMIT_BYTES)


def _sigmoid(x):
    return 1.0 / (1.0 + jnp.exp(-x))


NN = (((1,), (0,)), ((), ()))
NT = (((1,), (1,)), ((), ()))
TN = (((0,), (0,)), ((), ()))


def _dot(a, b, dims=NN):
    return lax.dot_general(a, b, dims, preferred_element_type=F32)


def _matmul(name, grid, ins, in_specs, products, dims, out_shapes, out_specs, acc_shapes, epilogue):
    n_in, n_out, nk = len(ins), len(out_shapes), grid[2]

    def body(*refs):
        in_refs = refs[:n_in]
        out_refs = refs[n_in:n_in + n_out]
        acc_refs = refs[n_in + n_out:]

        def prods():
            vals = [None] * len(acc_shapes)
            for ai, bi, ci in products:
                d = _dot(in_refs[ai][...].astype(BF16), in_refs[bi][...].astype(BF16), dims)
                vals[ci] = d if vals[ci] is None else vals[ci] + d
            return vals

        if nk == 1:
            epilogue(in_refs, out_refs, prods())
        else:
            k = pl.program_id(2)

            @pl.when(k == 0)
            def _():
                for a in acc_refs:
                    a[...] = jnp.zeros_like(a)

            for a, v in zip(acc_refs, prods()):
                a[...] += v

            @pl.when(k == nk - 1)
            def _():
                epilogue(in_refs, out_refs, [a[...] for a in acc_refs])

    scratch = [] if nk == 1 else [pltpu.VMEM(s, F32) for s in acc_shapes]
    return pl.pallas_call(
        body, grid=grid, in_specs=in_specs, out_specs=out_specs, out_shape=out_shapes,
        scratch_shapes=scratch, compiler_params=_params(("parallel", "parallel", "arbitrary")),
        name=name)(*ins)


def _mm_dual_fwd(name, a, w, res, kind):
    T, K = a.shape
    N = w.shape[1] // 2
    tm, tn = _tile(T, MM_TILE), _tile(N, MM_TILE_NARROW)
    nj = N // tn
    grid = (T // tm, nj, 1)
    ins = [a, w, w]
    in_specs = [pl.BlockSpec((tm, K), lambda i, j, k: (i, 0)),
                pl.BlockSpec((K, tn), lambda i, j, k: (0, j)),
                pl.BlockSpec((K, tn), lambda i, j, k: (0, j + nj))]
    pair_spec = pl.BlockSpec((2, tm, tn), lambda i, j, k: (0, i, j))
    tile_spec = pl.BlockSpec((tm, tn), lambda i, j, k: (i, j))
    if kind == "glu":
        ins.append(res)
        in_specs.append(tile_spec)

        def epilogue(in_refs, out_refs, accs):
            val, gate = accs
            s = _sigmoid(gate)
            out_refs[0][...] = in_refs[3][...] + val * s
            out_refs[1][0] = s.astype(BF16)
            out_refs[1][1] = (val * s * (1.0 - s)).astype(BF16)

        out_shapes = [jax.ShapeDtypeStruct((T, N), F32), jax.ShapeDtypeStruct((2, T, N), BF16)]
        out_specs = [tile_spec, pair_spec]
    else:
        def epilogue(in_refs, out_refs, accs):
            g, u = accs
            s = _sigmoid(g)
            silu = g * s
            out_refs[0][0] = (u * (s * (1.0 + g * (1.0 - s)))).astype(BF16)
            out_refs[0][1] = silu.astype(BF16)
            out_refs[1][...] = (silu * u).astype(BF16)

        out_shapes = [jax.ShapeDtypeStruct((2, T, N), BF16), jax.ShapeDtypeStruct((T, N), BF16)]
        out_specs = [pair_spec, tile_spec]
    return _matmul(name, grid, ins, in_specs, [(0, 1, 0), (0, 2, 1)], NN, out_shapes, out_specs,
                   [(tm, tn), (tm, tn)], epilogue)


def _mm_kv(name, a, w):
    T, K = a.shape
    N = w.shape[1] // 2
    tm, tn = _tile(T, MM_TILE), _tile(N, MM_TILE_NARROW)
    nj = N // tn
    tile_spec = pl.BlockSpec((tm, tn), lambda i, j, k: (i, j))

    def epilogue(in_refs, out_refs, accs):
        out_refs[0][...] = accs[0]
        out_refs[1][...] = accs[1]

    return _matmul(name, (T // tm, nj, 1), [a, w, w],
                   [pl.BlockSpec((tm, K), lambda i, j, k: (i, 0)),
                    pl.BlockSpec((K, tn), lambda i, j, k: (0, j)),
                    pl.BlockSpec((K, tn), lambda i, j, k: (0, j + nj))],
                   [(0, 1, 0), (0, 2, 1)], NN,
                   [jax.ShapeDtypeStruct((T, N), F32)] * 2, [tile_spec, tile_spec],
                   [(tm, tn), (tm, tn)], epilogue)


def _mm_nn(name, a, w, res=None, out_dtype=F32):
    T, K = a.shape
    N = w.shape[1]
    tk = K if K <= 2 * MM_DEPTH else _tile(K, MM_DEPTH)
    tm, tn = _tile(T, MM_TILE), _tile(N, MM_TILE if K <= MM_DEPTH else MM_TILE_NARROW)
    grid = (T // tm, N // tn, K // tk)
    tile_spec = pl.BlockSpec((tm, tn), lambda i, j, k: (i, j))
    ins = [a, w]
    in_specs = [pl.BlockSpec((tm, tk), lambda i, j, k: (i, k)),
                pl.BlockSpec((tk, tn), lambda i, j, k: (k, j))]
    if res is not None:
        ins.append(res)
        in_specs.append(tile_spec)

    def epilogue(in_refs, out_refs, accs):
        v = accs[0]
        if res is not None:
            v = v + in_refs[2][...]
        out_refs[0][...] = v.astype(out_dtype)

    return _matmul(name, grid, ins, in_specs, [(0, 1, 0)], NN,
                   [jax.ShapeDtypeStruct((T, N), out_dtype)], [tile_spec], [(tm, tn)], epilogue)[0]


def _dep_operand(ins, in_specs, dep):
    if dep is not None:
        ins.append(dep)
        in_specs.append(pl.BlockSpec((8, 128), lambda *_: (0, 0)))


def _mm_nt(name, a_list, w, out_dtype=F32, dep=None):
    T, Np = a_list[0].shape
    Ko = w.shape[0]
    n_parts = len(a_list)
    wide_a = a_list[0].dtype != BF16
    tm, tn, tk = _tile(T, MM_TILE_NARROW if wide_a else MM_TILE), _tile(Ko, MM_TILE), _tile(Np, MM_DEPTH)
    nkp = Np // tk
    grid = (T // tm, Ko // tn, nkp)
    ins = list(a_list) + [w] * n_parts
    in_specs = [pl.BlockSpec((tm, tk), lambda i, j, k: (i, k)) for _ in a_list]
    in_specs += [pl.BlockSpec((tn, tk), functools.partial(lambda i, j, k, p: (j, p * nkp + k), p=p))
                 for p in range(n_parts)]
    products = [(p, n_parts + p, 0) for p in range(n_parts)]
    _dep_operand(ins, in_specs, dep)

    def epilogue(in_refs, out_refs, accs):
        out_refs[0][...] = accs[0].astype(out_dtype)

    return _matmul(name, grid, ins, in_specs, products, NT,
                   [jax.ShapeDtypeStruct((T, Ko), out_dtype)],
                   [pl.BlockSpec((tm, tn), lambda i, j, k: (i, j))], [(tm, tn)], epilogue)[0]


def _mm_nt_pair(name, a3, w, out_dtype=F32):
    _, T, N = a3.shape
    Ko = w.shape[0]
    tm, tn, tk = _tile(T, MM_TILE), _tile(Ko, MM_TILE), _tile(N, MM_DEPTH)
    nkh = N // tk
    grid = (T // tm, Ko // tn, 2 * nkh)

    def epilogue(in_refs, out_refs, accs):
        out_refs[0][...] = accs[0].astype(out_dtype)

    return _matmul(name, grid, [a3, w],
                   [pl.BlockSpec((None, tm, tk), lambda i, j, k: (k // nkh, i, k % nkh)),
                    pl.BlockSpec((tn, tk), lambda i, j, k: (j, k))],
                   [(0, 1, 0)], NT, [jax.ShapeDtypeStruct((T, Ko), out_dtype)],
                   [pl.BlockSpec((tm, tn), lambda i, j, k: (i, j))], [(tm, tn)], epilogue)[0]


def _mm_nt_ffn_bwd(name, dx, w_out, gu, dep=None):
    T, D = dx.shape
    Fh = w_out.shape[0]
    tm, tn = _tile(T, 2 * MM_TILE), _tile(Fh, MM_TILE_NARROW)
    nj = Fh // tn
    n_steps = (T // tm) * nj
    n_dep = 0 if dep is None else 1

    def body(dx_ref, w_ref, gu_any, *rest):
        out_ref, ring, sems = rest[n_dep:]
        s = pl.program_id(0) * nj + pl.program_id(1)

        def fetch(step):
            i, j = step // nj, step % nj
            return pltpu.make_async_copy(
                gu_any.at[:, pl.ds(i * tm, tm), pl.ds(j * tn, tn)], ring.at[step % GU_RING], sems.at[step % GU_RING])

        @pl.when(s == 0)
        def _():
            for first in range(min(GU_RING - 1, n_steps)):
                fetch(first).start()

        @pl.when(s + GU_RING - 1 < n_steps)
        def _():
            fetch(s + GU_RING - 1).start()

        fetch(s).wait()
        gu_ref = ring.at[s % GU_RING]
        da = _dot(dx_ref[...], w_ref[...], NT)
        out_ref[0] = (da * gu_ref[0].astype(F32)).astype(BF16)
        out_ref[1] = (da * gu_ref[1].astype(F32)).astype(BF16)

    pair_spec = pl.BlockSpec((2, tm, tn), lambda i, j: (0, i, j))
    ins = [dx, w_out, gu]
    in_specs = [pl.BlockSpec((tm, D), lambda i, j: (i, 0)),
                pl.BlockSpec((tn, D), lambda i, j: (j, 0)),
                ANY]
    _dep_operand(ins, in_specs, dep)
    return pl.pallas_call(
        body, grid=(T // tm, nj), in_specs=in_specs, out_specs=pair_spec,
        out_shape=jax.ShapeDtypeStruct((2, T, Fh), BF16),
        scratch_shapes=[pltpu.VMEM((GU_RING, 2, tm, tn), BF16), pltpu.SemaphoreType.DMA((GU_RING,))],
        compiler_params=_params(("arbitrary", "arbitrary")), name=name)(*ins)


def _mm_tn(name, a, d):
    T, Ko = a.shape
    N = d.shape[1]
    to, tn, tk = _tile(Ko, MM_TILE_NARROW), _tile(N, MM_TILE if d.dtype == BF16 else MM_TILE_NARROW), T
    grid = (Ko // to, N // tn, T // tk)

    def epilogue(in_refs, out_refs, accs):
        out_refs[0][...] = accs[0].astype(BF16)

    return _matmul(name, grid, [a, d],
                   [pl.BlockSpec((tk, to), lambda i, j, k: (k, i)),
                    pl.BlockSpec((tk, tn), lambda i, j, k: (k, j))],
                   [(0, 1, 0)], TN, [jax.ShapeDtypeStruct((Ko, N), BF16)],
                   [pl.BlockSpec((to, tn), lambda i, j, k: (i, j))], [(to, tn)], epilogue)[0]


def _mm_tn_pair(name, a, d3):
    T, Ko = a.shape
    N = d3.shape[2]
    to, tn, tk = _tile(Ko, MM_TILE), _tile(N, MM_TILE_NARROW), T
    njh = N // tn
    grid = (Ko // to, 2 * njh, T // tk)

    def epilogue(in_refs, out_refs, accs):
        out_refs[0][...] = accs[0].astype(BF16)

    return _matmul(name, grid, [a, d3],
                   [pl.BlockSpec((tk, to), lambda i, j, k: (k, i)),
                    pl.BlockSpec((None, tk, tn), lambda i, j, k: (j // njh, k, j % njh))],
                   [(0, 1, 0)], TN, [jax.ShapeDtypeStruct((Ko, 2 * N), BF16)],
                   [pl.BlockSpec((to, tn), lambda i, j, k: (i, j))], [(to, tn)], epilogue)[0]


def _rms_fwd(name, x, gains, dtypes, dep=None):
    T, D = x.shape
    n = len(gains)
    tr = _tile(T, 512, 8)
    n_dep = 0 if dep is None else 1

    def body(x_ref, *refs):
        xv = x_ref[...]
        xr = xv * lax.rsqrt(jnp.mean(xv * xv, axis=-1, keepdims=True) + EPS)
        for g_ref, o_ref in zip(refs[:n], refs[n + n_dep:]):
            o_ref[...] = (xr * g_ref[...]).astype(o_ref.dtype)

    row = pl.BlockSpec((tr, D), lambda i: (i, 0))
    vec = pl.BlockSpec((1, D), lambda i: (0, 0))
    ins, in_specs = [x, *gains], [row] + [vec] * n
    _dep_operand(ins, in_specs, dep)
    return pl.pallas_call(
        body, grid=(T // tr,), in_specs=in_specs, out_specs=[row] * n,
        out_shape=[jax.ShapeDtypeStruct((T, D), dt) for dt in dtypes],
        compiler_params=_params(("parallel",)), name=name)(*ins)


def _rms_bwd(name, x, dres, gains, dhs, dep=None):
    T, D = x.shape
    n = len(gains)
    tr = _tile(T, 256, 8)
    n_dep = 0 if dep is None else 1

    def body(x_ref, dres_ref, *refs):
        g_refs, dh_refs = refs[:n], refs[n:2 * n]
        dx_ref, dxb_ref = refs[2 * n + n_dep], refs[2 * n + n_dep + 1]
        dg_refs = refs[2 * n + n_dep + 2:]
        xv = x_ref[...]
        r = lax.rsqrt(jnp.mean(xv * xv, axis=-1, keepdims=True) + EPS)
        xr = xv * r
        w = None
        for g_ref, dh_ref, dg_ref in zip(g_refs, dh_refs, dg_refs):
            dh = dh_ref[...].astype(F32)

            @pl.when(pl.program_id(0) == 0)
            def _():
                dg_ref[...] = jnp.zeros_like(dg_ref)

            dg_ref[...] += jnp.sum(dh * xr, axis=0, keepdims=True)
            wi = dh * g_ref[...]
            w = wi if w is None else w + wi
        dx = dres_ref[...] + r * (w - xr * jnp.mean(w * xr, axis=-1, keepdims=True))
        dx_ref[...] = dx
        dxb_ref[...] = dx.astype(BF16)

    row = pl.BlockSpec((tr, D), lambda i: (i, 0))
    vec = pl.BlockSpec((1, D), lambda i: (0, 0))
    ins, in_specs = [x, dres, *gains, *dhs], [row, row] + [vec] * n + [row] * n
    _dep_operand(ins, in_specs, dep)
    outs = pl.pallas_call(
        body, grid=(T // tr,), in_specs=in_specs,
        out_specs=[row, row] + [vec] * n,
        out_shape=[jax.ShapeDtypeStruct((T, D), F32), jax.ShapeDtypeStruct((T, D), BF16)]
        + [jax.ShapeDtypeStruct((1, D), F32)] * n,
        compiler_params=_params(("arbitrary",)), name=name)(*ins)
    return outs[0], outs[1], outs[2:]


def _loss_head(x, gain, target):
    T, D = x.shape
    tr = _tile(T, 256, 8)

    def body(x_ref, g_ref, t_ref, loss_ref, dx_ref, dxb_ref, dg_ref):
        @pl.when(pl.program_id(0) == 0)
        def _():
            loss_ref[...] = jnp.zeros_like(loss_ref)
            dg_ref[...] = jnp.zeros_like(dg_ref)

        xv = x_ref[...]
        r = lax.rsqrt(jnp.mean(xv * xv, axis=-1, keepdims=True) + EPS)
        xr = xv * r
        err = xr * g_ref[...] - t_ref[...]
        part = jnp.sum(jnp.sum(err * err, axis=-1, keepdims=True), axis=0, keepdims=True) * (0.5 / D)
        loss_ref[...] += jnp.broadcast_to(part, loss_ref.shape)
        dy = err * (1.0 / D)
        dg_ref[...] += jnp.sum(dy * xr, axis=0, keepdims=True)
        w = dy * g_ref[...]
        dx = r * (w - xr * jnp.mean(w * xr, axis=-1, keepdims=True))
        dx_ref[...] = dx
        dxb_ref[...] = dx.astype(BF16)

    row = pl.BlockSpec((tr, D), lambda i: (i, 0))
    vec = pl.BlockSpec((1, D), lambda i: (0, 0))
    return pl.pallas_call(
        body, grid=(T // tr,), in_specs=[row, vec, row],
        out_specs=[pl.BlockSpec((8, 128), lambda i: (0, 0)), row, row, vec],
        out_shape=[jax.ShapeDtypeStruct((8, 128), F32), jax.ShapeDtypeStruct((T, D), F32),
                   jax.ShapeDtypeStruct((T, D), BF16), jax.ShapeDtypeStruct((1, D), F32)],
        compiler_params=_params(("arbitrary",)), name="loss_head")(x, gain, target)


def _glu_bwd(dmix, vg, dep=None):
    T, N = dmix.shape
    tr, tc = _tile(T, 512, 8), _tile(N, 1024)
    n_dep = 0 if dep is None else 1

    def body(d_ref, vg_ref, *refs):
        o_ref = refs[n_dep]
        d = d_ref[...]
        o_ref[0] = (d * vg_ref[0].astype(F32)).astype(BF16)
        o_ref[1] = (d * vg_ref[1].astype(F32)).astype(BF16)

    pair = pl.BlockSpec((2, tr, tc), lambda i, j: (0, i, j))
    ins, in_specs = [dmix, vg], [pl.BlockSpec((tr, tc), lambda i, j: (i, j)), pair]
    _dep_operand(ins, in_specs, dep)
    return pl.pallas_call(
        body, grid=(T // tr, N // tc), in_specs=in_specs,
        out_specs=pair, out_shape=jax.ShapeDtypeStruct((2, T, N), BF16),
        compiler_params=_params(("parallel", "parallel")), name="glu_bwd")(*ins)


def _to_state_tiles(x_ref, s, val):
    tc = val.shape[0]
    for j in range(S5_W // 128):
        x_ref[s, pl.ds(j, tc, stride=8), :] = val[:, 128 * j:128 * (j + 1)]


def _from_state_tiles(x_ref, s, tc):
    return jnp.concatenate([x_ref[s, pl.ds(j, tc, stride=8), :] for j in range(S5_W // 128)], axis=1)


def _s5_scan_fwd(xr_ref, xi_ref, ar_ref, ai_ref, cr_ref, ci_ref, tc, nblk):
    a = [(ar_ref[s], ai_ref[s]) for s in range(nblk)]

    def step(i, carry):
        carry = list(carry)
        for uu in range(S5_UNROLL):
            r0 = pl.multiple_of((i * S5_UNROLL + uu) * 8, 8)
            for s in range(nblk):
                cr, ci = carry[2 * s], carry[2 * s + 1]
                a_r, a_i = a[s]
                xr = a_r * cr - a_i * ci + xr_ref[s, pl.ds(r0, 8), :]
                xi = a_r * ci + a_i * cr + xi_ref[s, pl.ds(r0, 8), :]
                xr_ref[s, pl.ds(r0, 8), :] = xr
                xi_ref[s, pl.ds(r0, 8), :] = xi
                carry[2 * s], carry[2 * s + 1] = xr, xi
        return tuple(carry)

    init = []
    for s in range(nblk):
        init += [cr_ref[s], ci_ref[s]]
    out = lax.fori_loop(0, tc // S5_UNROLL, step, tuple(init))
    for s in range(nblk):
        cr_ref[s] = out[2 * s]
        ci_ref[s] = out[2 * s + 1]


def _s5_scan_bwd(lr_ref, li_ref, xr_ref, xi_ref, h_ref, ar_ref, ai_ref, cr_ref, ci_ref,
                 accr_ref, acci_ref, tc, nblk):
    a = [(ar_ref[s], ai_ref[s]) for s in range(nblk)]

    def one(s, r0, prev_r, prev_i, st):
        c_r, c_i, d_r, d_i = st
        a_r, a_i = a[s]
        l_r = lr_ref[s, pl.ds(r0, 8), :] + a_r * c_r + a_i * c_i
        l_i = li_ref[s, pl.ds(r0, 8), :] + a_r * c_i - a_i * c_r
        lr_ref[s, pl.ds(r0, 8), :] = l_r
        li_ref[s, pl.ds(r0, 8), :] = l_i
        return [l_r, l_i, d_r + l_r * prev_r + l_i * prev_i, d_i - l_r * prev_i + l_i * prev_r]

    def step(i, carry):
        carry = list(carry)
        for uu in range(S5_UNROLL):
            t = tc - 1 - (i * S5_UNROLL + uu)
            r0 = pl.multiple_of(t * 8, 8)
            p0 = pl.multiple_of((t - 1) * 8, 8)
            for s in range(nblk):
                carry[4 * s:4 * s + 4] = one(s, r0, xr_ref[s, pl.ds(p0, 8), :], xi_ref[s, pl.ds(p0, 8), :],
                                             carry[4 * s:4 * s + 4])
        return tuple(carry)

    init = []
    for s in range(nblk):
        init += [cr_ref[s], ci_ref[s], accr_ref[s], acci_ref[s]]
    carry = list(lax.fori_loop(0, tc // S5_UNROLL - 1, step, tuple(init)))
    for t in range(S5_UNROLL - 1, -1, -1):
        for s in range(nblk):
            if t > 0:
                prev_r, prev_i = xr_ref[s, 8 * (t - 1):8 * t, :], xi_ref[s, 8 * (t - 1):8 * t, :]
            else:
                prev_r, prev_i = h_ref[0, s], h_ref[1, s]
            carry[4 * s:4 * s + 4] = one(s, 8 * t, prev_r, prev_i, carry[4 * s:4 * s + 4])
    for s in range(nblk):
        cr_ref[s], ci_ref[s], accr_ref[s], acci_ref[s] = carry[4 * s:4 * s + 4]


def _gelu_and_grad(y):
    k = math.sqrt(2.0 / math.pi)
    t = jnp.tanh(k * (y + 0.044715 * (y * y * y)))
    return (0.5 * y * (1.0 + t),
            0.5 * (1.0 + t) + 0.5 * y * (1.0 - t * t) * (k * (1.0 + 3.0 * 0.044715 * (y * y))))


def _s5_specs(tc, nch, sbk, rev):
    def ch(c):
        return nch - 1 - c if rev else c

    return dict(
        act=pl.BlockSpec((tc, sbk * S5_CH), lambda i, c: (ch(c), i)),
        bb=pl.BlockSpec((2, sbk, S5_CH, 128), lambda i, c: (0, i, 0, 0)),
        cc=pl.BlockSpec((2, sbk, S5_P, S5_CH), lambda i, c: (0, i, 0, 0)),
        a=pl.BlockSpec((sbk, 8, 128), lambda i, c: (i, 0, 0)),
        d=pl.BlockSpec((1, sbk * S5_CH), lambda i, c: (0, i)),
        h=pl.BlockSpec((None, 2, sbk, 8, 128), lambda i, c: (ch(c), 0, i, 0, 0)),
    )


def _s5_blocks(nb, pref):
    return max(b for b in range(1, pref + 1) if nb % b == 0)


def _s5_group_masks():
    rb = lax.broadcasted_iota(jnp.int32, (S5_CH, S5_W), 0) // S5_C
    qb = lax.broadcasted_iota(jnp.int32, (S5_CH, S5_W), 1) // S5_P
    qc = lax.broadcasted_iota(jnp.int32, (S5_W, S5_CH), 0) // S5_P
    rc = lax.broadcasted_iota(jnp.int32, (S5_W, S5_CH), 1) // S5_C
    return rb == qb, qc == rc


def _s5_expand(bb_ref, cc_ref, bbd, ccd, sbk):
    mask_b, mask_c = _s5_group_masks()
    for k in range(2):
        for s in range(sbk):
            bbd[k, s] = jnp.where(mask_b, jnp.tile(bb_ref[k, s], (1, S5_W // 128)), 0).astype(BF16)
            ccd[k, s] = jnp.where(mask_c, jnp.tile(cc_ref[k, s], (S5_GB, 1)), 0).astype(BF16)


def _s5_fwd(u, bb2, cc2, a_re, a_im, d_skip):
    T, D = u.shape
    nb = D // S5_CH
    sbk = _s5_blocks(nb, 4)
    tc = _tile(T, 512, 8)
    nch = T // tc
    sp = _s5_specs(tc, nch, sbk, False)

    def body(u_ref, bb_ref, cc_ref, ar_ref, ai_ref, d_ref, z_ref, gp_ref, h_ref, xr, xi, cr, ci, bbd, ccd):
        @pl.when(pl.program_id(1) == 0)
        def _():
            cr[...] = jnp.zeros_like(cr)
            ci[...] = jnp.zeros_like(ci)
            _s5_expand(bb_ref, cc_ref, bbd, ccd, sbk)

        h_ref[0] = cr[...]
        h_ref[1] = ci[...]
        for s in range(sbk):
            ub = u_ref[:, s * S5_CH:(s + 1) * S5_CH].astype(BF16)
            _to_state_tiles(xr, s, _dot(ub, bbd[0, s]))
            _to_state_tiles(xi, s, _dot(ub, bbd[1, s]))
        _s5_scan_fwd(xr, xi, ar_ref, ai_ref, cr, ci, tc, sbk)
        for s in range(sbk):
            cols = slice(s * S5_CH, (s + 1) * S5_CH)
            y = (_dot(_from_state_tiles(xr, s, tc).astype(BF16), ccd[0, s])
                 - _dot(_from_state_tiles(xi, s, tc).astype(BF16), ccd[1, s])
                 + d_ref[:, cols] * u_ref[:, cols])
            act, act_grad = _gelu_and_grad(y)
            z_ref[:, cols] = act.astype(BF16)
            gp_ref[:, cols] = act_grad.astype(BF16)

    tiles = pltpu.VMEM((sbk, tc * 8, 128), F32)
    carry = pltpu.VMEM((sbk, 8, 128), F32)
    return pl.pallas_call(
        body, grid=(nb // sbk, nch),
        in_specs=[sp["act"], sp["bb"], sp["cc"], sp["a"], sp["a"], sp["d"]],
        out_specs=[sp["act"], sp["act"], sp["h"]],
        out_shape=[jax.ShapeDtypeStruct((T, D), BF16), jax.ShapeDtypeStruct((T, D), BF16),
                   jax.ShapeDtypeStruct((nch, 2, nb, 8, 128), F32)],
        scratch_shapes=[tiles, tiles, carry, carry, pltpu.VMEM((2, sbk, S5_CH, S5_W), BF16),
                        pltpu.VMEM((2, sbk, S5_W, S5_CH), BF16)],
        compiler_params=_params(("parallel", "arbitrary")), name="s5_fwd",
    )(u, bb2, cc2, a_re, a_im, d_skip)


def _s5_bwd(u, dz, gp, h0, bb2, cc2, a_re, a_im, d_skip, dep=None):
    T, D = u.shape
    nb = D // S5_CH
    sbk = _s5_blocks(nb, 2)
    tc = _tile(T, 512, 8)
    nch = T // tc
    sp = _s5_specs(tc, nch, sbk, True)

    n_dep = 0 if dep is None else 1

    def body(u_ref, dz_ref, gp_ref, h_ref, bb_ref, cc_ref, ar_ref, ai_ref, d_ref, *rest):
        (du_ref, dd_ref, dar_ref, dai_ref, dbb_ref, dcc_ref,
         xr, xi, lr, li, fr, fi, br, bi, accr, acci, bbd, ccd, dbbd, dccd) = rest[n_dep:]
        c = pl.program_id(1)

        @pl.when(c == 0)
        def _():
            for ref in (br, bi, accr, acci, dd_ref, dbbd, dccd):
                ref[...] = jnp.zeros_like(ref)
            _s5_expand(bb_ref, cc_ref, bbd, ccd, sbk)

        for s in range(sbk):
            ub = u_ref[:, s * S5_CH:(s + 1) * S5_CH].astype(BF16)
            _to_state_tiles(xr, s, _dot(ub, bbd[0, s]))
            _to_state_tiles(xi, s, _dot(ub, bbd[1, s]))
        fr[...] = h_ref[0]
        fi[...] = h_ref[1]
        _s5_scan_fwd(xr, xi, ar_ref, ai_ref, fr, fi, tc, sbk)
        for s in range(sbk):
            cols = slice(s * S5_CH, (s + 1) * S5_CH)
            uv = u_ref[:, cols]
            xrb = _from_state_tiles(xr, s, tc).astype(BF16)
            xib = _from_state_tiles(xi, s, tc).astype(BF16)
            dsk = d_ref[:, cols]
            dy = dz_ref[:, cols].astype(F32) * gp_ref[:, cols].astype(F32)
            dd_ref[:, cols] += jnp.sum(dy * uv, axis=0, keepdims=True)
            dyb = dy.astype(BF16)
            dccd[0, s] += _dot(xrb, dyb, TN)
            dccd[1, s] += _dot(xib, dyb, TN)
            _to_state_tiles(lr, s, _dot(dyb, ccd[0, s], NT))
            _to_state_tiles(li, s, -_dot(dyb, ccd[1, s], NT))
            du_ref[:, cols] = dy * dsk
        _s5_scan_bwd(lr, li, xr, xi, h_ref, ar_ref, ai_ref, br, bi, accr, acci, tc, sbk)
        for s in range(sbk):
            cols = slice(s * S5_CH, (s + 1) * S5_CH)
            ub = u_ref[:, cols].astype(BF16)
            lrb = _from_state_tiles(lr, s, tc).astype(BF16)
            lib = _from_state_tiles(li, s, tc).astype(BF16)
            dbbd[0, s] += _dot(ub, lrb, TN)
            dbbd[1, s] += _dot(ub, lib, TN)
            du_ref[:, cols] += _dot(lrb, bbd[0, s], NT) + _dot(lib, bbd[1, s], NT)

        @pl.when(c == nch - 1)
        def _():
            dar_ref[...] = accr[...]
            dai_ref[...] = acci[...]
            mask_b, mask_c = _s5_group_masks()
            for k in range(2):
                for s in range(sbk):
                    mb = jnp.where(mask_b, dbbd[k, s], 0.0)
                    fold = functools.reduce(
                        lambda a, b: a + b, [mb[:, 128 * j:128 * (j + 1)] for j in range(S5_W // 128)])
                    dbb_ref[k, s] = fold + pltpu.roll(fold, S5_P, 1)
                    mc = jnp.where(mask_c, dccd[k, s], 0.0)
                    dcc_ref[k, s] = functools.reduce(
                        lambda a, b: a + b, [mc[S5_P * j:S5_P * (j + 1), :] for j in range(S5_GB)])

    tiles = pltpu.VMEM((sbk, tc * 8, 128), F32)
    carry = pltpu.VMEM((sbk, 8, 128), F32)
    ins = [u, dz, gp, h0, bb2, cc2, a_re, a_im, d_skip]
    in_specs = [sp["act"], sp["act"], sp["act"], sp["h"], sp["bb"], sp["cc"], sp["a"], sp["a"], sp["d"]]
    _dep_operand(ins, in_specs, dep)
    return pl.pallas_call(
        body, grid=(nb // sbk, nch),
        in_specs=in_specs,
        out_specs=[sp["act"], sp["d"], sp["a"], sp["a"], sp["bb"], sp["cc"]],
        out_shape=[jax.ShapeDtypeStruct((T, D), F32), jax.ShapeDtypeStruct((1, D), F32),
                   jax.ShapeDtypeStruct((nb, 8, 128), F32), jax.ShapeDtypeStruct((nb, 8, 128), F32),
                   jax.ShapeDtypeStruct((2, nb, S5_CH, 128), F32), jax.ShapeDtypeStruct((2, nb, S5_P, S5_CH), F32)],
        scratch_shapes=[tiles, tiles, tiles, tiles, carry, carry, carry, carry, carry, carry,
                        pltpu.VMEM((2, sbk, S5_CH, S5_W), BF16), pltpu.VMEM((2, sbk, S5_W, S5_CH), BF16),
                        pltpu.VMEM((2, sbk, S5_CH, S5_W), F32), pltpu.VMEM((2, sbk, S5_W, S5_CH), F32)],
        compiler_params=_params(("parallel", "arbitrary")), name="s5_bwd",
    )(*ins)


def _s5_disc(lr, li, ldt):
    dt = jnp.exp(ldt)
    mag = jnp.exp(lr * dt)
    ang = li * dt
    cs, sn = jnp.cos(ang), jnp.sin(ang)
    lbr, lbi = mag * cs, mag * sn
    nr = lbr - 1.0
    den = lr * lr + li * li
    f_re = (nr * lr + lbi * li) / den
    f_im = (lbi * lr - nr * li) / den
    return dt, mag, cs, sn, lbr, lbi, nr, den, f_re, f_im


def _s5_param_fwd(lr, li, ldt, bt_re, bt_im):
    c, g, p = bt_re.shape

    def body(lr_ref, li_ref, ldt_ref, br_ref, bi_ref, lbr_ref, lbi_ref, bbr_ref, bbi_ref):
        _, _, _, _, lbr, lbi, _, _, f_re, f_im = _s5_disc(lr_ref[...], li_ref[...], ldt_ref[...])
        lbr_ref[...] = lbr
        lbi_ref[...] = lbi
        for ch in range(c):
            b_r, b_i = br_ref[ch], bi_ref[ch]
            bbr_ref[ch] = f_re * b_r - f_im * b_i
            bbi_ref[ch] = f_re * b_i + f_im * b_r

    gp = jax.ShapeDtypeStruct((g, p), F32)
    cgp = jax.ShapeDtypeStruct((c, g, p), F32)
    return pl.pallas_call(body, out_shape=[gp, gp, cgp, cgp], name="s5_param_fwd")(lr, li, ldt, bt_re, bt_im)


def _s5_param_bwd(lr, li, ldt, bt_re, bt_im, dlbr, dlbi, dbbr, dbbi):
    c, g, p = bt_re.shape

    def body(lr_ref, li_ref, ldt_ref, br_ref, bi_ref, dlbr_ref, dlbi_ref, dbbr_ref, dbbi_ref,
             dlr_ref, dli_ref, dldt_ref, dbr_ref, dbi_ref):
        l_r, l_i = lr_ref[...], li_ref[...]
        dt, mag, cs, sn, lbr, lbi, nr, den, f_re, f_im = _s5_disc(l_r, l_i, ldt_ref[...])
        dfr = jnp.zeros_like(l_r)
        dfi = jnp.zeros_like(l_r)
        for ch in range(c):
            b_r, b_i = br_ref[ch], bi_ref[ch]
            g_r, g_i = dbbr_ref[ch], dbbi_ref[ch]
            dbr_ref[ch] = f_re * g_r + f_im * g_i
            dbi_ref[ch] = f_re * g_i - f_im * g_r
            dfr = dfr + g_r * b_r + g_i * b_i
            dfi = dfi + g_i * b_r - g_r * b_i
        inv = 1.0 / den
        d_nr = (dfr * l_r - dfi * l_i) * inv
        d_lbi = (dfr * l_i + dfi * l_r) * inv + dlbi_ref[...]
        d_lbr = d_nr + dlbr_ref[...]
        d_den = -(dfr * f_re + dfi * f_im) * inv
        d_mag = d_lbr * cs + d_lbi * sn
        d_ang = d_lbi * lbr - d_lbr * lbi
        dlr_ref[...] = (dfr * nr + dfi * lbi) * inv + 2.0 * d_den * l_r + d_mag * mag * dt
        dli_ref[...] = (dfr * lbi - dfi * nr) * inv + 2.0 * d_den * l_i + d_ang * dt
        dldt_ref[...] = jnp.sum(d_mag * mag * l_r + d_ang * l_i, axis=1, keepdims=True) * dt

    gp = jax.ShapeDtypeStruct((g, p), F32)
    cgp = jax.ShapeDtypeStruct((c, g, p), F32)
    return pl.pallas_call(body, out_shape=[gp, gp, jax.ShapeDtypeStruct((g, 1), F32), cgp, cgp],
                          name="s5_param_bwd")(lr, li, ldt, bt_re, bt_im, dlbr, dlbi, dbbr, dbbi)


def _att_masks(rep, gb):
    rows = rep * ATT_BLK
    qi = lax.broadcasted_iota(jnp.int32, (rows, 2 * ATT_BLK), 0) % ATT_BLK
    si = lax.broadcasted_iota(jnp.int32, (rows, 2 * ATT_BLK), 1)
    prev = (si < ATT_BLK) & (si >= qi) & (gb > 0)
    cur = (si >= ATT_BLK) & (si - ATT_BLK <= qi)
    return prev | cur


def _att_rows(start, dil):
    return pl.ds(start, ATT_BLK) if dil == 1 else pl.ds(start, ATT_BLK, stride=dil)


def _att_plan(T, dil):
    span = ATT_BLK * dil
    sbr = max(span, min(T, 1024))
    return span, sbr, T // sbr


def _att_block(sb, i, sbr, span, dil):
    loc = (i // dil) * span + i % dil
    cur = sb * sbr + loc
    gb = sb * (sbr // span) + i // dil
    return loc, cur, jnp.where(gb > 0, cur - span, cur), gb


def _att_fwd(q, k, v, grp, dil):
    T = q.shape[0]
    H = q.shape[1] // HEAD_DIM // len(DILATIONS)
    rep = H // N_KV_HEADS
    span, sbr, nsb = _att_plan(T, dil)
    scale = HEAD_DIM ** -0.5

    def body(*refs):
        q_refs = refs[:rep]
        k_ref, v_ref, o_ref, l_ref, o_slab, l_slab = refs[rep:]
        sb = pl.program_id(1)

        def blk(i, _):
            loc, cur, prv, gb = _att_block(sb, i, sbr, span, dil)
            rows = _att_rows(loc, dil)
            qs = jnp.concatenate([r[rows, :] for r in q_refs], axis=0).astype(BF16)
            kcat = jnp.concatenate([k_ref[_att_rows(prv, dil), :], k_ref[_att_rows(cur, dil), :]], axis=0)
            vcat = jnp.concatenate([v_ref[_att_rows(prv, dil), :], v_ref[_att_rows(cur, dil), :]], axis=0)
            s = jnp.where(_att_masks(rep, gb), _dot(qs, kcat.astype(BF16), NT) * scale, NEG_INF)
            m = jnp.max(s, axis=-1, keepdims=True)
            p = jnp.exp(s - m)
            l = jnp.sum(p, axis=-1, keepdims=True)
            o = _dot(p.astype(BF16), vcat.astype(BF16)) / l
            lse = jnp.broadcast_to(m + jnp.log(l), (rep * ATT_BLK, HEAD_DIM))
            for j in range(rep):
                o_slab[j, rows, :] = o[j * ATT_BLK:(j + 1) * ATT_BLK]
                l_slab[j, rows, :] = lse[j * ATT_BLK:(j + 1) * ATT_BLK]
            return 0

        lax.fori_loop(0, sbr // ATT_BLK, blk, 0)
        for j in range(rep):
            o_ref[:, j * HEAD_DIM:(j + 1) * HEAD_DIM] = o_slab[j].astype(BF16)
            l_ref[:, j * HEAD_DIM:(j + 1) * HEAD_DIM] = l_slab[j]

    qspecs = [pl.BlockSpec((sbr, HEAD_DIM), functools.partial(lambda h, s, j: (s, grp * H + h * rep + j), j=j))
              for j in range(rep)]
    kspec = pl.BlockSpec((T, HEAD_DIM), lambda h, s: (0, h))
    ospec = pl.BlockSpec((sbr, rep * HEAD_DIM), lambda h, s: (s, h))
    slab = pltpu.VMEM((rep, sbr, HEAD_DIM), F32)
    return pl.pallas_call(
        body, grid=(N_KV_HEADS, nsb), in_specs=qspecs + [kspec, kspec], out_specs=[ospec, ospec],
        out_shape=[jax.ShapeDtypeStruct((T, H * HEAD_DIM), BF16), jax.ShapeDtypeStruct((T, H * HEAD_DIM), F32)],
        scratch_shapes=[slab, slab],
        compiler_params=_params(("parallel", "arbitrary")), name=f"att_fwd_d{dil}",
    )(*([q] * rep), k, v)


def _att_combine(outs, lses):
    T, W = outs[0].shape
    ng = len(outs)
    tr, tcol = _tile(T, 512, 8), _tile(W, 512)

    def body(*refs):
        o_refs, l_refs = refs[:ng], refs[ng:2 * ng]
        ob_ref, lse_ref = refs[2 * ng:]
        ls = [r[...] for r in l_refs]
        m = functools.reduce(jnp.maximum, ls)
        es = [jnp.exp(l - m) for l in ls]
        den = functools.reduce(lambda a, b: a + b, es)
        num = functools.reduce(lambda a, b: a + b, [e * o[...].astype(F32) for e, o in zip(es, o_refs)])
        ob_ref[...] = (num / den).astype(BF16)
        lse_ref[...] = m + jnp.log(den)

    spec = pl.BlockSpec((tr, tcol), lambda i, j: (i, j))
    return pl.pallas_call(
        body, grid=(T // tr, W // tcol), in_specs=[spec] * (2 * ng), out_specs=[spec, spec],
        out_shape=[jax.ShapeDtypeStruct((T, W), BF16), jax.ShapeDtypeStruct((T, W), F32)],
        compiler_params=_params(("parallel", "parallel")), name="att_combine")(*outs, *lses)


STAT_LANE = HEAD_DIM // 2


def _att_out_bwd(dx, w_o, o, lse, dep=None):
    T, D = dx.shape
    W = w_o.shape[0]
    tm, tn = _tile(T, MM_TILE), _tile(W, MM_TILE)
    n_dep = 0 if dep is None else 1

    def body(dx_ref, w_ref, o_ref, l_ref, *rest):
        do_ref, s_ref = rest[n_dep:]
        do_ref[...] = _dot(dx_ref[...], w_ref[...], NT)
        lane = lax.broadcasted_iota(jnp.int32, (tm, HEAD_DIM), 1)
        for h in range(tn // HEAD_DIM):
            cols = slice(h * HEAD_DIM, (h + 1) * HEAD_DIM)
            delta = jnp.sum(do_ref[:, cols] * o_ref[:, cols].astype(F32), axis=-1, keepdims=True)
            s_ref[:, cols] = jnp.where(lane < STAT_LANE, l_ref[:, cols], delta)

    tile = pl.BlockSpec((tm, tn), lambda i, j: (i, j))
    ins = [dx, w_o, o, lse]
    in_specs = [pl.BlockSpec((tm, D), lambda i, j: (i, 0)), pl.BlockSpec((tn, D), lambda i, j: (j, 0)), tile, tile]
    _dep_operand(ins, in_specs, dep)
    return pl.pallas_call(
        body, grid=(T // tm, W // tn), in_specs=in_specs, out_specs=[tile, tile],
        out_shape=[jax.ShapeDtypeStruct((T, W), F32)] * 2,
        compiler_params=_params(("parallel", "parallel")), name="o_proj_dx")(*ins)


def _att_bwd(q, k, v, do, stats, dq, grp, dil):
    T = q.shape[0]
    H = do.shape[1] // HEAD_DIM
    rep = H // N_KV_HEADS
    hs = rep
    span, sbr, nsb = _att_plan(T, dil)
    scale = HEAD_DIM ** -0.5

    def body(*refs):
        q_refs, do_refs, st_refs = refs[:hs], refs[hs + 2:2 * hs + 2], refs[2 * hs + 2:3 * hs + 2]
        k_ref, v_ref = refs[hs], refs[hs + 1]
        dq_ref, dk_ref, dv_ref, dq_slab = refs[3 * hs + 3:]
        sb = pl.program_id(2)

        @pl.when((pl.program_id(1) == 0) & (sb == 0))
        def _():
            dk_ref[...] = jnp.zeros_like(dk_ref)
            dv_ref[...] = jnp.zeros_like(dv_ref)

        def blk(i, _):
            loc, cur, prv, gb = _att_block(sb, i, sbr, span, dil)
            rows, kc, kp = _att_rows(loc, dil), _att_rows(cur, dil), _att_rows(prv, dil)
            qs = jnp.concatenate([r[rows, :] for r in q_refs], axis=0).astype(BF16)
            dos = jnp.concatenate([r[rows, :] for r in do_refs], axis=0).astype(BF16)
            st = jnp.concatenate([r[rows, :] for r in st_refs], axis=0)
            kcat = jnp.concatenate([k_ref[kp, :], k_ref[kc, :]], axis=0).astype(BF16)
            vcat = jnp.concatenate([v_ref[kp, :], v_ref[kc, :]], axis=0).astype(BF16)
            s = _dot(qs, kcat, NT) * scale
            p = jnp.where(_att_masks(hs, gb), jnp.exp(s - st[:, 0:1]), 0.0)
            dp = _dot(dos, vcat, NT)
            ds = (p * (dp - st[:, STAT_LANE:STAT_LANE + 1]) * scale).astype(BF16)
            dvc = _dot(p.astype(BF16), dos, TN)
            dkc = _dot(ds, qs, TN)
            dqs = _dot(ds, kcat)
            for j in range(hs):
                dq_slab[j, rows, :] = dqs[j * ATT_BLK:(j + 1) * ATT_BLK]
            dk_ref[kc, :] += dkc[ATT_BLK:]
            dv_ref[kc, :] += dvc[ATT_BLK:]

            @pl.when(gb > 0)
            def _():
                dk_ref[kp, :] += dkc[:ATT_BLK]
                dv_ref[kp, :] += dvc[:ATT_BLK]

            return 0

        lax.fori_loop(0, sbr // ATT_BLK, blk, 0)
        for j in range(hs):
            dq_ref[:, j * HEAD_DIM:(j + 1) * HEAD_DIM] = dq_slab[j].astype(BF16)

    def head_specs(col0):
        return [pl.BlockSpec((sbr, HEAD_DIM),
                             functools.partial(lambda h, f, s, j: (s, col0 + h * rep + f * hs + j), j=j))
                for j in range(hs)]

    kspec = pl.BlockSpec((T, HEAD_DIM), lambda h, f, s: (0, h))
    dqspec = pl.BlockSpec((sbr, hs * HEAD_DIM), lambda h, f, s: (s, (grp * H + h * rep) // hs + f))
    n_in = 3 * hs + 3
    return pl.pallas_call(
        body, grid=(N_KV_HEADS, rep // hs, nsb),
        in_specs=head_specs(grp * H) + [kspec, kspec] + head_specs(0) + head_specs(0) + [ANY],
        out_specs=[dqspec, kspec, kspec],
        out_shape=[jax.ShapeDtypeStruct(dq.shape, BF16),
                   jax.ShapeDtypeStruct((T, N_KV_HEADS * HEAD_DIM), F32),
                   jax.ShapeDtypeStruct((T, N_KV_HEADS * HEAD_DIM), F32)],
        scratch_shapes=[pltpu.VMEM((hs, sbr, HEAD_DIM), F32)],
        input_output_aliases={n_in - 1: 0},
        compiler_params=_params(("parallel", "arbitrary", "arbitrary")), name=f"att_bwd_d{dil}",
    )(*([q] * hs), k, v, *([do] * hs), *([stats] * hs), dq)


def _sum_kv(dks, dvs):
    T, W = dks[0].shape
    ng = len(dks)
    tr = _tile(T, 512, 8)

    def body(*refs):
        o_ref = refs[2 * ng]
        o_ref[0] = functools.reduce(lambda a, b: a + b, [r[...] for r in refs[:ng]]).astype(BF16)
        o_ref[1] = functools.reduce(lambda a, b: a + b, [r[...] for r in refs[ng:2 * ng]]).astype(BF16)

    spec = pl.BlockSpec((tr, W), lambda i: (i, 0))
    return pl.pallas_call(
        body, grid=(T // tr,), in_specs=[spec] * (2 * ng),
        out_specs=pl.BlockSpec((2, tr, W), lambda i: (0, i, 0)),
        out_shape=jax.ShapeDtypeStruct((2, T, W), BF16),
        compiler_params=_params(("parallel",)), name="sum_kv")(*dks, *dvs)


def _local_step(x, tgt, w, wts, ready, dep0=None, small_ready=None):
    T, D = x.shape
    g = {}

    (u0,) = _rms_fwd("rms_a", x, [w["a_norm"]], [F32], dep=dep0)
    lbr, lbi, bbt_re, bbt_im = _s5_param_fwd(w["lam_re"], w["lam_im"], w["log_dt"], w["bt_re"], w["bt_im"])
    a_re, a_im = lbr.reshape(-1, 8, 128), lbi.reshape(-1, 8, 128)
    C, G, P = w["bt_re"].shape
    nb = G // S5_GB
    bb2 = jnp.stack([bbt_re, bbt_im]).transpose(0, 2, 1, 3).reshape(2, nb, S5_GB * C, P)
    bb2 = jnp.concatenate([bb2, bb2], axis=-1)
    cc2 = jnp.stack([w["c_re"], w["c_im"]]).reshape(2, nb, S5_GB * C, P).transpose(0, 1, 3, 2)
    z, gp, h0 = _s5_fwd(u0, bb2, cc2, a_re, a_im, w["s5_d"])
    w_glu = wts("glu", z)
    x1, vg = _mm_dual_fwd("glu_fwd", z, w_glu, x, "glu")

    def ffn_fwd(xin, layer):
        (nrm,) = _rms_fwd(f"rms_f{layer}", xin, [w["ffn_norm"][layer:layer + 1]], [BF16])
        w_in = wts(f"win{layer}", nrm)
        gu, act = _mm_dual_fwd(f"ffn_in{layer}", nrm, w_in, None, "ffn")
        w_out = wts(f"wout{layer}", act)
        xout = _mm_nn(f"ffn_out{layer}", act, w_out, res=xin)
        return xout, (nrm, gu, act, w_in, w_out)

    x2, saved0 = ffn_fwd(x1, 0)
    kvn, hb = _rms_fwd("rms_b", x2, [w["kv_norm"], w["b_norm"]], [BF16, BF16])
    w_kv, w_q, w_o = wts("wkv", hb), wts("wq", hb), wts("wo", hb)
    k, v = _mm_kv("kv_proj", kvn, w_kv)
    q = _mm_nn("q_proj", hb, w_q)
    outs, lses = [], []
    for grp, dil in enumerate(DILATIONS):
        o_g, l_g = _att_fwd(q, k, v, grp, dil)
        outs.append(o_g)
        lses.append(l_g)
    o, lse = _att_combine(outs, lses)
    wts("win1", o, prefetch=True)
    x3 = _mm_nn("o_proj", o, w_o, res=x2)
    x4, saved1 = ffn_fwd(x3, 1)
    loss_blk, dx4, dx4b, g["final_norm"] = _loss_head(x4, w["final_norm"], tgt)

    def ffn_bwd(dx, dxb, xin, saved, layer, dep):
        nrm, gu, act, w_in, w_out = saved
        dgu = _mm_nt_ffn_bwd(f"ffn_dact{layer}", dxb, w_out, gu, dep=dep)
        g_wout = _mm_tn(f"ffn_dwout{layer}", act, dxb)
        g_win = _mm_tn_pair(f"ffn_dwin{layer}", nrm, dgu)
        dn = _mm_nt_pair(f"ffn_dn{layer}", dgu, w_in, out_dtype=BF16)
        dxo, dxob, (dgn,) = _rms_bwd(f"rms_f{layer}_bwd", xin, dx, [w["ffn_norm"][layer:layer + 1]], [dn])
        tok = ready({f"win{layer}": g_win, f"wout{layer}": g_wout})
        return dxo, dxob, dgn, tok

    dx3, dx3b, dfn1, tok = ffn_bwd(dx4, dx4b, x3, saved1, 1, None)
    do, stats = _att_out_bwd(dx3b, w_o, o, lse, dep=tok)
    g_wo = _mm_tn("o_proj_dw", o, dx3b)
    dq = lax.empty(q.shape, BF16)
    dks, dvs = [], []
    for grp, dil in enumerate(DILATIONS):
        dq, dk_g, dv_g = _att_bwd(q, k, v, do, stats, dq, grp, dil)
        dks.append(dk_g)
        dvs.append(dv_g)
    dkv = _sum_kv(dks, dvs)
    dhb = _mm_nt("q_proj_dx", [dq], w_q, out_dtype=BF16)
    g_wq = _mm_tn("q_proj_dw", hb, dq)
    dkvn = _mm_nt_pair("kv_proj_dx", dkv, w_kv, out_dtype=BF16)
    g_wkv = _mm_tn_pair("kv_proj_dw", kvn, dkv)
    dx2, dx2b, (g["kv_norm"], g["b_norm"]) = _rms_bwd(
        "rms_b_bwd", x2, dx3, [w["kv_norm"], w["b_norm"]], [dkvn, dhb])
    tok = ready({"wkv": g_wkv, "wq": g_wq, "wo": g_wo})
    dx1, dx1b, dfn0, tok = ffn_bwd(dx2, dx2b, x1, saved0, 0, tok)
    g["ffn_norm"] = jnp.concatenate([dfn0, dfn1], axis=0)

    dvg = _glu_bwd(dx1, vg, dep=tok)
    dz = _mm_nt_pair("glu_dx", dvg, w_glu, out_dtype=BF16)
    tok = ready({"glu": _mm_tn_pair("glu_dw", z, dvg)})
    tok = ready({}, after=tok)
    du, g["s5_d"], da_re, da_im, dbb2, dcc2 = _s5_bwd(
        u0, dz, gp, h0, bb2, cc2, a_re, a_im, w["s5_d"], dep=tok)
    dcc = dcc2.transpose(0, 1, 3, 2).reshape(2, G, C, P)
    g["c_re"], g["c_im"] = dcc[0], -dcc[1]
    dbbt = dbb2[..., :P].reshape(2, G, C, P).transpose(0, 2, 1, 3)
    g["lam_re"], g["lam_im"], g["log_dt"], g["bt_re"], g["bt_im"] = _s5_param_bwd(
        w["lam_re"], w["lam_im"], w["log_dt"], w["bt_re"], w["bt_im"],
        da_re.reshape(G, P), da_im.reshape(G, P), dbbt[0], dbbt[1])
    tok = small_ready(g) if small_ready is not None else None
    grad_x, _, (g["a_norm"],) = _rms_bwd("rms_a_bwd", x, dx1, [w["a_norm"]], [du], dep=tok)
    return loss_blk, grad_x, g


def _coords():
    return lax.axis_index("x"), lax.axis_index("y"), lax.axis_index("c")


def _dev_index(dev):
    return 4 * dev[0] + 2 * dev[1] + dev[2]


def _shard_window(ref, axis, width, idx):
    sl = [slice(None)] * len(ref.shape)
    sl[axis] = pl.ds(pl.multiple_of(idx * width, width), width)
    return ref.at[tuple(sl)]


def _all_gather(name, shards, axes):
    na = len(shards)
    widths = [s.shape[ax] for s, ax in zip(shards, axes)]
    out_shapes = []
    for s, ax in zip(shards, axes):
        shp = list(s.shape)
        shp[ax] *= N_DEV
        out_shapes.append(jax.ShapeDtypeStruct(tuple(shp), s.dtype))

    def body(*refs):
        ins, outs = refs[:na], refs[na:2 * na]
        send_sems, recv_sems, local_sems = refs[2 * na:]
        x, y, c = _coords()
        me, sib = (x, y, c), (x, y, 1 - c)
        chips = [(1 - x, y), (x, 1 - y), (1 - x, 1 - y)]

        def blk(a, dev):
            return _shard_window(outs[a], axes[a], widths[a], _dev_index(dev))

        def copy(a, kk, block, to, src=None):
            return pltpu.make_async_remote_copy(
                src_ref=blk(a, block) if src is None else src, dst_ref=blk(a, block),
                send_sem=send_sems.at[a, kk], recv_sem=recv_sems.at[a, kk],
                device_id=to, device_id_type=MESH)

        local = [pltpu.make_async_copy(ins[a], blk(a, me), local_sems.at[a]) for a in range(na)]
        for cp in local:
            cp.start()
        sent = []
        for a in range(na):
            first = [copy(a, 0, me, sib, src=ins[a])]
            first += [copy(a, 1 + j, me, (*chip, c), src=ins[a]) for j, chip in enumerate(chips)]
            for cp in first:
                cp.start()
            sent += first
        for a in range(na):
            for j, chip in enumerate(chips):
                copy(a, 1 + j, (*chip, c), me).wait_recv()
                fwd = copy(a, 4 + j, (*chip, c), sib)
                fwd.start()
                sent.append(fwd)
        for a in range(na):
            copy(a, 0, sib, me).wait_recv()
            for j, chip in enumerate(chips):
                copy(a, 4 + j, (*chip, 1 - c), me).wait_recv()
        for cp in sent:
            cp.wait_send()
        for cp in local:
            cp.wait()

    return pl.pallas_call(
        body, out_shape=out_shapes, in_specs=[ANY] * na, out_specs=[ANY] * na,
        scratch_shapes=[pltpu.SemaphoreType.DMA((na, 7)), pltpu.SemaphoreType.DMA((na, 7)),
                        pltpu.SemaphoreType.DMA((na,))],
        name=name)(*shards)


HBM = pl.BlockSpec(memory_space=pltpu.HBM)
SEM = pl.BlockSpec(memory_space=pltpu.SEMAPHORE)
TOKEN_SPEC = pl.BlockSpec(memory_space=pltpu.VMEM)
TOKEN_SHAPE = jax.ShapeDtypeStruct((8, 128), F32)
SPLIT_PARAMS = pltpu.CompilerParams(has_side_effects=pltpu.SideEffectType.DATAFLOW_SIDE_EFFECTING)


def _hbm(x):
    return pltpu.with_memory_space_constraint(x, pltpu.HBM)


def _hbm_like(x):
    return pltpu.HBM(x.shape, x.dtype)


def _dma_sems(*shape):
    return pltpu.SemaphoreType.DMA(shape)


def _cast_and_place(name, shard, layer, axis, pos, dtype):
    rows, cols = shard.shape[-2:]
    tr = _tile(rows, 256, 16)
    nt = rows // tr
    full = (rows, cols * N_DEV) if axis == 1 else (rows * N_DEV, cols)

    def dev(p):
        return 4 * p[0] + 2 * p[1] + p[2]

    def body(pos_ref, s_ref, b_ref, l_ref):
        v = s_ref[...].astype(dtype)
        b_ref[...] = v
        l_ref[...] = v

    blk = pl.BlockSpec((tr, cols), lambda i, p: (i, 0))
    if axis == 1:
        lspec = pl.BlockSpec((tr, cols), lambda i, p: (i, dev(p)))
    else:
        lspec = pl.BlockSpec((tr, cols), lambda i, p: (dev(p) * nt + i, 0))
    return pl.pallas_call(
        body, grid_spec=pltpu.PrefetchScalarGridSpec(
            num_scalar_prefetch=1, grid=(nt,),
            in_specs=[pl.BlockSpec((None, tr, cols), lambda i, p: (layer, i, 0))], out_specs=[blk, lspec]),
        out_shape=[jax.ShapeDtypeStruct((rows, cols), dtype), jax.ShapeDtypeStruct(full, dtype)],
        compiler_params=_params(("parallel",)), name=name)(pos, shard)


def _gather_start(name, shards, lands, axes, groups):
    na, ng = len(shards), len(groups)
    widths = [s.shape[ax] for s, ax in zip(shards, axes)]

    def body(*refs):
        sh, ld = refs[:na], refs[na:2 * na]
        sems = refs[2 * na:2 * na + 3 * ng]
        token = refs[-1]
        x, y, c = _coords()
        me, sib = (x, y, c), (x, y, 1 - c)
        chips = [(1 - x, y), (x, 1 - y), (1 - x, 1 - y)]
        for gi, grp in enumerate(groups):
            send, r_d2d, r_ici = sems[3 * gi:3 * gi + 3]
            for li, a in enumerate(grp):
                dst = _shard_window(ld[a], axes[a], widths[a], _dev_index(me))
                pltpu.make_async_remote_copy(
                    src_ref=sh[a], dst_ref=dst, send_sem=send.at[4 * li], recv_sem=r_d2d.at[li],
                    device_id=sib, device_id_type=MESH).start()
                for j, chip in enumerate(chips):
                    pltpu.make_async_remote_copy(
                        src_ref=sh[a], dst_ref=dst, send_sem=send.at[4 * li + 1 + j], recv_sem=r_ici.at[3 * li + j],
                        device_id=(*chip, c), device_id_type=MESH).start()
        token[...] = jnp.zeros_like(token)

    out_shape, out_specs = [], []
    for grp in groups:
        out_shape += [_dma_sems(4 * len(grp)), _dma_sems(len(grp)), _dma_sems(3 * len(grp))]
        out_specs += [SEM] * 3
    out_shape += [_hbm_like(s) for s in shards] + [_hbm_like(l) for l in lands] + [TOKEN_SHAPE]
    out_specs += [HBM] * (2 * na) + [TOKEN_SPEC]
    aliases = {a: 3 * ng + a for a in range(2 * na)}
    res = pl.pallas_call(
        body, name=name, out_shape=out_shape, in_specs=[HBM] * (2 * na),
        out_specs=out_specs, input_output_aliases=aliases, compiler_params=SPLIT_PARAMS,
    )(*[_hbm(s) for s in shards], *[_hbm(l) for l in lands])
    sems = [tuple(res[3 * gi:3 * gi + 3]) for gi in range(ng)]
    return sems, list(res[3 * ng:3 * ng + na]), list(res[3 * ng + na:3 * ng + 2 * na]), res[-1]


def _gather_forward(name, lands, axes, r_ici, after):
    n = len(lands)
    widths = [l.shape[ax] // N_DEV for l, ax in zip(lands, axes)]

    def body(*refs):
        ld, r_ici_ref = refs[:n], refs[n]
        f_send, f_recv = refs[n + 2], refs[n + 3]
        x, y, c = _coords()
        sib = (x, y, 1 - c)
        chips = [(1 - x, y), (x, 1 - y), (1 - x, 1 - y)]
        for li in range(n):
            for j, chip in enumerate(chips):
                blk = _shard_window(ld[li], axes[li], widths[li], _dev_index((*chip, c)))
                pltpu.make_async_remote_copy(
                    src_ref=blk, dst_ref=blk, send_sem=f_send.at[3 * li + j], recv_sem=r_ici_ref.at[3 * li + j],
                    device_id=(*chip, c), device_id_type=MESH).wait_recv()
                pltpu.make_async_remote_copy(
                    src_ref=blk, dst_ref=blk, send_sem=f_send.at[3 * li + j], recv_sem=f_recv.at[3 * li + j],
                    device_id=sib, device_id_type=MESH).start()

    res = pl.pallas_call(
        body, name=name, out_shape=[_dma_sems(3 * n), _dma_sems(3 * n)] + [_hbm_like(l) for l in lands],
        in_specs=[HBM] * n + [SEM, ANY], out_specs=[SEM, SEM] + [HBM] * n,
        input_output_aliases={li: 2 + li for li in range(n)}, compiler_params=SPLIT_PARAMS,
    )(*lands, r_ici, after)
    return res[0], res[1], list(res[2:])


def _gather_finish(name, shards, lands, axes, send, r_d2d, f_send, f_recv, after):
    n = len(lands)
    widths = [l.shape[ax] // N_DEV for l, ax in zip(lands, axes)]

    def body(*refs):
        sh, ld = refs[:n], refs[n:2 * n]
        send_ref, r_d2d_ref, f_send_ref, f_recv_ref = refs[2 * n:2 * n + 4]
        x, y, c = _coords()
        me, sib = (x, y, c), (x, y, 1 - c)
        chips = [(1 - x, y), (x, 1 - y), (1 - x, 1 - y)]

        def blk(li, dev):
            return _shard_window(ld[li], axes[li], widths[li], _dev_index(dev))

        for li in range(n):
            for kk in range(4):
                pltpu.make_async_remote_copy(
                    src_ref=sh[li], dst_ref=blk(li, me), send_sem=send_ref.at[4 * li + kk], recv_sem=r_d2d_ref.at[li],
                    device_id=sib, device_id_type=MESH).wait_send()
            pltpu.make_async_remote_copy(
                src_ref=blk(li, sib), dst_ref=blk(li, sib), send_sem=send_ref.at[4 * li], recv_sem=r_d2d_ref.at[li],
                device_id=sib, device_id_type=MESH).wait_recv()
            for j, chip in enumerate(chips):
                pltpu.make_async_remote_copy(
                    src_ref=blk(li, (*chip, c)), dst_ref=blk(li, (*chip, c)), send_sem=f_send_ref.at[3 * li + j],
                    recv_sem=f_recv_ref.at[3 * li + j], device_id=sib, device_id_type=MESH).wait_send()
                pltpu.make_async_remote_copy(
                    src_ref=blk(li, (*chip, 1 - c)), dst_ref=blk(li, (*chip, 1 - c)), send_sem=f_send_ref.at[3 * li + j],
                    recv_sem=f_recv_ref.at[3 * li + j], device_id=sib, device_id_type=MESH).wait_recv()

    res = pl.pallas_call(
        body, name=name, out_shape=[_hbm_like(s) for s in shards] + [_hbm_like(l) for l in lands],
        in_specs=[HBM] * (2 * n) + [SEM] * 4 + [ANY], out_specs=[HBM] * (2 * n),
        input_output_aliases={i: i for i in range(2 * n)}, compiler_params=SPLIT_PARAMS,
    )(*shards, *lands, send, r_d2d, f_send, f_recv, after)
    return list(res[n:])


def _chip_exchange_start(name, parts):
    n = len(parts)

    def body(*refs):
        src, ld = refs[:n], refs[n:2 * n]
        send, recv = refs[2 * n], refs[2 * n + 1]
        token = refs[-1]
        x, y, c = _coords()
        chips = [(1 - x, y), (x, 1 - y), (1 - x, 1 - y)]
        for li in range(n):
            for kk, chip in enumerate(chips):
                pltpu.make_async_remote_copy(
                    src_ref=src[li].at[kk], dst_ref=ld[li].at[kk], send_sem=send.at[3 * li + kk],
                    recv_sem=recv.at[3 * li + kk], device_id=(*chip, c), device_id_type=MESH).start()
        token[...] = jnp.zeros_like(token)

    lands = [lax.empty(p.shape, p.dtype) for p in parts]
    res = pl.pallas_call(
        body, name=name,
        out_shape=[_dma_sems(3 * n), _dma_sems(3 * n)] + [_hbm_like(p) for p in parts] * 2 + [TOKEN_SHAPE],
        in_specs=[HBM] * (2 * n), out_specs=[SEM, SEM] + [HBM] * (2 * n) + [TOKEN_SPEC],
        input_output_aliases={i: 2 + i for i in range(2 * n)}, compiler_params=SPLIT_PARAMS,
    )(*[_hbm(p) for p in parts], *[_hbm(l) for l in lands])
    return res[0], res[1], list(res[2:2 + n]), list(res[2 + n:2 + 2 * n]), res[-1]


def _chip_exchange_finish(name, started, after):
    counts = [len(st[2]) for st in started]
    total = sum(counts)
    ns = len(started)

    def body(*refs):
        src, ld = refs[:total], refs[total:2 * total]
        sems = refs[2 * total:2 * total + 2 * ns]
        x, y, c = _coords()
        chips = [(1 - x, y), (x, 1 - y), (1 - x, 1 - y)]
        off = 0
        for si, cnt in enumerate(counts):
            send, recv = sems[2 * si], sems[2 * si + 1]
            for li in range(cnt):
                for kk, chip in enumerate(chips):
                    cp = pltpu.make_async_remote_copy(
                        src_ref=src[off + li].at[kk], dst_ref=ld[off + li].at[kk], send_sem=send.at[3 * li + kk],
                        recv_sem=recv.at[3 * li + kk], device_id=(*chip, c), device_id_type=MESH)
                    cp.wait_send()
                    cp.wait_recv()
            off += cnt

    srcs = [p for st in started for p in st[2]]
    lands = [l for st in started for l in st[3]]
    sems = [s for st in started for s in st[:2]]
    res = pl.pallas_call(
        body, name=name, out_shape=[_hbm_like(p) for p in srcs + lands],
        in_specs=[HBM] * (2 * total) + [SEM] * (2 * ns) + [ANY], out_specs=[HBM] * (2 * total),
        input_output_aliases={i: i for i in range(2 * total)}, compiler_params=SPLIT_PARAMS,
    )(*srcs, *lands, *sems, after)
    out, off = [], total
    for cnt in counts:
        out.append(list(res[off:off + cnt]))
        off += cnt
    return out


def _pair_copies(src, ld, send, recv, axes, widths):
    x, y, c = _coords()
    chips = [(x, y), (1 - x, y), (x, 1 - y), (1 - x, 1 - y)]
    return [pltpu.make_async_remote_copy(
        src_ref=_shard_window(src[li], axes[li], widths[li], _dev_index((*chip, 1 - c))),
        dst_ref=ld[li].at[kk], send_sem=send.at[4 * li + kk], recv_sem=recv.at[4 * li + kk],
        device_id=(x, y, 1 - c), device_id_type=MESH)
        for li in range(len(src)) for kk, chip in enumerate(chips)]


def _pair_exchange_start(name, grads, axes):
    n = len(grads)
    widths = [gr.shape[ax] // N_DEV for gr, ax in zip(grads, axes)]
    lands = []
    for gr, ax, wd in zip(grads, axes, widths):
        shp = list(gr.shape)
        shp[ax] = wd
        lands.append(lax.empty((4, *shp), gr.dtype))

    def body(*refs):
        for cp in _pair_copies(refs[:n], refs[n:2 * n], refs[2 * n], refs[2 * n + 1], axes, widths):
            cp.start()
        refs[-1][...] = jnp.zeros_like(refs[-1])

    res = pl.pallas_call(
        body, name=name,
        out_shape=[_dma_sems(4 * n), _dma_sems(4 * n)] + [_hbm_like(a) for a in grads + lands] + [TOKEN_SHAPE],
        in_specs=[HBM] * (2 * n), out_specs=[SEM, SEM] + [HBM] * (2 * n) + [TOKEN_SPEC],
        input_output_aliases={i: 2 + i for i in range(2 * n)}, compiler_params=SPLIT_PARAMS,
    )(*[_hbm(a) for a in grads + lands])
    return res[0], res[1], list(res[2:2 + n]), list(res[2 + n:2 + 2 * n]), res[-1]


def _pair_exchange_finish(name, started, axes, after):
    send, recv, grads, lands, _ = started
    n = len(grads)
    widths = [gr.shape[ax] // N_DEV for gr, ax in zip(grads, axes)]

    def body(*refs):
        for cp in _pair_copies(refs[:n], refs[n:2 * n], refs[2 * n], refs[2 * n + 1], axes, widths):
            cp.wait_send()
            cp.wait_recv()

    res = pl.pallas_call(
        body, name=name, out_shape=[_hbm_like(a) for a in grads + lands],
        in_specs=[HBM] * (2 * n) + [SEM, SEM, ANY], out_specs=[HBM] * (2 * n),
        input_output_aliases={i: i for i in range(2 * n)}, compiler_params=SPLIT_PARAMS,
    )(*grads, *lands, send, recv, after)
    return list(res[:n]), list(res[n:])


def _pair_sum(name, grad, land, axis, pos):
    wd = grad.shape[axis] // N_DEV
    shard_shape = land.shape[1:]
    rows, cols = shard_shape
    tr = _tile(rows, 256, 16)
    nt = rows // tr

    def dev_of(kk, pos_ref):
        return 4 * (pos_ref[0] ^ (kk & 1)) + 2 * (pos_ref[1] ^ (kk >> 1)) + pos_ref[2]

    def gspec(kk):
        if axis == 1:
            return pl.BlockSpec((tr, wd), lambda t, p: (t, dev_of(kk, p)))
        return pl.BlockSpec((tr, cols), lambda t, p: (dev_of(kk, p) * nt + t, 0))

    def body(pos_ref, g0, g1, g2, g3, l_ref, p0_ref, pb_ref):
        p0_ref[...] = g0[...].astype(F32) + l_ref[0].astype(F32)
        for kk, g_ref in enumerate((g1, g2, g3)):
            pb_ref[kk] = (g_ref[...].astype(F32) + l_ref[kk + 1].astype(F32)).astype(BF16)

    return pl.pallas_call(
        body,
        grid_spec=pltpu.PrefetchScalarGridSpec(
            num_scalar_prefetch=1, grid=(nt,),
            in_specs=[gspec(kk) for kk in range(4)] + [pl.BlockSpec((4, tr, cols), lambda t, p: (0, t, 0))],
            out_specs=[pl.BlockSpec((tr, cols), lambda t, p: (t, 0)),
                       pl.BlockSpec((3, tr, cols), lambda t, p: (0, t, 0))]),
        out_shape=[jax.ShapeDtypeStruct(shard_shape, F32), jax.ShapeDtypeStruct((3, *shard_shape), BF16)],
        compiler_params=_params(("parallel",)), name=name)(pos, grad, grad, grad, grad, land)


def _adamw_math(g, w, m, v):
    c1 = 1.0 - ADAM_B1 ** ADAM_STEP
    c2 = 1.0 - ADAM_B2 ** ADAM_STEP
    nm = ADAM_B1 * m + (1.0 - ADAM_B1) * g
    nv = ADAM_B2 * v + (1.0 - ADAM_B2) * (g * g)
    return -ADAM_LR * ((nm / c1) / (jnp.sqrt(nv / c2) + ADAM_EPS) + ADAM_WD * w), nm, nv


def _sum_parts(name, parts):
    n, rows, cols = parts.shape
    tr = _tile(rows, 1024, 8)

    def body(p_ref, o_ref):
        o_ref[...] = functools.reduce(lambda a, b: a + b, [p_ref[i] for i in range(n)])

    return pl.pallas_call(
        body, grid=(rows // tr,), in_specs=[pl.BlockSpec((n, tr, cols), lambda i: (0, i, 0))],
        out_specs=pl.BlockSpec((tr, cols), lambda i: (i, 0)), out_shape=jax.ShapeDtypeStruct((rows, cols), F32),
        compiler_params=_params(("parallel",)), name=name)(parts)


def _adamw_multi(name, gs, ws, ms, vs):
    k = len(gs)
    rows, cols = ws[0].shape
    tr = _tile(rows, 1024, 8)

    def body(*refs):
        for i in range(k):
            g_ref, w_ref, m_ref, v_ref = (refs[j * k + i] for j in range(4))
            og_ref, d_ref, nm_ref, nv_ref = (refs[(4 + j) * k + i] for j in range(4))
            g = g_ref[...]
            og_ref[...] = g
            d_ref[...], nm_ref[...], nv_ref[...] = _adamw_math(g, w_ref[...], m_ref[...], v_ref[...])

    spec = pl.BlockSpec((tr, cols), lambda i: (i, 0))
    res = pl.pallas_call(
        body, grid=(rows // tr,), in_specs=[spec] * (4 * k), out_specs=[spec] * (4 * k),
        out_shape=[jax.ShapeDtypeStruct((rows, cols), F32)] * (4 * k),
        compiler_params=_params(("parallel",)), name=name)(*gs, *ws, *ms, *vs)
    return [[res[j * k + i] for j in range(4)] for i in range(k)]


def _adamw(name, parts, w, m, v, layer=None, prev=None):
    rows, cols = w.shape[-2:]
    tr = _tile(rows, 256, 8)
    npart = len(parts)

    def body(*refs):
        p_refs = refs[:npart]
        w_ref, m_ref, v_ref = refs[npart:npart + 3]
        g_ref, d_ref, nm_ref, nv_ref = refs[-4:]
        g = None
        for r in p_refs:
            if len(r.shape) == 3:
                for i in range(r.shape[0]):
                    t = r[i].astype(F32)
                    g = t if g is None else g + t
            else:
                t = r[...].astype(F32)
                g = t if g is None else g + t
        g_ref[...] = g
        d_ref[...], nm_ref[...], nv_ref[...] = _adamw_math(g, w_ref[...], m_ref[...], v_ref[...])

    spec = pl.BlockSpec((tr, cols), lambda i: (i, 0))
    wspec = spec if layer is None else pl.BlockSpec((None, tr, cols), lambda i: (layer, i, 0))
    pspecs = [pl.BlockSpec((p.shape[0], tr, cols), lambda i: (0, i, 0)) if p.ndim == 3 else spec
              for p in parts]
    prev = list(prev) if prev else []
    return pl.pallas_call(
        body, grid=(rows // tr,), in_specs=pspecs + [wspec] * 3 + [ANY] * len(prev), out_specs=[wspec] * 4,
        out_shape=[jax.ShapeDtypeStruct(w.shape, F32)] * 4,
        input_output_aliases={npart + 3 + i: i for i in range(len(prev))},
        compiler_params=_params(("parallel",)), name=name)(*parts, w, m, v, *prev)


SMALL_NAMES = ("s5_lam_re", "s5_lam_im", "s5_log_dt", "s5_b_re", "s5_b_im", "s5_c_re", "s5_c_im",
               "ffn_norm", "b_norm_mix", "kv_norm", "final_norm")
SMALL_PAD = 1024
SMALL_GROUPS = (("s5_lam_re", "s5_lam_im"), ("s5_log_dt",), ("s5_b_re", "s5_b_im"), ("s5_c_re", "s5_c_im"),
                ("ffn_norm",), ("b_norm_mix", "kv_norm", "final_norm"))


def _pack(parts):
    flat = []
    for p in parts:
        f = p.reshape(-1)
        pad = (-f.shape[0]) % SMALL_PAD
        if pad:
            f = jnp.concatenate([f, jnp.zeros((pad,), f.dtype)])
        flat.append(f)
    return jnp.concatenate(flat).reshape(-1, 128)


def _unpack(packed, shapes):
    flat = packed.reshape(-1)
    out, off = [], 0
    for shp in shapes:
        size = math.prod(shp)
        out.append(flat[off:off + size].reshape(shp))
        off += size + (-size) % SMALL_PAD
    return out


def kernel(x, s5_lam_re, s5_lam_im, s5_log_dt, s5_b_re, s5_b_im, s5_c_re, s5_c_im, s5_d, s5_w_glu, a_norm_mix, ffn_norm, ffn_w_in, ffn_w_out, b_norm_mix, attn_w_q, attn_w_o, kv_norm, w_kv, final_norm, loss_target, m_s5_lam_re, m_s5_lam_im, m_s5_log_dt, m_s5_b_re, m_s5_b_im, m_s5_c_re, m_s5_c_im, m_s5_d, m_s5_w_glu, m_a_norm_mix, m_ffn_norm, m_ffn_w_in, m_ffn_w_out, m_b_norm_mix, m_attn_w_q, m_attn_w_o, m_kv_norm, m_w_kv, m_final_norm, v_s5_lam_re, v_s5_lam_im, v_s5_log_dt, v_s5_b_re, v_s5_b_im, v_s5_c_re, v_s5_c_im, v_s5_d, v_s5_w_glu, v_a_norm_mix, v_ffn_norm, v_ffn_w_in, v_ffn_w_out, v_b_norm_mix, v_attn_w_q, v_attn_w_o, v_kv_norm, v_w_kv, v_final_norm):
    args = dict(locals())
    T, D = x.shape[1], x.shape[2]
    n_layers = ffn_w_in.shape[0]
    xi_, yi_, ci_ = _coords()
    pos = jnp.stack([xi_, yi_, ci_]).astype(jnp.int32)
    me = 4 * xi_ + 2 * yi_ + ci_

    big_names = ["glu"] + [f"win{l}" for l in range(n_layers)] + [f"wout{l}" for l in range(n_layers)] \
        + ["wkv", "wq", "wo"]
    big_shards = [(s5_w_glu, 0)] + [(ffn_w_in, l) for l in range(n_layers)] \
        + [(ffn_w_out, l) for l in range(n_layers)] + [(w_kv[None], 0), (attn_w_q, 0), (attn_w_o, 0)]
    big_axes = [1] + [1] * n_layers + [0] * n_layers + [0, 1, 0]
    big_out_names = ["s5_w_glu"] + ["ffn_w_in"] * n_layers + ["ffn_w_out"] * n_layers \
        + ["w_kv", "attn_w_q", "attn_w_o"]
    index_of = {n: i for i, n in enumerate(big_names)}
    vec_shard = jnp.concatenate([s5_d, a_norm_mix], axis=0)
    (vecs,) = _all_gather("vectors_all_gather", [vec_shard], [1])

    gather_groups = [["glu"], ["win0"], ["wout0"], ["wkv", "wq", "wo"], ["win1", "wout1"]]
    group_idx = [[index_of[n] for n in grp] for grp in gather_groups]
    group_of = {n: gi for gi, grp in enumerate(gather_groups) for n in grp}
    placed = [_cast_and_place(f"cast_place_{n}", s, l, ax, pos, BF16)
              for n, (s, l), ax in zip(big_names, big_shards, big_axes)]
    gather_sems, shards_thru, lands_thru, start_token = _gather_start(
        "weights_gather_start", [p[0] for p in placed], [p[1] for p in placed], big_axes, group_idx)
    full = {}

    forwarded = {}

    def wts(name, after, prefetch=False):
        gi = group_of[name]
        idx = group_idx[gi]
        axes = [big_axes[a] for a in idx]
        send, r_d2d, r_ici = gather_sems[gi]
        if gi not in forwarded:
            forwarded[gi] = _gather_forward(
                f"weights_gather_forward{gi}", [lands_thru[a] for a in idx], axes, r_ici, after)
        if prefetch:
            return None
        if name not in full:
            f_send, f_recv, lands = forwarded[gi]
            done = _gather_finish(f"weights_gather_finish{gi}", [shards_thru[a] for a in idx], lands, axes,
                                  send, r_d2d, f_send, f_recv, after)
            full.update(zip(gather_groups[gi], done))
        return full[name]

    exchanges, pending = [], []

    def flush(after):
        names, axes, started = pending.pop()
        tag = "_".join(names)
        grads, lands = _pair_exchange_finish(f"rs_pair_exchange_finish_{tag}", started, axes, after)
        p0s, pbs = [], []
        for n, gr, land, ax in zip(names, grads, lands, axes):
            p0, pb = _pair_sum(f"rs_pair_sum_{n}", gr, land, ax, pos)
            p0s.append(p0)
            pbs.append(pb)
        started = _chip_exchange_start(f"rs_chip_exchange_start_{tag}", pbs)
        exchanges.append((names, p0s, started))
        return started[4]

    def ready(grads, after=None):
        if not grads:
            return flush(after)
        names = list(grads)
        axes = [big_axes[index_of[n]] for n in names]
        started = _pair_exchange_start(f"rs_pair_exchange_start_{'_'.join(names)}", [grads[n] for n in names], axes)
        token = flush(started[4]) if pending else started[4]
        pending.append((names, axes, started))
        return token

    G, P, C = s5_b_re.shape[1:]
    w = dict(
        a_norm=vecs[1:2], s5_d=vecs[0:1],
        ffn_norm=ffn_norm, b_norm=b_norm_mix, kv_norm=kv_norm.reshape(1, D), final_norm=final_norm.reshape(1, D),
        lam_re=s5_lam_re[0], lam_im=s5_lam_im[0], log_dt=s5_log_dt.reshape(G, 1),
        bt_re=s5_b_re[0].transpose(2, 0, 1), bt_im=s5_b_im[0].transpose(2, 0, 1),
        c_re=s5_c_re[0], c_im=s5_c_im[0],
    )
    small = {}

    def small_ready(g):
        small_g = dict(
            s5_lam_re=g["lam_re"], s5_lam_im=g["lam_im"], s5_log_dt=g["log_dt"],
            s5_b_re=g["bt_re"].transpose(1, 2, 0), s5_b_im=g["bt_im"].transpose(1, 2, 0),
            s5_c_re=g["c_re"], s5_c_im=g["c_im"], ffn_norm=g["ffn_norm"], b_norm_mix=g["b_norm"],
            kv_norm=g["kv_norm"], final_norm=g["final_norm"])
        packed = _pack([small_g[n] for n in SMALL_NAMES])
        shard, land = _cast_and_place("place_small_grads", packed[None], 0, 0, pos, F32)
        sems, thru, lands, token = _gather_start("small_grads_gather_start", [shard], [land], [0], [[0]])
        small.update(sems=sems[0], thru=thru, lands=lands, rows=packed.shape[0])
        return token

    loss_blk, grad_x, g = _local_step(x[0], loss_target[0], w, wts, ready, dep0=start_token,
                                      small_ready=small_ready)
    loss = lax.psum(loss_blk[0, 0], ("x", "y", "c"))
    (tail,) = _all_gather("vector_grads_all_gather", [_pack([g["s5_d"], g["a_norm"]])], [0])

    out = {}

    def put(name, res, shape):
        for kind, r in zip(("grad", "delta", "new_m", "new_v"), res):
            out[f"{kind}_{name}"] = r.reshape(shape)

    updated = {}

    def update(names, p0s, recvd):
        for name, p0, rc in zip(names, p0s, recvd):
            oname = big_out_names[index_of[name]]
            w3, layer = big_shards[index_of[name]]
            updated[oname] = _adamw(f"adamw_{name}", [p0, rc], w3, args["m_" + oname].reshape(w3.shape),
                                    args["v_" + oname].reshape(w3.shape), layer=layer, prev=updated.get(oname))
        return updated[big_out_names[index_of[names[0]]]][0]

    early, last = exchanges[:-1], exchanges[-1]
    landed = _chip_exchange_finish("rs_chip_exchange_finish_early", [e[2] for e in early], grad_x)
    for (names, p0s, _), recvd in zip(early[:-1], landed[:-1]):
        marker = update(names, p0s, recvd)
    send, r_d2d, r_ici = small["sems"]
    f_send, f_recv, small_lands = _gather_forward("small_grads_gather_forward", small["lands"], [0], r_ici, marker)
    marker = update(early[-1][0], early[-1][1], landed[-1])
    (all_parts,) = _gather_finish("small_grads_gather_finish", small["thru"], small_lands, [0],
                                  send, r_d2d, f_send, f_recv, marker)
    (recvd,) = _chip_exchange_finish("rs_chip_exchange_finish_last", [last[2]], marker)
    update(last[0], last[1], recvd)
    for oname, res in updated.items():
        put(oname, res, args[oname].shape)
    g_small = _sum_parts("small_grads_sum", all_parts.reshape(N_DEV, small["rows"], 128))
    g_small = dict(zip(SMALL_NAMES, _unpack(g_small, [args[n].shape for n in SMALL_NAMES])))

    def view(a):
        return a.reshape(1, -1) if a.ndim == 1 else a.reshape(-1, a.shape[-1])

    for grp in SMALL_GROUPS:
        res = _adamw_multi(f"adamw_{grp[0]}", [view(g_small[n]) for n in grp], [view(args[n]) for n in grp],
                           [view(args["m_" + n]) for n in grp], [view(args["v_" + n]) for n in grp])
        for n, r in zip(grp, res):
            put(n, r, args[n].shape)
    ws = D // N_DEV
    tail = lax.dynamic_slice_in_dim(tail.reshape(N_DEV, 2, D), me * ws, ws, axis=2)
    res = _adamw("adamw_vec", [tail], vec_shard,
                 jnp.concatenate([m_s5_d, m_a_norm_mix], axis=0), jnp.concatenate([v_s5_d, v_a_norm_mix], axis=0))
    put("s5_d", [r[0:1] for r in res], s5_d.shape)
    put("a_norm_mix", [r[1:2] for r in res], a_norm_mix.shape)

    names = ("s5_lam_re", "s5_lam_im", "s5_log_dt", "s5_b_re", "s5_b_im", "s5_c_re", "s5_c_im", "s5_d",
             "s5_w_glu", "a_norm_mix", "ffn_norm", "ffn_w_in", "ffn_w_out", "b_norm_mix", "attn_w_q",
             "attn_w_o", "kv_norm", "w_kv", "final_norm")
    result = [loss, grad_x.reshape(x.shape)]
    for kind in ("grad", "delta", "new_m", "new_v"):
        result += [out[f"{kind}_{n}"] for n in names]
    return tuple(result)
```

```python
import functools
import math

import jax
import jax.numpy as jnp
from jax import lax
from jax.experimental import pallas as pl
from jax.experimental.pallas import tpu as pltpu

F32 = jnp.float32
BF16 = jnp.bfloat16

EPS = 1e-6
NEG_INF = -1e30
HEAD_DIM = 128
N_KV_HEADS = 4
DILATIONS = (1, 4, 16)
ATT_BLK = 128
S5_C = 16
S5_P = 64
S5_GB = 16
S5_CH = S5_GB * S5_C
S5_W = S5_GB * S5_P
S5_UNROLL = 4
N_DEV = 8

ADAM_LR = 0.001
ADAM_B1 = 0.9
ADAM_B2 = 0.999
ADAM_EPS = 1e-08
ADAM_WD = 0.01
ADAM_STEP = 10

VMEM_LIMIT_BYTES = 56 * 1024 * 1024
MM_TILE = 1024
MM_TILE_NARROW = 512
MM_DEPTH = 2816
MESH = pl.DeviceIdType.MESH
ANY = pl.BlockSpec(memory_space=pl.ANY)


def _tile(n, pref, align=128):
    t = (min(pref, n) // align) * align
    while t >= align:
        if n % t == 0:
            return t
        t -= align
    return n


def _params(sem):
    return pltpu.CompilerParams(dimension_semantics=sem, vmem_limit_bytes=VMEM_LIMIT_BYTES)


def _sigmoid(x):
    return 1.0 / (1.0 + jnp.exp(-x))


NN = (((1,), (0,)), ((), ()))
NT = (((1,), (1,)), ((), ()))
TN = (((0,), (0,)), ((), ()))


def _dot(a, b, dims=NN):
    return lax.dot_general(a, b, dims, preferred_element_type=F32)


def _matmul(name, grid, ins, in_specs, products, dims, out_shapes, out_specs, acc_shapes, epilogue):
    n_in, n_out, nk = len(ins), len(out_shapes), grid[2]

    def body(*refs):
        in_refs = refs[:n_in]
        out_refs = refs[n_in:n_in + n_out]
        acc_refs = refs[n_in + n_out:]

        def prods():
            vals = [None] * len(acc_shapes)
            for ai, bi, ci in products:
                d = _dot(in_refs[ai][...].astype(BF16), in_refs[bi][...].astype(BF16), dims)
                vals[ci] = d if vals[ci] is None else vals[ci] + d
            return vals

        if nk == 1:
            epilogue(in_refs, out_refs, prods())
        else:
            k = pl.program_id(2)

            @pl.when(k == 0)
            def _():
                for a in acc_refs:
                    a[...] = jnp.zeros_like(a)

            for a, v in zip(acc_refs, prods()):
                a[...] += v

            @pl.when(k == nk - 1)
            def _():
                epilogue(in_refs, out_refs, [a[...] for a in acc_refs])

    scratch = [] if nk == 1 else [pltpu.VMEM(s, F32) for s in acc_shapes]
    return pl.pallas_call(
        body, grid=grid, in_specs=in_specs, out_specs=out_specs, out_shape=out_shapes,
        scratch_shapes=scratch, compiler_params=_params(("parallel", "parallel", "arbitrary")),
        name=name)(*ins)


def _mm_dual_fwd(name, a, w, res, kind):
    T, K = a.shape
    N = w.shape[1] // 2
    tm, tn = _tile(T, MM_TILE), _tile(N, MM_TILE_NARROW)
    nj = N // tn
    grid = (T // tm, nj, 1)
    ins = [a, w, w]
    in_specs = [pl.BlockSpec((tm, K), lambda i, j, k: (i, 0)),
                pl.BlockSpec((K, tn), lambda i, j, k: (0, j)),
                pl.BlockSpec((K, tn), lambda i, j, k: (0, j + nj))]
    pair_spec = pl.BlockSpec((2, tm, tn), lambda i, j, k: (0, i, j))
    tile_spec = pl.BlockSpec((tm, tn), lambda i, j, k: (i, j))
    if kind == "glu":
        ins.append(res)
        in_specs.append(tile_spec)

        def epilogue(in_refs, out_refs, accs):
            val, gate = accs
            s = _sigmoid(gate)
            out_refs[0][...] = in_refs[3][...] + val * s
            out_refs[1][0] = s.astype(BF16)
            out_refs[1][1] = (val * s * (1.0 - s)).astype(BF16)

        out_shapes = [jax.ShapeDtypeStruct((T, N), F32), jax.ShapeDtypeStruct((2, T, N), BF16)]
        out_specs = [tile_spec, pair_spec]
    else:
        def epilogue(in_refs, out_refs, accs):
            g, u = accs
            s = _sigmoid(g)
            silu = g * s
            out_refs[0][0] = (u * (s * (1.0 + g * (1.0 - s)))).astype(BF16)
            out_refs[0][1] = silu.astype(BF16)
            out_refs[1][...] = (silu * u).astype(BF16)

        out_shapes = [jax.ShapeDtypeStruct((2, T, N), BF16), jax.ShapeDtypeStruct((T, N), BF16)]
        out_specs = [pair_spec, tile_spec]
    return _matmul(name, grid, ins, in_specs, [(0, 1, 0), (0, 2, 1)], NN, out_shapes, out_specs,
                   [(tm, tn), (tm, tn)], epilogue)


def _mm_kv(name, a, w):
    T, K = a.shape
    N = w.shape[1] // 2
    tm, tn = _tile(T, MM_TILE), _tile(N, MM_TILE_NARROW)
    nj = N // tn
    tile_spec = pl.BlockSpec((tm, tn), lambda i, j, k: (i, j))

    def epilogue(in_refs, out_refs, accs):
        out_refs[0][...] = accs[0]
        out_refs[1][...] = accs[1]

    return _matmul(name, (T // tm, nj, 1), [a, w, w],
                   [pl.BlockSpec((tm, K), lambda i, j, k: (i, 0)),
                    pl.BlockSpec((K, tn), lambda i, j, k: (0, j)),
                    pl.BlockSpec((K, tn), lambda i, j, k: (0, j + nj))],
                   [(0, 1, 0), (0, 2, 1)], NN,
                   [jax.ShapeDtypeStruct((T, N), F32)] * 2, [tile_spec, tile_spec],
                   [(tm, tn), (tm, tn)], epilogue)


def _mm_nn(name, a, w, res=None, out_dtype=F32):
    T, K = a.shape
    N = w.shape[1]
    tk = K if K <= 2 * MM_DEPTH else _tile(K, MM_DEPTH)
    tm, tn = _tile(T, MM_TILE), _tile(N, MM_TILE if K <= MM_DEPTH else MM_TILE_NARROW)
    grid = (T // tm, N // tn, K // tk)
    tile_spec = pl.BlockSpec((tm, tn), lambda i, j, k: (i, j))
    ins = [a, w]
    in_specs = [pl.BlockSpec((tm, tk), lambda i, j, k: (i, k)),
                pl.BlockSpec((tk, tn), lambda i, j, k: (k, j))]
    if res is not None:
        ins.append(res)
        in_specs.append(tile_spec)

    def epilogue(in_refs, out_refs, accs):
        v = accs[0]
        if res is not None:
            v = v + in_refs[2][...]
        out_refs[0][...] = v.astype(out_dtype)

    return _matmul(name, grid, ins, in_specs, [(0, 1, 0)], NN,
                   [jax.ShapeDtypeStruct((T, N), out_dtype)], [tile_spec], [(tm, tn)], epilogue)[0]


def _dep_operand(ins, in_specs, dep):
    if dep is not None:
        ins.append(dep)
        in_specs.append(pl.BlockSpec((8, 128), lambda *_: (0, 0)))


def _mm_nt(name, a_list, w, out_dtype=F32, dep=None):
    T, Np = a_list[0].shape
    Ko = w.shape[0]
    n_parts = len(a_list)
    wide_a = a_list[0].dtype != BF16
    tm, tn, tk = _tile(T, MM_TILE_NARROW if wide_a else MM_TILE), _tile(Ko, MM_TILE), _tile(Np, MM_DEPTH)
    nkp = Np // tk
    grid = (T // tm, Ko // tn, nkp)
    ins = list(a_list) + [w] * n_parts
    in_specs = [pl.BlockSpec((tm, tk), lambda i, j, k: (i, k)) for _ in a_list]
    in_specs += [pl.BlockSpec((tn, tk), functools.partial(lambda i, j, k, p: (j, p * nkp + k), p=p))
                 for p in range(n_parts)]
    products = [(p, n_parts + p, 0) for p in range(n_parts)]
    _dep_operand(ins, in_specs, dep)

    def epilogue(in_refs, out_refs, accs):
        out_refs[0][...] = accs[0].astype(out_dtype)

    return _matmul(name, grid, ins, in_specs, products, NT,
                   [jax.ShapeDtypeStruct((T, Ko), out_dtype)],
                   [pl.BlockSpec((tm, tn), lambda i, j, k: (i, j))], [(tm, tn)], epilogue)[0]


def _mm_nt_pair(name, a3, w, out_dtype=F32):
    _, T, N = a3.shape
    Ko = w.shape[0]
    tm, tn, tk = _tile(T, MM_TILE), _tile(Ko, MM_TILE), _tile(N, MM_DEPTH)
    nkh = N // tk
    grid = (T // tm, Ko // tn, 2 * nkh)

    def epilogue(in_refs, out_refs, accs):
        out_refs[0][...] = accs[0].astype(out_dtype)

    return _matmul(name, grid, [a3, w],
                   [pl.BlockSpec((None, tm, tk), lambda i, j, k: (k // nkh, i, k % nkh)),
                    pl.BlockSpec((tn, tk), lambda i, j, k: (j, k))],
                   [(0, 1, 0)], NT, [jax.ShapeDtypeStruct((T, Ko), out_dtype)],
                   [pl.BlockSpec((tm, tn), lambda i, j, k: (i, j))], [(tm, tn)], epilogue)[0]


def _mm_nt_ffn_bwd(name, dx, w_out, gu, dep=None):
    T, D = dx.shape
    Fh = w_out.shape[0]
    tm, tn = _tile(T, 2 * MM_TILE), _tile(Fh, MM_TILE_NARROW)
    n_dep = 0 if dep is None else 1

    def body(dx_ref, w_ref, gu_ref, *rest):
        out_ref = rest[n_dep]
        da = _dot(dx_ref[...], w_ref[...], NT)
        out_ref[0] = (da * gu_ref[0].astype(F32)).astype(BF16)
        out_ref[1] = (da * gu_ref[1].astype(F32)).astype(BF16)

    pair_spec = pl.BlockSpec((2, tm, tn), lambda i, j: (0, i, j))
    ins = [dx, w_out, gu]
    in_specs = [pl.BlockSpec((tm, D), lambda i, j: (i, 0)),
                pl.BlockSpec((tn, D), lambda i, j: (j, 0)),
                pair_spec]
    _dep_operand(ins, in_specs, dep)
    return pl.pallas_call(
        body, grid=(T // tm, Fh // tn), in_specs=in_specs, out_specs=pair_spec,
        out_shape=jax.ShapeDtypeStruct((2, T, Fh), BF16),
        compiler_params=_params(("parallel", "parallel")), name=name)(*ins)


def _mm_tn(name, a, d):
    T, Ko = a.shape
    N = d.shape[1]
    to, tn, tk = _tile(Ko, MM_TILE_NARROW), _tile(N, MM_TILE if d.dtype == BF16 else MM_TILE_NARROW), T
    grid = (Ko // to, N // tn, T // tk)

    def epilogue(in_refs, out_refs, accs):
        out_refs[0][...] = accs[0].astype(BF16)

    return _matmul(name, grid, [a, d],
                   [pl.BlockSpec((tk, to), lambda i, j, k: (k, i)),
                    pl.BlockSpec((tk, tn), lambda i, j, k: (k, j))],
                   [(0, 1, 0)], TN, [jax.ShapeDtypeStruct((Ko, N), BF16)],
                   [pl.BlockSpec((to, tn), lambda i, j, k: (i, j))], [(to, tn)], epilogue)[0]


def _mm_tn_pair(name, a, d3):
    T, Ko = a.shape
    N = d3.shape[2]
    to, tn, tk = _tile(Ko, MM_TILE), _tile(N, MM_TILE_NARROW), T
    njh = N // tn
    grid = (Ko // to, 2 * njh, T // tk)

    def epilogue(in_refs, out_refs, accs):
        out_refs[0][...] = accs[0].astype(BF16)

    return _matmul(name, grid, [a, d3],
                   [pl.BlockSpec((tk, to), lambda i, j, k: (k, i)),
                    pl.BlockSpec((None, tk, tn), lambda i, j, k: (j // njh, k, j % njh))],
                   [(0, 1, 0)], TN, [jax.ShapeDtypeStruct((Ko, 2 * N), BF16)],
                   [pl.BlockSpec((to, tn), lambda i, j, k: (i, j))], [(to, tn)], epilogue)[0]


def _rms_fwd(name, x, gains, dtypes, dep=None):
    T, D = x.shape
    n = len(gains)
    tr = _tile(T, 512, 8)
    n_dep = 0 if dep is None else 1

    def body(x_ref, *refs):
        xv = x_ref[...]
        xr = xv * lax.rsqrt(jnp.mean(xv * xv, axis=-1, keepdims=True) + EPS)
        for g_ref, o_ref in zip(refs[:n], refs[n + n_dep:]):
            o_ref[...] = (xr * g_ref[...]).astype(o_ref.dtype)

    row = pl.BlockSpec((tr, D), lambda i: (i, 0))
    vec = pl.BlockSpec((1, D), lambda i: (0, 0))
    ins, in_specs = [x, *gains], [row] + [vec] * n
    _dep_operand(ins, in_specs, dep)
    return pl.pallas_call(
        body, grid=(T // tr,), in_specs=in_specs, out_specs=[row] * n,
        out_shape=[jax.ShapeDtypeStruct((T, D), dt) for dt in dtypes],
        compiler_params=_params(("parallel",)), name=name)(*ins)


def _rms_bwd(name, x, dres, gains, dhs, dep=None):
    T, D = x.shape
    n = len(gains)
    tr = _tile(T, 256, 8)
    n_dep = 0 if dep is None else 1

    def body(x_ref, dres_ref, *refs):
        g_refs, dh_refs = refs[:n], refs[n:2 * n]
        dx_ref, dxb_ref = refs[2 * n + n_dep], refs[2 * n + n_dep + 1]
        dg_refs = refs[2 * n + n_dep + 2:]
        xv = x_ref[...]
        r = lax.rsqrt(jnp.mean(xv * xv, axis=-1, keepdims=True) + EPS)
        xr = xv * r
        w = None
        for g_ref, dh_ref, dg_ref in zip(g_refs, dh_refs, dg_refs):
            dh = dh_ref[...].astype(F32)

            @pl.when(pl.program_id(0) == 0)
            def _():
                dg_ref[...] = jnp.zeros_like(dg_ref)

            dg_ref[...] += jnp.sum(dh * xr, axis=0, keepdims=True)
            wi = dh * g_ref[...]
            w = wi if w is None else w + wi
        dx = dres_ref[...] + r * (w - xr * jnp.mean(w * xr, axis=-1, keepdims=True))
        dx_ref[...] = dx
        dxb_ref[...] = dx.astype(BF16)

    row = pl.BlockSpec((tr, D), lambda i: (i, 0))
    vec = pl.BlockSpec((1, D), lambda i: (0, 0))
    ins, in_specs = [x, dres, *gains, *dhs], [row, row] + [vec] * n + [row] * n
    _dep_operand(ins, in_specs, dep)
    outs = pl.pallas_call(
        body, grid=(T // tr,), in_specs=in_specs,
        out_specs=[row, row] + [vec] * n,
        out_shape=[jax.ShapeDtypeStruct((T, D), F32), jax.ShapeDtypeStruct((T, D), BF16)]
        + [jax.ShapeDtypeStruct((1, D), F32)] * n,
        compiler_params=_params(("arbitrary",)), name=name)(*ins)
    return outs[0], outs[1], outs[2:]


def _loss_head(x, gain, target):
    T, D = x.shape
    tr = _tile(T, 256, 8)

    def body(x_ref, g_ref, t_ref, loss_ref, dx_ref, dxb_ref, dg_ref):
        @pl.when(pl.program_id(0) == 0)
        def _():
            loss_ref[...] = jnp.zeros_like(loss_ref)
            dg_ref[...] = jnp.zeros_like(dg_ref)

        xv = x_ref[...]
        r = lax.rsqrt(jnp.mean(xv * xv, axis=-1, keepdims=True) + EPS)
        xr = xv * r
        err = xr * g_ref[...] - t_ref[...]
        part = jnp.sum(jnp.sum(err * err, axis=-1, keepdims=True), axis=0, keepdims=True) * (0.5 / D)
        loss_ref[...] += jnp.broadcast_to(part, loss_ref.shape)
        dy = err * (1.0 / D)
        dg_ref[...] += jnp.sum(dy * xr, axis=0, keepdims=True)
        w = dy * g_ref[...]
        dx = r * (w - xr * jnp.mean(w * xr, axis=-1, keepdims=True))
        dx_ref[...] = dx
        dxb_ref[...] = dx.astype(BF16)

    row = pl.BlockSpec((tr, D), lambda i: (i, 0))
    vec = pl.BlockSpec((1, D), lambda i: (0, 0))
    return pl.pallas_call(
        body, grid=(T // tr,), in_specs=[row, vec, row],
        out_specs=[pl.BlockSpec((8, 128), lambda i: (0, 0)), row, row, vec],
        out_shape=[jax.ShapeDtypeStruct((8, 128), F32), jax.ShapeDtypeStruct((T, D), F32),
                   jax.ShapeDtypeStruct((T, D), BF16), jax.ShapeDtypeStruct((1, D), F32)],
        compiler_params=_params(("arbitrary",)), name="loss_head")(x, gain, target)


def _glu_bwd(dmix, vg, dep=None):
    T, N = dmix.shape
    tr, tc = _tile(T, 512, 8), _tile(N, 1024)
    n_dep = 0 if dep is None else 1

    def body(d_ref, vg_ref, *refs):
        o_ref = refs[n_dep]
        d = d_ref[...]
        o_ref[0] = (d * vg_ref[0].astype(F32)).astype(BF16)
        o_ref[1] = (d * vg_ref[1].astype(F32)).astype(BF16)

    pair = pl.BlockSpec((2, tr, tc), lambda i, j: (0, i, j))
    ins, in_specs = [dmix, vg], [pl.BlockSpec((tr, tc), lambda i, j: (i, j)), pair]
    _dep_operand(ins, in_specs, dep)
    return pl.pallas_call(
        body, grid=(T // tr, N // tc), in_specs=in_specs,
        out_specs=pair, out_shape=jax.ShapeDtypeStruct((2, T, N), BF16),
        compiler_params=_params(("parallel", "parallel")), name="glu_bwd")(*ins)


def _to_state_tiles(x_ref, s, val):
    tc = val.shape[0]
    for j in range(S5_W // 128):
        x_ref[s, pl.ds(j, tc, stride=8), :] = val[:, 128 * j:128 * (j + 1)]


def _from_state_tiles(x_ref, s, tc):
    return jnp.concatenate([x_ref[s, pl.ds(j, tc, stride=8), :] for j in range(S5_W // 128)], axis=1)


def _s5_scan_fwd(xr_ref, xi_ref, ar_ref, ai_ref, cr_ref, ci_ref, tc, nblk):
    a = [(ar_ref[s], ai_ref[s]) for s in range(nblk)]

    def step(i, carry):
        carry = list(carry)
        for uu in range(S5_UNROLL):
            r0 = pl.multiple_of((i * S5_UNROLL + uu) * 8, 8)
            for s in range(nblk):
                cr, ci = carry[2 * s], carry[2 * s + 1]
                a_r, a_i = a[s]
                xr = a_r * cr - a_i * ci + xr_ref[s, pl.ds(r0, 8), :]
                xi = a_r * ci + a_i * cr + xi_ref[s, pl.ds(r0, 8), :]
                xr_ref[s, pl.ds(r0, 8), :] = xr
                xi_ref[s, pl.ds(r0, 8), :] = xi
                carry[2 * s], carry[2 * s + 1] = xr, xi
        return tuple(carry)

    init = []
    for s in range(nblk):
        init += [cr_ref[s], ci_ref[s]]
    out = lax.fori_loop(0, tc // S5_UNROLL, step, tuple(init))
    for s in range(nblk):
        cr_ref[s] = out[2 * s]
        ci_ref[s] = out[2 * s + 1]


def _s5_scan_bwd(lr_ref, li_ref, xr_ref, xi_ref, h_ref, ar_ref, ai_ref, cr_ref, ci_ref,
                 accr_ref, acci_ref, tc, nblk):
    a = [(ar_ref[s], ai_ref[s]) for s in range(nblk)]

    def one(s, r0, prev_r, prev_i, st):
        c_r, c_i, d_r, d_i = st
        a_r, a_i = a[s]
        l_r = lr_ref[s, pl.ds(r0, 8), :] + a_r * c_r + a_i * c_i
        l_i = li_ref[s, pl.ds(r0, 8), :] + a_r * c_i - a_i * c_r
        lr_ref[s, pl.ds(r0, 8), :] = l_r
        li_ref[s, pl.ds(r0, 8), :] = l_i
        return [l_r, l_i, d_r + l_r * prev_r + l_i * prev_i, d_i - l_r * prev_i + l_i * prev_r]

    def step(i, carry):
        carry = list(carry)
        for uu in range(S5_UNROLL):
            t = tc - 1 - (i * S5_UNROLL + uu)
            r0 = pl.multiple_of(t * 8, 8)
            p0 = pl.multiple_of((t - 1) * 8, 8)
            for s in range(nblk):
                carry[4 * s:4 * s + 4] = one(s, r0, xr_ref[s, pl.ds(p0, 8), :], xi_ref[s, pl.ds(p0, 8), :],
                                             carry[4 * s:4 * s + 4])
        return tuple(carry)

    init = []
    for s in range(nblk):
        init += [cr_ref[s], ci_ref[s], accr_ref[s], acci_ref[s]]
    carry = list(lax.fori_loop(0, tc // S5_UNROLL - 1, step, tuple(init)))
    for t in range(S5_UNROLL - 1, -1, -1):
        for s in range(nblk):
            if t > 0:
                prev_r, prev_i = xr_ref[s, 8 * (t - 1):8 * t, :], xi_ref[s, 8 * (t - 1):8 * t, :]
            else:
                prev_r, prev_i = h_ref[0, s], h_ref[1, s]
            carry[4 * s:4 * s + 4] = one(s, 8 * t, prev_r, prev_i, carry[4 * s:4 * s + 4])
    for s in range(nblk):
        cr_ref[s], ci_ref[s], accr_ref[s], acci_ref[s] = carry[4 * s:4 * s + 4]


def _gelu_and_grad(y):
    k = math.sqrt(2.0 / math.pi)
    t = jnp.tanh(k * (y + 0.044715 * (y * y * y)))
    return (0.5 * y * (1.0 + t),
            0.5 * (1.0 + t) + 0.5 * y * (1.0 - t * t) * (k * (1.0 + 3.0 * 0.044715 * (y * y))))


def _s5_specs(tc, nch, sbk, rev):
    def ch(c):
        return nch - 1 - c if rev else c

    return dict(
        act=pl.BlockSpec((tc, sbk * S5_CH), lambda i, c: (ch(c), i)),
        bb=pl.BlockSpec((2, sbk, S5_CH, 128), lambda i, c: (0, i, 0, 0)),
        cc=pl.BlockSpec((2, sbk, S5_P, S5_CH), lambda i, c: (0, i, 0, 0)),
        a=pl.BlockSpec((sbk, 8, 128), lambda i, c: (i, 0, 0)),
        d=pl.BlockSpec((1, sbk * S5_CH), lambda i, c: (0, i)),
        h=pl.BlockSpec((None, 2, sbk, 8, 128), lambda i, c: (ch(c), 0, i, 0, 0)),
    )


def _s5_blocks(nb, pref):
    return max(b for b in range(1, pref + 1) if nb % b == 0)


def _s5_group_masks():
    rb = lax.broadcasted_iota(jnp.int32, (S5_CH, S5_W), 0) // S5_C
    qb = lax.broadcasted_iota(jnp.int32, (S5_CH, S5_W), 1) // S5_P
    qc = lax.broadcasted_iota(jnp.int32, (S5_W, S5_CH), 0) // S5_P
    rc = lax.broadcasted_iota(jnp.int32, (S5_W, S5_CH), 1) // S5_C
    return rb == qb, qc == rc


def _s5_expand(bb_ref, cc_ref, bbd, ccd, sbk):
    mask_b, mask_c = _s5_group_masks()
    for k in range(2):
        for s in range(sbk):
            bbd[k, s] = jnp.where(mask_b, jnp.tile(bb_ref[k, s], (1, S5_W // 128)), 0).astype(BF16)
            ccd[k, s] = jnp.where(mask_c, jnp.tile(cc_ref[k, s], (S5_GB, 1)), 0).astype(BF16)


def _s5_fwd(u, bb2, cc2, a_re, a_im, d_skip):
    T, D = u.shape
    nb = D // S5_CH
    sbk = _s5_blocks(nb, 4)
    tc = _tile(T, 512, 8)
    nch = T // tc
    sp = _s5_specs(tc, nch, sbk, False)

    def body(u_ref, bb_ref, cc_ref, ar_ref, ai_ref, d_ref, z_ref, gp_ref, h_ref, xr, xi, cr, ci, bbd, ccd):
        @pl.when(pl.program_id(1) == 0)
        def _():
            cr[...] = jnp.zeros_like(cr)
            ci[...] = jnp.zeros_like(ci)
            _s5_expand(bb_ref, cc_ref, bbd, ccd, sbk)

        h_ref[0] = cr[...]
        h_ref[1] = ci[...]
        for s in range(sbk):
            ub = u_ref[:, s * S5_CH:(s + 1) * S5_CH].astype(BF16)
            _to_state_tiles(xr, s, _dot(ub, bbd[0, s]))
            _to_state_tiles(xi, s, _dot(ub, bbd[1, s]))
        _s5_scan_fwd(xr, xi, ar_ref, ai_ref, cr, ci, tc, sbk)
        for s in range(sbk):
            cols = slice(s * S5_CH, (s + 1) * S5_CH)
            y = (_dot(_from_state_tiles(xr, s, tc).astype(BF16), ccd[0, s])
                 - _dot(_from_state_tiles(xi, s, tc).astype(BF16), ccd[1, s])
                 + d_ref[:, cols] * u_ref[:, cols])
            act, act_grad = _gelu_and_grad(y)
            z_ref[:, cols] = act.astype(BF16)
            gp_ref[:, cols] = act_grad.astype(BF16)

    tiles = pltpu.VMEM((sbk, tc * 8, 128), F32)
    carry = pltpu.VMEM((sbk, 8, 128), F32)
    return pl.pallas_call(
        body, grid=(nb // sbk, nch),
        in_specs=[sp["act"], sp["bb"], sp["cc"], sp["a"], sp["a"], sp["d"]],
        out_specs=[sp["act"], sp["act"], sp["h"]],
        out_shape=[jax.ShapeDtypeStruct((T, D), BF16), jax.ShapeDtypeStruct((T, D), BF16),
                   jax.ShapeDtypeStruct((nch, 2, nb, 8, 128), F32)],
        scratch_shapes=[tiles, tiles, carry, carry, pltpu.VMEM((2, sbk, S5_CH, S5_W), BF16),
                        pltpu.VMEM((2, sbk, S5_W, S5_CH), BF16)],
        compiler_params=_params(("parallel", "arbitrary")), name="s5_fwd",
    )(u, bb2, cc2, a_re, a_im, d_skip)


def _s5_bwd(u, dz, gp, h0, bb2, cc2, a_re, a_im, d_skip, dep=None):
    T, D = u.shape
    nb = D // S5_CH
    sbk = _s5_blocks(nb, 2)
    tc = _tile(T, 512, 8)
    nch = T // tc
    sp = _s5_specs(tc, nch, sbk, True)

    n_dep = 0 if dep is None else 1

    def body(u_ref, dz_ref, gp_ref, h_ref, bb_ref, cc_ref, ar_ref, ai_ref, d_ref, *rest):
        (du_ref, dd_ref, dar_ref, dai_ref, dbb_ref, dcc_ref,
         xr, xi, lr, li, fr, fi, br, bi, accr, acci, bbd, ccd, dbbd, dccd) = rest[n_dep:]
        c = pl.program_id(1)

        @pl.when(c == 0)
        def _():
            for ref in (br, bi, accr, acci, dd_ref, dbbd, dccd):
                ref[...] = jnp.zeros_like(ref)
            _s5_expand(bb_ref, cc_ref, bbd, ccd, sbk)

        for s in range(sbk):
            ub = u_ref[:, s * S5_CH:(s + 1) * S5_CH].astype(BF16)
            _to_state_tiles(xr, s, _dot(ub, bbd[0, s]))
            _to_state_tiles(xi, s, _dot(ub, bbd[1, s]))
        fr[...] = h_ref[0]
        fi[...] = h_ref[1]
        _s5_scan_fwd(xr, xi, ar_ref, ai_ref, fr, fi, tc, sbk)
        for s in range(sbk):
            cols = slice(s * S5_CH, (s + 1) * S5_CH)
            uv = u_ref[:, cols]
            xrb = _from_state_tiles(xr, s, tc).astype(BF16)
            xib = _from_state_tiles(xi, s, tc).astype(BF16)
            dsk = d_ref[:, cols]
            dy = dz_ref[:, cols].astype(F32) * gp_ref[:, cols].astype(F32)
            dd_ref[:, cols] += jnp.sum(dy * uv, axis=0, keepdims=True)
            dyb = dy.astype(BF16)
            dccd[0, s] += _dot(xrb, dyb, TN)
            dccd[1, s] += _dot(xib, dyb, TN)
            _to_state_tiles(lr, s, _dot(dyb, ccd[0, s], NT))
            _to_state_tiles(li, s, -_dot(dyb, ccd[1, s], NT))
            du_ref[:, cols] = dy * dsk
        _s5_scan_bwd(lr, li, xr, xi, h_ref, ar_ref, ai_ref, br, bi, accr, acci, tc, sbk)
        for s in range(sbk):
            cols = slice(s * S5_CH, (s + 1) * S5_CH)
            ub = u_ref[:, cols].astype(BF16)
            lrb = _from_state_tiles(lr, s, tc).astype(BF16)
            lib = _from_state_tiles(li, s, tc).astype(BF16)
            dbbd[0, s] += _dot(ub, lrb, TN)
            dbbd[1, s] += _dot(ub, lib, TN)
            du_ref[:, cols] += _dot(lrb, bbd[0, s], NT) + _dot(lib, bbd[1, s], NT)

        @pl.when(c == nch - 1)
        def _():
            dar_ref[...] = accr[...]
            dai_ref[...] = acci[...]
            mask_b, mask_c = _s5_group_masks()
            for k in range(2):
                for s in range(sbk):
                    mb = jnp.where(mask_b, dbbd[k, s], 0.0)
                    fold = functools.reduce(
                        lambda a, b: a + b, [mb[:, 128 * j:128 * (j + 1)] for j in range(S5_W // 128)])
                    dbb_ref[k, s] = fold + pltpu.roll(fold, S5_P, 1)
                    mc = jnp.where(mask_c, dccd[k, s], 0.0)
                    dcc_ref[k, s] = functools.reduce(
                        lambda a, b: a + b, [mc[S5_P * j:S5_P * (j + 1), :] for j in range(S5_GB)])

    tiles = pltpu.VMEM((sbk, tc * 8, 128), F32)
    carry = pltpu.VMEM((sbk, 8, 128), F32)
    ins = [u, dz, gp, h0, bb2, cc2, a_re, a_im, d_skip]
    in_specs = [sp["act"], sp["act"], sp["act"], sp["h"], sp["bb"], sp["cc"], sp["a"], sp["a"], sp["d"]]
    _dep_operand(ins, in_specs, dep)
    return pl.pallas_call(
        body, grid=(nb // sbk, nch),
        in_specs=in_specs,
        out_specs=[sp["act"], sp["d"], sp["a"], sp["a"], sp["bb"], sp["cc"]],
        out_shape=[jax.ShapeDtypeStruct((T, D), F32), jax.ShapeDtypeStruct((1, D), F32),
                   jax.ShapeDtypeStruct((nb, 8, 128), F32), jax.ShapeDtypeStruct((nb, 8, 128), F32),
                   jax.ShapeDtypeStruct((2, nb, S5_CH, 128), F32), jax.ShapeDtypeStruct((2, nb, S5_P, S5_CH), F32)],
        scratch_shapes=[tiles, tiles, tiles, tiles, carry, carry, carry, carry, carry, carry,
                        pltpu.VMEM((2, sbk, S5_CH, S5_W), BF16), pltpu.VMEM((2, sbk, S5_W, S5_CH), BF16),
                        pltpu.VMEM((2, sbk, S5_CH, S5_W), F32), pltpu.VMEM((2, sbk, S5_W, S5_CH), F32)],
        compiler_params=_params(("parallel", "arbitrary")), name="s5_bwd",
    )(*ins)


def _s5_disc(lr, li, ldt):
    dt = jnp.exp(ldt)
    mag = jnp.exp(lr * dt)
    ang = li * dt
    cs, sn = jnp.cos(ang), jnp.sin(ang)
    lbr, lbi = mag * cs, mag * sn
    nr = lbr - 1.0
    den = lr * lr + li * li
    f_re = (nr * lr + lbi * li) / den
    f_im = (lbi * lr - nr * li) / den
    return dt, mag, cs, sn, lbr, lbi, nr, den, f_re, f_im


def _s5_param_fwd(lr, li, ldt, bt_re, bt_im):
    c, g, p = bt_re.shape

    def body(lr_ref, li_ref, ldt_ref, br_ref, bi_ref, lbr_ref, lbi_ref, bbr_ref, bbi_ref):
        _, _, _, _, lbr, lbi, _, _, f_re, f_im = _s5_disc(lr_ref[...], li_ref[...], ldt_ref[...])
        lbr_ref[...] = lbr
        lbi_ref[...] = lbi
        for ch in range(c):
            b_r, b_i = br_ref[ch], bi_ref[ch]
            bbr_ref[ch] = f_re * b_r - f_im * b_i
            bbi_ref[ch] = f_re * b_i + f_im * b_r

    gp = jax.ShapeDtypeStruct((g, p), F32)
    cgp = jax.ShapeDtypeStruct((c, g, p), F32)
    return pl.pallas_call(body, out_shape=[gp, gp, cgp, cgp], name="s5_param_fwd")(lr, li, ldt, bt_re, bt_im)


def _s5_param_bwd(lr, li, ldt, bt_re, bt_im, dlbr, dlbi, dbbr, dbbi):
    c, g, p = bt_re.shape

    def body(lr_ref, li_ref, ldt_ref, br_ref, bi_ref, dlbr_ref, dlbi_ref, dbbr_ref, dbbi_ref,
             dlr_ref, dli_ref, dldt_ref, dbr_ref, dbi_ref):
        l_r, l_i = lr_ref[...], li_ref[...]
        dt, mag, cs, sn, lbr, lbi, nr, den, f_re, f_im = _s5_disc(l_r, l_i, ldt_ref[...])
        dfr = jnp.zeros_like(l_r)
        dfi = jnp.zeros_like(l_r)
        for ch in range(c):
            b_r, b_i = br_ref[ch], bi_ref[ch]
            g_r, g_i = dbbr_ref[ch], dbbi_ref[ch]
            dbr_ref[ch] = f_re * g_r + f_im * g_i
            dbi_ref[ch] = f_re * g_i - f_im * g_r
            dfr = dfr + g_r * b_r + g_i * b_i
            dfi = dfi + g_i * b_r - g_r * b_i
        inv = 1.0 / den
        d_nr = (dfr * l_r - dfi * l_i) * inv
        d_lbi = (dfr * l_i + dfi * l_r) * inv + dlbi_ref[...]
        d_lbr = d_nr + dlbr_ref[...]
        d_den = -(dfr * f_re + dfi * f_im) * inv
        d_mag = d_lbr * cs + d_lbi * sn
        d_ang = d_lbi * lbr - d_lbr * lbi
        dlr_ref[...] = (dfr * nr + dfi * lbi) * inv + 2.0 * d_den * l_r + d_mag * mag * dt
        dli_ref[...] = (dfr * lbi - dfi * nr) * inv + 2.0 * d_den * l_i + d_ang * dt
        dldt_ref[...] = jnp.sum(d_mag * mag * l_r + d_ang * l_i, axis=1, keepdims=True) * dt

    gp = jax.ShapeDtypeStruct((g, p), F32)
    cgp = jax.ShapeDtypeStruct((c, g, p), F32)
    return pl.pallas_call(body, out_shape=[gp, gp, jax.ShapeDtypeStruct((g, 1), F32), cgp, cgp],
                          name="s5_param_bwd")(lr, li, ldt, bt_re, bt_im, dlbr, dlbi, dbbr, dbbi)


def _att_masks(rep, gb):
    rows = rep * ATT_BLK
    qi = lax.broadcasted_iota(jnp.int32, (rows, 2 * ATT_BLK), 0) % ATT_BLK
    si = lax.broadcasted_iota(jnp.int32, (rows, 2 * ATT_BLK), 1)
    prev = (si < ATT_BLK) & (si >= qi) & (gb > 0)
    cur = (si >= ATT_BLK) & (si - ATT_BLK <= qi)
    return prev | cur


def _att_rows(start, dil):
    return pl.ds(start, ATT_BLK) if dil == 1 else pl.ds(start, ATT_BLK, stride=dil)


def _att_plan(T, dil):
    span = ATT_BLK * dil
    sbr = max(span, min(T, 1024))
    return span, sbr, T // sbr


def _att_block(sb, i, sbr, span, dil):
    loc = (i // dil) * span + i % dil
    cur = sb * sbr + loc
    gb = sb * (sbr // span) + i // dil
    return loc, cur, jnp.where(gb > 0, cur - span, cur), gb


def _att_fwd(q, k, v, grp, dil):
    T = q.shape[0]
    H = q.shape[1] // HEAD_DIM // len(DILATIONS)
    rep = H // N_KV_HEADS
    span, sbr, nsb = _att_plan(T, dil)
    scale = HEAD_DIM ** -0.5

    def body(*refs):
        q_refs = refs[:rep]
        k_ref, v_ref, o_ref, l_ref = refs[rep:rep + 4]
        slabs = refs[rep + 4:]
        sb = pl.program_id(1)

        def blk(i, _):
            loc, cur, prv, gb = _att_block(sb, i, sbr, span, dil)
            rows = _att_rows(loc, dil)
            qs = jnp.concatenate([r[rows, :] for r in q_refs], axis=0).astype(BF16)
            kcat = jnp.concatenate([k_ref[_att_rows(prv, dil), :], k_ref[_att_rows(cur, dil), :]], axis=0)
            vcat = jnp.concatenate([v_ref[_att_rows(prv, dil), :], v_ref[_att_rows(cur, dil), :]], axis=0)
            s = jnp.where(_att_masks(rep, gb), _dot(qs, kcat.astype(BF16), NT) * scale, NEG_INF)
            m = jnp.max(s, axis=-1, keepdims=True)
            p = jnp.exp(s - m)
            l = jnp.sum(p, axis=-1, keepdims=True)
            o = _dot(p.astype(BF16), vcat.astype(BF16)) / l
            lse = jnp.broadcast_to(m + jnp.log(l), (rep * ATT_BLK, HEAD_DIM))
            for j in range(rep):
                part, cols = slice(j * ATT_BLK, (j + 1) * ATT_BLK), slice(j * HEAD_DIM, (j + 1) * HEAD_DIM)
                if dil == 1:
                    dst = pl.ds(pl.multiple_of(loc, ATT_BLK), ATT_BLK)
                    o_ref[dst, cols] = o[part].astype(BF16)
                    l_ref[dst, cols] = lse[part]
                else:
                    slabs[0][j, rows, :] = o[part]
                    slabs[1][j, rows, :] = lse[part]
            return 0

        lax.fori_loop(0, sbr // ATT_BLK, blk, 0)
        if dil > 1:
            for j in range(rep):
                o_ref[:, j * HEAD_DIM:(j + 1) * HEAD_DIM] = slabs[0][j].astype(BF16)
                l_ref[:, j * HEAD_DIM:(j + 1) * HEAD_DIM] = slabs[1][j]

    qspecs = [pl.BlockSpec((sbr, HEAD_DIM), functools.partial(lambda h, s, j: (s, grp * H + h * rep + j), j=j))
              for j in range(rep)]
    kspec = pl.BlockSpec((T, HEAD_DIM), lambda h, s: (0, h))
    ospec = pl.BlockSpec((sbr, rep * HEAD_DIM), lambda h, s: (s, h))
    slab = pltpu.VMEM((rep, sbr, HEAD_DIM), F32)
    return pl.pallas_call(
        body, grid=(N_KV_HEADS, nsb), in_specs=qspecs + [kspec, kspec], out_specs=[ospec, ospec],
        out_shape=[jax.ShapeDtypeStruct((T, H * HEAD_DIM), BF16), jax.ShapeDtypeStruct((T, H * HEAD_DIM), F32)],
        scratch_shapes=[] if dil == 1 else [slab, slab],
        compiler_params=_params(("parallel", "arbitrary")), name=f"att_fwd_d{dil}",
    )(*([q] * rep), k, v)


def _att_combine(outs, lses):
    T, W = outs[0].shape
    ng = len(outs)
    tr, tcol = _tile(T, 512, 8), _tile(W, 512)

    def body(*refs):
        o_refs, l_refs = refs[:ng], refs[ng:2 * ng]
        ob_ref, lse_ref = refs[2 * ng:]
        ls = [r[...] for r in l_refs]
        m = functools.reduce(jnp.maximum, ls)
        es = [jnp.exp(l - m) for l in ls]
        den = functools.reduce(lambda a, b: a + b, es)
        num = functools.reduce(lambda a, b: a + b, [e * o[...].astype(F32) for e, o in zip(es, o_refs)])
        ob_ref[...] = (num / den).astype(BF16)
        lse_ref[...] = m + jnp.log(den)

    spec = pl.BlockSpec((tr, tcol), lambda i, j: (i, j))
    return pl.pallas_call(
        body, grid=(T // tr, W // tcol), in_specs=[spec] * (2 * ng), out_specs=[spec, spec],
        out_shape=[jax.ShapeDtypeStruct((T, W), BF16), jax.ShapeDtypeStruct((T, W), F32)],
        compiler_params=_params(("parallel", "parallel")), name="att_combine")(*outs, *lses)


STAT_LANE = HEAD_DIM // 2


def _att_out_bwd(dx, w_o, o, lse, dep=None):
    T, D = dx.shape
    W = w_o.shape[0]
    tm, tn = _tile(T, MM_TILE), _tile(W, MM_TILE)
    n_dep = 0 if dep is None else 1

    def body(dx_ref, w_ref, o_ref, l_ref, *rest):
        do_ref, s_ref = rest[n_dep:]
        do_ref[...] = _dot(dx_ref[...], w_ref[...], NT)
        lane = lax.broadcasted_iota(jnp.int32, (tm, HEAD_DIM), 1)
        for h in range(tn // HEAD_DIM):
            cols = slice(h * HEAD_DIM, (h + 1) * HEAD_DIM)
            delta = jnp.sum(do_ref[:, cols] * o_ref[:, cols].astype(F32), axis=-1, keepdims=True)
            s_ref[:, cols] = jnp.where(lane < STAT_LANE, l_ref[:, cols], delta)

    tile = pl.BlockSpec((tm, tn), lambda i, j: (i, j))
    ins = [dx, w_o, o, lse]
    in_specs = [pl.BlockSpec((tm, D), lambda i, j: (i, 0)), pl.BlockSpec((tn, D), lambda i, j: (j, 0)), tile, tile]
    _dep_operand(ins, in_specs, dep)
    return pl.pallas_call(
        body, grid=(T // tm, W // tn), in_specs=in_specs, out_specs=[tile, tile],
        out_shape=[jax.ShapeDtypeStruct((T, W), F32)] * 2,
        compiler_params=_params(("parallel", "parallel")), name="o_proj_dx")(*ins)


def _att_bwd(q, k, v, do, stats, dq, grp, dil):
    T = q.shape[0]
    H = do.shape[1] // HEAD_DIM
    rep = H // N_KV_HEADS
    hs = rep
    span, sbr, nsb = _att_plan(T, dil)
    scale = HEAD_DIM ** -0.5

    def body(*refs):
        q_refs, do_refs, st_refs = refs[:hs], refs[hs + 2:2 * hs + 2], refs[2 * hs + 2:3 * hs + 2]
        k_ref, v_ref = refs[hs], refs[hs + 1]
        dq_ref, dk_ref, dv_ref = refs[3 * hs + 3:3 * hs + 6]
        slabs = refs[3 * hs + 6:]
        sb = pl.program_id(2)

        @pl.when((pl.program_id(1) == 0) & (sb == 0))
        def _():
            dk_ref[...] = jnp.zeros_like(dk_ref)
            dv_ref[...] = jnp.zeros_like(dv_ref)

        def blk(i, _):
            loc, cur, prv, gb = _att_block(sb, i, sbr, span, dil)
            rows, kc, kp = _att_rows(loc, dil), _att_rows(cur, dil), _att_rows(prv, dil)
            qs = jnp.concatenate([r[rows, :] for r in q_refs], axis=0).astype(BF16)
            dos = jnp.concatenate([r[rows, :] for r in do_refs], axis=0).astype(BF16)
            st = jnp.concatenate([r[rows, :] for r in st_refs], axis=0)
            kcat = jnp.concatenate([k_ref[kp, :], k_ref[kc, :]], axis=0).astype(BF16)
            vcat = jnp.concatenate([v_ref[kp, :], v_ref[kc, :]], axis=0).astype(BF16)
            s = _dot(qs, kcat, NT) * scale
            p = jnp.where(_att_masks(hs, gb), jnp.exp(s - st[:, 0:1]), 0.0)
            dp = _dot(dos, vcat, NT)
            ds = (p * (dp - st[:, STAT_LANE:STAT_LANE + 1]) * scale).astype(BF16)
            dvc = _dot(p.astype(BF16), dos, TN)
            dkc = _dot(ds, qs, TN)
            dqs = _dot(ds, kcat)
            for j in range(hs):
                part = slice(j * ATT_BLK, (j + 1) * ATT_BLK)
                if dil == 1:
                    dq_ref[pl.ds(pl.multiple_of(loc, ATT_BLK), ATT_BLK), j * HEAD_DIM:(j + 1) * HEAD_DIM] = (
                        dqs[part].astype(BF16))
                else:
                    slabs[0][j, rows, :] = dqs[part]
            dk_ref[kc, :] += dkc[ATT_BLK:]
            dv_ref[kc, :] += dvc[ATT_BLK:]

            @pl.when(gb > 0)
            def _():
                dk_ref[kp, :] += dkc[:ATT_BLK]
                dv_ref[kp, :] += dvc[:ATT_BLK]

            return 0

        lax.fori_loop(0, sbr // ATT_BLK, blk, 0)
        if dil > 1:
            for j in range(hs):
                dq_ref[:, j * HEAD_DIM:(j + 1) * HEAD_DIM] = slabs[0][j].astype(BF16)

    def head_specs(col0):
        return [pl.BlockSpec((sbr, HEAD_DIM),
                             functools.partial(lambda h, f, s, j: (s, col0 + h * rep + f * hs + j), j=j))
                for j in range(hs)]

    kspec = pl.BlockSpec((T, HEAD_DIM), lambda h, f, s: (0, h))
    dqspec = pl.BlockSpec((sbr, hs * HEAD_DIM), lambda h, f, s: (s, (grp * H + h * rep) // hs + f))
    n_in = 3 * hs + 3
    return pl.pallas_call(
        body, grid=(N_KV_HEADS, rep // hs, nsb),
        in_specs=head_specs(grp * H) + [kspec, kspec] + head_specs(0) + head_specs(0) + [ANY],
        out_specs=[dqspec, kspec, kspec],
        out_shape=[jax.ShapeDtypeStruct(dq.shape, BF16),
                   jax.ShapeDtypeStruct((T, N_KV_HEADS * HEAD_DIM), F32),
                   jax.ShapeDtypeStruct((T, N_KV_HEADS * HEAD_DIM), F32)],
        scratch_shapes=[] if dil == 1 else [pltpu.VMEM((hs, sbr, HEAD_DIM), F32)],
        input_output_aliases={n_in - 1: 0},
        compiler_params=_params(("parallel", "arbitrary", "arbitrary")), name=f"att_bwd_d{dil}",
    )(*([q] * hs), k, v, *([do] * hs), *([stats] * hs), dq)


def _sum_kv(dks, dvs):
    T, W = dks[0].shape
    ng = len(dks)
    tr = _tile(T, 512, 8)

    def body(*refs):
        o_ref = refs[2 * ng]
        o_ref[0] = functools.reduce(lambda a, b: a + b, [r[...] for r in refs[:ng]]).astype(BF16)
        o_ref[1] = functools.reduce(lambda a, b: a + b, [r[...] for r in refs[ng:2 * ng]]).astype(BF16)

    spec = pl.BlockSpec((tr, W), lambda i: (i, 0))
    return pl.pallas_call(
        body, grid=(T // tr,), in_specs=[spec] * (2 * ng),
        out_specs=pl.BlockSpec((2, tr, W), lambda i: (0, i, 0)),
        out_shape=jax.ShapeDtypeStruct((2, T, W), BF16),
        compiler_params=_params(("parallel",)), name="sum_kv")(*dks, *dvs)


def _local_step(x, tgt, w, wts, ready, dep0=None, small_ready=None):
    T, D = x.shape
    g = {}

    (u0,) = _rms_fwd("rms_a", x, [w["a_norm"]], [F32], dep=dep0)
    lbr, lbi, bbt_re, bbt_im = _s5_param_fwd(w["lam_re"], w["lam_im"], w["log_dt"], w["bt_re"], w["bt_im"])
    a_re, a_im = lbr.reshape(-1, 8, 128), lbi.reshape(-1, 8, 128)
    C, G, P = w["bt_re"].shape
    nb = G // S5_GB
    bb2 = jnp.stack([bbt_re, bbt_im]).transpose(0, 2, 1, 3).reshape(2, nb, S5_GB * C, P)
    bb2 = jnp.concatenate([bb2, bb2], axis=-1)
    cc2 = jnp.stack([w["c_re"], w["c_im"]]).reshape(2, nb, S5_GB * C, P).transpose(0, 1, 3, 2)
    z, gp, h0 = _s5_fwd(u0, bb2, cc2, a_re, a_im, w["s5_d"])
    w_glu = wts("glu", z)
    x1, vg = _mm_dual_fwd("glu_fwd", z, w_glu, x, "glu")

    def ffn_fwd(xin, layer):
        (nrm,) = _rms_fwd(f"rms_f{layer}", xin, [w["ffn_norm"][layer:layer + 1]], [BF16])
        w_in = wts(f"win{layer}", nrm)
        gu, act = _mm_dual_fwd(f"ffn_in{layer}", nrm, w_in, None, "ffn")
        w_out = wts(f"wout{layer}", act)
        xout = _mm_nn(f"ffn_out{layer}", act, w_out, res=xin)
        return xout, (nrm, gu, act, w_in, w_out)

    x2, saved0 = ffn_fwd(x1, 0)
    kvn, hb = _rms_fwd("rms_b", x2, [w["kv_norm"], w["b_norm"]], [BF16, BF16])
    w_kv, w_q, w_o = wts("wkv", hb), wts("wq", hb), wts("wo", hb)
    k, v = _mm_kv("kv_proj", kvn, w_kv)
    q = _mm_nn("q_proj", hb, w_q)
    outs, lses = [], []
    for grp, dil in enumerate(DILATIONS):
        o_g, l_g = _att_fwd(q, k, v, grp, dil)
        outs.append(o_g)
        lses.append(l_g)
    o, lse = _att_combine(outs, lses)
    wts("win1", o, prefetch=True)
    x3 = _mm_nn("o_proj", o, w_o, res=x2)
    x4, saved1 = ffn_fwd(x3, 1)
    loss_blk, dx4, dx4b, g["final_norm"] = _loss_head(x4, w["final_norm"], tgt)

    def ffn_bwd(dx, dxb, xin, saved, layer, dep):
        nrm, gu, act, w_in, w_out = saved
        dgu = _mm_nt_ffn_bwd(f"ffn_dact{layer}", dxb, w_out, gu, dep=dep)
        g_wout = _mm_tn(f"ffn_dwout{layer}", act, dxb)
        g_win = _mm_tn_pair(f"ffn_dwin{layer}", nrm, dgu)
        dn = _mm_nt_pair(f"ffn_dn{layer}", dgu, w_in, out_dtype=BF16)
        dxo, dxob, (dgn,) = _rms_bwd(f"rms_f{layer}_bwd", xin, dx, [w["ffn_norm"][layer:layer + 1]], [dn])
        tok = ready({f"win{layer}": g_win, f"wout{layer}": g_wout})
        return dxo, dxob, dgn, tok

    dx3, dx3b, dfn1, tok = ffn_bwd(dx4, dx4b, x3, saved1, 1, None)
    do, stats = _att_out_bwd(dx3b, w_o, o, lse, dep=tok)
    g_wo = _mm_tn("o_proj_dw", o, dx3b)
    dq = lax.empty(q.shape, BF16)
    dks, dvs = [], []
    for grp, dil in enumerate(DILATIONS):
        dq, dk_g, dv_g = _att_bwd(q, k, v, do, stats, dq, grp, dil)
        dks.append(dk_g)
        dvs.append(dv_g)
    dkv = _sum_kv(dks, dvs)
    dhb = _mm_nt("q_proj_dx", [dq], w_q, out_dtype=BF16)
    g_wq = _mm_tn("q_proj_dw", hb, dq)
    dkvn = _mm_nt_pair("kv_proj_dx", dkv, w_kv, out_dtype=BF16)
    g_wkv = _mm_tn_pair("kv_proj_dw", kvn, dkv)
    dx2, dx2b, (g["kv_norm"], g["b_norm"]) = _rms_bwd(
        "rms_b_bwd", x2, dx3, [w["kv_norm"], w["b_norm"]], [dkvn, dhb])
    tok = ready({"wkv": g_wkv, "wq": g_wq, "wo": g_wo})
    dx1, dx1b, dfn0, tok = ffn_bwd(dx2, dx2b, x1, saved0, 0, tok)
    g["ffn_norm"] = jnp.concatenate([dfn0, dfn1], axis=0)

    dvg = _glu_bwd(dx1, vg, dep=tok)
    dz = _mm_nt_pair("glu_dx", dvg, w_glu, out_dtype=BF16)
    tok = ready({"glu": _mm_tn_pair("glu_dw", z, dvg)})
    tok = ready({}, after=tok)
    du, g["s5_d"], da_re, da_im, dbb2, dcc2 = _s5_bwd(
        u0, dz, gp, h0, bb2, cc2, a_re, a_im, w["s5_d"], dep=tok)
    dcc = dcc2.transpose(0, 1, 3, 2).reshape(2, G, C, P)
    g["c_re"], g["c_im"] = dcc[0], -dcc[1]
    dbbt = dbb2[..., :P].reshape(2, G, C, P).transpose(0, 2, 1, 3)
    g["lam_re"], g["lam_im"], g["log_dt"], g["bt_re"], g["bt_im"] = _s5_param_bwd(
        w["lam_re"], w["lam_im"], w["log_dt"], w["bt_re"], w["bt_im"],
        da_re.reshape(G, P), da_im.reshape(G, P), dbbt[0], dbbt[1])
    tok = small_ready(g) if small_ready is not None else None
    grad_x, _, (g["a_norm"],) = _rms_bwd("rms_a_bwd", x, dx1, [w["a_norm"]], [du], dep=tok)
    return loss_blk, grad_x, g


def _coords():
    return lax.axis_index("x"), lax.axis_index("y"), lax.axis_index("c")


def _dev_index(dev):
    return 4 * dev[0] + 2 * dev[1] + dev[2]


def _shard_window(ref, axis, width, idx):
    sl = [slice(None)] * len(ref.shape)
    sl[axis] = pl.ds(pl.multiple_of(idx * width, width), width)
    return ref.at[tuple(sl)]


def _all_gather(name, shards, axes):
    na = len(shards)
    widths = [s.shape[ax] for s, ax in zip(shards, axes)]
    out_shapes = []
    for s, ax in zip(shards, axes):
        shp = list(s.shape)
        shp[ax] *= N_DEV
        out_shapes.append(jax.ShapeDtypeStruct(tuple(shp), s.dtype))

    def body(*refs):
        ins, outs = refs[:na], refs[na:2 * na]
        send_sems, recv_sems, local_sems = refs[2 * na:]
        x, y, c = _coords()
        me, sib = (x, y, c), (x, y, 1 - c)
        chips = [(1 - x, y), (x, 1 - y), (1 - x, 1 - y)]

        def blk(a, dev):
            return _shard_window(outs[a], axes[a], widths[a], _dev_index(dev))

        def copy(a, kk, block, to, src=None):
            return pltpu.make_async_remote_copy(
                src_ref=blk(a, block) if src is None else src, dst_ref=blk(a, block),
                send_sem=send_sems.at[a, kk], recv_sem=recv_sems.at[a, kk],
                device_id=to, device_id_type=MESH)

        local = [pltpu.make_async_copy(ins[a], blk(a, me), local_sems.at[a]) for a in range(na)]
        for cp in local:
            cp.start()
        sent = []
        for a in range(na):
            first = [copy(a, 0, me, sib, src=ins[a])]
            first += [copy(a, 1 + j, me, (*chip, c), src=ins[a]) for j, chip in enumerate(chips)]
            for cp in first:
                cp.start()
            sent += first
        for a in range(na):
            for j, chip in enumerate(chips):
                copy(a, 1 + j, (*chip, c), me).wait_recv()
                fwd = copy(a, 4 + j, (*chip, c), sib)
                fwd.start()
                sent.append(fwd)
        for a in range(na):
            copy(a, 0, sib, me).wait_recv()
            for j, chip in enumerate(chips):
                copy(a, 4 + j, (*chip, 1 - c), me).wait_recv()
        for cp in sent:
            cp.wait_send()
        for cp in local:
            cp.wait()

    return pl.pallas_call(
        body, out_shape=out_shapes, in_specs=[ANY] * na, out_specs=[ANY] * na,
        scratch_shapes=[pltpu.SemaphoreType.DMA((na, 7)), pltpu.SemaphoreType.DMA((na, 7)),
                        pltpu.SemaphoreType.DMA((na,))],
        name=name)(*shards)


HBM = pl.BlockSpec(memory_space=pltpu.HBM)
SEM = pl.BlockSpec(memory_space=pltpu.SEMAPHORE)
TOKEN_SPEC = pl.BlockSpec(memory_space=pltpu.VMEM)
TOKEN_SHAPE = jax.ShapeDtypeStruct((8, 128), F32)
SPLIT_PARAMS = pltpu.CompilerParams(has_side_effects=pltpu.SideEffectType.DATAFLOW_SIDE_EFFECTING)


def _hbm(x):
    return pltpu.with_memory_space_constraint(x, pltpu.HBM)


def _hbm_like(x):
    return pltpu.HBM(x.shape, x.dtype)


def _dma_sems(*shape):
    return pltpu.SemaphoreType.DMA(shape)


def _cast_and_place(name, shard, layer, axis, pos, dtype):
    rows, cols = shard.shape[-2:]
    tr = _tile(rows, 256, 16)
    nt = rows // tr
    full = (rows, cols * N_DEV) if axis == 1 else (rows * N_DEV, cols)

    def dev(p):
        return 4 * p[0] + 2 * p[1] + p[2]

    def body(pos_ref, s_ref, b_ref, l_ref):
        v = s_ref[...].astype(dtype)
        b_ref[...] = v
        l_ref[...] = v

    blk = pl.BlockSpec((tr, cols), lambda i, p: (i, 0))
    if axis == 1:
        lspec = pl.BlockSpec((tr, cols), lambda i, p: (i, dev(p)))
    else:
        lspec = pl.BlockSpec((tr, cols), lambda i, p: (dev(p) * nt + i, 0))
    return pl.pallas_call(
        body, grid_spec=pltpu.PrefetchScalarGridSpec(
            num_scalar_prefetch=1, grid=(nt,),
            in_specs=[pl.BlockSpec((None, tr, cols), lambda i, p: (layer, i, 0))], out_specs=[blk, lspec]),
        out_shape=[jax.ShapeDtypeStruct((rows, cols), dtype), jax.ShapeDtypeStruct(full, dtype)],
        compiler_params=_params(("parallel",)), name=name)(pos, shard)


def _gather_start(name, shards, lands, axes, groups):
    na, ng = len(shards), len(groups)
    widths = [s.shape[ax] for s, ax in zip(shards, axes)]

    def body(*refs):
        sh, ld = refs[:na], refs[na:2 * na]
        sems = refs[2 * na:2 * na + 3 * ng]
        token = refs[-1]
        x, y, c = _coords()
        me, sib = (x, y, c), (x, y, 1 - c)
        chips = [(1 - x, y), (x, 1 - y), (1 - x, 1 - y)]
        for gi, grp in enumerate(groups):
            send, r_d2d, r_ici = sems[3 * gi:3 * gi + 3]
            for li, a in enumerate(grp):
                dst = _shard_window(ld[a], axes[a], widths[a], _dev_index(me))
                pltpu.make_async_remote_copy(
                    src_ref=sh[a], dst_ref=dst, send_sem=send.at[4 * li], recv_sem=r_d2d.at[li],
                    device_id=sib, device_id_type=MESH).start()
                for j, chip in enumerate(chips):
                    pltpu.make_async_remote_copy(
                        src_ref=sh[a], dst_ref=dst, send_sem=send.at[4 * li + 1 + j], recv_sem=r_ici.at[3 * li + j],
                        device_id=(*chip, c), device_id_type=MESH).start()
        token[...] = jnp.zeros_like(token)

    out_shape, out_specs = [], []
    for grp in groups:
        out_shape += [_dma_sems(4 * len(grp)), _dma_sems(len(grp)), _dma_sems(3 * len(grp))]
        out_specs += [SEM] * 3
    out_shape += [_hbm_like(s) for s in shards] + [_hbm_like(l) for l in lands] + [TOKEN_SHAPE]
    out_specs += [HBM] * (2 * na) + [TOKEN_SPEC]
    aliases = {a: 3 * ng + a for a in range(2 * na)}
    res = pl.pallas_call(
        body, name=name, out_shape=out_shape, in_specs=[HBM] * (2 * na),
        out_specs=out_specs, input_output_aliases=aliases, compiler_params=SPLIT_PARAMS,
    )(*[_hbm(s) for s in shards], *[_hbm(l) for l in lands])
    sems = [tuple(res[3 * gi:3 * gi + 3]) for gi in range(ng)]
    return sems, list(res[3 * ng:3 * ng + na]), list(res[3 * ng + na:3 * ng + 2 * na]), res[-1]


def _gather_forward(name, lands, axes, r_ici, after):
    n = len(lands)
    widths = [l.shape[ax] // N_DEV for l, ax in zip(lands, axes)]

    def body(*refs):
        ld, r_ici_ref = refs[:n], refs[n]
        f_send, f_recv = refs[n + 2], refs[n + 3]
        x, y, c = _coords()
        sib = (x, y, 1 - c)
        chips = [(1 - x, y), (x, 1 - y), (1 - x, 1 - y)]
        for li in range(n):
            for j, chip in enumerate(chips):
                blk = _shard_window(ld[li], axes[li], widths[li], _dev_index((*chip, c)))
                pltpu.make_async_remote_copy(
                    src_ref=blk, dst_ref=blk, send_sem=f_send.at[3 * li + j], recv_sem=r_ici_ref.at[3 * li + j],
                    device_id=(*chip, c), device_id_type=MESH).wait_recv()
                pltpu.make_async_remote_copy(
                    src_ref=blk, dst_ref=blk, send_sem=f_send.at[3 * li + j], recv_sem=f_recv.at[3 * li + j],
                    device_id=sib, device_id_type=MESH).start()

    res = pl.pallas_call(
        body, name=name, out_shape=[_dma_sems(3 * n), _dma_sems(3 * n)] + [_hbm_like(l) for l in lands],
        in_specs=[HBM] * n + [SEM, ANY], out_specs=[SEM, SEM] + [HBM] * n,
        input_output_aliases={li: 2 + li for li in range(n)}, compiler_params=SPLIT_PARAMS,
    )(*lands, r_ici, after)
    return res[0], res[1], list(res[2:])


def _gather_finish(name, shards, lands, axes, send, r_d2d, f_send, f_recv, after):
    n = len(lands)
    widths = [l.shape[ax] // N_DEV for l, ax in zip(lands, axes)]

    def body(*refs):
        sh, ld = refs[:n], refs[n:2 * n]
        send_ref, r_d2d_ref, f_send_ref, f_recv_ref = refs[2 * n:2 * n + 4]
        x, y, c = _coords()
        me, sib = (x, y, c), (x, y, 1 - c)
        chips = [(1 - x, y), (x, 1 - y), (1 - x, 1 - y)]

        def blk(li, dev):
            return _shard_window(ld[li], axes[li], widths[li], _dev_index(dev))

        for li in range(n):
            for kk in range(4):
                pltpu.make_async_remote_copy(
                    src_ref=sh[li], dst_ref=blk(li, me), send_sem=send_ref.at[4 * li + kk], recv_sem=r_d2d_ref.at[li],
                    device_id=sib, device_id_type=MESH).wait_send()
            pltpu.make_async_remote_copy(
                src_ref=blk(li, sib), dst_ref=blk(li, sib), send_sem=send_ref.at[4 * li], recv_sem=r_d2d_ref.at[li],
                device_id=sib, device_id_type=MESH).wait_recv()
            for j, chip in enumerate(chips):
                pltpu.make_async_remote_copy(
                    src_ref=blk(li, (*chip, c)), dst_ref=blk(li, (*chip, c)), send_sem=f_send_ref.at[3 * li + j],
                    recv_sem=f_recv_ref.at[3 * li + j], device_id=sib, device_id_type=MESH).wait_send()
                pltpu.make_async_remote_copy(
                    src_ref=blk(li, (*chip, 1 - c)), dst_ref=blk(li, (*chip, 1 - c)), send_sem=f_send_ref.at[3 * li + j],
                    recv_sem=f_recv_ref.at[3 * li + j], device_id=sib, device_id_type=MESH).wait_recv()

    res = pl.pallas_call(
        body, name=name, out_shape=[_hbm_like(s) for s in shards] + [_hbm_like(l) for l in lands],
        in_specs=[HBM] * (2 * n) + [SEM] * 4 + [ANY], out_specs=[HBM] * (2 * n),
        input_output_aliases={i: i for i in range(2 * n)}, compiler_params=SPLIT_PARAMS,
    )(*shards, *lands, send, r_d2d, f_send, f_recv, after)
    return list(res[n:])


def _chip_exchange_start(name, parts):
    n = len(parts)

    def body(*refs):
        src, ld = refs[:n], refs[n:2 * n]
        send, recv = refs[2 * n], refs[2 * n + 1]
        token = refs[-1]
        x, y, c = _coords()
        chips = [(1 - x, y), (x, 1 - y), (1 - x, 1 - y)]
        for li in range(n):
            for kk, chip in enumerate(chips):
                pltpu.make_async_remote_copy(
                    src_ref=src[li].at[kk], dst_ref=ld[li].at[kk], send_sem=send.at[3 * li + kk],
                    recv_sem=recv.at[3 * li + kk], device_id=(*chip, c), device_id_type=MESH).start()
        token[...] = jnp.zeros_like(token)

    lands = [lax.empty(p.shape, p.dtype) for p in parts]
    res = pl.pallas_call(
        body, name=name,
        out_shape=[_dma_sems(3 * n), _dma_sems(3 * n)] + [_hbm_like(p) for p in parts] * 2 + [TOKEN_SHAPE],
        in_specs=[HBM] * (2 * n), out_specs=[SEM, SEM] + [HBM] * (2 * n) + [TOKEN_SPEC],
        input_output_aliases={i: 2 + i for i in range(2 * n)}, compiler_params=SPLIT_PARAMS,
    )(*[_hbm(p) for p in parts], *[_hbm(l) for l in lands])
    return res[0], res[1], list(res[2:2 + n]), list(res[2 + n:2 + 2 * n]), res[-1]


def _chip_exchange_finish(name, started, after):
    counts = [len(st[2]) for st in started]
    total = sum(counts)
    ns = len(started)

    def body(*refs):
        src, ld = refs[:total], refs[total:2 * total]
        sems = refs[2 * total:2 * total + 2 * ns]
        x, y, c = _coords()
        chips = [(1 - x, y), (x, 1 - y), (1 - x, 1 - y)]
        off = 0
        for si, cnt in enumerate(counts):
            send, recv = sems[2 * si], sems[2 * si + 1]
            for li in range(cnt):
                for kk, chip in enumerate(chips):
                    cp = pltpu.make_async_remote_copy(
                        src_ref=src[off + li].at[kk], dst_ref=ld[off + li].at[kk], send_sem=send.at[3 * li + kk],
                        recv_sem=recv.at[3 * li + kk], device_id=(*chip, c), device_id_type=MESH)
                    cp.wait_send()
                    cp.wait_recv()
            off += cnt

    srcs = [p for st in started for p in st[2]]
    lands = [l for st in started for l in st[3]]
    sems = [s for st in started for s in st[:2]]
    res = pl.pallas_call(
        body, name=name, out_shape=[_hbm_like(p) for p in srcs + lands],
        in_specs=[HBM] * (2 * total) + [SEM] * (2 * ns) + [ANY], out_specs=[HBM] * (2 * total),
        input_output_aliases={i: i for i in range(2 * total)}, compiler_params=SPLIT_PARAMS,
    )(*srcs, *lands, *sems, after)
    out, off = [], total
    for cnt in counts:
        out.append(list(res[off:off + cnt]))
        off += cnt
    return out


def _pair_copies(src, ld, send, recv, axes, widths):
    x, y, c = _coords()
    chips = [(x, y), (1 - x, y), (x, 1 - y), (1 - x, 1 - y)]
    return [pltpu.make_async_remote_copy(
        src_ref=_shard_window(src[li], axes[li], widths[li], _dev_index((*chip, 1 - c))),
        dst_ref=ld[li].at[kk], send_sem=send.at[4 * li + kk], recv_sem=recv.at[4 * li + kk],
        device_id=(x, y, 1 - c), device_id_type=MESH)
        for li in range(len(src)) for kk, chip in enumerate(chips)]


def _pair_exchange_start(name, grads, axes):
    n = len(grads)
    widths = [gr.shape[ax] // N_DEV for gr, ax in zip(grads, axes)]
    lands = []
    for gr, ax, wd in zip(grads, axes, widths):
        shp = list(gr.shape)
        shp[ax] = wd
        lands.append(lax.empty((4, *shp), gr.dtype))

    def body(*refs):
        for cp in _pair_copies(refs[:n], refs[n:2 * n], refs[2 * n], refs[2 * n + 1], axes, widths):
            cp.start()
        refs[-1][...] = jnp.zeros_like(refs[-1])

    res = pl.pallas_call(
        body, name=name,
        out_shape=[_dma_sems(4 * n), _dma_sems(4 * n)] + [_hbm_like(a) for a in grads + lands] + [TOKEN_SHAPE],
        in_specs=[HBM] * (2 * n), out_specs=[SEM, SEM] + [HBM] * (2 * n) + [TOKEN_SPEC],
        input_output_aliases={i: 2 + i for i in range(2 * n)}, compiler_params=SPLIT_PARAMS,
    )(*[_hbm(a) for a in grads + lands])
    return res[0], res[1], list(res[2:2 + n]), list(res[2 + n:2 + 2 * n]), res[-1]


def _pair_exchange_finish(name, started, axes, after):
    send, recv, grads, lands, _ = started
    n = len(grads)
    widths = [gr.shape[ax] // N_DEV for gr, ax in zip(grads, axes)]

    def body(*refs):
        for cp in _pair_copies(refs[:n], refs[n:2 * n], refs[2 * n], refs[2 * n + 1], axes, widths):
            cp.wait_send()
            cp.wait_recv()

    res = pl.pallas_call(
        body, name=name, out_shape=[_hbm_like(a) for a in grads + lands],
        in_specs=[HBM] * (2 * n) + [SEM, SEM, ANY], out_specs=[HBM] * (2 * n),
        input_output_aliases={i: i for i in range(2 * n)}, compiler_params=SPLIT_PARAMS,
    )(*grads, *lands, send, recv, after)
    return list(res[:n]), list(res[n:])


def _pair_sum(name, grad, land, axis, pos):
    wd = grad.shape[axis] // N_DEV
    shard_shape = land.shape[1:]
    rows, cols = shard_shape
    tr = _tile(rows, 256, 16)
    nt = rows // tr

    def dev_of(kk, pos_ref):
        return 4 * (pos_ref[0] ^ (kk & 1)) + 2 * (pos_ref[1] ^ (kk >> 1)) + pos_ref[2]

    def gspec(kk):
        if axis == 1:
            return pl.BlockSpec((tr, wd), lambda t, p: (t, dev_of(kk, p)))
        return pl.BlockSpec((tr, cols), lambda t, p: (dev_of(kk, p) * nt + t, 0))

    def body(pos_ref, g0, g1, g2, g3, l_ref, p0_ref, pb_ref):
        p0_ref[...] = g0[...].astype(F32) + l_ref[0].astype(F32)
        for kk, g_ref in enumerate((g1, g2, g3)):
            pb_ref[kk] = (g_ref[...].astype(F32) + l_ref[kk + 1].astype(F32)).astype(BF16)

    return pl.pallas_call(
        body,
        grid_spec=pltpu.PrefetchScalarGridSpec(
            num_scalar_prefetch=1, grid=(nt,),
            in_specs=[gspec(kk) for kk in range(4)] + [pl.BlockSpec((4, tr, cols), lambda t, p: (0, t, 0))],
            out_specs=[pl.BlockSpec((tr, cols), lambda t, p: (t, 0)),
                       pl.BlockSpec((3, tr, cols), lambda t, p: (0, t, 0))]),
        out_shape=[jax.ShapeDtypeStruct(shard_shape, F32), jax.ShapeDtypeStruct((3, *shard_shape), BF16)],
        compiler_params=_params(("parallel",)), name=name)(pos, grad, grad, grad, grad, land)


def _adamw_math(g, w, m, v):
    c1 = 1.0 - ADAM_B1 ** ADAM_STEP
    c2 = 1.0 - ADAM_B2 ** ADAM_STEP
    nm = ADAM_B1 * m + (1.0 - ADAM_B1) * g
    nv = ADAM_B2 * v + (1.0 - ADAM_B2) * (g * g)
    return -ADAM_LR * ((nm / c1) / (jnp.sqrt(nv / c2) + ADAM_EPS) + ADAM_WD * w), nm, nv


def _sum_parts(name, parts):
    n, rows, cols = parts.shape
    tr = _tile(rows, 1024, 8)

    def body(p_ref, o_ref):
        o_ref[...] = functools.reduce(lambda a, b: a + b, [p_ref[i] for i in range(n)])

    return pl.pallas_call(
        body, grid=(rows // tr,), in_specs=[pl.BlockSpec((n, tr, cols), lambda i: (0, i, 0))],
        out_specs=pl.BlockSpec((tr, cols), lambda i: (i, 0)), out_shape=jax.ShapeDtypeStruct((rows, cols), F32),
        compiler_params=_params(("parallel",)), name=name)(parts)


def _adamw_multi(name, gs, ws, ms, vs):
    k = len(gs)
    rows, cols = ws[0].shape
    tr = _tile(rows, 1024, 8)

    def body(*refs):
        for i in range(k):
            g_ref, w_ref, m_ref, v_ref = (refs[j * k + i] for j in range(4))
            og_ref, d_ref, nm_ref, nv_ref = (refs[(4 + j) * k + i] for j in range(4))
            g = g_ref[...]
            og_ref[...] = g
            d_ref[...], nm_ref[...], nv_ref[...] = _adamw_math(g, w_ref[...], m_ref[...], v_ref[...])

    spec = pl.BlockSpec((tr, cols), lambda i: (i, 0))
    res = pl.pallas_call(
        body, grid=(rows // tr,), in_specs=[spec] * (4 * k), out_specs=[spec] * (4 * k),
        out_shape=[jax.ShapeDtypeStruct((rows, cols), F32)] * (4 * k),
        compiler_params=_params(("parallel",)), name=name)(*gs, *ws, *ms, *vs)
    return [[res[j * k + i] for j in range(4)] for i in range(k)]


def _adamw(name, parts, w, m, v, layer=None, prev=None):
    rows, cols = w.shape[-2:]
    tr = _tile(rows, 256, 8)
    npart = len(parts)

    def body(*refs):
        p_refs = refs[:npart]
        w_ref, m_ref, v_ref = refs[npart:npart + 3]
        g_ref, d_ref, nm_ref, nv_ref = refs[-4:]
        g = None
        for r in p_refs:
            if len(r.shape) == 3:
                for i in range(r.shape[0]):
                    t = r[i].astype(F32)
                    g = t if g is None else g + t
            else:
                t = r[...].astype(F32)
                g = t if g is None else g + t
        g_ref[...] = g
        d_ref[...], nm_ref[...], nv_ref[...] = _adamw_math(g, w_ref[...], m_ref[...], v_ref[...])

    spec = pl.BlockSpec((tr, cols), lambda i: (i, 0))
    wspec = spec if layer is None else pl.BlockSpec((None, tr, cols), lambda i: (layer, i, 0))
    pspecs = [pl.BlockSpec((p.shape[0], tr, cols), lambda i: (0, i, 0)) if p.ndim == 3 else spec
              for p in parts]
    prev = list(prev) if prev else []
    return pl.pallas_call(
        body, grid=(rows // tr,), in_specs=pspecs + [wspec] * 3 + [ANY] * len(prev), out_specs=[wspec] * 4,
        out_shape=[jax.ShapeDtypeStruct(w.shape, F32)] * 4,
        input_output_aliases={npart + 3 + i: i for i in range(len(prev))},
        compiler_params=_params(("parallel",)), name=name)(*parts, w, m, v, *prev)


SMALL_NAMES = ("s5_lam_re", "s5_lam_im", "s5_log_dt", "s5_b_re", "s5_b_im", "s5_c_re", "s5_c_im",
               "ffn_norm", "b_norm_mix", "kv_norm", "final_norm")
SMALL_PAD = 1024
SMALL_GROUPS = (("s5_lam_re", "s5_lam_im"), ("s5_log_dt",), ("s5_b_re", "s5_b_im"), ("s5_c_re", "s5_c_im"),
                ("ffn_norm",), ("b_norm_mix", "kv_norm", "final_norm"))


def _pack(parts):
    flat = []
    for p in parts:
        f = p.reshape(-1)
        pad = (-f.shape[0]) % SMALL_PAD
        if pad:
            f = jnp.concatenate([f, jnp.zeros((pad,), f.dtype)])
        flat.append(f)
    return jnp.concatenate(flat).reshape(-1, 128)


def _unpack(packed, shapes):
    flat = packed.reshape(-1)
    out, off = [], 0
    for shp in shapes:
        size = math.prod(shp)
        out.append(flat[off:off + size].reshape(shp))
        off += size + (-size) % SMALL_PAD
    return out


def kernel(x, s5_lam_re, s5_lam_im, s5_log_dt, s5_b_re, s5_b_im, s5_c_re, s5_c_im, s5_d, s5_w_glu, a_norm_mix, ffn_norm, ffn_w_in, ffn_w_out, b_norm_mix, attn_w_q, attn_w_o, kv_norm, w_kv, final_norm, loss_target, m_s5_lam_re, m_s5_lam_im, m_s5_log_dt, m_s5_b_re, m_s5_b_im, m_s5_c_re, m_s5_c_im, m_s5_d, m_s5_w_glu, m_a_norm_mix, m_ffn_norm, m_ffn_w_in, m_ffn_w_out, m_b_norm_mix, m_attn_w_q, m_attn_w_o, m_kv_norm, m_w_kv, m_final_norm, v_s5_lam_re, v_s5_lam_im, v_s5_log_dt, v_s5_b_re, v_s5_b_im, v_s5_c_re, v_s5_c_im, v_s5_d, v_s5_w_glu, v_a_norm_mix, v_ffn_norm, v_ffn_w_in, v_ffn_w_out, v_b_norm_mix, v_attn_w_q, v_attn_w_o, v_kv_norm, v_w_kv, v_final_norm):
    args = dict(locals())
    T, D = x.shape[1], x.shape[2]
    n_layers = ffn_w_in.shape[0]
    xi_, yi_, ci_ = _coords()
    pos = jnp.stack([xi_, yi_, ci_]).astype(jnp.int32)
    me = 4 * xi_ + 2 * yi_ + ci_

    big_names = ["glu"] + [f"win{l}" for l in range(n_layers)] + [f"wout{l}" for l in range(n_layers)] \
        + ["wkv", "wq", "wo"]
    big_shards = [(s5_w_glu, 0)] + [(ffn_w_in, l) for l in range(n_layers)] \
        + [(ffn_w_out, l) for l in range(n_layers)] + [(w_kv[None], 0), (attn_w_q, 0), (attn_w_o, 0)]
    big_axes = [1] + [1] * n_layers + [0] * n_layers + [0, 1, 0]
    big_out_names = ["s5_w_glu"] + ["ffn_w_in"] * n_layers + ["ffn_w_out"] * n_layers \
        + ["w_kv", "attn_w_q", "attn_w_o"]
    index_of = {n: i for i, n in enumerate(big_names)}
    vec_shard = jnp.concatenate([s5_d, a_norm_mix], axis=0)
    (vecs,) = _all_gather("vectors_all_gather", [vec_shard], [1])

    gather_groups = [["glu"], ["win0"], ["wout0"], ["wkv", "wq", "wo"], ["win1", "wout1"]]
    group_idx = [[index_of[n] for n in grp] for grp in gather_groups]
    group_of = {n: gi for gi, grp in enumerate(gather_groups) for n in grp}
    placed = [_cast_and_place(f"cast_place_{n}", s, l, ax, pos, BF16)
              for n, (s, l), ax in zip(big_names, big_shards, big_axes)]
    gather_sems, shards_thru, lands_thru, start_token = _gather_start(
        "weights_gather_start", [p[0] for p in placed], [p[1] for p in placed], big_axes, group_idx)
    full = {}

    forwarded = {}

    def wts(name, after, prefetch=False):
        gi = group_of[name]
        idx = group_idx[gi]
        axes = [big_axes[a] for a in idx]
        send, r_d2d, r_ici = gather_sems[gi]
        if gi not in forwarded:
            forwarded[gi] = _gather_forward(
                f"weights_gather_forward{gi}", [lands_thru[a] for a in idx], axes, r_ici, after)
        if prefetch:
            return None
        if name not in full:
            f_send, f_recv, lands = forwarded[gi]
            done = _gather_finish(f"weights_gather_finish{gi}", [shards_thru[a] for a in idx], lands, axes,
                                  send, r_d2d, f_send, f_recv, after)
            full.update(zip(gather_groups[gi], done))
        return full[name]

    exchanges, pending = [], []

    def flush(after):
        names, axes, started = pending.pop()
        tag = "_".join(names)
        grads, lands = _pair_exchange_finish(f"rs_pair_exchange_finish_{tag}", started, axes, after)
        p0s, pbs = [], []
        for n, gr, land, ax in zip(names, grads, lands, axes):
            p0, pb = _pair_sum(f"rs_pair_sum_{n}", gr, land, ax, pos)
            p0s.append(p0)
            pbs.append(pb)
        started = _chip_exchange_start(f"rs_chip_exchange_start_{tag}", pbs)
        exchanges.append((names, p0s, started))
        return started[4]

    def ready(grads, after=None):
        if not grads:
            return flush(after)
        names = list(grads)
        axes = [big_axes[index_of[n]] for n in names]
        started = _pair_exchange_start(f"rs_pair_exchange_start_{'_'.join(names)}", [grads[n] for n in names], axes)
        token = flush(started[4]) if pending else started[4]
        pending.append((names, axes, started))
        return token

    G, P, C = s5_b_re.shape[1:]
    w = dict(
        a_norm=vecs[1:2], s5_d=vecs[0:1],
        ffn_norm=ffn_norm, b_norm=b_norm_mix, kv_norm=kv_norm.reshape(1, D), final_norm=final_norm.reshape(1, D),
        lam_re=s5_lam_re[0], lam_im=s5_lam_im[0], log_dt=s5_log_dt.reshape(G, 1),
        bt_re=s5_b_re[0].transpose(2, 0, 1), bt_im=s5_b_im[0].transpose(2, 0, 1),
        c_re=s5_c_re[0], c_im=s5_c_im[0],
    )
    small = {}

    def small_ready(g):
        small_g = dict(
            s5_lam_re=g["lam_re"], s5_lam_im=g["lam_im"], s5_log_dt=g["log_dt"],
            s5_b_re=g["bt_re"].transpose(1, 2, 0), s5_b_im=g["bt_im"].transpose(1, 2, 0),
            s5_c_re=g["c_re"], s5_c_im=g["c_im"], ffn_norm=g["ffn_norm"], b_norm_mix=g["b_norm"],
            kv_norm=g["kv_norm"], final_norm=g["final_norm"])
        packed = _pack([small_g[n] for n in SMALL_NAMES])
        shard, land = _cast_and_place("place_small_grads", packed[None], 0, 0, pos, F32)
        sems, thru, lands, token = _gather_start("small_grads_gather_start", [shard], [land], [0], [[0]])
        small.update(sems=sems[0], thru=thru, lands=lands, rows=packed.shape[0])
        return token

    loss_blk, grad_x, g = _local_step(x[0], loss_target[0], w, wts, ready, dep0=start_token,
                                      small_ready=small_ready)
    loss = lax.psum(loss_blk[0, 0], ("x", "y", "c"))
    (tail,) = _all_gather("vector_grads_all_gather", [_pack([g["s5_d"], g["a_norm"]])], [0])

    out = {}

    def put(name, res, shape):
        for kind, r in zip(("grad", "delta", "new_m", "new_v"), res):
            out[f"{kind}_{name}"] = r.reshape(shape)

    updated = {}

    def update(names, p0s, recvd):
        for name, p0, rc in zip(names, p0s, recvd):
            oname = big_out_names[index_of[name]]
            w3, layer = big_shards[index_of[name]]
            updated[oname] = _adamw(f"adamw_{name}", [p0, rc], w3, args["m_" + oname].reshape(w3.shape),
                                    args["v_" + oname].reshape(w3.shape), layer=layer, prev=updated.get(oname))
        return updated[big_out_names[index_of[names[0]]]][0]

    early, last = exchanges[:-1], exchanges[-1]
    landed = _chip_exchange_finish("rs_chip_exchange_finish_early", [e[2] for e in early], grad_x)
    for (names, p0s, _), recvd in zip(early[:-1], landed[:-1]):
        marker = update(names, p0s, recvd)
    send, r_d2d, r_ici = small["sems"]
    f_send, f_recv, small_lands = _gather_forward("small_grads_gather_forward", small["lands"], [0], r_ici, marker)
    marker = update(early[-1][0], early[-1][1], landed[-1])
    (all_parts,) = _gather_finish("small_grads_gather_finish", small["thru"], small_lands, [0],
                                  send, r_d2d, f_send, f_recv, marker)
    (recvd,) = _chip_exchange_finish("rs_chip_exchange_finish_last", [last[2]], marker)
    update(last[0], last[1], recvd)
    for oname, res in updated.items():
        put(oname, res, args[oname].shape)
    g_small = _sum_parts("small_grads_sum", all_parts.reshape(N_DEV, small["rows"], 128))
    g_small = dict(zip(SMALL_NAMES, _unpack(g_small, [args[n].shape for n in SMALL_NAMES])))

    def view(a):
        return a.reshape(1, -1) if a.ndim == 1 else a.reshape(-1, a.shape[-1])

    for grp in SMALL_GROUPS:
        res = _adamw_multi(f"adamw_{grp[0]}", [view(g_small[n]) for n in grp], [view(args[n]) for n in grp],
                           [view(args["m_" + n]) for n in grp], [view(args["v_" + n]) for n in grp])
        for n, r in zip(grp, res):
            put(n, r, args[n].shape)
    ws = D // N_DEV
    tail = lax.dynamic_slice_in_dim(tail.reshape(N_DEV, 2, D), me * ws, ws, axis=2)
    res = _adamw("adamw_vec", [tail], vec_shard,
                 jnp.concatenate([m_s5_d, m_a_norm_mix], axis=0), jnp.concatenate([v_s5_d, v_a_norm_mix], axis=0))
    put("s5_d", [r[0:1] for r in res], s5_d.shape)
    put("a_norm_mix", [r[1:2] for r in res], a_norm_mix.shape)

    names = ("s5_lam_re", "s5_lam_im", "s5_log_dt", "s5_b_re", "s5_b_im", "s5_c_re", "s5_c_im", "s5_d",
             "s5_w_glu", "a_norm_mix", "ffn_norm", "ffn_w_in", "ffn_w_out", "b_norm_mix", "attn_w_q",
             "attn_w_o", "kv_norm", "w_kv", "final_norm")
    result = [loss, grad_x.reshape(x.shape)]
    for kind in ("grad", "delta", "new_m", "new_v"):
        result += [out[f"{kind}_{n}"] for n in names]
    return tuple(result)
```
